```python
import math
import jax
import jax.numpy as jnp
from jax import lax
import numpy as np

D_MODEL = 1024
BATCH = 16
SEQ = 256
DEPTH = 2
DEC_BATCH = 2
DEC_SEQ = 2048
PAST_LEN = 512

GRID_W = 64
NORM_EPS = 1e-6
CONV_DIM = 512
MLA_HEADS = 8
Q_RANK = 384
KV_RANK = 256
QK_NOPE = 64
QK_ROPE = 32
V_HEAD = 64
ROPE_BASE = 10000.0
MLA_SCALE = (QK_NOPE + QK_ROPE) ** -0.5
Q_BLOCK = 128
SSM_HEADS = 16
SSM_HEAD_DIM = 64
SSM_INNER = SSM_HEADS * SSM_HEAD_DIM
SSM_GROUPS = 2
SSM_STATE = 128
SSM_CONV_CH = SSM_INNER + 2 * SSM_GROUPS * SSM_STATE
SSM_CHUNK = 128
DT_MIN = 0.001
DT_MAX = 0.1
N_EXPERTS = 32
TOP_K = 4
EXPERT_FF = D_MODEL
SWIGLU_ALPHA = 1.702
SWIGLU_LIMIT = 7.0
MOE_BLOCK = 128
SPLIT_SIZES = (CONV_DIM, CONV_DIM, CONV_DIM, Q_RANK, KV_RANK, QK_ROPE, SSM_INNER, SSM_CONV_CH, SSM_HEADS, SSM_HEADS, D_MODEL, D_MODEL, D_MODEL)
IN_COLS = sum(SPLIT_SIZES)

kernel_name = 'hybrid_flow_prefix_conv_mla_ssd_moe'


def rmsnorm(x, w):
    xf = x.astype(jnp.float32)
    y = xf * lax.rsqrt(jnp.mean(xf * xf, axis=-1, keepdims=True) + NORM_EPS)
    return y.astype(x.dtype) * w


def short_conv3(u, w):
    up = jnp.pad(u, ((0, 0), (1, 1), (0, 0)))
    return up[:, :-2] * w[0] + up[:, 1:-1] * w[1] + up[:, 2:] * w[2]


def axial_angles(n_tokens):
    rows = n_tokens // GRID_W
    t = jnp.arange(rows * GRID_W)
    row = (t // GRID_W).astype(jnp.float32)
    col = (t % GRID_W).astype(jnp.float32)
    half = QK_ROPE // 2
    inv = ROPE_BASE ** (-jnp.arange(0, half, 2, dtype=jnp.float32) / half)
    return row[:, None] * inv, col[:, None] * inv


def rotate_pairs(x, ang):
    x1, x2 = jnp.split(x, 2, axis=-1)
    cos = jnp.cos(ang).astype(x.dtype)
    sin = jnp.sin(ang).astype(x.dtype)
    return jnp.concatenate([x1 * cos - x2 * sin, x1 * sin + x2 * cos], axis=-1)


def rope2d(x, ang_r, ang_c):
    xr, xc = jnp.split(x, 2, axis=-1)
    return jnp.concatenate([rotate_pairs(xr, ang_r), rotate_pairs(xc, ang_c)], axis=-1)


def blocked_attention(q, k, v, scale):
    b, lq, h, dk = q.shape
    dv = v.shape[-1]
    nb = lq // Q_BLOCK
    qb = q.reshape(b, nb, Q_BLOCK, h, dk).swapaxes(0, 1)

    def one_block(qblk):
        s = jnp.einsum('bqhd,bkhd->bhqk', qblk, k, preferred_element_type=jnp.float32) * scale
        pr = jax.nn.softmax(s, axis=-1).astype(v.dtype)
        return jnp.einsum('bhqk,bkhd->bqhd', pr, v)

    o = lax.map(one_block, qb)
    return o.swapaxes(0, 1).reshape(b, lq, h, dv)


def segsum(a):
    T = a.shape[-1]
    a_rep = jnp.broadcast_to(a[..., None], a.shape + (T,))
    a_rep = jnp.where(jnp.tril(jnp.ones((T, T), bool), -1), a_rep, 0.0)
    s = jnp.cumsum(a_rep, axis=-2)
    return jnp.where(jnp.tril(jnp.ones((T, T), bool)), s, -jnp.inf)


def ssd(x, dt, A, Bm, Cm, init):
    b, l, h, p = x.shape
    n = Bm.shape[-1]
    nc = l // SSM_CHUNK
    xd = (x * dt[..., None]).reshape(b, nc, SSM_CHUNK, h, p)
    a = (dt * A).reshape(b, nc, SSM_CHUNK, h).transpose(0, 3, 1, 2)
    Bc = Bm.reshape(b, nc, SSM_CHUNK, h, n)
    Cc = Cm.reshape(b, nc, SSM_CHUNK, h, n)
    a_cs = jnp.cumsum(a, axis=-1)
    lmat = jnp.exp(segsum(a))
    scores = jnp.einsum('bcqhn,bckhn->bhcqk', Cc, Bc) * lmat
    y_diag = jnp.einsum('bhcqk,bckhp->bcqhp', scores, xd)
    decay_states = jnp.exp(a_cs[..., -1:] - a_cs)
    states = jnp.einsum('bckhn,bhck,bckhp->bchpn', Bc, decay_states, xd)
    states = jnp.concatenate([init[:, None], states], axis=1)
    chunk_tot = jnp.pad(a_cs[..., -1], ((0, 0), (0, 0), (1, 0)))
    decay_chunk = jnp.exp(segsum(chunk_tot))
    new_states = jnp.einsum('bhzc,bchpn->bzhpn', decay_chunk, states)
    states_in, final = new_states[:, :-1], new_states[:, -1]
    y_off = jnp.einsum('bcqhn,bchpn,bhcq->bcqhp', Cc, states_in, jnp.exp(a_cs))
    return (y_diag + y_off).reshape(b, l, h, p), final


def bidir_ssd(xh, dt_f, dt_b, a_f, a_b, bm, cm, init_f, init_b):
    def flip(t):
        return jnp.flip(t, axis=1)
    y_f, s_f = ssd(xh, dt_f, a_f, bm, cm, init_f)
    y_b, s_b = ssd(flip(xh), flip(dt_b), a_b, flip(bm), flip(cm), init_b)
    return y_f + flip(y_b), s_f, s_b


def mixing_sublayer(h, p, l, angles, ctx):
    bsz, n, _ = h.shape
    proj = h @ p['w_in'][l]
    idx = [int(i) for i in np.cumsum(SPLIT_SIZES)[:-1]]
    (u_a, gb_a, gc_a, cq, ckv_raw, kpe_raw, z_ssm, xbc, dtr_f, dtr_b,
     m_a, m_b, m_c) = jnp.split(proj, idx, axis=-1)

    y_a = (gb_a * short_conv3(gc_a * u_a, p['conv_a_w'][l])) @ p['w_out_a'][l]

    q = (rmsnorm(cq, p['q_norm_w'][l]) @ p['w_uq'][l]).reshape(bsz, n, MLA_HEADS, QK_NOPE + QK_ROPE)
    q_nope, q_pe = jnp.split(q, [QK_NOPE], axis=-1)
    c_kv = rmsnorm(ckv_raw, p['kv_norm_w'][l])
    k_pe = kpe_raw
    if ctx is None:
        ckv_all, kpe_all = c_kv, k_pe
    else:
        ang_r, ang_c = angles
        q_pe = rope2d(q_pe, ang_r[None, :, None, :], ang_c[None, :, None, :])
        k_pe = rope2d(k_pe, ang_r[None], ang_c[None])
        ckv_all = jnp.concatenate([c_kv, ctx[0]], axis=1)
        kpe_all = jnp.concatenate([k_pe, ctx[1]], axis=1)
    k_nope = jnp.einsum('bkr,rhd->bkhd', ckv_all, p['w_uk'][l])
    v = jnp.einsum('bkr,rhd->bkhd', ckv_all, p['w_uv'][l])
    k = jnp.concatenate([k_nope, jnp.broadcast_to(kpe_all[:, :, None, :], k_nope.shape[:3] + (QK_ROPE,))], axis=-1)
    att = blocked_attention(jnp.concatenate([q_nope, q_pe], axis=-1), k, v, MLA_SCALE)
    y_b = att.reshape(bsz, n, MLA_HEADS * V_HEAD) @ p['w_o_mla'][l]

    f32 = jnp.float32
    xbc = jax.nn.silu(short_conv3(xbc, p['ssm_conv_w'][l]) + p['ssm_conv_b'][l])
    x_s, b_s, c_s = jnp.split(xbc, [SSM_INNER, SSM_INNER + SSM_GROUPS * SSM_STATE], axis=-1)
    rep = SSM_HEADS // SSM_GROUPS
    xh = x_s.reshape(bsz, n, SSM_HEADS, SSM_HEAD_DIM).astype(f32)
    bm = jnp.repeat(b_s.reshape(bsz, n, SSM_GROUPS, SSM_STATE), rep, axis=2).astype(f32)
    cm = jnp.repeat(c_s.reshape(bsz, n, SSM_GROUPS, SSM_STATE), rep, axis=2).astype(f32)
    dt_f = jax.nn.softplus((dtr_f + p['dt_bias_fwd'][l]).astype(f32))
    dt_b = jax.nn.softplus((dtr_b + p['dt_bias_bwd'][l]).astype(f32))
    a_f = -jnp.exp(p['a_log_fwd'][l].astype(f32))
    a_b = -jnp.exp(p['a_log_bwd'][l].astype(f32))
    if ctx is None:
        init_f = jnp.zeros((bsz, SSM_HEADS, SSM_HEAD_DIM, SSM_STATE), f32)
        init_b = init_f
    else:
        init_f = ctx[2].astype(f32)
        init_b = ctx[3].astype(f32)
    y_ssm, s_f, s_b = bidir_ssd(xh, dt_f, dt_b, a_f, a_b, bm, cm, init_f, init_b)
    y_ssm = y_ssm + p['d_skip'][l].astype(f32)[:, None] * xh
    y_ssm = y_ssm.reshape(bsz, n, SSM_INNER).astype(h.dtype) * jax.nn.silu(z_ssm)
    y_c = rmsnorm(y_ssm, p['ssm_norm_w'][l]) @ p['w_o_ssm'][l]

    merged = jax.nn.sigmoid(m_a) * y_a + jax.nn.sigmoid(m_b) * y_b + jax.nn.sigmoid(m_c) * y_c
    out = merged @ p['w_o'][l]
    if ctx is None:
        return out, (c_kv, k_pe, s_f.astype(h.dtype), s_b.astype(h.dtype))
    return out, None


def moe_ffn(h, p, l):
    bsz, n, d = h.shape
    tokens = h.reshape(-1, d)
    n_tok = tokens.shape[0]
    n_assign = n_tok * TOP_K
    logits = (tokens @ p['router_w'][l] + p['router_b'][l]).astype(jnp.float32)
    top_v, top_i = lax.top_k(logits, TOP_K)
    gates = jax.nn.softmax(top_v, axis=-1)
    flat_e = top_i.reshape(-1)
    flat_tok = jnp.arange(n_assign, dtype=jnp.int32) // TOP_K
    order = jnp.argsort(flat_e)
    sorted_e = flat_e[order]
    counts = jnp.bincount(flat_e, length=N_EXPERTS)
    padded = (counts + MOE_BLOCK - 1) // MOE_BLOCK * MOE_BLOCK
    pad_end = jnp.cumsum(padded)
    pad_start = pad_end - padded
    sort_start = jnp.cumsum(counts) - counts
    dest = pad_start[sorted_e] + (jnp.arange(n_assign) - sort_start[sorted_e])
    n_slots = -(-n_assign // MOE_BLOCK) * MOE_BLOCK + N_EXPERTS * MOE_BLOCK
    n_blocks = n_slots // MOE_BLOCK
    slot_tok = jnp.full((n_slots,), n_tok, jnp.int32).at[dest].set(flat_tok[order])
    slot_gate = jnp.zeros((n_slots,), jnp.float32).at[dest].set(gates.reshape(-1)[order])
    blk_e = jnp.minimum(jnp.searchsorted(pad_end, jnp.arange(n_blocks) * MOE_BLOCK, side='right'), N_EXPERTS - 1)
    tok_pad = jnp.concatenate([tokens, jnp.zeros((1, d), tokens.dtype)], axis=0)
    xb = tok_pad[slot_tok].reshape(n_blocks, MOE_BLOCK, d)
    w_gu, b_gu, w_dn, b_dn = p['w_gu'][l], p['b_gu'][l], p['w_down'][l], p['b_down'][l]

    def expert_block(args):
        xe, e = args
        gu = xe @ w_gu[e] + b_gu[e]
        g, u = gu[:, 0::2], gu[:, 1::2]
        g = jnp.minimum(g, SWIGLU_LIMIT)
        u = jnp.clip(u, -SWIGLU_LIMIT, SWIGLU_LIMIT)
        return (g * jax.nn.sigmoid(SWIGLU_ALPHA * g) * (u + 1.0)) @ w_dn[e] + b_dn[e]

    yb = lax.map(expert_block, (xb, blk_e)).reshape(n_slots, d)
    out = jnp.zeros((n_tok + 1, d), h.dtype).at[slot_tok].add(yb * slot_gate[:, None].astype(h.dtype))
    return out[:n_tok].reshape(bsz, n, d)


def trunk_layer(x, cond, p, l, angles, ctx):
    mod = (jax.nn.silu(cond) @ p['w_ada'][l] + p['b_ada'][l])[:, None, :]
    shift1, scale1, gate1, shift2, scale2, gate2 = jnp.split(mod, 6, axis=-1)
    h = rmsnorm(x, p['norm1_w'][l]) * (1.0 + scale1) + shift1
    mix, new_ctx = mixing_sublayer(h, p, l, angles, ctx)
    x = x + gate1 * mix
    h = rmsnorm(x, p['norm2_w'][l]) * (1.0 + scale2) + shift2
    x = x + gate2 * moe_ffn(h, p, l)
    return x, new_ctx


def setup_inputs(seed: int = 0) -> dict:
    key = jax.random.key(seed)
    ks = iter(jax.random.split(key, 64))
    f32 = jnp.float32
    L, D = DEPTH, D_MODEL

    def nrm(shape, scale):
        return jax.random.normal(next(ks), shape, f32) * scale

    def gain(shape):
        return 1.0 + 0.02 * jax.random.normal(next(ks), shape, f32)

    def dt_bias():
        u = jax.random.uniform(next(ks), (L, SSM_HEADS), f32)
        dt0 = jnp.exp(u * (math.log(DT_MAX) - math.log(DT_MIN)) + math.log(DT_MIN))
        return dt0 + jnp.log(-jnp.expm1(-dt0))

    def a_log():
        return jnp.log(jax.random.uniform(next(ks), (L, SSM_HEADS), f32, 1.0, 16.0))

    return {
        'x_prompt': nrm((BATCH, SEQ, D), 1.0),
        'x_sample': nrm((DEC_BATCH, DEC_SEQ, D), 1.0),
        'cache_ckv': nrm((DEC_BATCH, L, PAST_LEN, KV_RANK), 1.0),
        'cache_kpe': nrm((DEC_BATCH, L, PAST_LEN, QK_ROPE), 1.0),
        'state_ssm_fwd': nrm((DEC_BATCH, L, SSM_HEADS, SSM_HEAD_DIM, SSM_STATE), 0.5),
        'state_ssm_bwd': nrm((DEC_BATCH, L, SSM_HEADS, SSM_HEAD_DIM, SSM_STATE), 0.5),
        'c': nrm((DEC_BATCH, D), 1.0),
        'c_ctx': nrm((D,), 1.0),
        'w_ada': nrm((L, D, 6 * D), 0.5 * D ** -0.5),
        'b_ada': nrm((L, 6 * D), 0.02),
        'norm1_w': gain((L, D)),
        'w_in': nrm((L, D, IN_COLS), D ** -0.5),
        'conv_a_w': nrm((L, 3, CONV_DIM), 3 ** -0.5),
        'w_out_a': nrm((L, CONV_DIM, D), CONV_DIM ** -0.5),
        'q_norm_w': gain((L, Q_RANK)),
        'w_uq': nrm((L, Q_RANK, MLA_HEADS * (QK_NOPE + QK_ROPE)), Q_RANK ** -0.5),
        'kv_norm_w': gain((L, KV_RANK)),
        'w_uk': nrm((L, KV_RANK, MLA_HEADS, QK_NOPE), KV_RANK ** -0.5),
        'w_uv': nrm((L, KV_RANK, MLA_HEADS, V_HEAD), KV_RANK ** -0.5),
        'w_o_mla': nrm((L, MLA_HEADS * V_HEAD, D), (MLA_HEADS * V_HEAD) ** -0.5),
        'ssm_conv_w': nrm((L, 3, SSM_CONV_CH), 3 ** -0.5),
        'ssm_conv_b': nrm((L, SSM_CONV_CH), 0.02),
        'dt_bias_fwd': dt_bias(),
        'dt_bias_bwd': dt_bias(),
        'a_log_fwd': a_log(),
        'a_log_bwd': a_log(),
        'd_skip': gain((L, SSM_HEADS)),
        'ssm_norm_w': gain((L, SSM_INNER)),
        'w_o_ssm': nrm((L, SSM_INNER, D), SSM_INNER ** -0.5),
        'w_o': nrm((L, D, D), D ** -0.5),
        'norm2_w': gain((L, D)),
        'router_w': nrm((L, D, N_EXPERTS), D ** -0.5),
        'router_b': nrm((L, N_EXPERTS), 0.01),
        'w_gu': nrm((L, N_EXPERTS, D, 2 * EXPERT_FF), D ** -0.5),
        'b_gu': nrm((L, N_EXPERTS, 2 * EXPERT_FF), 0.02),
        'w_down': nrm((L, N_EXPERTS, EXPERT_FF, D), EXPERT_FF ** -0.5),
        'b_down': nrm((L, N_EXPERTS, D), 0.02),
        'final_norm_w': gain((D,)),
    }


def reference(x_prompt, x_sample, cache_ckv, cache_kpe, state_ssm_fwd, state_ssm_bwd, c, c_ctx,
              w_ada, b_ada, norm1_w, w_in, conv_a_w, w_out_a, q_norm_w, w_uq, kv_norm_w, w_uk, w_uv,
              w_o_mla, ssm_conv_w, ssm_conv_b, dt_bias_fwd, dt_bias_bwd, a_log_fwd, a_log_bwd, d_skip,
              ssm_norm_w, w_o_ssm, w_o, norm2_w, router_w, router_b, w_gu, b_gu, w_down, b_down,
              final_norm_w):
    p = {
        'w_ada': w_ada, 'b_ada': b_ada, 'norm1_w': norm1_w, 'w_in': w_in,
        'conv_a_w': conv_a_w, 'w_out_a': w_out_a,
        'q_norm_w': q_norm_w, 'w_uq': w_uq, 'kv_norm_w': kv_norm_w, 'w_uk': w_uk, 'w_uv': w_uv,
        'w_o_mla': w_o_mla,
        'ssm_conv_w': ssm_conv_w, 'ssm_conv_b': ssm_conv_b, 'dt_bias_fwd': dt_bias_fwd,
        'dt_bias_bwd': dt_bias_bwd, 'a_log_fwd': a_log_fwd, 'a_log_bwd': a_log_bwd, 'd_skip': d_skip,
        'ssm_norm_w': ssm_norm_w, 'w_o_ssm': w_o_ssm, 'w_o': w_o,
        'norm2_w': norm2_w, 'router_w': router_w, 'router_b': router_b,
        'w_gu': w_gu, 'b_gu': b_gu, 'w_down': w_down, 'b_down': b_down,
    }

    xc = x_prompt
    ckv_list, kpe_list, sf_list, sb_list = [], [], [], []
    for l in range(DEPTH):
        xc, (ckv_l, kpe_l, sf_l, sb_l) = trunk_layer(xc, c_ctx[None, :], p, l, None, None)
        ckv_list.append(ckv_l)
        kpe_list.append(kpe_l)
        sf_list.append(sf_l)
        sb_list.append(sb_l)
    y_prompt = rmsnorm(xc, final_norm_w)
    new_cache_ckv = jnp.stack(ckv_list, axis=1)
    new_cache_kpe = jnp.stack(kpe_list, axis=1)
    new_state_ssm_fwd = jnp.stack(sf_list, axis=1)
    new_state_ssm_bwd = jnp.stack(sb_list, axis=1)

    angles = axial_angles(x_sample.shape[1])
    xs = x_sample
    for l in range(DEPTH):
        ctx = (cache_ckv[:, l], cache_kpe[:, l], state_ssm_fwd[:, l], state_ssm_bwd[:, l])
        xs, _ = trunk_layer(xs, c, p, l, angles, ctx)
    y_sample = rmsnorm(xs, final_norm_w)

    return (y_prompt, y_sample, new_cache_ckv, new_cache_kpe, new_state_ssm_fwd, new_state_ssm_bwd)
```

```python
import functools
import math

import jax
import jax.numpy as jnp
from jax import lax
from jax.experimental import pallas as pl
from jax.experimental.pallas import tpu as pltpu

F32 = jnp.float32
BF16 = jnp.bfloat16
I32 = jnp.int32

D_MODEL = 1024
BATCH = 16
SEQ = 256
DEPTH = 2
DEC_BATCH = 2
DEC_SEQ = 2048
PAST_LEN = 512
GRID_W = 64
NORM_EPS = 1e-6
CONV_DIM = 512
MLA_HEADS = 8
Q_RANK = 384
KV_RANK = 256
QK_NOPE = 64
QK_ROPE = 32
V_HEAD = 64
ROPE_BASE = 10000.0
MLA_SCALE = (QK_NOPE + QK_ROPE) ** -0.5
SSM_HEADS = 16
SSM_HEAD_DIM = 64
SSM_INNER = SSM_HEADS * SSM_HEAD_DIM
SSM_GROUPS = 2
SSM_STATE = 128
SSM_CONV_CH = SSM_INNER + 2 * SSM_GROUPS * SSM_STATE
N_EXPERTS = 32
TOP_K = 4
EXPERT_FF = D_MODEL
SWIGLU_ALPHA = 1.702
SWIGLU_LIMIT = 7.0

N_CTX = BATCH * SEQ
N_LAT = DEC_BATCH * DEC_SEQ
N_TOK = N_CTX + N_LAT
N_SEQS = BATCH + DEC_BATCH

LANES = 128
SUBLANES = 8
HEAD_PAD = 128
TM = 256
N_TILES = N_TOK // TM
CTX_TILES = N_CTX // TM
LAT_TILES_PER_SEQ = DEC_SEQ // TM
CHUNK = 128
CTX_CHUNKS_PER_SEQ = SEQ // CHUNK
LAT_CHUNKS_PER_SEQ = DEC_SEQ // CHUNK
N_CTX_CHUNKS = N_CTX // CHUNK
N_CHUNKS = N_TOK // CHUNK
MOE_BLK = 256
N_ASSIGN = N_TOK * TOP_K
N_SLOTS = N_ASSIGN + N_EXPERTS * MOE_BLK
N_BLOCKS = N_SLOTS // MOE_BLK
VMEM_LIMIT = 56 * 1024 * 1024

C_A3 = 0
C_CQ = C_A3 + 3 * CONV_DIM
C_CKV = C_CQ + Q_RANK
C_Z = C_CKV + KV_RANK
C_XBC = C_Z + SSM_INNER
C_GATE = C_XBC + SSM_CONV_CH
C_SMALL = C_GATE + 3 * D_MODEL
IN_COLS2 = C_SMALL + LANES
SM_DTF = QK_ROPE
SM_DTB = QK_ROPE + SSM_HEADS
SM_KPE_ROT = 64


def _rms(x, w):
    return x * lax.rsqrt(jnp.mean(x * x, axis=-1, keepdims=True) + NORM_EPS) * w


def _silu(x):
    return x * jax.nn.sigmoid(x)


def _dot(a, b):
    return jnp.dot(a, b, preferred_element_type=F32)


def _dot_nt(a, b):
    return lax.dot_general(a, b, (((1,), (1,)), ((), ())), preferred_element_type=F32)


def _resident(shape):
    nd = len(shape)
    return pl.BlockSpec(shape, lambda *_: (0,) * nd, pipeline_mode=pl.Buffered(1))


def _mod_row(i):
    return jnp.where(i < CTX_TILES, 0, 1 + (i - CTX_TILES) // LAT_TILES_PER_SEQ)


def _pos_block(i):
    return jnp.where(i < CTX_TILES, 0, 1 + (i - CTX_TILES) % LAT_TILES_PER_SEQ)


def _ada_kernel(c_ref, w_ref, b_ref, o_ref):
    s = _silu(c_ref[...]).astype(BF16)
    o_ref[0] = _dot(s, w_ref[0].astype(BF16)) + b_ref[0]


def _ada_mods(cond8, w_ada, b_ada):
    tn = 1536
    n_mod = 6 * D_MODEL
    return pl.pallas_call(
        _ada_kernel,
        grid=(DEPTH, n_mod // tn),
        in_specs=[
            pl.BlockSpec((SUBLANES, D_MODEL), lambda l, j: (0, 0)),
            pl.BlockSpec((1, D_MODEL, tn), lambda l, j: (l, 0, j)),
            pl.BlockSpec((1, 1, tn), lambda l, j: (l, 0, j)),
        ],
        out_specs=pl.BlockSpec((1, SUBLANES, tn), lambda l, j: (l, 0, j)),
        out_shape=jax.ShapeDtypeStruct((DEPTH, SUBLANES, n_mod), F32),
        compiler_params=pltpu.CompilerParams(dimension_semantics=("arbitrary", "arbitrary")),
        name="ada_mods",
    )(cond8, w_ada, b_ada.reshape(DEPTH, 1, n_mod))


def _inproj_kernel(x_ref, mod_ref, n1w_ref, w_ref, qnw_ref, wqa_ref, wqb_ref, kvnw_ref, wuk_ref, wuv_ref,
                   tile_ref, cosq_ref, sinq_ref, cosk_ref, sink_ref,
                   a2_ref, q_ref, ckv_ref, kf_ref, v_ref, small_ref, z_ref, xbc_ref, g_ref):
    mod = mod_ref[0]
    shift1 = mod[:, 0:D_MODEL]
    scale1 = mod[:, D_MODEL:2 * D_MODEL]
    hb = (_rms(x_ref[...], n1w_ref[...]) * (1.0 + scale1) + shift1).astype(BF16)

    def seg(a, b):
        return _dot(hb, w_ref[:, a:b])

    a3 = seg(C_A3, C_CQ)
    a2_ref[:, 0:CONV_DIM] = a3[:, 2 * CONV_DIM:3 * CONV_DIM] * a3[:, 0:CONV_DIM]
    a2_ref[:, CONV_DIM:2 * CONV_DIM] = a3[:, CONV_DIM:2 * CONV_DIM]

    cqn = _rms(seg(C_CQ, C_CKV), qnw_ref[...]).astype(BF16)
    q = _dot(cqn, wqa_ref[...]) * cosq_ref[...] + _dot(cqn, wqb_ref[...]) * sinq_ref[...]
    q_ref[...] = q.astype(BF16)

    ckv = _rms(seg(C_CKV, C_Z), kvnw_ref[...])
    ckv_ref[...] = ckv
    ckvb = ckv.astype(BF16)
    small = seg(C_SMALL, IN_COLS2)
    small_ref[...] = small
    kpe = small * cosk_ref[...] + pltpu.roll(small, LANES - SM_KPE_ROT, 1) * sink_ref[...]
    kf_ref[...] = (_dot(ckvb, wuk_ref[...]) + _dot(kpe.astype(BF16), tile_ref[...])).astype(BF16)
    v_ref[...] = _dot(ckvb, wuv_ref[...]).astype(BF16)

    z_ref[...] = seg(C_Z, C_XBC)
    xbc_ref[...] = seg(C_XBC, C_GATE)
    g_ref[...] = jax.nn.sigmoid(seg(C_GATE, C_SMALL))


def _inproj(x, mod3, lw, tabs):
    row = lambda n: pl.BlockSpec((TM, n), lambda i: (i, 0))
    tab = lambda n: pl.BlockSpec((TM, n), lambda i: (_pos_block(i), 0))
    qw = MLA_HEADS * HEAD_PAD
    vw = MLA_HEADS * V_HEAD
    out_shape = (
        jax.ShapeDtypeStruct((N_TOK, 2 * CONV_DIM), F32),
        jax.ShapeDtypeStruct((N_TOK, qw), BF16),
        jax.ShapeDtypeStruct((N_TOK, KV_RANK), F32),
        jax.ShapeDtypeStruct((N_TOK, qw), BF16),
        jax.ShapeDtypeStruct((N_TOK, vw), BF16),
        jax.ShapeDtypeStruct((N_TOK, LANES), F32),
        jax.ShapeDtypeStruct((N_TOK, SSM_INNER), F32),
        jax.ShapeDtypeStruct((N_TOK, SSM_CONV_CH), F32),
        jax.ShapeDtypeStruct((N_TOK, 3 * D_MODEL), F32),
    )
    return pl.pallas_call(
        _inproj_kernel,
        grid=(N_TILES,),
        in_specs=[
            row(D_MODEL),
            pl.BlockSpec((1, 1, 6 * D_MODEL), lambda i: (_mod_row(i), 0, 0)),
            _resident((1, D_MODEL)),
            _resident((D_MODEL, IN_COLS2)),
            _resident((1, Q_RANK)),
            _resident((Q_RANK, qw)),
            _resident((Q_RANK, qw)),
            _resident((1, KV_RANK)),
            _resident((KV_RANK, qw)),
            _resident((KV_RANK, vw)),
            _resident((LANES, qw)),
            tab(qw), tab(qw), tab(LANES), tab(LANES),
        ],
        out_specs=tuple(row(s.shape[1]) for s in out_shape),
        out_shape=out_shape,
        compiler_params=pltpu.CompilerParams(dimension_semantics=("arbitrary",), vmem_limit_bytes=VMEM_LIMIT),
        name="inproj",
    )(x, mod3, lw["norm1_w"], lw["w_in2"], lw["q_norm_w"], lw["wq_a"], lw["wq_b"], lw["kv_norm_w"],
      lw["wuk"], lw["wuv"], tabs["tile"], tabs["cosq"], tabs["sinq"], tabs["cosk"], tabs["sink"])


def _kvcache_kernel(ckv_ref, kpe_ref, wuk_ref, wuv_ref, tile_ref, kf_ref, v_ref):
    ckvb = ckv_ref[...].astype(BF16)
    kf_ref[...] = (_dot(ckvb, wuk_ref[...]) + _dot(kpe_ref[...].astype(BF16), tile_ref[...])).astype(BF16)
    v_ref[...] = _dot(ckvb, wuv_ref[...]).astype(BF16)


def _kvcache(ckv, kpe128, lw, tabs):
    n = ckv.shape[0]
    qw = MLA_HEADS * HEAD_PAD
    vw = MLA_HEADS * V_HEAD
    return pl.pallas_call(
        _kvcache_kernel,
        grid=(n // PAST_LEN,),
        in_specs=[
            pl.BlockSpec((PAST_LEN, KV_RANK), lambda i: (i, 0)),
            pl.BlockSpec((PAST_LEN, LANES), lambda i: (i, 0)),
            _resident((KV_RANK, qw)),
            _resident((KV_RANK, vw)),
            _resident((LANES, qw)),
        ],
        out_specs=(pl.BlockSpec((PAST_LEN, qw), lambda i: (i, 0)), pl.BlockSpec((PAST_LEN, vw), lambda i: (i, 0))),
        out_shape=(jax.ShapeDtypeStruct((n, qw), BF16), jax.ShapeDtypeStruct((n, vw), BF16)),
        compiler_params=pltpu.CompilerParams(dimension_semantics=("arbitrary",)),
        name="kvcache",
    )(ckv, kpe128, lw["wuk"], lw["wuv"], tabs["tile"])


def _conv_kernel(a2_ref, sp_ref, sn_ref, xbc_ref, xp_ref, xn_ref, caw_ref, cw_ref, cb_ref, ta_ref, xact_ref):
    i = pl.program_id(0)
    j = (i - CTX_TILES) % LAT_TILES_PER_SEQ
    is_ctx = i < CTX_TILES
    keep_prev = jnp.where(jnp.logical_or(is_ctx, j == 0), 0.0, 1.0)
    keep_next = jnp.where(jnp.logical_or(is_ctx, j == LAT_TILES_PER_SEQ - 1), 0.0, 1.0)
    row = lax.broadcasted_iota(I32, (TM, 1), 0)

    def conv3(x, prev_row, next_row, w_ref):
        x_prev = jnp.where(row == 0, prev_row * keep_prev, pltpu.roll(x, 1, 0))
        x_next = jnp.where(row == TM - 1, next_row * keep_next, pltpu.roll(x, TM - 1, 0))
        return x_prev * w_ref[0:1, :] + x * w_ref[1:2, :] + x_next * w_ref[2:3, :]

    s = a2_ref[:, 0:CONV_DIM]
    gb = a2_ref[:, CONV_DIM:2 * CONV_DIM]
    ta_ref[...] = (gb * conv3(s, sp_ref[SUBLANES - 1:SUBLANES, :], sn_ref[0:1, :], caw_ref)).astype(BF16)
    xc = conv3(xbc_ref[...], xp_ref[SUBLANES - 1:SUBLANES, :], xn_ref[0:1, :], cw_ref) + cb_ref[...]
    xact_ref[...] = _silu(xc)


def _convs(a2, xbc, lw):
    per = TM // SUBLANES
    last = N_TOK // SUBLANES - 1
    prev = lambda n: pl.BlockSpec((SUBLANES, n), lambda i: (jnp.maximum(i * per - 1, 0), 0))
    nxt = lambda n: pl.BlockSpec((SUBLANES, n), lambda i: (jnp.minimum((i + 1) * per, last), 0))
    return pl.pallas_call(
        _conv_kernel,
        grid=(N_TILES,),
        in_specs=[
            pl.BlockSpec((TM, 2 * CONV_DIM), lambda i: (i, 0)), prev(CONV_DIM), nxt(CONV_DIM),
            pl.BlockSpec((TM, SSM_CONV_CH), lambda i: (i, 0)), prev(SSM_CONV_CH), nxt(SSM_CONV_CH),
            pl.BlockSpec((3, CONV_DIM), lambda i: (0, 0)),
            pl.BlockSpec((3, SSM_CONV_CH), lambda i: (0, 0)),
            pl.BlockSpec((1, SSM_CONV_CH), lambda i: (0, 0)),
        ],
        out_specs=(pl.BlockSpec((TM, CONV_DIM), lambda i: (i, 0)), pl.BlockSpec((TM, SSM_CONV_CH), lambda i: (i, 0))),
        out_shape=(jax.ShapeDtypeStruct((N_TOK, CONV_DIM), BF16), jax.ShapeDtypeStruct((N_TOK, SSM_CONV_CH), F32)),
        compiler_params=pltpu.CompilerParams(dimension_semantics=("arbitrary",)),
        name="convs",
    )(a2, a2, a2, xbc, xbc, xbc, lw["conv_a_w"], lw["ssm_conv_w"], lw["ssm_conv_b"])


def _attn_heads(q_ref, kv_refs, o_ref, acc_ref):
    for h in range(MLA_HEADS):
        qh = q_ref[:, h * HEAD_PAD:(h + 1) * HEAD_PAD]
        ss = [_dot_nt(qh, k_ref[:, h * HEAD_PAD:(h + 1) * HEAD_PAD]) * MLA_SCALE for k_ref, _ in kv_refs]
        m = functools.reduce(jnp.maximum, [jnp.max(s, axis=-1, keepdims=True) for s in ss])
        ps = [jnp.exp(s - m) for s in ss]
        l = functools.reduce(jnp.add, [jnp.sum(p, axis=-1, keepdims=True) for p in ps])
        o = functools.reduce(jnp.add, [_dot(p.astype(BF16), v_ref[:, h * V_HEAD:(h + 1) * V_HEAD])
                                       for p, (_, v_ref) in zip(ps, kv_refs)])
        acc_ref[:, h * V_HEAD:(h + 1) * V_HEAD] = o / l
    o_ref[...] = acc_ref[...].astype(BF16)


def _attn_ctx_kernel(q_ref, k_ref, v_ref, o_ref, acc_ref):
    _attn_heads(q_ref, [(k_ref, v_ref)], o_ref, acc_ref)


def _attn_lat_kernel(q_ref, k_ref, v_ref, kc_ref, vc_ref, o_ref, acc_ref):
    _attn_heads(q_ref, [(k_ref, v_ref), (kc_ref, vc_ref)], o_ref, acc_ref)


def _attention(q, kf, v, kf_c, v_c):
    qw = MLA_HEADS * HEAD_PAD
    vw = MLA_HEADS * V_HEAD
    att_ctx = pl.pallas_call(
        _attn_ctx_kernel,
        grid=(BATCH,),
        in_specs=[pl.BlockSpec((SEQ, qw), lambda b: (b, 0)), pl.BlockSpec((SEQ, qw), lambda b: (b, 0)),
                  pl.BlockSpec((SEQ, vw), lambda b: (b, 0))],
        out_specs=pl.BlockSpec((SEQ, vw), lambda b: (b, 0)),
        out_shape=jax.ShapeDtypeStruct((N_CTX, vw), BF16),
        scratch_shapes=[pltpu.VMEM((SEQ, vw), F32)],
        compiler_params=pltpu.CompilerParams(dimension_semantics=("arbitrary",)),
        name="attn_ctx",
    )(q, kf, v)
    lat0 = N_CTX // DEC_SEQ
    att_lat = pl.pallas_call(
        _attn_lat_kernel,
        grid=(DEC_BATCH, LAT_TILES_PER_SEQ),
        in_specs=[
            pl.BlockSpec((TM, qw), lambda b, t: (CTX_TILES + b * LAT_TILES_PER_SEQ + t, 0)),
            pl.BlockSpec((DEC_SEQ, qw), lambda b, t: (lat0 + b, 0)),
            pl.BlockSpec((DEC_SEQ, vw), lambda b, t: (lat0 + b, 0)),
            pl.BlockSpec((PAST_LEN, qw), lambda b, t: (b, 0)),
            pl.BlockSpec((PAST_LEN, vw), lambda b, t: (b, 0)),
        ],
        out_specs=pl.BlockSpec((TM, vw), lambda b, t: (b * LAT_TILES_PER_SEQ + t, 0)),
        out_shape=jax.ShapeDtypeStruct((N_LAT, vw), BF16),
        scratch_shapes=[pltpu.VMEM((TM, vw), F32)],
        compiler_params=pltpu.CompilerParams(dimension_semantics=("arbitrary", "arbitrary"),
                                             vmem_limit_bytes=VMEM_LIMIT),
        name="attn_lat",
    )(q, kf, v, kf_c, v_c)
    return jnp.concatenate([att_ctx, att_lat], axis=0)


def _seq_of_chunk(s):
    return jnp.where(s < N_CTX_CHUNKS, s // CTX_CHUNKS_PER_SEQ,
                     BATCH + (s - N_CTX_CHUNKS) // LAT_CHUNKS_PER_SEQ)


def _chunk_in_seq(s):
    return jnp.where(s < N_CTX_CHUNKS, s % CTX_CHUNKS_PER_SEQ, (s - N_CTX_CHUNKS) % LAT_CHUNKS_PER_SEQ)


def _chunks_in_seq(s):
    return jnp.where(s < N_CTX_CHUNKS, CTX_CHUNKS_PER_SEQ, LAT_CHUNKS_PER_SEQ)


def _mirror_chunk(s):
    return s + _chunks_in_seq(s) - 1 - 2 * _chunk_in_seq(s)


def _split3(a):
    a1 = a.astype(BF16)
    r1 = a - a1.astype(F32)
    a2 = r1.astype(BF16)
    a3 = (r1 - a2.astype(F32)).astype(BF16)
    return a1, a2, a3


def _ssd_direction(x_ref, sm_ref, par_ref, st_ref, xdd_ref, y_ref, lane0, backward):
    ri = lax.broadcasted_iota(I32, (CHUNK, CHUNK), 0)
    ci = lax.broadcasted_iota(I32, (CHUNK, CHUNK), 1)
    tri = (ci >= ri) if backward else (ci <= ri)
    tri_b = jnp.where(tri, 1.0, 0.0).astype(BF16)
    tot_row = 0 if backward else CHUNK - 1

    dt = jax.nn.softplus(sm_ref[...] + par_ref[0:1, :])
    a = dt * (-jnp.exp(par_ref[1:2, :])) * par_ref[2:3, :]
    a1, a2, a3 = _split3(a)
    acs = _dot(tri_b, a1) + _dot(tri_b, a2) + _dot(tri_b, a3)
    acs_t = acs.T

    for g in range(SSM_GROUPS):
        b0 = SSM_INNER + g * SSM_STATE
        c0 = SSM_INNER + SSM_GROUPS * SSM_STATE + g * SSM_STATE
        bg = x_ref[:, b0:b0 + SSM_STATE]
        cgb = x_ref[:, c0:c0 + SSM_STATE].astype(BF16)
        cb = _dot_nt(cgb, bg.astype(BF16))
        bgt = bg.T.astype(BF16)
        heads = SSM_HEADS // SSM_GROUPS
        gw = heads * SSM_HEAD_DIM
        y_off = _dot(cgb, st_ref[:, g * gw:(g + 1) * gw].astype(BF16))
        for hh in range(heads):
            h = g * heads + hh
            lane = lane0 + h
            col = acs[:, lane:lane + 1]
            diff = col - acs_t[lane:lane + 1, :]
            scores = (cb * jnp.exp(jnp.where(tri, diff, -jnp.inf))).astype(BF16)
            xd = x_ref[:, h * SSM_HEAD_DIM:(h + 1) * SSM_HEAD_DIM] * dt[:, lane:lane + 1]
            y_diag = _dot(scores, xd.astype(BF16))
            y_ref[:, h * SSM_HEAD_DIM:(h + 1) * SSM_HEAD_DIM] = (
                y_diag + y_off[:, hh * SSM_HEAD_DIM:(hh + 1) * SSM_HEAD_DIM] * jnp.exp(col))
            tot = acs[tot_row:tot_row + 1, lane:lane + 1]
            xdd_ref[:, h * SSM_HEAD_DIM:(h + 1) * SSM_HEAD_DIM] = xd * jnp.exp(tot - col)
        upd = _dot(bgt, xdd_ref[:, g * gw:(g + 1) * gw].astype(BF16))
        for hh in range(heads):
            h = g * heads + hh
            tot = acs[tot_row:tot_row + 1, lane0 + h:lane0 + h + 1]
            sl = slice(h * SSM_HEAD_DIM, (h + 1) * SSM_HEAD_DIM)
            st_ref[:, sl] = st_ref[:, sl] * jnp.exp(tot) + upd[:, hh * SSM_HEAD_DIM:(hh + 1) * SSM_HEAD_DIM]


def _ssd_kernel(xf_ref, xb_ref, smf_ref, smb_ref, if_ref, ib_ref, par_ref,
                yf_ref, yb_ref, sf_ref, sb_ref, stf_ref, stb_ref, xdd_ref):
    s = pl.program_id(0)
    c = _chunk_in_seq(s)

    @pl.when(c == 0)
    def _():
        stf_ref[...] = if_ref[0].T
        stb_ref[...] = ib_ref[0].T

    _ssd_direction(xf_ref, smf_ref, par_ref, stf_ref, xdd_ref, yf_ref, SM_DTF, False)
    _ssd_direction(xb_ref, smb_ref, par_ref, stb_ref, xdd_ref, yb_ref, SM_DTB, True)

    @pl.when(c == _chunks_in_seq(s) - 1)
    def _():
        sf_ref[0] = stf_ref[...].T
        sb_ref[0] = stb_ref[...].T


def _ssd(xact, small, init_f, init_b, par):
    hp = SSM_INNER
    fwd = lambda n: pl.BlockSpec((CHUNK, n), lambda s: (s, 0))
    bwd = lambda n: pl.BlockSpec((CHUNK, n), lambda s: (_mirror_chunk(s), 0))
    st = pl.BlockSpec((1, hp, SSM_STATE), lambda s: (_seq_of_chunk(s), 0, 0))
    return pl.pallas_call(
        _ssd_kernel,
        grid=(N_CHUNKS,),
        in_specs=[fwd(SSM_CONV_CH), bwd(SSM_CONV_CH), fwd(LANES), bwd(LANES), st, st,
                  pl.BlockSpec((SUBLANES, LANES), lambda s: (0, 0))],
        out_specs=(fwd(hp), bwd(hp), st, st),
        out_shape=(jax.ShapeDtypeStruct((N_TOK, hp), F32), jax.ShapeDtypeStruct((N_TOK, hp), F32),
                   jax.ShapeDtypeStruct((N_SEQS, hp, SSM_STATE), F32),
                   jax.ShapeDtypeStruct((N_SEQS, hp, SSM_STATE), F32)),
        scratch_shapes=[pltpu.VMEM((SSM_STATE, hp), F32), pltpu.VMEM((SSM_STATE, hp), F32),
                        pltpu.VMEM((CHUNK, hp), F32)],
        compiler_params=pltpu.CompilerParams(dimension_semantics=("arbitrary",)),
        name="ssd",
    )(xact, xact, small, small, init_f, init_b, par)


def _merge_kernel(ta_ref, att_ref, yf_ref, yb_ref, xs_ref, z_ref, g_ref, x_ref, mod_ref,
                  woa_ref, wom_ref, dsk_ref, snw_ref, wos_ref, wo_ref, n2w_ref, rw_ref, rb_ref,
                  x1_ref, h2_ref, ti_ref, tg_ref):
    mod = mod_ref[0]
    gate1 = mod[:, 2 * D_MODEL:3 * D_MODEL]
    shift2 = mod[:, 3 * D_MODEL:4 * D_MODEL]
    scale2 = mod[:, 4 * D_MODEL:5 * D_MODEL]
    y_a = _dot(ta_ref[...], woa_ref[...])
    y_b = _dot(att_ref[...], wom_ref[...])
    y_ssm = (yf_ref[...] + yb_ref[...] + dsk_ref[...] * xs_ref[...]) * _silu(z_ref[...])
    y_c = _dot(_rms(y_ssm, snw_ref[...]).astype(BF16), wos_ref[...])
    merged = (g_ref[:, 0:D_MODEL] * y_a + g_ref[:, D_MODEL:2 * D_MODEL] * y_b
              + g_ref[:, 2 * D_MODEL:3 * D_MODEL] * y_c)
    x1 = x_ref[...] + gate1 * _dot(merged.astype(BF16), wo_ref[...])
    x1_ref[...] = x1
    h2 = _rms(x1, n2w_ref[...]) * (1.0 + scale2) + shift2
    h2_ref[...] = h2

    logits = _dot(h2.astype(BF16), rw_ref[...]) + rb_ref[...]
    lane = lax.broadcasted_iota(I32, (TM, LANES), 1)
    ti = jnp.zeros((TM, LANES), I32)
    tv = jnp.full((TM, LANES), -jnp.inf, F32)
    for k in range(TOP_K):
        m = jnp.max(logits, axis=-1, keepdims=True)
        idx = jnp.min(jnp.where(logits == m, lane, LANES), axis=-1, keepdims=True)
        ti = jnp.where(lane == k, idx, ti)
        tv = jnp.where(lane == k, m, tv)
        logits = jnp.where(lane == idx, -jnp.inf, logits)
    e = jnp.exp(tv - jnp.max(tv, axis=-1, keepdims=True))
    ti_ref[...] = ti
    tg_ref[...] = e / jnp.sum(e, axis=-1, keepdims=True)


def _merge(ta, att, yf, yb, xact, z, g, x, mod3, lw):
    row = lambda n: pl.BlockSpec((TM, n), lambda i: (i, 0))
    out_shape = (jax.ShapeDtypeStruct((N_TOK, D_MODEL), F32), jax.ShapeDtypeStruct((N_TOK, D_MODEL), F32),
                 jax.ShapeDtypeStruct((N_TOK, LANES), I32), jax.ShapeDtypeStruct((N_TOK, LANES), F32))
    return pl.pallas_call(
        _merge_kernel,
        grid=(N_TILES,),
        in_specs=[
            row(CONV_DIM), row(MLA_HEADS * V_HEAD), row(SSM_INNER), row(SSM_INNER), row(SSM_INNER),
            row(SSM_INNER), row(3 * D_MODEL), row(D_MODEL),
            pl.BlockSpec((1, 1, 6 * D_MODEL), lambda i: (_mod_row(i), 0, 0)),
            _resident((CONV_DIM, D_MODEL)), _resident((MLA_HEADS * V_HEAD, D_MODEL)),
            _resident((1, SSM_INNER)), _resident((1, SSM_INNER)), _resident((SSM_INNER, D_MODEL)),
            _resident((D_MODEL, D_MODEL)), _resident((1, D_MODEL)),
            _resident((D_MODEL, LANES)), _resident((1, LANES)),
        ],
        out_specs=tuple(row(s.shape[1]) for s in out_shape),
        out_shape=out_shape,
        compiler_params=pltpu.CompilerParams(dimension_semantics=("arbitrary",), vmem_limit_bytes=VMEM_LIMIT),
        name="merge",
    )(ta, att, yf, yb, xact, z, g, x, mod3, lw["w_out_a"], lw["w_o_mla"], lw["d_skip"], lw["ssm_norm_w"],
      lw["w_o_ssm"], lw["w_o"], lw["norm2_w"], lw["router_w"], lw["router_b"])


def _moe_kernel(be_ref, nu_ref, tc_ref, tn_ref, h_ref, wgu_ref, wdn_ref, bg_ref, bu_ref, bd_ref, sel_ref,
                y_ref, xbuf, wgu_s, wdn_s, sem):
    i = pl.program_id(0)
    n_used = nu_ref[0]
    slot = i % 2

    def gather(tok_ref, dst_slot):
        def body(r, carry):
            t = tok_ref[0, 0, r]
            pltpu.make_async_copy(h_ref.at[pl.ds(t, 1), :], xbuf.at[dst_slot, pl.ds(r, 1), :],
                                  sem.at[dst_slot]).start()
            return carry
        lax.fori_loop(0, MOE_BLK, body, 0, unroll=8)

    @pl.when(jnp.logical_and(i == 0, n_used > 0))
    def _():
        gather(tc_ref, 0)

    @pl.when(i + 1 < n_used)
    def _():
        gather(tn_ref, 1 - slot)

    @pl.when(i < n_used)
    def _():
        pltpu.make_async_copy(h_ref.at[pl.ds(0, MOE_BLK), :], xbuf.at[slot], sem.at[slot]).wait()

        @pl.when(jnp.logical_or(i == 0, be_ref[i] != be_ref[jnp.maximum(i - 1, 0)]))
        def _():
            half = LANES
            for c in range(2 * EXPERT_FF // (2 * half)):
                r = _dot(wgu_ref[:, c * 2 * half:(c + 1) * 2 * half].astype(BF16), sel_ref[...])
                wgu_s[:, c * half:(c + 1) * half] = r[:, 0:half].astype(BF16)
                wgu_s[:, EXPERT_FF + c * half:EXPERT_FF + (c + 1) * half] = r[:, half:2 * half].astype(BF16)
            wdn_s[...] = wdn_ref[...].astype(BF16)

        gu = _dot(xbuf[slot].astype(BF16), wgu_s[...])
        gate = jnp.minimum(gu[:, 0:EXPERT_FF] + bg_ref[0], SWIGLU_LIMIT)
        up = jnp.clip(gu[:, EXPERT_FF:2 * EXPERT_FF] + bu_ref[0], -SWIGLU_LIMIT, SWIGLU_LIMIT)
        act = gate * jax.nn.sigmoid(SWIGLU_ALPHA * gate) * (up + 1.0)
        y_ref[...] = _dot(act.astype(BF16), wdn_s[...]) + bd_ref[0]

    @pl.when(i >= n_used)
    def _():
        y_ref[...] = jnp.zeros_like(y_ref)


def _moe(layer, blk_e, n_used, slot_tok, h2, w_gu, w_down, bg, bu, bd, sel):
    tok3 = slot_tok.reshape(N_BLOCKS, 1, MOE_BLK)
    bias = pl.BlockSpec((1, 1, D_MODEL), lambda i, be, nu: (be[i], 0, 0))
    return pl.pallas_call(
        _moe_kernel,
        grid_spec=pltpu.PrefetchScalarGridSpec(
            num_scalar_prefetch=2,
            grid=(N_BLOCKS,),
            in_specs=[
                pl.BlockSpec((1, 1, MOE_BLK), lambda i, be, nu: (i, 0, 0), memory_space=pltpu.SMEM),
                pl.BlockSpec((1, 1, MOE_BLK), lambda i, be, nu: (jnp.minimum(i + 1, N_BLOCKS - 1), 0, 0),
                             memory_space=pltpu.SMEM),
                pl.BlockSpec(memory_space=pl.ANY),
                pl.BlockSpec((None, None, D_MODEL, 2 * EXPERT_FF), lambda i, be, nu: (layer, be[i], 0, 0)),
                pl.BlockSpec((None, None, EXPERT_FF, D_MODEL), lambda i, be, nu: (layer, be[i], 0, 0)),
                bias, bias, bias,
                pl.BlockSpec((2 * LANES, 2 * LANES), lambda i, be, nu: (0, 0)),
            ],
            out_specs=pl.BlockSpec((MOE_BLK, D_MODEL), lambda i, be, nu: (i, 0)),
            scratch_shapes=[
                pltpu.VMEM((2, MOE_BLK, D_MODEL), F32),
                pltpu.VMEM((D_MODEL, 2 * EXPERT_FF), BF16),
                pltpu.VMEM((EXPERT_FF, D_MODEL), BF16),
                pltpu.SemaphoreType.DMA((2,)),
            ],
        ),
        out_shape=jax.ShapeDtypeStruct((N_SLOTS, D_MODEL), F32),
        compiler_params=pltpu.CompilerParams(dimension_semantics=("arbitrary",), vmem_limit_bytes=VMEM_LIMIT),
        name="moe",
    )(blk_e, n_used, tok3, tok3, h2, w_gu, w_down, bg, bu, bd, sel)


def _combine_kernel(dc_ref, dn_ref, yb_ref, tg_ref, x1_ref, mod_ref, fw_ref, x2_ref, yn_ref, buf, sem):
    i = pl.program_id(0)
    slot = i % 2

    def gather(d_ref, dst_slot):
        def body(t, carry):
            for k in range(TOP_K):
                src = d_ref[0, 0, t * TOP_K + k]
                pltpu.make_async_copy(yb_ref.at[pl.ds(src, 1), :], buf.at[dst_slot, k, pl.ds(t, 1), :],
                                      sem.at[dst_slot]).start()
            return carry
        lax.fori_loop(0, TM, body, 0, unroll=2)

    @pl.when(i == 0)
    def _():
        gather(dc_ref, 0)

    @pl.when(i + 1 < N_TILES)
    def _():
        gather(dn_ref, 1 - slot)

    for k in range(TOP_K):
        pltpu.make_async_copy(yb_ref.at[pl.ds(0, TM), :], buf.at[slot, k], sem.at[slot]).wait()
    tg = tg_ref[...]
    moe = tg[:, 0:1] * buf[slot, 0]
    for k in range(1, TOP_K):
        moe = moe + tg[:, k:k + 1] * buf[slot, k]
    gate2 = mod_ref[0][:, 5 * D_MODEL:6 * D_MODEL]
    x2 = x1_ref[...] + gate2 * moe
    x2_ref[...] = x2
    yn_ref[...] = _rms(x2, fw_ref[...])


def _combine(dest, yb, tg, x1, mod3, final_w):
    d3 = dest.reshape(N_TILES, 1, TM * TOP_K)
    row = lambda n: pl.BlockSpec((TM, n), lambda i: (i, 0))
    return pl.pallas_call(
        _combine_kernel,
        grid=(N_TILES,),
        in_specs=[
            pl.BlockSpec((1, 1, TM * TOP_K), lambda i: (i, 0, 0), memory_space=pltpu.SMEM),
            pl.BlockSpec((1, 1, TM * TOP_K), lambda i: (jnp.minimum(i + 1, N_TILES - 1), 0, 0),
                         memory_space=pltpu.SMEM),
            pl.BlockSpec(memory_space=pl.ANY),
            row(LANES), row(D_MODEL),
            pl.BlockSpec((1, 1, 6 * D_MODEL), lambda i: (_mod_row(i), 0, 0)),
            pl.BlockSpec((1, D_MODEL), lambda i: (0, 0)),
        ],
        out_specs=(row(D_MODEL), row(D_MODEL)),
        out_shape=(jax.ShapeDtypeStruct((N_TOK, D_MODEL), F32), jax.ShapeDtypeStruct((N_TOK, D_MODEL), F32)),
        scratch_shapes=[pltpu.VMEM((2, TOP_K, TM, D_MODEL), F32), pltpu.SemaphoreType.DMA((2,))],
        compiler_params=pltpu.CompilerParams(dimension_semantics=("arbitrary",), vmem_limit_bytes=VMEM_LIMIT),
        name="combine",
    )(d3, d3, yb, tg, x1, mod3, final_w)


def _rope_tables():
    rows = DEC_SEQ // GRID_W
    t = jnp.arange(rows * GRID_W)
    row = (t // GRID_W).astype(F32)
    col = (t % GRID_W).astype(F32)
    half = QK_ROPE // 2
    inv = ROPE_BASE ** (-jnp.arange(0, half, 2, dtype=F32) / half)
    ang_r, ang_c = row[:, None] * inv, col[:, None] * inv
    cos32 = jnp.concatenate([jnp.cos(ang_r), jnp.cos(ang_r), jnp.cos(ang_c), jnp.cos(ang_c)], axis=-1)
    sin32 = jnp.concatenate([-jnp.sin(ang_r), jnp.sin(ang_r), -jnp.sin(ang_c), jnp.sin(ang_c)], axis=-1)
    cos32 = jnp.concatenate([jnp.ones((TM, QK_ROPE), F32), cos32], axis=0)
    sin32 = jnp.concatenate([jnp.zeros((TM, QK_ROPE), F32), sin32], axis=0)
    n = cos32.shape[0]
    pad = HEAD_PAD - QK_NOPE - QK_ROPE
    cos_h = jnp.concatenate([jnp.ones((n, QK_NOPE), F32), cos32, jnp.zeros((n, pad), F32)], axis=-1)
    sin_h = jnp.concatenate([jnp.zeros((n, QK_NOPE), F32), sin32, jnp.zeros((n, pad), F32)], axis=-1)
    zeros = jnp.zeros((n, LANES - QK_ROPE), F32)
    j = jnp.arange(QK_ROPE)
    tile = jnp.zeros((LANES, MLA_HEADS, HEAD_PAD), F32).at[j, :, QK_NOPE + j].set(1.0)
    return {
        "cosq": jnp.tile(cos_h, (1, MLA_HEADS)), "sinq": jnp.tile(sin_h, (1, MLA_HEADS)),
        "cosk": jnp.concatenate([cos32, zeros], axis=-1), "sink": jnp.concatenate([sin32, zeros], axis=-1),
        "tile": tile.reshape(LANES, MLA_HEADS * HEAD_PAD).astype(BF16),
    }


def _rot_partner():
    i = jnp.arange(QK_ROPE)
    quarter = QK_ROPE // 4
    return (i // (2 * quarter)) * (2 * quarter) + (i % (2 * quarter) + quarter) % (2 * quarter)


def _layer_weights(p, l):
    w_in = p["w_in"][l]
    sizes = (CONV_DIM, CONV_DIM, CONV_DIM, Q_RANK, KV_RANK, QK_ROPE, SSM_INNER, SSM_CONV_CH, SSM_HEADS, SSM_HEADS,
             D_MODEL, D_MODEL, D_MODEL)
    offs = [0]
    for s in sizes:
        offs.append(offs[-1] + s)
    seg = lambda k: w_in[:, offs[k]:offs[k + 1]]
    kpe = seg(5)
    small = jnp.concatenate([kpe, seg(8), seg(9), kpe[:, _rot_partner()],
                             jnp.zeros((D_MODEL, LANES - SM_KPE_ROT - QK_ROPE), F32)], axis=1)
    w_in2 = jnp.concatenate([seg(0), seg(1), seg(2), seg(3), seg(4), seg(6), seg(7), seg(10), seg(11), seg(12),
                             small], axis=1).astype(BF16)
    hd = QK_NOPE + QK_ROPE
    pad = HEAD_PAD - hd
    wq = p["w_uq"][l].reshape(Q_RANK, MLA_HEADS, hd)
    wq_a = jnp.pad(wq, ((0, 0), (0, 0), (0, pad)))
    wq_rot = wq[:, :, QK_NOPE:][:, :, _rot_partner()]
    wq_b = jnp.pad(wq_rot, ((0, 0), (0, 0), (QK_NOPE, pad)))
    wuk = jnp.pad(p["w_uk"][l], ((0, 0), (0, 0), (0, HEAD_PAD - QK_NOPE)))
    rw = jnp.pad(p["router_w"][l], ((0, 0), (0, LANES - N_EXPERTS)))
    rb = jnp.concatenate([p["router_b"][l], jnp.full((LANES - N_EXPERTS,), -jnp.inf, F32)])
    return {
        "norm1_w": p["norm1_w"][l][None], "w_in2": w_in2, "q_norm_w": p["q_norm_w"][l][None],
        "wq_a": wq_a.reshape(Q_RANK, -1).astype(BF16), "wq_b": wq_b.reshape(Q_RANK, -1).astype(BF16),
        "kv_norm_w": p["kv_norm_w"][l][None],
        "wuk": wuk.reshape(KV_RANK, -1).astype(BF16), "wuv": p["w_uv"][l].reshape(KV_RANK, -1).astype(BF16),
        "conv_a_w": p["conv_a_w"][l], "ssm_conv_w": p["ssm_conv_w"][l], "ssm_conv_b": p["ssm_conv_b"][l][None],
        "w_out_a": p["w_out_a"][l].astype(BF16), "w_o_mla": p["w_o_mla"][l].astype(BF16),
        "d_skip": jnp.repeat(p["d_skip"][l], SSM_HEAD_DIM)[None], "ssm_norm_w": p["ssm_norm_w"][l][None],
        "w_o_ssm": p["w_o_ssm"][l].astype(BF16), "w_o": p["w_o"][l].astype(BF16),
        "norm2_w": p["norm2_w"][l][None], "router_w": rw.astype(BF16), "router_b": rb[None],
        "bg": p["b_gu"][l][:, None, 0::2], "bu": p["b_gu"][l][:, None, 1::2], "bd": p["b_down"][l][:, None, :],
    }


def _ssd_params(p, l):
    z = lambda n: jnp.zeros((n,), F32)
    lanes = lambda f, b: jnp.concatenate([z(SM_DTF), f, b, z(LANES - SM_DTB - SSM_HEADS)])
    ones = jnp.ones((SSM_HEADS,), F32)
    rows = [lanes(p["dt_bias_fwd"][l], p["dt_bias_bwd"][l]), lanes(p["a_log_fwd"][l], p["a_log_bwd"][l]),
            lanes(ones, ones)]
    return jnp.concatenate([jnp.stack(rows), jnp.zeros((SUBLANES - 3, LANES), F32)], axis=0)


def _routing(top_i):
    e = top_i.reshape(-1)
    onehot = (e[:, None] == jnp.arange(N_EXPERTS, dtype=I32)[None, :]).astype(I32)
    cum = jnp.cumsum(onehot, axis=0)
    pos = jnp.take_along_axis(cum, e[:, None], axis=1)[:, 0] - 1
    counts = cum[-1]
    padded = (counts + MOE_BLK - 1) // MOE_BLK * MOE_BLK
    pad_end = jnp.cumsum(padded)
    dest = (pad_end - padded)[e] + pos
    slot_tok = jnp.zeros((N_SLOTS,), I32).at[dest].set(jnp.arange(N_ASSIGN, dtype=I32) // TOP_K)
    blk_e = jnp.minimum(jnp.searchsorted(pad_end, jnp.arange(N_BLOCKS, dtype=I32) * MOE_BLK, side="right"),
                        N_EXPERTS - 1).astype(I32)
    n_used = (pad_end[-1] // MOE_BLK).astype(I32).reshape(1)
    return dest.astype(I32), slot_tok, blk_e, n_used


def _deinterleave_matrix():
    k = jnp.arange(2 * LANES)[:, None]
    n = jnp.arange(2 * LANES)[None, :]
    src = jnp.where(n < LANES, 2 * n, 2 * (n - LANES) + 1)
    return (k == src).astype(BF16)


def kernel(x_prompt, x_sample, cache_ckv, cache_kpe, state_ssm_fwd, state_ssm_bwd, c, c_ctx, w_ada, b_ada, norm1_w, w_in, conv_a_w, w_out_a, q_norm_w, w_uq, kv_norm_w, w_uk, w_uv, w_o_mla, ssm_conv_w, ssm_conv_b, dt_bias_fwd, dt_bias_bwd, a_log_fwd, a_log_bwd, d_skip, ssm_norm_w, w_o_ssm, w_o, norm2_w, router_w, router_b, w_gu, b_gu, w_down, b_down, final_norm_w):
    p = dict(norm1_w=norm1_w, w_in=w_in, conv_a_w=conv_a_w, w_out_a=w_out_a, q_norm_w=q_norm_w, w_uq=w_uq,
             kv_norm_w=kv_norm_w, w_uk=w_uk, w_uv=w_uv, w_o_mla=w_o_mla, ssm_conv_w=ssm_conv_w,
             ssm_conv_b=ssm_conv_b, dt_bias_fwd=dt_bias_fwd, dt_bias_bwd=dt_bias_bwd, a_log_fwd=a_log_fwd,
             a_log_bwd=a_log_bwd, d_skip=d_skip, ssm_norm_w=ssm_norm_w, w_o_ssm=w_o_ssm, w_o=w_o,
             norm2_w=norm2_w, router_w=router_w, router_b=router_b, b_gu=b_gu, b_down=b_down)
    x = jnp.concatenate([x_prompt.reshape(N_CTX, D_MODEL), x_sample.reshape(N_LAT, D_MODEL)], axis=0)
    cond8 = jnp.concatenate([c_ctx[None], c, jnp.zeros((SUBLANES - 1 - DEC_BATCH, D_MODEL), F32)], axis=0)
    mods = _ada_mods(cond8, w_ada, b_ada)
    tabs = _rope_tables()
    sel = _deinterleave_matrix()
    final_w = final_norm_w[None]
    hp = SSM_INNER
    zeros_state = jnp.zeros((BATCH, hp, SSM_STATE), F32)

    ckv_out, kpe_out, sf_out, sb_out = [], [], [], []
    y_norm = None
    for l in range(DEPTH):
        lw = _layer_weights(p, l)
        mod3 = mods[l].reshape(SUBLANES, 1, 6 * D_MODEL)
        a2, q, ckv, kf, v, small, z, xbc, g = _inproj(x, mod3, lw, tabs)
        ta, xact = _convs(a2, xbc, lw)
        kpe_c = jnp.pad(cache_kpe[:, l].reshape(DEC_BATCH * PAST_LEN, QK_ROPE), ((0, 0), (0, LANES - QK_ROPE)))
        kf_c, v_c = _kvcache(cache_ckv[:, l].reshape(DEC_BATCH * PAST_LEN, KV_RANK), kpe_c, lw, tabs)
        att = _attention(q, kf, v, kf_c, v_c)
        init_f = jnp.concatenate([zeros_state, state_ssm_fwd[:, l].reshape(DEC_BATCH, hp, SSM_STATE)], axis=0)
        init_b = jnp.concatenate([zeros_state, state_ssm_bwd[:, l].reshape(DEC_BATCH, hp, SSM_STATE)], axis=0)
        yf, yb, sf, sb = _ssd(xact, small, init_f, init_b, _ssd_params(p, l))
        x1, h2, top_i, top_g = _merge(ta, att, yf, yb, xact, z, g, x, mod3, lw)
        dest, slot_tok, blk_e, n_used = _routing(top_i[:, 0:TOP_K])
        y_slots = _moe(l, blk_e, n_used, slot_tok, h2, w_gu, w_down, lw["bg"], lw["bu"], lw["bd"], sel)
        x, y_norm = _combine(dest, y_slots, top_g, x1, mod3, final_w)
        ckv_out.append(ckv[:N_CTX].reshape(BATCH, SEQ, KV_RANK))
        kpe_out.append(small[:N_CTX, 0:QK_ROPE].reshape(BATCH, SEQ, QK_ROPE))
        sf_out.append(sf[:BATCH].reshape(BATCH, SSM_HEADS, SSM_HEAD_DIM, SSM_STATE))
        sb_out.append(sb[:BATCH].reshape(BATCH, SSM_HEADS, SSM_HEAD_DIM, SSM_STATE))

    y_prompt = y_norm[:N_CTX].reshape(BATCH, SEQ, D_MODEL)
    y_sample = y_norm[N_CTX:].reshape(DEC_BATCH, DEC_SEQ, D_MODEL)
    return (y_prompt, y_sample, jnp.stack(ckv_out, axis=1), jnp.stack(kpe_out, axis=1),
            jnp.stack(sf_out, axis=1), jnp.stack(sb_out, axis=1))
```

```python
import functools
import math

import jax
import jax.numpy as jnp
from jax import lax
from jax.experimental import pallas as pl
from jax.experimental.pallas import tpu as pltpu

F32 = jnp.float32
BF16 = jnp.bfloat16
I32 = jnp.int32

D_MODEL = 1024
BATCH = 16
SEQ = 256
DEPTH = 2
DEC_BATCH = 2
DEC_SEQ = 2048
PAST_LEN = 512
GRID_W = 64
NORM_EPS = 1e-6
CONV_DIM = 512
MLA_HEADS = 8
Q_RANK = 384
KV_RANK = 256
QK_NOPE = 64
QK_ROPE = 32
V_HEAD = 64
ROPE_BASE = 10000.0
MLA_SCALE = (QK_NOPE + QK_ROPE) ** -0.5
SSM_HEADS = 16
SSM_HEAD_DIM = 64
SSM_INNER = SSM_HEADS * SSM_HEAD_DIM
SSM_GROUPS = 2
SSM_STATE = 128
SSM_CONV_CH = SSM_INNER + 2 * SSM_GROUPS * SSM_STATE
N_EXPERTS = 32
TOP_K = 4
EXPERT_FF = D_MODEL
SWIGLU_ALPHA = 1.702
SWIGLU_LIMIT = 7.0

N_CTX = BATCH * SEQ
N_LAT = DEC_BATCH * DEC_SEQ
N_TOK = N_CTX + N_LAT
N_SEQS = BATCH + DEC_BATCH

LANES = 128
SUBLANES = 8
HEAD_PAD = 128
TM = 256
N_TILES = N_TOK // TM
CTX_TILES = N_CTX // TM
LAT_TILES_PER_SEQ = DEC_SEQ // TM
CHUNK = 128
CTX_CHUNKS_PER_SEQ = SEQ // CHUNK
LAT_CHUNKS_PER_SEQ = DEC_SEQ // CHUNK
N_CTX_CHUNKS = N_CTX // CHUNK
N_CHUNKS = N_TOK // CHUNK
MOE_BLK = 256
N_ASSIGN = N_TOK * TOP_K
RUN_ALIGN = SUBLANES
RUN_BITS = tuple(range(8, 2, -1))
TAIL_BITS = tuple(range(7, 2, -1))
STAGE_ROWS = 1280
XS_COLS = D_MODEL + LANES
N_BLOCKS = -(-(N_ASSIGN + N_TILES * N_EXPERTS * (RUN_ALIGN - 1) + N_EXPERTS * (MOE_BLK - 1)) // MOE_BLK)
N_SLOTS = N_BLOCKS * MOE_BLK
VMEM_LIMIT = 56 * 1024 * 1024

C_A3 = 0
C_CQ = C_A3 + 3 * CONV_DIM
C_CKV = C_CQ + Q_RANK
C_Z = C_CKV + KV_RANK
C_XBC = C_Z + SSM_INNER
C_GATE = C_XBC + SSM_CONV_CH
C_SMALL = C_GATE + 3 * D_MODEL
IN_COLS2 = C_SMALL + LANES
SM_DTF = QK_ROPE
SM_DTB = QK_ROPE + SSM_HEADS
SM_KPE_ROT = 64


def _rms(x, w):
    return x * lax.rsqrt(jnp.mean(x * x, axis=-1, keepdims=True) + NORM_EPS) * w


def _silu(x):
    return x * jax.nn.sigmoid(x)


def _dot(a, b):
    return jnp.dot(a, b, preferred_element_type=F32)


def _dot_nt(a, b):
    return lax.dot_general(a, b, (((1,), (1,)), ((), ())), preferred_element_type=F32)


def _resident(shape):
    nd = len(shape)
    return pl.BlockSpec(shape, lambda *_: (0,) * nd, pipeline_mode=pl.Buffered(1))


def _mod_row(i):
    return jnp.where(i < CTX_TILES, 0, 1 + (i - CTX_TILES) // LAT_TILES_PER_SEQ)


def _pos_block(i):
    return jnp.where(i < CTX_TILES, 0, 1 + (i - CTX_TILES) % LAT_TILES_PER_SEQ)


def _ada_kernel(c_ref, w_ref, b_ref, o_ref):
    s = _silu(c_ref[...]).astype(BF16)
    o_ref[0] = _dot(s, w_ref[0].astype(BF16)) + b_ref[0]


def _ada_mods(cond8, w_ada, b_ada):
    tn = 1536
    n_mod = 6 * D_MODEL
    return pl.pallas_call(
        _ada_kernel,
        grid=(DEPTH, n_mod // tn),
        in_specs=[
            pl.BlockSpec((SUBLANES, D_MODEL), lambda l, j: (0, 0)),
            pl.BlockSpec((1, D_MODEL, tn), lambda l, j: (l, 0, j)),
            pl.BlockSpec((1, 1, tn), lambda l, j: (l, 0, j)),
        ],
        out_specs=pl.BlockSpec((1, SUBLANES, tn), lambda l, j: (l, 0, j)),
        out_shape=jax.ShapeDtypeStruct((DEPTH, SUBLANES, n_mod), F32),
        compiler_params=pltpu.CompilerParams(dimension_semantics=("arbitrary", "arbitrary")),
        name="ada_mods",
    )(cond8, w_ada, b_ada.reshape(DEPTH, 1, n_mod))


def _inproj_kernel(x_ref, mod_ref, n1w_ref, w_ref, qnw_ref, wqa_ref, wqb_ref, kvnw_ref, wuk_ref, wuv_ref,
                   tile_ref, cosq_ref, sinq_ref, cosk_ref, sink_ref,
                   a2_ref, q_ref, ckv_ref, kf_ref, v_ref, small_ref, z_ref, xbc_ref, g_ref):
    mod = mod_ref[0]
    shift1 = mod[:, 0:D_MODEL]
    scale1 = mod[:, D_MODEL:2 * D_MODEL]
    hb = (_rms(x_ref[...], n1w_ref[...]) * (1.0 + scale1) + shift1).astype(BF16)

    def seg(a, b):
        return _dot(hb, w_ref[:, a:b])

    a3 = seg(C_A3, C_CQ)
    a2_ref[:, 0:CONV_DIM] = a3[:, 2 * CONV_DIM:3 * CONV_DIM] * a3[:, 0:CONV_DIM]
    a2_ref[:, CONV_DIM:2 * CONV_DIM] = a3[:, CONV_DIM:2 * CONV_DIM]

    cqn = _rms(seg(C_CQ, C_CKV), qnw_ref[...]).astype(BF16)
    q = _dot(cqn, wqa_ref[...]) * cosq_ref[...] + _dot(cqn, wqb_ref[...]) * sinq_ref[...]
    q_ref[...] = q.astype(BF16)

    ckv = _rms(seg(C_CKV, C_Z), kvnw_ref[...])
    ckv_ref[...] = ckv
    ckvb = ckv.astype(BF16)
    small = seg(C_SMALL, IN_COLS2)
    small_ref[...] = small
    kpe = small * cosk_ref[...] + pltpu.roll(small, LANES - SM_KPE_ROT, 1) * sink_ref[...]
    kf_ref[...] = (_dot(ckvb, wuk_ref[...]) + _dot(kpe.astype(BF16), tile_ref[...])).astype(BF16)
    v_ref[...] = _dot(ckvb, wuv_ref[...]).astype(BF16)

    z_ref[...] = seg(C_Z, C_XBC)
    xbc_ref[...] = seg(C_XBC, C_GATE)
    g_ref[...] = jax.nn.sigmoid(seg(C_GATE, C_SMALL))


def _inproj(x, mod3, lw, tabs):
    row = lambda n: pl.BlockSpec((TM, n), lambda i: (i, 0))
    tab = lambda n: pl.BlockSpec((TM, n), lambda i: (_pos_block(i), 0))
    qw = MLA_HEADS * HEAD_PAD
    vw = MLA_HEADS * V_HEAD
    out_shape = (
        jax.ShapeDtypeStruct((N_TOK, 2 * CONV_DIM), F32),
        jax.ShapeDtypeStruct((N_TOK, qw), BF16),
        jax.ShapeDtypeStruct((N_TOK, KV_RANK), F32),
        jax.ShapeDtypeStruct((N_TOK, qw), BF16),
        jax.ShapeDtypeStruct((N_TOK, vw), BF16),
        jax.ShapeDtypeStruct((N_TOK, LANES), F32),
        jax.ShapeDtypeStruct((N_TOK, SSM_INNER), F32),
        jax.ShapeDtypeStruct((N_TOK, SSM_CONV_CH), F32),
        jax.ShapeDtypeStruct((N_TOK, 3 * D_MODEL), F32),
    )
    return pl.pallas_call(
        _inproj_kernel,
        grid=(N_TILES,),
        in_specs=[
            row(D_MODEL),
            pl.BlockSpec((1, 1, 6 * D_MODEL), lambda i: (_mod_row(i), 0, 0)),
            _resident((1, D_MODEL)),
            _resident((D_MODEL, IN_COLS2)),
            _resident((1, Q_RANK)),
            _resident((Q_RANK, qw)),
            _resident((Q_RANK, qw)),
            _resident((1, KV_RANK)),
            _resident((KV_RANK, qw)),
            _resident((KV_RANK, vw)),
            _resident((LANES, qw)),
            tab(qw), tab(qw), tab(LANES), tab(LANES),
        ],
        out_specs=tuple(row(s.shape[1]) for s in out_shape),
        out_shape=out_shape,
        compiler_params=pltpu.CompilerParams(dimension_semantics=("arbitrary",), vmem_limit_bytes=VMEM_LIMIT),
        name="inproj",
    )(x, mod3, lw["norm1_w"], lw["w_in2"], lw["q_norm_w"], lw["wq_a"], lw["wq_b"], lw["kv_norm_w"],
      lw["wuk"], lw["wuv"], tabs["tile"], tabs["cosq"], tabs["sinq"], tabs["cosk"], tabs["sink"])


def _kvcache_kernel(ckv_ref, kpe_ref, wuk_ref, wuv_ref, tile_ref, kf_ref, v_ref):
    ckvb = ckv_ref[...].astype(BF16)
    kf_ref[...] = (_dot(ckvb, wuk_ref[...]) + _dot(kpe_ref[...].astype(BF16), tile_ref[...])).astype(BF16)
    v_ref[...] = _dot(ckvb, wuv_ref[...]).astype(BF16)


def _kvcache(ckv, kpe128, lw, tabs):
    n = ckv.shape[0]
    qw = MLA_HEADS * HEAD_PAD
    vw = MLA_HEADS * V_HEAD
    return pl.pallas_call(
        _kvcache_kernel,
        grid=(n // PAST_LEN,),
        in_specs=[
            pl.BlockSpec((PAST_LEN, KV_RANK), lambda i: (i, 0)),
            pl.BlockSpec((PAST_LEN, LANES), lambda i: (i, 0)),
            _resident((KV_RANK, qw)),
            _resident((KV_RANK, vw)),
            _resident((LANES, qw)),
        ],
        out_specs=(pl.BlockSpec((PAST_LEN, qw), lambda i: (i, 0)), pl.BlockSpec((PAST_LEN, vw), lambda i: (i, 0))),
        out_shape=(jax.ShapeDtypeStruct((n, qw), BF16), jax.ShapeDtypeStruct((n, vw), BF16)),
        compiler_params=pltpu.CompilerParams(dimension_semantics=("arbitrary",)),
        name="kvcache",
    )(ckv, kpe128, lw["wuk"], lw["wuv"], tabs["tile"])


def _conv_kernel(a2_ref, sp_ref, sn_ref, xbc_ref, xp_ref, xn_ref, caw_ref, cw_ref, cb_ref, ta_ref, xact_ref):
    i = pl.program_id(0)
    j = (i - CTX_TILES) % LAT_TILES_PER_SEQ
    is_ctx = i < CTX_TILES
    keep_prev = jnp.where(jnp.logical_or(is_ctx, j == 0), 0.0, 1.0)
    keep_next = jnp.where(jnp.logical_or(is_ctx, j == LAT_TILES_PER_SEQ - 1), 0.0, 1.0)
    row = lax.broadcasted_iota(I32, (TM, 1), 0)

    def conv3(x, prev_row, next_row, w_ref):
        x_prev = jnp.where(row == 0, prev_row * keep_prev, pltpu.roll(x, 1, 0))
        x_next = jnp.where(row == TM - 1, next_row * keep_next, pltpu.roll(x, TM - 1, 0))
        return x_prev * w_ref[0:1, :] + x * w_ref[1:2, :] + x_next * w_ref[2:3, :]

    s = a2_ref[:, 0:CONV_DIM]
    gb = a2_ref[:, CONV_DIM:2 * CONV_DIM]
    ta_ref[...] = (gb * conv3(s, sp_ref[SUBLANES - 1:SUBLANES, :], sn_ref[0:1, :], caw_ref)).astype(BF16)
    xc = conv3(xbc_ref[...], xp_ref[SUBLANES - 1:SUBLANES, :], xn_ref[0:1, :], cw_ref) + cb_ref[...]
    xact_ref[...] = _silu(xc)


def _convs(a2, xbc, lw):
    per = TM // SUBLANES
    last = N_TOK // SUBLANES - 1
    prev = lambda n: pl.BlockSpec((SUBLANES, n), lambda i: (jnp.maximum(i * per - 1, 0), 0))
    nxt = lambda n: pl.BlockSpec((SUBLANES, n), lambda i: (jnp.minimum((i + 1) * per, last), 0))
    return pl.pallas_call(
        _conv_kernel,
        grid=(N_TILES,),
        in_specs=[
            pl.BlockSpec((TM, 2 * CONV_DIM), lambda i: (i, 0)), prev(CONV_DIM), nxt(CONV_DIM),
            pl.BlockSpec((TM, SSM_CONV_CH), lambda i: (i, 0)), prev(SSM_CONV_CH), nxt(SSM_CONV_CH),
            pl.BlockSpec((3, CONV_DIM), lambda i: (0, 0)),
            pl.BlockSpec((3, SSM_CONV_CH), lambda i: (0, 0)),
            pl.BlockSpec((1, SSM_CONV_CH), lambda i: (0, 0)),
        ],
        out_specs=(pl.BlockSpec((TM, CONV_DIM), lambda i: (i, 0)), pl.BlockSpec((TM, SSM_CONV_CH), lambda i: (i, 0))),
        out_shape=(jax.ShapeDtypeStruct((N_TOK, CONV_DIM), BF16), jax.ShapeDtypeStruct((N_TOK, SSM_CONV_CH), F32)),
        compiler_params=pltpu.CompilerParams(dimension_semantics=("arbitrary",)),
        name="convs",
    )(a2, a2, a2, xbc, xbc, xbc, lw["conv_a_w"], lw["ssm_conv_w"], lw["ssm_conv_b"])


def _attn_heads(q_ref, kv_refs, o_ref, acc_ref):
    for h in range(MLA_HEADS):
        qh = q_ref[:, h * HEAD_PAD:(h + 1) * HEAD_PAD]
        ss = [_dot_nt(qh, k_ref[:, h * HEAD_PAD:(h + 1) * HEAD_PAD]) * MLA_SCALE for k_ref, _ in kv_refs]
        m = functools.reduce(jnp.maximum, [jnp.max(s, axis=-1, keepdims=True) for s in ss])
        ps = [jnp.exp(s - m) for s in ss]
        l = functools.reduce(jnp.add, [jnp.sum(p, axis=-1, keepdims=True) for p in ps])
        o = functools.reduce(jnp.add, [_dot(p.astype(BF16), v_ref[:, h * V_HEAD:(h + 1) * V_HEAD])
                                       for p, (_, v_ref) in zip(ps, kv_refs)])
        acc_ref[:, h * V_HEAD:(h + 1) * V_HEAD] = o / l
    o_ref[...] = acc_ref[...].astype(BF16)


def _attn_ctx_kernel(q_ref, k_ref, v_ref, o_ref, acc_ref):
    _attn_heads(q_ref, [(k_ref, v_ref)], o_ref, acc_ref)


def _attn_lat_kernel(q_ref, k_ref, v_ref, kc_ref, vc_ref, o_ref, acc_ref):
    _attn_heads(q_ref, [(k_ref, v_ref), (kc_ref, vc_ref)], o_ref, acc_ref)


def _attention(q, kf, v, kf_c, v_c):
    qw = MLA_HEADS * HEAD_PAD
    vw = MLA_HEADS * V_HEAD
    att_ctx = pl.pallas_call(
        _attn_ctx_kernel,
        grid=(BATCH,),
        in_specs=[pl.BlockSpec((SEQ, qw), lambda b: (b, 0)), pl.BlockSpec((SEQ, qw), lambda b: (b, 0)),
                  pl.BlockSpec((SEQ, vw), lambda b: (b, 0))],
        out_specs=pl.BlockSpec((SEQ, vw), lambda b: (b, 0)),
        out_shape=jax.ShapeDtypeStruct((N_CTX, vw), BF16),
        scratch_shapes=[pltpu.VMEM((SEQ, vw), F32)],
        compiler_params=pltpu.CompilerParams(dimension_semantics=("arbitrary",)),
        name="attn_ctx",
    )(q, kf, v)
    lat0 = N_CTX // DEC_SEQ
    att_lat = pl.pallas_call(
        _attn_lat_kernel,
        grid=(DEC_BATCH, LAT_TILES_PER_SEQ),
        in_specs=[
            pl.BlockSpec((TM, qw), lambda b, t: (CTX_TILES + b * LAT_TILES_PER_SEQ + t, 0)),
            pl.BlockSpec((DEC_SEQ, qw), lambda b, t: (lat0 + b, 0)),
            pl.BlockSpec((DEC_SEQ, vw), lambda b, t: (lat0 + b, 0)),
            pl.BlockSpec((PAST_LEN, qw), lambda b, t: (b, 0)),
            pl.BlockSpec((PAST_LEN, vw), lambda b, t: (b, 0)),
        ],
        out_specs=pl.BlockSpec((TM, vw), lambda b, t: (b * LAT_TILES_PER_SEQ + t, 0)),
        out_shape=jax.ShapeDtypeStruct((N_LAT, vw), BF16),
        scratch_shapes=[pltpu.VMEM((TM, vw), F32)],
        compiler_params=pltpu.CompilerParams(dimension_semantics=("arbitrary", "arbitrary"),
                                             vmem_limit_bytes=VMEM_LIMIT),
        name="attn_lat",
    )(q, kf, v, kf_c, v_c)
    return jnp.concatenate([att_ctx, att_lat], axis=0)


def _seq_of_chunk(s):
    return jnp.where(s < N_CTX_CHUNKS, s // CTX_CHUNKS_PER_SEQ,
                     BATCH + (s - N_CTX_CHUNKS) // LAT_CHUNKS_PER_SEQ)


def _chunk_in_seq(s):
    return jnp.where(s < N_CTX_CHUNKS, s % CTX_CHUNKS_PER_SEQ, (s - N_CTX_CHUNKS) % LAT_CHUNKS_PER_SEQ)


def _chunks_in_seq(s):
    return jnp.where(s < N_CTX_CHUNKS, CTX_CHUNKS_PER_SEQ, LAT_CHUNKS_PER_SEQ)


def _mirror_chunk(s):
    return s + _chunks_in_seq(s) - 1 - 2 * _chunk_in_seq(s)


def _split3(a):
    a1 = a.astype(BF16)
    r1 = a - a1.astype(F32)
    a2 = r1.astype(BF16)
    a3 = (r1 - a2.astype(F32)).astype(BF16)
    return a1, a2, a3


def _ssd_direction(x_ref, sm_ref, par_ref, st_ref, xdd_ref, y_ref, lane0, backward):
    ri = lax.broadcasted_iota(I32, (CHUNK, CHUNK), 0)
    ci = lax.broadcasted_iota(I32, (CHUNK, CHUNK), 1)
    tri = (ci >= ri) if backward else (ci <= ri)
    tri_b = jnp.where(tri, 1.0, 0.0).astype(BF16)
    tot_row = 0 if backward else CHUNK - 1

    dt = jax.nn.softplus(sm_ref[...] + par_ref[0:1, :])
    a = dt * (-jnp.exp(par_ref[1:2, :])) * par_ref[2:3, :]
    a1, a2, a3 = _split3(a)
    acs = _dot(tri_b, a1) + _dot(tri_b, a2) + _dot(tri_b, a3)
    acs_t = acs.T

    for g in range(SSM_GROUPS):
        b0 = SSM_INNER + g * SSM_STATE
        c0 = SSM_INNER + SSM_GROUPS * SSM_STATE + g * SSM_STATE
        bg = x_ref[:, b0:b0 + SSM_STATE]
        cgb = x_ref[:, c0:c0 + SSM_STATE].astype(BF16)
        cb = _dot_nt(cgb, bg.astype(BF16))
        bgt = bg.T.astype(BF16)
        heads = SSM_HEADS // SSM_GROUPS
        gw = heads * SSM_HEAD_DIM
        y_off = _dot(cgb, st_ref[:, g * gw:(g + 1) * gw].astype(BF16))
        for hh in range(heads):
            h = g * heads + hh
            lane = lane0 + h
            col = acs[:, lane:lane + 1]
            diff = col - acs_t[lane:lane + 1, :]
            scores = (cb * jnp.exp(jnp.where(tri, diff, -jnp.inf))).astype(BF16)
            xd = x_ref[:, h * SSM_HEAD_DIM:(h + 1) * SSM_HEAD_DIM] * dt[:, lane:lane + 1]
            y_diag = _dot(scores, xd.astype(BF16))
            y_ref[:, h * SSM_HEAD_DIM:(h + 1) * SSM_HEAD_DIM] = (
                y_diag + y_off[:, hh * SSM_HEAD_DIM:(hh + 1) * SSM_HEAD_DIM] * jnp.exp(col))
            tot = acs[tot_row:tot_row + 1, lane:lane + 1]
            xdd_ref[:, h * SSM_HEAD_DIM:(h + 1) * SSM_HEAD_DIM] = xd * jnp.exp(tot - col)
        upd = _dot(bgt, xdd_ref[:, g * gw:(g + 1) * gw].astype(BF16))
        for hh in range(heads):
            h = g * heads + hh
            tot = acs[tot_row:tot_row + 1, lane0 + h:lane0 + h + 1]
            sl = slice(h * SSM_HEAD_DIM, (h + 1) * SSM_HEAD_DIM)
            st_ref[:, sl] = st_ref[:, sl] * jnp.exp(tot) + upd[:, hh * SSM_HEAD_DIM:(hh + 1) * SSM_HEAD_DIM]


def _ssd_kernel(xf_ref, xb_ref, smf_ref, smb_ref, if_ref, ib_ref, par_ref,
                yf_ref, yb_ref, sf_ref, sb_ref, stf_ref, stb_ref, xdd_ref):
    s = pl.program_id(0)
    c = _chunk_in_seq(s)

    @pl.when(c == 0)
    def _():
        stf_ref[...] = if_ref[0].T
        stb_ref[...] = ib_ref[0].T

    _ssd_direction(xf_ref, smf_ref, par_ref, stf_ref, xdd_ref, yf_ref, SM_DTF, False)
    _ssd_direction(xb_ref, smb_ref, par_ref, stb_ref, xdd_ref, yb_ref, SM_DTB, True)

    @pl.when(c == _chunks_in_seq(s) - 1)
    def _():
        sf_ref[0] = stf_ref[...].T
        sb_ref[0] = stb_ref[...].T


def _ssd(xact, small, init_f, init_b, par):
    hp = SSM_INNER
    fwd = lambda n: pl.BlockSpec((CHUNK, n), lambda s: (s, 0))
    bwd = lambda n: pl.BlockSpec((CHUNK, n), lambda s: (_mirror_chunk(s), 0))
    st = pl.BlockSpec((1, hp, SSM_STATE), lambda s: (_seq_of_chunk(s), 0, 0))
    return pl.pallas_call(
        _ssd_kernel,
        grid=(N_CHUNKS,),
        in_specs=[fwd(SSM_CONV_CH), bwd(SSM_CONV_CH), fwd(LANES), bwd(LANES), st, st,
                  pl.BlockSpec((SUBLANES, LANES), lambda s: (0, 0))],
        out_specs=(fwd(hp), bwd(hp), st, st),
        out_shape=(jax.ShapeDtypeStruct((N_TOK, hp), F32), jax.ShapeDtypeStruct((N_TOK, hp), F32),
                   jax.ShapeDtypeStruct((N_SEQS, hp, SSM_STATE), F32),
                   jax.ShapeDtypeStruct((N_SEQS, hp, SSM_STATE), F32)),
        scratch_shapes=[pltpu.VMEM((SSM_STATE, hp), F32), pltpu.VMEM((SSM_STATE, hp), F32),
                        pltpu.VMEM((CHUNK, hp), F32)],
        compiler_params=pltpu.CompilerParams(dimension_semantics=("arbitrary",)),
        name="ssd",
    )(xact, xact, small, small, init_f, init_b, par)


def _merge_kernel(ta_ref, att_ref, yf_ref, yb_ref, xs_ref, z_ref, g_ref, x_ref, mod_ref,
                  woa_ref, wom_ref, dsk_ref, snw_ref, wos_ref, wo_ref, n2w_ref, rw_ref, rb_ref,
                  x1_ref, h2_ref, ti_ref, tg_ref, cnt_ref):
    mod = mod_ref[0]
    gate1 = mod[:, 2 * D_MODEL:3 * D_MODEL]
    shift2 = mod[:, 3 * D_MODEL:4 * D_MODEL]
    scale2 = mod[:, 4 * D_MODEL:5 * D_MODEL]
    y_a = _dot(ta_ref[...], woa_ref[...])
    y_b = _dot(att_ref[...], wom_ref[...])
    y_ssm = (yf_ref[...] + yb_ref[...] + dsk_ref[...] * xs_ref[...]) * _silu(z_ref[...])
    y_c = _dot(_rms(y_ssm, snw_ref[...]).astype(BF16), wos_ref[...])
    merged = (g_ref[:, 0:D_MODEL] * y_a + g_ref[:, D_MODEL:2 * D_MODEL] * y_b
              + g_ref[:, 2 * D_MODEL:3 * D_MODEL] * y_c)
    x1 = x_ref[...] + gate1 * _dot(merged.astype(BF16), wo_ref[...])
    x1_ref[...] = x1
    h2 = _rms(x1, n2w_ref[...]) * (1.0 + scale2) + shift2
    h2_ref[...] = h2

    logits = _dot(h2.astype(BF16), rw_ref[...]) + rb_ref[...]
    lane = lax.broadcasted_iota(I32, (TM, LANES), 1)
    ti = jnp.zeros((TM, LANES), I32)
    tv = jnp.full((TM, LANES), -jnp.inf, F32)
    chosen = jnp.zeros((TM, LANES), F32)
    for k in range(TOP_K):
        m = jnp.max(logits, axis=-1, keepdims=True)
        idx = jnp.min(jnp.where(logits == m, lane, LANES), axis=-1, keepdims=True)
        ti = jnp.where(lane == k, idx, ti)
        tv = jnp.where(lane == k, m, tv)
        chosen = jnp.where(lane == idx, 1.0, chosen)
        logits = jnp.where(lane == idx, -jnp.inf, logits)
    e = jnp.exp(tv - jnp.max(tv, axis=-1, keepdims=True))
    ti_ref[...] = ti
    tg_ref[...] = e / jnp.sum(e, axis=-1, keepdims=True)
    cnt_ref[0] = jnp.broadcast_to(jnp.sum(chosen, axis=0, keepdims=True), (SUBLANES, LANES)).astype(I32)


def _merge(ta, att, yf, yb, xact, z, g, x, mod3, lw):
    row = lambda n: pl.BlockSpec((TM, n), lambda i: (i, 0))
    out_shape = (jax.ShapeDtypeStruct((N_TOK, D_MODEL), F32), jax.ShapeDtypeStruct((N_TOK, D_MODEL), F32),
                 jax.ShapeDtypeStruct((N_TOK, LANES), I32), jax.ShapeDtypeStruct((N_TOK, LANES), F32))
    cnt_shape = jax.ShapeDtypeStruct((N_TILES, SUBLANES, LANES), I32)
    cnt_spec = pl.BlockSpec((1, SUBLANES, LANES), lambda i: (i, 0, 0))
    return pl.pallas_call(
        _merge_kernel,
        grid=(N_TILES,),
        in_specs=[
            row(CONV_DIM), row(MLA_HEADS * V_HEAD), row(SSM_INNER), row(SSM_INNER), row(SSM_INNER),
            row(SSM_INNER), row(3 * D_MODEL), row(D_MODEL),
            pl.BlockSpec((1, 1, 6 * D_MODEL), lambda i: (_mod_row(i), 0, 0)),
            _resident((CONV_DIM, D_MODEL)), _resident((MLA_HEADS * V_HEAD, D_MODEL)),
            _resident((1, SSM_INNER)), _resident((1, SSM_INNER)), _resident((SSM_INNER, D_MODEL)),
            _resident((D_MODEL, D_MODEL)), _resident((1, D_MODEL)),
            _resident((D_MODEL, LANES)), _resident((1, LANES)),
        ],
        out_specs=tuple(row(s.shape[1]) for s in out_shape) + (cnt_spec,),
        out_shape=out_shape + (cnt_shape,),
        compiler_params=pltpu.CompilerParams(dimension_semantics=("arbitrary",), vmem_limit_bytes=VMEM_LIMIT),
        name="merge",
    )(ta, att, yf, yb, xact, z, g, x, mod3, lw["w_out_a"], lw["w_o_mla"], lw["d_skip"], lw["ssm_norm_w"],
      lw["w_o_ssm"], lw["w_o"], lw["norm2_w"], lw["router_w"], lw["router_b"])


def _run_copies(cnt_ref, src_ref, dst_ref, first, count, bits, make_copy, start):
    def body(e, carry):
        n = cnt_ref[first + e]
        s0 = src_ref[first + e] if src_ref is not None else 0
        d0 = dst_ref[first + e]
        for b in bits:
            above = (n >> (b + 1)) << (b + 1)

            @pl.when(((n >> b) & 1) == 1)
            def _():
                cp = make_copy(pl.multiple_of(s0 + above, RUN_ALIGN), pl.multiple_of(d0 + above, RUN_ALIGN), 1 << b)
                if start:
                    cp.start()
                else:
                    cp.wait()
        return carry
    lax.fori_loop(0, count, body, 0)


def _dispatch_kernel(cnt_ref, off_ref, run_ref, tcnt_ref, tdst_ref, nu_ref, h2_ref, ti_ref, tg_ref, offv_ref,
                     xs_ref, lp_ref, stage, zeros, sem, semz):
    i = pl.program_id(0)
    slot = i % 2

    def copy_out(s_):
        return lambda s, d, n: pltpu.make_async_copy(stage.at[s_, pl.ds(s, n), :], xs_ref.at[pl.ds(d, n), :],
                                                     sem.at[s_])

    @pl.when(i == 0)
    def _():
        zeros[...] = jnp.zeros_like(zeros)
        zero_out = lambda s, d, n: pltpu.make_async_copy(zeros.at[pl.ds(0, n), :], xs_ref.at[pl.ds(d, n), :], semz)
        _run_copies(tcnt_ref, None, tdst_ref, 0, N_EXPERTS, TAIL_BITS, zero_out, True)
        _run_copies(tcnt_ref, None, tdst_ref, 0, N_EXPERTS, TAIL_BITS, zero_out, False)
        zrows = zeros.shape[0]

        def unused_blocks(start):
            def body(b, carry):
                for part in range(MOE_BLK // zrows):
                    cp = zero_out(0, pl.multiple_of(b * MOE_BLK + part * zrows, RUN_ALIGN), zrows)
                    if start:
                        cp.start()
                    else:
                        cp.wait()
                return carry
            lax.fori_loop(nu_ref[0], N_BLOCKS, body, 0)
        unused_blocks(True)
        unused_blocks(False)

    @pl.when(i >= 2)
    def _():
        _run_copies(cnt_ref, off_ref, run_ref, (i - 2) * N_EXPERTS, N_EXPERTS, RUN_BITS, copy_out(slot), False)

    lane = lax.broadcasted_iota(I32, (TM, LANES), 1)
    picks = [jnp.where(lane == ti_ref[:, k:k + 1], 1.0, 0.0) for k in range(TOP_K)]
    ri = lax.broadcasted_iota(I32, (TM, TM), 0)
    ci = lax.broadcasted_iota(I32, (TM, TM), 1)
    earlier = jnp.where(ci < ri, 1.0, 0.0).astype(BF16)
    base = _dot(earlier, functools.reduce(jnp.add, picks).astype(BF16)) + offv_ref[0][0:1, :]
    lp = jnp.full((TM, LANES), -1.0, F32)
    for k in range(TOP_K):
        lp = jnp.where(lane == k, jnp.sum(picks[k] * base, axis=-1, keepdims=True), lp)
    lp_ref[...] = lp.astype(I32)

    lp_t = lp.T
    tg_t = tg_ref[...].T
    row = lax.broadcasted_iota(I32, (STAGE_ROWS, TM), 0).astype(F32)
    perm = jnp.zeros((STAGE_ROWS, TM), F32)
    gates = jnp.zeros((STAGE_ROWS, TM), F32)
    for k in range(TOP_K):
        hit = row == lp_t[k:k + 1, :]
        perm = jnp.where(hit, 1.0, perm)
        gates = jnp.where(hit, tg_t[k:k + 1, :], gates)
    stage[slot, :, 0:D_MODEL] = _dot(perm.astype(BF16), h2_ref[...].astype(BF16))
    g_hi = gates.astype(BF16)
    g_lo = (gates - g_hi.astype(F32)).astype(BF16)
    ones = jnp.ones((TM, LANES), BF16)
    stage[slot, :, D_MODEL:XS_COLS] = _dot(g_hi, ones) + _dot(g_lo, ones)

    _run_copies(cnt_ref, off_ref, run_ref, i * N_EXPERTS, N_EXPERTS, RUN_BITS, copy_out(slot), True)

    @pl.when(i == N_TILES - 1)
    def _():
        _run_copies(cnt_ref, off_ref, run_ref, (i - 1) * N_EXPERTS, N_EXPERTS, RUN_BITS, copy_out(1 - slot), False)
        _run_copies(cnt_ref, off_ref, run_ref, i * N_EXPERTS, N_EXPERTS, RUN_BITS, copy_out(slot), False)


def _dispatch(rt, h2, top_i, top_g):
    row = lambda n: pl.BlockSpec((TM, n), lambda i, *_: (i, 0))
    return pl.pallas_call(
        _dispatch_kernel,
        grid_spec=pltpu.PrefetchScalarGridSpec(
            num_scalar_prefetch=6,
            grid=(N_TILES,),
            in_specs=[row(D_MODEL), row(LANES), row(LANES),
                      pl.BlockSpec((1, SUBLANES, LANES), lambda i, *_: (i, 0, 0))],
            out_specs=(pl.BlockSpec(memory_space=pl.ANY), row(LANES)),
            scratch_shapes=[
                pltpu.VMEM((2, STAGE_ROWS, XS_COLS), F32),
                pltpu.VMEM((1 << TAIL_BITS[0], XS_COLS), F32),
                pltpu.SemaphoreType.DMA((2,)),
                pltpu.SemaphoreType.DMA,
            ],
        ),
        out_shape=(jax.ShapeDtypeStruct((N_SLOTS, XS_COLS), F32), jax.ShapeDtypeStruct((N_TOK, LANES), I32)),
        compiler_params=pltpu.CompilerParams(dimension_semantics=("arbitrary",), vmem_limit_bytes=VMEM_LIMIT),
        name="dispatch",
    )(rt["cnt"], rt["off"], rt["run"], rt["tail_cnt"], rt["tail_dst"], rt["n_used"], h2, top_i, top_g, rt["off_v"])


def _moe_kernel(be_ref, nu_ref, x_ref, wgu_ref, wdn_ref, bg_ref, bu_ref, bd_ref, sel_ref, y_ref, wgu_s, wdn_s):
    i = pl.program_id(0)
    n_used = nu_ref[0]

    @pl.when(i < n_used)
    def _():
        @pl.when(jnp.logical_or(i == 0, be_ref[i] != be_ref[jnp.maximum(i - 1, 0)]))
        def _():
            half = LANES
            for c in range(2 * EXPERT_FF // (2 * half)):
                r = _dot(wgu_ref[:, c * 2 * half:(c + 1) * 2 * half].astype(BF16), sel_ref[...])
                wgu_s[:, c * half:(c + 1) * half] = r[:, 0:half].astype(BF16)
                wgu_s[:, EXPERT_FF + c * half:EXPERT_FF + (c + 1) * half] = r[:, half:2 * half].astype(BF16)
            wdn_s[...] = wdn_ref[...].astype(BF16)

        gu = _dot(x_ref[:, 0:D_MODEL].astype(BF16), wgu_s[...])
        gate = jnp.minimum(gu[:, 0:EXPERT_FF] + bg_ref[0], SWIGLU_LIMIT)
        up = jnp.clip(gu[:, EXPERT_FF:2 * EXPERT_FF] + bu_ref[0], -SWIGLU_LIMIT, SWIGLU_LIMIT)
        act = gate * jax.nn.sigmoid(SWIGLU_ALPHA * gate) * (up + 1.0)
        y = _dot(act.astype(BF16), wdn_s[...]) + bd_ref[0]
        slot_gate = x_ref[:, D_MODEL:XS_COLS]
        for j in range(D_MODEL // LANES):
            y_ref[:, j * LANES:(j + 1) * LANES] = y[:, j * LANES:(j + 1) * LANES] * slot_gate

    @pl.when(i >= n_used)
    def _():
        y_ref[...] = jnp.zeros_like(y_ref)


def _moe(layer, rt, xs, w_gu, w_down, bg, bu, bd, sel):
    bias = pl.BlockSpec((1, 1, D_MODEL), lambda i, be, nu: (be[i], 0, 0))
    return pl.pallas_call(
        _moe_kernel,
        grid_spec=pltpu.PrefetchScalarGridSpec(
            num_scalar_prefetch=2,
            grid=(N_BLOCKS,),
            in_specs=[
                pl.BlockSpec((MOE_BLK, XS_COLS), lambda i, be, nu: (jnp.minimum(i, nu[0] - 1), 0)),
                pl.BlockSpec((None, None, D_MODEL, 2 * EXPERT_FF), lambda i, be, nu: (layer, be[i], 0, 0)),
                pl.BlockSpec((None, None, EXPERT_FF, D_MODEL), lambda i, be, nu: (layer, be[i], 0, 0)),
                bias, bias, bias,
                pl.BlockSpec((2 * LANES, 2 * LANES), lambda i, be, nu: (0, 0)),
            ],
            out_specs=pl.BlockSpec((MOE_BLK, D_MODEL), lambda i, be, nu: (i, 0)),
            scratch_shapes=[
                pltpu.VMEM((D_MODEL, 2 * EXPERT_FF), BF16),
                pltpu.VMEM((EXPERT_FF, D_MODEL), BF16),
            ],
        ),
        out_shape=jax.ShapeDtypeStruct((N_SLOTS, D_MODEL), F32),
        compiler_params=pltpu.CompilerParams(dimension_semantics=("arbitrary",), vmem_limit_bytes=VMEM_LIMIT),
        name="moe",
    )(rt["blk_e"], rt["n_used"], xs, w_gu, w_down, bg, bu, bd, sel)


def _combine_kernel(cnt_ref, off_ref, run_ref, y_ref, lp_ref, x1_ref, mod_ref, fw_ref, x2_ref, yn_ref, stage, sem):
    i = pl.program_id(0)
    slot = i % 2

    def copy_in(s_):
        return lambda s, d, n: pltpu.make_async_copy(y_ref.at[pl.ds(d, n), :], stage.at[s_, pl.ds(s, n), :],
                                                     sem.at[s_])

    @pl.when(i == 0)
    def _():
        stage[...] = jnp.zeros_like(stage)
        _run_copies(cnt_ref, off_ref, run_ref, 0, N_EXPERTS, RUN_BITS, copy_in(0), True)

    @pl.when(i + 1 < N_TILES)
    def _():
        _run_copies(cnt_ref, off_ref, run_ref, (i + 1) * N_EXPERTS, N_EXPERTS, RUN_BITS, copy_in(1 - slot), True)

    _run_copies(cnt_ref, off_ref, run_ref, i * N_EXPERTS, N_EXPERTS, RUN_BITS, copy_in(slot), False)
    lane = lax.broadcasted_iota(I32, (TM, STAGE_ROWS), 1)
    mine = jnp.zeros((TM, STAGE_ROWS), F32)
    for k in range(TOP_K):
        mine = jnp.where(lane == lp_ref[:, k:k + 1], 1.0, mine)
    moe = _dot(mine.astype(BF16), stage[slot].astype(BF16))
    gate2 = mod_ref[0][:, 5 * D_MODEL:6 * D_MODEL]
    x2 = x1_ref[...] + gate2 * moe
    x2_ref[...] = x2
    yn_ref[...] = _rms(x2, fw_ref[...])


def _combine(rt, y_slots, lp, x1, mod3, final_w):
    row = lambda n: pl.BlockSpec((TM, n), lambda i, *_: (i, 0))
    return pl.pallas_call(
        _combine_kernel,
        grid_spec=pltpu.PrefetchScalarGridSpec(
            num_scalar_prefetch=3,
            grid=(N_TILES,),
            in_specs=[
                pl.BlockSpec(memory_space=pl.ANY),
                row(LANES), row(D_MODEL),
                pl.BlockSpec((1, 1, 6 * D_MODEL), lambda i, *_: (_mod_row(i), 0, 0)),
                pl.BlockSpec((1, D_MODEL), lambda i, *_: (0, 0)),
            ],
            out_specs=(row(D_MODEL), row(D_MODEL)),
            scratch_shapes=[pltpu.VMEM((2, STAGE_ROWS, D_MODEL), F32), pltpu.SemaphoreType.DMA((2,))],
        ),
        out_shape=(jax.ShapeDtypeStruct((N_TOK, D_MODEL), F32), jax.ShapeDtypeStruct((N_TOK, D_MODEL), F32)),
        compiler_params=pltpu.CompilerParams(dimension_semantics=("arbitrary",), vmem_limit_bytes=VMEM_LIMIT),
        name="combine",
    )(rt["cnt"], rt["off"], rt["run"], y_slots, lp, x1, mod3, final_w)


def _rope_tables():
    rows = DEC_SEQ // GRID_W
    t = jnp.arange(rows * GRID_W)
    row = (t // GRID_W).astype(F32)
    col = (t % GRID_W).astype(F32)
    half = QK_ROPE // 2
    inv = ROPE_BASE ** (-jnp.arange(0, half, 2, dtype=F32) / half)
    ang_r, ang_c = row[:, None] * inv, col[:, None] * inv
    cos32 = jnp.concatenate([jnp.cos(ang_r), jnp.cos(ang_r), jnp.cos(ang_c), jnp.cos(ang_c)], axis=-1)
    sin32 = jnp.concatenate([-jnp.sin(ang_r), jnp.sin(ang_r), -jnp.sin(ang_c), jnp.sin(ang_c)], axis=-1)
    cos32 = jnp.concatenate([jnp.ones((TM, QK_ROPE), F32), cos32], axis=0)
    sin32 = jnp.concatenate([jnp.zeros((TM, QK_ROPE), F32), sin32], axis=0)
    n = cos32.shape[0]
    pad = HEAD_PAD - QK_NOPE - QK_ROPE
    cos_h = jnp.concatenate([jnp.ones((n, QK_NOPE), F32), cos32, jnp.zeros((n, pad), F32)], axis=-1)
    sin_h = jnp.concatenate([jnp.zeros((n, QK_NOPE), F32), sin32, jnp.zeros((n, pad), F32)], axis=-1)
    zeros = jnp.zeros((n, LANES - QK_ROPE), F32)
    j = jnp.arange(QK_ROPE)
    tile = jnp.zeros((LANES, MLA_HEADS, HEAD_PAD), F32).at[j, :, QK_NOPE + j].set(1.0)
    return {
        "cosq": jnp.tile(cos_h, (1, MLA_HEADS)), "sinq": jnp.tile(sin_h, (1, MLA_HEADS)),
        "cosk": jnp.concatenate([cos32, zeros], axis=-1), "sink": jnp.concatenate([sin32, zeros], axis=-1),
        "tile": tile.reshape(LANES, MLA_HEADS * HEAD_PAD).astype(BF16),
    }


def _rot_partner():
    i = jnp.arange(QK_ROPE)
    quarter = QK_ROPE // 4
    return (i // (2 * quarter)) * (2 * quarter) + (i % (2 * quarter) + quarter) % (2 * quarter)


def _layer_weights(p, l):
    w_in = p["w_in"][l]
    sizes = (CONV_DIM, CONV_DIM, CONV_DIM, Q_RANK, KV_RANK, QK_ROPE, SSM_INNER, SSM_CONV_CH, SSM_HEADS, SSM_HEADS,
             D_MODEL, D_MODEL, D_MODEL)
    offs = [0]
    for s in sizes:
        offs.append(offs[-1] + s)
    seg = lambda k: w_in[:, offs[k]:offs[k + 1]]
    kpe = seg(5)
    small = jnp.concatenate([kpe, seg(8), seg(9), kpe[:, _rot_partner()],
                             jnp.zeros((D_MODEL, LANES - SM_KPE_ROT - QK_ROPE), F32)], axis=1)
    w_in2 = jnp.concatenate([seg(0), seg(1), seg(2), seg(3), seg(4), seg(6), seg(7), seg(10), seg(11), seg(12),
                             small], axis=1).astype(BF16)
    hd = QK_NOPE + QK_ROPE
    pad = HEAD_PAD - hd
    wq = p["w_uq"][l].reshape(Q_RANK, MLA_HEADS, hd)
    wq_a = jnp.pad(wq, ((0, 0), (0, 0), (0, pad)))
    wq_rot = wq[:, :, QK_NOPE:][:, :, _rot_partner()]
    wq_b = jnp.pad(wq_rot, ((0, 0), (0, 0), (QK_NOPE, pad)))
    wuk = jnp.pad(p["w_uk"][l], ((0, 0), (0, 0), (0, HEAD_PAD - QK_NOPE)))
    rw = jnp.pad(p["router_w"][l], ((0, 0), (0, LANES - N_EXPERTS)))
    rb = jnp.concatenate([p["router_b"][l], jnp.full((LANES - N_EXPERTS,), -jnp.inf, F32)])
    return {
        "norm1_w": p["norm1_w"][l][None], "w_in2": w_in2, "q_norm_w": p["q_norm_w"][l][None],
        "wq_a": wq_a.reshape(Q_RANK, -1).astype(BF16), "wq_b": wq_b.reshape(Q_RANK, -1).astype(BF16),
        "kv_norm_w": p["kv_norm_w"][l][None],
        "wuk": wuk.reshape(KV_RANK, -1).astype(BF16), "wuv": p["w_uv"][l].reshape(KV_RANK, -1).astype(BF16),
        "conv_a_w": p["conv_a_w"][l], "ssm_conv_w": p["ssm_conv_w"][l], "ssm_conv_b": p["ssm_conv_b"][l][None],
        "w_out_a": p["w_out_a"][l].astype(BF16), "w_o_mla": p["w_o_mla"][l].astype(BF16),
        "d_skip": jnp.repeat(p["d_skip"][l], SSM_HEAD_DIM)[None], "ssm_norm_w": p["ssm_norm_w"][l][None],
        "w_o_ssm": p["w_o_ssm"][l].astype(BF16), "w_o": p["w_o"][l].astype(BF16),
        "norm2_w": p["norm2_w"][l][None], "router_w": rw.astype(BF16), "router_b": rb[None],
        "bg": p["b_gu"][l][:, None, 0::2], "bu": p["b_gu"][l][:, None, 1::2], "bd": p["b_down"][l][:, None, :],
    }


def _ssd_params(p, l):
    z = lambda n: jnp.zeros((n,), F32)
    lanes = lambda f, b: jnp.concatenate([z(SM_DTF), f, b, z(LANES - SM_DTB - SSM_HEADS)])
    ones = jnp.ones((SSM_HEADS,), F32)
    rows = [lanes(p["dt_bias_fwd"][l], p["dt_bias_bwd"][l]), lanes(p["a_log_fwd"][l], p["a_log_bwd"][l]),
            lanes(ones, ones)]
    return jnp.concatenate([jnp.stack(rows), jnp.zeros((SUBLANES - 3, LANES), F32)], axis=0)


def _routing(cnt_tiles):
    cnt = cnt_tiles[:, 0, 0:N_EXPERTS]
    cnt = (cnt + RUN_ALIGN - 1) // RUN_ALIGN * RUN_ALIGN
    per_expert = jnp.sum(cnt, axis=0)
    padded = (per_expert + MOE_BLK - 1) // MOE_BLK * MOE_BLK
    pad_end = jnp.cumsum(padded)
    pad_start = pad_end - padded
    run = pad_start[None, :] + jnp.cumsum(cnt, axis=0) - cnt
    off = jnp.cumsum(cnt, axis=1) - cnt
    starts = jnp.arange(N_BLOCKS, dtype=I32) * MOE_BLK
    blk_e = jnp.minimum(jnp.sum((pad_end[None, :] <= starts[:, None]).astype(I32), axis=1), N_EXPERTS - 1)
    off_v = jnp.zeros((N_TILES, SUBLANES, LANES), F32).at[:, 0, 0:N_EXPERTS].set(off.astype(F32))
    return {
        "cnt": cnt.reshape(-1).astype(I32), "off": off.reshape(-1).astype(I32), "run": run.reshape(-1).astype(I32),
        "tail_cnt": (padded - per_expert).astype(I32), "tail_dst": (pad_start + per_expert).astype(I32),
        "blk_e": blk_e.astype(I32), "n_used": (pad_end[-1] // MOE_BLK).astype(I32).reshape(1), "off_v": off_v,
    }


def _deinterleave_matrix():
    k = jnp.arange(2 * LANES)[:, None]
    n = jnp.arange(2 * LANES)[None, :]
    src = jnp.where(n < LANES, 2 * n, 2 * (n - LANES) + 1)
    return (k == src).astype(BF16)


def kernel(x_prompt, x_sample, cache_ckv, cache_kpe, state_ssm_fwd, state_ssm_bwd, c, c_ctx, w_ada, b_ada, norm1_w, w_in, conv_a_w, w_out_a, q_norm_w, w_uq, kv_norm_w, w_uk, w_uv, w_o_mla, ssm_conv_w, ssm_conv_b, dt_bias_fwd, dt_bias_bwd, a_log_fwd, a_log_bwd, d_skip, ssm_norm_w, w_o_ssm, w_o, norm2_w, router_w, router_b, w_gu, b_gu, w_down, b_down, final_norm_w):
    p = dict(norm1_w=norm1_w, w_in=w_in, conv_a_w=conv_a_w, w_out_a=w_out_a, q_norm_w=q_norm_w, w_uq=w_uq,
             kv_norm_w=kv_norm_w, w_uk=w_uk, w_uv=w_uv, w_o_mla=w_o_mla, ssm_conv_w=ssm_conv_w,
             ssm_conv_b=ssm_conv_b, dt_bias_fwd=dt_bias_fwd, dt_bias_bwd=dt_bias_bwd, a_log_fwd=a_log_fwd,
             a_log_bwd=a_log_bwd, d_skip=d_skip, ssm_norm_w=ssm_norm_w, w_o_ssm=w_o_ssm, w_o=w_o,
             norm2_w=norm2_w, router_w=router_w, router_b=router_b, b_gu=b_gu, b_down=b_down)
    x = jnp.concatenate([x_prompt.reshape(N_CTX, D_MODEL), x_sample.reshape(N_LAT, D_MODEL)], axis=0)
    cond8 = jnp.concatenate([c_ctx[None], c, jnp.zeros((SUBLANES - 1 - DEC_BATCH, D_MODEL), F32)], axis=0)
    mods = _ada_mods(cond8, w_ada, b_ada)
    tabs = _rope_tables()
    sel = _deinterleave_matrix()
    final_w = final_norm_w[None]
    hp = SSM_INNER
    zeros_state = jnp.zeros((BATCH, hp, SSM_STATE), F32)

    ckv_out, kpe_out, sf_out, sb_out = [], [], [], []
    y_norm = None
    for l in range(DEPTH):
        lw = _layer_weights(p, l)
        mod3 = mods[l].reshape(SUBLANES, 1, 6 * D_MODEL)
        a2, q, ckv, kf, v, small, z, xbc, g = _inproj(x, mod3, lw, tabs)
        ta, xact = _convs(a2, xbc, lw)
        kpe_c = jnp.pad(cache_kpe[:, l].reshape(DEC_BATCH * PAST_LEN, QK_ROPE), ((0, 0), (0, LANES - QK_ROPE)))
        kf_c, v_c = _kvcache(cache_ckv[:, l].reshape(DEC_BATCH * PAST_LEN, KV_RANK), kpe_c, lw, tabs)
        att = _attention(q, kf, v, kf_c, v_c)
        init_f = jnp.concatenate([zeros_state, state_ssm_fwd[:, l].reshape(DEC_BATCH, hp, SSM_STATE)], axis=0)
        init_b = jnp.concatenate([zeros_state, state_ssm_bwd[:, l].reshape(DEC_BATCH, hp, SSM_STATE)], axis=0)
        yf, yb, sf, sb = _ssd(xact, small, init_f, init_b, _ssd_params(p, l))
        x1, h2, top_i, top_g, cnt_tiles = _merge(ta, att, yf, yb, xact, z, g, x, mod3, lw)
        rt = _routing(cnt_tiles)
        xs, lp = _dispatch(rt, h2, top_i, top_g)
        y_slots = _moe(l, rt, xs, w_gu, w_down, lw["bg"], lw["bu"], lw["bd"], sel)
        x, y_norm = _combine(rt, y_slots, lp, x1, mod3, final_w)
        ckv_out.append(ckv[:N_CTX].reshape(BATCH, SEQ, KV_RANK))
        kpe_out.append(small[:N_CTX, 0:QK_ROPE].reshape(BATCH, SEQ, QK_ROPE))
        sf_out.append(sf[:BATCH].reshape(BATCH, SSM_HEADS, SSM_HEAD_DIM, SSM_STATE))
        sb_out.append(sb[:BATCH].reshape(BATCH, SSM_HEADS, SSM_HEAD_DIM, SSM_STATE))

    y_prompt = y_norm[:N_CTX].reshape(BATCH, SEQ, D_MODEL)
    y_sample = y_norm[N_CTX:].reshape(DEC_BATCH, DEC_SEQ, D_MODEL)
    return (y_prompt, y_sample, jnp.stack(ckv_out, axis=1), jnp.stack(kpe_out, axis=1),
            jnp.stack(sf_out, axis=1), jnp.stack(sb_out, axis=1))
```

```python
import functools
import math

import jax
import jax.numpy as jnp
from jax import lax
from jax.experimental import pallas as pl
from jax.experimental.pallas import tpu as pltpu

F32 = jnp.float32
BF16 = jnp.bfloat16
I32 = jnp.int32

D_MODEL = 1024
BATCH = 16
SEQ = 256
DEPTH = 2
DEC_BATCH = 2
DEC_SEQ = 2048
PAST_LEN = 512
GRID_W = 64
NORM_EPS = 1e-6
CONV_DIM = 512
MLA_HEADS = 8
Q_RANK = 384
KV_RANK = 256
QK_NOPE = 64
QK_ROPE = 32
V_HEAD = 64
ROPE_BASE = 10000.0
MLA_SCALE = (QK_NOPE + QK_ROPE) ** -0.5
SSM_HEADS = 16
SSM_HEAD_DIM = 64
SSM_INNER = SSM_HEADS * SSM_HEAD_DIM
SSM_GROUPS = 2
SSM_STATE = 128
SSM_CONV_CH = SSM_INNER + 2 * SSM_GROUPS * SSM_STATE
N_EXPERTS = 32
TOP_K = 4
EXPERT_FF = D_MODEL
SWIGLU_ALPHA = 1.702
SWIGLU_LIMIT = 7.0

N_CTX = BATCH * SEQ
N_LAT = DEC_BATCH * DEC_SEQ
N_TOK = N_CTX + N_LAT
N_SEQS = BATCH + DEC_BATCH

LANES = 128
SUBLANES = 8
HEAD_PAD = 128
TM = 256
N_TILES = N_TOK // TM
CTX_TILES = N_CTX // TM
LAT_TILES_PER_SEQ = DEC_SEQ // TM
CHUNK = 128
CTX_CHUNKS_PER_SEQ = SEQ // CHUNK
LAT_CHUNKS_PER_SEQ = DEC_SEQ // CHUNK
N_CTX_CHUNKS = N_CTX // CHUNK
N_CHUNKS = N_TOK // CHUNK
MOE_BLK = 256
N_ASSIGN = N_TOK * TOP_K
RUN_ALIGN = SUBLANES
RUN_BITS = tuple(range(8, 2, -1))
TAIL_BITS = tuple(range(7, 2, -1))
STAGE_ROWS = 1280
XS_COLS = D_MODEL + LANES
N_BLOCKS = -(-(N_ASSIGN + N_TILES * N_EXPERTS * (RUN_ALIGN - 1) + N_EXPERTS * (MOE_BLK - 1)) // MOE_BLK)
N_SLOTS = N_BLOCKS * MOE_BLK
VMEM_LIMIT = 56 * 1024 * 1024

C_A3 = 0
C_CQ = C_A3 + 3 * CONV_DIM
C_CKV = C_CQ + Q_RANK
C_Z = C_CKV + KV_RANK
C_XBC = C_Z + SSM_INNER
C_GATE = C_XBC + SSM_CONV_CH
C_SMALL = C_GATE + 3 * D_MODEL
IN_COLS2 = C_SMALL + LANES
SM_DTF = QK_ROPE
SM_DTB = QK_ROPE + SSM_HEADS
SM_KPE_ROT = 64


def _rms(x, w):
    return x * lax.rsqrt(jnp.mean(x * x, axis=-1, keepdims=True) + NORM_EPS) * w


def _silu(x):
    return x * jax.nn.sigmoid(x)


def _dot(a, b):
    return jnp.dot(a, b, preferred_element_type=F32)


def _dot_nt(a, b):
    return lax.dot_general(a, b, (((1,), (1,)), ((), ())), preferred_element_type=F32)


def _resident(shape):
    nd = len(shape)
    return pl.BlockSpec(shape, lambda *_: (0,) * nd, pipeline_mode=pl.Buffered(1))


def _mod_row(i):
    return jnp.where(i < CTX_TILES, 0, 1 + (i - CTX_TILES) // LAT_TILES_PER_SEQ)


def _pos_block(i):
    return jnp.where(i < CTX_TILES, 0, 1 + (i - CTX_TILES) % LAT_TILES_PER_SEQ)


def _ada_kernel(c_ref, w_ref, b_ref, o_ref):
    s = _silu(c_ref[...]).astype(BF16)
    o_ref[0] = _dot(s, w_ref[0].astype(BF16)) + b_ref[0]


def _ada_mods(cond8, w_ada, b_ada):
    tn = 1536
    n_mod = 6 * D_MODEL
    return pl.pallas_call(
        _ada_kernel,
        grid=(DEPTH, n_mod // tn),
        in_specs=[
            pl.BlockSpec((SUBLANES, D_MODEL), lambda l, j: (0, 0)),
            pl.BlockSpec((1, D_MODEL, tn), lambda l, j: (l, 0, j)),
            pl.BlockSpec((1, 1, tn), lambda l, j: (l, 0, j)),
        ],
        out_specs=pl.BlockSpec((1, SUBLANES, tn), lambda l, j: (l, 0, j)),
        out_shape=jax.ShapeDtypeStruct((DEPTH, SUBLANES, n_mod), F32),
        compiler_params=pltpu.CompilerParams(dimension_semantics=("arbitrary", "arbitrary")),
        name="ada_mods",
    )(cond8, w_ada, b_ada.reshape(DEPTH, 1, n_mod))


def _inproj_kernel(x_ref, mod_ref, n1w_ref, w_ref, qnw_ref, wqa_ref, wqb_ref, kvnw_ref, wuk_ref, wuv_ref,
                   tile_ref, cosq_ref, sinq_ref, cosk_ref, sink_ref,
                   a2_ref, q_ref, ckv_ref, kf_ref, v_ref, small_ref, z_ref, xbc_ref, g_ref):
    mod = mod_ref[0]
    shift1 = mod[:, 0:D_MODEL]
    scale1 = mod[:, D_MODEL:2 * D_MODEL]
    hb = (_rms(x_ref[...], n1w_ref[...]) * (1.0 + scale1) + shift1).astype(BF16)

    def seg(a, b):
        return _dot(hb, w_ref[:, a:b])

    a3 = seg(C_A3, C_CQ)
    a2_ref[:, 0:CONV_DIM] = a3[:, 2 * CONV_DIM:3 * CONV_DIM] * a3[:, 0:CONV_DIM]
    a2_ref[:, CONV_DIM:2 * CONV_DIM] = a3[:, CONV_DIM:2 * CONV_DIM]

    cqn = _rms(seg(C_CQ, C_CKV), qnw_ref[...]).astype(BF16)
    q = _dot(cqn, wqa_ref[...]) * cosq_ref[...] + _dot(cqn, wqb_ref[...]) * sinq_ref[...]
    q_ref[...] = q.astype(BF16)

    ckv = _rms(seg(C_CKV, C_Z), kvnw_ref[...])
    ckv_ref[...] = ckv
    ckvb = ckv.astype(BF16)
    small = seg(C_SMALL, IN_COLS2)
    small_ref[...] = small
    kpe = small * cosk_ref[...] + pltpu.roll(small, LANES - SM_KPE_ROT, 1) * sink_ref[...]
    kf_ref[...] = (_dot(ckvb, wuk_ref[...]) + _dot(kpe.astype(BF16), tile_ref[...])).astype(BF16)
    v_ref[...] = _dot(ckvb, wuv_ref[...]).astype(BF16)

    z_ref[...] = seg(C_Z, C_XBC)
    xbc_ref[...] = seg(C_XBC, C_GATE)
    g_ref[...] = jax.nn.sigmoid(seg(C_GATE, C_SMALL))


def _inproj(x, mod3, lw, tabs):
    row = lambda n: pl.BlockSpec((TM, n), lambda i: (i, 0))
    tab = lambda n: pl.BlockSpec((TM, n), lambda i: (_pos_block(i), 0))
    qw = MLA_HEADS * HEAD_PAD
    vw = MLA_HEADS * V_HEAD
    out_shape = (
        jax.ShapeDtypeStruct((N_TOK, 2 * CONV_DIM), F32),
        jax.ShapeDtypeStruct((N_TOK, qw), BF16),
        jax.ShapeDtypeStruct((N_TOK, KV_RANK), F32),
        jax.ShapeDtypeStruct((N_TOK, qw), BF16),
        jax.ShapeDtypeStruct((N_TOK, vw), BF16),
        jax.ShapeDtypeStruct((N_TOK, LANES), F32),
        jax.ShapeDtypeStruct((N_TOK, SSM_INNER), F32),
        jax.ShapeDtypeStruct((N_TOK, SSM_CONV_CH), F32),
        jax.ShapeDtypeStruct((N_TOK, 3 * D_MODEL), F32),
    )
    return pl.pallas_call(
        _inproj_kernel,
        grid=(N_TILES,),
        in_specs=[
            row(D_MODEL),
            pl.BlockSpec((1, 1, 6 * D_MODEL), lambda i: (_mod_row(i), 0, 0)),
            _resident((1, D_MODEL)),
            _resident((D_MODEL, IN_COLS2)),
            _resident((1, Q_RANK)),
            _resident((Q_RANK, qw)),
            _resident((Q_RANK, qw)),
            _resident((1, KV_RANK)),
            _resident((KV_RANK, qw)),
            _resident((KV_RANK, vw)),
            _resident((LANES, qw)),
            tab(qw), tab(qw), tab(LANES), tab(LANES),
        ],
        out_specs=tuple(row(s.shape[1]) for s in out_shape),
        out_shape=out_shape,
        compiler_params=pltpu.CompilerParams(dimension_semantics=("arbitrary",), vmem_limit_bytes=VMEM_LIMIT),
        name="inproj",
    )(x, mod3, lw["norm1_w"], lw["w_in2"], lw["q_norm_w"], lw["wq_a"], lw["wq_b"], lw["kv_norm_w"],
      lw["wuk"], lw["wuv"], tabs["tile"], tabs["cosq"], tabs["sinq"], tabs["cosk"], tabs["sink"])


def _kvcache_kernel(ckv_ref, kpe_ref, wuk_ref, wuv_ref, tile_ref, kf_ref, v_ref):
    ckvb = ckv_ref[...].astype(BF16)
    kf_ref[...] = (_dot(ckvb, wuk_ref[...]) + _dot(kpe_ref[...].astype(BF16), tile_ref[...])).astype(BF16)
    v_ref[...] = _dot(ckvb, wuv_ref[...]).astype(BF16)


def _kvcache(ckv, kpe128, lw, tabs):
    n = ckv.shape[0]
    qw = MLA_HEADS * HEAD_PAD
    vw = MLA_HEADS * V_HEAD
    return pl.pallas_call(
        _kvcache_kernel,
        grid=(n // PAST_LEN,),
        in_specs=[
            pl.BlockSpec((PAST_LEN, KV_RANK), lambda i: (i, 0)),
            pl.BlockSpec((PAST_LEN, LANES), lambda i: (i, 0)),
            _resident((KV_RANK, qw)),
            _resident((KV_RANK, vw)),
            _resident((LANES, qw)),
        ],
        out_specs=(pl.BlockSpec((PAST_LEN, qw), lambda i: (i, 0)), pl.BlockSpec((PAST_LEN, vw), lambda i: (i, 0))),
        out_shape=(jax.ShapeDtypeStruct((n, qw), BF16), jax.ShapeDtypeStruct((n, vw), BF16)),
        compiler_params=pltpu.CompilerParams(dimension_semantics=("arbitrary",)),
        name="kvcache",
    )(ckv, kpe128, lw["wuk"], lw["wuv"], tabs["tile"])


def _conv_kernel(a2_ref, sp_ref, sn_ref, xbc_ref, xp_ref, xn_ref, caw_ref, cw_ref, cb_ref, ta_ref, xact_ref):
    i = pl.program_id(0)
    j = (i - CTX_TILES) % LAT_TILES_PER_SEQ
    is_ctx = i < CTX_TILES
    keep_prev = jnp.where(jnp.logical_or(is_ctx, j == 0), 0.0, 1.0)
    keep_next = jnp.where(jnp.logical_or(is_ctx, j == LAT_TILES_PER_SEQ - 1), 0.0, 1.0)
    row = lax.broadcasted_iota(I32, (TM, 1), 0)

    def conv3(x, prev_row, next_row, w_ref):
        x_prev = jnp.where(row == 0, prev_row * keep_prev, pltpu.roll(x, 1, 0))
        x_next = jnp.where(row == TM - 1, next_row * keep_next, pltpu.roll(x, TM - 1, 0))
        return x_prev * w_ref[0:1, :] + x * w_ref[1:2, :] + x_next * w_ref[2:3, :]

    s = a2_ref[:, 0:CONV_DIM]
    gb = a2_ref[:, CONV_DIM:2 * CONV_DIM]
    ta_ref[...] = (gb * conv3(s, sp_ref[SUBLANES - 1:SUBLANES, :], sn_ref[0:1, :], caw_ref)).astype(BF16)
    xc = conv3(xbc_ref[...], xp_ref[SUBLANES - 1:SUBLANES, :], xn_ref[0:1, :], cw_ref) + cb_ref[...]
    xact_ref[...] = _silu(xc)


def _convs(a2, xbc, lw):
    per = TM // SUBLANES
    last = N_TOK // SUBLANES - 1
    prev = lambda n: pl.BlockSpec((SUBLANES, n), lambda i: (jnp.maximum(i * per - 1, 0), 0))
    nxt = lambda n: pl.BlockSpec((SUBLANES, n), lambda i: (jnp.minimum((i + 1) * per, last), 0))
    return pl.pallas_call(
        _conv_kernel,
        grid=(N_TILES,),
        in_specs=[
            pl.BlockSpec((TM, 2 * CONV_DIM), lambda i: (i, 0)), prev(CONV_DIM), nxt(CONV_DIM),
            pl.BlockSpec((TM, SSM_CONV_CH), lambda i: (i, 0)), prev(SSM_CONV_CH), nxt(SSM_CONV_CH),
            pl.BlockSpec((3, CONV_DIM), lambda i: (0, 0)),
            pl.BlockSpec((3, SSM_CONV_CH), lambda i: (0, 0)),
            pl.BlockSpec((1, SSM_CONV_CH), lambda i: (0, 0)),
        ],
        out_specs=(pl.BlockSpec((TM, CONV_DIM), lambda i: (i, 0)), pl.BlockSpec((TM, SSM_CONV_CH), lambda i: (i, 0))),
        out_shape=(jax.ShapeDtypeStruct((N_TOK, CONV_DIM), BF16), jax.ShapeDtypeStruct((N_TOK, SSM_CONV_CH), F32)),
        compiler_params=pltpu.CompilerParams(dimension_semantics=("arbitrary",)),
        name="convs",
    )(a2, a2, a2, xbc, xbc, xbc, lw["conv_a_w"], lw["ssm_conv_w"], lw["ssm_conv_b"])


def _attn_heads(q_ref, kv_refs, o_ref, acc_ref):
    for h in range(MLA_HEADS):
        qh = q_ref[:, h * HEAD_PAD:(h + 1) * HEAD_PAD]
        ss = [_dot_nt(qh, k_ref[:, h * HEAD_PAD:(h + 1) * HEAD_PAD]) * MLA_SCALE for k_ref, _ in kv_refs]
        m = functools.reduce(jnp.maximum, [jnp.max(s, axis=-1, keepdims=True) for s in ss])
        ps = [jnp.exp(s - m) for s in ss]
        l = functools.reduce(jnp.add, [jnp.sum(p, axis=-1, keepdims=True) for p in ps])
        o = functools.reduce(jnp.add, [_dot(p.astype(BF16), v_ref[:, h * V_HEAD:(h + 1) * V_HEAD])
                                       for p, (_, v_ref) in zip(ps, kv_refs)])
        acc_ref[:, h * V_HEAD:(h + 1) * V_HEAD] = o / l
    o_ref[...] = acc_ref[...].astype(BF16)


def _attn_ctx_kernel(q_ref, k_ref, v_ref, o_ref, acc_ref):
    _attn_heads(q_ref, [(k_ref, v_ref)], o_ref, acc_ref)


def _attn_lat_kernel(q_ref, k_ref, v_ref, kc_ref, vc_ref, o_ref, acc_ref):
    _attn_heads(q_ref, [(k_ref, v_ref), (kc_ref, vc_ref)], o_ref, acc_ref)


def _attention(q, kf, v, kf_c, v_c):
    qw = MLA_HEADS * HEAD_PAD
    vw = MLA_HEADS * V_HEAD
    att_ctx = pl.pallas_call(
        _attn_ctx_kernel,
        grid=(BATCH,),
        in_specs=[pl.BlockSpec((SEQ, qw), lambda b: (b, 0)), pl.BlockSpec((SEQ, qw), lambda b: (b, 0)),
                  pl.BlockSpec((SEQ, vw), lambda b: (b, 0))],
        out_specs=pl.BlockSpec((SEQ, vw), lambda b: (b, 0)),
        out_shape=jax.ShapeDtypeStruct((N_CTX, vw), BF16),
        scratch_shapes=[pltpu.VMEM((SEQ, vw), F32)],
        compiler_params=pltpu.CompilerParams(dimension_semantics=("arbitrary",)),
        name="attn_ctx",
    )(q, kf, v)
    lat0 = N_CTX // DEC_SEQ
    att_lat = pl.pallas_call(
        _attn_lat_kernel,
        grid=(DEC_BATCH, LAT_TILES_PER_SEQ),
        in_specs=[
            pl.BlockSpec((TM, qw), lambda b, t: (CTX_TILES + b * LAT_TILES_PER_SEQ + t, 0)),
            pl.BlockSpec((DEC_SEQ, qw), lambda b, t: (lat0 + b, 0)),
            pl.BlockSpec((DEC_SEQ, vw), lambda b, t: (lat0 + b, 0)),
            pl.BlockSpec((PAST_LEN, qw), lambda b, t: (b, 0)),
            pl.BlockSpec((PAST_LEN, vw), lambda b, t: (b, 0)),
        ],
        out_specs=pl.BlockSpec((TM, vw), lambda b, t: (b * LAT_TILES_PER_SEQ + t, 0)),
        out_shape=jax.ShapeDtypeStruct((N_LAT, vw), BF16),
        scratch_shapes=[pltpu.VMEM((TM, vw), F32)],
        compiler_params=pltpu.CompilerParams(dimension_semantics=("arbitrary", "arbitrary"),
                                             vmem_limit_bytes=VMEM_LIMIT),
        name="attn_lat",
    )(q, kf, v, kf_c, v_c)
    return jnp.concatenate([att_ctx, att_lat], axis=0)


def _seq_of_chunk(s):
    return jnp.where(s < N_CTX_CHUNKS, s // CTX_CHUNKS_PER_SEQ,
                     BATCH + (s - N_CTX_CHUNKS) // LAT_CHUNKS_PER_SEQ)


def _chunk_in_seq(s):
    return jnp.where(s < N_CTX_CHUNKS, s % CTX_CHUNKS_PER_SEQ, (s - N_CTX_CHUNKS) % LAT_CHUNKS_PER_SEQ)


def _chunks_in_seq(s):
    return jnp.where(s < N_CTX_CHUNKS, CTX_CHUNKS_PER_SEQ, LAT_CHUNKS_PER_SEQ)


def _mirror_chunk(s):
    return s + _chunks_in_seq(s) - 1 - 2 * _chunk_in_seq(s)


def _split3(a):
    a1 = a.astype(BF16)
    r1 = a - a1.astype(F32)
    a2 = r1.astype(BF16)
    a3 = (r1 - a2.astype(F32)).astype(BF16)
    return a1, a2, a3


def _ssd_direction(x_ref, sm_ref, par_ref, st_ref, y_ref, lane0, backward):
    ri = lax.broadcasted_iota(I32, (CHUNK, CHUNK), 0)
    ci = lax.broadcasted_iota(I32, (CHUNK, CHUNK), 1)
    tri = (ci >= ri) if backward else (ci <= ri)
    tri_b = jnp.where(tri, 1.0, 0.0).astype(BF16)
    tot_row = 0 if backward else CHUNK - 1

    dt = jax.nn.softplus(sm_ref[...] + par_ref[0:1, :])
    a = dt * (-jnp.exp(par_ref[1:2, :])) * par_ref[2:3, :]
    a1, a2, a3 = _split3(a)
    acs = _dot(tri_b, a1) + _dot(tri_b, a2) + _dot(tri_b, a3)
    acs_t = acs.T
    dt_t = dt.T
    first_head = lax.broadcasted_iota(I32, (1, LANES), 1) < SSM_HEAD_DIM

    def block_diag(pair):
        return jnp.concatenate([jnp.where(first_head, pair, 0.0), jnp.where(first_head, 0.0, pair)],
                               axis=0).astype(BF16)

    for g in range(SSM_GROUPS):
        b0 = SSM_INNER + g * SSM_STATE
        c0 = SSM_INNER + SSM_GROUPS * SSM_STATE + g * SSM_STATE
        bg = x_ref[:, b0:b0 + SSM_STATE]
        cg = x_ref[:, c0:c0 + SSM_STATE]
        cb = _dot_nt(cg.astype(BF16), bg.astype(BF16))
        bg_t = bg.T
        heads = SSM_HEADS // SSM_GROUPS
        for pr in range(heads // 2):
            h0 = g * heads + 2 * pr
            sl = slice(h0 * SSM_HEAD_DIM, (h0 + 2) * SSM_HEAD_DIM)
            x_bd = block_diag(x_ref[:, sl])
            st_old = st_ref[:, sl]
            within, carried, to_state, keep = [], [], [], []
            for h in (h0, h0 + 1):
                lane = lane0 + h
                col = jnp.broadcast_to(acs[:, lane:lane + 1], (CHUNK, CHUNK))
                row = acs_t[lane:lane + 1, :]
                dt_row = dt_t[lane:lane + 1, :]
                decay = jnp.exp(jnp.where(tri, col - row, -jnp.inf))
                within.append((cb * decay * dt_row).astype(BF16))
                carried.append((cg * jnp.exp(col)).astype(BF16))
                tot = acs[tot_row:tot_row + 1, lane:lane + 1]
                to_state.append((bg_t * (dt_row * jnp.exp(tot - row))).astype(BF16))
                keep.append(jnp.exp(tot))
            lhs = jnp.concatenate(within + carried, axis=1)
            y_ref[:, sl] = _dot(lhs, jnp.concatenate([x_bd, block_diag(st_old)], axis=0))
            st_ref[:, sl] = (st_old * jnp.where(first_head, keep[0], keep[1])
                             + _dot(jnp.concatenate(to_state, axis=1), x_bd))


def _ssd_kernel(xf_ref, xb_ref, smf_ref, smb_ref, if_ref, ib_ref, par_ref,
                yf_ref, yb_ref, sf_ref, sb_ref, stf_ref, stb_ref):
    s = pl.program_id(0)
    c = _chunk_in_seq(s)

    @pl.when(c == 0)
    def _():
        stf_ref[...] = if_ref[0].T
        stb_ref[...] = ib_ref[0].T

    _ssd_direction(xf_ref, smf_ref, par_ref, stf_ref, yf_ref, SM_DTF, False)
    _ssd_direction(xb_ref, smb_ref, par_ref, stb_ref, yb_ref, SM_DTB, True)

    @pl.when(c == _chunks_in_seq(s) - 1)
    def _():
        sf_ref[0] = stf_ref[...].T
        sb_ref[0] = stb_ref[...].T


def _ssd(xact, small, init_f, init_b, par):
    hp = SSM_INNER
    fwd = lambda n: pl.BlockSpec((CHUNK, n), lambda s: (s, 0))
    bwd = lambda n: pl.BlockSpec((CHUNK, n), lambda s: (_mirror_chunk(s), 0))
    st = pl.BlockSpec((1, hp, SSM_STATE), lambda s: (_seq_of_chunk(s), 0, 0))
    return pl.pallas_call(
        _ssd_kernel,
        grid=(N_CHUNKS,),
        in_specs=[fwd(SSM_CONV_CH), bwd(SSM_CONV_CH), fwd(LANES), bwd(LANES), st, st,
                  pl.BlockSpec((SUBLANES, LANES), lambda s: (0, 0))],
        out_specs=(fwd(hp), bwd(hp), st, st),
        out_shape=(jax.ShapeDtypeStruct((N_TOK, hp), F32), jax.ShapeDtypeStruct((N_TOK, hp), F32),
                   jax.ShapeDtypeStruct((N_SEQS, hp, SSM_STATE), F32),
                   jax.ShapeDtypeStruct((N_SEQS, hp, SSM_STATE), F32)),
        scratch_shapes=[pltpu.VMEM((SSM_STATE, hp), F32), pltpu.VMEM((SSM_STATE, hp), F32)],
        compiler_params=pltpu.CompilerParams(dimension_semantics=("arbitrary",)),
        name="ssd",
    )(xact, xact, small, small, init_f, init_b, par)


def _merge_kernel(ta_ref, att_ref, yf_ref, yb_ref, xs_ref, z_ref, g_ref, x_ref, mod_ref,
                  woa_ref, wom_ref, dsk_ref, snw_ref, wos_ref, wo_ref, n2w_ref, rw_ref, rb_ref,
                  x1_ref, h2_ref, ti_ref, tg_ref, cnt_ref):
    mod = mod_ref[0]
    gate1 = mod[:, 2 * D_MODEL:3 * D_MODEL]
    shift2 = mod[:, 3 * D_MODEL:4 * D_MODEL]
    scale2 = mod[:, 4 * D_MODEL:5 * D_MODEL]
    y_a = _dot(ta_ref[...], woa_ref[...])
    y_b = _dot(att_ref[...], wom_ref[...])
    y_ssm = (yf_ref[...] + yb_ref[...] + dsk_ref[...] * xs_ref[...]) * _silu(z_ref[...])
    y_c = _dot(_rms(y_ssm, snw_ref[...]).astype(BF16), wos_ref[...])
    merged = (g_ref[:, 0:D_MODEL] * y_a + g_ref[:, D_MODEL:2 * D_MODEL] * y_b
              + g_ref[:, 2 * D_MODEL:3 * D_MODEL] * y_c)
    x1 = x_ref[...] + gate1 * _dot(merged.astype(BF16), wo_ref[...])
    x1_ref[...] = x1
    h2 = _rms(x1, n2w_ref[...]) * (1.0 + scale2) + shift2
    h2_ref[...] = h2

    logits = _dot(h2.astype(BF16), rw_ref[...]) + rb_ref[...]
    lane = lax.broadcasted_iota(I32, (TM, LANES), 1)
    ti = jnp.zeros((TM, LANES), I32)
    tv = jnp.full((TM, LANES), -jnp.inf, F32)
    chosen = jnp.zeros((TM, LANES), F32)
    for k in range(TOP_K):
        m = jnp.max(logits, axis=-1, keepdims=True)
        idx = jnp.min(jnp.where(logits == m, lane, LANES), axis=-1, keepdims=True)
        ti = jnp.where(lane == k, idx, ti)
        tv = jnp.where(lane == k, m, tv)
        chosen = jnp.where(lane == idx, 1.0, chosen)
        logits = jnp.where(lane == idx, -jnp.inf, logits)
    e = jnp.exp(tv - jnp.max(tv, axis=-1, keepdims=True))
    ti_ref[...] = ti
    tg_ref[...] = e / jnp.sum(e, axis=-1, keepdims=True)
    cnt_ref[0] = jnp.broadcast_to(jnp.sum(chosen, axis=0, keepdims=True), (SUBLANES, LANES)).astype(I32)


def _merge(ta, att, yf, yb, xact, z, g, x, mod3, lw):
    row = lambda n: pl.BlockSpec((TM, n), lambda i: (i, 0))
    out_shape = (jax.ShapeDtypeStruct((N_TOK, D_MODEL), F32), jax.ShapeDtypeStruct((N_TOK, D_MODEL), F32),
                 jax.ShapeDtypeStruct((N_TOK, LANES), I32), jax.ShapeDtypeStruct((N_TOK, LANES), F32))
    cnt_shape = jax.ShapeDtypeStruct((N_TILES, SUBLANES, LANES), I32)
    cnt_spec = pl.BlockSpec((1, SUBLANES, LANES), lambda i: (i, 0, 0))
    return pl.pallas_call(
        _merge_kernel,
        grid=(N_TILES,),
        in_specs=[
            row(CONV_DIM), row(MLA_HEADS * V_HEAD), row(SSM_INNER), row(SSM_INNER), row(SSM_INNER),
            row(SSM_INNER), row(3 * D_MODEL), row(D_MODEL),
            pl.BlockSpec((1, 1, 6 * D_MODEL), lambda i: (_mod_row(i), 0, 0)),
            _resident((CONV_DIM, D_MODEL)), _resident((MLA_HEADS * V_HEAD, D_MODEL)),
            _resident((1, SSM_INNER)), _resident((1, SSM_INNER)), _resident((SSM_INNER, D_MODEL)),
            _resident((D_MODEL, D_MODEL)), _resident((1, D_MODEL)),
            _resident((D_MODEL, LANES)), _resident((1, LANES)),
        ],
        out_specs=tuple(row(s.shape[1]) for s in out_shape) + (cnt_spec,),
        out_shape=out_shape + (cnt_shape,),
        compiler_params=pltpu.CompilerParams(dimension_semantics=("arbitrary",), vmem_limit_bytes=VMEM_LIMIT),
        name="merge",
    )(ta, att, yf, yb, xact, z, g, x, mod3, lw["w_out_a"], lw["w_o_mla"], lw["d_skip"], lw["ssm_norm_w"],
      lw["w_o_ssm"], lw["w_o"], lw["norm2_w"], lw["router_w"], lw["router_b"])


def _run_copies(cnt_ref, src_ref, dst_ref, first, count, bits, make_copy, start):
    def body(e, carry):
        n = cnt_ref[first + e]
        s0 = src_ref[first + e] if src_ref is not None else 0
        d0 = dst_ref[first + e]
        for b in bits:
            above = (n >> (b + 1)) << (b + 1)

            @pl.when(((n >> b) & 1) == 1)
            def _():
                cp = make_copy(pl.multiple_of(s0 + above, RUN_ALIGN), pl.multiple_of(d0 + above, RUN_ALIGN), 1 << b)
                if start:
                    cp.start()
                else:
                    cp.wait()
        return carry
    lax.fori_loop(0, count, body, 0)


def _dispatch_kernel(cnt_ref, off_ref, run_ref, tcnt_ref, tdst_ref, nu_ref, h2_ref, ti_ref, tg_ref, offv_ref,
                     xs_ref, lp_ref, stage, zeros, perm_s, ghi_s, glo_s, sem, semz):
    i = pl.program_id(0)
    slot = i % 2

    def copy_out(s_):
        return lambda s, d, n: pltpu.make_async_copy(stage.at[s_, pl.ds(s, n), :], xs_ref.at[pl.ds(d, n), :],
                                                     sem.at[s_])

    @pl.when(i == 0)
    def _():
        zeros[...] = jnp.zeros_like(zeros)
        zero_out = lambda s, d, n: pltpu.make_async_copy(zeros.at[pl.ds(0, n), :], xs_ref.at[pl.ds(d, n), :], semz)
        _run_copies(tcnt_ref, None, tdst_ref, 0, N_EXPERTS, TAIL_BITS, zero_out, True)
        _run_copies(tcnt_ref, None, tdst_ref, 0, N_EXPERTS, TAIL_BITS, zero_out, False)
        zrows = zeros.shape[0]

        def unused_blocks(start):
            def body(b, carry):
                for part in range(MOE_BLK // zrows):
                    cp = zero_out(0, pl.multiple_of(b * MOE_BLK + part * zrows, RUN_ALIGN), zrows)
                    if start:
                        cp.start()
                    else:
                        cp.wait()
                return carry
            lax.fori_loop(nu_ref[0], N_BLOCKS, body, 0)
        unused_blocks(True)
        unused_blocks(False)

    @pl.when(i >= 2)
    def _():
        _run_copies(cnt_ref, off_ref, run_ref, (i - 2) * N_EXPERTS, N_EXPERTS, RUN_BITS, copy_out(slot), False)

    lane = lax.broadcasted_iota(I32, (TM, LANES), 1)
    picks = [jnp.where(lane == ti_ref[:, k:k + 1], 1.0, 0.0) for k in range(TOP_K)]
    ri = lax.broadcasted_iota(I32, (TM, TM), 0)
    ci = lax.broadcasted_iota(I32, (TM, TM), 1)
    earlier = jnp.where(ci < ri, 1.0, 0.0).astype(BF16)
    base = _dot(earlier, functools.reduce(jnp.add, picks).astype(BF16)) + offv_ref[0][0:1, :]
    lp = jnp.full((TM, LANES), -1.0, F32)
    for k in range(TOP_K):
        lp = jnp.where(lane == k, jnp.sum(picks[k] * base, axis=-1, keepdims=True), lp)
    lp_ref[...] = lp.astype(I32)

    lp_t = lp.T
    tg_t = tg_ref[...].T
    for c in range(STAGE_ROWS // LANES):
        rows = slice(c * LANES, (c + 1) * LANES)
        row = (lax.broadcasted_iota(I32, (LANES, TM), 0) + c * LANES).astype(F32)
        perm = jnp.zeros((LANES, TM), F32)
        gates = jnp.zeros((LANES, TM), F32)
        for k in range(TOP_K):
            hit = row == lp_t[k:k + 1, :]
            perm = jnp.where(hit, 1.0, perm)
            gates = jnp.where(hit, tg_t[k:k + 1, :], gates)
        perm_s[rows, :] = perm.astype(BF16)
        g_hi = gates.astype(BF16)
        ghi_s[rows, :] = g_hi
        glo_s[rows, :] = (gates - g_hi.astype(F32)).astype(BF16)
    stage[slot, :, 0:D_MODEL] = _dot(perm_s[...], h2_ref[...].astype(BF16))
    ones = jnp.ones((TM, LANES), BF16)
    stage[slot, :, D_MODEL:XS_COLS] = _dot(ghi_s[...], ones) + _dot(glo_s[...], ones)

    _run_copies(cnt_ref, off_ref, run_ref, i * N_EXPERTS, N_EXPERTS, RUN_BITS, copy_out(slot), True)

    @pl.when(i == N_TILES - 1)
    def _():
        _run_copies(cnt_ref, off_ref, run_ref, (i - 1) * N_EXPERTS, N_EXPERTS, RUN_BITS, copy_out(1 - slot), False)
        _run_copies(cnt_ref, off_ref, run_ref, i * N_EXPERTS, N_EXPERTS, RUN_BITS, copy_out(slot), False)


def _dispatch(rt, h2, top_i, top_g):
    row = lambda n: pl.BlockSpec((TM, n), lambda i, *_: (i, 0))
    return pl.pallas_call(
        _dispatch_kernel,
        grid_spec=pltpu.PrefetchScalarGridSpec(
            num_scalar_prefetch=6,
            grid=(N_TILES,),
            in_specs=[row(D_MODEL), row(LANES), row(LANES),
                      pl.BlockSpec((1, SUBLANES, LANES), lambda i, *_: (i, 0, 0))],
            out_specs=(pl.BlockSpec(memory_space=pl.ANY), row(LANES)),
            scratch_shapes=[
                pltpu.VMEM((2, STAGE_ROWS, XS_COLS), F32),
                pltpu.VMEM((1 << TAIL_BITS[0], XS_COLS), F32),
                pltpu.VMEM((STAGE_ROWS, TM), BF16), pltpu.VMEM((STAGE_ROWS, TM), BF16),
                pltpu.VMEM((STAGE_ROWS, TM), BF16),
                pltpu.SemaphoreType.DMA((2,)),
                pltpu.SemaphoreType.DMA,
            ],
        ),
        out_shape=(jax.ShapeDtypeStruct((N_SLOTS, XS_COLS), F32), jax.ShapeDtypeStruct((N_TOK, LANES), I32)),
        compiler_params=pltpu.CompilerParams(dimension_semantics=("arbitrary",), vmem_limit_bytes=VMEM_LIMIT),
        name="dispatch",
    )(rt["cnt"], rt["off"], rt["run"], rt["tail_cnt"], rt["tail_dst"], rt["n_used"], h2, top_i, top_g, rt["off_v"])


def _moe_kernel(layer, be_ref, nxt_ref, nu_ref, x_ref, wgu_hbm, wdn_hbm, bg_ref, bu_ref, bd_ref, sel_ref, y_ref,
                wgu_ref, wdn_ref, wgu_s, wdn_s, sem):
    i = pl.program_id(0)
    n_used = nu_ref[0]

    def fetch(e):
        return (pltpu.make_async_copy(wgu_hbm.at[layer, e], wgu_ref, sem.at[0]),
                pltpu.make_async_copy(wdn_hbm.at[layer, e], wdn_ref, sem.at[1]))

    @pl.when(i < n_used)
    def _():
        @pl.when(jnp.logical_or(i == 0, be_ref[i] != be_ref[jnp.maximum(i - 1, 0)]))
        def _():
            @pl.when(i == 0)
            def _():
                for cp in fetch(be_ref[0]):
                    cp.start()
            for cp in fetch(be_ref[i]):
                cp.wait()
            half = LANES
            for c in range(2 * EXPERT_FF // (2 * half)):
                r = _dot(wgu_ref[:, c * 2 * half:(c + 1) * 2 * half].astype(BF16), sel_ref[...])
                wgu_s[:, c * half:(c + 1) * half] = r[:, 0:half].astype(BF16)
                wgu_s[:, EXPERT_FF + c * half:EXPERT_FF + (c + 1) * half] = r[:, half:2 * half].astype(BF16)
            wdn_s[...] = wdn_ref[...].astype(BF16)
            nxt = nxt_ref[i]

            @pl.when(nxt >= 0)
            def _():
                for cp in fetch(nxt):
                    cp.start()

        gu = _dot(x_ref[:, 0:D_MODEL].astype(BF16), wgu_s[...])
        gate = jnp.minimum(gu[:, 0:EXPERT_FF] + bg_ref[0], SWIGLU_LIMIT)
        up = jnp.clip(gu[:, EXPERT_FF:2 * EXPERT_FF] + bu_ref[0], -SWIGLU_LIMIT, SWIGLU_LIMIT)
        act = gate * jax.nn.sigmoid(SWIGLU_ALPHA * gate) * (up + 1.0)
        y = _dot(act.astype(BF16), wdn_s[...]) + bd_ref[0]
        slot_gate = x_ref[:, D_MODEL:XS_COLS]
        for j in range(D_MODEL // LANES):
            y_ref[:, j * LANES:(j + 1) * LANES] = y[:, j * LANES:(j + 1) * LANES] * slot_gate

    @pl.when(i >= n_used)
    def _():
        y_ref[...] = jnp.zeros_like(y_ref)


def _moe(layer, rt, xs, w_gu, w_down, bg, bu, bd, sel):
    bias = pl.BlockSpec((1, 1, D_MODEL), lambda i, be, nx, nu: (be[i], 0, 0))
    return pl.pallas_call(
        functools.partial(_moe_kernel, layer),
        grid_spec=pltpu.PrefetchScalarGridSpec(
            num_scalar_prefetch=3,
            grid=(N_BLOCKS,),
            in_specs=[
                pl.BlockSpec((MOE_BLK, XS_COLS), lambda i, be, nx, nu: (jnp.minimum(i, nu[0] - 1), 0)),
                pl.BlockSpec(memory_space=pl.ANY),
                pl.BlockSpec(memory_space=pl.ANY),
                bias, bias, bias,
                pl.BlockSpec((2 * LANES, 2 * LANES), lambda i, be, nx, nu: (0, 0)),
            ],
            out_specs=pl.BlockSpec((MOE_BLK, D_MODEL), lambda i, be, nx, nu: (i, 0)),
            scratch_shapes=[
                pltpu.VMEM((D_MODEL, 2 * EXPERT_FF), F32),
                pltpu.VMEM((EXPERT_FF, D_MODEL), F32),
                pltpu.VMEM((D_MODEL, 2 * EXPERT_FF), BF16),
                pltpu.VMEM((EXPERT_FF, D_MODEL), BF16),
                pltpu.SemaphoreType.DMA((2,)),
            ],
        ),
        out_shape=jax.ShapeDtypeStruct((N_SLOTS, D_MODEL), F32),
        compiler_params=pltpu.CompilerParams(dimension_semantics=("arbitrary",), vmem_limit_bytes=VMEM_LIMIT),
        name="moe",
    )(rt["blk_e"], rt["blk_next"], rt["n_used"], xs, w_gu, w_down, bg, bu, bd, sel)


def _combine_kernel(cnt_ref, off_ref, run_ref, y_ref, lp_ref, x1_ref, mod_ref, fw_ref, x2_ref, yn_ref,
                    stage, mine_s, sem):
    i = pl.program_id(0)
    slot = i % 2

    def copy_in(s_):
        return lambda s, d, n: pltpu.make_async_copy(y_ref.at[pl.ds(d, n), :], stage.at[s_, pl.ds(s, n), :],
                                                     sem.at[s_])

    @pl.when(i == 0)
    def _():
        stage[...] = jnp.zeros_like(stage)
        _run_copies(cnt_ref, off_ref, run_ref, 0, N_EXPERTS, RUN_BITS, copy_in(0), True)

    @pl.when(i + 1 < N_TILES)
    def _():
        _run_copies(cnt_ref, off_ref, run_ref, (i + 1) * N_EXPERTS, N_EXPERTS, RUN_BITS, copy_in(1 - slot), True)

    _run_copies(cnt_ref, off_ref, run_ref, i * N_EXPERTS, N_EXPERTS, RUN_BITS, copy_in(slot), False)
    lane = lax.broadcasted_iota(I32, (TM, LANES), 1)
    mine = [jnp.broadcast_to(lp_ref[:, k:k + 1], (TM, LANES)) for k in range(TOP_K)]
    for c in range(STAGE_ROWS // LANES):
        hit = jnp.zeros((TM, LANES), F32)
        for k in range(TOP_K):
            hit = jnp.where(lane + c * LANES == mine[k], 1.0, hit)
        mine_s[:, c * LANES:(c + 1) * LANES] = hit.astype(BF16)
    moe = _dot(mine_s[...], stage[slot].astype(BF16))
    gate2 = mod_ref[0][:, 5 * D_MODEL:6 * D_MODEL]
    x2 = x1_ref[...] + gate2 * moe
    x2_ref[...] = x2
    yn_ref[...] = _rms(x2, fw_ref[...])


def _combine(rt, y_slots, lp, x1, mod3, final_w):
    row = lambda n: pl.BlockSpec((TM, n), lambda i, *_: (i, 0))
    return pl.pallas_call(
        _combine_kernel,
        grid_spec=pltpu.PrefetchScalarGridSpec(
            num_scalar_prefetch=3,
            grid=(N_TILES,),
            in_specs=[
                pl.BlockSpec(memory_space=pl.ANY),
                row(LANES), row(D_MODEL),
                pl.BlockSpec((1, 1, 6 * D_MODEL), lambda i, *_: (_mod_row(i), 0, 0)),
                pl.BlockSpec((1, D_MODEL), lambda i, *_: (0, 0)),
            ],
            out_specs=(row(D_MODEL), row(D_MODEL)),
            scratch_shapes=[pltpu.VMEM((2, STAGE_ROWS, D_MODEL), F32), pltpu.VMEM((TM, STAGE_ROWS), BF16),
                            pltpu.SemaphoreType.DMA((2,))],
        ),
        out_shape=(jax.ShapeDtypeStruct((N_TOK, D_MODEL), F32), jax.ShapeDtypeStruct((N_TOK, D_MODEL), F32)),
        compiler_params=pltpu.CompilerParams(dimension_semantics=("arbitrary",), vmem_limit_bytes=VMEM_LIMIT),
        name="combine",
    )(rt["cnt"], rt["off"], rt["run"], y_slots, lp, x1, mod3, final_w)


def _rope_tables():
    rows = DEC_SEQ // GRID_W
    t = jnp.arange(rows * GRID_W)
    row = (t // GRID_W).astype(F32)
    col = (t % GRID_W).astype(F32)
    half = QK_ROPE // 2
    inv = ROPE_BASE ** (-jnp.arange(0, half, 2, dtype=F32) / half)
    ang_r, ang_c = row[:, None] * inv, col[:, None] * inv
    cos32 = jnp.concatenate([jnp.cos(ang_r), jnp.cos(ang_r), jnp.cos(ang_c), jnp.cos(ang_c)], axis=-1)
    sin32 = jnp.concatenate([-jnp.sin(ang_r), jnp.sin(ang_r), -jnp.sin(ang_c), jnp.sin(ang_c)], axis=-1)
    cos32 = jnp.concatenate([jnp.ones((TM, QK_ROPE), F32), cos32], axis=0)
    sin32 = jnp.concatenate([jnp.zeros((TM, QK_ROPE), F32), sin32], axis=0)
    n = cos32.shape[0]
    pad = HEAD_PAD - QK_NOPE - QK_ROPE
    cos_h = jnp.concatenate([jnp.ones((n, QK_NOPE), F32), cos32, jnp.zeros((n, pad), F32)], axis=-1)
    sin_h = jnp.concatenate([jnp.zeros((n, QK_NOPE), F32), sin32, jnp.zeros((n, pad), F32)], axis=-1)
    zeros = jnp.zeros((n, LANES - QK_ROPE), F32)
    j = jnp.arange(QK_ROPE)
    tile = jnp.zeros((LANES, MLA_HEADS, HEAD_PAD), F32).at[j, :, QK_NOPE + j].set(1.0)
    return {
        "cosq": jnp.tile(cos_h, (1, MLA_HEADS)), "sinq": jnp.tile(sin_h, (1, MLA_HEADS)),
        "cosk": jnp.concatenate([cos32, zeros], axis=-1), "sink": jnp.concatenate([sin32, zeros], axis=-1),
        "tile": tile.reshape(LANES, MLA_HEADS * HEAD_PAD).astype(BF16),
    }


def _rot_partner():
    i = jnp.arange(QK_ROPE)
    quarter = QK_ROPE // 4
    return (i // (2 * quarter)) * (2 * quarter) + (i % (2 * quarter) + quarter) % (2 * quarter)


def _layer_weights(p, l):
    w_in = p["w_in"][l]
    sizes = (CONV_DIM, CONV_DIM, CONV_DIM, Q_RANK, KV_RANK, QK_ROPE, SSM_INNER, SSM_CONV_CH, SSM_HEADS, SSM_HEADS,
             D_MODEL, D_MODEL, D_MODEL)
    offs = [0]
    for s in sizes:
        offs.append(offs[-1] + s)
    seg = lambda k: w_in[:, offs[k]:offs[k + 1]]
    kpe = seg(5)
    small = jnp.concatenate([kpe, seg(8), seg(9), kpe[:, _rot_partner()],
                             jnp.zeros((D_MODEL, LANES - SM_KPE_ROT - QK_ROPE), F32)], axis=1)
    w_in2 = jnp.concatenate([seg(0), seg(1), seg(2), seg(3), seg(4), seg(6), seg(7), seg(10), seg(11), seg(12),
                             small], axis=1).astype(BF16)
    hd = QK_NOPE + QK_ROPE
    pad = HEAD_PAD - hd
    wq = p["w_uq"][l].reshape(Q_RANK, MLA_HEADS, hd)
    wq_a = jnp.pad(wq, ((0, 0), (0, 0), (0, pad)))
    wq_rot = wq[:, :, QK_NOPE:][:, :, _rot_partner()]
    wq_b = jnp.pad(wq_rot, ((0, 0), (0, 0), (QK_NOPE, pad)))
    wuk = jnp.pad(p["w_uk"][l], ((0, 0), (0, 0), (0, HEAD_PAD - QK_NOPE)))
    rw = jnp.pad(p["router_w"][l], ((0, 0), (0, LANES - N_EXPERTS)))
    rb = jnp.concatenate([p["router_b"][l], jnp.full((LANES - N_EXPERTS,), -jnp.inf, F32)])
    return {
        "norm1_w": p["norm1_w"][l][None], "w_in2": w_in2, "q_norm_w": p["q_norm_w"][l][None],
        "wq_a": wq_a.reshape(Q_RANK, -1).astype(BF16), "wq_b": wq_b.reshape(Q_RANK, -1).astype(BF16),
        "kv_norm_w": p["kv_norm_w"][l][None],
        "wuk": wuk.reshape(KV_RANK, -1).astype(BF16), "wuv": p["w_uv"][l].reshape(KV_RANK, -1).astype(BF16),
        "conv_a_w": p["conv_a_w"][l], "ssm_conv_w": p["ssm_conv_w"][l], "ssm_conv_b": p["ssm_conv_b"][l][None],
        "w_out_a": p["w_out_a"][l].astype(BF16), "w_o_mla": p["w_o_mla"][l].astype(BF16),
        "d_skip": jnp.repeat(p["d_skip"][l], SSM_HEAD_DIM)[None], "ssm_norm_w": p["ssm_norm_w"][l][None],
        "w_o_ssm": p["w_o_ssm"][l].astype(BF16), "w_o": p["w_o"][l].astype(BF16),
        "norm2_w": p["norm2_w"][l][None], "router_w": rw.astype(BF16), "router_b": rb[None],
        "bg": p["b_gu"][l][:, None, 0::2], "bu": p["b_gu"][l][:, None, 1::2], "bd": p["b_down"][l][:, None, :],
    }


def _ssd_params(p, l):
    z = lambda n: jnp.zeros((n,), F32)
    lanes = lambda f, b: jnp.concatenate([z(SM_DTF), f, b, z(LANES - SM_DTB - SSM_HEADS)])
    ones = jnp.ones((SSM_HEADS,), F32)
    rows = [lanes(p["dt_bias_fwd"][l], p["dt_bias_bwd"][l]), lanes(p["a_log_fwd"][l], p["a_log_bwd"][l]),
            lanes(ones, ones)]
    return jnp.concatenate([jnp.stack(rows), jnp.zeros((SUBLANES - 3, LANES), F32)], axis=0)


def _routing(cnt_tiles):
    cnt = cnt_tiles[:, 0, 0:N_EXPERTS]
    cnt = (cnt + RUN_ALIGN - 1) // RUN_ALIGN * RUN_ALIGN
    per_expert = jnp.sum(cnt, axis=0)
    padded = (per_expert + MOE_BLK - 1) // MOE_BLK * MOE_BLK
    pad_end = jnp.cumsum(padded)
    pad_start = pad_end - padded
    run = pad_start[None, :] + jnp.cumsum(cnt, axis=0) - cnt
    off = jnp.cumsum(cnt, axis=1) - cnt
    starts = jnp.arange(N_BLOCKS, dtype=I32) * MOE_BLK
    blk_e = jnp.minimum(jnp.sum((pad_end[None, :] <= starts[:, None]).astype(I32), axis=1), N_EXPERTS - 1)
    off_v = jnp.zeros((N_TILES, SUBLANES, LANES), F32).at[:, 0, 0:N_EXPERTS].set(off.astype(F32))
    ids = jnp.arange(N_EXPERTS, dtype=I32)
    later = jnp.logical_and(ids[None, :] > ids[:, None], padded[None, :] > 0)
    nxt = jnp.min(jnp.where(later, ids[None, :], N_EXPERTS), axis=1)
    nxt = jnp.where(nxt == N_EXPERTS, -1, nxt)
    return {
        "blk_next": jnp.sum(jnp.where(blk_e[:, None] == ids[None, :], nxt[None, :], 0), axis=1).astype(I32),
        "cnt": cnt.reshape(-1).astype(I32), "off": off.reshape(-1).astype(I32), "run": run.reshape(-1).astype(I32),
        "tail_cnt": (padded - per_expert).astype(I32), "tail_dst": (pad_start + per_expert).astype(I32),
        "blk_e": blk_e.astype(I32), "n_used": (pad_end[-1] // MOE_BLK).astype(I32).reshape(1), "off_v": off_v,
    }


def _deinterleave_matrix():
    k = jnp.arange(2 * LANES)[:, None]
    n = jnp.arange(2 * LANES)[None, :]
    src = jnp.where(n < LANES, 2 * n, 2 * (n - LANES) + 1)
    return (k == src).astype(BF16)


def kernel(x_prompt, x_sample, cache_ckv, cache_kpe, state_ssm_fwd, state_ssm_bwd, c, c_ctx, w_ada, b_ada, norm1_w, w_in, conv_a_w, w_out_a, q_norm_w, w_uq, kv_norm_w, w_uk, w_uv, w_o_mla, ssm_conv_w, ssm_conv_b, dt_bias_fwd, dt_bias_bwd, a_log_fwd, a_log_bwd, d_skip, ssm_norm_w, w_o_ssm, w_o, norm2_w, router_w, router_b, w_gu, b_gu, w_down, b_down, final_norm_w):
    p = dict(norm1_w=norm1_w, w_in=w_in, conv_a_w=conv_a_w, w_out_a=w_out_a, q_norm_w=q_norm_w, w_uq=w_uq,
             kv_norm_w=kv_norm_w, w_uk=w_uk, w_uv=w_uv, w_o_mla=w_o_mla, ssm_conv_w=ssm_conv_w,
             ssm_conv_b=ssm_conv_b, dt_bias_fwd=dt_bias_fwd, dt_bias_bwd=dt_bias_bwd, a_log_fwd=a_log_fwd,
             a_log_bwd=a_log_bwd, d_skip=d_skip, ssm_norm_w=ssm_norm_w, w_o_ssm=w_o_ssm, w_o=w_o,
             norm2_w=norm2_w, router_w=router_w, router_b=router_b, b_gu=b_gu, b_down=b_down)
    x = jnp.concatenate([x_prompt.reshape(N_CTX, D_MODEL), x_sample.reshape(N_LAT, D_MODEL)], axis=0)
    cond8 = jnp.concatenate([c_ctx[None], c, jnp.zeros((SUBLANES - 1 - DEC_BATCH, D_MODEL), F32)], axis=0)
    mods = _ada_mods(cond8, w_ada, b_ada)
    tabs = _rope_tables()
    sel = _deinterleave_matrix()
    final_w = final_norm_w[None]
    hp = SSM_INNER
    zeros_state = jnp.zeros((BATCH, hp, SSM_STATE), F32)

    ckv_out, kpe_out, sf_out, sb_out = [], [], [], []
    y_norm = None
    for l in range(DEPTH):
        lw = _layer_weights(p, l)
        mod3 = mods[l].reshape(SUBLANES, 1, 6 * D_MODEL)
        a2, q, ckv, kf, v, small, z, xbc, g = _inproj(x, mod3, lw, tabs)
        ta, xact = _convs(a2, xbc, lw)
        kpe_c = jnp.pad(cache_kpe[:, l].reshape(DEC_BATCH * PAST_LEN, QK_ROPE), ((0, 0), (0, LANES - QK_ROPE)))
        kf_c, v_c = _kvcache(cache_ckv[:, l].reshape(DEC_BATCH * PAST_LEN, KV_RANK), kpe_c, lw, tabs)
        att = _attention(q, kf, v, kf_c, v_c)
        init_f = jnp.concatenate([zeros_state, state_ssm_fwd[:, l].reshape(DEC_BATCH, hp, SSM_STATE)], axis=0)
        init_b = jnp.concatenate([zeros_state, state_ssm_bwd[:, l].reshape(DEC_BATCH, hp, SSM_STATE)], axis=0)
        yf, yb, sf, sb = _ssd(xact, small, init_f, init_b, _ssd_params(p, l))
        x1, h2, top_i, top_g, cnt_tiles = _merge(ta, att, yf, yb, xact, z, g, x, mod3, lw)
        rt = _routing(cnt_tiles)
        xs, lp = _dispatch(rt, h2, top_i, top_g)
        y_slots = _moe(l, rt, xs, w_gu, w_down, lw["bg"], lw["bu"], lw["bd"], sel)
        x, y_norm = _combine(rt, y_slots, lp, x1, mod3, final_w)
        ckv_out.append(ckv[:N_CTX].reshape(BATCH, SEQ, KV_RANK))
        kpe_out.append(small[:N_CTX, 0:QK_ROPE].reshape(BATCH, SEQ, QK_ROPE))
        sf_out.append(sf[:BATCH].reshape(BATCH, SSM_HEADS, SSM_HEAD_DIM, SSM_STATE))
        sb_out.append(sb[:BATCH].reshape(BATCH, SSM_HEADS, SSM_HEAD_DIM, SSM_STATE))

    y_prompt = y_norm[:N_CTX].reshape(BATCH, SEQ, D_MODEL)
    y_sample = y_norm[N_CTX:].reshape(DEC_BATCH, DEC_SEQ, D_MODEL)
    return (y_prompt, y_sample, jnp.stack(ckv_out, axis=1), jnp.stack(kpe_out, axis=1),
            jnp.stack(sf_out, axis=1), jnp.stack(sb_out, axis=1))
```

```python
import functools
import math

import jax
import jax.numpy as jnp
from jax import lax
from jax.experimental import pallas as pl
from jax.experimental.pallas import tpu as pltpu

F32 = jnp.float32
BF16 = jnp.bfloat16
I32 = jnp.int32

D_MODEL = 1024
BATCH = 16
SEQ = 256
DEPTH = 2
DEC_BATCH = 2
DEC_SEQ = 2048
PAST_LEN = 512
GRID_W = 64
NORM_EPS = 1e-6
CONV_DIM = 512
MLA_HEADS = 8
Q_RANK = 384
KV_RANK = 256
QK_NOPE = 64
QK_ROPE = 32
V_HEAD = 64
ROPE_BASE = 10000.0
MLA_SCALE = (QK_NOPE + QK_ROPE) ** -0.5
SSM_HEADS = 16
SSM_HEAD_DIM = 64
SSM_INNER = SSM_HEADS * SSM_HEAD_DIM
SSM_GROUPS = 2
SSM_STATE = 128
SSM_CONV_CH = SSM_INNER + 2 * SSM_GROUPS * SSM_STATE
N_EXPERTS = 32
TOP_K = 4
EXPERT_FF = D_MODEL
SWIGLU_ALPHA = 1.702
SWIGLU_LIMIT = 7.0

N_CTX = BATCH * SEQ
N_LAT = DEC_BATCH * DEC_SEQ
N_TOK = N_CTX + N_LAT
N_SEQS = BATCH + DEC_BATCH

LANES = 128
SUBLANES = 8
HEAD_PAD = 128
TM = 256
N_TILES = N_TOK // TM
CTX_TILES = N_CTX // TM
LAT_TILES_PER_SEQ = DEC_SEQ // TM
CHUNK = 128
CTX_CHUNKS_PER_SEQ = SEQ // CHUNK
LAT_CHUNKS_PER_SEQ = DEC_SEQ // CHUNK
N_CTX_CHUNKS = N_CTX // CHUNK
N_CHUNKS = N_TOK // CHUNK
MOE_BLK = 256
N_ASSIGN = N_TOK * TOP_K
RUN_ALIGN = SUBLANES
RUN_BITS = tuple(range(8, 2, -1))
TAIL_BITS = tuple(range(7, 2, -1))
STAGE_ROWS = 1280
XS_COLS = D_MODEL + LANES
N_BLOCKS = -(-(N_ASSIGN + N_TILES * N_EXPERTS * (RUN_ALIGN - 1) + N_EXPERTS * (MOE_BLK - 1)) // MOE_BLK)
N_SLOTS = N_BLOCKS * MOE_BLK
VMEM_LIMIT = 56 * 1024 * 1024

C_A3 = 0
C_CQ = C_A3 + 3 * CONV_DIM
C_CKV = C_CQ + Q_RANK
C_Z = C_CKV + KV_RANK
C_XBC = C_Z + SSM_INNER
C_GATE = C_XBC + SSM_CONV_CH
C_SMALL = C_GATE + 3 * D_MODEL
IN_COLS2 = C_SMALL + LANES
SM_DTF = QK_ROPE
SM_DTB = QK_ROPE + SSM_HEADS
S_CQ = 3 * CONV_DIM
S_CKV = S_CQ + Q_RANK
S_KPE = S_CKV + KV_RANK
S_Z = S_KPE + QK_ROPE
S_XBC = S_Z + SSM_INNER
S_DTF = S_XBC + SSM_CONV_CH
S_GATE = S_DTF + 2 * SSM_HEADS
IN_COLS = S_GATE + 3 * D_MODEL
W_SEGMENTS = ((C_A3, 0, 3 * CONV_DIM), (C_CQ, S_CQ, Q_RANK), (C_CKV, S_CKV, KV_RANK), (C_Z, S_Z, SSM_INNER),
              (C_XBC, S_XBC, SSM_CONV_CH), (C_GATE, S_GATE, 3 * D_MODEL))
W_ROWS = 128
W_PIECE = 512


def _rms(x, w):
    return x * lax.rsqrt(jnp.mean(x * x, axis=-1, keepdims=True) + NORM_EPS) * w


def _silu(x):
    return x * jax.nn.sigmoid(x)


def _dot(a, b):
    return jnp.dot(a, b, preferred_element_type=F32)


def _dot_nt(a, b):
    return lax.dot_general(a, b, (((1,), (1,)), ((), ())), preferred_element_type=F32)


def _resident(shape):
    nd = len(shape)
    return pl.BlockSpec(shape, lambda *_: (0,) * nd, pipeline_mode=pl.Buffered(1))


def _mod_row(i):
    return jnp.where(i < CTX_TILES, 0, 1 + (i - CTX_TILES) // LAT_TILES_PER_SEQ)


def _pos_block(i):
    return jnp.where(i < CTX_TILES, 0, 1 + (i - CTX_TILES) % LAT_TILES_PER_SEQ)


def _ada_kernel(c_ref, w_ref, b_ref, o_ref):
    s = _silu(c_ref[...]).astype(BF16)
    o_ref[0] = _dot(s, w_ref[0].astype(BF16)) + b_ref[0]


def _ada_mods(cond8, w_ada, b_ada):
    tn = 1536
    n_mod = 6 * D_MODEL
    return pl.pallas_call(
        _ada_kernel,
        grid=(DEPTH, n_mod // tn),
        in_specs=[
            pl.BlockSpec((SUBLANES, D_MODEL), lambda l, j: (0, 0)),
            pl.BlockSpec((1, D_MODEL, tn), lambda l, j: (l, 0, j)),
            pl.BlockSpec((1, 1, tn), lambda l, j: (l, 0, j)),
        ],
        out_specs=pl.BlockSpec((1, SUBLANES, tn), lambda l, j: (l, 0, j)),
        out_shape=jax.ShapeDtypeStruct((DEPTH, SUBLANES, n_mod), F32),
        compiler_params=pltpu.CompilerParams(dimension_semantics=("arbitrary", "arbitrary")),
        name="ada_mods",
    )(cond8, w_ada, b_ada.reshape(DEPTH, 1, n_mod))


def _relayout_w_in(layer, w_hbm, w2, stg, sem):
    n_chunks = D_MODEL // W_ROWS
    fetch = lambda r: pltpu.make_async_copy(w_hbm.at[layer, pl.ds(r * W_ROWS, W_ROWS), :], stg.at[r % 2],
                                            sem.at[r % 2])
    lane = lax.broadcasted_iota(I32, (W_ROWS, LANES), 1)
    fetch(0).start()
    for r in range(n_chunks):
        if r + 1 < n_chunks:
            fetch(r + 1).start()
        fetch(r).wait()
        src_ref = stg.at[r % 2]
        rows = slice(r * W_ROWS, (r + 1) * W_ROWS)
        for dst, src, width in W_SEGMENTS:
            shift = src % LANES
            for p in range(0, width, W_PIECE):
                pw = min(W_PIECE, width - p)
                lo = src + p - shift
                window = src_ref[:, lo:lo + shift + pw]
                w2[rows, dst + p:dst + p + pw] = window[:, shift:shift + pw].astype(BF16)
        kpe_tile = src_ref[:, S_KPE:S_KPE + LANES]
        dt_tile = src_ref[:, S_DTF - SM_DTF:S_DTF - SM_DTF + LANES]
        small = jnp.where(lane < SM_DTF, kpe_tile, jnp.where(lane < SM_DTB + SSM_HEADS, dt_tile, 0.0))
        w2[rows, C_SMALL:IN_COLS2] = small.astype(BF16)


def _rot_partner(x):
    n = x.shape[1]
    lane = lax.broadcasted_iota(I32, x.shape, 1)
    quarter = QK_ROPE // 4
    return jnp.where(lane % (2 * quarter) < quarter, pltpu.roll(x, n - quarter, 1), pltpu.roll(x, quarter, 1))


def _inproj_kernel(layer, x_ref, xp_ref, xn_ref, mod_ref, n1w_ref, w_hbm, qnw_ref, wq_ref, kvnw_ref, wuk_ref,
                   wuv_ref, tile_ref, cosq_ref, sinq_ref, cosk_ref, sink_ref, caw_ref, cw_ref, cb_ref,
                   ta_ref, q_ref, ckv_ref, kf_ref, v_ref, small_ref, z_ref, xact_ref, g_ref, w2, stg, sem):
    i = pl.program_id(0)

    @pl.when(i == 0)
    def _():
        _relayout_w_in(layer, w_hbm, w2, stg, sem)

    mod = mod_ref[0]
    shift1 = mod[:, 0:D_MODEL]
    scale1 = mod[:, D_MODEL:2 * D_MODEL]
    x_ext = jnp.concatenate([xp_ref[...], x_ref[...], xn_ref[...]], axis=0)
    h_ext = _rms(x_ext, n1w_ref[...]) * (1.0 + scale1) + shift1
    hb_ext = h_ext.astype(BF16)
    hb = h_ext[SUBLANES:SUBLANES + TM].astype(BF16)

    j = (i - CTX_TILES) % LAT_TILES_PER_SEQ
    is_ctx = i < CTX_TILES
    keep_prev = jnp.where(jnp.logical_or(is_ctx, j == 0), 0.0, 1.0)
    keep_next = jnp.where(jnp.logical_or(is_ctx, j == LAT_TILES_PER_SEQ - 1), 0.0, 1.0)
    row = lax.broadcasted_iota(I32, (TM, 1), 0)
    prev_mask = jnp.where(row == 0, keep_prev, 1.0)
    next_mask = jnp.where(row == TM - 1, keep_next, 1.0)
    ext = TM + 2 * SUBLANES

    def conv3(x, w_ref):
        x_prev = pltpu.roll(x, 1, 0)[SUBLANES:SUBLANES + TM] * prev_mask
        x_next = pltpu.roll(x, ext - 1, 0)[SUBLANES:SUBLANES + TM] * next_mask
        return x_prev * w_ref[0:1, :] + x[SUBLANES:SUBLANES + TM] * w_ref[1:2, :] + x_next * w_ref[2:3, :]

    a3 = _dot(hb_ext, w2[:, C_A3:C_CQ])
    s = a3[:, 2 * CONV_DIM:3 * CONV_DIM] * a3[:, 0:CONV_DIM]
    ta_ref[...] = (a3[SUBLANES:SUBLANES + TM, CONV_DIM:2 * CONV_DIM] * conv3(s, caw_ref)).astype(BF16)
    xact_ref[...] = _silu(conv3(_dot(hb_ext, w2[:, C_XBC:C_GATE]), cw_ref) + cb_ref[...])

    def seg(a, b):
        return _dot(hb, w2[:, a:b])

    cqn = _rms(seg(C_CQ, C_CKV), qnw_ref[...]).astype(BF16)
    qa = _dot(cqn, wq_ref[...])
    q_ref[...] = (qa * cosq_ref[...] + _rot_partner(qa) * sinq_ref[...]).astype(BF16)

    ckv = _rms(seg(C_CKV, C_Z), kvnw_ref[...])
    ckv_ref[...] = ckv
    ckvb = ckv.astype(BF16)
    small = seg(C_SMALL, IN_COLS2)
    small_ref[...] = small
    kpe = small * cosk_ref[...] + _rot_partner(small) * sink_ref[...]
    kf_ref[...] = (_dot(ckvb, wuk_ref[...]) + _dot(kpe.astype(BF16), tile_ref[...])).astype(BF16)
    v_ref[...] = _dot(ckvb, wuv_ref[...]).astype(BF16)

    z_ref[...] = seg(C_Z, C_XBC)
    g_ref[...] = jax.nn.sigmoid(seg(C_GATE, C_SMALL))


def _inproj(layer, x, mod3, w_in, lw, tabs):
    row = lambda n: pl.BlockSpec((TM, n), lambda i: (i, 0))
    tab = lambda n: pl.BlockSpec((TM, n), lambda i: (_pos_block(i), 0))
    per = TM // SUBLANES
    last = N_TOK // SUBLANES - 1
    qw = MLA_HEADS * HEAD_PAD
    vw = MLA_HEADS * V_HEAD
    out_shape = (
        jax.ShapeDtypeStruct((N_TOK, CONV_DIM), BF16),
        jax.ShapeDtypeStruct((N_TOK, qw), BF16),
        jax.ShapeDtypeStruct((N_TOK, KV_RANK), F32),
        jax.ShapeDtypeStruct((N_TOK, qw), BF16),
        jax.ShapeDtypeStruct((N_TOK, vw), BF16),
        jax.ShapeDtypeStruct((N_TOK, LANES), F32),
        jax.ShapeDtypeStruct((N_TOK, SSM_INNER), F32),
        jax.ShapeDtypeStruct((N_TOK, SSM_CONV_CH), F32),
        jax.ShapeDtypeStruct((N_TOK, 3 * D_MODEL), F32),
    )
    return pl.pallas_call(
        functools.partial(_inproj_kernel, layer),
        grid=(N_TILES,),
        in_specs=[
            row(D_MODEL),
            pl.BlockSpec((SUBLANES, D_MODEL), lambda i: (jnp.maximum(i * per - 1, 0), 0)),
            pl.BlockSpec((SUBLANES, D_MODEL), lambda i: (jnp.minimum((i + 1) * per, last), 0)),
            pl.BlockSpec((1, 1, 6 * D_MODEL), lambda i: (_mod_row(i), 0, 0)),
            _resident((1, D_MODEL)),
            pl.BlockSpec(memory_space=pl.ANY),
            _resident((1, Q_RANK)),
            _resident((Q_RANK, qw)),
            _resident((1, KV_RANK)),
            _resident((KV_RANK, qw)),
            _resident((KV_RANK, vw)),
            _resident((LANES, qw)),
            tab(qw), tab(qw), tab(LANES), tab(LANES),
            _resident((3, CONV_DIM)), _resident((3, SSM_CONV_CH)), _resident((1, SSM_CONV_CH)),
        ],
        out_specs=tuple(row(s.shape[1]) for s in out_shape),
        out_shape=out_shape,
        scratch_shapes=[pltpu.VMEM((D_MODEL, IN_COLS2), BF16), pltpu.VMEM((2, W_ROWS, IN_COLS), F32),
                        pltpu.SemaphoreType.DMA((2,))],
        compiler_params=pltpu.CompilerParams(dimension_semantics=("arbitrary",), vmem_limit_bytes=VMEM_LIMIT),
        name="inproj",
    )(x, x, x, mod3, lw["norm1_w"], w_in, lw["q_norm_w"], lw["wq_a"], lw["kv_norm_w"],
      lw["wuk"], lw["wuv"], tabs["tile"], tabs["cosq"], tabs["sinq"], tabs["cosk"], tabs["sink"],
      lw["conv_a_w"], lw["ssm_conv_w"], lw["ssm_conv_b"])


def _kvcache_kernel(ckv_ref, kpe_ref, wuk_ref, wuv_ref, tile_ref, kf_ref, v_ref):
    ckvb = ckv_ref[...].astype(BF16)
    kf_ref[...] = (_dot(ckvb, wuk_ref[...]) + _dot(kpe_ref[...].astype(BF16), tile_ref[...])).astype(BF16)
    v_ref[...] = _dot(ckvb, wuv_ref[...]).astype(BF16)


def _kvcache(ckv, kpe128, lw, tabs):
    n = ckv.shape[0]
    qw = MLA_HEADS * HEAD_PAD
    vw = MLA_HEADS * V_HEAD
    return pl.pallas_call(
        _kvcache_kernel,
        grid=(n // PAST_LEN,),
        in_specs=[
            pl.BlockSpec((PAST_LEN, KV_RANK), lambda i: (i, 0)),
            pl.BlockSpec((PAST_LEN, LANES), lambda i: (i, 0)),
            _resident((KV_RANK, qw)),
            _resident((KV_RANK, vw)),
            _resident((LANES, qw)),
        ],
        out_specs=(pl.BlockSpec((PAST_LEN, qw), lambda i: (i, 0)), pl.BlockSpec((PAST_LEN, vw), lambda i: (i, 0))),
        out_shape=(jax.ShapeDtypeStruct((n, qw), BF16), jax.ShapeDtypeStruct((n, vw), BF16)),
        compiler_params=pltpu.CompilerParams(dimension_semantics=("arbitrary",)),
        name="kvcache",
    )(ckv, kpe128, lw["wuk"], lw["wuv"], tabs["tile"])


def _attn_heads(q_ref, kv_refs, o_ref, acc_ref):
    for h in range(MLA_HEADS):
        qh = q_ref[:, h * HEAD_PAD:(h + 1) * HEAD_PAD]
        ss = [_dot_nt(qh, k_ref[:, h * HEAD_PAD:(h + 1) * HEAD_PAD]) * MLA_SCALE for k_ref, _ in kv_refs]
        m = functools.reduce(jnp.maximum, [jnp.max(s, axis=-1, keepdims=True) for s in ss])
        ps = [jnp.exp(s - m) for s in ss]
        l = functools.reduce(jnp.add, [jnp.sum(p, axis=-1, keepdims=True) for p in ps])
        o = functools.reduce(jnp.add, [_dot(p.astype(BF16), v_ref[:, h * V_HEAD:(h + 1) * V_HEAD])
                                       for p, (_, v_ref) in zip(ps, kv_refs)])
        acc_ref[:, h * V_HEAD:(h + 1) * V_HEAD] = o / l
    o_ref[...] = acc_ref[...].astype(BF16)


def _attn_ctx_kernel(q_ref, k_ref, v_ref, o_ref, acc_ref):
    _attn_heads(q_ref, [(k_ref, v_ref)], o_ref, acc_ref)


def _attn_lat_kernel(q_ref, k_ref, v_ref, kc_ref, vc_ref, o_ref, acc_ref):
    _attn_heads(q_ref, [(k_ref, v_ref), (kc_ref, vc_ref)], o_ref, acc_ref)


def _attention(q, kf, v, kf_c, v_c):
    qw = MLA_HEADS * HEAD_PAD
    vw = MLA_HEADS * V_HEAD
    att_ctx = pl.pallas_call(
        _attn_ctx_kernel,
        grid=(BATCH,),
        in_specs=[pl.BlockSpec((SEQ, qw), lambda b: (b, 0)), pl.BlockSpec((SEQ, qw), lambda b: (b, 0)),
                  pl.BlockSpec((SEQ, vw), lambda b: (b, 0))],
        out_specs=pl.BlockSpec((SEQ, vw), lambda b: (b, 0)),
        out_shape=jax.ShapeDtypeStruct((N_CTX, vw), BF16),
        scratch_shapes=[pltpu.VMEM((SEQ, vw), F32)],
        compiler_params=pltpu.CompilerParams(dimension_semantics=("arbitrary",)),
        name="attn_ctx",
    )(q, kf, v)
    lat0 = N_CTX // DEC_SEQ
    att_lat = pl.pallas_call(
        _attn_lat_kernel,
        grid=(DEC_BATCH, LAT_TILES_PER_SEQ),
        in_specs=[
            pl.BlockSpec((TM, qw), lambda b, t: (CTX_TILES + b * LAT_TILES_PER_SEQ + t, 0)),
            pl.BlockSpec((DEC_SEQ, qw), lambda b, t: (lat0 + b, 0)),
            pl.BlockSpec((DEC_SEQ, vw), lambda b, t: (lat0 + b, 0)),
            pl.BlockSpec((PAST_LEN, qw), lambda b, t: (b, 0)),
            pl.BlockSpec((PAST_LEN, vw), lambda b, t: (b, 0)),
        ],
        out_specs=pl.BlockSpec((TM, vw), lambda b, t: (b * LAT_TILES_PER_SEQ + t, 0)),
        out_shape=jax.ShapeDtypeStruct((N_LAT, vw), BF16),
        scratch_shapes=[pltpu.VMEM((TM, vw), F32)],
        compiler_params=pltpu.CompilerParams(dimension_semantics=("arbitrary", "arbitrary"),
                                             vmem_limit_bytes=VMEM_LIMIT),
        name="attn_lat",
    )(q, kf, v, kf_c, v_c)
    return jnp.concatenate([att_ctx, att_lat], axis=0)


def _seq_of_chunk(s):
    return jnp.where(s < N_CTX_CHUNKS, s // CTX_CHUNKS_PER_SEQ,
                     BATCH + (s - N_CTX_CHUNKS) // LAT_CHUNKS_PER_SEQ)


def _chunk_in_seq(s):
    return jnp.where(s < N_CTX_CHUNKS, s % CTX_CHUNKS_PER_SEQ, (s - N_CTX_CHUNKS) % LAT_CHUNKS_PER_SEQ)


def _chunks_in_seq(s):
    return jnp.where(s < N_CTX_CHUNKS, CTX_CHUNKS_PER_SEQ, LAT_CHUNKS_PER_SEQ)


def _mirror_chunk(s):
    return s + _chunks_in_seq(s) - 1 - 2 * _chunk_in_seq(s)


def _split3(a):
    a1 = a.astype(BF16)
    r1 = a - a1.astype(F32)
    a2 = r1.astype(BF16)
    a3 = (r1 - a2.astype(F32)).astype(BF16)
    return a1, a2, a3


def _ssd_direction(x_ref, sm_ref, par_ref, st_ref, y_ref, lane0, backward):
    ri = lax.broadcasted_iota(I32, (CHUNK, CHUNK), 0)
    ci = lax.broadcasted_iota(I32, (CHUNK, CHUNK), 1)
    tri = (ci >= ri) if backward else (ci <= ri)
    tri_b = jnp.where(tri, 1.0, 0.0).astype(BF16)
    tot_row = 0 if backward else CHUNK - 1

    dt = jax.nn.softplus(sm_ref[...] + par_ref[0:1, :])
    a = dt * (-jnp.exp(par_ref[1:2, :])) * par_ref[2:3, :]
    a1, a2, a3 = _split3(a)
    acs = _dot(tri_b, a1) + _dot(tri_b, a2) + _dot(tri_b, a3)
    acs_t = acs.T
    dt_t = dt.T
    first_head = lax.broadcasted_iota(I32, (1, LANES), 1) < SSM_HEAD_DIM

    def block_diag(pair):
        return jnp.concatenate([jnp.where(first_head, pair, 0.0), jnp.where(first_head, 0.0, pair)],
                               axis=0).astype(BF16)

    for g in range(SSM_GROUPS):
        b0 = SSM_INNER + g * SSM_STATE
        c0 = SSM_INNER + SSM_GROUPS * SSM_STATE + g * SSM_STATE
        bg = x_ref[:, b0:b0 + SSM_STATE]
        cg = x_ref[:, c0:c0 + SSM_STATE]
        cb = _dot_nt(cg.astype(BF16), bg.astype(BF16))
        bg_t = bg.T
        heads = SSM_HEADS // SSM_GROUPS
        for pr in range(heads // 2):
            h0 = g * heads + 2 * pr
            sl = slice(h0 * SSM_HEAD_DIM, (h0 + 2) * SSM_HEAD_DIM)
            x_bd = block_diag(x_ref[:, sl])
            st_old = st_ref[:, sl]
            within, carried, to_state, keep = [], [], [], []
            for h in (h0, h0 + 1):
                lane = lane0 + h
                col = jnp.broadcast_to(acs[:, lane:lane + 1], (CHUNK, CHUNK))
                row = acs_t[lane:lane + 1, :]
                dt_row = dt_t[lane:lane + 1, :]
                decay = jnp.exp(jnp.where(tri, col - row, -jnp.inf))
                within.append((cb * decay * dt_row).astype(BF16))
                carried.append((cg * jnp.exp(col)).astype(BF16))
                tot = acs[tot_row:tot_row + 1, lane:lane + 1]
                to_state.append((bg_t * (dt_row * jnp.exp(tot - row))).astype(BF16))
                keep.append(jnp.exp(tot))
            lhs = jnp.concatenate(within + carried, axis=1)
            y_ref[:, sl] = _dot(lhs, jnp.concatenate([x_bd, block_diag(st_old)], axis=0))
            st_ref[:, sl] = (st_old * jnp.where(first_head, keep[0], keep[1])
                             + _dot(jnp.concatenate(to_state, axis=1), x_bd))


def _ssd_kernel(xf_ref, xb_ref, smf_ref, smb_ref, if_ref, ib_ref, par_ref,
                yf_ref, yb_ref, sf_ref, sb_ref, stf_ref, stb_ref):
    s = pl.program_id(0)
    c = _chunk_in_seq(s)

    @pl.when(c == 0)
    def _():
        stf_ref[...] = if_ref[0].T
        stb_ref[...] = ib_ref[0].T

    _ssd_direction(xf_ref, smf_ref, par_ref, stf_ref, yf_ref, SM_DTF, False)
    _ssd_direction(xb_ref, smb_ref, par_ref, stb_ref, yb_ref, SM_DTB, True)

    @pl.when(c == _chunks_in_seq(s) - 1)
    def _():
        sf_ref[0] = stf_ref[...].T
        sb_ref[0] = stb_ref[...].T


def _ssd(xact, small, init_f, init_b, par):
    hp = SSM_INNER
    fwd = lambda n: pl.BlockSpec((CHUNK, n), lambda s: (s, 0))
    bwd = lambda n: pl.BlockSpec((CHUNK, n), lambda s: (_mirror_chunk(s), 0))
    st = pl.BlockSpec((1, hp, SSM_STATE), lambda s: (_seq_of_chunk(s), 0, 0))
    return pl.pallas_call(
        _ssd_kernel,
        grid=(N_CHUNKS,),
        in_specs=[fwd(SSM_CONV_CH), bwd(SSM_CONV_CH), fwd(LANES), bwd(LANES), st, st,
                  pl.BlockSpec((SUBLANES, LANES), lambda s: (0, 0))],
        out_specs=(fwd(hp), bwd(hp), st, st),
        out_shape=(jax.ShapeDtypeStruct((N_TOK, hp), F32), jax.ShapeDtypeStruct((N_TOK, hp), F32),
                   jax.ShapeDtypeStruct((N_SEQS, hp, SSM_STATE), F32),
                   jax.ShapeDtypeStruct((N_SEQS, hp, SSM_STATE), F32)),
        scratch_shapes=[pltpu.VMEM((SSM_STATE, hp), F32), pltpu.VMEM((SSM_STATE, hp), F32)],
        compiler_params=pltpu.CompilerParams(dimension_semantics=("arbitrary",)),
        name="ssd",
    )(xact, xact, small, small, init_f, init_b, par)


def _merge_kernel(ta_ref, att_ref, yf_ref, yb_ref, xs_ref, z_ref, g_ref, x_ref, mod_ref,
                  woa_ref, wom_ref, dsk_ref, snw_ref, wos_ref, wo_ref, n2w_ref, rw_ref, rb_ref,
                  x1_ref, h2_ref, ti_ref, tg_ref, cnt_ref):
    mod = mod_ref[0]
    gate1 = mod[:, 2 * D_MODEL:3 * D_MODEL]
    shift2 = mod[:, 3 * D_MODEL:4 * D_MODEL]
    scale2 = mod[:, 4 * D_MODEL:5 * D_MODEL]
    y_a = _dot(ta_ref[...], woa_ref[...])
    y_b = _dot(att_ref[...], wom_ref[...])
    y_ssm = (yf_ref[...] + yb_ref[...] + dsk_ref[...] * xs_ref[...]) * _silu(z_ref[...])
    y_c = _dot(_rms(y_ssm, snw_ref[...]).astype(BF16), wos_ref[...])
    merged = (g_ref[:, 0:D_MODEL] * y_a + g_ref[:, D_MODEL:2 * D_MODEL] * y_b
              + g_ref[:, 2 * D_MODEL:3 * D_MODEL] * y_c)
    x1 = x_ref[...] + gate1 * _dot(merged.astype(BF16), wo_ref[...])
    x1_ref[...] = x1
    h2 = _rms(x1, n2w_ref[...]) * (1.0 + scale2) + shift2
    h2_ref[...] = h2

    logits = _dot(h2.astype(BF16), rw_ref[...]) + rb_ref[...]
    lane = lax.broadcasted_iota(I32, (TM, LANES), 1)
    ti = jnp.zeros((TM, LANES), I32)
    tv = jnp.full((TM, LANES), -jnp.inf, F32)
    chosen = jnp.zeros((TM, LANES), F32)
    for k in range(TOP_K):
        m = jnp.max(logits, axis=-1, keepdims=True)
        idx = jnp.min(jnp.where(logits == m, lane, LANES), axis=-1, keepdims=True)
        ti = jnp.where(lane == k, idx, ti)
        tv = jnp.where(lane == k, m, tv)
        chosen = jnp.where(lane == idx, 1.0, chosen)
        logits = jnp.where(lane == idx, -jnp.inf, logits)
    e = jnp.exp(tv - jnp.max(tv, axis=-1, keepdims=True))
    ti_ref[...] = ti
    tg_ref[...] = e / jnp.sum(e, axis=-1, keepdims=True)
    cnt_ref[0] = jnp.broadcast_to(jnp.sum(chosen, axis=0, keepdims=True), (SUBLANES, LANES)).astype(I32)


def _merge(ta, att, yf, yb, xact, z, g, x, mod3, lw):
    row = lambda n: pl.BlockSpec((TM, n), lambda i: (i, 0))
    out_shape = (jax.ShapeDtypeStruct((N_TOK, D_MODEL), F32), jax.ShapeDtypeStruct((N_TOK, D_MODEL), F32),
                 jax.ShapeDtypeStruct((N_TOK, LANES), I32), jax.ShapeDtypeStruct((N_TOK, LANES), F32))
    cnt_shape = jax.ShapeDtypeStruct((N_TILES, SUBLANES, LANES), I32)
    cnt_spec = pl.BlockSpec((1, SUBLANES, LANES), lambda i: (i, 0, 0))
    return pl.pallas_call(
        _merge_kernel,
        grid=(N_TILES,),
        in_specs=[
            row(CONV_DIM), row(MLA_HEADS * V_HEAD), row(SSM_INNER), row(SSM_INNER), row(SSM_INNER),
            row(SSM_INNER), row(3 * D_MODEL), row(D_MODEL),
            pl.BlockSpec((1, 1, 6 * D_MODEL), lambda i: (_mod_row(i), 0, 0)),
            _resident((CONV_DIM, D_MODEL)), _resident((MLA_HEADS * V_HEAD, D_MODEL)),
            _resident((1, SSM_INNER)), _resident((1, SSM_INNER)), _resident((SSM_INNER, D_MODEL)),
            _resident((D_MODEL, D_MODEL)), _resident((1, D_MODEL)),
            _resident((D_MODEL, LANES)), _resident((1, LANES)),
        ],
        out_specs=tuple(row(s.shape[1]) for s in out_shape) + (cnt_spec,),
        out_shape=out_shape + (cnt_shape,),
        compiler_params=pltpu.CompilerParams(dimension_semantics=("arbitrary",), vmem_limit_bytes=VMEM_LIMIT),
        name="merge",
    )(ta, att, yf, yb, xact, z, g, x, mod3, lw["w_out_a"], lw["w_o_mla"], lw["d_skip"], lw["ssm_norm_w"],
      lw["w_o_ssm"], lw["w_o"], lw["norm2_w"], lw["router_w"], lw["router_b"])


def _run_copies(cnt_ref, src_ref, dst_ref, first, count, bits, make_copy, start):
    def body(e, carry):
        n = cnt_ref[first + e]
        s0 = src_ref[first + e] if src_ref is not None else 0
        d0 = dst_ref[first + e]
        for b in bits:
            above = (n >> (b + 1)) << (b + 1)

            @pl.when(((n >> b) & 1) == 1)
            def _():
                cp = make_copy(pl.multiple_of(s0 + above, RUN_ALIGN), pl.multiple_of(d0 + above, RUN_ALIGN), 1 << b)
                if start:
                    cp.start()
                else:
                    cp.wait()
        return carry
    lax.fori_loop(0, count, body, 0)


def _dispatch_kernel(cnt_ref, off_ref, run_ref, tcnt_ref, tdst_ref, nu_ref, h2_ref, ti_ref, tg_ref, offv_ref,
                     xs_ref, lp_ref, stage, zeros, perm_s, ghi_s, glo_s, sem, semz):
    i = pl.program_id(0)
    slot = i % 2

    def copy_out(s_):
        return lambda s, d, n: pltpu.make_async_copy(stage.at[s_, pl.ds(s, n), :], xs_ref.at[pl.ds(d, n), :],
                                                     sem.at[s_])

    @pl.when(i == 0)
    def _():
        zeros[...] = jnp.zeros_like(zeros)
        zero_out = lambda s, d, n: pltpu.make_async_copy(zeros.at[pl.ds(0, n), :], xs_ref.at[pl.ds(d, n), :], semz)
        _run_copies(tcnt_ref, None, tdst_ref, 0, N_EXPERTS, TAIL_BITS, zero_out, True)
        _run_copies(tcnt_ref, None, tdst_ref, 0, N_EXPERTS, TAIL_BITS, zero_out, False)
        zrows = zeros.shape[0]

        def unused_blocks(start):
            def body(b, carry):
                for part in range(MOE_BLK // zrows):
                    cp = zero_out(0, pl.multiple_of(b * MOE_BLK + part * zrows, RUN_ALIGN), zrows)
                    if start:
                        cp.start()
                    else:
                        cp.wait()
                return carry
            lax.fori_loop(nu_ref[0], N_BLOCKS, body, 0)
        unused_blocks(True)
        unused_blocks(False)

    @pl.when(i >= 2)
    def _():
        _run_copies(cnt_ref, off_ref, run_ref, (i - 2) * N_EXPERTS, N_EXPERTS, RUN_BITS, copy_out(slot), False)

    lane = lax.broadcasted_iota(I32, (TM, LANES), 1)
    picks = [jnp.where(lane == ti_ref[:, k:k + 1], 1.0, 0.0) for k in range(TOP_K)]
    ri = lax.broadcasted_iota(I32, (TM, TM), 0)
    ci = lax.broadcasted_iota(I32, (TM, TM), 1)
    earlier = jnp.where(ci < ri, 1.0, 0.0).astype(BF16)
    base = _dot(earlier, functools.reduce(jnp.add, picks).astype(BF16)) + offv_ref[0][0:1, :]
    lp = jnp.full((TM, LANES), -1.0, F32)
    for k in range(TOP_K):
        lp = jnp.where(lane == k, jnp.sum(picks[k] * base, axis=-1, keepdims=True), lp)
    lp_ref[...] = lp.astype(I32)

    lp_t = lp.T
    tg_t = tg_ref[...].T
    for c in range(STAGE_ROWS // LANES):
        rows = slice(c * LANES, (c + 1) * LANES)
        row = (lax.broadcasted_iota(I32, (LANES, TM), 0) + c * LANES).astype(F32)
        perm = jnp.zeros((LANES, TM), F32)
        gates = jnp.zeros((LANES, TM), F32)
        for k in range(TOP_K):
            hit = row == lp_t[k:k + 1, :]
            perm = jnp.where(hit, 1.0, perm)
            gates = jnp.where(hit, tg_t[k:k + 1, :], gates)
        perm_s[rows, :] = perm.astype(BF16)
        g_hi = gates.astype(BF16)
        ghi_s[rows, :] = g_hi
        glo_s[rows, :] = (gates - g_hi.astype(F32)).astype(BF16)
    stage[slot, :, 0:D_MODEL] = _dot(perm_s[...], h2_ref[...].astype(BF16))
    ones = jnp.ones((TM, LANES), BF16)
    stage[slot, :, D_MODEL:XS_COLS] = _dot(ghi_s[...], ones) + _dot(glo_s[...], ones)

    _run_copies(cnt_ref, off_ref, run_ref, i * N_EXPERTS, N_EXPERTS, RUN_BITS, copy_out(slot), True)

    @pl.when(i == N_TILES - 1)
    def _():
        _run_copies(cnt_ref, off_ref, run_ref, (i - 1) * N_EXPERTS, N_EXPERTS, RUN_BITS, copy_out(1 - slot), False)
        _run_copies(cnt_ref, off_ref, run_ref, i * N_EXPERTS, N_EXPERTS, RUN_BITS, copy_out(slot), False)


def _dispatch(rt, h2, top_i, top_g):
    row = lambda n: pl.BlockSpec((TM, n), lambda i, *_: (i, 0))
    return pl.pallas_call(
        _dispatch_kernel,
        grid_spec=pltpu.PrefetchScalarGridSpec(
            num_scalar_prefetch=6,
            grid=(N_TILES,),
            in_specs=[row(D_MODEL), row(LANES), row(LANES),
                      pl.BlockSpec((1, SUBLANES, LANES), lambda i, *_: (i, 0, 0))],
            out_specs=(pl.BlockSpec(memory_space=pl.ANY), row(LANES)),
            scratch_shapes=[
                pltpu.VMEM((2, STAGE_ROWS, XS_COLS), F32),
                pltpu.VMEM((1 << TAIL_BITS[0], XS_COLS), F32),
                pltpu.VMEM((STAGE_ROWS, TM), BF16), pltpu.VMEM((STAGE_ROWS, TM), BF16),
                pltpu.VMEM((STAGE_ROWS, TM), BF16),
                pltpu.SemaphoreType.DMA((2,)),
                pltpu.SemaphoreType.DMA,
            ],
        ),
        out_shape=(jax.ShapeDtypeStruct((N_SLOTS, XS_COLS), F32), jax.ShapeDtypeStruct((N_TOK, LANES), I32)),
        compiler_params=pltpu.CompilerParams(dimension_semantics=("arbitrary",), vmem_limit_bytes=VMEM_LIMIT),
        name="dispatch",
    )(rt["cnt"], rt["off"], rt["run"], rt["tail_cnt"], rt["tail_dst"], rt["n_used"], h2, top_i, top_g, rt["off_v"])


def _moe_kernel(layer, be_ref, nxt_ref, nu_ref, x_ref, wgu_hbm, wdn_hbm, bg_ref, bu_ref, bd_ref, sel_ref, y_ref,
                wgu_ref, wdn_ref, wgu_s, wdn_s, sem):
    i = pl.program_id(0)
    n_used = nu_ref[0]

    def fetch(e):
        return (pltpu.make_async_copy(wgu_hbm.at[layer, e], wgu_ref, sem.at[0]),
                pltpu.make_async_copy(wdn_hbm.at[layer, e], wdn_ref, sem.at[1]))

    @pl.when(i < n_used)
    def _():
        @pl.when(jnp.logical_or(i == 0, be_ref[i] != be_ref[jnp.maximum(i - 1, 0)]))
        def _():
            @pl.when(i == 0)
            def _():
                for cp in fetch(be_ref[0]):
                    cp.start()
            for cp in fetch(be_ref[i]):
                cp.wait()
            half = LANES
            for c in range(2 * EXPERT_FF // (2 * half)):
                r = _dot(wgu_ref[:, c * 2 * half:(c + 1) * 2 * half].astype(BF16), sel_ref[...])
                wgu_s[:, c * half:(c + 1) * half] = r[:, 0:half].astype(BF16)
                wgu_s[:, EXPERT_FF + c * half:EXPERT_FF + (c + 1) * half] = r[:, half:2 * half].astype(BF16)
            wdn_s[...] = wdn_ref[...].astype(BF16)
            nxt = nxt_ref[i]

            @pl.when(nxt >= 0)
            def _():
                for cp in fetch(nxt):
                    cp.start()

        gu = _dot(x_ref[:, 0:D_MODEL].astype(BF16), wgu_s[...])
        gate = jnp.minimum(gu[:, 0:EXPERT_FF] + bg_ref[0], SWIGLU_LIMIT)
        up = jnp.clip(gu[:, EXPERT_FF:2 * EXPERT_FF] + bu_ref[0], -SWIGLU_LIMIT, SWIGLU_LIMIT)
        act = gate * jax.nn.sigmoid(SWIGLU_ALPHA * gate) * (up + 1.0)
        y = _dot(act.astype(BF16), wdn_s[...]) + bd_ref[0]
        slot_gate = x_ref[:, D_MODEL:XS_COLS]
        for j in range(D_MODEL // LANES):
            y_ref[:, j * LANES:(j + 1) * LANES] = y[:, j * LANES:(j + 1) * LANES] * slot_gate

    @pl.when(i >= n_used)
    def _():
        y_ref[...] = jnp.zeros_like(y_ref)


def _bias_split_kernel(b_ref, sel_ref, o_ref):
    for c in range(2 * EXPERT_FF // (2 * LANES)):
        terms = _split3(b_ref[:, c * 2 * LANES:(c + 1) * 2 * LANES])
        r = functools.reduce(jnp.add, [_dot(t, sel_ref[...]) for t in terms])
        o_ref[:, c * LANES:(c + 1) * LANES] = r[:, 0:LANES]
        o_ref[:, EXPERT_FF + c * LANES:EXPERT_FF + (c + 1) * LANES] = r[:, LANES:2 * LANES]


def _bias_split(b_gu, sel):
    n = DEPTH * N_EXPERTS
    out = pl.pallas_call(
        _bias_split_kernel,
        out_shape=jax.ShapeDtypeStruct((n, 2 * EXPERT_FF), F32),
        name="bias_split",
    )(b_gu.reshape(n, 2 * EXPERT_FF), sel)
    return out.reshape(n, 1, 2 * EXPERT_FF)


def _moe(layer, rt, xs, w_gu, w_down, b_gu_split, b_down, sel):
    first = layer * N_EXPERTS
    bg = pl.BlockSpec((1, 1, EXPERT_FF), lambda i, be, nx, nu: (first + be[i], 0, 0))
    bu = pl.BlockSpec((1, 1, EXPERT_FF), lambda i, be, nx, nu: (first + be[i], 0, 1))
    bd = pl.BlockSpec((1, 1, D_MODEL), lambda i, be, nx, nu: (first + be[i], 0, 0))
    return pl.pallas_call(
        functools.partial(_moe_kernel, layer),
        grid_spec=pltpu.PrefetchScalarGridSpec(
            num_scalar_prefetch=3,
            grid=(N_BLOCKS,),
            in_specs=[
                pl.BlockSpec((MOE_BLK, XS_COLS), lambda i, be, nx, nu: (jnp.minimum(i, nu[0] - 1), 0)),
                pl.BlockSpec(memory_space=pl.ANY),
                pl.BlockSpec(memory_space=pl.ANY),
                bg, bu, bd,
                pl.BlockSpec((2 * LANES, 2 * LANES), lambda i, be, nx, nu: (0, 0)),
            ],
            out_specs=pl.BlockSpec((MOE_BLK, D_MODEL), lambda i, be, nx, nu: (i, 0)),
            scratch_shapes=[
                pltpu.VMEM((D_MODEL, 2 * EXPERT_FF), F32),
                pltpu.VMEM((EXPERT_FF, D_MODEL), F32),
                pltpu.VMEM((D_MODEL, 2 * EXPERT_FF), BF16),
                pltpu.VMEM((EXPERT_FF, D_MODEL), BF16),
                pltpu.SemaphoreType.DMA((2,)),
            ],
        ),
        out_shape=jax.ShapeDtypeStruct((N_SLOTS, D_MODEL), F32),
        compiler_params=pltpu.CompilerParams(dimension_semantics=("arbitrary",), vmem_limit_bytes=VMEM_LIMIT),
        name="moe",
    )(rt["blk_e"], rt["blk_next"], rt["n_used"], xs, w_gu, w_down, b_gu_split, b_gu_split,
      b_down.reshape(DEPTH * N_EXPERTS, 1, D_MODEL), sel)


def _combine_kernel(cnt_ref, off_ref, run_ref, y_ref, lp_ref, x1_ref, mod_ref, fw_ref, x2_ref, yc_ref, yl_ref,
                    stage, mine_s, sem):
    i = pl.program_id(0)
    slot = i % 2

    def copy_in(s_):
        return lambda s, d, n: pltpu.make_async_copy(y_ref.at[pl.ds(d, n), :], stage.at[s_, pl.ds(s, n), :],
                                                     sem.at[s_])

    @pl.when(i == 0)
    def _():
        stage[...] = jnp.zeros_like(stage)
        _run_copies(cnt_ref, off_ref, run_ref, 0, N_EXPERTS, RUN_BITS, copy_in(0), True)

    @pl.when(i + 1 < N_TILES)
    def _():
        _run_copies(cnt_ref, off_ref, run_ref, (i + 1) * N_EXPERTS, N_EXPERTS, RUN_BITS, copy_in(1 - slot), True)

    _run_copies(cnt_ref, off_ref, run_ref, i * N_EXPERTS, N_EXPERTS, RUN_BITS, copy_in(slot), False)
    lane = lax.broadcasted_iota(I32, (TM, LANES), 1)
    mine = [jnp.broadcast_to(lp_ref[:, k:k + 1], (TM, LANES)) for k in range(TOP_K)]
    for c in range(STAGE_ROWS // LANES):
        hit = jnp.zeros((TM, LANES), F32)
        for k in range(TOP_K):
            hit = jnp.where(lane + c * LANES == mine[k], 1.0, hit)
        mine_s[:, c * LANES:(c + 1) * LANES] = hit.astype(BF16)
    moe = _dot(mine_s[...], stage[slot].astype(BF16))
    gate2 = mod_ref[0][:, 5 * D_MODEL:6 * D_MODEL]
    x2 = x1_ref[...] + gate2 * moe
    x2_ref[...] = x2
    y_norm = _rms(x2, fw_ref[...])

    @pl.when(i < CTX_TILES)
    def _():
        yc_ref[...] = y_norm

    @pl.when(i >= CTX_TILES)
    def _():
        yl_ref[...] = y_norm


def _combine(rt, y_slots, lp, x1, mod3, final_w):
    row = lambda n: pl.BlockSpec((TM, n), lambda i, *_: (i, 0))
    return pl.pallas_call(
        _combine_kernel,
        grid_spec=pltpu.PrefetchScalarGridSpec(
            num_scalar_prefetch=3,
            grid=(N_TILES,),
            in_specs=[
                pl.BlockSpec(memory_space=pl.ANY),
                row(LANES), row(D_MODEL),
                pl.BlockSpec((1, 1, 6 * D_MODEL), lambda i, *_: (_mod_row(i), 0, 0)),
                pl.BlockSpec((1, D_MODEL), lambda i, *_: (0, 0)),
            ],
            out_specs=(row(D_MODEL),
                       pl.BlockSpec((TM, D_MODEL), lambda i, *_: (jnp.minimum(i, CTX_TILES - 1), 0)),
                       pl.BlockSpec((TM, D_MODEL), lambda i, *_: (jnp.maximum(i - CTX_TILES, 0), 0))),
            scratch_shapes=[pltpu.VMEM((2, STAGE_ROWS, D_MODEL), F32), pltpu.VMEM((TM, STAGE_ROWS), BF16),
                            pltpu.SemaphoreType.DMA((2,))],
        ),
        out_shape=(jax.ShapeDtypeStruct((N_TOK, D_MODEL), F32), jax.ShapeDtypeStruct((N_CTX, D_MODEL), F32),
                   jax.ShapeDtypeStruct((N_LAT, D_MODEL), F32)),
        compiler_params=pltpu.CompilerParams(dimension_semantics=("arbitrary",), vmem_limit_bytes=VMEM_LIMIT),
        name="combine",
    )(rt["cnt"], rt["off"], rt["run"], y_slots, lp, x1, mod3, final_w)


def _rope_tables():
    rows = DEC_SEQ // GRID_W
    t = jnp.arange(rows * GRID_W)
    row = (t // GRID_W).astype(F32)
    col = (t % GRID_W).astype(F32)
    half = QK_ROPE // 2
    inv = ROPE_BASE ** (-jnp.arange(0, half, 2, dtype=F32) / half)
    ang_r, ang_c = row[:, None] * inv, col[:, None] * inv
    cos32 = jnp.concatenate([jnp.cos(ang_r), jnp.cos(ang_r), jnp.cos(ang_c), jnp.cos(ang_c)], axis=-1)
    sin32 = jnp.concatenate([-jnp.sin(ang_r), jnp.sin(ang_r), -jnp.sin(ang_c), jnp.sin(ang_c)], axis=-1)
    cos32 = jnp.concatenate([jnp.ones((TM, QK_ROPE), F32), cos32], axis=0)
    sin32 = jnp.concatenate([jnp.zeros((TM, QK_ROPE), F32), sin32], axis=0)
    n = cos32.shape[0]
    pad = HEAD_PAD - QK_NOPE - QK_ROPE
    cos_h = jnp.concatenate([jnp.ones((n, QK_NOPE), F32), cos32, jnp.zeros((n, pad), F32)], axis=-1)
    sin_h = jnp.concatenate([jnp.zeros((n, QK_NOPE), F32), sin32, jnp.zeros((n, pad), F32)], axis=-1)
    zeros = jnp.zeros((n, LANES - QK_ROPE), F32)
    j = jnp.arange(QK_ROPE)
    tile = jnp.zeros((LANES, MLA_HEADS, HEAD_PAD), F32).at[j, :, QK_NOPE + j].set(1.0)
    return {
        "cosq": jnp.tile(cos_h, (1, MLA_HEADS)), "sinq": jnp.tile(sin_h, (1, MLA_HEADS)),
        "cosk": jnp.concatenate([cos32, zeros], axis=-1), "sink": jnp.concatenate([sin32, zeros], axis=-1),
        "tile": tile.reshape(LANES, MLA_HEADS * HEAD_PAD).astype(BF16),
    }


def _layer_weights(p, l):
    hd = QK_NOPE + QK_ROPE
    pad = HEAD_PAD - hd
    wq = p["w_uq"][l].reshape(Q_RANK, MLA_HEADS, hd)
    wq_a = jnp.pad(wq, ((0, 0), (0, 0), (0, pad)))
    wuk = jnp.pad(p["w_uk"][l], ((0, 0), (0, 0), (0, HEAD_PAD - QK_NOPE)))
    rw = jnp.pad(p["router_w"][l], ((0, 0), (0, LANES - N_EXPERTS)))
    rb = jnp.concatenate([p["router_b"][l], jnp.full((LANES - N_EXPERTS,), -jnp.inf, F32)])
    return {
        "norm1_w": p["norm1_w"][l][None], "q_norm_w": p["q_norm_w"][l][None],
        "wq_a": wq_a.reshape(Q_RANK, -1).astype(BF16),
        "kv_norm_w": p["kv_norm_w"][l][None],
        "wuk": wuk.reshape(KV_RANK, -1).astype(BF16), "wuv": p["w_uv"][l].reshape(KV_RANK, -1).astype(BF16),
        "conv_a_w": p["conv_a_w"][l], "ssm_conv_w": p["ssm_conv_w"][l], "ssm_conv_b": p["ssm_conv_b"][l][None],
        "w_out_a": p["w_out_a"][l].astype(BF16), "w_o_mla": p["w_o_mla"][l].astype(BF16),
        "d_skip": jnp.repeat(p["d_skip"][l], SSM_HEAD_DIM)[None], "ssm_norm_w": p["ssm_norm_w"][l][None],
        "w_o_ssm": p["w_o_ssm"][l].astype(BF16), "w_o": p["w_o"][l].astype(BF16),
        "norm2_w": p["norm2_w"][l][None], "router_w": rw.astype(BF16), "router_b": rb[None],
    }


def _ssd_params(p, l):
    z = lambda n: jnp.zeros((n,), F32)
    lanes = lambda f, b: jnp.concatenate([z(SM_DTF), f, b, z(LANES - SM_DTB - SSM_HEADS)])
    ones = jnp.ones((SSM_HEADS,), F32)
    rows = [lanes(p["dt_bias_fwd"][l], p["dt_bias_bwd"][l]), lanes(p["a_log_fwd"][l], p["a_log_bwd"][l]),
            lanes(ones, ones)]
    return jnp.concatenate([jnp.stack(rows), jnp.zeros((SUBLANES - 3, LANES), F32)], axis=0)


def _routing(cnt_tiles):
    cnt = cnt_tiles[:, 0, 0:N_EXPERTS]
    cnt = (cnt + RUN_ALIGN - 1) // RUN_ALIGN * RUN_ALIGN
    per_expert = jnp.sum(cnt, axis=0)
    padded = (per_expert + MOE_BLK - 1) // MOE_BLK * MOE_BLK
    pad_end = jnp.cumsum(padded)
    pad_start = pad_end - padded
    run = pad_start[None, :] + jnp.cumsum(cnt, axis=0) - cnt
    off = jnp.cumsum(cnt, axis=1) - cnt
    starts = jnp.arange(N_BLOCKS, dtype=I32) * MOE_BLK
    blk_e = jnp.minimum(jnp.sum((pad_end[None, :] <= starts[:, None]).astype(I32), axis=1), N_EXPERTS - 1)
    off_v = jnp.zeros((N_TILES, SUBLANES, LANES), F32).at[:, 0, 0:N_EXPERTS].set(off.astype(F32))
    ids = jnp.arange(N_EXPERTS, dtype=I32)
    later = jnp.logical_and(ids[None, :] > ids[:, None], padded[None, :] > 0)
    nxt = jnp.min(jnp.where(later, ids[None, :], N_EXPERTS), axis=1)
    nxt = jnp.where(nxt == N_EXPERTS, -1, nxt)
    return {
        "blk_next": jnp.sum(jnp.where(blk_e[:, None] == ids[None, :], nxt[None, :], 0), axis=1).astype(I32),
        "cnt": cnt.reshape(-1).astype(I32), "off": off.reshape(-1).astype(I32), "run": run.reshape(-1).astype(I32),
        "tail_cnt": (padded - per_expert).astype(I32), "tail_dst": (pad_start + per_expert).astype(I32),
        "blk_e": blk_e.astype(I32), "n_used": (pad_end[-1] // MOE_BLK).astype(I32).reshape(1), "off_v": off_v,
    }


def _deinterleave_matrix():
    k = jnp.arange(2 * LANES)[:, None]
    n = jnp.arange(2 * LANES)[None, :]
    src = jnp.where(n < LANES, 2 * n, 2 * (n - LANES) + 1)
    return (k == src).astype(BF16)


def kernel(x_prompt, x_sample, cache_ckv, cache_kpe, state_ssm_fwd, state_ssm_bwd, c, c_ctx, w_ada, b_ada, norm1_w, w_in, conv_a_w, w_out_a, q_norm_w, w_uq, kv_norm_w, w_uk, w_uv, w_o_mla, ssm_conv_w, ssm_conv_b, dt_bias_fwd, dt_bias_bwd, a_log_fwd, a_log_bwd, d_skip, ssm_norm_w, w_o_ssm, w_o, norm2_w, router_w, router_b, w_gu, b_gu, w_down, b_down, final_norm_w):
    p = dict(norm1_w=norm1_w, w_in=w_in, conv_a_w=conv_a_w, w_out_a=w_out_a, q_norm_w=q_norm_w, w_uq=w_uq,
             kv_norm_w=kv_norm_w, w_uk=w_uk, w_uv=w_uv, w_o_mla=w_o_mla, ssm_conv_w=ssm_conv_w,
             ssm_conv_b=ssm_conv_b, dt_bias_fwd=dt_bias_fwd, dt_bias_bwd=dt_bias_bwd, a_log_fwd=a_log_fwd,
             a_log_bwd=a_log_bwd, d_skip=d_skip, ssm_norm_w=ssm_norm_w, w_o_ssm=w_o_ssm, w_o=w_o,
             norm2_w=norm2_w, router_w=router_w, router_b=router_b, b_gu=b_gu, b_down=b_down)
    x = jnp.concatenate([x_prompt.reshape(N_CTX, D_MODEL), x_sample.reshape(N_LAT, D_MODEL)], axis=0)
    cond8 = jnp.concatenate([c_ctx[None], c, jnp.zeros((SUBLANES - 1 - DEC_BATCH, D_MODEL), F32)], axis=0)
    mods = _ada_mods(cond8, w_ada, b_ada)
    tabs = _rope_tables()
    sel = _deinterleave_matrix()
    b_gu_split = _bias_split(b_gu, sel)
    final_w = final_norm_w[None]
    hp = SSM_INNER
    zeros_state = jnp.zeros((BATCH, hp, SSM_STATE), F32)

    ckv_out, kpe_out, sf_out, sb_out = [], [], [], []
    y_ctx = y_lat = None
    for l in range(DEPTH):
        lw = _layer_weights(p, l)
        mod3 = mods[l].reshape(SUBLANES, 1, 6 * D_MODEL)
        ta, q, ckv, kf, v, small, z, xact, g = _inproj(l, x, mod3, w_in, lw, tabs)
        kpe_c = jnp.pad(cache_kpe[:, l].reshape(DEC_BATCH * PAST_LEN, QK_ROPE), ((0, 0), (0, LANES - QK_ROPE)))
        kf_c, v_c = _kvcache(cache_ckv[:, l].reshape(DEC_BATCH * PAST_LEN, KV_RANK), kpe_c, lw, tabs)
        att = _attention(q, kf, v, kf_c, v_c)
        init_f = jnp.concatenate([zeros_state, state_ssm_fwd[:, l].reshape(DEC_BATCH, hp, SSM_STATE)], axis=0)
        init_b = jnp.concatenate([zeros_state, state_ssm_bwd[:, l].reshape(DEC_BATCH, hp, SSM_STATE)], axis=0)
        yf, yb, sf, sb = _ssd(xact, small, init_f, init_b, _ssd_params(p, l))
        x1, h2, top_i, top_g, cnt_tiles = _merge(ta, att, yf, yb, xact, z, g, x, mod3, lw)
        rt = _routing(cnt_tiles)
        xs, lp = _dispatch(rt, h2, top_i, top_g)
        y_slots = _moe(l, rt, xs, w_gu, w_down, b_gu_split, b_down, sel)
        x, y_ctx, y_lat = _combine(rt, y_slots, lp, x1, mod3, final_w)
        ckv_out.append(ckv[:N_CTX].reshape(BATCH, SEQ, KV_RANK))
        kpe_out.append(small[:N_CTX, 0:QK_ROPE].reshape(BATCH, SEQ, QK_ROPE))
        sf_out.append(sf[:BATCH].reshape(BATCH, SSM_HEADS, SSM_HEAD_DIM, SSM_STATE))
        sb_out.append(sb[:BATCH].reshape(BATCH, SSM_HEADS, SSM_HEAD_DIM, SSM_STATE))

    y_prompt = y_ctx.reshape(BATCH, SEQ, D_MODEL)
    y_sample = y_lat.reshape(DEC_BATCH, DEC_SEQ, D_MODEL)
    return (y_prompt, y_sample, jnp.stack(ckv_out, axis=1), jnp.stack(kpe_out, axis=1),
            jnp.stack(sf_out, axis=1), jnp.stack(sb_out, axis=1))
```

```python
import functools
import math

import jax
import jax.numpy as jnp
from jax import lax
from jax.experimental import pallas as pl
from jax.experimental.pallas import tpu as pltpu

F32 = jnp.float32
BF16 = jnp.bfloat16
I32 = jnp.int32

D_MODEL = 1024
BATCH = 16
SEQ = 256
DEPTH = 2
DEC_BATCH = 2
DEC_SEQ = 2048
PAST_LEN = 512
GRID_W = 64
NORM_EPS = 1e-6
CONV_DIM = 512
MLA_HEADS = 8
Q_RANK = 384
KV_RANK = 256
QK_NOPE = 64
QK_ROPE = 32
V_HEAD = 64
ROPE_BASE = 10000.0
MLA_SCALE = (QK_NOPE + QK_ROPE) ** -0.5
SSM_HEADS = 16
SSM_HEAD_DIM = 64
SSM_INNER = SSM_HEADS * SSM_HEAD_DIM
SSM_GROUPS = 2
SSM_STATE = 128
SSM_CONV_CH = SSM_INNER + 2 * SSM_GROUPS * SSM_STATE
N_EXPERTS = 32
TOP_K = 4
EXPERT_FF = D_MODEL
SWIGLU_ALPHA = 1.702
SWIGLU_LIMIT = 7.0

N_CTX = BATCH * SEQ
N_LAT = DEC_BATCH * DEC_SEQ
N_TOK = N_CTX + N_LAT
N_SEQS = BATCH + DEC_BATCH

LANES = 128
SUBLANES = 8
HEAD_PAD = 128
TM = 256
N_TILES = N_TOK // TM
CTX_TILES = N_CTX // TM
LAT_TILES_PER_SEQ = DEC_SEQ // TM
CHUNK = 128
CTX_CHUNKS_PER_SEQ = SEQ // CHUNK
LAT_CHUNKS_PER_SEQ = DEC_SEQ // CHUNK
N_CTX_CHUNKS = N_CTX // CHUNK
N_CHUNKS = N_TOK // CHUNK
MOE_BLK = 256
N_ASSIGN = N_TOK * TOP_K
RUN_ALIGN = SUBLANES
RUN_BITS = tuple(range(8, 2, -1))
TAIL_BITS = tuple(range(7, 2, -1))
COMMON_BITS = 6
WAIT_BITS = tuple(range(10, 2, -1))
STAGE_ROWS = 1280
XS_COLS = D_MODEL + LANES
N_BLOCKS = -(-(N_ASSIGN + N_TILES * N_EXPERTS * (RUN_ALIGN - 1) + N_EXPERTS * (MOE_BLK - 1)) // MOE_BLK)
N_SLOTS = N_BLOCKS * MOE_BLK
VMEM_LIMIT = 56 * 1024 * 1024

C_A3 = 0
C_CQ = C_A3 + 3 * CONV_DIM
C_CKV = C_CQ + Q_RANK
C_Z = C_CKV + KV_RANK
C_XBC = C_Z + SSM_INNER
C_GATE = C_XBC + SSM_CONV_CH
C_SMALL = C_GATE + 3 * D_MODEL
IN_COLS2 = C_SMALL + LANES
SM_DTF = QK_ROPE
SM_DTB = QK_ROPE + SSM_HEADS
S_CQ = 3 * CONV_DIM
S_CKV = S_CQ + Q_RANK
S_KPE = S_CKV + KV_RANK
S_Z = S_KPE + QK_ROPE
S_XBC = S_Z + SSM_INNER
S_DTF = S_XBC + SSM_CONV_CH
S_GATE = S_DTF + 2 * SSM_HEADS
IN_COLS = S_GATE + 3 * D_MODEL
W_SEGMENTS = ((C_A3, 0, 3 * CONV_DIM), (C_CQ, S_CQ, Q_RANK), (C_CKV, S_CKV, KV_RANK), (C_Z, S_Z, SSM_INNER),
              (C_XBC, S_XBC, SSM_CONV_CH), (C_GATE, S_GATE, 3 * D_MODEL))
W_PIECE = 512


def _rms(x, w):
    return x * lax.rsqrt(jnp.mean(x * x, axis=-1, keepdims=True) + NORM_EPS) * w


def _silu(x):
    return x * jax.nn.sigmoid(x)


def _dot(a, b):
    return jnp.dot(a, b, preferred_element_type=F32)


def _dot_nt(a, b):
    return lax.dot_general(a, b, (((1,), (1,)), ((), ())), preferred_element_type=F32)


def _resident(shape):
    nd = len(shape)
    return pl.BlockSpec(shape, lambda *_: (0,) * nd, pipeline_mode=pl.Buffered(1))


def _mod_row(i):
    return jnp.where(i < CTX_TILES, 0, 1 + (i - CTX_TILES) // LAT_TILES_PER_SEQ)


def _pos_block(i):
    return jnp.where(i < CTX_TILES, 0, 1 + (i - CTX_TILES) % LAT_TILES_PER_SEQ)


def _ada_kernel(c_ref, w_ref, b_ref, o_ref):
    s = _silu(c_ref[...]).astype(BF16)
    o_ref[0] = _dot(s, w_ref[0].astype(BF16)) + b_ref[0]


def _ada_mods(cond8, w_ada, b_ada):
    tn = 1536
    n_mod = 6 * D_MODEL
    return pl.pallas_call(
        _ada_kernel,
        grid=(DEPTH, n_mod // tn),
        in_specs=[
            pl.BlockSpec((SUBLANES, D_MODEL), lambda l, j: (0, 0)),
            pl.BlockSpec((1, D_MODEL, tn), lambda l, j: (l, 0, j)),
            pl.BlockSpec((1, 1, tn), lambda l, j: (l, 0, j)),
        ],
        out_specs=pl.BlockSpec((1, SUBLANES, tn), lambda l, j: (l, 0, j)),
        out_shape=jax.ShapeDtypeStruct((DEPTH, SUBLANES, n_mod), F32),
        compiler_params=pltpu.CompilerParams(dimension_semantics=("arbitrary", "arbitrary")),
        name="ada_mods",
    )(cond8, w_ada, b_ada.reshape(DEPTH, 1, n_mod))


def _relayout_w_in(layer, wt_hbm, w2, stg, small_stg, sem, small_sem):
    pieces = [(dst + p, src + p, min(W_PIECE, width - p))
              for dst, src, width in W_SEGMENTS for p in range(0, width, W_PIECE)]
    fetch = lambda k: pltpu.make_async_copy(wt_hbm.at[layer, pl.ds(pieces[k][1], pieces[k][2]), :],
                                            stg.at[k % 2, pl.ds(0, pieces[k][2]), :], sem.at[k % 2])
    small_stg[...] = jnp.zeros_like(small_stg)
    small_copies = [
        pltpu.make_async_copy(wt_hbm.at[layer, pl.ds(S_KPE, QK_ROPE), :], small_stg.at[pl.ds(0, QK_ROPE), :],
                              small_sem.at[0]),
        pltpu.make_async_copy(wt_hbm.at[layer, pl.ds(S_DTF, 2 * SSM_HEADS), :],
                              small_stg.at[pl.ds(SM_DTF, 2 * SSM_HEADS), :], small_sem.at[1]),
    ]
    for cp in small_copies:
        cp.start()
    fetch(0).start()
    for k, (dst, _, width) in enumerate(pieces):
        if k + 1 < len(pieces):
            fetch(k + 1).start()
        fetch(k).wait()
        w2[:, dst:dst + width] = stg[k % 2, 0:width, :].T.astype(BF16)
    for cp in small_copies:
        cp.wait()
    w2[:, C_SMALL:IN_COLS2] = small_stg[...].T.astype(BF16)


def _rot_partner(x):
    n = x.shape[1]
    lane = lax.broadcasted_iota(I32, x.shape, 1)
    quarter = QK_ROPE // 4
    return jnp.where(lane % (2 * quarter) < quarter, pltpu.roll(x, n - quarter, 1), pltpu.roll(x, quarter, 1))


def _inproj_kernel(layer, x_ref, xp_ref, xn_ref, mod_ref, n1w_ref, w_hbm, qnw_ref, wq_ref, kvnw_ref, wuk_ref,
                   wuv_ref, tile_ref, cosq_ref, sinq_ref, cosk_ref, sink_ref, caw_ref, cw_ref, cb_ref,
                   ta_ref, q_ref, ckv_ref, kf_ref, v_ref, small_ref, z_ref, xact_ref, g_ref,
                   w2, stg, small_stg, sem, small_sem):
    i = pl.program_id(0)

    @pl.when(i == 0)
    def _():
        _relayout_w_in(layer, w_hbm, w2, stg, small_stg, sem, small_sem)

    mod = mod_ref[0]
    shift1 = mod[:, 0:D_MODEL]
    scale1 = mod[:, D_MODEL:2 * D_MODEL]
    x_ext = jnp.concatenate([xp_ref[...], x_ref[...], xn_ref[...]], axis=0)
    h_ext = _rms(x_ext, n1w_ref[...]) * (1.0 + scale1) + shift1
    hb_ext = h_ext.astype(BF16)
    hb = h_ext[SUBLANES:SUBLANES + TM].astype(BF16)

    j = (i - CTX_TILES) % LAT_TILES_PER_SEQ
    is_ctx = i < CTX_TILES
    keep_prev = jnp.where(jnp.logical_or(is_ctx, j == 0), 0.0, 1.0)
    keep_next = jnp.where(jnp.logical_or(is_ctx, j == LAT_TILES_PER_SEQ - 1), 0.0, 1.0)
    row = lax.broadcasted_iota(I32, (TM, 1), 0)
    prev_mask = jnp.where(row == 0, keep_prev, 1.0)
    next_mask = jnp.where(row == TM - 1, keep_next, 1.0)
    ext = TM + 2 * SUBLANES

    def conv3(x, w_ref):
        x_prev = pltpu.roll(x, 1, 0)[SUBLANES:SUBLANES + TM] * prev_mask
        x_next = pltpu.roll(x, ext - 1, 0)[SUBLANES:SUBLANES + TM] * next_mask
        return x_prev * w_ref[0:1, :] + x[SUBLANES:SUBLANES + TM] * w_ref[1:2, :] + x_next * w_ref[2:3, :]

    a3 = _dot(hb_ext, w2[:, C_A3:C_CQ])
    s = a3[:, 2 * CONV_DIM:3 * CONV_DIM] * a3[:, 0:CONV_DIM]
    ta_ref[...] = (a3[SUBLANES:SUBLANES + TM, CONV_DIM:2 * CONV_DIM] * conv3(s, caw_ref)).astype(BF16)
    xact_ref[...] = _silu(conv3(_dot(hb_ext, w2[:, C_XBC:C_GATE]), cw_ref) + cb_ref[...])

    def seg(a, b):
        return _dot(hb, w2[:, a:b])

    cqn = _rms(seg(C_CQ, C_CKV), qnw_ref[...]).astype(BF16)
    qa = _dot(cqn, wq_ref[...])
    q_ref[...] = (qa * cosq_ref[...] + _rot_partner(qa) * sinq_ref[...]).astype(BF16)

    ckv = _rms(seg(C_CKV, C_Z), kvnw_ref[...])
    ckv_ref[...] = ckv
    ckvb = ckv.astype(BF16)
    small = seg(C_SMALL, IN_COLS2)
    small_ref[...] = small
    kpe = small * cosk_ref[...] + _rot_partner(small) * sink_ref[...]
    kf_ref[...] = (_dot(ckvb, wuk_ref[...]) + _dot(kpe.astype(BF16), tile_ref[...])).astype(BF16)
    v_ref[...] = _dot(ckvb, wuv_ref[...]).astype(BF16)

    z_ref[...] = seg(C_Z, C_XBC)
    g_ref[...] = jax.nn.sigmoid(seg(C_GATE, C_SMALL))


def _inproj(layer, x, mod3, w_in, lw, tabs):
    row = lambda n: pl.BlockSpec((TM, n), lambda i: (i, 0))
    tab = lambda n: pl.BlockSpec((TM, n), lambda i: (_pos_block(i), 0))
    per = TM // SUBLANES
    last = N_TOK // SUBLANES - 1
    qw = MLA_HEADS * HEAD_PAD
    vw = MLA_HEADS * V_HEAD
    out_shape = (
        jax.ShapeDtypeStruct((N_TOK, CONV_DIM), BF16),
        jax.ShapeDtypeStruct((N_TOK, qw), BF16),
        jax.ShapeDtypeStruct((N_TOK, KV_RANK), F32),
        jax.ShapeDtypeStruct((N_TOK, qw), BF16),
        jax.ShapeDtypeStruct((N_TOK, vw), BF16),
        jax.ShapeDtypeStruct((N_TOK, LANES), F32),
        jax.ShapeDtypeStruct((N_TOK, SSM_INNER), F32),
        jax.ShapeDtypeStruct((N_TOK, SSM_CONV_CH), F32),
        jax.ShapeDtypeStruct((N_TOK, 3 * D_MODEL), F32),
    )
    return pl.pallas_call(
        functools.partial(_inproj_kernel, layer),
        grid=(N_TILES,),
        in_specs=[
            row(D_MODEL),
            pl.BlockSpec((SUBLANES, D_MODEL), lambda i: (jnp.maximum(i * per - 1, 0), 0)),
            pl.BlockSpec((SUBLANES, D_MODEL), lambda i: (jnp.minimum((i + 1) * per, last), 0)),
            pl.BlockSpec((1, 1, 6 * D_MODEL), lambda i: (_mod_row(i), 0, 0)),
            _resident((1, D_MODEL)),
            pl.BlockSpec(memory_space=pl.ANY),
            _resident((1, Q_RANK)),
            _resident((Q_RANK, qw)),
            _resident((1, KV_RANK)),
            _resident((KV_RANK, qw)),
            _resident((KV_RANK, vw)),
            _resident((LANES, qw)),
            tab(qw), tab(qw), tab(LANES), tab(LANES),
            _resident((3, CONV_DIM)), _resident((3, SSM_CONV_CH)), _resident((1, SSM_CONV_CH)),
        ],
        out_specs=tuple(row(s.shape[1]) for s in out_shape),
        out_shape=out_shape,
        scratch_shapes=[pltpu.VMEM((D_MODEL, IN_COLS2), BF16), pltpu.VMEM((2, W_PIECE, D_MODEL), F32),
                        pltpu.VMEM((LANES, D_MODEL), F32), pltpu.SemaphoreType.DMA((2,)),
                        pltpu.SemaphoreType.DMA((2,))],
        compiler_params=pltpu.CompilerParams(dimension_semantics=("arbitrary",), vmem_limit_bytes=VMEM_LIMIT),
        name="inproj",
    )(x, x, x, mod3, lw["norm1_w"], w_in, lw["q_norm_w"], lw["wq_a"], lw["kv_norm_w"],
      lw["wuk"], lw["wuv"], tabs["tile"], tabs["cosq"], tabs["sinq"], tabs["cosk"], tabs["sink"],
      lw["conv_a_w"], lw["ssm_conv_w"], lw["ssm_conv_b"])


def _kvcache_kernel(ckv_ref, kpe_ref, wuk_ref, wuv_ref, tile_ref, kf_ref, v_ref):
    ckvb = ckv_ref[...].astype(BF16)
    kf_ref[...] = (_dot(ckvb, wuk_ref[...]) + _dot(kpe_ref[...].astype(BF16), tile_ref[...])).astype(BF16)
    v_ref[...] = _dot(ckvb, wuv_ref[...]).astype(BF16)


def _kvcache(ckv, kpe128, lw, tabs):
    n = ckv.shape[0]
    qw = MLA_HEADS * HEAD_PAD
    vw = MLA_HEADS * V_HEAD
    return pl.pallas_call(
        _kvcache_kernel,
        grid=(n // PAST_LEN,),
        in_specs=[
            pl.BlockSpec((PAST_LEN, KV_RANK), lambda i: (i, 0)),
            pl.BlockSpec((PAST_LEN, LANES), lambda i: (i, 0)),
            _resident((KV_RANK, qw)),
            _resident((KV_RANK, vw)),
            _resident((LANES, qw)),
        ],
        out_specs=(pl.BlockSpec((PAST_LEN, qw), lambda i: (i, 0)), pl.BlockSpec((PAST_LEN, vw), lambda i: (i, 0))),
        out_shape=(jax.ShapeDtypeStruct((n, qw), BF16), jax.ShapeDtypeStruct((n, vw), BF16)),
        compiler_params=pltpu.CompilerParams(dimension_semantics=("arbitrary",)),
        name="kvcache",
    )(ckv, kpe128, lw["wuk"], lw["wuv"], tabs["tile"])


def _attn_heads(q_ref, kv_refs, o_ref, acc_ref):
    for h in range(MLA_HEADS):
        qh = q_ref[:, h * HEAD_PAD:(h + 1) * HEAD_PAD]
        ss = [_dot_nt(qh, k_ref[:, h * HEAD_PAD:(h + 1) * HEAD_PAD]) * MLA_SCALE for k_ref, _ in kv_refs]
        m = functools.reduce(jnp.maximum, [jnp.max(s, axis=-1, keepdims=True) for s in ss])
        ps = [jnp.exp(s - m) for s in ss]
        l = functools.reduce(jnp.add, [jnp.sum(p, axis=-1, keepdims=True) for p in ps])
        o = functools.reduce(jnp.add, [_dot(p.astype(BF16), v_ref[:, h * V_HEAD:(h + 1) * V_HEAD])
                                       for p, (_, v_ref) in zip(ps, kv_refs)])
        acc_ref[:, h * V_HEAD:(h + 1) * V_HEAD] = o / l
    o_ref[...] = acc_ref[...].astype(BF16)


def _attn_ctx_kernel(q_ref, k_ref, v_ref, o_ref, acc_ref):
    _attn_heads(q_ref, [(k_ref, v_ref)], o_ref, acc_ref)


def _attn_lat_kernel(q_ref, k_ref, v_ref, kc_ref, vc_ref, o_ref, acc_ref):
    _attn_heads(q_ref, [(k_ref, v_ref), (kc_ref, vc_ref)], o_ref, acc_ref)


def _attention(q, kf, v, kf_c, v_c):
    qw = MLA_HEADS * HEAD_PAD
    vw = MLA_HEADS * V_HEAD
    att_ctx = pl.pallas_call(
        _attn_ctx_kernel,
        grid=(BATCH,),
        in_specs=[pl.BlockSpec((SEQ, qw), lambda b: (b, 0)), pl.BlockSpec((SEQ, qw), lambda b: (b, 0)),
                  pl.BlockSpec((SEQ, vw), lambda b: (b, 0))],
        out_specs=pl.BlockSpec((SEQ, vw), lambda b: (b, 0)),
        out_shape=jax.ShapeDtypeStruct((N_CTX, vw), BF16),
        scratch_shapes=[pltpu.VMEM((SEQ, vw), F32)],
        compiler_params=pltpu.CompilerParams(dimension_semantics=("arbitrary",)),
        name="attn_ctx",
    )(q, kf, v)
    lat0 = N_CTX // DEC_SEQ
    att_lat = pl.pallas_call(
        _attn_lat_kernel,
        grid=(DEC_BATCH, LAT_TILES_PER_SEQ),
        in_specs=[
            pl.BlockSpec((TM, qw), lambda b, t: (CTX_TILES + b * LAT_TILES_PER_SEQ + t, 0)),
            pl.BlockSpec((DEC_SEQ, qw), lambda b, t: (lat0 + b, 0)),
            pl.BlockSpec((DEC_SEQ, vw), lambda b, t: (lat0 + b, 0)),
            pl.BlockSpec((PAST_LEN, qw), lambda b, t: (b, 0)),
            pl.BlockSpec((PAST_LEN, vw), lambda b, t: (b, 0)),
        ],
        out_specs=pl.BlockSpec((TM, vw), lambda b, t: (b * LAT_TILES_PER_SEQ + t, 0)),
        out_shape=jax.ShapeDtypeStruct((N_LAT, vw), BF16),
        scratch_shapes=[pltpu.VMEM((TM, vw), F32)],
        compiler_params=pltpu.CompilerParams(dimension_semantics=("arbitrary", "arbitrary"),
                                             vmem_limit_bytes=VMEM_LIMIT),
        name="attn_lat",
    )(q, kf, v, kf_c, v_c)
    return jnp.concatenate([att_ctx, att_lat], axis=0)


def _seq_of_chunk(s):
    return jnp.where(s < N_CTX_CHUNKS, s // CTX_CHUNKS_PER_SEQ,
                     BATCH + (s - N_CTX_CHUNKS) // LAT_CHUNKS_PER_SEQ)


def _chunk_in_seq(s):
    return jnp.where(s < N_CTX_CHUNKS, s % CTX_CHUNKS_PER_SEQ, (s - N_CTX_CHUNKS) % LAT_CHUNKS_PER_SEQ)


def _chunks_in_seq(s):
    return jnp.where(s < N_CTX_CHUNKS, CTX_CHUNKS_PER_SEQ, LAT_CHUNKS_PER_SEQ)


def _mirror_chunk(s):
    return s + _chunks_in_seq(s) - 1 - 2 * _chunk_in_seq(s)


def _split3(a):
    a1 = a.astype(BF16)
    r1 = a - a1.astype(F32)
    a2 = r1.astype(BF16)
    a3 = (r1 - a2.astype(F32)).astype(BF16)
    return a1, a2, a3


def _ssd_direction(x_ref, sm_ref, par_ref, st_ref, y_ref, lane0, backward):
    ri = lax.broadcasted_iota(I32, (CHUNK, CHUNK), 0)
    ci = lax.broadcasted_iota(I32, (CHUNK, CHUNK), 1)
    tri = (ci >= ri) if backward else (ci <= ri)
    tri_b = jnp.where(tri, 1.0, 0.0).astype(BF16)
    tot_row = 0 if backward else CHUNK - 1

    dt = jax.nn.softplus(sm_ref[...] + par_ref[0:1, :])
    a = dt * (-jnp.exp(par_ref[1:2, :])) * par_ref[2:3, :]
    a1, a2, a3 = _split3(a)
    acs = _dot(tri_b, a1) + _dot(tri_b, a2) + _dot(tri_b, a3)
    acs_t = acs.T
    dt_t = dt.T
    first_head = lax.broadcasted_iota(I32, (1, LANES), 1) < SSM_HEAD_DIM

    def block_diag(pair):
        return jnp.concatenate([jnp.where(first_head, pair, 0.0), jnp.where(first_head, 0.0, pair)],
                               axis=0).astype(BF16)

    for g in range(SSM_GROUPS):
        b0 = SSM_INNER + g * SSM_STATE
        c0 = SSM_INNER + SSM_GROUPS * SSM_STATE + g * SSM_STATE
        bg = x_ref[:, b0:b0 + SSM_STATE]
        cg = x_ref[:, c0:c0 + SSM_STATE]
        cb = _dot_nt(cg.astype(BF16), bg.astype(BF16))
        bg_t = bg.T
        heads = SSM_HEADS // SSM_GROUPS
        for pr in range(heads // 2):
            h0 = g * heads + 2 * pr
            sl = slice(h0 * SSM_HEAD_DIM, (h0 + 2) * SSM_HEAD_DIM)
            x_bd = block_diag(x_ref[:, sl])
            st_old = st_ref[:, sl]
            within, carried, to_state, keep = [], [], [], []
            for h in (h0, h0 + 1):
                lane = lane0 + h
                col = jnp.broadcast_to(acs[:, lane:lane + 1], (CHUNK, CHUNK))
                row = acs_t[lane:lane + 1, :]
                dt_row = dt_t[lane:lane + 1, :]
                decay = jnp.exp(jnp.where(tri, col - row, -jnp.inf))
                within.append((cb * decay * dt_row).astype(BF16))
                carried.append((cg * jnp.exp(col)).astype(BF16))
                tot = acs[tot_row:tot_row + 1, lane:lane + 1]
                to_state.append((bg_t * (dt_row * jnp.exp(tot - row))).astype(BF16))
                keep.append(jnp.exp(tot))
            lhs = jnp.concatenate(within + carried, axis=1)
            y_ref[:, sl] = _dot(lhs, jnp.concatenate([x_bd, block_diag(st_old)], axis=0))
            st_ref[:, sl] = (st_old * jnp.where(first_head, keep[0], keep[1])
                             + _dot(jnp.concatenate(to_state, axis=1), x_bd))


def _ssd_kernel(xf_ref, xb_ref, smf_ref, smb_ref, if_ref, ib_ref, par_ref,
                yf_ref, yb_ref, sf_ref, sb_ref, stf_ref, stb_ref):
    s = pl.program_id(0)
    c = _chunk_in_seq(s)

    @pl.when(c == 0)
    def _():
        stf_ref[...] = if_ref[0].T
        stb_ref[...] = ib_ref[0].T

    _ssd_direction(xf_ref, smf_ref, par_ref, stf_ref, yf_ref, SM_DTF, False)
    _ssd_direction(xb_ref, smb_ref, par_ref, stb_ref, yb_ref, SM_DTB, True)

    @pl.when(c == _chunks_in_seq(s) - 1)
    def _():
        sf_ref[0] = stf_ref[...].T
        sb_ref[0] = stb_ref[...].T


def _ssd(xact, small, init_f, init_b, par):
    hp = SSM_INNER
    fwd = lambda n: pl.BlockSpec((CHUNK, n), lambda s: (s, 0))
    bwd = lambda n: pl.BlockSpec((CHUNK, n), lambda s: (_mirror_chunk(s), 0))
    st = pl.BlockSpec((1, hp, SSM_STATE), lambda s: (_seq_of_chunk(s), 0, 0))
    return pl.pallas_call(
        _ssd_kernel,
        grid=(N_CHUNKS,),
        in_specs=[fwd(SSM_CONV_CH), bwd(SSM_CONV_CH), fwd(LANES), bwd(LANES), st, st,
                  pl.BlockSpec((SUBLANES, LANES), lambda s: (0, 0))],
        out_specs=(fwd(hp), bwd(hp), st, st),
        out_shape=(jax.ShapeDtypeStruct((N_TOK, hp), F32), jax.ShapeDtypeStruct((N_TOK, hp), F32),
                   jax.ShapeDtypeStruct((N_SEQS, hp, SSM_STATE), F32),
                   jax.ShapeDtypeStruct((N_SEQS, hp, SSM_STATE), F32)),
        scratch_shapes=[pltpu.VMEM((SSM_STATE, hp), F32), pltpu.VMEM((SSM_STATE, hp), F32)],
        compiler_params=pltpu.CompilerParams(dimension_semantics=("arbitrary",)),
        name="ssd",
    )(xact, xact, small, small, init_f, init_b, par)


def _merge_kernel(ta_ref, att_ref, yf_ref, yb_ref, xs_ref, z_ref, g_ref, x_ref, mod_ref,
                  woa_ref, wom_ref, dsk_ref, snw_ref, wos_ref, wo_ref, n2w_ref, rw_ref, rb_ref,
                  x1_ref, h2_ref, ti_ref, tg_ref, cnt_ref):
    mod = mod_ref[0]
    gate1 = mod[:, 2 * D_MODEL:3 * D_MODEL]
    shift2 = mod[:, 3 * D_MODEL:4 * D_MODEL]
    scale2 = mod[:, 4 * D_MODEL:5 * D_MODEL]
    y_a = _dot(ta_ref[...], woa_ref[...])
    y_b = _dot(att_ref[...], wom_ref[...])
    y_ssm = (yf_ref[...] + yb_ref[...] + dsk_ref[...] * xs_ref[...]) * _silu(z_ref[...])
    y_c = _dot(_rms(y_ssm, snw_ref[...]).astype(BF16), wos_ref[...])
    merged = (g_ref[:, 0:D_MODEL] * y_a + g_ref[:, D_MODEL:2 * D_MODEL] * y_b
              + g_ref[:, 2 * D_MODEL:3 * D_MODEL] * y_c)
    x1 = x_ref[...] + gate1 * _dot(merged.astype(BF16), wo_ref[...])
    x1_ref[...] = x1
    h2 = _rms(x1, n2w_ref[...]) * (1.0 + scale2) + shift2
    h2_ref[...] = h2

    logits = _dot(h2.astype(BF16), rw_ref[...]) + rb_ref[...]
    lane = lax.broadcasted_iota(I32, (TM, LANES), 1)
    ti = jnp.zeros((TM, LANES), I32)
    tv = jnp.full((TM, LANES), -jnp.inf, F32)
    chosen = jnp.zeros((TM, LANES), F32)
    for k in range(TOP_K):
        m = jnp.max(logits, axis=-1, keepdims=True)
        idx = jnp.min(jnp.where(logits == m, lane, LANES), axis=-1, keepdims=True)
        ti = jnp.where(lane == k, idx, ti)
        tv = jnp.where(lane == k, m, tv)
        chosen = jnp.where(lane == idx, 1.0, chosen)
        logits = jnp.where(lane == idx, -jnp.inf, logits)
    e = jnp.exp(tv - jnp.max(tv, axis=-1, keepdims=True))
    ti_ref[...] = ti
    tg_ref[...] = e / jnp.sum(e, axis=-1, keepdims=True)
    cnt_ref[0] = jnp.broadcast_to(jnp.sum(chosen, axis=0, keepdims=True), (SUBLANES, LANES)).astype(I32)


def _merge(ta, att, yf, yb, xact, z, g, x, mod3, lw):
    row = lambda n: pl.BlockSpec((TM, n), lambda i: (i, 0))
    out_shape = (jax.ShapeDtypeStruct((N_TOK, D_MODEL), F32), jax.ShapeDtypeStruct((N_TOK, D_MODEL), F32),
                 jax.ShapeDtypeStruct((N_TOK, LANES), I32), jax.ShapeDtypeStruct((N_TOK, LANES), F32))
    cnt_shape = jax.ShapeDtypeStruct((N_TILES, SUBLANES, LANES), I32)
    cnt_spec = pl.BlockSpec((1, SUBLANES, LANES), lambda i: (i, 0, 0))
    return pl.pallas_call(
        _merge_kernel,
        grid=(N_TILES,),
        in_specs=[
            row(CONV_DIM), row(MLA_HEADS * V_HEAD), row(SSM_INNER), row(SSM_INNER), row(SSM_INNER),
            row(SSM_INNER), row(3 * D_MODEL), row(D_MODEL),
            pl.BlockSpec((1, 1, 6 * D_MODEL), lambda i: (_mod_row(i), 0, 0)),
            _resident((CONV_DIM, D_MODEL)), _resident((MLA_HEADS * V_HEAD, D_MODEL)),
            _resident((1, SSM_INNER)), _resident((1, SSM_INNER)), _resident((SSM_INNER, D_MODEL)),
            _resident((D_MODEL, D_MODEL)), _resident((1, D_MODEL)),
            _resident((D_MODEL, LANES)), _resident((1, LANES)),
        ],
        out_specs=tuple(row(s.shape[1]) for s in out_shape) + (cnt_spec,),
        out_shape=out_shape + (cnt_shape,),
        compiler_params=pltpu.CompilerParams(dimension_semantics=("arbitrary",), vmem_limit_bytes=VMEM_LIMIT),
        name="merge",
    )(ta, att, yf, yb, xact, z, g, x, mod3, lw["w_out_a"], lw["w_o_mla"], lw["d_skip"], lw["ssm_norm_w"],
      lw["w_o_ssm"], lw["w_o"], lw["norm2_w"], lw["router_w"], lw["router_b"])


def _run_copies(cnt_ref, src_ref, dst_ref, first, count, bits, make_copy, start):
    def body(e, carry):
        n = cnt_ref[first + e]
        s0 = src_ref[first + e] if src_ref is not None else 0
        d0 = dst_ref[first + e]

        def pieces(some_bits):
            for b in some_bits:
                above = (n >> (b + 1)) << (b + 1)

                @pl.when(((n >> b) & 1) == 1)
                def _():
                    cp = make_copy(pl.multiple_of(s0 + above, RUN_ALIGN), pl.multiple_of(d0 + above, RUN_ALIGN),
                                   1 << b)
                    if start:
                        cp.start()
                    else:
                        cp.wait()

        large = [b for b in bits if b >= COMMON_BITS]

        @pl.when(n >= (1 << COMMON_BITS))
        def _():
            pieces(large)
        pieces([b for b in bits if b < COMMON_BITS])
        return carry
    lax.fori_loop(0, count, body, 0)


def _wait_rows(total, make_copy):
    for b in WAIT_BITS:
        @pl.when(((total >> b) & 1) == 1)
        def _():
            make_copy(0, 0, 1 << b).wait()


def _dispatch_kernel(cnt_ref, off_ref, run_ref, tcnt_ref, tdst_ref, nu_ref, h2_ref, ti_ref, tg_ref, offv_ref,
                     xs_ref, lp_ref, stage, zeros, perm_s, ghi_s, glo_s, sem, semz):
    i = pl.program_id(0)
    slot = i % 2

    def copy_out(s_):
        return lambda s, d, n: pltpu.make_async_copy(stage.at[s_, pl.ds(s, n), :], xs_ref.at[pl.ds(d, n), :],
                                                     sem.at[s_])

    def tile_rows(t):
        last = t * N_EXPERTS + N_EXPERTS - 1
        return off_ref[last] + cnt_ref[last]

    @pl.when(i == 0)
    def _():
        zeros[...] = jnp.zeros_like(zeros)
        zero_out = lambda s, d, n: pltpu.make_async_copy(zeros.at[pl.ds(0, n), :], xs_ref.at[pl.ds(d, n), :], semz)
        _run_copies(tcnt_ref, None, tdst_ref, 0, N_EXPERTS, TAIL_BITS, zero_out, True)
        _run_copies(tcnt_ref, None, tdst_ref, 0, N_EXPERTS, TAIL_BITS, zero_out, False)
        zrows = zeros.shape[0]

        def unused_blocks(start):
            def body(b, carry):
                for part in range(MOE_BLK // zrows):
                    cp = zero_out(0, pl.multiple_of(b * MOE_BLK + part * zrows, RUN_ALIGN), zrows)
                    if start:
                        cp.start()
                    else:
                        cp.wait()
                return carry
            lax.fori_loop(nu_ref[0], N_BLOCKS, body, 0)
        unused_blocks(True)
        unused_blocks(False)

    @pl.when(i >= 2)
    def _():
        _wait_rows(tile_rows(i - 2), copy_out(slot))

    lane = lax.broadcasted_iota(I32, (TM, LANES), 1)
    picks = [jnp.where(lane == ti_ref[:, k:k + 1], 1.0, 0.0) for k in range(TOP_K)]
    ri = lax.broadcasted_iota(I32, (TM, TM), 0)
    ci = lax.broadcasted_iota(I32, (TM, TM), 1)
    earlier = jnp.where(ci < ri, 1.0, 0.0).astype(BF16)
    base = _dot(earlier, functools.reduce(jnp.add, picks).astype(BF16)) + offv_ref[0][0:1, :]
    lp = jnp.full((TM, LANES), -1.0, F32)
    for k in range(TOP_K):
        lp = jnp.where(lane == k, jnp.sum(picks[k] * base, axis=-1, keepdims=True), lp)
    lp_ref[...] = lp.astype(I32)

    lp_t = lp.T
    tg_t = tg_ref[...].T
    for c in range(STAGE_ROWS // LANES):
        rows = slice(c * LANES, (c + 1) * LANES)
        row = (lax.broadcasted_iota(I32, (LANES, TM), 0) + c * LANES).astype(F32)
        perm = jnp.zeros((LANES, TM), F32)
        gates = jnp.zeros((LANES, TM), F32)
        for k in range(TOP_K):
            hit = row == lp_t[k:k + 1, :]
            perm = jnp.where(hit, 1.0, perm)
            gates = jnp.where(hit, tg_t[k:k + 1, :], gates)
        perm_s[rows, :] = perm.astype(BF16)
        g_hi = gates.astype(BF16)
        ghi_s[rows, :] = g_hi
        glo_s[rows, :] = (gates - g_hi.astype(F32)).astype(BF16)
    stage[slot, :, 0:D_MODEL] = _dot(perm_s[...], h2_ref[...].astype(BF16))
    ones = jnp.ones((TM, LANES), BF16)
    stage[slot, :, D_MODEL:XS_COLS] = _dot(ghi_s[...], ones) + _dot(glo_s[...], ones)

    _run_copies(cnt_ref, off_ref, run_ref, i * N_EXPERTS, N_EXPERTS, RUN_BITS, copy_out(slot), True)

    @pl.when(i == N_TILES - 1)
    def _():
        _wait_rows(tile_rows(i - 1), copy_out(1 - slot))
        _wait_rows(tile_rows(i), copy_out(slot))


def _dispatch(rt, h2, top_i, top_g):
    row = lambda n: pl.BlockSpec((TM, n), lambda i, *_: (i, 0))
    return pl.pallas_call(
        _dispatch_kernel,
        grid_spec=pltpu.PrefetchScalarGridSpec(
            num_scalar_prefetch=6,
            grid=(N_TILES,),
            in_specs=[row(D_MODEL), row(LANES), row(LANES),
                      pl.BlockSpec((1, SUBLANES, LANES), lambda i, *_: (i, 0, 0))],
            out_specs=(pl.BlockSpec(memory_space=pl.ANY), row(LANES)),
            scratch_shapes=[
                pltpu.VMEM((2, STAGE_ROWS, XS_COLS), F32),
                pltpu.VMEM((1 << TAIL_BITS[0], XS_COLS), F32),
                pltpu.VMEM((STAGE_ROWS, TM), BF16), pltpu.VMEM((STAGE_ROWS, TM), BF16),
                pltpu.VMEM((STAGE_ROWS, TM), BF16),
                pltpu.SemaphoreType.DMA((2,)),
                pltpu.SemaphoreType.DMA,
            ],
        ),
        out_shape=(jax.ShapeDtypeStruct((N_SLOTS, XS_COLS), F32), jax.ShapeDtypeStruct((N_TOK, LANES), I32)),
        compiler_params=pltpu.CompilerParams(dimension_semantics=("arbitrary",), vmem_limit_bytes=VMEM_LIMIT),
        name="dispatch",
    )(rt["cnt"], rt["off"], rt["run"], rt["tail_cnt"], rt["tail_dst"], rt["n_used"], h2, top_i, top_g, rt["off_v"])


def _moe_kernel(layer, be_ref, nxt_ref, nu_ref, x_ref, wgu_hbm, wdn_hbm, bg_ref, bu_ref, bd_ref, sel_ref, y_ref,
                wgu_ref, wdn_ref, wgu_s, wdn_s, sem):
    i = pl.program_id(0)
    n_used = nu_ref[0]

    def fetch(e):
        return (pltpu.make_async_copy(wgu_hbm.at[layer, e], wgu_ref, sem.at[0]),
                pltpu.make_async_copy(wdn_hbm.at[layer, e], wdn_ref, sem.at[1]))

    @pl.when(i < n_used)
    def _():
        @pl.when(jnp.logical_or(i == 0, be_ref[i] != be_ref[jnp.maximum(i - 1, 0)]))
        def _():
            @pl.when(i == 0)
            def _():
                for cp in fetch(be_ref[0]):
                    cp.start()
            for cp in fetch(be_ref[i]):
                cp.wait()
            half = LANES
            for c in range(2 * EXPERT_FF // (2 * half)):
                r = _dot(wgu_ref[:, c * 2 * half:(c + 1) * 2 * half].astype(BF16), sel_ref[...])
                wgu_s[:, c * half:(c + 1) * half] = r[:, 0:half].astype(BF16)
                wgu_s[:, EXPERT_FF + c * half:EXPERT_FF + (c + 1) * half] = r[:, half:2 * half].astype(BF16)
            wdn_s[...] = wdn_ref[...].astype(BF16)
            nxt = nxt_ref[i]

            @pl.when(nxt >= 0)
            def _():
                for cp in fetch(nxt):
                    cp.start(priority=1)

        gu = _dot(x_ref[:, 0:D_MODEL].astype(BF16), wgu_s[...])
        gate = jnp.minimum(gu[:, 0:EXPERT_FF] + bg_ref[0], SWIGLU_LIMIT)
        up = jnp.clip(gu[:, EXPERT_FF:2 * EXPERT_FF] + bu_ref[0], -SWIGLU_LIMIT, SWIGLU_LIMIT)
        act = gate * jax.nn.sigmoid(SWIGLU_ALPHA * gate) * (up + 1.0)
        y = _dot(act.astype(BF16), wdn_s[...]) + bd_ref[0]
        slot_gate = x_ref[:, D_MODEL:XS_COLS]
        for j in range(D_MODEL // LANES):
            y_ref[:, j * LANES:(j + 1) * LANES] = y[:, j * LANES:(j + 1) * LANES] * slot_gate

    @pl.when(i >= n_used)
    def _():
        y_ref[...] = jnp.zeros_like(y_ref)


def _bias_split_kernel(b_ref, sel_ref, o_ref):
    for c in range(2 * EXPERT_FF // (2 * LANES)):
        terms = _split3(b_ref[:, c * 2 * LANES:(c + 1) * 2 * LANES])
        r = functools.reduce(jnp.add, [_dot(t, sel_ref[...]) for t in terms])
        o_ref[:, c * LANES:(c + 1) * LANES] = r[:, 0:LANES]
        o_ref[:, EXPERT_FF + c * LANES:EXPERT_FF + (c + 1) * LANES] = r[:, LANES:2 * LANES]


def _bias_split(b_gu, sel):
    n = DEPTH * N_EXPERTS
    out = pl.pallas_call(
        _bias_split_kernel,
        out_shape=jax.ShapeDtypeStruct((n, 2 * EXPERT_FF), F32),
        name="bias_split",
    )(b_gu.reshape(n, 2 * EXPERT_FF), sel)
    return out.reshape(n, 1, 2 * EXPERT_FF)


def _moe(layer, rt, xs, w_gu, w_down, b_gu_split, b_down, sel):
    first = layer * N_EXPERTS
    bg = pl.BlockSpec((1, 1, EXPERT_FF), lambda i, be, nx, nu: (first + be[i], 0, 0))
    bu = pl.BlockSpec((1, 1, EXPERT_FF), lambda i, be, nx, nu: (first + be[i], 0, 1))
    bd = pl.BlockSpec((1, 1, D_MODEL), lambda i, be, nx, nu: (first + be[i], 0, 0))
    return pl.pallas_call(
        functools.partial(_moe_kernel, layer),
        grid_spec=pltpu.PrefetchScalarGridSpec(
            num_scalar_prefetch=3,
            grid=(N_BLOCKS,),
            in_specs=[
                pl.BlockSpec((MOE_BLK, XS_COLS), lambda i, be, nx, nu: (jnp.minimum(i, nu[0] - 1), 0)),
                pl.BlockSpec(memory_space=pl.ANY),
                pl.BlockSpec(memory_space=pl.ANY),
                bg, bu, bd,
                pl.BlockSpec((2 * LANES, 2 * LANES), lambda i, be, nx, nu: (0, 0)),
            ],
            out_specs=pl.BlockSpec((MOE_BLK, D_MODEL), lambda i, be, nx, nu: (i, 0)),
            scratch_shapes=[
                pltpu.VMEM((D_MODEL, 2 * EXPERT_FF), F32),
                pltpu.VMEM((EXPERT_FF, D_MODEL), F32),
                pltpu.VMEM((D_MODEL, 2 * EXPERT_FF), BF16),
                pltpu.VMEM((EXPERT_FF, D_MODEL), BF16),
                pltpu.SemaphoreType.DMA((2,)),
            ],
        ),
        out_shape=jax.ShapeDtypeStruct((N_SLOTS, D_MODEL), F32),
        compiler_params=pltpu.CompilerParams(dimension_semantics=("arbitrary",), vmem_limit_bytes=VMEM_LIMIT),
        name="moe",
    )(rt["blk_e"], rt["blk_next"], rt["n_used"], xs, w_gu, w_down, b_gu_split, b_gu_split,
      b_down.reshape(DEPTH * N_EXPERTS, 1, D_MODEL), sel)


def _combine_kernel(cnt_ref, off_ref, run_ref, y_ref, lp_ref, x1_ref, mod_ref, fw_ref, x2_ref, yc_ref, yl_ref,
                    stage, mine_s, sem):
    i = pl.program_id(0)
    slot = i % 2

    def copy_in(s_):
        return lambda s, d, n: pltpu.make_async_copy(y_ref.at[pl.ds(d, n), :], stage.at[s_, pl.ds(s, n), :],
                                                     sem.at[s_])

    @pl.when(i == 0)
    def _():
        stage[...] = jnp.zeros_like(stage)
        _run_copies(cnt_ref, off_ref, run_ref, 0, N_EXPERTS, RUN_BITS, copy_in(0), True)

    @pl.when(i + 1 < N_TILES)
    def _():
        _run_copies(cnt_ref, off_ref, run_ref, (i + 1) * N_EXPERTS, N_EXPERTS, RUN_BITS, copy_in(1 - slot), True)

    last = i * N_EXPERTS + N_EXPERTS - 1
    _wait_rows(off_ref[last] + cnt_ref[last], copy_in(slot))
    lane = lax.broadcasted_iota(I32, (TM, LANES), 1)
    mine = [jnp.broadcast_to(lp_ref[:, k:k + 1], (TM, LANES)) for k in range(TOP_K)]
    for c in range(STAGE_ROWS // LANES):
        hit = jnp.zeros((TM, LANES), F32)
        for k in range(TOP_K):
            hit = jnp.where(lane + c * LANES == mine[k], 1.0, hit)
        mine_s[:, c * LANES:(c + 1) * LANES] = hit.astype(BF16)
    moe = _dot(mine_s[...], stage[slot].astype(BF16))
    gate2 = mod_ref[0][:, 5 * D_MODEL:6 * D_MODEL]
    x2 = x1_ref[...] + gate2 * moe
    x2_ref[...] = x2
    y_norm = _rms(x2, fw_ref[...])

    @pl.when(i < CTX_TILES)
    def _():
        yc_ref[...] = y_norm

    @pl.when(i >= CTX_TILES)
    def _():
        yl_ref[...] = y_norm


def _combine(rt, y_slots, lp, x1, mod3, final_w):
    row = lambda n: pl.BlockSpec((TM, n), lambda i, *_: (i, 0))
    return pl.pallas_call(
        _combine_kernel,
        grid_spec=pltpu.PrefetchScalarGridSpec(
            num_scalar_prefetch=3,
            grid=(N_TILES,),
            in_specs=[
                pl.BlockSpec(memory_space=pl.ANY),
                row(LANES), row(D_MODEL),
                pl.BlockSpec((1, 1, 6 * D_MODEL), lambda i, *_: (_mod_row(i), 0, 0)),
                pl.BlockSpec((1, D_MODEL), lambda i, *_: (0, 0)),
            ],
            out_specs=(row(D_MODEL),
                       pl.BlockSpec((TM, D_MODEL), lambda i, *_: (jnp.minimum(i, CTX_TILES - 1), 0)),
                       pl.BlockSpec((TM, D_MODEL), lambda i, *_: (jnp.maximum(i - CTX_TILES, 0), 0))),
            scratch_shapes=[pltpu.VMEM((2, STAGE_ROWS, D_MODEL), F32), pltpu.VMEM((TM, STAGE_ROWS), BF16),
                            pltpu.SemaphoreType.DMA((2,))],
        ),
        out_shape=(jax.ShapeDtypeStruct((N_TOK, D_MODEL), F32), jax.ShapeDtypeStruct((N_CTX, D_MODEL), F32),
                   jax.ShapeDtypeStruct((N_LAT, D_MODEL), F32)),
        compiler_params=pltpu.CompilerParams(dimension_semantics=("arbitrary",), vmem_limit_bytes=VMEM_LIMIT),
        name="combine",
    )(rt["cnt"], rt["off"], rt["run"], y_slots, lp, x1, mod3, final_w)


def _rope_tables():
    rows = DEC_SEQ // GRID_W
    t = jnp.arange(rows * GRID_W)
    row = (t // GRID_W).astype(F32)
    col = (t % GRID_W).astype(F32)
    half = QK_ROPE // 2
    inv = ROPE_BASE ** (-jnp.arange(0, half, 2, dtype=F32) / half)
    ang_r, ang_c = row[:, None] * inv, col[:, None] * inv
    cos32 = jnp.concatenate([jnp.cos(ang_r), jnp.cos(ang_r), jnp.cos(ang_c), jnp.cos(ang_c)], axis=-1)
    sin32 = jnp.concatenate([-jnp.sin(ang_r), jnp.sin(ang_r), -jnp.sin(ang_c), jnp.sin(ang_c)], axis=-1)
    cos32 = jnp.concatenate([jnp.ones((TM, QK_ROPE), F32), cos32], axis=0)
    sin32 = jnp.concatenate([jnp.zeros((TM, QK_ROPE), F32), sin32], axis=0)
    n = cos32.shape[0]
    pad = HEAD_PAD - QK_NOPE - QK_ROPE
    cos_h = jnp.concatenate([jnp.ones((n, QK_NOPE), F32), cos32, jnp.zeros((n, pad), F32)], axis=-1)
    sin_h = jnp.concatenate([jnp.zeros((n, QK_NOPE), F32), sin32, jnp.zeros((n, pad), F32)], axis=-1)
    zeros = jnp.zeros((n, LANES - QK_ROPE), F32)
    j = jnp.arange(QK_ROPE)
    tile = jnp.zeros((LANES, MLA_HEADS, HEAD_PAD), F32).at[j, :, QK_NOPE + j].set(1.0)
    return {
        "cosq": jnp.tile(cos_h, (1, MLA_HEADS)), "sinq": jnp.tile(sin_h, (1, MLA_HEADS)),
        "cosk": jnp.concatenate([cos32, zeros], axis=-1), "sink": jnp.concatenate([sin32, zeros], axis=-1),
        "tile": tile.reshape(LANES, MLA_HEADS * HEAD_PAD).astype(BF16),
    }


def _layer_weights(p, l):
    hd = QK_NOPE + QK_ROPE
    pad = HEAD_PAD - hd
    wq = p["w_uq"][l].reshape(Q_RANK, MLA_HEADS, hd)
    wq_a = jnp.pad(wq, ((0, 0), (0, 0), (0, pad)))
    wuk = jnp.pad(p["w_uk"][l], ((0, 0), (0, 0), (0, HEAD_PAD - QK_NOPE)))
    rw = jnp.pad(p["router_w"][l], ((0, 0), (0, LANES - N_EXPERTS)))
    rb = jnp.concatenate([p["router_b"][l], jnp.full((LANES - N_EXPERTS,), -jnp.inf, F32)])
    return {
        "norm1_w": p["norm1_w"][l][None], "q_norm_w": p["q_norm_w"][l][None],
        "wq_a": wq_a.reshape(Q_RANK, -1).astype(BF16),
        "kv_norm_w": p["kv_norm_w"][l][None],
        "wuk": wuk.reshape(KV_RANK, -1).astype(BF16), "wuv": p["w_uv"][l].reshape(KV_RANK, -1).astype(BF16),
        "conv_a_w": p["conv_a_w"][l], "ssm_conv_w": p["ssm_conv_w"][l], "ssm_conv_b": p["ssm_conv_b"][l][None],
        "w_out_a": p["w_out_a"][l].astype(BF16), "w_o_mla": p["w_o_mla"][l].astype(BF16),
        "d_skip": jnp.repeat(p["d_skip"][l], SSM_HEAD_DIM)[None], "ssm_norm_w": p["ssm_norm_w"][l][None],
        "w_o_ssm": p["w_o_ssm"][l].astype(BF16), "w_o": p["w_o"][l].astype(BF16),
        "norm2_w": p["norm2_w"][l][None], "router_w": rw.astype(BF16), "router_b": rb[None],
    }


def _ssd_params(p, l):
    z = lambda n: jnp.zeros((n,), F32)
    lanes = lambda f, b: jnp.concatenate([z(SM_DTF), f, b, z(LANES - SM_DTB - SSM_HEADS)])
    ones = jnp.ones((SSM_HEADS,), F32)
    rows = [lanes(p["dt_bias_fwd"][l], p["dt_bias_bwd"][l]), lanes(p["a_log_fwd"][l], p["a_log_bwd"][l]),
            lanes(ones, ones)]
    return jnp.concatenate([jnp.stack(rows), jnp.zeros((SUBLANES - 3, LANES), F32)], axis=0)


def _routing(cnt_tiles):
    cnt = cnt_tiles[:, 0, 0:N_EXPERTS]
    cnt = (cnt + RUN_ALIGN - 1) // RUN_ALIGN * RUN_ALIGN
    per_expert = jnp.sum(cnt, axis=0)
    padded = (per_expert + MOE_BLK - 1) // MOE_BLK * MOE_BLK
    pad_end = jnp.cumsum(padded)
    pad_start = pad_end - padded
    run = pad_start[None, :] + jnp.cumsum(cnt, axis=0) - cnt
    off = jnp.cumsum(cnt, axis=1) - cnt
    starts = jnp.arange(N_BLOCKS, dtype=I32) * MOE_BLK
    blk_e = jnp.minimum(jnp.sum((pad_end[None, :] <= starts[:, None]).astype(I32), axis=1), N_EXPERTS - 1)
    off_v = jnp.zeros((N_TILES, SUBLANES, LANES), F32).at[:, 0, 0:N_EXPERTS].set(off.astype(F32))
    ids = jnp.arange(N_EXPERTS, dtype=I32)
    later = jnp.logical_and(ids[None, :] > ids[:, None], padded[None, :] > 0)
    nxt = jnp.min(jnp.where(later, ids[None, :], N_EXPERTS), axis=1)
    nxt = jnp.where(nxt == N_EXPERTS, -1, nxt)
    return {
        "blk_next": jnp.sum(jnp.where(blk_e[:, None] == ids[None, :], nxt[None, :], 0), axis=1).astype(I32),
        "cnt": cnt.reshape(-1).astype(I32), "off": off.reshape(-1).astype(I32), "run": run.reshape(-1).astype(I32),
        "tail_cnt": (padded - per_expert).astype(I32), "tail_dst": (pad_start + per_expert).astype(I32),
        "blk_e": blk_e.astype(I32), "n_used": (pad_end[-1] // MOE_BLK).astype(I32).reshape(1), "off_v": off_v,
    }


def _deinterleave_matrix():
    k = jnp.arange(2 * LANES)[:, None]
    n = jnp.arange(2 * LANES)[None, :]
    src = jnp.where(n < LANES, 2 * n, 2 * (n - LANES) + 1)
    return (k == src).astype(BF16)


def kernel(x_prompt, x_sample, cache_ckv, cache_kpe, state_ssm_fwd, state_ssm_bwd, c, c_ctx, w_ada, b_ada, norm1_w, w_in, conv_a_w, w_out_a, q_norm_w, w_uq, kv_norm_w, w_uk, w_uv, w_o_mla, ssm_conv_w, ssm_conv_b, dt_bias_fwd, dt_bias_bwd, a_log_fwd, a_log_bwd, d_skip, ssm_norm_w, w_o_ssm, w_o, norm2_w, router_w, router_b, w_gu, b_gu, w_down, b_down, final_norm_w):
    p = dict(norm1_w=norm1_w, w_in=w_in, conv_a_w=conv_a_w, w_out_a=w_out_a, q_norm_w=q_norm_w, w_uq=w_uq,
             kv_norm_w=kv_norm_w, w_uk=w_uk, w_uv=w_uv, w_o_mla=w_o_mla, ssm_conv_w=ssm_conv_w,
             ssm_conv_b=ssm_conv_b, dt_bias_fwd=dt_bias_fwd, dt_bias_bwd=dt_bias_bwd, a_log_fwd=a_log_fwd,
             a_log_bwd=a_log_bwd, d_skip=d_skip, ssm_norm_w=ssm_norm_w, w_o_ssm=w_o_ssm, w_o=w_o,
             norm2_w=norm2_w, router_w=router_w, router_b=router_b, b_gu=b_gu, b_down=b_down)
    x = jnp.concatenate([x_prompt.reshape(N_CTX, D_MODEL), x_sample.reshape(N_LAT, D_MODEL)], axis=0)
    cond8 = jnp.concatenate([c_ctx[None], c, jnp.zeros((SUBLANES - 1 - DEC_BATCH, D_MODEL), F32)], axis=0)
    mods = _ada_mods(cond8, w_ada, b_ada)
    tabs = _rope_tables()
    sel = _deinterleave_matrix()
    b_gu_split = _bias_split(b_gu, sel)
    w_in_t = jnp.swapaxes(w_in, 1, 2)
    final_w = final_norm_w[None]
    hp = SSM_INNER
    zeros_state = jnp.zeros((BATCH, hp, SSM_STATE), F32)

    ckv_out, kpe_out, sf_out, sb_out = [], [], [], []
    y_ctx = y_lat = None
    for l in range(DEPTH):
        lw = _layer_weights(p, l)
        mod3 = mods[l].reshape(SUBLANES, 1, 6 * D_MODEL)
        ta, q, ckv, kf, v, small, z, xact, g = _inproj(l, x, mod3, w_in_t, lw, tabs)
        kpe_c = jnp.pad(cache_kpe[:, l].reshape(DEC_BATCH * PAST_LEN, QK_ROPE), ((0, 0), (0, LANES - QK_ROPE)))
        kf_c, v_c = _kvcache(cache_ckv[:, l].reshape(DEC_BATCH * PAST_LEN, KV_RANK), kpe_c, lw, tabs)
        att = _attention(q, kf, v, kf_c, v_c)
        init_f = jnp.concatenate([zeros_state, state_ssm_fwd[:, l].reshape(DEC_BATCH, hp, SSM_STATE)], axis=0)
        init_b = jnp.concatenate([zeros_state, state_ssm_bwd[:, l].reshape(DEC_BATCH, hp, SSM_STATE)], axis=0)
        yf, yb, sf, sb = _ssd(xact, small, init_f, init_b, _ssd_params(p, l))
        x1, h2, top_i, top_g, cnt_tiles = _merge(ta, att, yf, yb, xact, z, g, x, mod3, lw)
        rt = _routing(cnt_tiles)
        xs, lp = _dispatch(rt, h2, top_i, top_g)
        y_slots = _moe(l, rt, xs, w_gu, w_down, b_gu_split, b_down, sel)
        x, y_ctx, y_lat = _combine(rt, y_slots, lp, x1, mod3, final_w)
        ckv_out.append(ckv[:N_CTX].reshape(BATCH, SEQ, KV_RANK))
        kpe_out.append(small[:N_CTX, 0:QK_ROPE].reshape(BATCH, SEQ, QK_ROPE))
        sf_out.append(sf[:BATCH].reshape(BATCH, SSM_HEADS, SSM_HEAD_DIM, SSM_STATE))
        sb_out.append(sb[:BATCH].reshape(BATCH, SSM_HEADS, SSM_HEAD_DIM, SSM_STATE))

    y_prompt = y_ctx.reshape(BATCH, SEQ, D_MODEL)
    y_sample = y_lat.reshape(DEC_BATCH, DEC_SEQ, D_MODEL)
    return (y_prompt, y_sample, jnp.stack(ckv_out, axis=1), jnp.stack(kpe_out, axis=1),
            jnp.stack(sf_out, axis=1), jnp.stack(sb_out, axis=1))
```

```python
import functools
import math

import jax
import jax.numpy as jnp
from jax import lax
from jax.experimental import pallas as pl
from jax.experimental.pallas import tpu as pltpu

F32 = jnp.float32
BF16 = jnp.bfloat16
I32 = jnp.int32

D_MODEL = 1024
BATCH = 16
SEQ = 256
DEPTH = 2
DEC_BATCH = 2
DEC_SEQ = 2048
PAST_LEN = 512
GRID_W = 64
NORM_EPS = 1e-6
CONV_DIM = 512
MLA_HEADS = 8
Q_RANK = 384
KV_RANK = 256
QK_NOPE = 64
QK_ROPE = 32
V_HEAD = 64
ROPE_BASE = 10000.0
MLA_SCALE = (QK_NOPE + QK_ROPE) ** -0.5
SSM_HEADS = 16
SSM_HEAD_DIM = 64
SSM_INNER = SSM_HEADS * SSM_HEAD_DIM
SSM_GROUPS = 2
SSM_STATE = 128
SSM_CONV_CH = SSM_INNER + 2 * SSM_GROUPS * SSM_STATE
N_EXPERTS = 32
TOP_K = 4
EXPERT_FF = D_MODEL
SWIGLU_ALPHA = 1.702
SWIGLU_LIMIT = 7.0

N_CTX = BATCH * SEQ
N_LAT = DEC_BATCH * DEC_SEQ
N_TOK = N_CTX + N_LAT
N_SEQS = BATCH + DEC_BATCH

LANES = 128
SUBLANES = 8
HEAD_PAD = 128
TM = 256
N_TILES = N_TOK // TM
CTX_TILES = N_CTX // TM
LAT_TILES_PER_SEQ = DEC_SEQ // TM
CHUNK = 128
CTX_CHUNKS_PER_SEQ = SEQ // CHUNK
LAT_CHUNKS_PER_SEQ = DEC_SEQ // CHUNK
N_CTX_CHUNKS = N_CTX // CHUNK
N_CHUNKS = N_TOK // CHUNK
MOE_BLK = 256
N_ASSIGN = N_TOK * TOP_K
RUN_ALIGN = SUBLANES
RUN_BITS = tuple(range(8, 2, -1))
TAIL_BITS = tuple(range(7, 2, -1))
COMMON_BITS = 6
WAIT_BITS = tuple(range(10, 2, -1))
STAGE_ROWS = 1280
XS_COLS = D_MODEL + LANES
N_BLOCKS = -(-(N_ASSIGN + N_TILES * N_EXPERTS * (RUN_ALIGN - 1) + N_EXPERTS * (MOE_BLK - 1)) // MOE_BLK)
N_SLOTS = N_BLOCKS * MOE_BLK
VMEM_LIMIT = 56 * 1024 * 1024

C_A3 = 0
C_CQ = C_A3 + 3 * CONV_DIM
C_CKV = C_CQ + Q_RANK
C_Z = C_CKV + KV_RANK
C_XBC = C_Z + SSM_INNER
C_GATE = C_XBC + SSM_CONV_CH
C_SMALL = C_GATE + 3 * D_MODEL
IN_COLS2 = C_SMALL + LANES
SM_DTF = QK_ROPE
SM_DTB = QK_ROPE + SSM_HEADS
S_CQ = 3 * CONV_DIM
S_CKV = S_CQ + Q_RANK
S_KPE = S_CKV + KV_RANK
S_Z = S_KPE + QK_ROPE
S_XBC = S_Z + SSM_INNER
S_DTF = S_XBC + SSM_CONV_CH
S_GATE = S_DTF + 2 * SSM_HEADS
IN_COLS = S_GATE + 3 * D_MODEL
W_SEGMENTS = ((C_A3, 0, 3 * CONV_DIM), (C_CQ, S_CQ, Q_RANK), (C_CKV, S_CKV, KV_RANK), (C_Z, S_Z, SSM_INNER),
              (C_XBC, S_XBC, SSM_CONV_CH), (C_GATE, S_GATE, 3 * D_MODEL))
W_PIECE = 512


def _rms(x, w):
    return x * lax.rsqrt(jnp.mean(x * x, axis=-1, keepdims=True) + NORM_EPS) * w


def _silu(x):
    return x * jax.nn.sigmoid(x)


def _dot(a, b):
    return jnp.dot(a, b, preferred_element_type=F32)


def _dot_nt(a, b):
    return lax.dot_general(a, b, (((1,), (1,)), ((), ())), preferred_element_type=F32)


def _resident(shape):
    nd = len(shape)
    return pl.BlockSpec(shape, lambda *_: (0,) * nd, pipeline_mode=pl.Buffered(1))


def _mod_row(i):
    return jnp.where(i < CTX_TILES, 0, 1 + (i - CTX_TILES) // LAT_TILES_PER_SEQ)


def _pos_block(i):
    return jnp.where(i < CTX_TILES, 0, 1 + (i - CTX_TILES) % LAT_TILES_PER_SEQ)


def _ada_kernel(c_ref, w_ref, b_ref, o_ref):
    s = _silu(c_ref[...]).astype(BF16)
    o_ref[0] = _dot(s, w_ref[0].astype(BF16)) + b_ref[0]


def _ada_mods(cond8, w_ada, b_ada):
    tn = 1536
    n_mod = 6 * D_MODEL
    return pl.pallas_call(
        _ada_kernel,
        grid=(DEPTH, n_mod // tn),
        in_specs=[
            pl.BlockSpec((SUBLANES, D_MODEL), lambda l, j: (0, 0)),
            pl.BlockSpec((1, D_MODEL, tn), lambda l, j: (l, 0, j)),
            pl.BlockSpec((1, 1, tn), lambda l, j: (l, 0, j)),
        ],
        out_specs=pl.BlockSpec((1, SUBLANES, tn), lambda l, j: (l, 0, j)),
        out_shape=jax.ShapeDtypeStruct((DEPTH, SUBLANES, n_mod), F32),
        compiler_params=pltpu.CompilerParams(dimension_semantics=("arbitrary", "arbitrary")),
        name="ada_mods",
    )(cond8, w_ada, b_ada.reshape(DEPTH, 1, n_mod))


def _relayout_w_in(layer, wt_hbm, w2, stg, small_stg, sem, small_sem):
    pieces = [(dst + p, src + p, min(W_PIECE, width - p))
              for dst, src, width in W_SEGMENTS for p in range(0, width, W_PIECE)]
    fetch = lambda k: pltpu.make_async_copy(wt_hbm.at[layer, pl.ds(pieces[k][1], pieces[k][2]), :],
                                            stg.at[k % 2, pl.ds(0, pieces[k][2]), :], sem.at[k % 2])
    small_stg[...] = jnp.zeros_like(small_stg)
    small_copies = [
        pltpu.make_async_copy(wt_hbm.at[layer, pl.ds(S_KPE, QK_ROPE), :], small_stg.at[pl.ds(0, QK_ROPE), :],
                              small_sem.at[0]),
        pltpu.make_async_copy(wt_hbm.at[layer, pl.ds(S_DTF, 2 * SSM_HEADS), :],
                              small_stg.at[pl.ds(SM_DTF, 2 * SSM_HEADS), :], small_sem.at[1]),
    ]
    for cp in small_copies:
        cp.start()
    fetch(0).start()
    for k, (dst, _, width) in enumerate(pieces):
        if k + 1 < len(pieces):
            fetch(k + 1).start()
        fetch(k).wait()
        w2[:, dst:dst + width] = stg[k % 2, 0:width, :].T.astype(BF16)
    for cp in small_copies:
        cp.wait()
    w2[:, C_SMALL:IN_COLS2] = small_stg[...].T.astype(BF16)


def _rot_partner(x):
    n = x.shape[1]
    lane = lax.broadcasted_iota(I32, x.shape, 1)
    quarter = QK_ROPE // 4
    return jnp.where(lane % (2 * quarter) < quarter, pltpu.roll(x, n - quarter, 1), pltpu.roll(x, quarter, 1))


def _inproj_kernel(layer, x_ref, xp_ref, xn_ref, mod_ref, n1w_ref, w_hbm, qnw_ref, wq_ref, kvnw_ref, wuk_ref,
                   wuv_ref, tile_ref, cosq_ref, sinq_ref, cosk_ref, sink_ref, caw_ref, cw_ref, cb_ref,
                   ta_ref, q_ref, ckv_ref, kf_ref, v_ref, small_ref, z_ref, xact_ref, g_ref,
                   w2, stg, small_stg, sem, small_sem):
    i = pl.program_id(0)

    @pl.when(i == 0)
    def _():
        _relayout_w_in(layer, w_hbm, w2, stg, small_stg, sem, small_sem)

    mod = mod_ref[0]
    shift1 = mod[:, 0:D_MODEL]
    scale1 = mod[:, D_MODEL:2 * D_MODEL]
    x_ext = jnp.concatenate([xp_ref[...], x_ref[...], xn_ref[...]], axis=0)
    h_ext = _rms(x_ext, n1w_ref[...]) * (1.0 + scale1) + shift1
    hb_ext = h_ext.astype(BF16)
    hb = h_ext[SUBLANES:SUBLANES + TM].astype(BF16)

    j = (i - CTX_TILES) % LAT_TILES_PER_SEQ
    is_ctx = i < CTX_TILES
    keep_prev = jnp.where(jnp.logical_or(is_ctx, j == 0), 0.0, 1.0)
    keep_next = jnp.where(jnp.logical_or(is_ctx, j == LAT_TILES_PER_SEQ - 1), 0.0, 1.0)
    row = lax.broadcasted_iota(I32, (TM, 1), 0)
    prev_mask = jnp.where(row == 0, keep_prev, 1.0)
    next_mask = jnp.where(row == TM - 1, keep_next, 1.0)
    ext = TM + 2 * SUBLANES

    def conv3(x, w_ref):
        x_prev = pltpu.roll(x, 1, 0)[SUBLANES:SUBLANES + TM] * prev_mask
        x_next = pltpu.roll(x, ext - 1, 0)[SUBLANES:SUBLANES + TM] * next_mask
        return x_prev * w_ref[0:1, :] + x[SUBLANES:SUBLANES + TM] * w_ref[1:2, :] + x_next * w_ref[2:3, :]

    a3 = _dot(hb_ext, w2[:, C_A3:C_CQ])
    s = a3[:, 2 * CONV_DIM:3 * CONV_DIM] * a3[:, 0:CONV_DIM]
    ta_ref[...] = (a3[SUBLANES:SUBLANES + TM, CONV_DIM:2 * CONV_DIM] * conv3(s, caw_ref)).astype(BF16)
    xact_ref[...] = _silu(conv3(_dot(hb_ext, w2[:, C_XBC:C_GATE]), cw_ref) + cb_ref[...])

    def seg(a, b):
        return _dot(hb, w2[:, a:b])

    cqn = _rms(seg(C_CQ, C_CKV), qnw_ref[...]).astype(BF16)
    qa = _dot(cqn, wq_ref[...])
    qa_rot = _rot_partner(qa)
    for h in range(MLA_HEADS):
        hs = slice(h * HEAD_PAD, (h + 1) * HEAD_PAD)
        q_ref[:, hs] = (qa[:, hs] * cosq_ref[...] + qa_rot[:, hs] * sinq_ref[...]).astype(BF16)

    ckv = _rms(seg(C_CKV, C_Z), kvnw_ref[...])
    ckv_ref[...] = ckv
    ckvb = ckv.astype(BF16)
    small = seg(C_SMALL, IN_COLS2)
    small_ref[...] = small
    kpe = small * cosk_ref[...] + _rot_partner(small) * sink_ref[...]
    kf_ref[...] = (_dot(ckvb, wuk_ref[...]) + _dot(kpe.astype(BF16), tile_ref[...])).astype(BF16)
    v_ref[...] = _dot(ckvb, wuv_ref[...]).astype(BF16)

    z_ref[...] = seg(C_Z, C_XBC)
    g_ref[...] = jax.nn.sigmoid(seg(C_GATE, C_SMALL))


def _inproj(layer, x, mod3, w_in, lw, tabs):
    row = lambda n: pl.BlockSpec((TM, n), lambda i: (i, 0))
    tab = lambda n: pl.BlockSpec((TM, n), lambda i: (_pos_block(i), 0))
    per = TM // SUBLANES
    last = N_TOK // SUBLANES - 1
    qw = MLA_HEADS * HEAD_PAD
    vw = MLA_HEADS * V_HEAD
    out_shape = (
        jax.ShapeDtypeStruct((N_TOK, CONV_DIM), BF16),
        jax.ShapeDtypeStruct((N_TOK, qw), BF16),
        jax.ShapeDtypeStruct((N_TOK, KV_RANK), F32),
        jax.ShapeDtypeStruct((N_TOK, qw), BF16),
        jax.ShapeDtypeStruct((N_TOK, vw), BF16),
        jax.ShapeDtypeStruct((N_TOK, LANES), F32),
        jax.ShapeDtypeStruct((N_TOK, SSM_INNER), F32),
        jax.ShapeDtypeStruct((N_TOK, SSM_CONV_CH), F32),
        jax.ShapeDtypeStruct((N_TOK, 3 * D_MODEL), F32),
    )
    return pl.pallas_call(
        functools.partial(_inproj_kernel, layer),
        grid=(N_TILES,),
        in_specs=[
            row(D_MODEL),
            pl.BlockSpec((SUBLANES, D_MODEL), lambda i: (jnp.maximum(i * per - 1, 0), 0)),
            pl.BlockSpec((SUBLANES, D_MODEL), lambda i: (jnp.minimum((i + 1) * per, last), 0)),
            pl.BlockSpec((1, 1, 6 * D_MODEL), lambda i: (_mod_row(i), 0, 0)),
            _resident((1, D_MODEL)),
            pl.BlockSpec(memory_space=pl.ANY),
            _resident((1, Q_RANK)),
            _resident((Q_RANK, qw)),
            _resident((1, KV_RANK)),
            _resident((KV_RANK, qw)),
            _resident((KV_RANK, vw)),
            _resident((LANES, qw)),
            tab(HEAD_PAD), tab(HEAD_PAD), tab(LANES), tab(LANES),
            _resident((3, CONV_DIM)), _resident((3, SSM_CONV_CH)), _resident((1, SSM_CONV_CH)),
        ],
        out_specs=tuple(row(s.shape[1]) for s in out_shape),
        out_shape=out_shape,
        scratch_shapes=[pltpu.VMEM((D_MODEL, IN_COLS2), BF16), pltpu.VMEM((2, W_PIECE, D_MODEL), F32),
                        pltpu.VMEM((LANES, D_MODEL), F32), pltpu.SemaphoreType.DMA((2,)),
                        pltpu.SemaphoreType.DMA((2,))],
        compiler_params=pltpu.CompilerParams(dimension_semantics=("arbitrary",), vmem_limit_bytes=VMEM_LIMIT),
        name="inproj",
    )(x, x, x, mod3, lw["norm1_w"], w_in, lw["q_norm_w"], lw["wq_a"], lw["kv_norm_w"],
      lw["wuk"], lw["wuv"], tabs["tile"], tabs["cosq"], tabs["sinq"], tabs["cosk"], tabs["sink"],
      lw["conv_a_w"], lw["ssm_conv_w"], lw["ssm_conv_b"])


def _kvcache_kernel(ckv_ref, kpe_ref, wuk_ref, wuv_ref, tile_ref, kf_ref, v_ref):
    ckvb = ckv_ref[...].astype(BF16)
    kf_ref[...] = (_dot(ckvb, wuk_ref[...]) + _dot(kpe_ref[...].astype(BF16), tile_ref[...])).astype(BF16)
    v_ref[...] = _dot(ckvb, wuv_ref[...]).astype(BF16)


def _kvcache(ckv, kpe128, lw, tabs):
    n = ckv.shape[0]
    qw = MLA_HEADS * HEAD_PAD
    vw = MLA_HEADS * V_HEAD
    return pl.pallas_call(
        _kvcache_kernel,
        grid=(n // PAST_LEN,),
        in_specs=[
            pl.BlockSpec((PAST_LEN, KV_RANK), lambda i: (i, 0)),
            pl.BlockSpec((PAST_LEN, LANES), lambda i: (i, 0)),
            _resident((KV_RANK, qw)),
            _resident((KV_RANK, vw)),
            _resident((LANES, qw)),
        ],
        out_specs=(pl.BlockSpec((PAST_LEN, qw), lambda i: (i, 0)), pl.BlockSpec((PAST_LEN, vw), lambda i: (i, 0))),
        out_shape=(jax.ShapeDtypeStruct((n, qw), BF16), jax.ShapeDtypeStruct((n, vw), BF16)),
        compiler_params=pltpu.CompilerParams(dimension_semantics=("arbitrary",)),
        name="kvcache",
    )(ckv, kpe128, lw["wuk"], lw["wuv"], tabs["tile"])


def _attn_heads(q_ref, kv_refs, o_ref, acc_ref):
    for h in range(MLA_HEADS):
        qh = q_ref[:, h * HEAD_PAD:(h + 1) * HEAD_PAD]
        ss = [_dot_nt(qh, k_ref[:, h * HEAD_PAD:(h + 1) * HEAD_PAD]) * MLA_SCALE for k_ref, _ in kv_refs]
        m = functools.reduce(jnp.maximum, [jnp.max(s, axis=-1, keepdims=True) for s in ss])
        ps = [jnp.exp(s - m) for s in ss]
        l = functools.reduce(jnp.add, [jnp.sum(p, axis=-1, keepdims=True) for p in ps])
        o = functools.reduce(jnp.add, [_dot(p.astype(BF16), v_ref[:, h * V_HEAD:(h + 1) * V_HEAD])
                                       for p, (_, v_ref) in zip(ps, kv_refs)])
        acc_ref[:, h * V_HEAD:(h + 1) * V_HEAD] = o / l
    o_ref[...] = acc_ref[...].astype(BF16)


def _attn_ctx_kernel(q_ref, k_ref, v_ref, o_ref, acc_ref):
    _attn_heads(q_ref, [(k_ref, v_ref)], o_ref, acc_ref)


def _attn_lat_kernel(q_ref, k_ref, v_ref, kc_ref, vc_ref, o_ref, acc_ref):
    _attn_heads(q_ref, [(k_ref, v_ref), (kc_ref, vc_ref)], o_ref, acc_ref)


def _attention(q, kf, v, kf_c, v_c):
    qw = MLA_HEADS * HEAD_PAD
    vw = MLA_HEADS * V_HEAD
    att_ctx = pl.pallas_call(
        _attn_ctx_kernel,
        grid=(BATCH,),
        in_specs=[pl.BlockSpec((SEQ, qw), lambda b: (b, 0)), pl.BlockSpec((SEQ, qw), lambda b: (b, 0)),
                  pl.BlockSpec((SEQ, vw), lambda b: (b, 0))],
        out_specs=pl.BlockSpec((SEQ, vw), lambda b: (b, 0)),
        out_shape=jax.ShapeDtypeStruct((N_CTX, vw), BF16),
        scratch_shapes=[pltpu.VMEM((SEQ, vw), F32)],
        compiler_params=pltpu.CompilerParams(dimension_semantics=("arbitrary",)),
        name="attn_ctx",
    )(q, kf, v)
    lat0 = N_CTX // DEC_SEQ
    att_lat = pl.pallas_call(
        _attn_lat_kernel,
        grid=(DEC_BATCH, LAT_TILES_PER_SEQ),
        in_specs=[
            pl.BlockSpec((TM, qw), lambda b, t: (CTX_TILES + b * LAT_TILES_PER_SEQ + t, 0)),
            pl.BlockSpec((DEC_SEQ, qw), lambda b, t: (lat0 + b, 0)),
            pl.BlockSpec((DEC_SEQ, vw), lambda b, t: (lat0 + b, 0)),
            pl.BlockSpec((PAST_LEN, qw), lambda b, t: (b, 0)),
            pl.BlockSpec((PAST_LEN, vw), lambda b, t: (b, 0)),
        ],
        out_specs=pl.BlockSpec((TM, vw), lambda b, t: (b * LAT_TILES_PER_SEQ + t, 0)),
        out_shape=jax.ShapeDtypeStruct((N_LAT, vw), BF16),
        scratch_shapes=[pltpu.VMEM((TM, vw), F32)],
        compiler_params=pltpu.CompilerParams(dimension_semantics=("arbitrary", "arbitrary"),
                                             vmem_limit_bytes=VMEM_LIMIT),
        name="attn_lat",
    )(q, kf, v, kf_c, v_c)
    return att_ctx, att_lat


def _seq_of_chunk(s):
    return jnp.where(s < N_CTX_CHUNKS, s // CTX_CHUNKS_PER_SEQ,
                     BATCH + (s - N_CTX_CHUNKS) // LAT_CHUNKS_PER_SEQ)


def _chunk_in_seq(s):
    return jnp.where(s < N_CTX_CHUNKS, s % CTX_CHUNKS_PER_SEQ, (s - N_CTX_CHUNKS) % LAT_CHUNKS_PER_SEQ)


def _chunks_in_seq(s):
    return jnp.where(s < N_CTX_CHUNKS, CTX_CHUNKS_PER_SEQ, LAT_CHUNKS_PER_SEQ)


def _mirror_chunk(s):
    return s + _chunks_in_seq(s) - 1 - 2 * _chunk_in_seq(s)


def _split3(a):
    a1 = a.astype(BF16)
    r1 = a - a1.astype(F32)
    a2 = r1.astype(BF16)
    a3 = (r1 - a2.astype(F32)).astype(BF16)
    return a1, a2, a3


def _ssd_direction(x_ref, sm_ref, par_ref, st_ref, y_ref, lane0, backward):
    ri = lax.broadcasted_iota(I32, (CHUNK, CHUNK), 0)
    ci = lax.broadcasted_iota(I32, (CHUNK, CHUNK), 1)
    tri = (ci >= ri) if backward else (ci <= ri)
    tri_b = jnp.where(tri, 1.0, 0.0).astype(BF16)
    tot_row = 0 if backward else CHUNK - 1

    dt = jax.nn.softplus(sm_ref[...] + par_ref[0:1, :])
    a = dt * (-jnp.exp(par_ref[1:2, :])) * par_ref[2:3, :]
    a1, a2, a3 = _split3(a)
    acs = _dot(tri_b, a1) + _dot(tri_b, a2) + _dot(tri_b, a3)
    acs_t = acs.T
    dt_t = dt.T
    first_head = lax.broadcasted_iota(I32, (1, LANES), 1) < SSM_HEAD_DIM

    def block_diag(pair):
        return jnp.concatenate([jnp.where(first_head, pair, 0.0), jnp.where(first_head, 0.0, pair)],
                               axis=0).astype(BF16)

    for g in range(SSM_GROUPS):
        b0 = SSM_INNER + g * SSM_STATE
        c0 = SSM_INNER + SSM_GROUPS * SSM_STATE + g * SSM_STATE
        bg = x_ref[:, b0:b0 + SSM_STATE]
        cg = x_ref[:, c0:c0 + SSM_STATE]
        cb = _dot_nt(cg.astype(BF16), bg.astype(BF16))
        bg_t = bg.T
        heads = SSM_HEADS // SSM_GROUPS
        for pr in range(heads // 2):
            h0 = g * heads + 2 * pr
            sl = slice(h0 * SSM_HEAD_DIM, (h0 + 2) * SSM_HEAD_DIM)
            x_bd = block_diag(x_ref[:, sl])
            st_old = st_ref[:, sl]
            within, carried, to_state, keep = [], [], [], []
            for h in (h0, h0 + 1):
                lane = lane0 + h
                col = jnp.broadcast_to(acs[:, lane:lane + 1], (CHUNK, CHUNK))
                row = acs_t[lane:lane + 1, :]
                dt_row = dt_t[lane:lane + 1, :]
                decay = jnp.exp(jnp.where(tri, col - row, -jnp.inf))
                within.append((cb * decay * dt_row).astype(BF16))
                carried.append((cg * jnp.exp(col)).astype(BF16))
                tot = acs[tot_row:tot_row + 1, lane:lane + 1]
                to_state.append((bg_t * (dt_row * jnp.exp(tot - row))).astype(BF16))
                keep.append(jnp.exp(tot))
            lhs = jnp.concatenate(within + carried, axis=1)
            y_ref[:, sl] = _dot(lhs, jnp.concatenate([x_bd, block_diag(st_old)], axis=0))
            st_ref[:, sl] = (st_old * jnp.where(first_head, keep[0], keep[1])
                             + _dot(jnp.concatenate(to_state, axis=1), x_bd))


def _ssd_kernel(xf_ref, xb_ref, smf_ref, smb_ref, if_ref, ib_ref, par_ref,
                yf_ref, yb_ref, sf_ref, sb_ref, stf_ref, stb_ref):
    s = pl.program_id(0)
    c = _chunk_in_seq(s)

    @pl.when(jnp.logical_and(c == 0, s < N_CTX_CHUNKS))
    def _():
        stf_ref[...] = jnp.zeros_like(stf_ref)
        stb_ref[...] = jnp.zeros_like(stb_ref)

    @pl.when(jnp.logical_and(c == 0, s >= N_CTX_CHUNKS))
    def _():
        stf_ref[...] = if_ref[0].T
        stb_ref[...] = ib_ref[0].T

    _ssd_direction(xf_ref, smf_ref, par_ref, stf_ref, yf_ref, SM_DTF, False)
    _ssd_direction(xb_ref, smb_ref, par_ref, stb_ref, yb_ref, SM_DTB, True)

    @pl.when(c == _chunks_in_seq(s) - 1)
    def _():
        sf_ref[0] = stf_ref[...].T
        sb_ref[0] = stb_ref[...].T


def _ssd(xact, small, init_f, init_b, par):
    hp = SSM_INNER
    fwd = lambda n: pl.BlockSpec((CHUNK, n), lambda s: (s, 0))
    bwd = lambda n: pl.BlockSpec((CHUNK, n), lambda s: (_mirror_chunk(s), 0))
    st = pl.BlockSpec((1, hp, SSM_STATE), lambda s: (_seq_of_chunk(s), 0, 0))
    init = pl.BlockSpec((1, hp, SSM_STATE), lambda s: (jnp.maximum(_seq_of_chunk(s) - BATCH, 0), 0, 0))
    return pl.pallas_call(
        _ssd_kernel,
        grid=(N_CHUNKS,),
        in_specs=[fwd(SSM_CONV_CH), bwd(SSM_CONV_CH), fwd(LANES), bwd(LANES), init, init,
                  pl.BlockSpec((SUBLANES, LANES), lambda s: (0, 0))],
        out_specs=(fwd(hp), bwd(hp), st, st),
        out_shape=(jax.ShapeDtypeStruct((N_TOK, hp), F32), jax.ShapeDtypeStruct((N_TOK, hp), F32),
                   jax.ShapeDtypeStruct((N_SEQS, hp, SSM_STATE), F32),
                   jax.ShapeDtypeStruct((N_SEQS, hp, SSM_STATE), F32)),
        scratch_shapes=[pltpu.VMEM((SSM_STATE, hp), F32), pltpu.VMEM((SSM_STATE, hp), F32)],
        compiler_params=pltpu.CompilerParams(dimension_semantics=("arbitrary",)),
        name="ssd",
    )(xact, xact, small, small, init_f, init_b, par)


def _merge_kernel(ta_ref, attc_ref, attl_ref, yf_ref, yb_ref, xs_ref, z_ref, g_ref, x_ref, mod_ref,
                  woa_ref, wom_ref, dsk_ref, snw_ref, wos_ref, wo_ref, n2w_ref, rw_ref, rb_ref,
                  x1_ref, h2_ref, ti_ref, tg_ref, cnt_ref):
    mod = mod_ref[0]
    gate1 = mod[:, 2 * D_MODEL:3 * D_MODEL]
    shift2 = mod[:, 3 * D_MODEL:4 * D_MODEL]
    scale2 = mod[:, 4 * D_MODEL:5 * D_MODEL]
    y_a = _dot(ta_ref[...], woa_ref[...])
    att = jnp.where(pl.program_id(0) < CTX_TILES, attc_ref[...].astype(F32), attl_ref[...].astype(F32))
    y_b = _dot(att.astype(BF16), wom_ref[...])
    y_ssm = (yf_ref[...] + yb_ref[...] + dsk_ref[...] * xs_ref[...]) * _silu(z_ref[...])
    y_c = _dot(_rms(y_ssm, snw_ref[...]).astype(BF16), wos_ref[...])
    merged = (g_ref[:, 0:D_MODEL] * y_a + g_ref[:, D_MODEL:2 * D_MODEL] * y_b
              + g_ref[:, 2 * D_MODEL:3 * D_MODEL] * y_c)
    x1 = x_ref[...] + gate1 * _dot(merged.astype(BF16), wo_ref[...])
    x1_ref[...] = x1
    h2 = _rms(x1, n2w_ref[...]) * (1.0 + scale2) + shift2
    h2b = h2.astype(BF16)
    h2_ref[...] = h2b

    logits = (_dot(h2b, rw_ref[...]) + rb_ref[...]).T[0:N_EXPERTS, :]
    expert = lax.broadcasted_iota(I32, (N_EXPERTS, TM), 0)
    ids, vals = [], []
    for k in range(TOP_K):
        m = jnp.max(logits, axis=0, keepdims=True)
        idx = jnp.min(jnp.where(logits == m, expert, N_EXPERTS), axis=0, keepdims=True)
        ids.append(idx)
        vals.append(m)
        logits = jnp.where(expert == idx, -jnp.inf, logits)
    es = [jnp.exp(v - vals[0]) for v in vals]
    denom = functools.reduce(jnp.add, es)
    srow = lax.broadcasted_iota(I32, (LANES, TM), 0)
    ti_t = jnp.zeros((LANES, TM), F32)
    tg_t = jnp.zeros((LANES, TM), F32)
    chosen_t = jnp.zeros((LANES, TM), F32)
    for k in range(TOP_K):
        ti_t = jnp.where(srow == k, ids[k].astype(F32), ti_t)
        tg_t = jnp.where(srow == k, es[k] / denom, tg_t)
        chosen_t = jnp.where(srow == ids[k], 1.0, chosen_t)
    ti_ref[...] = ti_t.T.astype(I32)
    tg_ref[...] = tg_t.T
    cnt_ref[0] = _dot_nt(jnp.ones((SUBLANES, TM), BF16), chosen_t.astype(BF16)).astype(I32)


def _merge(ta, att_ctx, att_lat, yf, yb, xact, z, g, x, mod3, lw):
    row = lambda n: pl.BlockSpec((TM, n), lambda i: (i, 0))
    vw = MLA_HEADS * V_HEAD
    out_shape = (jax.ShapeDtypeStruct((N_TOK, D_MODEL), F32), jax.ShapeDtypeStruct((N_TOK, D_MODEL), BF16),
                 jax.ShapeDtypeStruct((N_TOK, LANES), I32), jax.ShapeDtypeStruct((N_TOK, LANES), F32))
    cnt_shape = jax.ShapeDtypeStruct((N_TILES, SUBLANES, LANES), I32)
    cnt_spec = pl.BlockSpec((1, SUBLANES, LANES), lambda i: (i, 0, 0))
    return pl.pallas_call(
        _merge_kernel,
        grid=(N_TILES,),
        in_specs=[
            row(CONV_DIM),
            pl.BlockSpec((TM, vw), lambda i: (jnp.minimum(i, CTX_TILES - 1), 0)),
            pl.BlockSpec((TM, vw), lambda i: (jnp.maximum(i - CTX_TILES, 0), 0)),
            row(SSM_INNER), row(SSM_INNER), row(SSM_INNER),
            row(SSM_INNER), row(3 * D_MODEL), row(D_MODEL),
            pl.BlockSpec((1, 1, 6 * D_MODEL), lambda i: (_mod_row(i), 0, 0)),
            _resident((CONV_DIM, D_MODEL)), _resident((MLA_HEADS * V_HEAD, D_MODEL)),
            _resident((1, SSM_INNER)), _resident((1, SSM_INNER)), _resident((SSM_INNER, D_MODEL)),
            _resident((D_MODEL, D_MODEL)), _resident((1, D_MODEL)),
            _resident((D_MODEL, LANES)), _resident((1, LANES)),
        ],
        out_specs=tuple(row(s.shape[1]) for s in out_shape) + (cnt_spec,),
        out_shape=out_shape + (cnt_shape,),
        compiler_params=pltpu.CompilerParams(dimension_semantics=("arbitrary",), vmem_limit_bytes=VMEM_LIMIT),
        name="merge",
    )(ta, att_ctx, att_lat, yf, yb, xact, z, g, x, mod3, lw["w_out_a"], lw["w_o_mla"], lw["d_skip"], lw["ssm_norm_w"],
      lw["w_o_ssm"], lw["w_o"], lw["norm2_w"], lw["router_w"], lw["router_b"])


def _run_copies(cnt_ref, src_ref, dst_ref, first, count, bits, make_copy, start):
    def body(e, carry):
        n = cnt_ref[first + e]
        s0 = src_ref[first + e] if src_ref is not None else 0
        d0 = dst_ref[first + e]

        def pieces(some_bits):
            for b in some_bits:
                above = (n >> (b + 1)) << (b + 1)

                @pl.when(((n >> b) & 1) == 1)
                def _():
                    cp = make_copy(pl.multiple_of(s0 + above, RUN_ALIGN), pl.multiple_of(d0 + above, RUN_ALIGN),
                                   1 << b)
                    if start:
                        cp.start()
                    else:
                        cp.wait()

        large = [b for b in bits if b >= COMMON_BITS]

        @pl.when(n >= (1 << COMMON_BITS))
        def _():
            pieces(large)
        pieces([b for b in bits if b < COMMON_BITS])
        return carry
    lax.fori_loop(0, count, body, 0)


def _wait_rows(total, make_copy):
    for b in WAIT_BITS:
        @pl.when(((total >> b) & 1) == 1)
        def _():
            make_copy(0, 0, 1 << b).wait()


def _dispatch_kernel(cnt_ref, off_ref, run_ref, tcnt_ref, tdst_ref, nu_ref, h2_ref, ti_ref, tg_ref, offv_ref,
                     xs_ref, lp_ref, stage, zeros, perm_s, ghi_s, glo_s, sem, semz):
    i = pl.program_id(0)
    slot = i % 2

    def copy_out(s_):
        return lambda s, d, n: pltpu.make_async_copy(stage.at[s_, pl.ds(s, n), :], xs_ref.at[pl.ds(d, n), :],
                                                     sem.at[s_])

    def tile_rows(t):
        last = t * N_EXPERTS + N_EXPERTS - 1
        return off_ref[last] + cnt_ref[last]

    @pl.when(i == 0)
    def _():
        zeros[...] = jnp.zeros_like(zeros)
        zero_out = lambda s, d, n: pltpu.make_async_copy(zeros.at[pl.ds(0, n), :], xs_ref.at[pl.ds(d, n), :], semz)
        _run_copies(tcnt_ref, None, tdst_ref, 0, N_EXPERTS, TAIL_BITS, zero_out, True)
        _run_copies(tcnt_ref, None, tdst_ref, 0, N_EXPERTS, TAIL_BITS, zero_out, False)
        zrows = zeros.shape[0]

        def unused_blocks(start):
            def body(b, carry):
                for part in range(MOE_BLK // zrows):
                    cp = zero_out(0, pl.multiple_of(b * MOE_BLK + part * zrows, RUN_ALIGN), zrows)
                    if start:
                        cp.start()
                    else:
                        cp.wait()
                return carry
            lax.fori_loop(nu_ref[0], N_BLOCKS, body, 0)
        unused_blocks(True)
        unused_blocks(False)

    @pl.when(i >= 2)
    def _():
        _wait_rows(tile_rows(i - 2), copy_out(slot))

    lane = lax.broadcasted_iota(I32, (TM, LANES), 1)
    picks = [jnp.where(lane == ti_ref[:, k:k + 1], 1.0, 0.0) for k in range(TOP_K)]
    ri = lax.broadcasted_iota(I32, (TM, TM), 0)
    ci = lax.broadcasted_iota(I32, (TM, TM), 1)
    earlier = jnp.where(ci < ri, 1.0, 0.0).astype(BF16)
    base = _dot(earlier, functools.reduce(jnp.add, picks).astype(BF16)) + offv_ref[0][0:1, :]
    lp = jnp.full((TM, LANES), -1.0, F32)
    for k in range(TOP_K):
        lp = jnp.where(lane == k, jnp.sum(picks[k] * base, axis=-1, keepdims=True), lp)
    lp_ref[...] = lp.astype(I32)

    lp_t = lp.T
    tg_t = tg_ref[...].T
    for c in range(STAGE_ROWS // LANES):
        rows = slice(c * LANES, (c + 1) * LANES)
        row = (lax.broadcasted_iota(I32, (LANES, TM), 0) + c * LANES).astype(F32)
        perm = jnp.zeros((LANES, TM), F32)
        gates = jnp.zeros((LANES, TM), F32)
        for k in range(TOP_K):
            hit = row == lp_t[k:k + 1, :]
            perm = jnp.where(hit, 1.0, perm)
            gates = jnp.where(hit, tg_t[k:k + 1, :], gates)
        perm_s[rows, :] = perm.astype(BF16)
        g_hi = gates.astype(BF16)
        ghi_s[rows, :] = g_hi
        glo_s[rows, :] = (gates - g_hi.astype(F32)).astype(BF16)
    stage[slot, :, 0:D_MODEL] = _dot(perm_s[...], h2_ref[...])
    ones = jnp.ones((TM, LANES), BF16)
    stage[slot, :, D_MODEL:XS_COLS] = _dot(ghi_s[...], ones) + _dot(glo_s[...], ones)

    _run_copies(cnt_ref, off_ref, run_ref, i * N_EXPERTS, N_EXPERTS, RUN_BITS, copy_out(slot), True)

    @pl.when(i == N_TILES - 1)
    def _():
        _wait_rows(tile_rows(i - 1), copy_out(1 - slot))
        _wait_rows(tile_rows(i), copy_out(slot))


def _dispatch(rt, h2, top_i, top_g):
    row = lambda n: pl.BlockSpec((TM, n), lambda i, *_: (i, 0))
    return pl.pallas_call(
        _dispatch_kernel,
        grid_spec=pltpu.PrefetchScalarGridSpec(
            num_scalar_prefetch=6,
            grid=(N_TILES,),
            in_specs=[row(D_MODEL), row(LANES), row(LANES),
                      pl.BlockSpec((1, SUBLANES, LANES), lambda i, *_: (i, 0, 0))],
            out_specs=(pl.BlockSpec(memory_space=pl.ANY), row(LANES)),
            scratch_shapes=[
                pltpu.VMEM((2, STAGE_ROWS, XS_COLS), F32),
                pltpu.VMEM((1 << TAIL_BITS[0], XS_COLS), F32),
                pltpu.VMEM((STAGE_ROWS, TM), BF16), pltpu.VMEM((STAGE_ROWS, TM), BF16),
                pltpu.VMEM((STAGE_ROWS, TM), BF16),
                pltpu.SemaphoreType.DMA((2,)),
                pltpu.SemaphoreType.DMA,
            ],
        ),
        out_shape=(jax.ShapeDtypeStruct((N_SLOTS, XS_COLS), F32), jax.ShapeDtypeStruct((N_TOK, LANES), I32)),
        compiler_params=pltpu.CompilerParams(dimension_semantics=("arbitrary",), vmem_limit_bytes=VMEM_LIMIT),
        name="dispatch",
    )(rt["cnt"], rt["off"], rt["run"], rt["tail_cnt"], rt["tail_dst"], rt["n_used"], h2, top_i, top_g, rt["off_v"])


def _moe_kernel(layer, be_ref, nxt_ref, nv_ref, nu_ref, x_ref, wgu_hbm, wdn_hbm, bg_ref, bu_ref, bd_ref, sel_ref,
                y_ref, wgu_ref, wdn_ref, wgu_s, wdn_s, sem):
    i = pl.program_id(0)
    n_used = nu_ref[0]

    def fetch(e):
        return (pltpu.make_async_copy(wgu_hbm.at[layer, e], wgu_ref, sem.at[0]),
                pltpu.make_async_copy(wdn_hbm.at[layer, e], wdn_ref, sem.at[1]))

    @pl.when(i < n_used)
    def _():
        @pl.when(jnp.logical_or(i == 0, be_ref[i] != be_ref[jnp.maximum(i - 1, 0)]))
        def _():
            @pl.when(i == 0)
            def _():
                for cp in fetch(be_ref[0]):
                    cp.start()
            for cp in fetch(be_ref[i]):
                cp.wait()
            half = LANES
            for c in range(2 * EXPERT_FF // (2 * half)):
                r = _dot(wgu_ref[:, c * 2 * half:(c + 1) * 2 * half].astype(BF16), sel_ref[...])
                wgu_s[:, c * half:(c + 1) * half] = r[:, 0:half].astype(BF16)
                wgu_s[:, EXPERT_FF + c * half:EXPERT_FF + (c + 1) * half] = r[:, half:2 * half].astype(BF16)
            wdn_s[...] = wdn_ref[...].astype(BF16)
            nxt = nxt_ref[i]

            @pl.when(nxt >= 0)
            def _():
                for cp in fetch(nxt):
                    cp.start(priority=1)

        def expert_rows(rows):
            gu = _dot(x_ref[0:rows, 0:D_MODEL].astype(BF16), wgu_s[...])
            gate = jnp.minimum(gu[:, 0:EXPERT_FF] + bg_ref[0], SWIGLU_LIMIT)
            up = jnp.clip(gu[:, EXPERT_FF:2 * EXPERT_FF] + bu_ref[0], -SWIGLU_LIMIT, SWIGLU_LIMIT)
            act = gate * jax.nn.sigmoid(SWIGLU_ALPHA * gate) * (up + 1.0)
            y = _dot(act.astype(BF16), wdn_s[...]) + bd_ref[0]
            slot_gate = x_ref[0:rows, D_MODEL:XS_COLS]
            for j in range(D_MODEL // LANES):
                y_ref[0:rows, j * LANES:(j + 1) * LANES] = y[:, j * LANES:(j + 1) * LANES] * slot_gate

        half = MOE_BLK // 2

        @pl.when(nv_ref[i] > half)
        def _():
            expert_rows(MOE_BLK)

        @pl.when(nv_ref[i] <= half)
        def _():
            expert_rows(half)
            y_ref[half:MOE_BLK, :] = jnp.zeros((MOE_BLK - half, D_MODEL), F32)

    @pl.when(i >= n_used)
    def _():
        y_ref[...] = jnp.zeros_like(y_ref)


def _bias_split_kernel(b_ref, sel_ref, o_ref):
    for c in range(2 * EXPERT_FF // (2 * LANES)):
        terms = _split3(b_ref[:, c * 2 * LANES:(c + 1) * 2 * LANES])
        r = functools.reduce(jnp.add, [_dot(t, sel_ref[...]) for t in terms])
        o_ref[:, c * LANES:(c + 1) * LANES] = r[:, 0:LANES]
        o_ref[:, EXPERT_FF + c * LANES:EXPERT_FF + (c + 1) * LANES] = r[:, LANES:2 * LANES]


def _bias_split(b_gu, sel):
    n = DEPTH * N_EXPERTS
    out = pl.pallas_call(
        _bias_split_kernel,
        out_shape=jax.ShapeDtypeStruct((n, 2 * EXPERT_FF), F32),
        name="bias_split",
    )(b_gu.reshape(n, 2 * EXPERT_FF), sel)
    return out.reshape(n, 1, 2 * EXPERT_FF)


def _moe(layer, rt, xs, w_gu, w_down, b_gu_split, b_down, sel):
    first = layer * N_EXPERTS
    bg = pl.BlockSpec((1, 1, EXPERT_FF), lambda i, be, *_: (first + be[i], 0, 0))
    bu = pl.BlockSpec((1, 1, EXPERT_FF), lambda i, be, *_: (first + be[i], 0, 1))
    bd = pl.BlockSpec((1, 1, D_MODEL), lambda i, be, *_: (first + be[i], 0, 0))
    return pl.pallas_call(
        functools.partial(_moe_kernel, layer),
        grid_spec=pltpu.PrefetchScalarGridSpec(
            num_scalar_prefetch=4,
            grid=(N_BLOCKS,),
            in_specs=[
                pl.BlockSpec((MOE_BLK, XS_COLS), lambda i, be, nx, nv, nu: (jnp.minimum(i, nu[0] - 1), 0)),
                pl.BlockSpec(memory_space=pl.ANY),
                pl.BlockSpec(memory_space=pl.ANY),
                bg, bu, bd,
                pl.BlockSpec((2 * LANES, 2 * LANES), lambda i, *_: (0, 0)),
            ],
            out_specs=pl.BlockSpec((MOE_BLK, D_MODEL), lambda i, *_: (i, 0)),
            scratch_shapes=[
                pltpu.VMEM((D_MODEL, 2 * EXPERT_FF), F32),
                pltpu.VMEM((EXPERT_FF, D_MODEL), F32),
                pltpu.VMEM((D_MODEL, 2 * EXPERT_FF), BF16),
                pltpu.VMEM((EXPERT_FF, D_MODEL), BF16),
                pltpu.SemaphoreType.DMA((2,)),
            ],
        ),
        out_shape=jax.ShapeDtypeStruct((N_SLOTS, D_MODEL), F32),
        compiler_params=pltpu.CompilerParams(dimension_semantics=("arbitrary",), vmem_limit_bytes=VMEM_LIMIT),
        name="moe",
    )(rt["blk_e"], rt["blk_next"], rt["blk_rows"], rt["n_used"], xs, w_gu, w_down, b_gu_split, b_gu_split,
      b_down.reshape(DEPTH * N_EXPERTS, 1, D_MODEL), sel)


def _combine_kernel(cnt_ref, off_ref, run_ref, y_ref, lp_ref, x1_ref, mod_ref, fw_ref, x2_ref, yc_ref, yl_ref,
                    stage, mine_s, sem):
    i = pl.program_id(0)
    slot = i % 2

    def copy_in(s_):
        return lambda s, d, n: pltpu.make_async_copy(y_ref.at[pl.ds(d, n), :], stage.at[s_, pl.ds(s, n), :],
                                                     sem.at[s_])

    @pl.when(i == 0)
    def _():
        stage[...] = jnp.zeros_like(stage)
        _run_copies(cnt_ref, off_ref, run_ref, 0, N_EXPERTS, RUN_BITS, copy_in(0), True)

    @pl.when(i + 1 < N_TILES)
    def _():
        _run_copies(cnt_ref, off_ref, run_ref, (i + 1) * N_EXPERTS, N_EXPERTS, RUN_BITS, copy_in(1 - slot), True)

    last = i * N_EXPERTS + N_EXPERTS - 1
    _wait_rows(off_ref[last] + cnt_ref[last], copy_in(slot))
    lane = lax.broadcasted_iota(I32, (TM, LANES), 1)
    mine = [jnp.broadcast_to(lp_ref[:, k:k + 1], (TM, LANES)) for k in range(TOP_K)]
    for c in range(STAGE_ROWS // LANES):
        hit = jnp.zeros((TM, LANES), F32)
        for k in range(TOP_K):
            hit = jnp.where(lane + c * LANES == mine[k], 1.0, hit)
        mine_s[:, c * LANES:(c + 1) * LANES] = hit.astype(BF16)
    moe = _dot(mine_s[...], stage[slot].astype(BF16))
    gate2 = mod_ref[0][:, 5 * D_MODEL:6 * D_MODEL]
    x2 = x1_ref[...] + gate2 * moe
    x2_ref[...] = x2
    y_norm = _rms(x2, fw_ref[...])

    @pl.when(i < CTX_TILES)
    def _():
        yc_ref[...] = y_norm

    @pl.when(i >= CTX_TILES)
    def _():
        yl_ref[...] = y_norm


def _combine(rt, y_slots, lp, x1, mod3, final_w):
    row = lambda n: pl.BlockSpec((TM, n), lambda i, *_: (i, 0))
    return pl.pallas_call(
        _combine_kernel,
        grid_spec=pltpu.PrefetchScalarGridSpec(
            num_scalar_prefetch=3,
            grid=(N_TILES,),
            in_specs=[
                pl.BlockSpec(memory_space=pl.ANY),
                row(LANES), row(D_MODEL),
                pl.BlockSpec((1, 1, 6 * D_MODEL), lambda i, *_: (_mod_row(i), 0, 0)),
                pl.BlockSpec((1, D_MODEL), lambda i, *_: (0, 0)),
            ],
            out_specs=(row(D_MODEL),
                       pl.BlockSpec((TM, D_MODEL), lambda i, *_: (jnp.minimum(i, CTX_TILES - 1), 0)),
                       pl.BlockSpec((TM, D_MODEL), lambda i, *_: (jnp.maximum(i - CTX_TILES, 0), 0))),
            scratch_shapes=[pltpu.VMEM((2, STAGE_ROWS, D_MODEL), F32), pltpu.VMEM((TM, STAGE_ROWS), BF16),
                            pltpu.SemaphoreType.DMA((2,))],
        ),
        out_shape=(jax.ShapeDtypeStruct((N_TOK, D_MODEL), F32), jax.ShapeDtypeStruct((N_CTX, D_MODEL), F32),
                   jax.ShapeDtypeStruct((N_LAT, D_MODEL), F32)),
        compiler_params=pltpu.CompilerParams(dimension_semantics=("arbitrary",), vmem_limit_bytes=VMEM_LIMIT),
        name="combine",
    )(rt["cnt"], rt["off"], rt["run"], y_slots, lp, x1, mod3, final_w)


def _rope_tables():
    rows = DEC_SEQ // GRID_W
    t = jnp.arange(rows * GRID_W)
    row = (t // GRID_W).astype(F32)
    col = (t % GRID_W).astype(F32)
    half = QK_ROPE // 2
    inv = ROPE_BASE ** (-jnp.arange(0, half, 2, dtype=F32) / half)
    ang_r, ang_c = row[:, None] * inv, col[:, None] * inv
    cos32 = jnp.concatenate([jnp.cos(ang_r), jnp.cos(ang_r), jnp.cos(ang_c), jnp.cos(ang_c)], axis=-1)
    sin32 = jnp.concatenate([-jnp.sin(ang_r), jnp.sin(ang_r), -jnp.sin(ang_c), jnp.sin(ang_c)], axis=-1)
    cos32 = jnp.concatenate([jnp.ones((TM, QK_ROPE), F32), cos32], axis=0)
    sin32 = jnp.concatenate([jnp.zeros((TM, QK_ROPE), F32), sin32], axis=0)
    n = cos32.shape[0]
    pad = HEAD_PAD - QK_NOPE - QK_ROPE
    cos_h = jnp.concatenate([jnp.ones((n, QK_NOPE), F32), cos32, jnp.zeros((n, pad), F32)], axis=-1)
    sin_h = jnp.concatenate([jnp.zeros((n, QK_NOPE), F32), sin32, jnp.zeros((n, pad), F32)], axis=-1)
    zeros = jnp.zeros((n, LANES - QK_ROPE), F32)
    j = jnp.arange(QK_ROPE)
    tile = jnp.zeros((LANES, MLA_HEADS, HEAD_PAD), F32).at[j, :, QK_NOPE + j].set(1.0)
    return {
        "cosq": cos_h, "sinq": sin_h,
        "cosk": jnp.concatenate([cos32, zeros], axis=-1), "sink": jnp.concatenate([sin32, zeros], axis=-1),
        "tile": tile.reshape(LANES, MLA_HEADS * HEAD_PAD).astype(BF16),
    }


def _layer_weights(p, l):
    hd = QK_NOPE + QK_ROPE
    pad = HEAD_PAD - hd
    wq = p["w_uq"][l].reshape(Q_RANK, MLA_HEADS, hd)
    wq_a = jnp.pad(wq, ((0, 0), (0, 0), (0, pad)))
    wuk = jnp.pad(p["w_uk"][l], ((0, 0), (0, 0), (0, HEAD_PAD - QK_NOPE)))
    rw = jnp.pad(p["router_w"][l], ((0, 0), (0, LANES - N_EXPERTS)))
    rb = jnp.concatenate([p["router_b"][l], jnp.full((LANES - N_EXPERTS,), -jnp.inf, F32)])
    return {
        "norm1_w": p["norm1_w"][l][None], "q_norm_w": p["q_norm_w"][l][None],
        "wq_a": wq_a.reshape(Q_RANK, -1).astype(BF16),
        "kv_norm_w": p["kv_norm_w"][l][None],
        "wuk": wuk.reshape(KV_RANK, -1).astype(BF16), "wuv": p["w_uv"][l].reshape(KV_RANK, -1).astype(BF16),
        "conv_a_w": p["conv_a_w"][l], "ssm_conv_w": p["ssm_conv_w"][l], "ssm_conv_b": p["ssm_conv_b"][l][None],
        "w_out_a": p["w_out_a"][l].astype(BF16), "w_o_mla": p["w_o_mla"][l].astype(BF16),
        "d_skip": jnp.repeat(p["d_skip"][l], SSM_HEAD_DIM)[None], "ssm_norm_w": p["ssm_norm_w"][l][None],
        "w_o_ssm": p["w_o_ssm"][l].astype(BF16), "w_o": p["w_o"][l].astype(BF16),
        "norm2_w": p["norm2_w"][l][None], "router_w": rw.astype(BF16), "router_b": rb[None],
    }


def _ssd_params(p, l):
    z = lambda n: jnp.zeros((n,), F32)
    lanes = lambda f, b: jnp.concatenate([z(SM_DTF), f, b, z(LANES - SM_DTB - SSM_HEADS)])
    ones = jnp.ones((SSM_HEADS,), F32)
    rows = [lanes(p["dt_bias_fwd"][l], p["dt_bias_bwd"][l]), lanes(p["a_log_fwd"][l], p["a_log_bwd"][l]),
            lanes(ones, ones)]
    return jnp.concatenate([jnp.stack(rows), jnp.zeros((SUBLANES - 3, LANES), F32)], axis=0)


def _routing(cnt_tiles):
    cnt = cnt_tiles[:, 0, 0:N_EXPERTS]
    cnt = (cnt + RUN_ALIGN - 1) // RUN_ALIGN * RUN_ALIGN
    per_expert = jnp.sum(cnt, axis=0)
    padded = (per_expert + MOE_BLK - 1) // MOE_BLK * MOE_BLK
    pad_end = jnp.cumsum(padded)
    pad_start = pad_end - padded
    run = pad_start[None, :] + jnp.cumsum(cnt, axis=0) - cnt
    off = jnp.cumsum(cnt, axis=1) - cnt
    starts = jnp.arange(N_BLOCKS, dtype=I32) * MOE_BLK
    blk_e = jnp.minimum(jnp.sum((pad_end[None, :] <= starts[:, None]).astype(I32), axis=1), N_EXPERTS - 1)
    off_v = jnp.zeros((N_TILES, SUBLANES, LANES), F32).at[:, 0, 0:N_EXPERTS].set(off.astype(F32))
    ids = jnp.arange(N_EXPERTS, dtype=I32)
    later = jnp.logical_and(ids[None, :] > ids[:, None], padded[None, :] > 0)
    nxt = jnp.min(jnp.where(later, ids[None, :], N_EXPERTS), axis=1)
    nxt = jnp.where(nxt == N_EXPERTS, -1, nxt)
    per_block = lambda v: jnp.sum(jnp.where(blk_e[:, None] == ids[None, :], v[None, :], 0), axis=1)
    blk_rows = jnp.clip(per_block(pad_start + per_expert) - starts, 0, MOE_BLK)
    return {
        "blk_next": per_block(nxt).astype(I32), "blk_rows": blk_rows.astype(I32),
        "cnt": cnt.reshape(-1).astype(I32), "off": off.reshape(-1).astype(I32), "run": run.reshape(-1).astype(I32),
        "tail_cnt": (padded - per_expert).astype(I32), "tail_dst": (pad_start + per_expert).astype(I32),
        "blk_e": blk_e.astype(I32), "n_used": (pad_end[-1] // MOE_BLK).astype(I32).reshape(1), "off_v": off_v,
    }


def _deinterleave_matrix():
    k = jnp.arange(2 * LANES)[:, None]
    n = jnp.arange(2 * LANES)[None, :]
    src = jnp.where(n < LANES, 2 * n, 2 * (n - LANES) + 1)
    return (k == src).astype(BF16)


def kernel(x_prompt, x_sample, cache_ckv, cache_kpe, state_ssm_fwd, state_ssm_bwd, c, c_ctx, w_ada, b_ada, norm1_w, w_in, conv_a_w, w_out_a, q_norm_w, w_uq, kv_norm_w, w_uk, w_uv, w_o_mla, ssm_conv_w, ssm_conv_b, dt_bias_fwd, dt_bias_bwd, a_log_fwd, a_log_bwd, d_skip, ssm_norm_w, w_o_ssm, w_o, norm2_w, router_w, router_b, w_gu, b_gu, w_down, b_down, final_norm_w):
    p = dict(norm1_w=norm1_w, w_in=w_in, conv_a_w=conv_a_w, w_out_a=w_out_a, q_norm_w=q_norm_w, w_uq=w_uq,
             kv_norm_w=kv_norm_w, w_uk=w_uk, w_uv=w_uv, w_o_mla=w_o_mla, ssm_conv_w=ssm_conv_w,
             ssm_conv_b=ssm_conv_b, dt_bias_fwd=dt_bias_fwd, dt_bias_bwd=dt_bias_bwd, a_log_fwd=a_log_fwd,
             a_log_bwd=a_log_bwd, d_skip=d_skip, ssm_norm_w=ssm_norm_w, w_o_ssm=w_o_ssm, w_o=w_o,
             norm2_w=norm2_w, router_w=router_w, router_b=router_b, b_gu=b_gu, b_down=b_down)
    x = jnp.concatenate([x_prompt.reshape(N_CTX, D_MODEL), x_sample.reshape(N_LAT, D_MODEL)], axis=0)
    cond8 = jnp.concatenate([c_ctx[None], c, jnp.zeros((SUBLANES - 1 - DEC_BATCH, D_MODEL), F32)], axis=0)
    mods = _ada_mods(cond8, w_ada, b_ada)
    tabs = _rope_tables()
    sel = _deinterleave_matrix()
    b_gu_split = _bias_split(b_gu, sel)
    w_in_t = jnp.swapaxes(w_in, 1, 2)
    final_w = final_norm_w[None]
    hp = SSM_INNER

    ckv_out, kpe_out, sf_out, sb_out = [], [], [], []
    y_ctx = y_lat = None
    for l in range(DEPTH):
        lw = _layer_weights(p, l)
        mod3 = mods[l].reshape(SUBLANES, 1, 6 * D_MODEL)
        ta, q, ckv, kf, v, small, z, xact, g = _inproj(l, x, mod3, w_in_t, lw, tabs)
        kpe_c = jnp.pad(cache_kpe[:, l].reshape(DEC_BATCH * PAST_LEN, QK_ROPE), ((0, 0), (0, LANES - QK_ROPE)))
        kf_c, v_c = _kvcache(cache_ckv[:, l].reshape(DEC_BATCH * PAST_LEN, KV_RANK), kpe_c, lw, tabs)
        att_ctx, att_lat = _attention(q, kf, v, kf_c, v_c)
        init_f = state_ssm_fwd[:, l].reshape(DEC_BATCH, hp, SSM_STATE)
        init_b = state_ssm_bwd[:, l].reshape(DEC_BATCH, hp, SSM_STATE)
        yf, yb, sf, sb = _ssd(xact, small, init_f, init_b, _ssd_params(p, l))
        x1, h2, top_i, top_g, cnt_tiles = _merge(ta, att_ctx, att_lat, yf, yb, xact, z, g, x, mod3, lw)
        rt = _routing(cnt_tiles)
        xs, lp = _dispatch(rt, h2, top_i, top_g)
        y_slots = _moe(l, rt, xs, w_gu, w_down, b_gu_split, b_down, sel)
        x, y_ctx, y_lat = _combine(rt, y_slots, lp, x1, mod3, final_w)
        ckv_out.append(ckv[:N_CTX].reshape(BATCH, SEQ, KV_RANK))
        kpe_out.append(small[:N_CTX, 0:QK_ROPE].reshape(BATCH, SEQ, QK_ROPE))
        sf_out.append(sf[:BATCH].reshape(BATCH, SSM_HEADS, SSM_HEAD_DIM, SSM_STATE))
        sb_out.append(sb[:BATCH].reshape(BATCH, SSM_HEADS, SSM_HEAD_DIM, SSM_STATE))

    y_prompt = y_ctx.reshape(BATCH, SEQ, D_MODEL)
    y_sample = y_lat.reshape(DEC_BATCH, DEC_SEQ, D_MODEL)
    return (y_prompt, y_sample, jnp.stack(ckv_out, axis=1), jnp.stack(kpe_out, axis=1),
            jnp.stack(sf_out, axis=1), jnp.stack(sb_out, axis=1))
```

```python
import functools
import math

import jax
import jax.numpy as jnp
from jax import lax
from jax.experimental import pallas as pl
from jax.experimental.pallas import tpu as pltpu

F32 = jnp.float32
BF16 = jnp.bfloat16
I32 = jnp.int32

D_MODEL = 1024
BATCH = 16
SEQ = 256
DEPTH = 2
DEC_BATCH = 2
DEC_SEQ = 2048
PAST_LEN = 512
GRID_W = 64
NORM_EPS = 1e-6
CONV_DIM = 512
MLA_HEADS = 8
Q_RANK = 384
KV_RANK = 256
QK_NOPE = 64
QK_ROPE = 32
V_HEAD = 64
ROPE_BASE = 10000.0
MLA_SCALE = (QK_NOPE + QK_ROPE) ** -0.5
SSM_HEADS = 16
SSM_HEAD_DIM = 64
SSM_INNER = SSM_HEADS * SSM_HEAD_DIM
SSM_GROUPS = 2
SSM_STATE = 128
SSM_CONV_CH = SSM_INNER + 2 * SSM_GROUPS * SSM_STATE
N_EXPERTS = 32
TOP_K = 4
EXPERT_FF = D_MODEL
SWIGLU_ALPHA = 1.702
SWIGLU_LIMIT = 7.0

N_CTX = BATCH * SEQ
N_LAT = DEC_BATCH * DEC_SEQ
N_TOK = N_CTX + N_LAT
N_SEQS = BATCH + DEC_BATCH

LANES = 128
SUBLANES = 8
HEAD_PAD = 128
TM = 256
N_TILES = N_TOK // TM
CTX_TILES = N_CTX // TM
LAT_TILES_PER_SEQ = DEC_SEQ // TM
CHUNK = 128
CTX_CHUNKS_PER_SEQ = SEQ // CHUNK
LAT_CHUNKS_PER_SEQ = DEC_SEQ // CHUNK
N_CTX_CHUNKS = N_CTX // CHUNK
N_CHUNKS = N_TOK // CHUNK
MOE_BLK = 256
N_ASSIGN = N_TOK * TOP_K
RUN_ALIGN = SUBLANES
RUN_BITS = tuple(range(8, 2, -1))
TAIL_BITS = tuple(range(7, 2, -1))
COMMON_BITS = 6
WAIT_BITS = tuple(range(10, 2, -1))
STAGE_ROWS = 1280
XS_COLS = D_MODEL + LANES
N_BLOCKS = -(-(N_ASSIGN + N_TILES * N_EXPERTS * (RUN_ALIGN - 1) + N_EXPERTS * (MOE_BLK - 1)) // MOE_BLK)
N_SLOTS = N_BLOCKS * MOE_BLK
VMEM_LIMIT = 56 * 1024 * 1024

C_A3 = 0
C_CQ = C_A3 + 3 * CONV_DIM
C_CKV = C_CQ + Q_RANK
C_Z = C_CKV + KV_RANK
C_XBC = C_Z + SSM_INNER
C_GATE = C_XBC + SSM_CONV_CH
C_SMALL = C_GATE + 3 * D_MODEL
IN_COLS2 = C_SMALL + LANES
SM_DTF = QK_ROPE
SM_DTB = QK_ROPE + SSM_HEADS
S_CQ = 3 * CONV_DIM
S_CKV = S_CQ + Q_RANK
S_KPE = S_CKV + KV_RANK
S_Z = S_KPE + QK_ROPE
S_XBC = S_Z + SSM_INNER
S_DTF = S_XBC + SSM_CONV_CH
S_GATE = S_DTF + 2 * SSM_HEADS
IN_COLS = S_GATE + 3 * D_MODEL
W_SEGMENTS = ((C_A3, 0, 3 * CONV_DIM), (C_CQ, S_CQ, Q_RANK), (C_CKV, S_CKV, KV_RANK), (C_Z, S_Z, SSM_INNER),
              (C_XBC, S_XBC, SSM_CONV_CH), (C_GATE, S_GATE, 3 * D_MODEL))
W_PIECE = 512


def _rms(x, w):
    return x * lax.rsqrt(jnp.mean(x * x, axis=-1, keepdims=True) + NORM_EPS) * w


def _silu(x):
    return x * jax.nn.sigmoid(x)


def _dot(a, b):
    return jnp.dot(a, b, preferred_element_type=F32)


def _dot_nt(a, b):
    return lax.dot_general(a, b, (((1,), (1,)), ((), ())), preferred_element_type=F32)


def _resident(shape):
    nd = len(shape)
    return pl.BlockSpec(shape, lambda *_: (0,) * nd, pipeline_mode=pl.Buffered(1))


def _mod_row(i):
    return jnp.where(i < CTX_TILES, 0, 1 + (i - CTX_TILES) // LAT_TILES_PER_SEQ)


def _pos_block(i):
    return jnp.where(i < CTX_TILES, 0, 1 + (i - CTX_TILES) % LAT_TILES_PER_SEQ)


def _ada_kernel(c_ref, w_ref, b_ref, o_ref):
    s = _silu(c_ref[...]).astype(BF16)
    o_ref[0] = _dot(s, w_ref[0].astype(BF16)) + b_ref[0]


def _ada_mods(cond8, w_ada, b_ada):
    tn = 1536
    n_mod = 6 * D_MODEL
    return pl.pallas_call(
        _ada_kernel,
        grid=(DEPTH, n_mod // tn),
        in_specs=[
            pl.BlockSpec((SUBLANES, D_MODEL), lambda l, j: (0, 0)),
            pl.BlockSpec((1, D_MODEL, tn), lambda l, j: (l, 0, j)),
            pl.BlockSpec((1, 1, tn), lambda l, j: (l, 0, j)),
        ],
        out_specs=pl.BlockSpec((1, SUBLANES, tn), lambda l, j: (l, 0, j)),
        out_shape=jax.ShapeDtypeStruct((DEPTH, SUBLANES, n_mod), F32),
        compiler_params=pltpu.CompilerParams(dimension_semantics=("arbitrary", "arbitrary")),
        name="ada_mods",
    )(cond8, w_ada, b_ada.reshape(DEPTH, 1, n_mod))


def _relayout_w_in(layer, wt_hbm, w2, stg, small_stg, sem, small_sem):
    pieces = [(dst + p, src + p, min(W_PIECE, width - p))
              for dst, src, width in W_SEGMENTS for p in range(0, width, W_PIECE)]
    fetch = lambda k: pltpu.make_async_copy(wt_hbm.at[layer, pl.ds(pieces[k][1], pieces[k][2]), :],
                                            stg.at[k % 2, pl.ds(0, pieces[k][2]), :], sem.at[k % 2])
    small_stg[...] = jnp.zeros_like(small_stg)
    small_copies = [
        pltpu.make_async_copy(wt_hbm.at[layer, pl.ds(S_KPE, QK_ROPE), :], small_stg.at[pl.ds(0, QK_ROPE), :],
                              small_sem.at[0]),
        pltpu.make_async_copy(wt_hbm.at[layer, pl.ds(S_DTF, 2 * SSM_HEADS), :],
                              small_stg.at[pl.ds(SM_DTF, 2 * SSM_HEADS), :], small_sem.at[1]),
    ]
    for cp in small_copies:
        cp.start()
    fetch(0).start()
    for k, (dst, _, width) in enumerate(pieces):
        if k + 1 < len(pieces):
            fetch(k + 1).start()
        fetch(k).wait()
        w2[:, dst:dst + width] = stg[k % 2, 0:width, :].T.astype(BF16)
    for cp in small_copies:
        cp.wait()
    w2[:, C_SMALL:IN_COLS2] = small_stg[...].T.astype(BF16)


def _rot_partner(x):
    n = x.shape[1]
    lane = lax.broadcasted_iota(I32, x.shape, 1)
    quarter = QK_ROPE // 4
    return jnp.where(lane % (2 * quarter) < quarter, pltpu.roll(x, n - quarter, 1), pltpu.roll(x, quarter, 1))


def _inproj_kernel(layer, xc_ref, xcp_ref, xcn_ref, xl_ref, xlp_ref, xln_ref, mod_ref, n1w_ref, w_hbm,
                   qnw_ref, wq_ref, kvnw_ref, wuk_ref,
                   wuv_ref, vone_ref, tile_ref, cosq_ref, sinq_ref, cosk_ref, sink_ref, caw_ref, cw_ref, cb_ref,
                   ta_ref, q_ref, ckv_ref, kpe_ref, kf_ref, v_ref, small_ref, z_ref, xact_ref, g_ref,
                   w2, stg, small_stg, sem, small_sem):
    i = pl.program_id(0)

    @pl.when(i == 0)
    def _():
        _relayout_w_in(layer, w_hbm, w2, stg, small_stg, sem, small_sem)

    mod = mod_ref[0]
    shift1 = mod[:, 0:D_MODEL]
    scale1 = mod[:, D_MODEL:2 * D_MODEL]
    x_ext = jnp.where(i < CTX_TILES, jnp.concatenate([xcp_ref[...], xc_ref[...], xcn_ref[...]], axis=0),
                      jnp.concatenate([xlp_ref[...], xl_ref[...], xln_ref[...]], axis=0))
    h_ext = _rms(x_ext, n1w_ref[...]) * (1.0 + scale1) + shift1
    hb_ext = h_ext.astype(BF16)
    hb = h_ext[SUBLANES:SUBLANES + TM].astype(BF16)

    j = (i - CTX_TILES) % LAT_TILES_PER_SEQ
    is_ctx = i < CTX_TILES
    keep_prev = jnp.where(jnp.logical_or(is_ctx, j == 0), 0.0, 1.0)
    keep_next = jnp.where(jnp.logical_or(is_ctx, j == LAT_TILES_PER_SEQ - 1), 0.0, 1.0)
    row = lax.broadcasted_iota(I32, (TM, 1), 0)
    prev_mask = jnp.where(row == 0, keep_prev, 1.0)
    next_mask = jnp.where(row == TM - 1, keep_next, 1.0)
    ext = TM + 2 * SUBLANES

    def conv3(x, w_ref):
        x_prev = pltpu.roll(x, 1, 0)[SUBLANES:SUBLANES + TM] * prev_mask
        x_next = pltpu.roll(x, ext - 1, 0)[SUBLANES:SUBLANES + TM] * next_mask
        return x_prev * w_ref[0:1, :] + x[SUBLANES:SUBLANES + TM] * w_ref[1:2, :] + x_next * w_ref[2:3, :]

    a3 = _dot(hb_ext, w2[:, C_A3:C_CQ])
    s = a3[:, 2 * CONV_DIM:3 * CONV_DIM] * a3[:, 0:CONV_DIM]
    ta_ref[...] = (a3[SUBLANES:SUBLANES + TM, CONV_DIM:2 * CONV_DIM] * conv3(s, caw_ref)).astype(BF16)
    xact_ref[...] = _silu(conv3(_dot(hb_ext, w2[:, C_XBC:C_GATE]), cw_ref) + cb_ref[...])

    def seg(a, b):
        return _dot(hb, w2[:, a:b])

    cqn = _rms(seg(C_CQ, C_CKV), qnw_ref[...]).astype(BF16)
    qa = _dot(cqn, wq_ref[...])
    qa_rot = _rot_partner(qa)
    for h in range(MLA_HEADS):
        hs = slice(h * HEAD_PAD, (h + 1) * HEAD_PAD)
        q_ref[:, hs] = (qa[:, hs] * cosq_ref[...] + qa_rot[:, hs] * sinq_ref[...]).astype(BF16)

    ckv = _rms(seg(C_CKV, C_Z), kvnw_ref[...])
    ckvb = ckv.astype(BF16)
    small = seg(C_SMALL, IN_COLS2)
    small_ref[...] = small

    @pl.when(i < CTX_TILES)
    def _():
        ckv_ref[...] = ckv
        kpe_ref[...] = small[:, 0:QK_ROPE]

    kpe = small * cosk_ref[...] + _rot_partner(small) * sink_ref[...]
    kf_ref[...] = (_dot(ckvb, wuk_ref[...]) + _dot(kpe.astype(BF16), tile_ref[...])).astype(BF16)
    v_ref[...] = (_dot(ckvb, wuv_ref[...]) + vone_ref[...]).astype(BF16)

    z_ref[...] = seg(C_Z, C_XBC)
    g_ref[...] = jax.nn.sigmoid(seg(C_GATE, C_SMALL))


def _inproj(layer, xc, xl, mod3, w_in, lw, tabs):
    row = lambda n: pl.BlockSpec((TM, n), lambda i: (i, 0))
    tab = lambda n: pl.BlockSpec((TM, n), lambda i: (_pos_block(i), 0))
    per = TM // SUBLANES
    qw = MLA_HEADS * HEAD_PAD
    ctx_row = lambda n: pl.BlockSpec((TM, n), lambda i: (jnp.minimum(i, CTX_TILES - 1), 0))
    lat_row = lambda n: pl.BlockSpec((TM, n), lambda i: (jnp.maximum(i - CTX_TILES, 0), 0))

    def halo(first_tile, n_rows, side):
        last = n_rows // SUBLANES - 1
        return pl.BlockSpec((SUBLANES, D_MODEL),
                            lambda i: (jnp.clip((i - first_tile + side) * per - 1 + side, 0, last), 0))

    out_shape = (
        jax.ShapeDtypeStruct((N_TOK, CONV_DIM), BF16),
        jax.ShapeDtypeStruct((N_TOK, qw), BF16),
        jax.ShapeDtypeStruct((N_CTX, KV_RANK), F32),
        jax.ShapeDtypeStruct((N_CTX, QK_ROPE), F32),
        jax.ShapeDtypeStruct((N_TOK, qw), BF16),
        jax.ShapeDtypeStruct((N_TOK, qw), BF16),
        jax.ShapeDtypeStruct((N_TOK, LANES), F32),
        jax.ShapeDtypeStruct((N_TOK, SSM_INNER), F32),
        jax.ShapeDtypeStruct((N_TOK, SSM_CONV_CH), F32),
        jax.ShapeDtypeStruct((N_TOK, 3 * D_MODEL), F32),
    )
    out_specs = [row(s.shape[1]) for s in out_shape]
    out_specs[2] = ctx_row(KV_RANK)
    out_specs[3] = ctx_row(QK_ROPE)
    return pl.pallas_call(
        functools.partial(_inproj_kernel, layer),
        grid=(N_TILES,),
        in_specs=[
            ctx_row(D_MODEL), halo(0, N_CTX, 0), halo(0, N_CTX, 1),
            lat_row(D_MODEL), halo(CTX_TILES, N_LAT, 0), halo(CTX_TILES, N_LAT, 1),
            pl.BlockSpec((1, 1, 6 * D_MODEL), lambda i: (_mod_row(i), 0, 0)),
            _resident((1, D_MODEL)),
            pl.BlockSpec(memory_space=pl.ANY),
            _resident((1, Q_RANK)),
            _resident((Q_RANK, qw)),
            _resident((1, KV_RANK)),
            _resident((KV_RANK, qw)),
            _resident((KV_RANK, qw)),
            _resident((1, qw)),
            _resident((LANES, qw)),
            tab(HEAD_PAD), tab(HEAD_PAD), tab(LANES), tab(LANES),
            _resident((3, CONV_DIM)), _resident((3, SSM_CONV_CH)), _resident((1, SSM_CONV_CH)),
        ],
        out_specs=tuple(out_specs),
        out_shape=out_shape,
        scratch_shapes=[pltpu.VMEM((D_MODEL, IN_COLS2), BF16), pltpu.VMEM((2, W_PIECE, D_MODEL), F32),
                        pltpu.VMEM((LANES, D_MODEL), F32), pltpu.SemaphoreType.DMA((2,)),
                        pltpu.SemaphoreType.DMA((2,))],
        compiler_params=pltpu.CompilerParams(dimension_semantics=("arbitrary",), vmem_limit_bytes=VMEM_LIMIT),
        name="inproj",
    )(xc, xc, xc, xl, xl, xl, mod3, lw["norm1_w"], w_in, lw["q_norm_w"], lw["wq_a"], lw["kv_norm_w"],
      lw["wuk"], lw["wuv"], tabs["vone"], tabs["tile"], tabs["cosq"], tabs["sinq"], tabs["cosk"], tabs["sink"],
      lw["conv_a_w"], lw["ssm_conv_w"], lw["ssm_conv_b"])


def _kvcache_kernel(ckv_ref, kpe_ref, wuk_ref, wuv_ref, vone_ref, tile_ref, kf_ref, v_ref):
    ckvb = ckv_ref[...].astype(BF16)
    kf_ref[...] = (_dot(ckvb, wuk_ref[...]) + _dot(kpe_ref[...].astype(BF16), tile_ref[...])).astype(BF16)
    v_ref[...] = (_dot(ckvb, wuv_ref[...]) + vone_ref[...]).astype(BF16)


def _kvcache(ckv, kpe128, lw, tabs):
    n = ckv.shape[0]
    qw = MLA_HEADS * HEAD_PAD
    return pl.pallas_call(
        _kvcache_kernel,
        grid=(n // PAST_LEN,),
        in_specs=[
            pl.BlockSpec((PAST_LEN, KV_RANK), lambda i: (i, 0)),
            pl.BlockSpec((PAST_LEN, LANES), lambda i: (i, 0)),
            _resident((KV_RANK, qw)),
            _resident((KV_RANK, qw)),
            _resident((1, qw)),
            _resident((LANES, qw)),
        ],
        out_specs=(pl.BlockSpec((PAST_LEN, qw), lambda i: (i, 0)), pl.BlockSpec((PAST_LEN, qw), lambda i: (i, 0))),
        out_shape=(jax.ShapeDtypeStruct((n, qw), BF16), jax.ShapeDtypeStruct((n, qw), BF16)),
        compiler_params=pltpu.CompilerParams(dimension_semantics=("arbitrary",)),
        name="kvcache",
    )(ckv, kpe128, lw["wuk"], lw["wuv"], tabs["vone"], tabs["tile"])


def _attn_heads(q_ref, kv_refs, o_ref, acc_ref):
    log2_scale = MLA_SCALE * math.log2(math.e)
    for h in range(MLA_HEADS):
        hs = slice(h * HEAD_PAD, (h + 1) * HEAD_PAD)
        qh = q_ref[:, hs]
        ss = [_dot_nt(qh, k_ref[:, hs]) for k_ref, _ in kv_refs]
        m = functools.reduce(jnp.maximum, [jnp.max(s, axis=-1, keepdims=True) for s in ss])
        ol = functools.reduce(jnp.add, [_dot(jnp.exp2((s - m) * log2_scale).astype(BF16), v_ref[:, hs])
                                        for s, (_, v_ref) in zip(ss, kv_refs)])
        acc_ref[:, h * V_HEAD:(h + 1) * V_HEAD] = ol[:, 0:V_HEAD] / ol[:, V_HEAD:V_HEAD + 1]
    o_ref[...] = acc_ref[...].astype(BF16)


def _attn_ctx_kernel(q_ref, k_ref, v_ref, o_ref, acc_ref):
    _attn_heads(q_ref, [(k_ref, v_ref)], o_ref, acc_ref)


def _attn_lat_kernel(q_ref, k_ref, v_ref, kc_ref, vc_ref, o_ref, acc_ref):
    _attn_heads(q_ref, [(k_ref, v_ref), (kc_ref, vc_ref)], o_ref, acc_ref)


def _attention(q, kf, v, kf_c, v_c):
    qw = MLA_HEADS * HEAD_PAD
    vw = MLA_HEADS * V_HEAD
    att_ctx = pl.pallas_call(
        _attn_ctx_kernel,
        grid=(BATCH,),
        in_specs=[pl.BlockSpec((SEQ, qw), lambda b: (b, 0)), pl.BlockSpec((SEQ, qw), lambda b: (b, 0)),
                  pl.BlockSpec((SEQ, qw), lambda b: (b, 0))],
        out_specs=pl.BlockSpec((SEQ, vw), lambda b: (b, 0)),
        out_shape=jax.ShapeDtypeStruct((N_CTX, vw), BF16),
        scratch_shapes=[pltpu.VMEM((SEQ, vw), F32)],
        compiler_params=pltpu.CompilerParams(dimension_semantics=("arbitrary",)),
        name="attn_ctx",
    )(q, kf, v)
    lat0 = N_CTX // DEC_SEQ
    att_lat = pl.pallas_call(
        _attn_lat_kernel,
        grid=(DEC_BATCH, LAT_TILES_PER_SEQ),
        in_specs=[
            pl.BlockSpec((TM, qw), lambda b, t: (CTX_TILES + b * LAT_TILES_PER_SEQ + t, 0)),
            pl.BlockSpec((DEC_SEQ, qw), lambda b, t: (lat0 + b, 0)),
            pl.BlockSpec((DEC_SEQ, qw), lambda b, t: (lat0 + b, 0)),
            pl.BlockSpec((PAST_LEN, qw), lambda b, t: (b, 0)),
            pl.BlockSpec((PAST_LEN, qw), lambda b, t: (b, 0)),
        ],
        out_specs=pl.BlockSpec((TM, vw), lambda b, t: (b * LAT_TILES_PER_SEQ + t, 0)),
        out_shape=jax.ShapeDtypeStruct((N_LAT, vw), BF16),
        scratch_shapes=[pltpu.VMEM((TM, vw), F32)],
        compiler_params=pltpu.CompilerParams(dimension_semantics=("arbitrary", "arbitrary"),
                                             vmem_limit_bytes=VMEM_LIMIT),
        name="attn_lat",
    )(q, kf, v, kf_c, v_c)
    return att_ctx, att_lat


def _seq_of_chunk(s):
    return jnp.where(s < N_CTX_CHUNKS, s // CTX_CHUNKS_PER_SEQ,
                     BATCH + (s - N_CTX_CHUNKS) // LAT_CHUNKS_PER_SEQ)


def _chunk_in_seq(s):
    return jnp.where(s < N_CTX_CHUNKS, s % CTX_CHUNKS_PER_SEQ, (s - N_CTX_CHUNKS) % LAT_CHUNKS_PER_SEQ)


def _chunks_in_seq(s):
    return jnp.where(s < N_CTX_CHUNKS, CTX_CHUNKS_PER_SEQ, LAT_CHUNKS_PER_SEQ)


def _mirror_chunk(s):
    return s + _chunks_in_seq(s) - 1 - 2 * _chunk_in_seq(s)


def _split3(a):
    a1 = a.astype(BF16)
    r1 = a - a1.astype(F32)
    a2 = r1.astype(BF16)
    a3 = (r1 - a2.astype(F32)).astype(BF16)
    return a1, a2, a3


def _ssd_direction(x_ref, sm_ref, par_ref, st_ref, y_ref, lane0, backward):
    ri = lax.broadcasted_iota(I32, (CHUNK, CHUNK), 0)
    ci = lax.broadcasted_iota(I32, (CHUNK, CHUNK), 1)
    tri = (ci >= ri) if backward else (ci <= ri)
    tri_b = jnp.where(tri, 1.0, 0.0).astype(BF16)
    tot_row = 0 if backward else CHUNK - 1

    dt = jax.nn.softplus(sm_ref[...] + par_ref[0:1, :])
    a = dt * (-jnp.exp(par_ref[1:2, :])) * par_ref[2:3, :]
    a1, a2, a3 = _split3(a)
    acs = _dot(tri_b, a1) + _dot(tri_b, a2) + _dot(tri_b, a3)
    acs_t = acs.T
    dt_t = dt.T
    first_head = lax.broadcasted_iota(I32, (1, LANES), 1) < SSM_HEAD_DIM

    def block_diag(pair):
        return jnp.concatenate([jnp.where(first_head, pair, 0.0), jnp.where(first_head, 0.0, pair)],
                               axis=0).astype(BF16)

    for g in range(SSM_GROUPS):
        b0 = SSM_INNER + g * SSM_STATE
        c0 = SSM_INNER + SSM_GROUPS * SSM_STATE + g * SSM_STATE
        bg = x_ref[:, b0:b0 + SSM_STATE]
        cg = x_ref[:, c0:c0 + SSM_STATE]
        cb = _dot_nt(cg.astype(BF16), bg.astype(BF16))
        bg_t = bg.T
        heads = SSM_HEADS // SSM_GROUPS
        for pr in range(heads // 2):
            h0 = g * heads + 2 * pr
            sl = slice(h0 * SSM_HEAD_DIM, (h0 + 2) * SSM_HEAD_DIM)
            x_bd = block_diag(x_ref[:, sl])
            st_old = st_ref[:, sl]
            within, carried, to_state, keep = [], [], [], []
            for h in (h0, h0 + 1):
                lane = lane0 + h
                col = jnp.broadcast_to(acs[:, lane:lane + 1], (CHUNK, CHUNK))
                row = acs_t[lane:lane + 1, :]
                dt_row = dt_t[lane:lane + 1, :]
                decay = jnp.exp(jnp.where(tri, col - row, -jnp.inf))
                within.append((cb * decay * dt_row).astype(BF16))
                carried.append((cg * jnp.exp(col)).astype(BF16))
                tot = acs[tot_row:tot_row + 1, lane:lane + 1]
                to_state.append((bg_t * (dt_row * jnp.exp(tot - row))).astype(BF16))
                keep.append(jnp.exp(tot))
            lhs = jnp.concatenate(within + carried, axis=1)
            y_ref[:, sl] = _dot(lhs, jnp.concatenate([x_bd, block_diag(st_old)], axis=0))
            st_ref[:, sl] = (st_old * jnp.where(first_head, keep[0], keep[1])
                             + _dot(jnp.concatenate(to_state, axis=1), x_bd))


def _ssd_kernel(xf_ref, xb_ref, smf_ref, smb_ref, if_ref, ib_ref, par_ref,
                yf_ref, yb_ref, sf_ref, sb_ref, stf_ref, stb_ref):
    s = pl.program_id(0)
    c = _chunk_in_seq(s)

    @pl.when(jnp.logical_and(c == 0, s < N_CTX_CHUNKS))
    def _():
        stf_ref[...] = jnp.zeros_like(stf_ref)
        stb_ref[...] = jnp.zeros_like(stb_ref)

    @pl.when(jnp.logical_and(c == 0, s >= N_CTX_CHUNKS))
    def _():
        stf_ref[...] = if_ref[0].T
        stb_ref[...] = ib_ref[0].T

    _ssd_direction(xf_ref, smf_ref, par_ref, stf_ref, yf_ref, SM_DTF, False)
    _ssd_direction(xb_ref, smb_ref, par_ref, stb_ref, yb_ref, SM_DTB, True)

    @pl.when(jnp.logical_and(c == _chunks_in_seq(s) - 1, s < N_CTX_CHUNKS))
    def _():
        sf_ref[0] = stf_ref[...].T
        sb_ref[0] = stb_ref[...].T


def _ssd(xact, small, init_f, init_b, par):
    hp = SSM_INNER
    fwd = lambda n: pl.BlockSpec((CHUNK, n), lambda s: (s, 0))
    bwd = lambda n: pl.BlockSpec((CHUNK, n), lambda s: (_mirror_chunk(s), 0))
    st = pl.BlockSpec((1, hp, SSM_STATE), lambda s: (jnp.minimum(_seq_of_chunk(s), BATCH - 1), 0, 0))
    init = pl.BlockSpec((1, hp, SSM_STATE), lambda s: (jnp.maximum(_seq_of_chunk(s) - BATCH, 0), 0, 0))
    return pl.pallas_call(
        _ssd_kernel,
        grid=(N_CHUNKS,),
        in_specs=[fwd(SSM_CONV_CH), bwd(SSM_CONV_CH), fwd(LANES), bwd(LANES), init, init,
                  pl.BlockSpec((SUBLANES, LANES), lambda s: (0, 0))],
        out_specs=(fwd(hp), bwd(hp), st, st),
        out_shape=(jax.ShapeDtypeStruct((N_TOK, hp), F32), jax.ShapeDtypeStruct((N_TOK, hp), F32),
                   jax.ShapeDtypeStruct((BATCH, hp, SSM_STATE), F32),
                   jax.ShapeDtypeStruct((BATCH, hp, SSM_STATE), F32)),
        scratch_shapes=[pltpu.VMEM((SSM_STATE, hp), F32), pltpu.VMEM((SSM_STATE, hp), F32)],
        compiler_params=pltpu.CompilerParams(dimension_semantics=("arbitrary",)),
        name="ssd",
    )(xact, xact, small, small, init_f, init_b, par)


def _merge_kernel(ta_ref, attc_ref, attl_ref, yf_ref, yb_ref, xs_ref, z_ref, g_ref, xc_ref, xl_ref, mod_ref,
                  woa_ref, wom_ref, dsk_ref, snw_ref, wos_ref, wo_ref, n2w_ref, rw_ref, rb_ref,
                  x1_ref, h2_ref, ti_ref, tg_ref, cnt_ref):
    mod = mod_ref[0]
    gate1 = mod[:, 2 * D_MODEL:3 * D_MODEL]
    shift2 = mod[:, 3 * D_MODEL:4 * D_MODEL]
    scale2 = mod[:, 4 * D_MODEL:5 * D_MODEL]
    y_a = _dot(ta_ref[...], woa_ref[...])
    att = jnp.where(pl.program_id(0) < CTX_TILES, attc_ref[...].astype(F32), attl_ref[...].astype(F32))
    y_b = _dot(att.astype(BF16), wom_ref[...])
    y_ssm = (yf_ref[...] + yb_ref[...] + dsk_ref[...] * xs_ref[...]) * _silu(z_ref[...])
    y_c = _dot(_rms(y_ssm, snw_ref[...]).astype(BF16), wos_ref[...])
    merged = (g_ref[:, 0:D_MODEL] * y_a + g_ref[:, D_MODEL:2 * D_MODEL] * y_b
              + g_ref[:, 2 * D_MODEL:3 * D_MODEL] * y_c)
    x = jnp.where(pl.program_id(0) < CTX_TILES, xc_ref[...], xl_ref[...])
    x1 = x + gate1 * _dot(merged.astype(BF16), wo_ref[...])
    x1_ref[...] = x1
    h2 = _rms(x1, n2w_ref[...]) * (1.0 + scale2) + shift2
    h2b = h2.astype(BF16)
    h2_ref[...] = h2b

    logits = (_dot(h2b, rw_ref[...]) + rb_ref[...]).T[0:N_EXPERTS, :]
    expert = lax.broadcasted_iota(I32, (N_EXPERTS, TM), 0)
    ids, vals = [], []
    for k in range(TOP_K):
        m = jnp.max(logits, axis=0, keepdims=True)
        idx = jnp.min(jnp.where(logits == m, expert, N_EXPERTS), axis=0, keepdims=True)
        ids.append(idx)
        vals.append(m)
        logits = jnp.where(expert == idx, -jnp.inf, logits)
    es = [jnp.exp(v - vals[0]) for v in vals]
    denom = functools.reduce(jnp.add, es)
    srow = lax.broadcasted_iota(I32, (LANES, TM), 0)
    ti_t = jnp.zeros((LANES, TM), F32)
    tg_t = jnp.zeros((LANES, TM), F32)
    chosen_t = jnp.zeros((LANES, TM), F32)
    for k in range(TOP_K):
        ti_t = jnp.where(srow == k, ids[k].astype(F32), ti_t)
        tg_t = jnp.where(srow == k, es[k] / denom, tg_t)
        chosen_t = jnp.where(srow == ids[k], 1.0, chosen_t)
    ti_ref[...] = ti_t.T.astype(I32)
    tg_ref[...] = tg_t.T
    cnt_ref[0] = _dot_nt(jnp.ones((SUBLANES, TM), BF16), chosen_t.astype(BF16)).astype(I32)


def _merge(ta, att_ctx, att_lat, yf, yb, xact, z, g, xc, xl, mod3, lw):
    row = lambda n: pl.BlockSpec((TM, n), lambda i: (i, 0))
    vw = MLA_HEADS * V_HEAD
    out_shape = (jax.ShapeDtypeStruct((N_TOK, D_MODEL), F32), jax.ShapeDtypeStruct((N_TOK, D_MODEL), BF16),
                 jax.ShapeDtypeStruct((N_TOK, LANES), I32), jax.ShapeDtypeStruct((N_TOK, LANES), F32))
    cnt_shape = jax.ShapeDtypeStruct((N_TILES, SUBLANES, LANES), I32)
    cnt_spec = pl.BlockSpec((1, SUBLANES, LANES), lambda i: (i, 0, 0))
    return pl.pallas_call(
        _merge_kernel,
        grid=(N_TILES,),
        in_specs=[
            row(CONV_DIM),
            pl.BlockSpec((TM, vw), lambda i: (jnp.minimum(i, CTX_TILES - 1), 0)),
            pl.BlockSpec((TM, vw), lambda i: (jnp.maximum(i - CTX_TILES, 0), 0)),
            row(SSM_INNER), row(SSM_INNER), row(SSM_INNER),
            row(SSM_INNER), row(3 * D_MODEL),
            pl.BlockSpec((TM, D_MODEL), lambda i: (jnp.minimum(i, CTX_TILES - 1), 0)),
            pl.BlockSpec((TM, D_MODEL), lambda i: (jnp.maximum(i - CTX_TILES, 0), 0)),
            pl.BlockSpec((1, 1, 6 * D_MODEL), lambda i: (_mod_row(i), 0, 0)),
            _resident((CONV_DIM, D_MODEL)), _resident((MLA_HEADS * V_HEAD, D_MODEL)),
            _resident((1, SSM_INNER)), _resident((1, SSM_INNER)), _resident((SSM_INNER, D_MODEL)),
            _resident((D_MODEL, D_MODEL)), _resident((1, D_MODEL)),
            _resident((D_MODEL, LANES)), _resident((1, LANES)),
        ],
        out_specs=tuple(row(s.shape[1]) for s in out_shape) + (cnt_spec,),
        out_shape=out_shape + (cnt_shape,),
        compiler_params=pltpu.CompilerParams(dimension_semantics=("arbitrary",), vmem_limit_bytes=VMEM_LIMIT),
        name="merge",
    )(ta, att_ctx, att_lat, yf, yb, xact, z, g, xc, xl, mod3, lw["w_out_a"], lw["w_o_mla"], lw["d_skip"],
      lw["ssm_norm_w"],
      lw["w_o_ssm"], lw["w_o"], lw["norm2_w"], lw["router_w"], lw["router_b"])


def _run_copies(cnt_ref, src_ref, dst_ref, first, count, bits, make_copy, start):
    def body(e, carry):
        n = cnt_ref[first + e]
        s0 = src_ref[first + e] if src_ref is not None else 0
        d0 = dst_ref[first + e]

        def pieces(some_bits):
            for b in some_bits:
                above = (n >> (b + 1)) << (b + 1)

                @pl.when(((n >> b) & 1) == 1)
                def _():
                    cp = make_copy(pl.multiple_of(s0 + above, RUN_ALIGN), pl.multiple_of(d0 + above, RUN_ALIGN),
                                   1 << b)
                    if start:
                        cp.start()
                    else:
                        cp.wait()

        large = [b for b in bits if b >= COMMON_BITS]

        @pl.when(n >= (1 << COMMON_BITS))
        def _():
            pieces(large)
        pieces([b for b in bits if b < COMMON_BITS])
        return carry
    lax.fori_loop(0, count, body, 0)


def _wait_rows(total, make_copy):
    for b in WAIT_BITS:
        @pl.when(((total >> b) & 1) == 1)
        def _():
            make_copy(0, 0, 1 << b).wait()


def _dispatch_kernel(cnt_ref, off_ref, run_ref, tcnt_ref, tdst_ref, nu_ref, h2_ref, ti_ref, tg_ref, offv_ref,
                     xs_ref, lp_ref, stage, zeros, perm_s, ghi_s, glo_s, sem, semz):
    i = pl.program_id(0)
    slot = i % 2

    def copy_out(s_):
        return lambda s, d, n: pltpu.make_async_copy(stage.at[s_, pl.ds(s, n), :], xs_ref.at[pl.ds(d, n), :],
                                                     sem.at[s_])

    def tile_rows(t):
        last = t * N_EXPERTS + N_EXPERTS - 1
        return off_ref[last] + cnt_ref[last]

    @pl.when(i == 0)
    def _():
        zeros[...] = jnp.zeros_like(zeros)
        zero_out = lambda s, d, n: pltpu.make_async_copy(zeros.at[pl.ds(0, n), :], xs_ref.at[pl.ds(d, n), :], semz)
        _run_copies(tcnt_ref, None, tdst_ref, 0, N_EXPERTS, TAIL_BITS, zero_out, True)
        _run_copies(tcnt_ref, None, tdst_ref, 0, N_EXPERTS, TAIL_BITS, zero_out, False)
        zrows = zeros.shape[0]

        def unused_blocks(start):
            def body(b, carry):
                for part in range(MOE_BLK // zrows):
                    cp = zero_out(0, pl.multiple_of(b * MOE_BLK + part * zrows, RUN_ALIGN), zrows)
                    if start:
                        cp.start()
                    else:
                        cp.wait()
                return carry
            lax.fori_loop(nu_ref[0], N_BLOCKS, body, 0)
        unused_blocks(True)
        unused_blocks(False)

    @pl.when(i >= 2)
    def _():
        _wait_rows(tile_rows(i - 2), copy_out(slot))

    lane = lax.broadcasted_iota(I32, (TM, LANES), 1)
    picks = [jnp.where(lane == ti_ref[:, k:k + 1], 1.0, 0.0) for k in range(TOP_K)]
    ri = lax.broadcasted_iota(I32, (TM, TM), 0)
    ci = lax.broadcasted_iota(I32, (TM, TM), 1)
    earlier = jnp.where(ci < ri, 1.0, 0.0).astype(BF16)
    base = _dot(earlier, functools.reduce(jnp.add, picks).astype(BF16)) + offv_ref[0][0:1, :]
    lp = jnp.full((TM, LANES), -1.0, F32)
    for k in range(TOP_K):
        lp = jnp.where(lane == k, jnp.sum(picks[k] * base, axis=-1, keepdims=True), lp)
    lp_ref[...] = lp.astype(I32)

    lp_t = lp.T
    tg_t = tg_ref[...].T
    for c in range(STAGE_ROWS // LANES):
        rows = slice(c * LANES, (c + 1) * LANES)
        row = (lax.broadcasted_iota(I32, (LANES, TM), 0) + c * LANES).astype(F32)
        perm = jnp.zeros((LANES, TM), F32)
        gates = jnp.zeros((LANES, TM), F32)
        for k in range(TOP_K):
            hit = row == lp_t[k:k + 1, :]
            perm = jnp.where(hit, 1.0, perm)
            gates = jnp.where(hit, tg_t[k:k + 1, :], gates)
        perm_s[rows, :] = perm.astype(BF16)
        g_hi = gates.astype(BF16)
        ghi_s[rows, :] = g_hi
        glo_s[rows, :] = (gates - g_hi.astype(F32)).astype(BF16)
    stage[slot, :, 0:D_MODEL] = _dot(perm_s[...], h2_ref[...])
    ones = jnp.ones((TM, LANES), BF16)
    stage[slot, :, D_MODEL:XS_COLS] = _dot(ghi_s[...], ones) + _dot(glo_s[...], ones)

    _run_copies(cnt_ref, off_ref, run_ref, i * N_EXPERTS, N_EXPERTS, RUN_BITS, copy_out(slot), True)

    @pl.when(i == N_TILES - 1)
    def _():
        _wait_rows(tile_rows(i - 1), copy_out(1 - slot))
        _wait_rows(tile_rows(i), copy_out(slot))


def _dispatch(rt, h2, top_i, top_g):
    row = lambda n: pl.BlockSpec((TM, n), lambda i, *_: (i, 0))
    return pl.pallas_call(
        _dispatch_kernel,
        grid_spec=pltpu.PrefetchScalarGridSpec(
            num_scalar_prefetch=6,
            grid=(N_TILES,),
            in_specs=[row(D_MODEL), row(LANES), row(LANES),
                      pl.BlockSpec((1, SUBLANES, LANES), lambda i, *_: (i, 0, 0))],
            out_specs=(pl.BlockSpec(memory_space=pl.ANY), row(LANES)),
            scratch_shapes=[
                pltpu.VMEM((2, STAGE_ROWS, XS_COLS), F32),
                pltpu.VMEM((1 << TAIL_BITS[0], XS_COLS), F32),
                pltpu.VMEM((STAGE_ROWS, TM), BF16), pltpu.VMEM((STAGE_ROWS, TM), BF16),
                pltpu.VMEM((STAGE_ROWS, TM), BF16),
                pltpu.SemaphoreType.DMA((2,)),
                pltpu.SemaphoreType.DMA,
            ],
        ),
        out_shape=(jax.ShapeDtypeStruct((N_SLOTS, XS_COLS), F32), jax.ShapeDtypeStruct((N_TOK, LANES), I32)),
        compiler_params=pltpu.CompilerParams(dimension_semantics=("arbitrary",), vmem_limit_bytes=VMEM_LIMIT),
        name="dispatch",
    )(rt["cnt"], rt["off"], rt["run"], rt["tail_cnt"], rt["tail_dst"], rt["n_used"], h2, top_i, top_g, rt["off_v"])


def _moe_kernel(layer, be_ref, nxt_ref, nv_ref, nu_ref, x_ref, wgu_hbm, wdn_hbm, bg_ref, bu_ref, bd_ref, sel_ref,
                y_ref, wgu_ref, wdn_ref, wgu_s, wdn_s, sem):
    i = pl.program_id(0)
    n_used = nu_ref[0]

    def fetch(e):
        return (pltpu.make_async_copy(wgu_hbm.at[layer, e], wgu_ref, sem.at[0]),
                pltpu.make_async_copy(wdn_hbm.at[layer, e], wdn_ref, sem.at[1]))

    @pl.when(i < n_used)
    def _():
        @pl.when(jnp.logical_or(i == 0, be_ref[i] != be_ref[jnp.maximum(i - 1, 0)]))
        def _():
            @pl.when(i == 0)
            def _():
                for cp in fetch(be_ref[0]):
                    cp.start()
            for cp in fetch(be_ref[i]):
                cp.wait()
            half = LANES
            for c in range(2 * EXPERT_FF // (2 * half)):
                r = _dot(wgu_ref[:, c * 2 * half:(c + 1) * 2 * half].astype(BF16), sel_ref[...])
                wgu_s[:, c * half:(c + 1) * half] = r[:, 0:half].astype(BF16)
                wgu_s[:, EXPERT_FF + c * half:EXPERT_FF + (c + 1) * half] = r[:, half:2 * half].astype(BF16)
            wdn_s[...] = wdn_ref[...].astype(BF16)
            nxt = nxt_ref[i]

            @pl.when(nxt >= 0)
            def _():
                for cp in fetch(nxt):
                    cp.start(priority=1)

        def expert_rows(rows):
            gu = _dot(x_ref[0:rows, 0:D_MODEL].astype(BF16), wgu_s[...])
            gate = jnp.minimum(gu[:, 0:EXPERT_FF] + bg_ref[0], SWIGLU_LIMIT)
            up = jnp.clip(gu[:, EXPERT_FF:2 * EXPERT_FF] + bu_ref[0], -SWIGLU_LIMIT, SWIGLU_LIMIT)
            act = gate * jax.nn.sigmoid(SWIGLU_ALPHA * gate) * (up + 1.0)
            y = _dot(act.astype(BF16), wdn_s[...]) + bd_ref[0]
            slot_gate = x_ref[0:rows, D_MODEL:XS_COLS]
            for j in range(D_MODEL // LANES):
                y_ref[0:rows, j * LANES:(j + 1) * LANES] = y[:, j * LANES:(j + 1) * LANES] * slot_gate

        half = MOE_BLK // 2

        @pl.when(nv_ref[i] > half)
        def _():
            expert_rows(MOE_BLK)

        @pl.when(nv_ref[i] <= half)
        def _():
            expert_rows(half)
            y_ref[half:MOE_BLK, :] = jnp.zeros((MOE_BLK - half, D_MODEL), F32)

    @pl.when(i >= n_used)
    def _():
        y_ref[...] = jnp.zeros_like(y_ref)


def _bias_split_kernel(b_ref, sel_ref, o_ref):
    for c in range(2 * EXPERT_FF // (2 * LANES)):
        terms = _split3(b_ref[:, c * 2 * LANES:(c + 1) * 2 * LANES])
        r = functools.reduce(jnp.add, [_dot(t, sel_ref[...]) for t in terms])
        o_ref[:, c * LANES:(c + 1) * LANES] = r[:, 0:LANES]
        o_ref[:, EXPERT_FF + c * LANES:EXPERT_FF + (c + 1) * LANES] = r[:, LANES:2 * LANES]


def _bias_split(b_gu, sel):
    n = DEPTH * N_EXPERTS
    out = pl.pallas_call(
        _bias_split_kernel,
        out_shape=jax.ShapeDtypeStruct((n, 2 * EXPERT_FF), F32),
        name="bias_split",
    )(b_gu.reshape(n, 2 * EXPERT_FF), sel)
    return out.reshape(n, 1, 2 * EXPERT_FF)


def _moe(layer, rt, xs, w_gu, w_down, b_gu_split, b_down, sel):
    first = layer * N_EXPERTS
    bg = pl.BlockSpec((1, 1, EXPERT_FF), lambda i, be, *_: (first + be[i], 0, 0))
    bu = pl.BlockSpec((1, 1, EXPERT_FF), lambda i, be, *_: (first + be[i], 0, 1))
    bd = pl.BlockSpec((1, 1, D_MODEL), lambda i, be, *_: (first + be[i], 0, 0))
    return pl.pallas_call(
        functools.partial(_moe_kernel, layer),
        grid_spec=pltpu.PrefetchScalarGridSpec(
            num_scalar_prefetch=4,
            grid=(N_BLOCKS,),
            in_specs=[
                pl.BlockSpec((MOE_BLK, XS_COLS), lambda i, be, nx, nv, nu: (jnp.minimum(i, nu[0] - 1), 0)),
                pl.BlockSpec(memory_space=pl.ANY),
                pl.BlockSpec(memory_space=pl.ANY),
                bg, bu, bd,
                pl.BlockSpec((2 * LANES, 2 * LANES), lambda i, *_: (0, 0)),
            ],
            out_specs=pl.BlockSpec((MOE_BLK, D_MODEL), lambda i, *_: (i, 0)),
            scratch_shapes=[
                pltpu.VMEM((D_MODEL, 2 * EXPERT_FF), F32),
                pltpu.VMEM((EXPERT_FF, D_MODEL), F32),
                pltpu.VMEM((D_MODEL, 2 * EXPERT_FF), BF16),
                pltpu.VMEM((EXPERT_FF, D_MODEL), BF16),
                pltpu.SemaphoreType.DMA((2,)),
            ],
        ),
        out_shape=jax.ShapeDtypeStruct((N_SLOTS, D_MODEL), F32),
        compiler_params=pltpu.CompilerParams(dimension_semantics=("arbitrary",), vmem_limit_bytes=VMEM_LIMIT),
        name="moe",
    )(rt["blk_e"], rt["blk_next"], rt["blk_rows"], rt["n_used"], xs, w_gu, w_down, b_gu_split, b_gu_split,
      b_down.reshape(DEPTH * N_EXPERTS, 1, D_MODEL), sel)


def _combine_kernel(cnt_ref, off_ref, run_ref, y_ref, lp_ref, x1_ref, mod_ref, fw_ref,
                    xc_ref, xl_ref, yc_ref, yl_ref, stage, mine_s, sem):
    i = pl.program_id(0)
    slot = i % 2

    def copy_in(s_):
        return lambda s, d, n: pltpu.make_async_copy(y_ref.at[pl.ds(d, n), :], stage.at[s_, pl.ds(s, n), :],
                                                     sem.at[s_])

    @pl.when(i == 0)
    def _():
        stage[...] = jnp.zeros_like(stage)
        _run_copies(cnt_ref, off_ref, run_ref, 0, N_EXPERTS, RUN_BITS, copy_in(0), True)

    @pl.when(i + 1 < N_TILES)
    def _():
        _run_copies(cnt_ref, off_ref, run_ref, (i + 1) * N_EXPERTS, N_EXPERTS, RUN_BITS, copy_in(1 - slot), True)

    last = i * N_EXPERTS + N_EXPERTS - 1
    _wait_rows(off_ref[last] + cnt_ref[last], copy_in(slot))
    lane = lax.broadcasted_iota(I32, (TM, LANES), 1)
    mine = [jnp.broadcast_to(lp_ref[:, k:k + 1], (TM, LANES)) for k in range(TOP_K)]
    for c in range(STAGE_ROWS // LANES):
        hit = jnp.zeros((TM, LANES), F32)
        for k in range(TOP_K):
            hit = jnp.where(lane + c * LANES == mine[k], 1.0, hit)
        mine_s[:, c * LANES:(c + 1) * LANES] = hit.astype(BF16)
    moe = _dot(mine_s[...], stage[slot].astype(BF16))
    gate2 = mod_ref[0][:, 5 * D_MODEL:6 * D_MODEL]
    x2 = x1_ref[...] + gate2 * moe
    y_norm = _rms(x2, fw_ref[...])

    @pl.when(i < CTX_TILES)
    def _():
        xc_ref[...] = x2
        yc_ref[...] = y_norm

    @pl.when(i >= CTX_TILES)
    def _():
        xl_ref[...] = x2
        yl_ref[...] = y_norm


def _combine(rt, y_slots, lp, x1, mod3, final_w):
    row = lambda n: pl.BlockSpec((TM, n), lambda i, *_: (i, 0))
    ctx_row = pl.BlockSpec((TM, D_MODEL), lambda i, *_: (jnp.minimum(i, CTX_TILES - 1), 0))
    lat_row = pl.BlockSpec((TM, D_MODEL), lambda i, *_: (jnp.maximum(i - CTX_TILES, 0), 0))
    return pl.pallas_call(
        _combine_kernel,
        grid_spec=pltpu.PrefetchScalarGridSpec(
            num_scalar_prefetch=3,
            grid=(N_TILES,),
            in_specs=[
                pl.BlockSpec(memory_space=pl.ANY),
                row(LANES), row(D_MODEL),
                pl.BlockSpec((1, 1, 6 * D_MODEL), lambda i, *_: (_mod_row(i), 0, 0)),
                pl.BlockSpec((1, D_MODEL), lambda i, *_: (0, 0)),
            ],
            out_specs=(ctx_row, lat_row, ctx_row, lat_row),
            scratch_shapes=[pltpu.VMEM((2, STAGE_ROWS, D_MODEL), F32), pltpu.VMEM((TM, STAGE_ROWS), BF16),
                            pltpu.SemaphoreType.DMA((2,))],
        ),
        out_shape=(jax.ShapeDtypeStruct((N_CTX, D_MODEL), F32), jax.ShapeDtypeStruct((N_LAT, D_MODEL), F32),
                   jax.ShapeDtypeStruct((N_CTX, D_MODEL), F32), jax.ShapeDtypeStruct((N_LAT, D_MODEL), F32)),
        compiler_params=pltpu.CompilerParams(dimension_semantics=("arbitrary",), vmem_limit_bytes=VMEM_LIMIT),
        name="combine",
    )(rt["cnt"], rt["off"], rt["run"], y_slots, lp, x1, mod3, final_w)


def _rope_tables():
    rows = DEC_SEQ // GRID_W
    t = jnp.arange(rows * GRID_W)
    row = (t // GRID_W).astype(F32)
    col = (t % GRID_W).astype(F32)
    half = QK_ROPE // 2
    inv = ROPE_BASE ** (-jnp.arange(0, half, 2, dtype=F32) / half)
    ang_r, ang_c = row[:, None] * inv, col[:, None] * inv
    cos32 = jnp.concatenate([jnp.cos(ang_r), jnp.cos(ang_r), jnp.cos(ang_c), jnp.cos(ang_c)], axis=-1)
    sin32 = jnp.concatenate([-jnp.sin(ang_r), jnp.sin(ang_r), -jnp.sin(ang_c), jnp.sin(ang_c)], axis=-1)
    cos32 = jnp.concatenate([jnp.ones((TM, QK_ROPE), F32), cos32], axis=0)
    sin32 = jnp.concatenate([jnp.zeros((TM, QK_ROPE), F32), sin32], axis=0)
    n = cos32.shape[0]
    pad = HEAD_PAD - QK_NOPE - QK_ROPE
    cos_h = jnp.concatenate([jnp.ones((n, QK_NOPE), F32), cos32, jnp.zeros((n, pad), F32)], axis=-1)
    sin_h = jnp.concatenate([jnp.zeros((n, QK_NOPE), F32), sin32, jnp.zeros((n, pad), F32)], axis=-1)
    zeros = jnp.zeros((n, LANES - QK_ROPE), F32)
    j = jnp.arange(QK_ROPE)
    tile = jnp.zeros((LANES, MLA_HEADS, HEAD_PAD), F32).at[j, :, QK_NOPE + j].set(1.0)
    return {
        "cosq": cos_h, "sinq": sin_h,
        "cosk": jnp.concatenate([cos32, zeros], axis=-1), "sink": jnp.concatenate([sin32, zeros], axis=-1),
        "tile": tile.reshape(LANES, MLA_HEADS * HEAD_PAD).astype(BF16),
        "vone": jnp.zeros((MLA_HEADS, HEAD_PAD), F32).at[:, V_HEAD].set(1.0).reshape(1, MLA_HEADS * HEAD_PAD),
    }


def _layer_weights(p, l):
    hd = QK_NOPE + QK_ROPE
    pad = HEAD_PAD - hd
    wq = p["w_uq"][l].reshape(Q_RANK, MLA_HEADS, hd)
    wq_a = jnp.pad(wq, ((0, 0), (0, 0), (0, pad)))
    wuk = jnp.pad(p["w_uk"][l], ((0, 0), (0, 0), (0, HEAD_PAD - QK_NOPE)))
    rw = jnp.pad(p["router_w"][l], ((0, 0), (0, LANES - N_EXPERTS)))
    rb = jnp.concatenate([p["router_b"][l], jnp.full((LANES - N_EXPERTS,), -jnp.inf, F32)])
    return {
        "norm1_w": p["norm1_w"][l][None], "q_norm_w": p["q_norm_w"][l][None],
        "wq_a": wq_a.reshape(Q_RANK, -1).astype(BF16),
        "kv_norm_w": p["kv_norm_w"][l][None],
        "wuk": wuk.reshape(KV_RANK, -1).astype(BF16),
        "wuv": jnp.pad(p["w_uv"][l], ((0, 0), (0, 0), (0, HEAD_PAD - V_HEAD))).reshape(KV_RANK, -1).astype(BF16),
        "conv_a_w": p["conv_a_w"][l], "ssm_conv_w": p["ssm_conv_w"][l], "ssm_conv_b": p["ssm_conv_b"][l][None],
        "w_out_a": p["w_out_a"][l].astype(BF16), "w_o_mla": p["w_o_mla"][l].astype(BF16),
        "d_skip": jnp.repeat(p["d_skip"][l], SSM_HEAD_DIM)[None], "ssm_norm_w": p["ssm_norm_w"][l][None],
        "w_o_ssm": p["w_o_ssm"][l].astype(BF16), "w_o": p["w_o"][l].astype(BF16),
        "norm2_w": p["norm2_w"][l][None], "router_w": rw.astype(BF16), "router_b": rb[None],
    }


def _ssd_params(p, l):
    z = lambda n: jnp.zeros((n,), F32)
    lanes = lambda f, b: jnp.concatenate([z(SM_DTF), f, b, z(LANES - SM_DTB - SSM_HEADS)])
    ones = jnp.ones((SSM_HEADS,), F32)
    rows = [lanes(p["dt_bias_fwd"][l], p["dt_bias_bwd"][l]), lanes(p["a_log_fwd"][l], p["a_log_bwd"][l]),
            lanes(ones, ones)]
    return jnp.concatenate([jnp.stack(rows), jnp.zeros((SUBLANES - 3, LANES), F32)], axis=0)


def _routing(cnt_tiles):
    cnt = cnt_tiles[:, 0, 0:N_EXPERTS]
    cnt = (cnt + RUN_ALIGN - 1) // RUN_ALIGN * RUN_ALIGN
    per_expert = jnp.sum(cnt, axis=0)
    padded = (per_expert + MOE_BLK - 1) // MOE_BLK * MOE_BLK
    pad_end = jnp.cumsum(padded)
    pad_start = pad_end - padded
    run = pad_start[None, :] + jnp.cumsum(cnt, axis=0) - cnt
    off = jnp.cumsum(cnt, axis=1) - cnt
    starts = jnp.arange(N_BLOCKS, dtype=I32) * MOE_BLK
    blk_e = jnp.minimum(jnp.sum((pad_end[None, :] <= starts[:, None]).astype(I32), axis=1), N_EXPERTS - 1)
    off_v = jnp.zeros((N_TILES, SUBLANES, LANES), F32).at[:, 0, 0:N_EXPERTS].set(off.astype(F32))
    ids = jnp.arange(N_EXPERTS, dtype=I32)
    later = jnp.logical_and(ids[None, :] > ids[:, None], padded[None, :] > 0)
    nxt = jnp.min(jnp.where(later, ids[None, :], N_EXPERTS), axis=1)
    nxt = jnp.where(nxt == N_EXPERTS, -1, nxt)
    per_block = lambda v: jnp.sum(jnp.where(blk_e[:, None] == ids[None, :], v[None, :], 0), axis=1)
    blk_rows = jnp.clip(per_block(pad_start + per_expert) - starts, 0, MOE_BLK)
    return {
        "blk_next": per_block(nxt).astype(I32), "blk_rows": blk_rows.astype(I32),
        "cnt": cnt.reshape(-1).astype(I32), "off": off.reshape(-1).astype(I32), "run": run.reshape(-1).astype(I32),
        "tail_cnt": (padded - per_expert).astype(I32), "tail_dst": (pad_start + per_expert).astype(I32),
        "blk_e": blk_e.astype(I32), "n_used": (pad_end[-1] // MOE_BLK).astype(I32).reshape(1), "off_v": off_v,
    }


def _deinterleave_matrix():
    k = jnp.arange(2 * LANES)[:, None]
    n = jnp.arange(2 * LANES)[None, :]
    src = jnp.where(n < LANES, 2 * n, 2 * (n - LANES) + 1)
    return (k == src).astype(BF16)


def kernel(x_prompt, x_sample, cache_ckv, cache_kpe, state_ssm_fwd, state_ssm_bwd, c, c_ctx, w_ada, b_ada, norm1_w, w_in, conv_a_w, w_out_a, q_norm_w, w_uq, kv_norm_w, w_uk, w_uv, w_o_mla, ssm_conv_w, ssm_conv_b, dt_bias_fwd, dt_bias_bwd, a_log_fwd, a_log_bwd, d_skip, ssm_norm_w, w_o_ssm, w_o, norm2_w, router_w, router_b, w_gu, b_gu, w_down, b_down, final_norm_w):
    p = dict(norm1_w=norm1_w, w_in=w_in, conv_a_w=conv_a_w, w_out_a=w_out_a, q_norm_w=q_norm_w, w_uq=w_uq,
             kv_norm_w=kv_norm_w, w_uk=w_uk, w_uv=w_uv, w_o_mla=w_o_mla, ssm_conv_w=ssm_conv_w,
             ssm_conv_b=ssm_conv_b, dt_bias_fwd=dt_bias_fwd, dt_bias_bwd=dt_bias_bwd, a_log_fwd=a_log_fwd,
             a_log_bwd=a_log_bwd, d_skip=d_skip, ssm_norm_w=ssm_norm_w, w_o_ssm=w_o_ssm, w_o=w_o,
             norm2_w=norm2_w, router_w=router_w, router_b=router_b, b_gu=b_gu, b_down=b_down)
    xc = x_prompt.reshape(N_CTX, D_MODEL)
    xl = x_sample.reshape(N_LAT, D_MODEL)
    cond8 = jnp.concatenate([c_ctx[None], c, jnp.zeros((SUBLANES - 1 - DEC_BATCH, D_MODEL), F32)], axis=0)
    mods = _ada_mods(cond8, w_ada, b_ada)
    tabs = _rope_tables()
    sel = _deinterleave_matrix()
    b_gu_split = _bias_split(b_gu, sel)
    w_in_t = jnp.swapaxes(w_in, 1, 2)
    final_w = final_norm_w[None]
    hp = SSM_INNER

    ckv_out, kpe_out, sf_out, sb_out = [], [], [], []
    y_ctx = y_lat = None
    for l in range(DEPTH):
        lw = _layer_weights(p, l)
        mod3 = mods[l].reshape(SUBLANES, 1, 6 * D_MODEL)
        ta, q, ckv, kpe, kf, v, small, z, xact, g = _inproj(l, xc, xl, mod3, w_in_t, lw, tabs)
        kpe_c = jnp.pad(cache_kpe[:, l].reshape(DEC_BATCH * PAST_LEN, QK_ROPE), ((0, 0), (0, LANES - QK_ROPE)))
        kf_c, v_c = _kvcache(cache_ckv[:, l].reshape(DEC_BATCH * PAST_LEN, KV_RANK), kpe_c, lw, tabs)
        att_ctx, att_lat = _attention(q, kf, v, kf_c, v_c)
        init_f = state_ssm_fwd[:, l].reshape(DEC_BATCH, hp, SSM_STATE)
        init_b = state_ssm_bwd[:, l].reshape(DEC_BATCH, hp, SSM_STATE)
        yf, yb, sf, sb = _ssd(xact, small, init_f, init_b, _ssd_params(p, l))
        x1, h2, top_i, top_g, cnt_tiles = _merge(ta, att_ctx, att_lat, yf, yb, xact, z, g, xc, xl, mod3, lw)
        rt = _routing(cnt_tiles)
        xs, lp = _dispatch(rt, h2, top_i, top_g)
        y_slots = _moe(l, rt, xs, w_gu, w_down, b_gu_split, b_down, sel)
        xc, xl, y_ctx, y_lat = _combine(rt, y_slots, lp, x1, mod3, final_w)
        ckv_out.append(ckv.reshape(BATCH, SEQ, KV_RANK))
        kpe_out.append(kpe.reshape(BATCH, SEQ, QK_ROPE))
        sf_out.append(sf.reshape(BATCH, SSM_HEADS, SSM_HEAD_DIM, SSM_STATE))
        sb_out.append(sb.reshape(BATCH, SSM_HEADS, SSM_HEAD_DIM, SSM_STATE))

    y_prompt = y_ctx.reshape(BATCH, SEQ, D_MODEL)
    y_sample = y_lat.reshape(DEC_BATCH, DEC_SEQ, D_MODEL)
    return (y_prompt, y_sample, jnp.stack(ckv_out, axis=1), jnp.stack(kpe_out, axis=1),
            jnp.stack(sf_out, axis=1), jnp.stack(sb_out, axis=1))
```

```python
import functools
import math

import jax
import jax.numpy as jnp
from jax import lax
from jax.experimental import pallas as pl
from jax.experimental.pallas import tpu as pltpu

F32 = jnp.float32
BF16 = jnp.bfloat16
I32 = jnp.int32

D_MODEL = 1024
BATCH = 16
SEQ = 256
DEPTH = 2
DEC_BATCH = 2
DEC_SEQ = 2048
PAST_LEN = 512
GRID_W = 64
NORM_EPS = 1e-6
CONV_DIM = 512
MLA_HEADS = 8
Q_RANK = 384
KV_RANK = 256
QK_NOPE = 64
QK_ROPE = 32
V_HEAD = 64
ROPE_BASE = 10000.0
MLA_SCALE = (QK_NOPE + QK_ROPE) ** -0.5
SSM_HEADS = 16
SSM_HEAD_DIM = 64
SSM_INNER = SSM_HEADS * SSM_HEAD_DIM
SSM_GROUPS = 2
SSM_STATE = 128
SSM_CONV_CH = SSM_INNER + 2 * SSM_GROUPS * SSM_STATE
N_EXPERTS = 32
TOP_K = 4
EXPERT_FF = D_MODEL
SWIGLU_ALPHA = 1.702
SWIGLU_LIMIT = 7.0

N_CTX = BATCH * SEQ
N_LAT = DEC_BATCH * DEC_SEQ
N_TOK = N_CTX + N_LAT
N_SEQS = BATCH + DEC_BATCH

LANES = 128
SUBLANES = 8
HEAD_PAD = 128
TM = 256
N_TILES = N_TOK // TM
CTX_TILES = N_CTX // TM
LAT_TILES_PER_SEQ = DEC_SEQ // TM
CHUNK = 128
CTX_CHUNKS_PER_SEQ = SEQ // CHUNK
LAT_CHUNKS_PER_SEQ = DEC_SEQ // CHUNK
N_CTX_CHUNKS = N_CTX // CHUNK
N_CHUNKS = N_TOK // CHUNK
MOE_BLK = 256
N_ASSIGN = N_TOK * TOP_K
RUN_ALIGN = SUBLANES
RUN_BITS = tuple(range(8, 2, -1))
TAIL_BITS = tuple(range(7, 2, -1))
COMMON_BITS = 6
WAIT_BITS = tuple(range(10, 2, -1))
STAGE_ROWS = 1280
XS_COLS = D_MODEL + LANES
N_BLOCKS = -(-(N_ASSIGN + N_TILES * N_EXPERTS * (RUN_ALIGN - 1) + N_EXPERTS * (MOE_BLK - 1)) // MOE_BLK)
N_SLOTS = N_BLOCKS * MOE_BLK
VMEM_LIMIT = 56 * 1024 * 1024

C_A3 = 0
C_CQ = C_A3 + 3 * CONV_DIM
C_CKV = C_CQ + Q_RANK
C_Z = C_CKV + KV_RANK
C_XBC = C_Z + SSM_INNER
C_GATE = C_XBC + SSM_CONV_CH
C_SMALL = C_GATE + 3 * D_MODEL
IN_COLS2 = C_SMALL + LANES
SM_DTF = QK_ROPE
SM_DTB = QK_ROPE + SSM_HEADS
S_CQ = 3 * CONV_DIM
S_CKV = S_CQ + Q_RANK
S_KPE = S_CKV + KV_RANK
S_Z = S_KPE + QK_ROPE
S_XBC = S_Z + SSM_INNER
S_DTF = S_XBC + SSM_CONV_CH
S_GATE = S_DTF + 2 * SSM_HEADS
IN_COLS = S_GATE + 3 * D_MODEL
W_SEGMENTS = ((C_A3, 0, 3 * CONV_DIM), (C_CQ, S_CQ, Q_RANK), (C_CKV, S_CKV, KV_RANK), (C_Z, S_Z, SSM_INNER),
              (C_XBC, S_XBC, SSM_CONV_CH), (C_GATE, S_GATE, 3 * D_MODEL))
W_PIECE = 512


def _rms(x, w):
    return x * lax.rsqrt(jnp.mean(x * x, axis=-1, keepdims=True) + NORM_EPS) * w


def _silu(x):
    return x * jax.nn.sigmoid(x)


def _dot(a, b):
    return jnp.dot(a, b, preferred_element_type=F32)


def _dot_nt(a, b):
    return lax.dot_general(a, b, (((1,), (1,)), ((), ())), preferred_element_type=F32)


def _resident(shape):
    nd = len(shape)
    return pl.BlockSpec(shape, lambda *_: (0,) * nd, pipeline_mode=pl.Buffered(1))


def _mod_row(i):
    return jnp.where(i < CTX_TILES, 0, 1 + (i - CTX_TILES) // LAT_TILES_PER_SEQ)


def _pos_block(i):
    return jnp.where(i < CTX_TILES, 0, 1 + (i - CTX_TILES) % LAT_TILES_PER_SEQ)


def _ada_kernel(c_ref, w_ref, b_ref, o_ref):
    s = _silu(c_ref[...]).astype(BF16)
    o_ref[0] = _dot(s, w_ref[0].astype(BF16)) + b_ref[0]


def _ada_mods(cond8, w_ada, b_ada):
    tn = 1536
    n_mod = 6 * D_MODEL
    return pl.pallas_call(
        _ada_kernel,
        grid=(DEPTH, n_mod // tn),
        in_specs=[
            pl.BlockSpec((SUBLANES, D_MODEL), lambda l, j: (0, 0)),
            pl.BlockSpec((1, D_MODEL, tn), lambda l, j: (l, 0, j)),
            pl.BlockSpec((1, 1, tn), lambda l, j: (l, 0, j)),
        ],
        out_specs=pl.BlockSpec((1, SUBLANES, tn), lambda l, j: (l, 0, j)),
        out_shape=jax.ShapeDtypeStruct((DEPTH, SUBLANES, n_mod), F32),
        compiler_params=pltpu.CompilerParams(dimension_semantics=("arbitrary", "arbitrary")),
        name="ada_mods",
    )(cond8, w_ada, b_ada.reshape(DEPTH, 1, n_mod))


def _relayout_w_in(layer, wt_hbm, w2, stg, small_stg, sem, small_sem):
    pieces = [(dst + p, src + p, min(W_PIECE, width - p))
              for dst, src, width in W_SEGMENTS for p in range(0, width, W_PIECE)]
    fetch = lambda k: pltpu.make_async_copy(wt_hbm.at[layer, pl.ds(pieces[k][1], pieces[k][2]), :],
                                            stg.at[k % 2, pl.ds(0, pieces[k][2]), :], sem.at[k % 2])
    small_stg[...] = jnp.zeros_like(small_stg)
    small_copies = [
        pltpu.make_async_copy(wt_hbm.at[layer, pl.ds(S_KPE, QK_ROPE), :], small_stg.at[pl.ds(0, QK_ROPE), :],
                              small_sem.at[0]),
        pltpu.make_async_copy(wt_hbm.at[layer, pl.ds(S_DTF, 2 * SSM_HEADS), :],
                              small_stg.at[pl.ds(SM_DTF, 2 * SSM_HEADS), :], small_sem.at[1]),
    ]
    for cp in small_copies:
        cp.start()
    fetch(0).start()
    for k, (dst, _, width) in enumerate(pieces):
        if k + 1 < len(pieces):
            fetch(k + 1).start()
        fetch(k).wait()
        w2[:, dst:dst + width] = stg[k % 2, 0:width, :].T.astype(BF16)
    for cp in small_copies:
        cp.wait()
    w2[:, C_SMALL:IN_COLS2] = small_stg[...].T.astype(BF16)


def _rot_partner(x):
    n = x.shape[1]
    lane = lax.broadcasted_iota(I32, x.shape, 1)
    quarter = QK_ROPE // 4
    return jnp.where(lane % (2 * quarter) < quarter, pltpu.roll(x, n - quarter, 1), pltpu.roll(x, quarter, 1))


def _inproj_kernel(layer, xc_ref, xcp_ref, xcn_ref, xl_ref, xlp_ref, xln_ref, mod_ref, n1w_ref, w_hbm,
                   qnw_ref, wq_ref, kvnw_ref, wuk_ref,
                   wuv_ref, wuvc_ref, vone_ref, tile_ref, cosq_ref, sinq_ref, cosk_ref, sink_ref, caw_ref, cw_ref,
                   cb_ref,
                   ta_ref, q_ref, ckv_ref, kpe_ref, kf_ref, vc_ref, vl_ref, small_ref, z_ref, xact_ref, g_ref,
                   w2, stg, small_stg, sem, small_sem):
    i = pl.program_id(0)

    @pl.when(i == 0)
    def _():
        _relayout_w_in(layer, w_hbm, w2, stg, small_stg, sem, small_sem)

    mod = mod_ref[0]
    shift1 = mod[:, 0:D_MODEL]
    scale1 = mod[:, D_MODEL:2 * D_MODEL]
    x_ext = jnp.where(i < CTX_TILES, jnp.concatenate([xcp_ref[...], xc_ref[...], xcn_ref[...]], axis=0),
                      jnp.concatenate([xlp_ref[...], xl_ref[...], xln_ref[...]], axis=0))
    h_ext = _rms(x_ext, n1w_ref[...]) * (1.0 + scale1) + shift1
    hb_ext = h_ext.astype(BF16)
    hb = h_ext[SUBLANES:SUBLANES + TM].astype(BF16)

    j = (i - CTX_TILES) % LAT_TILES_PER_SEQ
    is_ctx = i < CTX_TILES
    keep_prev = jnp.where(jnp.logical_or(is_ctx, j == 0), 0.0, 1.0)
    keep_next = jnp.where(jnp.logical_or(is_ctx, j == LAT_TILES_PER_SEQ - 1), 0.0, 1.0)
    row = lax.broadcasted_iota(I32, (TM, 1), 0)
    prev_mask = jnp.where(row == 0, keep_prev, 1.0)
    next_mask = jnp.where(row == TM - 1, keep_next, 1.0)
    ext = TM + 2 * SUBLANES

    def conv3(x, w_ref):
        x_prev = pltpu.roll(x, 1, 0)[SUBLANES:SUBLANES + TM] * prev_mask
        x_next = pltpu.roll(x, ext - 1, 0)[SUBLANES:SUBLANES + TM] * next_mask
        return x_prev * w_ref[0:1, :] + x[SUBLANES:SUBLANES + TM] * w_ref[1:2, :] + x_next * w_ref[2:3, :]

    a3 = _dot(hb_ext, w2[:, C_A3:C_CQ])
    s = a3[:, 2 * CONV_DIM:3 * CONV_DIM] * a3[:, 0:CONV_DIM]
    ta_ref[...] = (a3[SUBLANES:SUBLANES + TM, CONV_DIM:2 * CONV_DIM] * conv3(s, caw_ref)).astype(BF16)
    xact_ref[...] = _silu(conv3(_dot(hb_ext, w2[:, C_XBC:C_GATE]), cw_ref) + cb_ref[...])

    def seg(a, b):
        return _dot(hb, w2[:, a:b])

    cqn = _rms(seg(C_CQ, C_CKV), qnw_ref[...]).astype(BF16)
    qa = _dot(cqn, wq_ref[...])
    qa_rot = _rot_partner(qa)
    for h in range(MLA_HEADS):
        hs = slice(h * HEAD_PAD, (h + 1) * HEAD_PAD)
        q_ref[:, hs] = (qa[:, hs] * cosq_ref[...] + qa_rot[:, hs] * sinq_ref[...]).astype(BF16)

    ckv = _rms(seg(C_CKV, C_Z), kvnw_ref[...])
    ckvb = ckv.astype(BF16)
    small = seg(C_SMALL, IN_COLS2)
    small_ref[...] = small
    kpe = small * cosk_ref[...] + _rot_partner(small) * sink_ref[...]
    kf_ref[...] = (_dot(ckvb, wuk_ref[...]) + _dot(kpe.astype(BF16), tile_ref[...])).astype(BF16)
    v_lat = (_dot(ckvb, wuv_ref[...]) + vone_ref[...]).astype(BF16)
    v_ctx = _dot(ckvb, wuvc_ref[...]).astype(BF16)

    z_ref[...] = seg(C_Z, C_XBC)
    g_ref[...] = jax.nn.sigmoid(seg(C_GATE, C_SMALL))

    @pl.when(i < CTX_TILES)
    def _():
        ckv_ref[...] = ckv
        kpe_ref[...] = small[:, 0:QK_ROPE]
        vc_ref[...] = v_ctx

    @pl.when(i >= CTX_TILES)
    def _():
        vl_ref[...] = v_lat


def _inproj(layer, xc, xl, mod3, w_in, lw, tabs):
    row = lambda n: pl.BlockSpec((TM, n), lambda i: (i, 0))
    tab = lambda n: pl.BlockSpec((TM, n), lambda i: (_pos_block(i), 0))
    per = TM // SUBLANES
    qw = MLA_HEADS * HEAD_PAD
    ctx_row = lambda n: pl.BlockSpec((TM, n), lambda i: (jnp.minimum(i, CTX_TILES - 1), 0))
    lat_row = lambda n: pl.BlockSpec((TM, n), lambda i: (jnp.maximum(i - CTX_TILES, 0), 0))

    def halo(first_tile, n_rows, side):
        last = n_rows // SUBLANES - 1
        return pl.BlockSpec((SUBLANES, D_MODEL),
                            lambda i: (jnp.clip((i - first_tile + side) * per - 1 + side, 0, last), 0))

    out_shape = (
        jax.ShapeDtypeStruct((N_TOK, CONV_DIM), BF16),
        jax.ShapeDtypeStruct((N_TOK, qw), BF16),
        jax.ShapeDtypeStruct((N_CTX, KV_RANK), F32),
        jax.ShapeDtypeStruct((N_CTX, QK_ROPE), F32),
        jax.ShapeDtypeStruct((N_TOK, qw), BF16),
        jax.ShapeDtypeStruct((N_CTX, MLA_HEADS * V_HEAD), BF16),
        jax.ShapeDtypeStruct((N_LAT, qw), BF16),
        jax.ShapeDtypeStruct((N_TOK, LANES), F32),
        jax.ShapeDtypeStruct((N_TOK, SSM_INNER), F32),
        jax.ShapeDtypeStruct((N_TOK, SSM_CONV_CH), F32),
        jax.ShapeDtypeStruct((N_TOK, 3 * D_MODEL), F32),
    )
    out_specs = [row(s.shape[1]) for s in out_shape]
    out_specs[2] = ctx_row(KV_RANK)
    out_specs[3] = ctx_row(QK_ROPE)
    out_specs[5] = ctx_row(MLA_HEADS * V_HEAD)
    out_specs[6] = lat_row(qw)
    return pl.pallas_call(
        functools.partial(_inproj_kernel, layer),
        grid=(N_TILES,),
        in_specs=[
            ctx_row(D_MODEL), halo(0, N_CTX, 0), halo(0, N_CTX, 1),
            lat_row(D_MODEL), halo(CTX_TILES, N_LAT, 0), halo(CTX_TILES, N_LAT, 1),
            pl.BlockSpec((1, 1, 6 * D_MODEL), lambda i: (_mod_row(i), 0, 0)),
            _resident((1, D_MODEL)),
            pl.BlockSpec(memory_space=pl.ANY),
            _resident((1, Q_RANK)),
            _resident((Q_RANK, qw)),
            _resident((1, KV_RANK)),
            _resident((KV_RANK, qw)),
            _resident((KV_RANK, qw)),
            _resident((KV_RANK, MLA_HEADS * V_HEAD)),
            _resident((1, qw)),
            _resident((LANES, qw)),
            tab(HEAD_PAD), tab(HEAD_PAD), tab(LANES), tab(LANES),
            _resident((3, CONV_DIM)), _resident((3, SSM_CONV_CH)), _resident((1, SSM_CONV_CH)),
        ],
        out_specs=tuple(out_specs),
        out_shape=out_shape,
        scratch_shapes=[pltpu.VMEM((D_MODEL, IN_COLS2), BF16), pltpu.VMEM((2, W_PIECE, D_MODEL), F32),
                        pltpu.VMEM((LANES, D_MODEL), F32), pltpu.SemaphoreType.DMA((2,)),
                        pltpu.SemaphoreType.DMA((2,))],
        compiler_params=pltpu.CompilerParams(dimension_semantics=("arbitrary",), vmem_limit_bytes=VMEM_LIMIT),
        name="inproj",
    )(xc, xc, xc, xl, xl, xl, mod3, lw["norm1_w"], w_in, lw["q_norm_w"], lw["wq_a"], lw["kv_norm_w"],
      lw["wuk"], lw["wuv"], lw["wuv_c"], tabs["vone"], tabs["tile"], tabs["cosq"], tabs["sinq"], tabs["cosk"],
      tabs["sink"], lw["conv_a_w"], lw["ssm_conv_w"], lw["ssm_conv_b"])


def _kvcache_kernel(ckv_ref, kpe_ref, wuk_ref, wuv_ref, vone_ref, tile_ref, kf_ref, v_ref):
    ckvb = ckv_ref[...].astype(BF16)
    kf_ref[...] = (_dot(ckvb, wuk_ref[...]) + _dot(kpe_ref[...].astype(BF16), tile_ref[...])).astype(BF16)
    v_ref[...] = (_dot(ckvb, wuv_ref[...]) + vone_ref[...]).astype(BF16)


def _kvcache(ckv, kpe128, lw, tabs):
    n = ckv.shape[0]
    qw = MLA_HEADS * HEAD_PAD
    return pl.pallas_call(
        _kvcache_kernel,
        grid=(n // PAST_LEN,),
        in_specs=[
            pl.BlockSpec((PAST_LEN, KV_RANK), lambda i: (i, 0)),
            pl.BlockSpec((PAST_LEN, LANES), lambda i: (i, 0)),
            _resident((KV_RANK, qw)),
            _resident((KV_RANK, qw)),
            _resident((1, qw)),
            _resident((LANES, qw)),
        ],
        out_specs=(pl.BlockSpec((PAST_LEN, qw), lambda i: (i, 0)), pl.BlockSpec((PAST_LEN, qw), lambda i: (i, 0))),
        out_shape=(jax.ShapeDtypeStruct((n, qw), BF16), jax.ShapeDtypeStruct((n, qw), BF16)),
        compiler_params=pltpu.CompilerParams(dimension_semantics=("arbitrary",)),
        name="kvcache",
    )(ckv, kpe128, lw["wuk"], lw["wuv"], tabs["vone"], tabs["tile"])


def _attn_heads(q_ref, kv_refs, o_ref, acc_ref, denom_from_matmul):
    log2_scale = MLA_SCALE * math.log2(math.e)
    for h in range(MLA_HEADS):
        hs = slice(h * HEAD_PAD, (h + 1) * HEAD_PAD)
        qh = q_ref[:, hs]
        ss = [_dot_nt(qh, k_ref[:, hs]) for k_ref, _ in kv_refs]
        if denom_from_matmul:
            m = functools.reduce(jnp.maximum, [jnp.max(s, axis=-1, keepdims=True) for s in ss])
            ps = [jnp.exp2((s - m) * log2_scale) for s in ss]
            ol = functools.reduce(jnp.add, [_dot(p.astype(BF16), v_ref[:, hs])
                                            for p, (_, v_ref) in zip(ps, kv_refs)])
            acc_ref[:, h * V_HEAD:(h + 1) * V_HEAD] = ol[:, 0:V_HEAD] / ol[:, V_HEAD:V_HEAD + 1]
        else:
            ss = [s * MLA_SCALE for s in ss]
            m = functools.reduce(jnp.maximum, [jnp.max(s, axis=-1, keepdims=True) for s in ss])
            ps = [jnp.exp(s - m) for s in ss]
            l = functools.reduce(jnp.add, [jnp.sum(p, axis=-1, keepdims=True) for p in ps])
            o = functools.reduce(jnp.add, [_dot(p.astype(BF16), v_ref[:, h * V_HEAD:(h + 1) * V_HEAD])
                                           for p, (_, v_ref) in zip(ps, kv_refs)])
            acc_ref[:, h * V_HEAD:(h + 1) * V_HEAD] = o / l
    o_ref[...] = acc_ref[...].astype(BF16)


def _attn_ctx_kernel(q_ref, k_ref, v_ref, o_ref, acc_ref):
    _attn_heads(q_ref, [(k_ref, v_ref)], o_ref, acc_ref, denom_from_matmul=False)


def _attn_lat_kernel(q_ref, k_ref, v_ref, kc_ref, vc_ref, o_ref, acc_ref):
    _attn_heads(q_ref, [(k_ref, v_ref), (kc_ref, vc_ref)], o_ref, acc_ref, denom_from_matmul=True)


def _attention(q, kf, v_ctx, v_lat, kf_c, v_c):
    qw = MLA_HEADS * HEAD_PAD
    vw = MLA_HEADS * V_HEAD
    att_ctx = pl.pallas_call(
        _attn_ctx_kernel,
        grid=(BATCH,),
        in_specs=[pl.BlockSpec((SEQ, qw), lambda b: (b, 0)), pl.BlockSpec((SEQ, qw), lambda b: (b, 0)),
                  pl.BlockSpec((SEQ, vw), lambda b: (b, 0))],
        out_specs=pl.BlockSpec((SEQ, vw), lambda b: (b, 0)),
        out_shape=jax.ShapeDtypeStruct((N_CTX, vw), BF16),
        scratch_shapes=[pltpu.VMEM((SEQ, vw), F32)],
        compiler_params=pltpu.CompilerParams(dimension_semantics=("arbitrary",)),
        name="attn_ctx",
    )(q, kf, v_ctx)
    lat0 = N_CTX // DEC_SEQ
    att_lat = pl.pallas_call(
        _attn_lat_kernel,
        grid=(DEC_BATCH, LAT_TILES_PER_SEQ),
        in_specs=[
            pl.BlockSpec((TM, qw), lambda b, t: (CTX_TILES + b * LAT_TILES_PER_SEQ + t, 0)),
            pl.BlockSpec((DEC_SEQ, qw), lambda b, t: (lat0 + b, 0)),
            pl.BlockSpec((DEC_SEQ, qw), lambda b, t: (b, 0)),
            pl.BlockSpec((PAST_LEN, qw), lambda b, t: (b, 0)),
            pl.BlockSpec((PAST_LEN, qw), lambda b, t: (b, 0)),
        ],
        out_specs=pl.BlockSpec((TM, vw), lambda b, t: (b * LAT_TILES_PER_SEQ + t, 0)),
        out_shape=jax.ShapeDtypeStruct((N_LAT, vw), BF16),
        scratch_shapes=[pltpu.VMEM((TM, vw), F32)],
        compiler_params=pltpu.CompilerParams(dimension_semantics=("arbitrary", "arbitrary"),
                                             vmem_limit_bytes=VMEM_LIMIT),
        name="attn_lat",
    )(q, kf, v_lat, kf_c, v_c)
    return att_ctx, att_lat


def _seq_of_chunk(s):
    return jnp.where(s < N_CTX_CHUNKS, s // CTX_CHUNKS_PER_SEQ,
                     BATCH + (s - N_CTX_CHUNKS) // LAT_CHUNKS_PER_SEQ)


def _chunk_in_seq(s):
    return jnp.where(s < N_CTX_CHUNKS, s % CTX_CHUNKS_PER_SEQ, (s - N_CTX_CHUNKS) % LAT_CHUNKS_PER_SEQ)


def _chunks_in_seq(s):
    return jnp.where(s < N_CTX_CHUNKS, CTX_CHUNKS_PER_SEQ, LAT_CHUNKS_PER_SEQ)


def _mirror_chunk(s):
    return s + _chunks_in_seq(s) - 1 - 2 * _chunk_in_seq(s)


def _split3(a):
    a1 = a.astype(BF16)
    r1 = a - a1.astype(F32)
    a2 = r1.astype(BF16)
    a3 = (r1 - a2.astype(F32)).astype(BF16)
    return a1, a2, a3


def _ssd_direction(x_ref, sm_ref, par_ref, st_ref, y_ref, lane0, backward):
    ri = lax.broadcasted_iota(I32, (CHUNK, CHUNK), 0)
    ci = lax.broadcasted_iota(I32, (CHUNK, CHUNK), 1)
    tri = (ci >= ri) if backward else (ci <= ri)
    tri_b = jnp.where(tri, 1.0, 0.0).astype(BF16)
    tot_row = 0 if backward else CHUNK - 1

    dt = jax.nn.softplus(sm_ref[...] + par_ref[0:1, :])
    a = dt * (-jnp.exp(par_ref[1:2, :])) * par_ref[2:3, :]
    a1, a2, a3 = _split3(a)
    acs = (_dot(tri_b, a1) + _dot(tri_b, a2) + _dot(tri_b, a3)) * math.log2(math.e)
    acs_t = acs.T
    dt_t = dt.T
    first_head = lax.broadcasted_iota(I32, (1, LANES), 1) < SSM_HEAD_DIM

    def block_diag(pair):
        return jnp.concatenate([jnp.where(first_head, pair, 0.0), jnp.where(first_head, 0.0, pair)],
                               axis=0).astype(BF16)

    for g in range(SSM_GROUPS):
        b0 = SSM_INNER + g * SSM_STATE
        c0 = SSM_INNER + SSM_GROUPS * SSM_STATE + g * SSM_STATE
        bg = x_ref[:, b0:b0 + SSM_STATE]
        cg = x_ref[:, c0:c0 + SSM_STATE]
        cb = _dot_nt(cg.astype(BF16), bg.astype(BF16))
        bg_t = bg.T
        heads = SSM_HEADS // SSM_GROUPS
        for pr in range(heads // 2):
            h0 = g * heads + 2 * pr
            sl = slice(h0 * SSM_HEAD_DIM, (h0 + 2) * SSM_HEAD_DIM)
            x_bd = block_diag(x_ref[:, sl])
            st_old = st_ref[:, sl]
            within, carried, to_state, keep = [], [], [], []
            for h in (h0, h0 + 1):
                lane = lane0 + h
                col = jnp.broadcast_to(acs[:, lane:lane + 1], (CHUNK, CHUNK))
                row = acs_t[lane:lane + 1, :]
                dt_row = dt_t[lane:lane + 1, :]
                decay = jnp.exp2(jnp.where(tri, col - row, -jnp.inf))
                within.append((cb * decay * dt_row).astype(BF16))
                carried.append((cg * jnp.exp2(col)).astype(BF16))
                tot = acs[tot_row:tot_row + 1, lane:lane + 1]
                to_state.append((bg_t * (dt_row * jnp.exp2(tot - row))).astype(BF16))
                keep.append(jnp.exp2(tot))
            lhs = jnp.concatenate(within + carried, axis=1)
            y_ref[:, sl] = _dot(lhs, jnp.concatenate([x_bd, block_diag(st_old)], axis=0))
            st_ref[:, sl] = (st_old * jnp.where(first_head, keep[0], keep[1])
                             + _dot(jnp.concatenate(to_state, axis=1), x_bd))


def _ssd_kernel(xf_ref, xb_ref, smf_ref, smb_ref, if_ref, ib_ref, par_ref,
                yf_ref, yb_ref, sf_ref, sb_ref, stf_ref, stb_ref):
    s = pl.program_id(0)
    c = _chunk_in_seq(s)

    @pl.when(jnp.logical_and(c == 0, s < N_CTX_CHUNKS))
    def _():
        stf_ref[...] = jnp.zeros_like(stf_ref)
        stb_ref[...] = jnp.zeros_like(stb_ref)

    @pl.when(jnp.logical_and(c == 0, s >= N_CTX_CHUNKS))
    def _():
        stf_ref[...] = if_ref[0].T
        stb_ref[...] = ib_ref[0].T

    _ssd_direction(xf_ref, smf_ref, par_ref, stf_ref, yf_ref, SM_DTF, False)
    _ssd_direction(xb_ref, smb_ref, par_ref, stb_ref, yb_ref, SM_DTB, True)

    @pl.when(jnp.logical_and(c == _chunks_in_seq(s) - 1, s < N_CTX_CHUNKS))
    def _():
        sf_ref[0] = stf_ref[...].T
        sb_ref[0] = stb_ref[...].T


def _ssd(xact, small, init_f, init_b, par):
    hp = SSM_INNER
    fwd = lambda n: pl.BlockSpec((CHUNK, n), lambda s: (s, 0))
    bwd = lambda n: pl.BlockSpec((CHUNK, n), lambda s: (_mirror_chunk(s), 0))
    st = pl.BlockSpec((1, hp, SSM_STATE), lambda s: (jnp.minimum(_seq_of_chunk(s), BATCH - 1), 0, 0))
    init = pl.BlockSpec((1, hp, SSM_STATE), lambda s: (jnp.maximum(_seq_of_chunk(s) - BATCH, 0), 0, 0))
    return pl.pallas_call(
        _ssd_kernel,
        grid=(N_CHUNKS,),
        in_specs=[fwd(SSM_CONV_CH), bwd(SSM_CONV_CH), fwd(LANES), bwd(LANES), init, init,
                  pl.BlockSpec((SUBLANES, LANES), lambda s: (0, 0))],
        out_specs=(fwd(hp), bwd(hp), st, st),
        out_shape=(jax.ShapeDtypeStruct((N_TOK, hp), F32), jax.ShapeDtypeStruct((N_TOK, hp), F32),
                   jax.ShapeDtypeStruct((BATCH, hp, SSM_STATE), F32),
                   jax.ShapeDtypeStruct((BATCH, hp, SSM_STATE), F32)),
        scratch_shapes=[pltpu.VMEM((SSM_STATE, hp), F32), pltpu.VMEM((SSM_STATE, hp), F32)],
        compiler_params=pltpu.CompilerParams(dimension_semantics=("arbitrary",)),
        name="ssd",
    )(xact, xact, small, small, init_f, init_b, par)


def _merge_kernel(ta_ref, attc_ref, attl_ref, yf_ref, yb_ref, xs_ref, z_ref, g_ref, xc_ref, xl_ref, mod_ref,
                  woa_ref, wom_ref, dsk_ref, snw_ref, wos_ref, wo_ref, n2w_ref, rw_ref, rb_ref,
                  x1_ref, h2_ref, ti_ref, tg_ref, cnt_ref):
    mod = mod_ref[0]
    gate1 = mod[:, 2 * D_MODEL:3 * D_MODEL]
    shift2 = mod[:, 3 * D_MODEL:4 * D_MODEL]
    scale2 = mod[:, 4 * D_MODEL:5 * D_MODEL]
    y_a = _dot(ta_ref[...], woa_ref[...])
    att = jnp.where(pl.program_id(0) < CTX_TILES, attc_ref[...].astype(F32), attl_ref[...].astype(F32))
    y_b = _dot(att.astype(BF16), wom_ref[...])
    y_ssm = (yf_ref[...] + yb_ref[...] + dsk_ref[...] * xs_ref[...]) * _silu(z_ref[...])
    y_c = _dot(_rms(y_ssm, snw_ref[...]).astype(BF16), wos_ref[...])
    merged = (g_ref[:, 0:D_MODEL] * y_a + g_ref[:, D_MODEL:2 * D_MODEL] * y_b
              + g_ref[:, 2 * D_MODEL:3 * D_MODEL] * y_c)
    x = jnp.where(pl.program_id(0) < CTX_TILES, xc_ref[...], xl_ref[...])
    x1 = x + gate1 * _dot(merged.astype(BF16), wo_ref[...])
    x1_ref[...] = x1
    h2 = _rms(x1, n2w_ref[...]) * (1.0 + scale2) + shift2
    h2b = h2.astype(BF16)
    h2_ref[...] = h2b

    logits = (_dot(h2b, rw_ref[...]) + rb_ref[...]).T[0:N_EXPERTS, :]
    expert = lax.broadcasted_iota(I32, (N_EXPERTS, TM), 0)
    ids, vals = [], []
    for k in range(TOP_K):
        m = jnp.max(logits, axis=0, keepdims=True)
        idx = jnp.min(jnp.where(logits == m, expert, N_EXPERTS), axis=0, keepdims=True)
        ids.append(idx)
        vals.append(m)
        logits = jnp.where(expert == idx, -jnp.inf, logits)
    es = [jnp.exp(v - vals[0]) for v in vals]
    denom = functools.reduce(jnp.add, es)
    srow = lax.broadcasted_iota(I32, (LANES, TM), 0)
    ti_t = jnp.zeros((LANES, TM), F32)
    tg_t = jnp.zeros((LANES, TM), F32)
    chosen_t = jnp.zeros((LANES, TM), F32)
    for k in range(TOP_K):
        ti_t = jnp.where(srow == k, ids[k].astype(F32), ti_t)
        tg_t = jnp.where(srow == k, es[k] / denom, tg_t)
        chosen_t = jnp.where(srow == ids[k], 1.0, chosen_t)
    ti_ref[...] = ti_t.T.astype(I32)
    tg_ref[...] = tg_t.T
    cnt_ref[0] = _dot_nt(jnp.ones((SUBLANES, TM), BF16), chosen_t.astype(BF16)).astype(I32)


def _merge(ta, att_ctx, att_lat, yf, yb, xact, z, g, xc, xl, mod3, lw):
    row = lambda n: pl.BlockSpec((TM, n), lambda i: (i, 0))
    vw = MLA_HEADS * V_HEAD
    out_shape = (jax.ShapeDtypeStruct((N_TOK, D_MODEL), F32), jax.ShapeDtypeStruct((N_TOK, D_MODEL), BF16),
                 jax.ShapeDtypeStruct((N_TOK, LANES), I32), jax.ShapeDtypeStruct((N_TOK, LANES), F32))
    cnt_shape = jax.ShapeDtypeStruct((N_TILES, SUBLANES, LANES), I32)
    cnt_spec = pl.BlockSpec((1, SUBLANES, LANES), lambda i: (i, 0, 0))
    return pl.pallas_call(
        _merge_kernel,
        grid=(N_TILES,),
        in_specs=[
            row(CONV_DIM),
            pl.BlockSpec((TM, vw), lambda i: (jnp.minimum(i, CTX_TILES - 1), 0)),
            pl.BlockSpec((TM, vw), lambda i: (jnp.maximum(i - CTX_TILES, 0), 0)),
            row(SSM_INNER), row(SSM_INNER), row(SSM_INNER),
            row(SSM_INNER), row(3 * D_MODEL),
            pl.BlockSpec((TM, D_MODEL), lambda i: (jnp.minimum(i, CTX_TILES - 1), 0)),
            pl.BlockSpec((TM, D_MODEL), lambda i: (jnp.maximum(i - CTX_TILES, 0), 0)),
            pl.BlockSpec((1, 1, 6 * D_MODEL), lambda i: (_mod_row(i), 0, 0)),
            _resident((CONV_DIM, D_MODEL)), _resident((MLA_HEADS * V_HEAD, D_MODEL)),
            _resident((1, SSM_INNER)), _resident((1, SSM_INNER)), _resident((SSM_INNER, D_MODEL)),
            _resident((D_MODEL, D_MODEL)), _resident((1, D_MODEL)),
            _resident((D_MODEL, LANES)), _resident((1, LANES)),
        ],
        out_specs=tuple(row(s.shape[1]) for s in out_shape) + (cnt_spec,),
        out_shape=out_shape + (cnt_shape,),
        compiler_params=pltpu.CompilerParams(dimension_semantics=("arbitrary",), vmem_limit_bytes=VMEM_LIMIT),
        name="merge",
    )(ta, att_ctx, att_lat, yf, yb, xact, z, g, xc, xl, mod3, lw["w_out_a"], lw["w_o_mla"], lw["d_skip"],
      lw["ssm_norm_w"],
      lw["w_o_ssm"], lw["w_o"], lw["norm2_w"], lw["router_w"], lw["router_b"])


def _run_copies(cnt_ref, src_ref, dst_ref, first, count, bits, make_copy, start):
    def body(e, carry):
        n = cnt_ref[first + e]
        s0 = src_ref[first + e] if src_ref is not None else 0
        d0 = dst_ref[first + e]

        def pieces(some_bits):
            for b in some_bits:
                above = (n >> (b + 1)) << (b + 1)

                @pl.when(((n >> b) & 1) == 1)
                def _():
                    cp = make_copy(pl.multiple_of(s0 + above, RUN_ALIGN), pl.multiple_of(d0 + above, RUN_ALIGN),
                                   1 << b)
                    if start:
                        cp.start()
                    else:
                        cp.wait()

        large = [b for b in bits if b >= COMMON_BITS]

        @pl.when(n >= (1 << COMMON_BITS))
        def _():
            pieces(large)
        pieces([b for b in bits if b < COMMON_BITS])
        return carry
    lax.fori_loop(0, count, body, 0)


def _wait_rows(total, make_copy):
    for b in WAIT_BITS:
        @pl.when(((total >> b) & 1) == 1)
        def _():
            make_copy(0, 0, 1 << b).wait()


def _dispatch_kernel(cnt_ref, off_ref, run_ref, tcnt_ref, tdst_ref, nu_ref, h2_ref, ti_ref, tg_ref, offv_ref,
                     xs_ref, lp_ref, stage, zeros, perm_s, ghi_s, glo_s, sem, semz):
    i = pl.program_id(0)
    slot = i % 2

    def copy_out(s_):
        return lambda s, d, n: pltpu.make_async_copy(stage.at[s_, pl.ds(s, n), :], xs_ref.at[pl.ds(d, n), :],
                                                     sem.at[s_])

    def tile_rows(t):
        last = t * N_EXPERTS + N_EXPERTS - 1
        return off_ref[last] + cnt_ref[last]

    @pl.when(i == 0)
    def _():
        zeros[...] = jnp.zeros_like(zeros)
        zero_out = lambda s, d, n: pltpu.make_async_copy(zeros.at[pl.ds(0, n), :], xs_ref.at[pl.ds(d, n), :], semz)
        _run_copies(tcnt_ref, None, tdst_ref, 0, N_EXPERTS, TAIL_BITS, zero_out, True)
        _run_copies(tcnt_ref, None, tdst_ref, 0, N_EXPERTS, TAIL_BITS, zero_out, False)
        zrows = zeros.shape[0]

        def unused_blocks(start):
            def body(b, carry):
                for part in range(MOE_BLK // zrows):
                    cp = zero_out(0, pl.multiple_of(b * MOE_BLK + part * zrows, RUN_ALIGN), zrows)
                    if start:
                        cp.start()
                    else:
                        cp.wait()
                return carry
            lax.fori_loop(nu_ref[0], N_BLOCKS, body, 0)
        unused_blocks(True)
        unused_blocks(False)

    @pl.when(i >= 2)
    def _():
        _wait_rows(tile_rows(i - 2), copy_out(slot))

    lane = lax.broadcasted_iota(I32, (TM, LANES), 1)
    picks = [jnp.where(lane == ti_ref[:, k:k + 1], 1.0, 0.0) for k in range(TOP_K)]
    ri = lax.broadcasted_iota(I32, (TM, TM), 0)
    ci = lax.broadcasted_iota(I32, (TM, TM), 1)
    earlier = jnp.where(ci < ri, 1.0, 0.0).astype(BF16)
    base = _dot(earlier, functools.reduce(jnp.add, picks).astype(BF16)) + offv_ref[0][0:1, :]
    lp = jnp.full((TM, LANES), -1.0, F32)
    for k in range(TOP_K):
        lp = jnp.where(lane == k, jnp.sum(picks[k] * base, axis=-1, keepdims=True), lp)
    lp_ref[...] = lp.astype(I32)

    lp_t = lp.T
    tg_t = tg_ref[...].T
    for c in range(STAGE_ROWS // LANES):
        rows = slice(c * LANES, (c + 1) * LANES)
        row = (lax.broadcasted_iota(I32, (LANES, TM), 0) + c * LANES).astype(F32)
        perm = jnp.zeros((LANES, TM), F32)
        gates = jnp.zeros((LANES, TM), F32)
        for k in range(TOP_K):
            hit = row == lp_t[k:k + 1, :]
            perm = jnp.where(hit, 1.0, perm)
            gates = jnp.where(hit, tg_t[k:k + 1, :], gates)
        perm_s[rows, :] = perm.astype(BF16)
        g_hi = gates.astype(BF16)
        ghi_s[rows, :] = g_hi
        glo_s[rows, :] = (gates - g_hi.astype(F32)).astype(BF16)
    stage[slot, :, 0:D_MODEL] = _dot(perm_s[...], h2_ref[...])
    ones = jnp.ones((TM, LANES), BF16)
    stage[slot, :, D_MODEL:XS_COLS] = _dot(ghi_s[...], ones) + _dot(glo_s[...], ones)

    _run_copies(cnt_ref, off_ref, run_ref, i * N_EXPERTS, N_EXPERTS, RUN_BITS, copy_out(slot), True)

    @pl.when(i == N_TILES - 1)
    def _():
        _wait_rows(tile_rows(i - 1), copy_out(1 - slot))
        _wait_rows(tile_rows(i), copy_out(slot))


def _dispatch(rt, h2, top_i, top_g):
    row = lambda n: pl.BlockSpec((TM, n), lambda i, *_: (i, 0))
    return pl.pallas_call(
        _dispatch_kernel,
        grid_spec=pltpu.PrefetchScalarGridSpec(
            num_scalar_prefetch=6,
            grid=(N_TILES,),
            in_specs=[row(D_MODEL), row(LANES), row(LANES),
                      pl.BlockSpec((1, SUBLANES, LANES), lambda i, *_: (i, 0, 0))],
            out_specs=(pl.BlockSpec(memory_space=pl.ANY), row(LANES)),
            scratch_shapes=[
                pltpu.VMEM((2, STAGE_ROWS, XS_COLS), F32),
                pltpu.VMEM((1 << TAIL_BITS[0], XS_COLS), F32),
                pltpu.VMEM((STAGE_ROWS, TM), BF16), pltpu.VMEM((STAGE_ROWS, TM), BF16),
                pltpu.VMEM((STAGE_ROWS, TM), BF16),
                pltpu.SemaphoreType.DMA((2,)),
                pltpu.SemaphoreType.DMA,
            ],
        ),
        out_shape=(jax.ShapeDtypeStruct((N_SLOTS, XS_COLS), F32), jax.ShapeDtypeStruct((N_TOK, LANES), I32)),
        compiler_params=pltpu.CompilerParams(dimension_semantics=("arbitrary",), vmem_limit_bytes=VMEM_LIMIT),
        name="dispatch",
    )(rt["cnt"], rt["off"], rt["run"], rt["tail_cnt"], rt["tail_dst"], rt["n_used"], h2, top_i, top_g, rt["off_v"])


def _moe_kernel(layer, be_ref, nxt_ref, nv_ref, nu_ref, x_ref, wgu_hbm, wdn_hbm, bg_ref, bu_ref, bd_ref, sel_ref,
                y_ref, wgu_ref, wdn_ref, wgu_s, wdn_s, sem):
    i = pl.program_id(0)
    n_used = nu_ref[0]

    def fetch(e):
        return (pltpu.make_async_copy(wgu_hbm.at[layer, e], wgu_ref, sem.at[0]),
                pltpu.make_async_copy(wdn_hbm.at[layer, e], wdn_ref, sem.at[1]))

    @pl.when(i < n_used)
    def _():
        @pl.when(jnp.logical_or(i == 0, be_ref[i] != be_ref[jnp.maximum(i - 1, 0)]))
        def _():
            @pl.when(i == 0)
            def _():
                for cp in fetch(be_ref[0]):
                    cp.start()
            for cp in fetch(be_ref[i]):
                cp.wait()
            half = LANES
            for c in range(2 * EXPERT_FF // (2 * half)):
                r = _dot(wgu_ref[:, c * 2 * half:(c + 1) * 2 * half].astype(BF16), sel_ref[...])
                wgu_s[:, c * half:(c + 1) * half] = r[:, 0:half].astype(BF16)
                wgu_s[:, EXPERT_FF + c * half:EXPERT_FF + (c + 1) * half] = r[:, half:2 * half].astype(BF16)
            wdn_s[...] = wdn_ref[...].astype(BF16)
            nxt = nxt_ref[i]

            @pl.when(nxt >= 0)
            def _():
                for cp in fetch(nxt):
                    cp.start(priority=1)

        def expert_rows(rows):
            gu = _dot(x_ref[0:rows, 0:D_MODEL].astype(BF16), wgu_s[...])
            gate = jnp.minimum(gu[:, 0:EXPERT_FF] + bg_ref[0], SWIGLU_LIMIT)
            up = jnp.clip(gu[:, EXPERT_FF:2 * EXPERT_FF] + bu_ref[0], -SWIGLU_LIMIT, SWIGLU_LIMIT)
            act = gate * jax.nn.sigmoid(SWIGLU_ALPHA * gate) * (up + 1.0)
            y = _dot(act.astype(BF16), wdn_s[...]) + bd_ref[0]
            slot_gate = x_ref[0:rows, D_MODEL:XS_COLS]
            for j in range(D_MODEL // LANES):
                y_ref[0:rows, j * LANES:(j + 1) * LANES] = y[:, j * LANES:(j + 1) * LANES] * slot_gate

        half = MOE_BLK // 2

        @pl.when(nv_ref[i] > half)
        def _():
            expert_rows(MOE_BLK)

        @pl.when(nv_ref[i] <= half)
        def _():
            expert_rows(half)
            y_ref[half:MOE_BLK, :] = jnp.zeros((MOE_BLK - half, D_MODEL), F32)

    @pl.when(i >= n_used)
    def _():
        y_ref[...] = jnp.zeros_like(y_ref)


def _bias_split_kernel(b_ref, sel_ref, o_ref):
    for c in range(2 * EXPERT_FF // (2 * LANES)):
        terms = _split3(b_ref[:, c * 2 * LANES:(c + 1) * 2 * LANES])
        r = functools.reduce(jnp.add, [_dot(t, sel_ref[...]) for t in terms])
        o_ref[:, c * LANES:(c + 1) * LANES] = r[:, 0:LANES]
        o_ref[:, EXPERT_FF + c * LANES:EXPERT_FF + (c + 1) * LANES] = r[:, LANES:2 * LANES]


def _bias_split(b_gu, sel):
    n = DEPTH * N_EXPERTS
    out = pl.pallas_call(
        _bias_split_kernel,
        out_shape=jax.ShapeDtypeStruct((n, 2 * EXPERT_FF), F32),
        name="bias_split",
    )(b_gu.reshape(n, 2 * EXPERT_FF), sel)
    return out.reshape(n, 1, 2 * EXPERT_FF)


def _moe(layer, rt, xs, w_gu, w_down, b_gu_split, b_down, sel):
    first = layer * N_EXPERTS
    bg = pl.BlockSpec((1, 1, EXPERT_FF), lambda i, be, *_: (first + be[i], 0, 0))
    bu = pl.BlockSpec((1, 1, EXPERT_FF), lambda i, be, *_: (first + be[i], 0, 1))
    bd = pl.BlockSpec((1, 1, D_MODEL), lambda i, be, *_: (first + be[i], 0, 0))
    return pl.pallas_call(
        functools.partial(_moe_kernel, layer),
        grid_spec=pltpu.PrefetchScalarGridSpec(
            num_scalar_prefetch=4,
            grid=(N_BLOCKS,),
            in_specs=[
                pl.BlockSpec((MOE_BLK, XS_COLS), lambda i, be, nx, nv, nu: (jnp.minimum(i, nu[0] - 1), 0)),
                pl.BlockSpec(memory_space=pl.ANY),
                pl.BlockSpec(memory_space=pl.ANY),
                bg, bu, bd,
                pl.BlockSpec((2 * LANES, 2 * LANES), lambda i, *_: (0, 0)),
            ],
            out_specs=pl.BlockSpec((MOE_BLK, D_MODEL), lambda i, *_: (i, 0)),
            scratch_shapes=[
                pltpu.VMEM((D_MODEL, 2 * EXPERT_FF), F32),
                pltpu.VMEM((EXPERT_FF, D_MODEL), F32),
                pltpu.VMEM((D_MODEL, 2 * EXPERT_FF), BF16),
                pltpu.VMEM((EXPERT_FF, D_MODEL), BF16),
                pltpu.SemaphoreType.DMA((2,)),
            ],
        ),
        out_shape=jax.ShapeDtypeStruct((N_SLOTS, D_MODEL), F32),
        compiler_params=pltpu.CompilerParams(dimension_semantics=("arbitrary",), vmem_limit_bytes=VMEM_LIMIT),
        name="moe",
    )(rt["blk_e"], rt["blk_next"], rt["blk_rows"], rt["n_used"], xs, w_gu, w_down, b_gu_split, b_gu_split,
      b_down.reshape(DEPTH * N_EXPERTS, 1, D_MODEL), sel)


def _combine_kernel(cnt_ref, off_ref, run_ref, y_ref, lp_ref, x1_ref, mod_ref, fw_ref,
                    xc_ref, xl_ref, yc_ref, yl_ref, stage, mine_s, sem):
    i = pl.program_id(0)
    slot = i % 2

    def copy_in(s_):
        return lambda s, d, n: pltpu.make_async_copy(y_ref.at[pl.ds(d, n), :], stage.at[s_, pl.ds(s, n), :],
                                                     sem.at[s_])

    @pl.when(i == 0)
    def _():
        stage[...] = jnp.zeros_like(stage)
        _run_copies(cnt_ref, off_ref, run_ref, 0, N_EXPERTS, RUN_BITS, copy_in(0), True)

    @pl.when(i + 1 < N_TILES)
    def _():
        _run_copies(cnt_ref, off_ref, run_ref, (i + 1) * N_EXPERTS, N_EXPERTS, RUN_BITS, copy_in(1 - slot), True)

    last = i * N_EXPERTS + N_EXPERTS - 1
    _wait_rows(off_ref[last] + cnt_ref[last], copy_in(slot))
    lane = lax.broadcasted_iota(I32, (TM, LANES), 1)
    mine = [jnp.broadcast_to(lp_ref[:, k:k + 1], (TM, LANES)) for k in range(TOP_K)]
    for c in range(STAGE_ROWS // LANES):
        hit = jnp.zeros((TM, LANES), F32)
        for k in range(TOP_K):
            hit = jnp.where(lane + c * LANES == mine[k], 1.0, hit)
        mine_s[:, c * LANES:(c + 1) * LANES] = hit.astype(BF16)
    moe = _dot(mine_s[...], stage[slot].astype(BF16))
    gate2 = mod_ref[0][:, 5 * D_MODEL:6 * D_MODEL]
    x2 = x1_ref[...] + gate2 * moe
    y_norm = _rms(x2, fw_ref[...])

    @pl.when(i < CTX_TILES)
    def _():
        xc_ref[...] = x2
        yc_ref[...] = y_norm

    @pl.when(i >= CTX_TILES)
    def _():
        xl_ref[...] = x2
        yl_ref[...] = y_norm


def _combine(rt, y_slots, lp, x1, mod3, final_w):
    row = lambda n: pl.BlockSpec((TM, n), lambda i, *_: (i, 0))
    ctx_row = pl.BlockSpec((TM, D_MODEL), lambda i, *_: (jnp.minimum(i, CTX_TILES - 1), 0))
    lat_row = pl.BlockSpec((TM, D_MODEL), lambda i, *_: (jnp.maximum(i - CTX_TILES, 0), 0))
    return pl.pallas_call(
        _combine_kernel,
        grid_spec=pltpu.PrefetchScalarGridSpec(
            num_scalar_prefetch=3,
            grid=(N_TILES,),
            in_specs=[
                pl.BlockSpec(memory_space=pl.ANY),
                row(LANES), row(D_MODEL),
                pl.BlockSpec((1, 1, 6 * D_MODEL), lambda i, *_: (_mod_row(i), 0, 0)),
                pl.BlockSpec((1, D_MODEL), lambda i, *_: (0, 0)),
            ],
            out_specs=(ctx_row, lat_row, ctx_row, lat_row),
            scratch_shapes=[pltpu.VMEM((2, STAGE_ROWS, D_MODEL), F32), pltpu.VMEM((TM, STAGE_ROWS), BF16),
                            pltpu.SemaphoreType.DMA((2,))],
        ),
        out_shape=(jax.ShapeDtypeStruct((N_CTX, D_MODEL), F32), jax.ShapeDtypeStruct((N_LAT, D_MODEL), F32),
                   jax.ShapeDtypeStruct((N_CTX, D_MODEL), F32), jax.ShapeDtypeStruct((N_LAT, D_MODEL), F32)),
        compiler_params=pltpu.CompilerParams(dimension_semantics=("arbitrary",), vmem_limit_bytes=VMEM_LIMIT),
        name="combine",
    )(rt["cnt"], rt["off"], rt["run"], y_slots, lp, x1, mod3, final_w)


def _rope_tables():
    rows = DEC_SEQ // GRID_W
    t = jnp.arange(rows * GRID_W)
    row = (t // GRID_W).astype(F32)
    col = (t % GRID_W).astype(F32)
    half = QK_ROPE // 2
    inv = ROPE_BASE ** (-jnp.arange(0, half, 2, dtype=F32) / half)
    ang_r, ang_c = row[:, None] * inv, col[:, None] * inv
    cos32 = jnp.concatenate([jnp.cos(ang_r), jnp.cos(ang_r), jnp.cos(ang_c), jnp.cos(ang_c)], axis=-1)
    sin32 = jnp.concatenate([-jnp.sin(ang_r), jnp.sin(ang_r), -jnp.sin(ang_c), jnp.sin(ang_c)], axis=-1)
    cos32 = jnp.concatenate([jnp.ones((TM, QK_ROPE), F32), cos32], axis=0)
    sin32 = jnp.concatenate([jnp.zeros((TM, QK_ROPE), F32), sin32], axis=0)
    n = cos32.shape[0]
    pad = HEAD_PAD - QK_NOPE - QK_ROPE
    cos_h = jnp.concatenate([jnp.ones((n, QK_NOPE), F32), cos32, jnp.zeros((n, pad), F32)], axis=-1)
    sin_h = jnp.concatenate([jnp.zeros((n, QK_NOPE), F32), sin32, jnp.zeros((n, pad), F32)], axis=-1)
    zeros = jnp.zeros((n, LANES - QK_ROPE), F32)
    j = jnp.arange(QK_ROPE)
    tile = jnp.zeros((LANES, MLA_HEADS, HEAD_PAD), F32).at[j, :, QK_NOPE + j].set(1.0)
    return {
        "cosq": cos_h, "sinq": sin_h,
        "cosk": jnp.concatenate([cos32, zeros], axis=-1), "sink": jnp.concatenate([sin32, zeros], axis=-1),
        "tile": tile.reshape(LANES, MLA_HEADS * HEAD_PAD).astype(BF16),
        "vone": jnp.zeros((MLA_HEADS, HEAD_PAD), F32).at[:, V_HEAD].set(1.0).reshape(1, MLA_HEADS * HEAD_PAD),
    }


def _layer_weights(p, l):
    hd = QK_NOPE + QK_ROPE
    pad = HEAD_PAD - hd
    wq = p["w_uq"][l].reshape(Q_RANK, MLA_HEADS, hd)
    wq_a = jnp.pad(wq, ((0, 0), (0, 0), (0, pad)))
    wuk = jnp.pad(p["w_uk"][l], ((0, 0), (0, 0), (0, HEAD_PAD - QK_NOPE)))
    rw = jnp.pad(p["router_w"][l], ((0, 0), (0, LANES - N_EXPERTS)))
    rb = jnp.concatenate([p["router_b"][l], jnp.full((LANES - N_EXPERTS,), -jnp.inf, F32)])
    return {
        "norm1_w": p["norm1_w"][l][None], "q_norm_w": p["q_norm_w"][l][None],
        "wq_a": wq_a.reshape(Q_RANK, -1).astype(BF16),
        "kv_norm_w": p["kv_norm_w"][l][None],
        "wuk": wuk.reshape(KV_RANK, -1).astype(BF16),
        "wuv": jnp.pad(p["w_uv"][l], ((0, 0), (0, 0), (0, HEAD_PAD - V_HEAD))).reshape(KV_RANK, -1).astype(BF16),
        "wuv_c": p["w_uv"][l].reshape(KV_RANK, -1).astype(BF16),
        "conv_a_w": p["conv_a_w"][l], "ssm_conv_w": p["ssm_conv_w"][l], "ssm_conv_b": p["ssm_conv_b"][l][None],
        "w_out_a": p["w_out_a"][l].astype(BF16), "w_o_mla": p["w_o_mla"][l].astype(BF16),
        "d_skip": jnp.repeat(p["d_skip"][l], SSM_HEAD_DIM)[None], "ssm_norm_w": p["ssm_norm_w"][l][None],
        "w_o_ssm": p["w_o_ssm"][l].astype(BF16), "w_o": p["w_o"][l].astype(BF16),
        "norm2_w": p["norm2_w"][l][None], "router_w": rw.astype(BF16), "router_b": rb[None],
    }


def _ssd_params(p, l):
    z = lambda n: jnp.zeros((n,), F32)
    lanes = lambda f, b: jnp.concatenate([z(SM_DTF), f, b, z(LANES - SM_DTB - SSM_HEADS)])
    ones = jnp.ones((SSM_HEADS,), F32)
    rows = [lanes(p["dt_bias_fwd"][l], p["dt_bias_bwd"][l]), lanes(p["a_log_fwd"][l], p["a_log_bwd"][l]),
            lanes(ones, ones)]
    return jnp.concatenate([jnp.stack(rows), jnp.zeros((SUBLANES - 3, LANES), F32)], axis=0)


def _routing(cnt_tiles):
    cnt = cnt_tiles[:, 0, 0:N_EXPERTS]
    cnt = (cnt + RUN_ALIGN - 1) // RUN_ALIGN * RUN_ALIGN
    per_expert = jnp.sum(cnt, axis=0)
    padded = (per_expert + MOE_BLK - 1) // MOE_BLK * MOE_BLK
    pad_end = jnp.cumsum(padded)
    pad_start = pad_end - padded
    run = pad_start[None, :] + jnp.cumsum(cnt, axis=0) - cnt
    off = jnp.cumsum(cnt, axis=1) - cnt
    starts = jnp.arange(N_BLOCKS, dtype=I32) * MOE_BLK
    blk_e = jnp.minimum(jnp.sum((pad_end[None, :] <= starts[:, None]).astype(I32), axis=1), N_EXPERTS - 1)
    off_v = jnp.zeros((N_TILES, SUBLANES, LANES), F32).at[:, 0, 0:N_EXPERTS].set(off.astype(F32))
    ids = jnp.arange(N_EXPERTS, dtype=I32)
    later = jnp.logical_and(ids[None, :] > ids[:, None], padded[None, :] > 0)
    nxt = jnp.min(jnp.where(later, ids[None, :], N_EXPERTS), axis=1)
    nxt = jnp.where(nxt == N_EXPERTS, -1, nxt)
    per_block = lambda v: jnp.sum(jnp.where(blk_e[:, None] == ids[None, :], v[None, :], 0), axis=1)
    blk_rows = jnp.clip(per_block(pad_start + per_expert) - starts, 0, MOE_BLK)
    return {
        "blk_next": per_block(nxt).astype(I32), "blk_rows": blk_rows.astype(I32),
        "cnt": cnt.reshape(-1).astype(I32), "off": off.reshape(-1).astype(I32), "run": run.reshape(-1).astype(I32),
        "tail_cnt": (padded - per_expert).astype(I32), "tail_dst": (pad_start + per_expert).astype(I32),
        "blk_e": blk_e.astype(I32), "n_used": (pad_end[-1] // MOE_BLK).astype(I32).reshape(1), "off_v": off_v,
    }


def _deinterleave_matrix():
    k = jnp.arange(2 * LANES)[:, None]
    n = jnp.arange(2 * LANES)[None, :]
    src = jnp.where(n < LANES, 2 * n, 2 * (n - LANES) + 1)
    return (k == src).astype(BF16)


def kernel(x_prompt, x_sample, cache_ckv, cache_kpe, state_ssm_fwd, state_ssm_bwd, c, c_ctx, w_ada, b_ada, norm1_w, w_in, conv_a_w, w_out_a, q_norm_w, w_uq, kv_norm_w, w_uk, w_uv, w_o_mla, ssm_conv_w, ssm_conv_b, dt_bias_fwd, dt_bias_bwd, a_log_fwd, a_log_bwd, d_skip, ssm_norm_w, w_o_ssm, w_o, norm2_w, router_w, router_b, w_gu, b_gu, w_down, b_down, final_norm_w):
    p = dict(norm1_w=norm1_w, w_in=w_in, conv_a_w=conv_a_w, w_out_a=w_out_a, q_norm_w=q_norm_w, w_uq=w_uq,
             kv_norm_w=kv_norm_w, w_uk=w_uk, w_uv=w_uv, w_o_mla=w_o_mla, ssm_conv_w=ssm_conv_w,
             ssm_conv_b=ssm_conv_b, dt_bias_fwd=dt_bias_fwd, dt_bias_bwd=dt_bias_bwd, a_log_fwd=a_log_fwd,
             a_log_bwd=a_log_bwd, d_skip=d_skip, ssm_norm_w=ssm_norm_w, w_o_ssm=w_o_ssm, w_o=w_o,
             norm2_w=norm2_w, router_w=router_w, router_b=router_b, b_gu=b_gu, b_down=b_down)
    xc = x_prompt.reshape(N_CTX, D_MODEL)
    xl = x_sample.reshape(N_LAT, D_MODEL)
    cond8 = jnp.concatenate([c_ctx[None], c, jnp.zeros((SUBLANES - 1 - DEC_BATCH, D_MODEL), F32)], axis=0)
    mods = _ada_mods(cond8, w_ada, b_ada)
    tabs = _rope_tables()
    sel = _deinterleave_matrix()
    b_gu_split = _bias_split(b_gu, sel)
    w_in_t = jnp.swapaxes(w_in, 1, 2)
    final_w = final_norm_w[None]
    hp = SSM_INNER

    ckv_out, kpe_out, sf_out, sb_out = [], [], [], []
    y_ctx = y_lat = None
    for l in range(DEPTH):
        lw = _layer_weights(p, l)
        mod3 = mods[l].reshape(SUBLANES, 1, 6 * D_MODEL)
        ta, q, ckv, kpe, kf, v_ctx, v_lat, small, z, xact, g = _inproj(l, xc, xl, mod3, w_in_t, lw, tabs)
        kpe_c = jnp.pad(cache_kpe[:, l].reshape(DEC_BATCH * PAST_LEN, QK_ROPE), ((0, 0), (0, LANES - QK_ROPE)))
        kf_c, v_c = _kvcache(cache_ckv[:, l].reshape(DEC_BATCH * PAST_LEN, KV_RANK), kpe_c, lw, tabs)
        att_ctx, att_lat = _attention(q, kf, v_ctx, v_lat, kf_c, v_c)
        init_f = state_ssm_fwd[:, l].reshape(DEC_BATCH, hp, SSM_STATE)
        init_b = state_ssm_bwd[:, l].reshape(DEC_BATCH, hp, SSM_STATE)
        yf, yb, sf, sb = _ssd(xact, small, init_f, init_b, _ssd_params(p, l))
        x1, h2, top_i, top_g, cnt_tiles = _merge(ta, att_ctx, att_lat, yf, yb, xact, z, g, xc, xl, mod3, lw)
        rt = _routing(cnt_tiles)
        xs, lp = _dispatch(rt, h2, top_i, top_g)
        y_slots = _moe(l, rt, xs, w_gu, w_down, b_gu_split, b_down, sel)
        xc, xl, y_ctx, y_lat = _combine(rt, y_slots, lp, x1, mod3, final_w)
        ckv_out.append(ckv.reshape(BATCH, SEQ, KV_RANK))
        kpe_out.append(kpe.reshape(BATCH, SEQ, QK_ROPE))
        sf_out.append(sf.reshape(BATCH, SSM_HEADS, SSM_HEAD_DIM, SSM_STATE))
        sb_out.append(sb.reshape(BATCH, SSM_HEADS, SSM_HEAD_DIM, SSM_STATE))

    y_prompt = y_ctx.reshape(BATCH, SEQ, D_MODEL)
    y_sample = y_lat.reshape(DEC_BATCH, DEC_SEQ, D_MODEL)
    return (y_prompt, y_sample, jnp.stack(ckv_out, axis=1), jnp.stack(kpe_out, axis=1),
            jnp.stack(sf_out, axis=1), jnp.stack(sb_out, axis=1))
```

```python
import functools
import math

import jax
import jax.numpy as jnp
from jax import lax
from jax.experimental import pallas as pl
from jax.experimental.pallas import tpu as pltpu

F32 = jnp.float32
BF16 = jnp.bfloat16
I32 = jnp.int32

D_MODEL = 1024
BATCH = 16
SEQ = 256
DEPTH = 2
DEC_BATCH = 2
DEC_SEQ = 2048
PAST_LEN = 512
GRID_W = 64
NORM_EPS = 1e-6
CONV_DIM = 512
MLA_HEADS = 8
Q_RANK = 384
KV_RANK = 256
QK_NOPE = 64
QK_ROPE = 32
V_HEAD = 64
ROPE_BASE = 10000.0
MLA_SCALE = (QK_NOPE + QK_ROPE) ** -0.5
SSM_HEADS = 16
SSM_HEAD_DIM = 64
SSM_INNER = SSM_HEADS * SSM_HEAD_DIM
SSM_GROUPS = 2
SSM_STATE = 128
SSM_CONV_CH = SSM_INNER + 2 * SSM_GROUPS * SSM_STATE
N_EXPERTS = 32
TOP_K = 4
EXPERT_FF = D_MODEL
SWIGLU_ALPHA = 1.702
SWIGLU_LIMIT = 7.0

N_CTX = BATCH * SEQ
N_LAT = DEC_BATCH * DEC_SEQ
N_TOK = N_CTX + N_LAT
N_SEQS = BATCH + DEC_BATCH

LANES = 128
SUBLANES = 8
HEAD_PAD = 128
TM = 256
N_TILES = N_TOK // TM
CTX_TILES = N_CTX // TM
LAT_TILES_PER_SEQ = DEC_SEQ // TM
CHUNK = 128
CTX_CHUNKS_PER_SEQ = SEQ // CHUNK
LAT_CHUNKS_PER_SEQ = DEC_SEQ // CHUNK
N_CTX_CHUNKS = N_CTX // CHUNK
N_CHUNKS = N_TOK // CHUNK
MOE_BLK = 256
N_ASSIGN = N_TOK * TOP_K
RUN_ALIGN = SUBLANES
RUN_BITS = tuple(range(8, 2, -1))
TAIL_BITS = tuple(range(7, 2, -1))
COMMON_BITS = 6
WAIT_BITS = tuple(range(10, 2, -1))
STAGE_ROWS = 1280
XS_COLS = D_MODEL + LANES
N_BLOCKS = -(-(N_ASSIGN + N_TILES * N_EXPERTS * (RUN_ALIGN - 1) + N_EXPERTS * (MOE_BLK - 1)) // MOE_BLK)
N_SLOTS = N_BLOCKS * MOE_BLK
VMEM_LIMIT = 56 * 1024 * 1024

C_A3 = 0
C_CQ = C_A3 + 3 * CONV_DIM
C_CKV = C_CQ + Q_RANK
C_Z = C_CKV + KV_RANK
C_XBC = C_Z + SSM_INNER
C_GATE = C_XBC + SSM_CONV_CH
C_SMALL = C_GATE + 3 * D_MODEL
IN_COLS2 = C_SMALL + LANES
SM_DTF = QK_ROPE
SM_DTB = QK_ROPE + SSM_HEADS
S_CQ = 3 * CONV_DIM
S_CKV = S_CQ + Q_RANK
S_KPE = S_CKV + KV_RANK
S_Z = S_KPE + QK_ROPE
S_XBC = S_Z + SSM_INNER
S_DTF = S_XBC + SSM_CONV_CH
S_GATE = S_DTF + 2 * SSM_HEADS
IN_COLS = S_GATE + 3 * D_MODEL
W_SEGMENTS = ((C_A3, 0, 3 * CONV_DIM), (C_CQ, S_CQ, Q_RANK), (C_CKV, S_CKV, KV_RANK), (C_Z, S_Z, SSM_INNER),
              (C_XBC, S_XBC, SSM_CONV_CH), (C_GATE, S_GATE, 3 * D_MODEL))
W_PIECE = 512


def _rms(x, w):
    return x * lax.rsqrt(jnp.mean(x * x, axis=-1, keepdims=True) + NORM_EPS) * w


def _silu(x):
    return x * jax.nn.sigmoid(x)


def _dot(a, b):
    return jnp.dot(a, b, preferred_element_type=F32)


def _dot_nt(a, b):
    return lax.dot_general(a, b, (((1,), (1,)), ((), ())), preferred_element_type=F32)


def _resident(shape):
    nd = len(shape)
    return pl.BlockSpec(shape, lambda *_: (0,) * nd, pipeline_mode=pl.Buffered(1))


def _mod_row(i):
    return jnp.where(i < CTX_TILES, 0, 1 + (i - CTX_TILES) // LAT_TILES_PER_SEQ)


def _pos_block(i):
    return jnp.where(i < CTX_TILES, 0, 1 + (i - CTX_TILES) % LAT_TILES_PER_SEQ)


def _ada_kernel(c_ref, w_ref, b_ref, o_ref):
    s = _silu(c_ref[...]).astype(BF16)
    o_ref[0] = _dot(s, w_ref[0].astype(BF16)) + b_ref[0]


def _ada_mods(cond8, w_ada, b_ada):
    tn = 1536
    n_mod = 6 * D_MODEL
    return pl.pallas_call(
        _ada_kernel,
        grid=(DEPTH, n_mod // tn),
        in_specs=[
            pl.BlockSpec((SUBLANES, D_MODEL), lambda l, j: (0, 0)),
            pl.BlockSpec((1, D_MODEL, tn), lambda l, j: (l, 0, j)),
            pl.BlockSpec((1, 1, tn), lambda l, j: (l, 0, j)),
        ],
        out_specs=pl.BlockSpec((1, SUBLANES, tn), lambda l, j: (l, 0, j)),
        out_shape=jax.ShapeDtypeStruct((DEPTH, SUBLANES, n_mod), F32),
        compiler_params=pltpu.CompilerParams(dimension_semantics=("arbitrary", "arbitrary")),
        name="ada_mods",
    )(cond8, w_ada, b_ada.reshape(DEPTH, 1, n_mod))


def _relayout_w_in(layer, wt_hbm, w2, stg, small_stg, sem, small_sem):
    pieces = [(dst + p, src + p, min(W_PIECE, width - p))
              for dst, src, width in W_SEGMENTS for p in range(0, width, W_PIECE)]
    fetch = lambda k: pltpu.make_async_copy(wt_hbm.at[layer, pl.ds(pieces[k][1], pieces[k][2]), :],
                                            stg.at[k % 2, pl.ds(0, pieces[k][2]), :], sem.at[k % 2])
    small_stg[...] = jnp.zeros_like(small_stg)
    small_copies = [
        pltpu.make_async_copy(wt_hbm.at[layer, pl.ds(S_KPE, QK_ROPE), :], small_stg.at[pl.ds(0, QK_ROPE), :],
                              small_sem.at[0]),
        pltpu.make_async_copy(wt_hbm.at[layer, pl.ds(S_DTF, 2 * SSM_HEADS), :],
                              small_stg.at[pl.ds(SM_DTF, 2 * SSM_HEADS), :], small_sem.at[1]),
    ]
    for cp in small_copies:
        cp.start()
    fetch(0).start()
    for k, (dst, _, width) in enumerate(pieces):
        if k + 1 < len(pieces):
            fetch(k + 1).start()
        fetch(k).wait()
        w2[:, dst:dst + width] = stg[k % 2, 0:width, :].T.astype(BF16)
    for cp in small_copies:
        cp.wait()
    w2[:, C_SMALL:IN_COLS2] = small_stg[...].T.astype(BF16)


def _place_rope_key(k_nope, kpe):
    shifted = pltpu.roll(kpe, QK_NOPE, 1)
    return jnp.concatenate([k_nope[:, h * HEAD_PAD:(h + 1) * HEAD_PAD] + shifted for h in range(MLA_HEADS)],
                           axis=1).astype(BF16)


def _rot_partner(x):
    n = x.shape[1]
    lane = lax.broadcasted_iota(I32, x.shape, 1)
    quarter = QK_ROPE // 4
    return jnp.where(lane % (2 * quarter) < quarter, pltpu.roll(x, n - quarter, 1), pltpu.roll(x, quarter, 1))


def _first_layer_slabs(ref, value):
    ref[0, 0] = value
    ref[0, 1:DEPTH] = jnp.zeros((DEPTH - 1,) + value.shape, value.dtype)


def _inproj_kernel(layer, *refs):
    n_in = 23 + (2 if layer > 0 else 0)
    (xc_ref, xcp_ref, xcn_ref, xl_ref, xlp_ref, xln_ref, mod_ref, n1w_ref, w_hbm, qnw_ref, wq_ref, kvnw_ref, wuk_ref,
     wuv_ref, wuvc_ref, vone_ref, cosq_ref, sinq_ref, cosk_ref, sink_ref, caw_ref, cw_ref, cb_ref) = refs[:23]
    (ta_ref, q_ref, ckv_ref, kpe_ref, kf_ref, vc_ref, vl_ref, small_ref, z_ref, xact_ref, g_ref,
     w2, stg, small_stg, sem, small_sem) = refs[n_in:]
    i = pl.program_id(0)

    @pl.when(i == 0)
    def _():
        _relayout_w_in(layer, w_hbm, w2, stg, small_stg, sem, small_sem)

    mod = mod_ref[0]
    shift1 = mod[:, 0:D_MODEL]
    scale1 = mod[:, D_MODEL:2 * D_MODEL]
    x_ext = jnp.where(i < CTX_TILES, jnp.concatenate([xcp_ref[...], xc_ref[...], xcn_ref[...]], axis=0),
                      jnp.concatenate([xlp_ref[...], xl_ref[...], xln_ref[...]], axis=0))
    h_ext = _rms(x_ext, n1w_ref[...]) * (1.0 + scale1) + shift1
    hb_ext = h_ext.astype(BF16)
    hb = h_ext[SUBLANES:SUBLANES + TM].astype(BF16)

    j = (i - CTX_TILES) % LAT_TILES_PER_SEQ
    is_ctx = i < CTX_TILES
    keep_prev = jnp.where(jnp.logical_or(is_ctx, j == 0), 0.0, 1.0)
    keep_next = jnp.where(jnp.logical_or(is_ctx, j == LAT_TILES_PER_SEQ - 1), 0.0, 1.0)
    row = lax.broadcasted_iota(I32, (TM, 1), 0)
    prev_mask = jnp.where(row == 0, keep_prev, 1.0)
    next_mask = jnp.where(row == TM - 1, keep_next, 1.0)
    ext = TM + 2 * SUBLANES

    def conv3(x, w_ref):
        x_prev = pltpu.roll(x, 1, 0)[SUBLANES:SUBLANES + TM] * prev_mask
        x_next = pltpu.roll(x, ext - 1, 0)[SUBLANES:SUBLANES + TM] * next_mask
        return x_prev * w_ref[0:1, :] + x[SUBLANES:SUBLANES + TM] * w_ref[1:2, :] + x_next * w_ref[2:3, :]

    a3 = _dot(hb_ext, w2[:, C_A3:C_CQ])
    s = a3[:, 2 * CONV_DIM:3 * CONV_DIM] * a3[:, 0:CONV_DIM]
    ta_ref[...] = (a3[SUBLANES:SUBLANES + TM, CONV_DIM:2 * CONV_DIM] * conv3(s, caw_ref)).astype(BF16)
    xact_ref[...] = _silu(conv3(_dot(hb_ext, w2[:, C_XBC:C_GATE]), cw_ref) + cb_ref[...])

    def seg(a, b):
        return _dot(hb, w2[:, a:b])

    cqn = _rms(seg(C_CQ, C_CKV), qnw_ref[...]).astype(BF16)
    qa = _dot(cqn, wq_ref[...])
    qa_rot = _rot_partner(qa)
    for h in range(MLA_HEADS):
        hs = slice(h * HEAD_PAD, (h + 1) * HEAD_PAD)
        q_ref[:, hs] = (qa[:, hs] * cosq_ref[...] + qa_rot[:, hs] * sinq_ref[...]).astype(BF16)

    ckv = _rms(seg(C_CKV, C_Z), kvnw_ref[...])
    ckvb = ckv.astype(BF16)
    small = seg(C_SMALL, IN_COLS2)
    small_ref[...] = small
    kpe = small * cosk_ref[...] + _rot_partner(small) * sink_ref[...]
    kf_ref[...] = _place_rope_key(_dot(ckvb, wuk_ref[...]), kpe)
    v_lat = (_dot(ckvb, wuv_ref[...]) + vone_ref[...]).astype(BF16)
    v_ctx = _dot(ckvb, wuvc_ref[...]).astype(BF16)

    z_ref[...] = seg(C_Z, C_XBC)
    g_ref[...] = jax.nn.sigmoid(seg(C_GATE, C_SMALL))

    @pl.when(i < CTX_TILES)
    def _():
        if layer == 0:
            _first_layer_slabs(ckv_ref, ckv)
            _first_layer_slabs(kpe_ref, small[:, 0:QK_ROPE])
        else:
            ckv_ref[0] = ckv
            kpe_ref[0] = small[:, 0:QK_ROPE]
        vc_ref[...] = v_ctx

    @pl.when(i >= CTX_TILES)
    def _():
        vl_ref[...] = v_lat


def _inproj(layer, xc, xl, mod3, w_in, lw, tabs, caches):
    row = lambda n: pl.BlockSpec((TM, n), lambda i: (i, 0))
    tab = lambda n: pl.BlockSpec((TM, n), lambda i: (_pos_block(i), 0))
    per = TM // SUBLANES
    qw = MLA_HEADS * HEAD_PAD
    ctx_row = lambda n: pl.BlockSpec((TM, n), lambda i: (jnp.minimum(i, CTX_TILES - 1), 0))
    lat_row = lambda n: pl.BlockSpec((TM, n), lambda i: (jnp.maximum(i - CTX_TILES, 0), 0))

    def halo(first_tile, n_rows, side):
        last = n_rows // SUBLANES - 1
        return pl.BlockSpec((SUBLANES, D_MODEL),
                            lambda i: (jnp.clip((i - first_tile + side) * per - 1 + side, 0, last), 0))

    out_shape = (
        jax.ShapeDtypeStruct((N_TOK, CONV_DIM), BF16),
        jax.ShapeDtypeStruct((N_TOK, qw), BF16),
        jax.ShapeDtypeStruct((BATCH, DEPTH, SEQ, KV_RANK), F32),
        jax.ShapeDtypeStruct((BATCH, DEPTH, SEQ, QK_ROPE), F32),
        jax.ShapeDtypeStruct((N_TOK, qw), BF16),
        jax.ShapeDtypeStruct((N_CTX, MLA_HEADS * V_HEAD), BF16),
        jax.ShapeDtypeStruct((N_LAT, qw), BF16),
        jax.ShapeDtypeStruct((N_TOK, LANES), F32),
        jax.ShapeDtypeStruct((N_TOK, SSM_INNER), F32),
        jax.ShapeDtypeStruct((N_TOK, SSM_CONV_CH), F32),
        jax.ShapeDtypeStruct((N_TOK, 3 * D_MODEL), F32),
    )
    out_specs = [row(s.shape[-1]) for s in out_shape]
    cache_block = (1, DEPTH, SEQ) if layer == 0 else (1, None, SEQ)
    cache_index = lambda i: (jnp.minimum(i, CTX_TILES - 1), 0 if layer == 0 else layer, 0, 0)
    out_specs[2] = pl.BlockSpec(cache_block + (KV_RANK,), cache_index)
    out_specs[3] = pl.BlockSpec(cache_block + (QK_ROPE,), cache_index)
    extra = {} if layer == 0 else dict(input_output_aliases={23: 2, 24: 3})
    out_specs[5] = ctx_row(MLA_HEADS * V_HEAD)
    out_specs[6] = lat_row(qw)
    return pl.pallas_call(
        functools.partial(_inproj_kernel, layer),
        grid=(N_TILES,),
        in_specs=[
            ctx_row(D_MODEL), halo(0, N_CTX, 0), halo(0, N_CTX, 1),
            lat_row(D_MODEL), halo(CTX_TILES, N_LAT, 0), halo(CTX_TILES, N_LAT, 1),
            pl.BlockSpec((1, 1, 6 * D_MODEL), lambda i: (_mod_row(i), 0, 0)),
            _resident((1, D_MODEL)),
            pl.BlockSpec(memory_space=pl.ANY),
            _resident((1, Q_RANK)),
            _resident((Q_RANK, qw)),
            _resident((1, KV_RANK)),
            _resident((KV_RANK, qw)),
            _resident((KV_RANK, qw)),
            _resident((KV_RANK, MLA_HEADS * V_HEAD)),
            _resident((1, qw)),
            tab(HEAD_PAD), tab(HEAD_PAD), tab(LANES), tab(LANES),
            _resident((3, CONV_DIM)), _resident((3, SSM_CONV_CH)), _resident((1, SSM_CONV_CH)),
        ] + [pl.BlockSpec(memory_space=pl.ANY)] * len(caches),
        out_specs=tuple(out_specs),
        out_shape=out_shape,
        scratch_shapes=[pltpu.VMEM((D_MODEL, IN_COLS2), BF16), pltpu.VMEM((2, W_PIECE, D_MODEL), F32),
                        pltpu.VMEM((LANES, D_MODEL), F32), pltpu.SemaphoreType.DMA((2,)),
                        pltpu.SemaphoreType.DMA((2,))],
        compiler_params=pltpu.CompilerParams(dimension_semantics=("arbitrary",), vmem_limit_bytes=VMEM_LIMIT),
        name="inproj",
        **extra,
    )(xc, xc, xc, xl, xl, xl, mod3, lw["norm1_w"], w_in, lw["q_norm_w"], lw["wq_a"], lw["kv_norm_w"],
      lw["wuk"], lw["wuv"], lw["wuv_c"], tabs["vone"], tabs["cosq"], tabs["sinq"], tabs["cosk"],
      tabs["sink"], lw["conv_a_w"], lw["ssm_conv_w"], lw["ssm_conv_b"], *caches)


def _kvcache_kernel(ckv_ref, kpe_ref, wuk_ref, wuv_ref, vone_ref, kf_ref, v_ref):
    ckvb = ckv_ref[...].astype(BF16)
    kf_ref[...] = _place_rope_key(_dot(ckvb, wuk_ref[...]), kpe_ref[...])
    v_ref[...] = (_dot(ckvb, wuv_ref[...]) + vone_ref[...]).astype(BF16)


def _kvcache(ckv, kpe128, lw, tabs):
    n = ckv.shape[0]
    qw = MLA_HEADS * HEAD_PAD
    return pl.pallas_call(
        _kvcache_kernel,
        grid=(n // PAST_LEN,),
        in_specs=[
            pl.BlockSpec((PAST_LEN, KV_RANK), lambda i: (i, 0)),
            pl.BlockSpec((PAST_LEN, LANES), lambda i: (i, 0)),
            _resident((KV_RANK, qw)),
            _resident((KV_RANK, qw)),
            _resident((1, qw)),
        ],
        out_specs=(pl.BlockSpec((PAST_LEN, qw), lambda i: (i, 0)), pl.BlockSpec((PAST_LEN, qw), lambda i: (i, 0))),
        out_shape=(jax.ShapeDtypeStruct((n, qw), BF16), jax.ShapeDtypeStruct((n, qw), BF16)),
        compiler_params=pltpu.CompilerParams(dimension_semantics=("arbitrary",)),
        name="kvcache",
    )(ckv, kpe128, lw["wuk"], lw["wuv"], tabs["vone"])


def _attn_heads(q_ref, kv_refs, o_ref, acc_ref, denom_from_matmul):
    log2_scale = MLA_SCALE * math.log2(math.e)
    for h in range(MLA_HEADS):
        hs = slice(h * HEAD_PAD, (h + 1) * HEAD_PAD)
        qh = q_ref[:, hs]
        ss = [_dot_nt(qh, k_ref[:, hs]) for k_ref, _ in kv_refs]
        if denom_from_matmul:
            m = functools.reduce(jnp.maximum, [jnp.max(s, axis=-1, keepdims=True) for s in ss])
            ps = [jnp.exp2((s - m) * log2_scale) for s in ss]
            ol = functools.reduce(jnp.add, [_dot(p.astype(BF16), v_ref[:, hs])
                                            for p, (_, v_ref) in zip(ps, kv_refs)])
            acc_ref[:, h * V_HEAD:(h + 1) * V_HEAD] = ol[:, 0:V_HEAD] / ol[:, V_HEAD:V_HEAD + 1]
        else:
            ss = [s * MLA_SCALE for s in ss]
            m = functools.reduce(jnp.maximum, [jnp.max(s, axis=-1, keepdims=True) for s in ss])
            ps = [jnp.exp(s - m) for s in ss]
            l = functools.reduce(jnp.add, [jnp.sum(p, axis=-1, keepdims=True) for p in ps])
            o = functools.reduce(jnp.add, [_dot(p.astype(BF16), v_ref[:, h * V_HEAD:(h + 1) * V_HEAD])
                                           for p, (_, v_ref) in zip(ps, kv_refs)])
            acc_ref[:, h * V_HEAD:(h + 1) * V_HEAD] = o / l
    o_ref[...] = acc_ref[...].astype(BF16)


def _attn_ctx_kernel(q_ref, k_ref, v_ref, o_ref, acc_ref):
    _attn_heads(q_ref, [(k_ref, v_ref)], o_ref, acc_ref, denom_from_matmul=False)


def _attn_lat_kernel(q_ref, k_ref, v_ref, kc_ref, vc_ref, o_ref, acc_ref):
    _attn_heads(q_ref, [(k_ref, v_ref), (kc_ref, vc_ref)], o_ref, acc_ref, denom_from_matmul=True)


def _attention(q, kf, v_ctx, v_lat, kf_c, v_c):
    qw = MLA_HEADS * HEAD_PAD
    vw = MLA_HEADS * V_HEAD
    att_ctx = pl.pallas_call(
        _attn_ctx_kernel,
        grid=(BATCH,),
        in_specs=[pl.BlockSpec((SEQ, qw), lambda b: (b, 0)), pl.BlockSpec((SEQ, qw), lambda b: (b, 0)),
                  pl.BlockSpec((SEQ, vw), lambda b: (b, 0))],
        out_specs=pl.BlockSpec((SEQ, vw), lambda b: (b, 0)),
        out_shape=jax.ShapeDtypeStruct((N_CTX, vw), BF16),
        scratch_shapes=[pltpu.VMEM((SEQ, vw), F32)],
        compiler_params=pltpu.CompilerParams(dimension_semantics=("arbitrary",)),
        name="attn_ctx",
    )(q, kf, v_ctx)
    lat0 = N_CTX // DEC_SEQ
    att_lat = pl.pallas_call(
        _attn_lat_kernel,
        grid=(DEC_BATCH, LAT_TILES_PER_SEQ),
        in_specs=[
            pl.BlockSpec((TM, qw), lambda b, t: (CTX_TILES + b * LAT_TILES_PER_SEQ + t, 0)),
            pl.BlockSpec((DEC_SEQ, qw), lambda b, t: (lat0 + b, 0)),
            pl.BlockSpec((DEC_SEQ, qw), lambda b, t: (b, 0)),
            pl.BlockSpec((PAST_LEN, qw), lambda b, t: (b, 0)),
            pl.BlockSpec((PAST_LEN, qw), lambda b, t: (b, 0)),
        ],
        out_specs=pl.BlockSpec((TM, vw), lambda b, t: (b * LAT_TILES_PER_SEQ + t, 0)),
        out_shape=jax.ShapeDtypeStruct((N_LAT, vw), BF16),
        scratch_shapes=[pltpu.VMEM((TM, vw), F32)],
        compiler_params=pltpu.CompilerParams(dimension_semantics=("arbitrary", "arbitrary"),
                                             vmem_limit_bytes=VMEM_LIMIT),
        name="attn_lat",
    )(q, kf, v_lat, kf_c, v_c)
    return att_ctx, att_lat


def _seq_of_chunk(s):
    return jnp.where(s < N_CTX_CHUNKS, s // CTX_CHUNKS_PER_SEQ,
                     BATCH + (s - N_CTX_CHUNKS) // LAT_CHUNKS_PER_SEQ)


def _chunk_in_seq(s):
    return jnp.where(s < N_CTX_CHUNKS, s % CTX_CHUNKS_PER_SEQ, (s - N_CTX_CHUNKS) % LAT_CHUNKS_PER_SEQ)


def _chunks_in_seq(s):
    return jnp.where(s < N_CTX_CHUNKS, CTX_CHUNKS_PER_SEQ, LAT_CHUNKS_PER_SEQ)


def _mirror_chunk(s):
    return s + _chunks_in_seq(s) - 1 - 2 * _chunk_in_seq(s)


def _split3(a):
    a1 = a.astype(BF16)
    r1 = a - a1.astype(F32)
    a2 = r1.astype(BF16)
    a3 = (r1 - a2.astype(F32)).astype(BF16)
    return a1, a2, a3


def _ssd_direction(x_ref, sm_ref, par_ref, st_ref, y_ref, lane0, backward):
    ri = lax.broadcasted_iota(I32, (CHUNK, CHUNK), 0)
    ci = lax.broadcasted_iota(I32, (CHUNK, CHUNK), 1)
    tri = (ci >= ri) if backward else (ci <= ri)
    tri_b = jnp.where(tri, 1.0, 0.0).astype(BF16)
    tot_row = 0 if backward else CHUNK - 1

    dt = jax.nn.softplus(sm_ref[...] + par_ref[0:1, :])
    a = dt * (-jnp.exp(par_ref[1:2, :])) * par_ref[2:3, :]
    a1, a2, a3 = _split3(a)
    acs = (_dot(tri_b, a1) + _dot(tri_b, a2) + _dot(tri_b, a3)) * math.log2(math.e)
    acs_t = acs.T
    dt_t = dt.T
    first_head = lax.broadcasted_iota(I32, (1, LANES), 1) < SSM_HEAD_DIM

    def block_diag(pair):
        return jnp.concatenate([jnp.where(first_head, pair, 0.0), jnp.where(first_head, 0.0, pair)],
                               axis=0).astype(BF16)

    for g in range(SSM_GROUPS):
        b0 = SSM_INNER + g * SSM_STATE
        c0 = SSM_INNER + SSM_GROUPS * SSM_STATE + g * SSM_STATE
        bg = x_ref[:, b0:b0 + SSM_STATE]
        cg = x_ref[:, c0:c0 + SSM_STATE]
        cb = _dot_nt(cg.astype(BF16), bg.astype(BF16))
        bg_t = bg.T
        heads = SSM_HEADS // SSM_GROUPS
        for pr in range(heads // 2):
            h0 = g * heads + 2 * pr
            sl = slice(h0 * SSM_HEAD_DIM, (h0 + 2) * SSM_HEAD_DIM)
            x_bd = block_diag(x_ref[:, sl])
            st_old = st_ref[:, sl]
            within, carried, to_state, keep = [], [], [], []
            for h in (h0, h0 + 1):
                lane = lane0 + h
                col = jnp.broadcast_to(acs[:, lane:lane + 1], (CHUNK, CHUNK))
                row = acs_t[lane:lane + 1, :]
                dt_row = dt_t[lane:lane + 1, :]
                decay = jnp.exp2(jnp.where(tri, col - row, -jnp.inf))
                within.append((cb * decay * dt_row).astype(BF16))
                carried.append((cg * jnp.exp2(col)).astype(BF16))
                tot = acs[tot_row:tot_row + 1, lane:lane + 1]
                to_state.append((bg_t * (dt_row * jnp.exp2(tot - row))).astype(BF16))
                keep.append(jnp.exp2(tot))
            lhs = jnp.concatenate(within + carried, axis=1)
            y_ref[:, sl] = _dot(lhs, jnp.concatenate([x_bd, block_diag(st_old)], axis=0))
            st_ref[:, sl] = (st_old * jnp.where(first_head, keep[0], keep[1])
                             + _dot(jnp.concatenate(to_state, axis=1), x_bd))


def _ssd_kernel(layer, *refs):
    n_in = 7 + (2 if layer > 0 else 0)
    xf_ref, xb_ref, smf_ref, smb_ref, if_ref, ib_ref, par_ref = refs[:7]
    yf_ref, yb_ref, sf_ref, sb_ref, stf_ref, stb_ref = refs[n_in:]
    s = pl.program_id(0)
    c = _chunk_in_seq(s)

    @pl.when(jnp.logical_and(c == 0, s < N_CTX_CHUNKS))
    def _():
        stf_ref[...] = jnp.zeros_like(stf_ref)
        stb_ref[...] = jnp.zeros_like(stb_ref)

    @pl.when(jnp.logical_and(c == 0, s >= N_CTX_CHUNKS))
    def _():
        stf_ref[...] = if_ref[0].T
        stb_ref[...] = ib_ref[0].T

    _ssd_direction(xf_ref, smf_ref, par_ref, stf_ref, yf_ref, SM_DTF, False)
    _ssd_direction(xb_ref, smb_ref, par_ref, stb_ref, yb_ref, SM_DTB, True)

    @pl.when(jnp.logical_and(c == _chunks_in_seq(s) - 1, s < N_CTX_CHUNKS))
    def _():
        if layer == 0:
            _first_layer_slabs(sf_ref, stf_ref[...].T)
            _first_layer_slabs(sb_ref, stb_ref[...].T)
        else:
            sf_ref[0] = stf_ref[...].T
            sb_ref[0] = stb_ref[...].T


def _ssd(layer, xact, small, init_f, init_b, par, states):
    hp = SSM_INNER
    fwd = lambda n: pl.BlockSpec((CHUNK, n), lambda s: (s, 0))
    bwd = lambda n: pl.BlockSpec((CHUNK, n), lambda s: (_mirror_chunk(s), 0))
    st = pl.BlockSpec((1, DEPTH, hp, SSM_STATE) if layer == 0 else (1, None, hp, SSM_STATE),
                      lambda s: (jnp.minimum(_seq_of_chunk(s), BATCH - 1), 0 if layer == 0 else layer, 0, 0))
    init = pl.BlockSpec((1, hp, SSM_STATE), lambda s: (jnp.maximum(_seq_of_chunk(s) - BATCH, 0), 0, 0))
    extra = {} if layer == 0 else dict(input_output_aliases={7: 2, 8: 3})
    return pl.pallas_call(
        functools.partial(_ssd_kernel, layer),
        grid=(N_CHUNKS,),
        in_specs=[fwd(SSM_CONV_CH), bwd(SSM_CONV_CH), fwd(LANES), bwd(LANES), init, init,
                  pl.BlockSpec((SUBLANES, LANES), lambda s: (0, 0))] + [pl.BlockSpec(memory_space=pl.ANY)] * len(states),
        out_specs=(fwd(hp), bwd(hp), st, st),
        out_shape=(jax.ShapeDtypeStruct((N_TOK, hp), F32), jax.ShapeDtypeStruct((N_TOK, hp), F32),
                   jax.ShapeDtypeStruct((BATCH, DEPTH, hp, SSM_STATE), F32),
                   jax.ShapeDtypeStruct((BATCH, DEPTH, hp, SSM_STATE), F32)),
        scratch_shapes=[pltpu.VMEM((SSM_STATE, hp), F32), pltpu.VMEM((SSM_STATE, hp), F32)],
        compiler_params=pltpu.CompilerParams(dimension_semantics=("arbitrary",)),
        name="ssd",
        **extra,
    )(xact, xact, small, small, init_f, init_b, par, *states)


def _merge_kernel(ta_ref, attc_ref, attl_ref, yf_ref, yb_ref, xs_ref, z_ref, g_ref, xc_ref, xl_ref, mod_ref,
                  woa_ref, wom_ref, dsk_ref, snw_ref, wos_ref, wo_ref, n2w_ref, rw_ref, rb_ref,
                  x1_ref, h2_ref, ti_ref, tg_ref, cnt_ref):
    mod = mod_ref[0]
    gate1 = mod[:, 2 * D_MODEL:3 * D_MODEL]
    shift2 = mod[:, 3 * D_MODEL:4 * D_MODEL]
    scale2 = mod[:, 4 * D_MODEL:5 * D_MODEL]
    y_a = _dot(ta_ref[...], woa_ref[...])
    att = jnp.where(pl.program_id(0) < CTX_TILES, attc_ref[...].astype(F32), attl_ref[...].astype(F32))
    y_b = _dot(att.astype(BF16), wom_ref[...])
    y_ssm = (yf_ref[...] + yb_ref[...] + dsk_ref[...] * xs_ref[...]) * _silu(z_ref[...])
    y_c = _dot(_rms(y_ssm, snw_ref[...]).astype(BF16), wos_ref[...])
    merged = (g_ref[:, 0:D_MODEL] * y_a + g_ref[:, D_MODEL:2 * D_MODEL] * y_b
              + g_ref[:, 2 * D_MODEL:3 * D_MODEL] * y_c)
    x = jnp.where(pl.program_id(0) < CTX_TILES, xc_ref[...], xl_ref[...])
    x1 = x + gate1 * _dot(merged.astype(BF16), wo_ref[...])
    x1_ref[...] = x1
    h2 = _rms(x1, n2w_ref[...]) * (1.0 + scale2) + shift2
    h2b = h2.astype(BF16)
    h2_ref[...] = h2b

    logits = (_dot(h2b, rw_ref[...]) + rb_ref[...]).T[0:N_EXPERTS, :]
    expert = lax.broadcasted_iota(I32, (N_EXPERTS, TM), 0)
    ids, vals = [], []
    for k in range(TOP_K):
        m = jnp.max(logits, axis=0, keepdims=True)
        idx = jnp.min(jnp.where(logits == m, expert, N_EXPERTS), axis=0, keepdims=True)
        ids.append(idx)
        vals.append(m)
        logits = jnp.where(expert == idx, -jnp.inf, logits)
    es = [jnp.exp(v - vals[0]) for v in vals]
    denom = functools.reduce(jnp.add, es)
    srow = lax.broadcasted_iota(I32, (LANES, TM), 0)
    ti_t = jnp.zeros((LANES, TM), F32)
    tg_t = jnp.zeros((LANES, TM), F32)
    chosen_t = jnp.zeros((LANES, TM), F32)
    for k in range(TOP_K):
        ti_t = jnp.where(srow == k, ids[k].astype(F32), ti_t)
        tg_t = jnp.where(srow == k, es[k] / denom, tg_t)
        chosen_t = jnp.where(srow == ids[k], 1.0, chosen_t)
    ti_ref[...] = ti_t.T.astype(I32)
    tg_ref[...] = tg_t.T
    cnt_ref[0] = _dot_nt(jnp.ones((SUBLANES, TM), BF16), chosen_t.astype(BF16)).astype(I32)


def _merge(ta, att_ctx, att_lat, yf, yb, xact, z, g, xc, xl, mod3, lw):
    row = lambda n: pl.BlockSpec((TM, n), lambda i: (i, 0))
    vw = MLA_HEADS * V_HEAD
    out_shape = (jax.ShapeDtypeStruct((N_TOK, D_MODEL), F32), jax.ShapeDtypeStruct((N_TOK, D_MODEL), BF16),
                 jax.ShapeDtypeStruct((N_TOK, LANES), I32), jax.ShapeDtypeStruct((N_TOK, LANES), F32))
    cnt_shape = jax.ShapeDtypeStruct((N_TILES, SUBLANES, LANES), I32)
    cnt_spec = pl.BlockSpec((1, SUBLANES, LANES), lambda i: (i, 0, 0))
    return pl.pallas_call(
        _merge_kernel,
        grid=(N_TILES,),
        in_specs=[
            row(CONV_DIM),
            pl.BlockSpec((TM, vw), lambda i: (jnp.minimum(i, CTX_TILES - 1), 0)),
            pl.BlockSpec((TM, vw), lambda i: (jnp.maximum(i - CTX_TILES, 0), 0)),
            row(SSM_INNER), row(SSM_INNER), row(SSM_INNER),
            row(SSM_INNER), row(3 * D_MODEL),
            pl.BlockSpec((TM, D_MODEL), lambda i: (jnp.minimum(i, CTX_TILES - 1), 0)),
            pl.BlockSpec((TM, D_MODEL), lambda i: (jnp.maximum(i - CTX_TILES, 0), 0)),
            pl.BlockSpec((1, 1, 6 * D_MODEL), lambda i: (_mod_row(i), 0, 0)),
            _resident((CONV_DIM, D_MODEL)), _resident((MLA_HEADS * V_HEAD, D_MODEL)),
            _resident((1, SSM_INNER)), _resident((1, SSM_INNER)), _resident((SSM_INNER, D_MODEL)),
            _resident((D_MODEL, D_MODEL)), _resident((1, D_MODEL)),
            _resident((D_MODEL, LANES)), _resident((1, LANES)),
        ],
        out_specs=tuple(row(s.shape[1]) for s in out_shape) + (cnt_spec,),
        out_shape=out_shape + (cnt_shape,),
        compiler_params=pltpu.CompilerParams(dimension_semantics=("arbitrary",), vmem_limit_bytes=VMEM_LIMIT),
        name="merge",
    )(ta, att_ctx, att_lat, yf, yb, xact, z, g, xc, xl, mod3, lw["w_out_a"], lw["w_o_mla"], lw["d_skip"],
      lw["ssm_norm_w"],
      lw["w_o_ssm"], lw["w_o"], lw["norm2_w"], lw["router_w"], lw["router_b"])


def _run_copies(cnt_ref, src_ref, dst_ref, first, count, bits, make_copy, start):
    def body(e, carry):
        n = cnt_ref[first + e]
        s0 = src_ref[first + e] if src_ref is not None else 0
        d0 = dst_ref[first + e]

        def pieces(some_bits):
            for b in some_bits:
                above = (n >> (b + 1)) << (b + 1)

                @pl.when(((n >> b) & 1) == 1)
                def _():
                    cp = make_copy(pl.multiple_of(s0 + above, RUN_ALIGN), pl.multiple_of(d0 + above, RUN_ALIGN),
                                   1 << b)
                    if start:
                        cp.start()
                    else:
                        cp.wait()

        large = [b for b in bits if b >= COMMON_BITS]

        @pl.when(n >= (1 << COMMON_BITS))
        def _():
            pieces(large)
        pieces([b for b in bits if b < COMMON_BITS])
        return carry
    lax.fori_loop(0, count, body, 0)


def _wait_rows(total, make_copy):
    for b in WAIT_BITS:
        @pl.when(((total >> b) & 1) == 1)
        def _():
            make_copy(0, 0, 1 << b).wait()


def _dispatch_kernel(cnt_ref, off_ref, run_ref, tcnt_ref, tdst_ref, nu_ref, h2_ref, ti_ref, tg_ref, offv_ref,
                     xs_ref, lp_ref, stage, zeros, perm_s, ghi_s, glo_s, sem, semz):
    i = pl.program_id(0)
    slot = i % 2

    def copy_out(s_):
        return lambda s, d, n: pltpu.make_async_copy(stage.at[s_, pl.ds(s, n), :], xs_ref.at[pl.ds(d, n), :],
                                                     sem.at[s_])

    def tile_rows(t):
        last = t * N_EXPERTS + N_EXPERTS - 1
        return off_ref[last] + cnt_ref[last]

    @pl.when(i == 0)
    def _():
        zeros[...] = jnp.zeros_like(zeros)
        zero_out = lambda s, d, n: pltpu.make_async_copy(zeros.at[pl.ds(0, n), :], xs_ref.at[pl.ds(d, n), :], semz)
        _run_copies(tcnt_ref, None, tdst_ref, 0, N_EXPERTS, TAIL_BITS, zero_out, True)
        _run_copies(tcnt_ref, None, tdst_ref, 0, N_EXPERTS, TAIL_BITS, zero_out, False)
        zrows = zeros.shape[0]

        def unused_blocks(start):
            def body(b, carry):
                for part in range(MOE_BLK // zrows):
                    cp = zero_out(0, pl.multiple_of(b * MOE_BLK + part * zrows, RUN_ALIGN), zrows)
                    if start:
                        cp.start()
                    else:
                        cp.wait()
                return carry
            lax.fori_loop(nu_ref[0], N_BLOCKS, body, 0)
        unused_blocks(True)
        unused_blocks(False)

    @pl.when(i >= 2)
    def _():
        _wait_rows(tile_rows(i - 2), copy_out(slot))

    lane = lax.broadcasted_iota(I32, (TM, LANES), 1)
    picks = [jnp.where(lane == ti_ref[:, k:k + 1], 1.0, 0.0) for k in range(TOP_K)]
    ri = lax.broadcasted_iota(I32, (TM, TM), 0)
    ci = lax.broadcasted_iota(I32, (TM, TM), 1)
    earlier = jnp.where(ci < ri, 1.0, 0.0).astype(BF16)
    base = _dot(earlier, functools.reduce(jnp.add, picks).astype(BF16)) + offv_ref[0][0:1, :]
    lp = jnp.full((TM, LANES), -1.0, F32)
    for k in range(TOP_K):
        lp = jnp.where(lane == k, jnp.sum(picks[k] * base, axis=-1, keepdims=True), lp)
    lp_ref[...] = lp.astype(I32)

    lp_t = lp.T
    tg_t = tg_ref[...].T
    for c in range(STAGE_ROWS // LANES):
        rows = slice(c * LANES, (c + 1) * LANES)
        row = (lax.broadcasted_iota(I32, (LANES, TM), 0) + c * LANES).astype(F32)
        perm = jnp.zeros((LANES, TM), F32)
        gates = jnp.zeros((LANES, TM), F32)
        for k in range(TOP_K):
            hit = row == lp_t[k:k + 1, :]
            perm = jnp.where(hit, 1.0, perm)
            gates = jnp.where(hit, tg_t[k:k + 1, :], gates)
        perm_s[rows, :] = perm.astype(BF16)
        g_hi = gates.astype(BF16)
        ghi_s[rows, :] = g_hi
        glo_s[rows, :] = (gates - g_hi.astype(F32)).astype(BF16)
    stage[slot, :, 0:D_MODEL] = _dot(perm_s[...], h2_ref[...])
    ones = jnp.ones((TM, LANES), BF16)
    stage[slot, :, D_MODEL:XS_COLS] = _dot(ghi_s[...], ones) + _dot(glo_s[...], ones)

    _run_copies(cnt_ref, off_ref, run_ref, i * N_EXPERTS, N_EXPERTS, RUN_BITS, copy_out(slot), True)

    @pl.when(i == N_TILES - 1)
    def _():
        _wait_rows(tile_rows(i - 1), copy_out(1 - slot))
        _wait_rows(tile_rows(i), copy_out(slot))


def _dispatch(rt, h2, top_i, top_g):
    row = lambda n: pl.BlockSpec((TM, n), lambda i, *_: (i, 0))
    return pl.pallas_call(
        _dispatch_kernel,
        grid_spec=pltpu.PrefetchScalarGridSpec(
            num_scalar_prefetch=6,
            grid=(N_TILES,),
            in_specs=[row(D_MODEL), row(LANES), row(LANES),
                      pl.BlockSpec((1, SUBLANES, LANES), lambda i, *_: (i, 0, 0))],
            out_specs=(pl.BlockSpec(memory_space=pl.ANY), row(LANES)),
            scratch_shapes=[
                pltpu.VMEM((2, STAGE_ROWS, XS_COLS), F32),
                pltpu.VMEM((1 << TAIL_BITS[0], XS_COLS), F32),
                pltpu.VMEM((STAGE_ROWS, TM), BF16), pltpu.VMEM((STAGE_ROWS, TM), BF16),
                pltpu.VMEM((STAGE_ROWS, TM), BF16),
                pltpu.SemaphoreType.DMA((2,)),
                pltpu.SemaphoreType.DMA,
            ],
        ),
        out_shape=(jax.ShapeDtypeStruct((N_SLOTS, XS_COLS), F32), jax.ShapeDtypeStruct((N_TOK, LANES), I32)),
        compiler_params=pltpu.CompilerParams(dimension_semantics=("arbitrary",), vmem_limit_bytes=VMEM_LIMIT),
        name="dispatch",
    )(rt["cnt"], rt["off"], rt["run"], rt["tail_cnt"], rt["tail_dst"], rt["n_used"], h2, top_i, top_g, rt["off_v"])


def _moe_kernel(layer, be_ref, nxt_ref, nv_ref, nu_ref, x_ref, wgu_hbm, wdn_hbm, bg_ref, bu_ref, bd_ref, sel_ref,
                y_ref, wgu_ref, wdn_ref, wgu_s, wdn_s, sem):
    i = pl.program_id(0)
    n_used = nu_ref[0]

    def fetch(e):
        return (pltpu.make_async_copy(wgu_hbm.at[layer, e], wgu_ref, sem.at[0]),
                pltpu.make_async_copy(wdn_hbm.at[layer, e], wdn_ref, sem.at[1]))

    @pl.when(i < n_used)
    def _():
        @pl.when(jnp.logical_or(i == 0, be_ref[i] != be_ref[jnp.maximum(i - 1, 0)]))
        def _():
            @pl.when(i == 0)
            def _():
                for cp in fetch(be_ref[0]):
                    cp.start()
            for cp in fetch(be_ref[i]):
                cp.wait()
            half = LANES
            for c in range(2 * EXPERT_FF // (2 * half)):
                r = _dot(wgu_ref[:, c * 2 * half:(c + 1) * 2 * half].astype(BF16), sel_ref[...])
                wgu_s[:, c * half:(c + 1) * half] = r[:, 0:half].astype(BF16)
                wgu_s[:, EXPERT_FF + c * half:EXPERT_FF + (c + 1) * half] = r[:, half:2 * half].astype(BF16)
            wdn_s[...] = wdn_ref[...].astype(BF16)
            nxt = nxt_ref[i]

            @pl.when(nxt >= 0)
            def _():
                for cp in fetch(nxt):
                    cp.start(priority=1)

        def expert_rows(rows):
            gu = _dot(x_ref[0:rows, 0:D_MODEL].astype(BF16), wgu_s[...])
            gate = jnp.minimum(gu[:, 0:EXPERT_FF] + bg_ref[0], SWIGLU_LIMIT)
            up = jnp.clip(gu[:, EXPERT_FF:2 * EXPERT_FF] + bu_ref[0], -SWIGLU_LIMIT, SWIGLU_LIMIT)
            act = gate * jax.nn.sigmoid(SWIGLU_ALPHA * gate) * (up + 1.0)
            y = _dot(act.astype(BF16), wdn_s[...]) + bd_ref[0]
            slot_gate = x_ref[0:rows, D_MODEL:XS_COLS]
            for j in range(D_MODEL // LANES):
                y_ref[0:rows, j * LANES:(j + 1) * LANES] = y[:, j * LANES:(j + 1) * LANES] * slot_gate

        half = MOE_BLK // 2

        @pl.when(nv_ref[i] > half)
        def _():
            expert_rows(MOE_BLK)

        @pl.when(nv_ref[i] <= half)
        def _():
            expert_rows(half)
            y_ref[half:MOE_BLK, :] = jnp.zeros((MOE_BLK - half, D_MODEL), F32)

    @pl.when(i >= n_used)
    def _():
        y_ref[...] = jnp.zeros_like(y_ref)


def _bias_split_kernel(b_ref, sel_ref, o_ref):
    for c in range(2 * EXPERT_FF // (2 * LANES)):
        terms = _split3(b_ref[:, c * 2 * LANES:(c + 1) * 2 * LANES])
        r = functools.reduce(jnp.add, [_dot(t, sel_ref[...]) for t in terms])
        o_ref[:, c * LANES:(c + 1) * LANES] = r[:, 0:LANES]
        o_ref[:, EXPERT_FF + c * LANES:EXPERT_FF + (c + 1) * LANES] = r[:, LANES:2 * LANES]


def _bias_split(b_gu, sel):
    n = DEPTH * N_EXPERTS
    out = pl.pallas_call(
        _bias_split_kernel,
        out_shape=jax.ShapeDtypeStruct((n, 2 * EXPERT_FF), F32),
        name="bias_split",
    )(b_gu.reshape(n, 2 * EXPERT_FF), sel)
    return out.reshape(n, 1, 2 * EXPERT_FF)


def _moe(layer, rt, xs, w_gu, w_down, b_gu_split, b_down, sel):
    first = layer * N_EXPERTS
    bg = pl.BlockSpec((1, 1, EXPERT_FF), lambda i, be, *_: (first + be[i], 0, 0))
    bu = pl.BlockSpec((1, 1, EXPERT_FF), lambda i, be, *_: (first + be[i], 0, 1))
    bd = pl.BlockSpec((1, 1, D_MODEL), lambda i, be, *_: (first + be[i], 0, 0))
    return pl.pallas_call(
        functools.partial(_moe_kernel, layer),
        grid_spec=pltpu.PrefetchScalarGridSpec(
            num_scalar_prefetch=4,
            grid=(N_BLOCKS,),
            in_specs=[
                pl.BlockSpec((MOE_BLK, XS_COLS), lambda i, be, nx, nv, nu: (jnp.minimum(i, nu[0] - 1), 0)),
                pl.BlockSpec(memory_space=pl.ANY),
                pl.BlockSpec(memory_space=pl.ANY),
                bg, bu, bd,
                pl.BlockSpec((2 * LANES, 2 * LANES), lambda i, *_: (0, 0)),
            ],
            out_specs=pl.BlockSpec((MOE_BLK, D_MODEL), lambda i, *_: (i, 0)),
            scratch_shapes=[
                pltpu.VMEM((D_MODEL, 2 * EXPERT_FF), F32),
                pltpu.VMEM((EXPERT_FF, D_MODEL), F32),
                pltpu.VMEM((D_MODEL, 2 * EXPERT_FF), BF16),
                pltpu.VMEM((EXPERT_FF, D_MODEL), BF16),
                pltpu.SemaphoreType.DMA((2,)),
            ],
        ),
        out_shape=jax.ShapeDtypeStruct((N_SLOTS, D_MODEL), F32),
        compiler_params=pltpu.CompilerParams(dimension_semantics=("arbitrary",), vmem_limit_bytes=VMEM_LIMIT),
        name="moe",
    )(rt["blk_e"], rt["blk_next"], rt["blk_rows"], rt["n_used"], xs, w_gu, w_down, b_gu_split, b_gu_split,
      b_down.reshape(DEPTH * N_EXPERTS, 1, D_MODEL), sel)


def _combine_kernel(cnt_ref, off_ref, run_ref, y_ref, lp_ref, x1_ref, mod_ref, fw_ref,
                    xc_ref, xl_ref, yc_ref, yl_ref, stage, mine_s, sem):
    i = pl.program_id(0)
    slot = i % 2

    def copy_in(s_):
        return lambda s, d, n: pltpu.make_async_copy(y_ref.at[pl.ds(d, n), :], stage.at[s_, pl.ds(s, n), :],
                                                     sem.at[s_])

    @pl.when(i == 0)
    def _():
        stage[...] = jnp.zeros_like(stage)
        _run_copies(cnt_ref, off_ref, run_ref, 0, N_EXPERTS, RUN_BITS, copy_in(0), True)

    @pl.when(i + 1 < N_TILES)
    def _():
        _run_copies(cnt_ref, off_ref, run_ref, (i + 1) * N_EXPERTS, N_EXPERTS, RUN_BITS, copy_in(1 - slot), True)

    last = i * N_EXPERTS + N_EXPERTS - 1
    _wait_rows(off_ref[last] + cnt_ref[last], copy_in(slot))
    lane = lax.broadcasted_iota(I32, (TM, LANES), 1)
    mine = [jnp.broadcast_to(lp_ref[:, k:k + 1], (TM, LANES)) for k in range(TOP_K)]
    for c in range(STAGE_ROWS // LANES):
        hit = jnp.zeros((TM, LANES), F32)
        for k in range(TOP_K):
            hit = jnp.where(lane + c * LANES == mine[k], 1.0, hit)
        mine_s[:, c * LANES:(c + 1) * LANES] = hit.astype(BF16)
    moe = _dot(mine_s[...], stage[slot].astype(BF16))
    gate2 = mod_ref[0][:, 5 * D_MODEL:6 * D_MODEL]
    x2 = x1_ref[...] + gate2 * moe
    y_norm = _rms(x2, fw_ref[...])

    @pl.when(i < CTX_TILES)
    def _():
        xc_ref[...] = x2
        yc_ref[...] = y_norm

    @pl.when(i >= CTX_TILES)
    def _():
        xl_ref[...] = x2
        yl_ref[...] = y_norm


def _combine(rt, y_slots, lp, x1, mod3, final_w):
    row = lambda n: pl.BlockSpec((TM, n), lambda i, *_: (i, 0))
    ctx_row = pl.BlockSpec((TM, D_MODEL), lambda i, *_: (jnp.minimum(i, CTX_TILES - 1), 0))
    lat_row = pl.BlockSpec((TM, D_MODEL), lambda i, *_: (jnp.maximum(i - CTX_TILES, 0), 0))
    return pl.pallas_call(
        _combine_kernel,
        grid_spec=pltpu.PrefetchScalarGridSpec(
            num_scalar_prefetch=3,
            grid=(N_TILES,),
            in_specs=[
                pl.BlockSpec(memory_space=pl.ANY),
                row(LANES), row(D_MODEL),
                pl.BlockSpec((1, 1, 6 * D_MODEL), lambda i, *_: (_mod_row(i), 0, 0)),
                pl.BlockSpec((1, D_MODEL), lambda i, *_: (0, 0)),
            ],
            out_specs=(ctx_row, lat_row, ctx_row, lat_row),
            scratch_shapes=[pltpu.VMEM((2, STAGE_ROWS, D_MODEL), F32), pltpu.VMEM((TM, STAGE_ROWS), BF16),
                            pltpu.SemaphoreType.DMA((2,))],
        ),
        out_shape=(jax.ShapeDtypeStruct((N_CTX, D_MODEL), F32), jax.ShapeDtypeStruct((N_LAT, D_MODEL), F32),
                   jax.ShapeDtypeStruct((N_CTX, D_MODEL), F32), jax.ShapeDtypeStruct((N_LAT, D_MODEL), F32)),
        compiler_params=pltpu.CompilerParams(dimension_semantics=("arbitrary",), vmem_limit_bytes=VMEM_LIMIT),
        name="combine",
    )(rt["cnt"], rt["off"], rt["run"], y_slots, lp, x1, mod3, final_w)


def _rope_tables():
    rows = DEC_SEQ // GRID_W
    t = jnp.arange(rows * GRID_W)
    row = (t // GRID_W).astype(F32)
    col = (t % GRID_W).astype(F32)
    half = QK_ROPE // 2
    inv = ROPE_BASE ** (-jnp.arange(0, half, 2, dtype=F32) / half)
    ang_r, ang_c = row[:, None] * inv, col[:, None] * inv
    cos32 = jnp.concatenate([jnp.cos(ang_r), jnp.cos(ang_r), jnp.cos(ang_c), jnp.cos(ang_c)], axis=-1)
    sin32 = jnp.concatenate([-jnp.sin(ang_r), jnp.sin(ang_r), -jnp.sin(ang_c), jnp.sin(ang_c)], axis=-1)
    cos32 = jnp.concatenate([jnp.ones((TM, QK_ROPE), F32), cos32], axis=0)
    sin32 = jnp.concatenate([jnp.zeros((TM, QK_ROPE), F32), sin32], axis=0)
    n = cos32.shape[0]
    pad = HEAD_PAD - QK_NOPE - QK_ROPE
    cos_h = jnp.concatenate([jnp.ones((n, QK_NOPE), F32), cos32, jnp.zeros((n, pad), F32)], axis=-1)
    sin_h = jnp.concatenate([jnp.zeros((n, QK_NOPE), F32), sin32, jnp.zeros((n, pad), F32)], axis=-1)
    zeros = jnp.zeros((n, LANES - QK_ROPE), F32)
    return {
        "cosq": cos_h, "sinq": sin_h,
        "cosk": jnp.concatenate([cos32, zeros], axis=-1), "sink": jnp.concatenate([sin32, zeros], axis=-1),
        "vone": jnp.zeros((MLA_HEADS, HEAD_PAD), F32).at[:, V_HEAD].set(1.0).reshape(1, MLA_HEADS * HEAD_PAD),
    }


def _layer_weights(p, l):
    hd = QK_NOPE + QK_ROPE
    pad = HEAD_PAD - hd
    wq = p["w_uq"][l].reshape(Q_RANK, MLA_HEADS, hd)
    wq_a = jnp.pad(wq, ((0, 0), (0, 0), (0, pad)))
    wuk = jnp.pad(p["w_uk"][l], ((0, 0), (0, 0), (0, HEAD_PAD - QK_NOPE)))
    rw = jnp.pad(p["router_w"][l], ((0, 0), (0, LANES - N_EXPERTS)))
    rb = jnp.concatenate([p["router_b"][l], jnp.full((LANES - N_EXPERTS,), -jnp.inf, F32)])
    return {
        "norm1_w": p["norm1_w"][l][None], "q_norm_w": p["q_norm_w"][l][None],
        "wq_a": wq_a.reshape(Q_RANK, -1).astype(BF16),
        "kv_norm_w": p["kv_norm_w"][l][None],
        "wuk": wuk.reshape(KV_RANK, -1).astype(BF16),
        "wuv": jnp.pad(p["w_uv"][l], ((0, 0), (0, 0), (0, HEAD_PAD - V_HEAD))).reshape(KV_RANK, -1).astype(BF16),
        "wuv_c": p["w_uv"][l].reshape(KV_RANK, -1).astype(BF16),
        "conv_a_w": p["conv_a_w"][l], "ssm_conv_w": p["ssm_conv_w"][l], "ssm_conv_b": p["ssm_conv_b"][l][None],
        "w_out_a": p["w_out_a"][l].astype(BF16), "w_o_mla": p["w_o_mla"][l].astype(BF16),
        "d_skip": jnp.repeat(p["d_skip"][l], SSM_HEAD_DIM)[None], "ssm_norm_w": p["ssm_norm_w"][l][None],
        "w_o_ssm": p["w_o_ssm"][l].astype(BF16), "w_o": p["w_o"][l].astype(BF16),
        "norm2_w": p["norm2_w"][l][None], "router_w": rw.astype(BF16), "router_b": rb[None],
    }


def _ssd_params(p, l):
    z = lambda n: jnp.zeros((n,), F32)
    lanes = lambda f, b: jnp.concatenate([z(SM_DTF), f, b, z(LANES - SM_DTB - SSM_HEADS)])
    ones = jnp.ones((SSM_HEADS,), F32)
    rows = [lanes(p["dt_bias_fwd"][l], p["dt_bias_bwd"][l]), lanes(p["a_log_fwd"][l], p["a_log_bwd"][l]),
            lanes(ones, ones)]
    return jnp.concatenate([jnp.stack(rows), jnp.zeros((SUBLANES - 3, LANES), F32)], axis=0)


def _routing(cnt_tiles):
    cnt = cnt_tiles[:, 0, 0:N_EXPERTS]
    cnt = (cnt + RUN_ALIGN - 1) // RUN_ALIGN * RUN_ALIGN
    per_expert = jnp.sum(cnt, axis=0)
    padded = (per_expert + MOE_BLK - 1) // MOE_BLK * MOE_BLK
    pad_end = jnp.cumsum(padded)
    pad_start = pad_end - padded
    run = pad_start[None, :] + jnp.cumsum(cnt, axis=0) - cnt
    off = jnp.cumsum(cnt, axis=1) - cnt
    starts = jnp.arange(N_BLOCKS, dtype=I32) * MOE_BLK
    blk_e = jnp.minimum(jnp.sum((pad_end[None, :] <= starts[:, None]).astype(I32), axis=1), N_EXPERTS - 1)
    off_v = jnp.zeros((N_TILES, SUBLANES, LANES), F32).at[:, 0, 0:N_EXPERTS].set(off.astype(F32))
    ids = jnp.arange(N_EXPERTS, dtype=I32)
    later = jnp.logical_and(ids[None, :] > ids[:, None], padded[None, :] > 0)
    nxt = jnp.min(jnp.where(later, ids[None, :], N_EXPERTS), axis=1)
    nxt = jnp.where(nxt == N_EXPERTS, -1, nxt)
    per_block = lambda v: jnp.sum(jnp.where(blk_e[:, None] == ids[None, :], v[None, :], 0), axis=1)
    blk_rows = jnp.clip(per_block(pad_start + per_expert) - starts, 0, MOE_BLK)
    return {
        "blk_next": per_block(nxt).astype(I32), "blk_rows": blk_rows.astype(I32),
        "cnt": cnt.reshape(-1).astype(I32), "off": off.reshape(-1).astype(I32), "run": run.reshape(-1).astype(I32),
        "tail_cnt": (padded - per_expert).astype(I32), "tail_dst": (pad_start + per_expert).astype(I32),
        "blk_e": blk_e.astype(I32), "n_used": (pad_end[-1] // MOE_BLK).astype(I32).reshape(1), "off_v": off_v,
    }


def _deinterleave_matrix():
    k = jnp.arange(2 * LANES)[:, None]
    n = jnp.arange(2 * LANES)[None, :]
    src = jnp.where(n < LANES, 2 * n, 2 * (n - LANES) + 1)
    return (k == src).astype(BF16)


def kernel(x_prompt, x_sample, cache_ckv, cache_kpe, state_ssm_fwd, state_ssm_bwd, c, c_ctx, w_ada, b_ada, norm1_w, w_in, conv_a_w, w_out_a, q_norm_w, w_uq, kv_norm_w, w_uk, w_uv, w_o_mla, ssm_conv_w, ssm_conv_b, dt_bias_fwd, dt_bias_bwd, a_log_fwd, a_log_bwd, d_skip, ssm_norm_w, w_o_ssm, w_o, norm2_w, router_w, router_b, w_gu, b_gu, w_down, b_down, final_norm_w):
    p = dict(norm1_w=norm1_w, w_in=w_in, conv_a_w=conv_a_w, w_out_a=w_out_a, q_norm_w=q_norm_w, w_uq=w_uq,
             kv_norm_w=kv_norm_w, w_uk=w_uk, w_uv=w_uv, w_o_mla=w_o_mla, ssm_conv_w=ssm_conv_w,
             ssm_conv_b=ssm_conv_b, dt_bias_fwd=dt_bias_fwd, dt_bias_bwd=dt_bias_bwd, a_log_fwd=a_log_fwd,
             a_log_bwd=a_log_bwd, d_skip=d_skip, ssm_norm_w=ssm_norm_w, w_o_ssm=w_o_ssm, w_o=w_o,
             norm2_w=norm2_w, router_w=router_w, router_b=router_b, b_gu=b_gu, b_down=b_down)
    xc = x_prompt.reshape(N_CTX, D_MODEL)
    xl = x_sample.reshape(N_LAT, D_MODEL)
    cond8 = jnp.concatenate([c_ctx[None], c, jnp.zeros((SUBLANES - 1 - DEC_BATCH, D_MODEL), F32)], axis=0)
    mods = _ada_mods(cond8, w_ada, b_ada)
    tabs = _rope_tables()
    sel = _deinterleave_matrix()
    b_gu_split = _bias_split(b_gu, sel)
    w_in_t = jnp.swapaxes(w_in, 1, 2)
    final_w = final_norm_w[None]
    hp = SSM_INNER

    caches, states = (), ()
    y_ctx = y_lat = None
    for l in range(DEPTH):
        lw = _layer_weights(p, l)
        mod3 = mods[l].reshape(SUBLANES, 1, 6 * D_MODEL)
        ta, q, ckv, kpe, kf, v_ctx, v_lat, small, z, xact, g = _inproj(l, xc, xl, mod3, w_in_t, lw, tabs, caches)
        caches = (ckv, kpe)
        kpe_c = jnp.pad(cache_kpe[:, l].reshape(DEC_BATCH * PAST_LEN, QK_ROPE), ((0, 0), (0, LANES - QK_ROPE)))
        kf_c, v_c = _kvcache(cache_ckv[:, l].reshape(DEC_BATCH * PAST_LEN, KV_RANK), kpe_c, lw, tabs)
        att_ctx, att_lat = _attention(q, kf, v_ctx, v_lat, kf_c, v_c)
        init_f = state_ssm_fwd[:, l].reshape(DEC_BATCH, hp, SSM_STATE)
        init_b = state_ssm_bwd[:, l].reshape(DEC_BATCH, hp, SSM_STATE)
        yf, yb, sf, sb = _ssd(l, xact, small, init_f, init_b, _ssd_params(p, l), states)
        states = (sf, sb)
        x1, h2, top_i, top_g, cnt_tiles = _merge(ta, att_ctx, att_lat, yf, yb, xact, z, g, xc, xl, mod3, lw)
        rt = _routing(cnt_tiles)
        xs, lp = _dispatch(rt, h2, top_i, top_g)
        y_slots = _moe(l, rt, xs, w_gu, w_down, b_gu_split, b_down, sel)
        xc, xl, y_ctx, y_lat = _combine(rt, y_slots, lp, x1, mod3, final_w)

    y_prompt = y_ctx.reshape(BATCH, SEQ, D_MODEL)
    y_sample = y_lat.reshape(DEC_BATCH, DEC_SEQ, D_MODEL)
    state_shape = (BATCH, DEPTH, SSM_HEADS, SSM_HEAD_DIM, SSM_STATE)
    return (y_prompt, y_sample, caches[0], caches[1], states[0].reshape(state_shape), states[1].reshape(state_shape))
```

```python
import functools
import math

import jax
import jax.numpy as jnp
from jax import lax
from jax.experimental import pallas as pl
from jax.experimental.pallas import tpu as pltpu

F32 = jnp.float32
BF16 = jnp.bfloat16
I32 = jnp.int32

D_MODEL = 1024
BATCH = 16
SEQ = 256
DEPTH = 2
DEC_BATCH = 2
DEC_SEQ = 2048
PAST_LEN = 512
GRID_W = 64
NORM_EPS = 1e-6
CONV_DIM = 512
MLA_HEADS = 8
Q_RANK = 384
KV_RANK = 256
QK_NOPE = 64
QK_ROPE = 32
V_HEAD = 64
ROPE_BASE = 10000.0
MLA_SCALE = (QK_NOPE + QK_ROPE) ** -0.5
SSM_HEADS = 16
SSM_HEAD_DIM = 64
SSM_INNER = SSM_HEADS * SSM_HEAD_DIM
SSM_GROUPS = 2
SSM_STATE = 128
SSM_CONV_CH = SSM_INNER + 2 * SSM_GROUPS * SSM_STATE
N_EXPERTS = 32
TOP_K = 4
EXPERT_FF = D_MODEL
SWIGLU_ALPHA = 1.702
SWIGLU_LIMIT = 7.0

N_CTX = BATCH * SEQ
N_LAT = DEC_BATCH * DEC_SEQ
N_TOK = N_CTX + N_LAT
N_SEQS = BATCH + DEC_BATCH

LANES = 128
SUBLANES = 8
HEAD_PAD = 128
TM = 256
N_TILES = N_TOK // TM
CTX_TILES = N_CTX // TM
LAT_TILES_PER_SEQ = DEC_SEQ // TM
TQ_LAT = 512
CHUNK = 128
SSD_STEP = 2 * CHUNK
CTX_STEPS_PER_SEQ = SEQ // SSD_STEP
LAT_STEPS_PER_SEQ = DEC_SEQ // SSD_STEP
N_CTX_STEPS = N_CTX // SSD_STEP
N_SSD_STEPS = N_TOK // SSD_STEP
MOE_BLK = 256
N_ASSIGN = N_TOK * TOP_K
RUN_ALIGN = SUBLANES
RUN_BITS = tuple(range(8, 2, -1))
TAIL_BITS = tuple(range(7, 2, -1))
COMMON_BITS = 6
WAIT_BITS = tuple(range(10, 2, -1))
STAGE_ROWS = 1280
XS_COLS = D_MODEL + LANES
N_BLOCKS = -(-(N_ASSIGN + N_TILES * N_EXPERTS * (RUN_ALIGN - 1) + N_EXPERTS * (MOE_BLK - 1)) // MOE_BLK)
N_SLOTS = N_BLOCKS * MOE_BLK
VMEM_LIMIT = 56 * 1024 * 1024

C_A3 = 0
C_CQ = C_A3 + 3 * CONV_DIM
C_CKV = C_CQ + Q_RANK
C_Z = C_CKV + KV_RANK
C_XBC = C_Z + SSM_INNER
C_GATE = C_XBC + SSM_CONV_CH
C_SMALL = C_GATE + 3 * D_MODEL
IN_COLS2 = C_SMALL + LANES
SM_DTF = QK_ROPE
SM_DTB = QK_ROPE + SSM_HEADS
S_CQ = 3 * CONV_DIM
S_CKV = S_CQ + Q_RANK
S_KPE = S_CKV + KV_RANK
S_Z = S_KPE + QK_ROPE
S_XBC = S_Z + SSM_INNER
S_DTF = S_XBC + SSM_CONV_CH
S_GATE = S_DTF + 2 * SSM_HEADS
IN_COLS = S_GATE + 3 * D_MODEL
W_SEGMENTS = ((C_A3, 0, 3 * CONV_DIM), (C_CQ, S_CQ, Q_RANK), (C_CKV, S_CKV, KV_RANK), (C_Z, S_Z, SSM_INNER),
              (C_XBC, S_XBC, SSM_CONV_CH), (C_GATE, S_GATE, 3 * D_MODEL))
W_PIECE = 512


def _rms(x, w):
    return x * lax.rsqrt(jnp.mean(x * x, axis=-1, keepdims=True) + NORM_EPS) * w


def _silu(x):
    return x * jax.nn.sigmoid(x)


def _dot(a, b):
    return jnp.dot(a, b, preferred_element_type=F32)


def _dot_nt(a, b):
    return lax.dot_general(a, b, (((1,), (1,)), ((), ())), preferred_element_type=F32)


def _resident(shape):
    nd = len(shape)
    return pl.BlockSpec(shape, lambda *_: (0,) * nd, pipeline_mode=pl.Buffered(1))


def _mod_row(i):
    return jnp.where(i < CTX_TILES, 0, 1 + (i - CTX_TILES) // LAT_TILES_PER_SEQ)


def _pos_block(i):
    return jnp.where(i < CTX_TILES, 0, 1 + (i - CTX_TILES) % LAT_TILES_PER_SEQ)


def _ada_kernel(c_ref, w_ref, b_ref, o_ref):
    s = _silu(c_ref[...]).astype(BF16)
    o_ref[0] = _dot(s, w_ref[0].astype(BF16)) + b_ref[0]


def _ada_mods(cond8, w_ada, b_ada):
    tn = 1536
    n_mod = 6 * D_MODEL
    return pl.pallas_call(
        _ada_kernel,
        grid=(DEPTH, n_mod // tn),
        in_specs=[
            pl.BlockSpec((SUBLANES, D_MODEL), lambda l, j: (0, 0)),
            pl.BlockSpec((1, D_MODEL, tn), lambda l, j: (l, 0, j)),
            pl.BlockSpec((1, 1, tn), lambda l, j: (l, 0, j)),
        ],
        out_specs=pl.BlockSpec((1, SUBLANES, tn), lambda l, j: (l, 0, j)),
        out_shape=jax.ShapeDtypeStruct((DEPTH, SUBLANES, n_mod), F32),
        compiler_params=pltpu.CompilerParams(dimension_semantics=("arbitrary", "arbitrary")),
        name="ada_mods",
    )(cond8, w_ada, b_ada.reshape(DEPTH, 1, n_mod))


def _relayout_w_in(layer, wt_hbm, w2, stg, small_stg, sem, small_sem):
    pieces = [(dst + p, src + p, min(W_PIECE, width - p))
              for dst, src, width in W_SEGMENTS for p in range(0, width, W_PIECE)]
    fetch = lambda k: pltpu.make_async_copy(wt_hbm.at[layer, pl.ds(pieces[k][1], pieces[k][2]), :],
                                            stg.at[k % 2, pl.ds(0, pieces[k][2]), :], sem.at[k % 2])
    small_stg[...] = jnp.zeros_like(small_stg)
    small_copies = [
        pltpu.make_async_copy(wt_hbm.at[layer, pl.ds(S_KPE, QK_ROPE), :], small_stg.at[pl.ds(0, QK_ROPE), :],
                              small_sem.at[0]),
        pltpu.make_async_copy(wt_hbm.at[layer, pl.ds(S_DTF, 2 * SSM_HEADS), :],
                              small_stg.at[pl.ds(SM_DTF, 2 * SSM_HEADS), :], small_sem.at[1]),
    ]
    for cp in small_copies:
        cp.start()
    fetch(0).start()
    for k, (dst, _, width) in enumerate(pieces):
        if k + 1 < len(pieces):
            fetch(k + 1).start()
        fetch(k).wait()
        w2[:, dst:dst + width] = stg[k % 2, 0:width, :].T.astype(BF16)
    for cp in small_copies:
        cp.wait()
    w2[:, C_SMALL:IN_COLS2] = small_stg[...].T.astype(BF16)


def _place_rope_key(k_nope, kpe):
    shifted = pltpu.roll(kpe, QK_NOPE, 1)
    return jnp.concatenate([k_nope[:, h * HEAD_PAD:(h + 1) * HEAD_PAD] + shifted for h in range(MLA_HEADS)],
                           axis=1).astype(BF16)


def _rot_partner(x):
    n = x.shape[1]
    lane = lax.broadcasted_iota(I32, x.shape, 1)
    quarter = QK_ROPE // 4
    return jnp.where(lane % (2 * quarter) < quarter, pltpu.roll(x, n - quarter, 1), pltpu.roll(x, quarter, 1))


def _first_layer_slabs(ref, value):
    ref[0, 0] = value
    ref[0, 1:DEPTH] = jnp.zeros((DEPTH - 1,) + value.shape, value.dtype)


def _inproj_kernel(layer, *refs):
    n_in = 23 + (2 if layer > 0 else 0)
    (xc_ref, xcp_ref, xcn_ref, xl_ref, xlp_ref, xln_ref, mod_ref, n1w_ref, w_hbm, qnw_ref, wq_ref, kvnw_ref, wuk_ref,
     wuv_ref, wuvc_ref, vone_ref, cosq_ref, sinq_ref, cosk_ref, sink_ref, caw_ref, cw_ref, cb_ref) = refs[:23]
    (ta_ref, q_ref, ckv_ref, kpe_ref, kf_ref, vc_ref, vl_ref, small_ref, z_ref, xact_ref, g_ref,
     w2, stg, small_stg, sem, small_sem) = refs[n_in:]
    i = pl.program_id(0)

    @pl.when(i == 0)
    def _():
        _relayout_w_in(layer, w_hbm, w2, stg, small_stg, sem, small_sem)

    mod = mod_ref[0]
    shift1 = mod[:, 0:D_MODEL]
    scale1 = mod[:, D_MODEL:2 * D_MODEL]
    x_ext = jnp.where(i < CTX_TILES, jnp.concatenate([xcp_ref[...], xc_ref[...], xcn_ref[...]], axis=0),
                      jnp.concatenate([xlp_ref[...], xl_ref[...], xln_ref[...]], axis=0))
    h_ext = _rms(x_ext, n1w_ref[...]) * (1.0 + scale1) + shift1
    hb_ext = h_ext.astype(BF16)
    hb = h_ext[SUBLANES:SUBLANES + TM].astype(BF16)

    j = (i - CTX_TILES) % LAT_TILES_PER_SEQ
    is_ctx = i < CTX_TILES
    keep_prev = jnp.where(jnp.logical_or(is_ctx, j == 0), 0.0, 1.0)
    keep_next = jnp.where(jnp.logical_or(is_ctx, j == LAT_TILES_PER_SEQ - 1), 0.0, 1.0)
    row = lax.broadcasted_iota(I32, (TM, 1), 0)
    prev_mask = jnp.where(row == 0, keep_prev, 1.0)
    next_mask = jnp.where(row == TM - 1, keep_next, 1.0)
    ext = TM + 2 * SUBLANES

    def conv3(x, w_ref):
        x_prev = pltpu.roll(x, 1, 0)[SUBLANES:SUBLANES + TM] * prev_mask
        x_next = pltpu.roll(x, ext - 1, 0)[SUBLANES:SUBLANES + TM] * next_mask
        return x_prev * w_ref[0:1, :] + x[SUBLANES:SUBLANES + TM] * w_ref[1:2, :] + x_next * w_ref[2:3, :]

    a3 = _dot(hb_ext, w2[:, C_A3:C_CQ])
    s = a3[:, 2 * CONV_DIM:3 * CONV_DIM] * a3[:, 0:CONV_DIM]
    ta_ref[...] = (a3[SUBLANES:SUBLANES + TM, CONV_DIM:2 * CONV_DIM] * conv3(s, caw_ref)).astype(BF16)
    xact_ref[...] = _silu(conv3(_dot(hb_ext, w2[:, C_XBC:C_GATE]), cw_ref) + cb_ref[...])

    def seg(a, b):
        return _dot(hb, w2[:, a:b])

    cqn = _rms(seg(C_CQ, C_CKV), qnw_ref[...]).astype(BF16)
    qa = _dot(cqn, wq_ref[...])
    qa_rot = _rot_partner(qa)
    for h in range(MLA_HEADS):
        hs = slice(h * HEAD_PAD, (h + 1) * HEAD_PAD)
        q_ref[:, hs] = (qa[:, hs] * cosq_ref[...] + qa_rot[:, hs] * sinq_ref[...]).astype(BF16)

    ckv = _rms(seg(C_CKV, C_Z), kvnw_ref[...])
    ckvb = ckv.astype(BF16)
    small = seg(C_SMALL, IN_COLS2)
    small_ref[...] = small
    kpe = small * cosk_ref[...] + _rot_partner(small) * sink_ref[...]
    kf_ref[...] = _place_rope_key(_dot(ckvb, wuk_ref[...]), kpe)
    v_lat = (_dot(ckvb, wuv_ref[...]) + vone_ref[...]).astype(BF16)
    v_ctx = _dot(ckvb, wuvc_ref[...]).astype(BF16)

    z_ref[...] = seg(C_Z, C_XBC)
    g_ref[...] = jax.nn.sigmoid(seg(C_GATE, C_SMALL))

    @pl.when(i < CTX_TILES)
    def _():
        if layer == 0:
            _first_layer_slabs(ckv_ref, ckv)
            _first_layer_slabs(kpe_ref, small[:, 0:QK_ROPE])
        else:
            ckv_ref[0] = ckv
            kpe_ref[0] = small[:, 0:QK_ROPE]
        vc_ref[...] = v_ctx

    @pl.when(i >= CTX_TILES)
    def _():
        vl_ref[...] = v_lat


def _inproj(layer, xc, xl, mod3, w_in, lw, tabs, caches):
    row = lambda n: pl.BlockSpec((TM, n), lambda i: (i, 0))
    tab = lambda n: pl.BlockSpec((TM, n), lambda i: (_pos_block(i), 0))
    per = TM // SUBLANES
    qw = MLA_HEADS * HEAD_PAD
    ctx_row = lambda n: pl.BlockSpec((TM, n), lambda i: (jnp.minimum(i, CTX_TILES - 1), 0))
    lat_row = lambda n: pl.BlockSpec((TM, n), lambda i: (jnp.maximum(i - CTX_TILES, 0), 0))

    def halo(first_tile, n_rows, side):
        last = n_rows // SUBLANES - 1
        return pl.BlockSpec((SUBLANES, D_MODEL),
                            lambda i: (jnp.clip((i - first_tile + side) * per - 1 + side, 0, last), 0))

    out_shape = (
        jax.ShapeDtypeStruct((N_TOK, CONV_DIM), BF16),
        jax.ShapeDtypeStruct((N_TOK, qw), BF16),
        jax.ShapeDtypeStruct((BATCH, DEPTH, SEQ, KV_RANK), F32),
        jax.ShapeDtypeStruct((BATCH, DEPTH, SEQ, QK_ROPE), F32),
        jax.ShapeDtypeStruct((N_TOK, qw), BF16),
        jax.ShapeDtypeStruct((N_CTX, MLA_HEADS * V_HEAD), BF16),
        jax.ShapeDtypeStruct((N_LAT, qw), BF16),
        jax.ShapeDtypeStruct((N_TOK, LANES), F32),
        jax.ShapeDtypeStruct((N_TOK, SSM_INNER), F32),
        jax.ShapeDtypeStruct((N_TOK, SSM_CONV_CH), F32),
        jax.ShapeDtypeStruct((N_TOK, 3 * D_MODEL), F32),
    )
    out_specs = [row(s.shape[-1]) for s in out_shape]
    cache_block = (1, DEPTH, SEQ) if layer == 0 else (1, None, SEQ)
    cache_index = lambda i: (jnp.minimum(i, CTX_TILES - 1), 0 if layer == 0 else layer, 0, 0)
    out_specs[2] = pl.BlockSpec(cache_block + (KV_RANK,), cache_index)
    out_specs[3] = pl.BlockSpec(cache_block + (QK_ROPE,), cache_index)
    extra = {} if layer == 0 else dict(input_output_aliases={23: 2, 24: 3})
    out_specs[5] = ctx_row(MLA_HEADS * V_HEAD)
    out_specs[6] = lat_row(qw)
    return pl.pallas_call(
        functools.partial(_inproj_kernel, layer),
        grid=(N_TILES,),
        in_specs=[
            ctx_row(D_MODEL), halo(0, N_CTX, 0), halo(0, N_CTX, 1),
            lat_row(D_MODEL), halo(CTX_TILES, N_LAT, 0), halo(CTX_TILES, N_LAT, 1),
            pl.BlockSpec((1, 1, 6 * D_MODEL), lambda i: (_mod_row(i), 0, 0)),
            _resident((1, D_MODEL)),
            pl.BlockSpec(memory_space=pl.ANY),
            _resident((1, Q_RANK)),
            _resident((Q_RANK, qw)),
            _resident((1, KV_RANK)),
            _resident((KV_RANK, qw)),
            _resident((KV_RANK, qw)),
            _resident((KV_RANK, MLA_HEADS * V_HEAD)),
            _resident((1, qw)),
            tab(HEAD_PAD), tab(HEAD_PAD), tab(LANES), tab(LANES),
            _resident((3, CONV_DIM)), _resident((3, SSM_CONV_CH)), _resident((1, SSM_CONV_CH)),
        ] + [pl.BlockSpec(memory_space=pl.ANY)] * len(caches),
        out_specs=tuple(out_specs),
        out_shape=out_shape,
        scratch_shapes=[pltpu.VMEM((D_MODEL, IN_COLS2), BF16), pltpu.VMEM((2, W_PIECE, D_MODEL), F32),
                        pltpu.VMEM((LANES, D_MODEL), F32), pltpu.SemaphoreType.DMA((2,)),
                        pltpu.SemaphoreType.DMA((2,))],
        compiler_params=pltpu.CompilerParams(dimension_semantics=("arbitrary",), vmem_limit_bytes=VMEM_LIMIT),
        name="inproj",
        **extra,
    )(xc, xc, xc, xl, xl, xl, mod3, lw["norm1_w"], w_in, lw["q_norm_w"], lw["wq_a"], lw["kv_norm_w"],
      lw["wuk"], lw["wuv"], lw["wuv_c"], tabs["vone"], tabs["cosq"], tabs["sinq"], tabs["cosk"],
      tabs["sink"], lw["conv_a_w"], lw["ssm_conv_w"], lw["ssm_conv_b"], *caches)


def _kvcache_kernel(ckv_ref, kpe_ref, wuk_ref, wuv_ref, vone_ref, kf_ref, v_ref):
    ckvb = ckv_ref[...].astype(BF16)
    kf_ref[...] = _place_rope_key(_dot(ckvb, wuk_ref[...]), kpe_ref[...])
    v_ref[...] = (_dot(ckvb, wuv_ref[...]) + vone_ref[...]).astype(BF16)


def _kvcache(ckv, kpe128, lw, tabs):
    n = ckv.shape[0]
    qw = MLA_HEADS * HEAD_PAD
    return pl.pallas_call(
        _kvcache_kernel,
        grid=(n // PAST_LEN,),
        in_specs=[
            pl.BlockSpec((PAST_LEN, KV_RANK), lambda i: (i, 0)),
            pl.BlockSpec((PAST_LEN, LANES), lambda i: (i, 0)),
            _resident((KV_RANK, qw)),
            _resident((KV_RANK, qw)),
            _resident((1, qw)),
        ],
        out_specs=(pl.BlockSpec((PAST_LEN, qw), lambda i: (i, 0)), pl.BlockSpec((PAST_LEN, qw), lambda i: (i, 0))),
        out_shape=(jax.ShapeDtypeStruct((n, qw), BF16), jax.ShapeDtypeStruct((n, qw), BF16)),
        compiler_params=pltpu.CompilerParams(dimension_semantics=("arbitrary",)),
        name="kvcache",
    )(ckv, kpe128, lw["wuk"], lw["wuv"], tabs["vone"])


def _attn_heads(q_ref, kv_refs, o_ref, acc_ref, denom_from_matmul):
    log2_scale = MLA_SCALE * math.log2(math.e)
    for h in range(MLA_HEADS):
        hs = slice(h * HEAD_PAD, (h + 1) * HEAD_PAD)
        qh = q_ref[:, hs]
        ss = [_dot_nt(qh, k_ref[:, hs]) for k_ref, _ in kv_refs]
        if denom_from_matmul:
            m = functools.reduce(jnp.maximum, [jnp.max(s, axis=-1, keepdims=True) for s in ss])
            ps = [jnp.exp2((s - m) * log2_scale) for s in ss]
            ol = functools.reduce(jnp.add, [_dot(p.astype(BF16), v_ref[:, hs])
                                            for p, (_, v_ref) in zip(ps, kv_refs)])
            acc_ref[:, h * V_HEAD:(h + 1) * V_HEAD] = ol[:, 0:V_HEAD] / ol[:, V_HEAD:V_HEAD + 1]
        else:
            ss = [s * MLA_SCALE for s in ss]
            m = functools.reduce(jnp.maximum, [jnp.max(s, axis=-1, keepdims=True) for s in ss])
            ps = [jnp.exp(s - m) for s in ss]
            l = functools.reduce(jnp.add, [jnp.sum(p, axis=-1, keepdims=True) for p in ps])
            o = functools.reduce(jnp.add, [_dot(p.astype(BF16), v_ref[:, h * V_HEAD:(h + 1) * V_HEAD])
                                           for p, (_, v_ref) in zip(ps, kv_refs)])
            acc_ref[:, h * V_HEAD:(h + 1) * V_HEAD] = o / l
    o_ref[...] = acc_ref[...].astype(BF16)


def _attn_ctx_kernel(q_ref, k_ref, v_ref, o_ref, acc_ref):
    _attn_heads(q_ref, [(k_ref, v_ref)], o_ref, acc_ref, denom_from_matmul=False)


def _attn_lat_kernel(q_ref, k_ref, v_ref, kc_ref, vc_ref, o_ref, acc_ref):
    _attn_heads(q_ref, [(k_ref, v_ref), (kc_ref, vc_ref)], o_ref, acc_ref, denom_from_matmul=True)


def _attention(q, kf, v_ctx, v_lat, kf_c, v_c):
    qw = MLA_HEADS * HEAD_PAD
    vw = MLA_HEADS * V_HEAD
    att_ctx = pl.pallas_call(
        _attn_ctx_kernel,
        grid=(BATCH,),
        in_specs=[pl.BlockSpec((SEQ, qw), lambda b: (b, 0)), pl.BlockSpec((SEQ, qw), lambda b: (b, 0)),
                  pl.BlockSpec((SEQ, vw), lambda b: (b, 0))],
        out_specs=pl.BlockSpec((SEQ, vw), lambda b: (b, 0)),
        out_shape=jax.ShapeDtypeStruct((N_CTX, vw), BF16),
        scratch_shapes=[pltpu.VMEM((SEQ, vw), F32)],
        compiler_params=pltpu.CompilerParams(dimension_semantics=("arbitrary",)),
        name="attn_ctx",
    )(q, kf, v_ctx)
    lat0 = N_CTX // DEC_SEQ
    q_tiles = DEC_SEQ // TQ_LAT
    att_lat = pl.pallas_call(
        _attn_lat_kernel,
        grid=(DEC_BATCH, q_tiles),
        in_specs=[
            pl.BlockSpec((TQ_LAT, qw), lambda b, t: (N_CTX // TQ_LAT + b * q_tiles + t, 0)),
            pl.BlockSpec((DEC_SEQ, qw), lambda b, t: (lat0 + b, 0)),
            pl.BlockSpec((DEC_SEQ, qw), lambda b, t: (b, 0)),
            pl.BlockSpec((PAST_LEN, qw), lambda b, t: (b, 0)),
            pl.BlockSpec((PAST_LEN, qw), lambda b, t: (b, 0)),
        ],
        out_specs=pl.BlockSpec((TQ_LAT, vw), lambda b, t: (b * q_tiles + t, 0)),
        out_shape=jax.ShapeDtypeStruct((N_LAT, vw), BF16),
        scratch_shapes=[pltpu.VMEM((TQ_LAT, vw), F32)],
        compiler_params=pltpu.CompilerParams(dimension_semantics=("arbitrary", "arbitrary"),
                                             vmem_limit_bytes=VMEM_LIMIT),
        name="attn_lat",
    )(q, kf, v_lat, kf_c, v_c)
    return att_ctx, att_lat


def _seq_of_step(s):
    return jnp.where(s < N_CTX_STEPS, s // CTX_STEPS_PER_SEQ,
                     BATCH + (s - N_CTX_STEPS) // LAT_STEPS_PER_SEQ)


def _step_in_seq(s):
    return jnp.where(s < N_CTX_STEPS, s % CTX_STEPS_PER_SEQ, (s - N_CTX_STEPS) % LAT_STEPS_PER_SEQ)


def _steps_in_seq(s):
    return jnp.where(s < N_CTX_STEPS, CTX_STEPS_PER_SEQ, LAT_STEPS_PER_SEQ)


def _mirror_step(s):
    return s + _steps_in_seq(s) - 1 - 2 * _step_in_seq(s)


def _split3(a):
    a1 = a.astype(BF16)
    r1 = a - a1.astype(F32)
    a2 = r1.astype(BF16)
    a3 = (r1 - a2.astype(F32)).astype(BF16)
    return a1, a2, a3


def _ssd_direction(x_ref, sm_ref, par_ref, st_ref, y_ref, lane0, backward, r0):
    rows = slice(r0, r0 + CHUNK)
    ri = lax.broadcasted_iota(I32, (CHUNK, CHUNK), 0)
    ci = lax.broadcasted_iota(I32, (CHUNK, CHUNK), 1)
    tri = (ci >= ri) if backward else (ci <= ri)
    tri_b = jnp.where(tri, 1.0, 0.0).astype(BF16)
    tot_row = 0 if backward else CHUNK - 1

    dt = jax.nn.softplus(sm_ref[rows, :] + par_ref[0:1, :])
    a = dt * (-jnp.exp(par_ref[1:2, :])) * par_ref[2:3, :]
    a1, a2, a3 = _split3(a)
    acs = (_dot(tri_b, a1) + _dot(tri_b, a2) + _dot(tri_b, a3)) * math.log2(math.e)
    acs_t = acs.T
    dt_t = dt.T
    first_head = lax.broadcasted_iota(I32, (1, LANES), 1) < SSM_HEAD_DIM

    def block_diag(pair):
        return jnp.concatenate([jnp.where(first_head, pair, 0.0), jnp.where(first_head, 0.0, pair)],
                               axis=0).astype(BF16)

    for g in range(SSM_GROUPS):
        b0 = SSM_INNER + g * SSM_STATE
        c0 = SSM_INNER + SSM_GROUPS * SSM_STATE + g * SSM_STATE
        bg = x_ref[rows, b0:b0 + SSM_STATE]
        cg = x_ref[rows, c0:c0 + SSM_STATE]
        cb = _dot_nt(cg.astype(BF16), bg.astype(BF16))
        bg_t = bg.T
        heads = SSM_HEADS // SSM_GROUPS
        for pr in range(heads // 2):
            h0 = g * heads + 2 * pr
            sl = slice(h0 * SSM_HEAD_DIM, (h0 + 2) * SSM_HEAD_DIM)
            x_bd = block_diag(x_ref[rows, sl])
            st_old = st_ref[:, sl]
            within, carried, to_state, keep = [], [], [], []
            for h in (h0, h0 + 1):
                lane = lane0 + h
                col = jnp.broadcast_to(acs[:, lane:lane + 1], (CHUNK, CHUNK))
                row = acs_t[lane:lane + 1, :]
                dt_row = dt_t[lane:lane + 1, :]
                decay = jnp.exp2(jnp.where(tri, col - row, -jnp.inf))
                within.append((cb * decay * dt_row).astype(BF16))
                carried.append((cg * jnp.exp2(col)).astype(BF16))
                tot = acs[tot_row:tot_row + 1, lane:lane + 1]
                to_state.append((bg_t * (dt_row * jnp.exp2(tot - row))).astype(BF16))
                keep.append(jnp.exp2(tot))
            lhs = jnp.concatenate(within + carried, axis=1)
            y_ref[rows, sl] = _dot(lhs, jnp.concatenate([x_bd, block_diag(st_old)], axis=0))
            st_ref[:, sl] = (st_old * jnp.where(first_head, keep[0], keep[1])
                             + _dot(jnp.concatenate(to_state, axis=1), x_bd))


def _ssd_kernel(layer, *refs):
    n_in = 7 + (2 if layer > 0 else 0)
    xf_ref, xb_ref, smf_ref, smb_ref, if_ref, ib_ref, par_ref = refs[:7]
    yf_ref, yb_ref, sf_ref, sb_ref, stf_ref, stb_ref = refs[n_in:]
    s = pl.program_id(0)
    c = _step_in_seq(s)

    @pl.when(jnp.logical_and(c == 0, s < N_CTX_STEPS))
    def _():
        stf_ref[...] = jnp.zeros_like(stf_ref)
        stb_ref[...] = jnp.zeros_like(stb_ref)

    @pl.when(jnp.logical_and(c == 0, s >= N_CTX_STEPS))
    def _():
        stf_ref[...] = if_ref[0].T
        stb_ref[...] = ib_ref[0].T

    for r0 in range(0, SSD_STEP, CHUNK):
        _ssd_direction(xf_ref, smf_ref, par_ref, stf_ref, yf_ref, SM_DTF, False, r0)
        _ssd_direction(xb_ref, smb_ref, par_ref, stb_ref, yb_ref, SM_DTB, True, SSD_STEP - CHUNK - r0)

    @pl.when(jnp.logical_and(c == _steps_in_seq(s) - 1, s < N_CTX_STEPS))
    def _():
        if layer == 0:
            _first_layer_slabs(sf_ref, stf_ref[...].T)
            _first_layer_slabs(sb_ref, stb_ref[...].T)
        else:
            sf_ref[0] = stf_ref[...].T
            sb_ref[0] = stb_ref[...].T


def _ssd(layer, xact, small, init_f, init_b, par, states):
    hp = SSM_INNER
    fwd = lambda n: pl.BlockSpec((SSD_STEP, n), lambda s: (s, 0))
    bwd = lambda n: pl.BlockSpec((SSD_STEP, n), lambda s: (_mirror_step(s), 0))
    st = pl.BlockSpec((1, DEPTH, hp, SSM_STATE) if layer == 0 else (1, None, hp, SSM_STATE),
                      lambda s: (jnp.minimum(_seq_of_step(s), BATCH - 1), 0 if layer == 0 else layer, 0, 0))
    init = pl.BlockSpec((1, hp, SSM_STATE), lambda s: (jnp.maximum(_seq_of_step(s) - BATCH, 0), 0, 0))
    extra = {} if layer == 0 else dict(input_output_aliases={7: 2, 8: 3})
    return pl.pallas_call(
        functools.partial(_ssd_kernel, layer),
        grid=(N_SSD_STEPS,),
        in_specs=[fwd(SSM_CONV_CH), bwd(SSM_CONV_CH), fwd(LANES), bwd(LANES), init, init,
                  pl.BlockSpec((SUBLANES, LANES), lambda s: (0, 0))] + [pl.BlockSpec(memory_space=pl.ANY)] * len(states),
        out_specs=(fwd(hp), bwd(hp), st, st),
        out_shape=(jax.ShapeDtypeStruct((N_TOK, hp), F32), jax.ShapeDtypeStruct((N_TOK, hp), F32),
                   jax.ShapeDtypeStruct((BATCH, DEPTH, hp, SSM_STATE), F32),
                   jax.ShapeDtypeStruct((BATCH, DEPTH, hp, SSM_STATE), F32)),
        scratch_shapes=[pltpu.VMEM((SSM_STATE, hp), F32), pltpu.VMEM((SSM_STATE, hp), F32)],
        compiler_params=pltpu.CompilerParams(dimension_semantics=("arbitrary",)),
        name="ssd",
        **extra,
    )(xact, xact, small, small, init_f, init_b, par, *states)


def _merge_kernel(ta_ref, attc_ref, attl_ref, yf_ref, yb_ref, xs_ref, z_ref, g_ref, xc_ref, xl_ref, mod_ref,
                  woa_ref, wom_ref, dsk_ref, snw_ref, wos_ref, wo_ref, n2w_ref, rw_ref, rb_ref,
                  x1_ref, h2_ref, ti_ref, tg_ref, cnt_ref):
    mod = mod_ref[0]
    gate1 = mod[:, 2 * D_MODEL:3 * D_MODEL]
    shift2 = mod[:, 3 * D_MODEL:4 * D_MODEL]
    scale2 = mod[:, 4 * D_MODEL:5 * D_MODEL]
    y_a = _dot(ta_ref[...], woa_ref[...])
    att = jnp.where(pl.program_id(0) < CTX_TILES, attc_ref[...].astype(F32), attl_ref[...].astype(F32))
    y_b = _dot(att.astype(BF16), wom_ref[...])
    y_ssm = (yf_ref[...] + yb_ref[...] + dsk_ref[...] * xs_ref[...]) * _silu(z_ref[...])
    y_c = _dot(_rms(y_ssm, snw_ref[...]).astype(BF16), wos_ref[...])
    merged = (g_ref[:, 0:D_MODEL] * y_a + g_ref[:, D_MODEL:2 * D_MODEL] * y_b
              + g_ref[:, 2 * D_MODEL:3 * D_MODEL] * y_c)
    x = jnp.where(pl.program_id(0) < CTX_TILES, xc_ref[...], xl_ref[...])
    x1 = x + gate1 * _dot(merged.astype(BF16), wo_ref[...])
    x1_ref[...] = x1
    h2 = _rms(x1, n2w_ref[...]) * (1.0 + scale2) + shift2
    h2b = h2.astype(BF16)
    h2_ref[...] = h2b

    logits = (_dot(h2b, rw_ref[...]) + rb_ref[...]).T[0:N_EXPERTS, :]
    expert = lax.broadcasted_iota(I32, (N_EXPERTS, TM), 0)
    ids, vals = [], []
    for k in range(TOP_K):
        m = jnp.max(logits, axis=0, keepdims=True)
        idx = jnp.min(jnp.where(logits == m, expert, N_EXPERTS), axis=0, keepdims=True)
        ids.append(idx)
        vals.append(m)
        logits = jnp.where(expert == idx, -jnp.inf, logits)
    es = [jnp.exp(v - vals[0]) for v in vals]
    denom = functools.reduce(jnp.add, es)
    srow = lax.broadcasted_iota(I32, (LANES, TM), 0)
    ti_t = jnp.zeros((LANES, TM), F32)
    tg_t = jnp.zeros((LANES, TM), F32)
    chosen_t = jnp.zeros((LANES, TM), F32)
    for k in range(TOP_K):
        ti_t = jnp.where(srow == k, ids[k].astype(F32), ti_t)
        tg_t = jnp.where(srow == k, es[k] / denom, tg_t)
        chosen_t = jnp.where(srow == ids[k], 1.0, chosen_t)
    ti_ref[...] = ti_t.T.astype(I32)
    tg_ref[...] = tg_t.T
    cnt_ref[0] = _dot_nt(jnp.ones((SUBLANES, TM), BF16), chosen_t.astype(BF16)).astype(I32)


def _merge(ta, att_ctx, att_lat, yf, yb, xact, z, g, xc, xl, mod3, lw):
    row = lambda n: pl.BlockSpec((TM, n), lambda i: (i, 0))
    vw = MLA_HEADS * V_HEAD
    out_shape = (jax.ShapeDtypeStruct((N_TOK, D_MODEL), F32), jax.ShapeDtypeStruct((N_TOK, D_MODEL), BF16),
                 jax.ShapeDtypeStruct((N_TOK, LANES), I32), jax.ShapeDtypeStruct((N_TOK, LANES), F32))
    cnt_shape = jax.ShapeDtypeStruct((N_TILES, SUBLANES, LANES), I32)
    cnt_spec = pl.BlockSpec((1, SUBLANES, LANES), lambda i: (i, 0, 0))
    return pl.pallas_call(
        _merge_kernel,
        grid=(N_TILES,),
        in_specs=[
            row(CONV_DIM),
            pl.BlockSpec((TM, vw), lambda i: (jnp.minimum(i, CTX_TILES - 1), 0)),
            pl.BlockSpec((TM, vw), lambda i: (jnp.maximum(i - CTX_TILES, 0), 0)),
            row(SSM_INNER), row(SSM_INNER), row(SSM_INNER),
            row(SSM_INNER), row(3 * D_MODEL),
            pl.BlockSpec((TM, D_MODEL), lambda i: (jnp.minimum(i, CTX_TILES - 1), 0)),
            pl.BlockSpec((TM, D_MODEL), lambda i: (jnp.maximum(i - CTX_TILES, 0), 0)),
            pl.BlockSpec((1, 1, 6 * D_MODEL), lambda i: (_mod_row(i), 0, 0)),
            _resident((CONV_DIM, D_MODEL)), _resident((MLA_HEADS * V_HEAD, D_MODEL)),
            _resident((1, SSM_INNER)), _resident((1, SSM_INNER)), _resident((SSM_INNER, D_MODEL)),
            _resident((D_MODEL, D_MODEL)), _resident((1, D_MODEL)),
            _resident((D_MODEL, LANES)), _resident((1, LANES)),
        ],
        out_specs=tuple(row(s.shape[1]) for s in out_shape) + (cnt_spec,),
        out_shape=out_shape + (cnt_shape,),
        compiler_params=pltpu.CompilerParams(dimension_semantics=("arbitrary",), vmem_limit_bytes=VMEM_LIMIT),
        name="merge",
    )(ta, att_ctx, att_lat, yf, yb, xact, z, g, xc, xl, mod3, lw["w_out_a"], lw["w_o_mla"], lw["d_skip"],
      lw["ssm_norm_w"],
      lw["w_o_ssm"], lw["w_o"], lw["norm2_w"], lw["router_w"], lw["router_b"])


def _run_copies(cnt_ref, src_ref, dst_ref, first, count, bits, make_copy, start):
    def body(e, carry):
        n = cnt_ref[first + e]
        s0 = src_ref[first + e] if src_ref is not None else 0
        d0 = dst_ref[first + e]

        def pieces(some_bits):
            for b in some_bits:
                above = (n >> (b + 1)) << (b + 1)

                @pl.when(((n >> b) & 1) == 1)
                def _():
                    cp = make_copy(pl.multiple_of(s0 + above, RUN_ALIGN), pl.multiple_of(d0 + above, RUN_ALIGN),
                                   1 << b)
                    if start:
                        cp.start()
                    else:
                        cp.wait()

        large = [b for b in bits if b >= COMMON_BITS]

        @pl.when(n >= (1 << COMMON_BITS))
        def _():
            pieces(large)
        pieces([b for b in bits if b < COMMON_BITS])
        return carry
    lax.fori_loop(0, count, body, 0)


def _wait_rows(total, make_copy):
    for b in WAIT_BITS:
        @pl.when(((total >> b) & 1) == 1)
        def _():
            make_copy(0, 0, 1 << b).wait()


def _dispatch_kernel(cnt_ref, off_ref, run_ref, tcnt_ref, tdst_ref, nu_ref, h2_ref, ti_ref, tg_ref, offv_ref,
                     xs_ref, lp_ref, stage, zeros, perm_s, ghi_s, glo_s, sem, semz):
    i = pl.program_id(0)
    slot = i % 2

    def copy_out(s_):
        return lambda s, d, n: pltpu.make_async_copy(stage.at[s_, pl.ds(s, n), :], xs_ref.at[pl.ds(d, n), :],
                                                     sem.at[s_])

    def tile_rows(t):
        last = t * N_EXPERTS + N_EXPERTS - 1
        return off_ref[last] + cnt_ref[last]

    @pl.when(i == 0)
    def _():
        zeros[...] = jnp.zeros_like(zeros)
        zero_out = lambda s, d, n: pltpu.make_async_copy(zeros.at[pl.ds(0, n), :], xs_ref.at[pl.ds(d, n), :], semz)
        _run_copies(tcnt_ref, None, tdst_ref, 0, N_EXPERTS, TAIL_BITS, zero_out, True)
        _run_copies(tcnt_ref, None, tdst_ref, 0, N_EXPERTS, TAIL_BITS, zero_out, False)
        zrows = zeros.shape[0]

        def unused_blocks(start):
            def body(b, carry):
                for part in range(MOE_BLK // zrows):
                    cp = zero_out(0, pl.multiple_of(b * MOE_BLK + part * zrows, RUN_ALIGN), zrows)
                    if start:
                        cp.start()
                    else:
                        cp.wait()
                return carry
            lax.fori_loop(nu_ref[0], N_BLOCKS, body, 0)
        unused_blocks(True)
        unused_blocks(False)

    @pl.when(i >= 2)
    def _():
        _wait_rows(tile_rows(i - 2), copy_out(slot))

    lane = lax.broadcasted_iota(I32, (TM, LANES), 1)
    picks = [jnp.where(lane == ti_ref[:, k:k + 1], 1.0, 0.0) for k in range(TOP_K)]
    ri = lax.broadcasted_iota(I32, (TM, TM), 0)
    ci = lax.broadcasted_iota(I32, (TM, TM), 1)
    earlier = jnp.where(ci < ri, 1.0, 0.0).astype(BF16)
    base = _dot(earlier, functools.reduce(jnp.add, picks).astype(BF16)) + offv_ref[0][0:1, :]
    lp = jnp.full((TM, LANES), -1.0, F32)
    for k in range(TOP_K):
        lp = jnp.where(lane == k, jnp.sum(picks[k] * base, axis=-1, keepdims=True), lp)
    lp_ref[...] = lp.astype(I32)

    lp_t = lp.T
    tg_t = tg_ref[...].T
    for c in range(STAGE_ROWS // LANES):
        rows = slice(c * LANES, (c + 1) * LANES)
        row = (lax.broadcasted_iota(I32, (LANES, TM), 0) + c * LANES).astype(F32)
        perm = jnp.zeros((LANES, TM), F32)
        gates = jnp.zeros((LANES, TM), F32)
        for k in range(TOP_K):
            hit = row == lp_t[k:k + 1, :]
            perm = jnp.where(hit, 1.0, perm)
            gates = jnp.where(hit, tg_t[k:k + 1, :], gates)
        perm_s[rows, :] = perm.astype(BF16)
        g_hi = gates.astype(BF16)
        ghi_s[rows, :] = g_hi
        glo_s[rows, :] = (gates - g_hi.astype(F32)).astype(BF16)
    stage[slot, :, 0:D_MODEL] = _dot(perm_s[...], h2_ref[...])
    ones = jnp.ones((TM, LANES), BF16)
    stage[slot, :, D_MODEL:XS_COLS] = _dot(ghi_s[...], ones) + _dot(glo_s[...], ones)

    _run_copies(cnt_ref, off_ref, run_ref, i * N_EXPERTS, N_EXPERTS, RUN_BITS, copy_out(slot), True)

    @pl.when(i == N_TILES - 1)
    def _():
        _wait_rows(tile_rows(i - 1), copy_out(1 - slot))
        _wait_rows(tile_rows(i), copy_out(slot))


def _dispatch(rt, h2, top_i, top_g):
    row = lambda n: pl.BlockSpec((TM, n), lambda i, *_: (i, 0))
    return pl.pallas_call(
        _dispatch_kernel,
        grid_spec=pltpu.PrefetchScalarGridSpec(
            num_scalar_prefetch=6,
            grid=(N_TILES,),
            in_specs=[row(D_MODEL), row(LANES), row(LANES),
                      pl.BlockSpec((1, SUBLANES, LANES), lambda i, *_: (i, 0, 0))],
            out_specs=(pl.BlockSpec(memory_space=pl.ANY), row(LANES)),
            scratch_shapes=[
                pltpu.VMEM((2, STAGE_ROWS, XS_COLS), F32),
                pltpu.VMEM((1 << TAIL_BITS[0], XS_COLS), F32),
                pltpu.VMEM((STAGE_ROWS, TM), BF16), pltpu.VMEM((STAGE_ROWS, TM), BF16),
                pltpu.VMEM((STAGE_ROWS, TM), BF16),
                pltpu.SemaphoreType.DMA((2,)),
                pltpu.SemaphoreType.DMA,
            ],
        ),
        out_shape=(jax.ShapeDtypeStruct((N_SLOTS, XS_COLS), F32), jax.ShapeDtypeStruct((N_TOK, LANES), I32)),
        compiler_params=pltpu.CompilerParams(dimension_semantics=("arbitrary",), vmem_limit_bytes=VMEM_LIMIT),
        name="dispatch",
    )(rt["cnt"], rt["off"], rt["run"], rt["tail_cnt"], rt["tail_dst"], rt["n_used"], h2, top_i, top_g, rt["off_v"])


def _moe_kernel(layer, be_ref, nxt_ref, nv_ref, nu_ref, x_ref, wgu_hbm, wdn_hbm, bg_ref, bu_ref, bd_ref, sel_ref,
                y_ref, wgu_ref, wdn_ref, wgu_s, wdn_s, sem):
    i = pl.program_id(0)
    n_used = nu_ref[0]

    def fetch(e):
        return (pltpu.make_async_copy(wgu_hbm.at[layer, e], wgu_ref, sem.at[0]),
                pltpu.make_async_copy(wdn_hbm.at[layer, e], wdn_ref, sem.at[1]))

    @pl.when(i < n_used)
    def _():
        @pl.when(jnp.logical_or(i == 0, be_ref[i] != be_ref[jnp.maximum(i - 1, 0)]))
        def _():
            @pl.when(i == 0)
            def _():
                for cp in fetch(be_ref[0]):
                    cp.start()
            for cp in fetch(be_ref[i]):
                cp.wait()
            half = LANES
            for c in range(2 * EXPERT_FF // (2 * half)):
                r = _dot(wgu_ref[:, c * 2 * half:(c + 1) * 2 * half].astype(BF16), sel_ref[...])
                wgu_s[:, c * half:(c + 1) * half] = r[:, 0:half].astype(BF16)
                wgu_s[:, EXPERT_FF + c * half:EXPERT_FF + (c + 1) * half] = r[:, half:2 * half].astype(BF16)
            wdn_s[...] = wdn_ref[...].astype(BF16)
            nxt = nxt_ref[i]

            @pl.when(nxt >= 0)
            def _():
                for cp in fetch(nxt):
                    cp.start(priority=1)

        def expert_rows(rows):
            gu = _dot(x_ref[0:rows, 0:D_MODEL].astype(BF16), wgu_s[...])
            gate = jnp.minimum(gu[:, 0:EXPERT_FF] + bg_ref[0], SWIGLU_LIMIT)
            up = jnp.clip(gu[:, EXPERT_FF:2 * EXPERT_FF] + bu_ref[0], -SWIGLU_LIMIT, SWIGLU_LIMIT)
            act = gate * jax.nn.sigmoid(SWIGLU_ALPHA * gate) * (up + 1.0)
            y = _dot(act.astype(BF16), wdn_s[...]) + bd_ref[0]
            slot_gate = x_ref[0:rows, D_MODEL:XS_COLS]
            for j in range(D_MODEL // LANES):
                y_ref[0:rows, j * LANES:(j + 1) * LANES] = y[:, j * LANES:(j + 1) * LANES] * slot_gate

        half = MOE_BLK // 2

        @pl.when(nv_ref[i] > half)
        def _():
            expert_rows(MOE_BLK)

        @pl.when(nv_ref[i] <= half)
        def _():
            expert_rows(half)
            y_ref[half:MOE_BLK, :] = jnp.zeros((MOE_BLK - half, D_MODEL), F32)

    @pl.when(i >= n_used)
    def _():
        y_ref[...] = jnp.zeros_like(y_ref)


def _bias_split_kernel(b_ref, sel_ref, o_ref):
    for c in range(2 * EXPERT_FF // (2 * LANES)):
        terms = _split3(b_ref[:, c * 2 * LANES:(c + 1) * 2 * LANES])
        r = functools.reduce(jnp.add, [_dot(t, sel_ref[...]) for t in terms])
        o_ref[:, c * LANES:(c + 1) * LANES] = r[:, 0:LANES]
        o_ref[:, EXPERT_FF + c * LANES:EXPERT_FF + (c + 1) * LANES] = r[:, LANES:2 * LANES]


def _bias_split(b_gu, sel):
    n = DEPTH * N_EXPERTS
    out = pl.pallas_call(
        _bias_split_kernel,
        out_shape=jax.ShapeDtypeStruct((n, 2 * EXPERT_FF), F32),
        name="bias_split",
    )(b_gu.reshape(n, 2 * EXPERT_FF), sel)
    return out.reshape(n, 1, 2 * EXPERT_FF)


def _moe(layer, rt, xs, w_gu, w_down, b_gu_split, b_down, sel):
    first = layer * N_EXPERTS
    bg = pl.BlockSpec((1, 1, EXPERT_FF), lambda i, be, *_: (first + be[i], 0, 0))
    bu = pl.BlockSpec((1, 1, EXPERT_FF), lambda i, be, *_: (first + be[i], 0, 1))
    bd = pl.BlockSpec((1, 1, D_MODEL), lambda i, be, *_: (first + be[i], 0, 0))
    return pl.pallas_call(
        functools.partial(_moe_kernel, layer),
        grid_spec=pltpu.PrefetchScalarGridSpec(
            num_scalar_prefetch=4,
            grid=(N_BLOCKS,),
            in_specs=[
                pl.BlockSpec((MOE_BLK, XS_COLS), lambda i, be, nx, nv, nu: (jnp.minimum(i, nu[0] - 1), 0)),
                pl.BlockSpec(memory_space=pl.ANY),
                pl.BlockSpec(memory_space=pl.ANY),
                bg, bu, bd,
                pl.BlockSpec((2 * LANES, 2 * LANES), lambda i, *_: (0, 0)),
            ],
            out_specs=pl.BlockSpec((MOE_BLK, D_MODEL), lambda i, *_: (i, 0)),
            scratch_shapes=[
                pltpu.VMEM((D_MODEL, 2 * EXPERT_FF), F32),
                pltpu.VMEM((EXPERT_FF, D_MODEL), F32),
                pltpu.VMEM((D_MODEL, 2 * EXPERT_FF), BF16),
                pltpu.VMEM((EXPERT_FF, D_MODEL), BF16),
                pltpu.SemaphoreType.DMA((2,)),
            ],
        ),
        out_shape=jax.ShapeDtypeStruct((N_SLOTS, D_MODEL), F32),
        compiler_params=pltpu.CompilerParams(dimension_semantics=("arbitrary",), vmem_limit_bytes=VMEM_LIMIT),
        name="moe",
    )(rt["blk_e"], rt["blk_next"], rt["blk_rows"], rt["n_used"], xs, w_gu, w_down, b_gu_split, b_gu_split,
      b_down.reshape(DEPTH * N_EXPERTS, 1, D_MODEL), sel)


def _combine_kernel(cnt_ref, off_ref, run_ref, y_ref, lp_ref, x1_ref, mod_ref, fw_ref,
                    xc_ref, xl_ref, yc_ref, yl_ref, stage, mine_s, sem):
    i = pl.program_id(0)
    slot = i % 2

    def copy_in(s_):
        return lambda s, d, n: pltpu.make_async_copy(y_ref.at[pl.ds(d, n), :], stage.at[s_, pl.ds(s, n), :],
                                                     sem.at[s_])

    @pl.when(i == 0)
    def _():
        stage[...] = jnp.zeros_like(stage)
        _run_copies(cnt_ref, off_ref, run_ref, 0, N_EXPERTS, RUN_BITS, copy_in(0), True)

    @pl.when(i + 1 < N_TILES)
    def _():
        _run_copies(cnt_ref, off_ref, run_ref, (i + 1) * N_EXPERTS, N_EXPERTS, RUN_BITS, copy_in(1 - slot), True)

    last = i * N_EXPERTS + N_EXPERTS - 1
    _wait_rows(off_ref[last] + cnt_ref[last], copy_in(slot))
    lane = lax.broadcasted_iota(I32, (TM, LANES), 1)
    mine = [jnp.broadcast_to(lp_ref[:, k:k + 1], (TM, LANES)) for k in range(TOP_K)]
    for c in range(STAGE_ROWS // LANES):
        hit = jnp.zeros((TM, LANES), F32)
        for k in range(TOP_K):
            hit = jnp.where(lane + c * LANES == mine[k], 1.0, hit)
        mine_s[:, c * LANES:(c + 1) * LANES] = hit.astype(BF16)
    moe = _dot(mine_s[...], stage[slot].astype(BF16))
    gate2 = mod_ref[0][:, 5 * D_MODEL:6 * D_MODEL]
    x2 = x1_ref[...] + gate2 * moe
    y_norm = _rms(x2, fw_ref[...])

    @pl.when(i < CTX_TILES)
    def _():
        xc_ref[...] = x2
        yc_ref[...] = y_norm

    @pl.when(i >= CTX_TILES)
    def _():
        xl_ref[...] = x2
        yl_ref[...] = y_norm


def _combine(rt, y_slots, lp, x1, mod3, final_w):
    row = lambda n: pl.BlockSpec((TM, n), lambda i, *_: (i, 0))
    ctx_row = pl.BlockSpec((TM, D_MODEL), lambda i, *_: (jnp.minimum(i, CTX_TILES - 1), 0))
    lat_row = pl.BlockSpec((TM, D_MODEL), lambda i, *_: (jnp.maximum(i - CTX_TILES, 0), 0))
    return pl.pallas_call(
        _combine_kernel,
        grid_spec=pltpu.PrefetchScalarGridSpec(
            num_scalar_prefetch=3,
            grid=(N_TILES,),
            in_specs=[
                pl.BlockSpec(memory_space=pl.ANY),
                row(LANES), row(D_MODEL),
                pl.BlockSpec((1, 1, 6 * D_MODEL), lambda i, *_: (_mod_row(i), 0, 0)),
                pl.BlockSpec((1, D_MODEL), lambda i, *_: (0, 0)),
            ],
            out_specs=(ctx_row, lat_row, ctx_row, lat_row),
            scratch_shapes=[pltpu.VMEM((2, STAGE_ROWS, D_MODEL), F32), pltpu.VMEM((TM, STAGE_ROWS), BF16),
                            pltpu.SemaphoreType.DMA((2,))],
        ),
        out_shape=(jax.ShapeDtypeStruct((N_CTX, D_MODEL), F32), jax.ShapeDtypeStruct((N_LAT, D_MODEL), F32),
                   jax.ShapeDtypeStruct((N_CTX, D_MODEL), F32), jax.ShapeDtypeStruct((N_LAT, D_MODEL), F32)),
        compiler_params=pltpu.CompilerParams(dimension_semantics=("arbitrary",), vmem_limit_bytes=VMEM_LIMIT),
        name="combine",
    )(rt["cnt"], rt["off"], rt["run"], y_slots, lp, x1, mod3, final_w)


def _rope_tables():
    rows = DEC_SEQ // GRID_W
    t = jnp.arange(rows * GRID_W)
    row = (t // GRID_W).astype(F32)
    col = (t % GRID_W).astype(F32)
    half = QK_ROPE // 2
    inv = ROPE_BASE ** (-jnp.arange(0, half, 2, dtype=F32) / half)
    ang_r, ang_c = row[:, None] * inv, col[:, None] * inv
    cos32 = jnp.concatenate([jnp.cos(ang_r), jnp.cos(ang_r), jnp.cos(ang_c), jnp.cos(ang_c)], axis=-1)
    sin32 = jnp.concatenate([-jnp.sin(ang_r), jnp.sin(ang_r), -jnp.sin(ang_c), jnp.sin(ang_c)], axis=-1)
    cos32 = jnp.concatenate([jnp.ones((TM, QK_ROPE), F32), cos32], axis=0)
    sin32 = jnp.concatenate([jnp.zeros((TM, QK_ROPE), F32), sin32], axis=0)
    n = cos32.shape[0]
    pad = HEAD_PAD - QK_NOPE - QK_ROPE
    cos_h = jnp.concatenate([jnp.ones((n, QK_NOPE), F32), cos32, jnp.zeros((n, pad), F32)], axis=-1)
    sin_h = jnp.concatenate([jnp.zeros((n, QK_NOPE), F32), sin32, jnp.zeros((n, pad), F32)], axis=-1)
    zeros = jnp.zeros((n, LANES - QK_ROPE), F32)
    return {
        "cosq": cos_h, "sinq": sin_h,
        "cosk": jnp.concatenate([cos32, zeros], axis=-1), "sink": jnp.concatenate([sin32, zeros], axis=-1),
        "vone": jnp.zeros((MLA_HEADS, HEAD_PAD), F32).at[:, V_HEAD].set(1.0).reshape(1, MLA_HEADS * HEAD_PAD),
    }


def _layer_weights(p, l):
    hd = QK_NOPE + QK_ROPE
    pad = HEAD_PAD - hd
    wq = p["w_uq"][l].reshape(Q_RANK, MLA_HEADS, hd)
    wq_a = jnp.pad(wq, ((0, 0), (0, 0), (0, pad)))
    wuk = jnp.pad(p["w_uk"][l], ((0, 0), (0, 0), (0, HEAD_PAD - QK_NOPE)))
    rw = jnp.pad(p["router_w"][l], ((0, 0), (0, LANES - N_EXPERTS)))
    rb = jnp.concatenate([p["router_b"][l], jnp.full((LANES - N_EXPERTS,), -jnp.inf, F32)])
    return {
        "norm1_w": p["norm1_w"][l][None], "q_norm_w": p["q_norm_w"][l][None],
        "wq_a": wq_a.reshape(Q_RANK, -1).astype(BF16),
        "kv_norm_w": p["kv_norm_w"][l][None],
        "wuk": wuk.reshape(KV_RANK, -1).astype(BF16),
        "wuv": jnp.pad(p["w_uv"][l], ((0, 0), (0, 0), (0, HEAD_PAD - V_HEAD))).reshape(KV_RANK, -1).astype(BF16),
        "wuv_c": p["w_uv"][l].reshape(KV_RANK, -1).astype(BF16),
        "conv_a_w": p["conv_a_w"][l], "ssm_conv_w": p["ssm_conv_w"][l], "ssm_conv_b": p["ssm_conv_b"][l][None],
        "w_out_a": p["w_out_a"][l].astype(BF16), "w_o_mla": p["w_o_mla"][l].astype(BF16),
        "d_skip": jnp.repeat(p["d_skip"][l], SSM_HEAD_DIM)[None], "ssm_norm_w": p["ssm_norm_w"][l][None],
        "w_o_ssm": p["w_o_ssm"][l].astype(BF16), "w_o": p["w_o"][l].astype(BF16),
        "norm2_w": p["norm2_w"][l][None], "router_w": rw.astype(BF16), "router_b": rb[None],
    }


def _ssd_params(p, l):
    z = lambda n: jnp.zeros((n,), F32)
    lanes = lambda f, b: jnp.concatenate([z(SM_DTF), f, b, z(LANES - SM_DTB - SSM_HEADS)])
    ones = jnp.ones((SSM_HEADS,), F32)
    rows = [lanes(p["dt_bias_fwd"][l], p["dt_bias_bwd"][l]), lanes(p["a_log_fwd"][l], p["a_log_bwd"][l]),
            lanes(ones, ones)]
    return jnp.concatenate([jnp.stack(rows), jnp.zeros((SUBLANES - 3, LANES), F32)], axis=0)


def _routing(cnt_tiles):
    cnt = cnt_tiles[:, 0, 0:N_EXPERTS]
    cnt = (cnt + RUN_ALIGN - 1) // RUN_ALIGN * RUN_ALIGN
    per_expert = jnp.sum(cnt, axis=0)
    padded = (per_expert + MOE_BLK - 1) // MOE_BLK * MOE_BLK
    pad_end = jnp.cumsum(padded)
    pad_start = pad_end - padded
    run = pad_start[None, :] + jnp.cumsum(cnt, axis=0) - cnt
    off = jnp.cumsum(cnt, axis=1) - cnt
    starts = jnp.arange(N_BLOCKS, dtype=I32) * MOE_BLK
    blk_e = jnp.minimum(jnp.sum((pad_end[None, :] <= starts[:, None]).astype(I32), axis=1), N_EXPERTS - 1)
    off_v = jnp.zeros((N_TILES, SUBLANES, LANES), F32).at[:, 0, 0:N_EXPERTS].set(off.astype(F32))
    ids = jnp.arange(N_EXPERTS, dtype=I32)
    later = jnp.logical_and(ids[None, :] > ids[:, None], padded[None, :] > 0)
    nxt = jnp.min(jnp.where(later, ids[None, :], N_EXPERTS), axis=1)
    nxt = jnp.where(nxt == N_EXPERTS, -1, nxt)
    per_block = lambda v: jnp.sum(jnp.where(blk_e[:, None] == ids[None, :], v[None, :], 0), axis=1)
    blk_rows = jnp.clip(per_block(pad_start + per_expert) - starts, 0, MOE_BLK)
    return {
        "blk_next": per_block(nxt).astype(I32), "blk_rows": blk_rows.astype(I32),
        "cnt": cnt.reshape(-1).astype(I32), "off": off.reshape(-1).astype(I32), "run": run.reshape(-1).astype(I32),
        "tail_cnt": (padded - per_expert).astype(I32), "tail_dst": (pad_start + per_expert).astype(I32),
        "blk_e": blk_e.astype(I32), "n_used": (pad_end[-1] // MOE_BLK).astype(I32).reshape(1), "off_v": off_v,
    }


def _deinterleave_matrix():
    k = jnp.arange(2 * LANES)[:, None]
    n = jnp.arange(2 * LANES)[None, :]
    src = jnp.where(n < LANES, 2 * n, 2 * (n - LANES) + 1)
    return (k == src).astype(BF16)


def kernel(x_prompt, x_sample, cache_ckv, cache_kpe, state_ssm_fwd, state_ssm_bwd, c, c_ctx, w_ada, b_ada, norm1_w, w_in, conv_a_w, w_out_a, q_norm_w, w_uq, kv_norm_w, w_uk, w_uv, w_o_mla, ssm_conv_w, ssm_conv_b, dt_bias_fwd, dt_bias_bwd, a_log_fwd, a_log_bwd, d_skip, ssm_norm_w, w_o_ssm, w_o, norm2_w, router_w, router_b, w_gu, b_gu, w_down, b_down, final_norm_w):
    p = dict(norm1_w=norm1_w, w_in=w_in, conv_a_w=conv_a_w, w_out_a=w_out_a, q_norm_w=q_norm_w, w_uq=w_uq,
             kv_norm_w=kv_norm_w, w_uk=w_uk, w_uv=w_uv, w_o_mla=w_o_mla, ssm_conv_w=ssm_conv_w,
             ssm_conv_b=ssm_conv_b, dt_bias_fwd=dt_bias_fwd, dt_bias_bwd=dt_bias_bwd, a_log_fwd=a_log_fwd,
             a_log_bwd=a_log_bwd, d_skip=d_skip, ssm_norm_w=ssm_norm_w, w_o_ssm=w_o_ssm, w_o=w_o,
             norm2_w=norm2_w, router_w=router_w, router_b=router_b, b_gu=b_gu, b_down=b_down)
    xc = x_prompt.reshape(N_CTX, D_MODEL)
    xl = x_sample.reshape(N_LAT, D_MODEL)
    cond8 = jnp.concatenate([c_ctx[None], c, jnp.zeros((SUBLANES - 1 - DEC_BATCH, D_MODEL), F32)], axis=0)
    mods = _ada_mods(cond8, w_ada, b_ada)
    tabs = _rope_tables()
    sel = _deinterleave_matrix()
    b_gu_split = _bias_split(b_gu, sel)
    w_in_t = jnp.swapaxes(w_in, 1, 2)
    final_w = final_norm_w[None]
    hp = SSM_INNER

    caches, states = (), ()
    y_ctx = y_lat = None
    for l in range(DEPTH):
        lw = _layer_weights(p, l)
        mod3 = mods[l].reshape(SUBLANES, 1, 6 * D_MODEL)
        ta, q, ckv, kpe, kf, v_ctx, v_lat, small, z, xact, g = _inproj(l, xc, xl, mod3, w_in_t, lw, tabs, caches)
        caches = (ckv, kpe)
        kpe_c = jnp.pad(cache_kpe[:, l].reshape(DEC_BATCH * PAST_LEN, QK_ROPE), ((0, 0), (0, LANES - QK_ROPE)))
        kf_c, v_c = _kvcache(cache_ckv[:, l].reshape(DEC_BATCH * PAST_LEN, KV_RANK), kpe_c, lw, tabs)
        att_ctx, att_lat = _attention(q, kf, v_ctx, v_lat, kf_c, v_c)
        init_f = state_ssm_fwd[:, l].reshape(DEC_BATCH, hp, SSM_STATE)
        init_b = state_ssm_bwd[:, l].reshape(DEC_BATCH, hp, SSM_STATE)
        yf, yb, sf, sb = _ssd(l, xact, small, init_f, init_b, _ssd_params(p, l), states)
        states = (sf, sb)
        x1, h2, top_i, top_g, cnt_tiles = _merge(ta, att_ctx, att_lat, yf, yb, xact, z, g, xc, xl, mod3, lw)
        rt = _routing(cnt_tiles)
        xs, lp = _dispatch(rt, h2, top_i, top_g)
        y_slots = _moe(l, rt, xs, w_gu, w_down, b_gu_split, b_down, sel)
        xc, xl, y_ctx, y_lat = _combine(rt, y_slots, lp, x1, mod3, final_w)

    y_prompt = y_ctx.reshape(BATCH, SEQ, D_MODEL)
    y_sample = y_lat.reshape(DEC_BATCH, DEC_SEQ, D_MODEL)
    state_shape = (BATCH, DEPTH, SSM_HEADS, SSM_HEAD_DIM, SSM_STATE)
    return (y_prompt, y_sample, caches[0], caches[1], states[0].reshape(state_shape), states[1].reshape(state_shape))
```

```python
import functools
import math

import jax
import jax.numpy as jnp
from jax import lax
from jax.experimental import pallas as pl
from jax.experimental.pallas import tpu as pltpu

F32 = jnp.float32
BF16 = jnp.bfloat16
I32 = jnp.int32

D_MODEL = 1024
BATCH = 16
SEQ = 256
DEPTH = 2
DEC_BATCH = 2
DEC_SEQ = 2048
PAST_LEN = 512
GRID_W = 64
NORM_EPS = 1e-6
CONV_DIM = 512
MLA_HEADS = 8
Q_RANK = 384
KV_RANK = 256
QK_NOPE = 64
QK_ROPE = 32
V_HEAD = 64
ROPE_BASE = 10000.0
MLA_SCALE = (QK_NOPE + QK_ROPE) ** -0.5
SSM_HEADS = 16
SSM_HEAD_DIM = 64
SSM_INNER = SSM_HEADS * SSM_HEAD_DIM
SSM_GROUPS = 2
SSM_STATE = 128
SSM_CONV_CH = SSM_INNER + 2 * SSM_GROUPS * SSM_STATE
N_EXPERTS = 32
TOP_K = 4
EXPERT_FF = D_MODEL
SWIGLU_ALPHA = 1.702
SWIGLU_LIMIT = 7.0

N_CTX = BATCH * SEQ
N_LAT = DEC_BATCH * DEC_SEQ
N_TOK = N_CTX + N_LAT
N_SEQS = BATCH + DEC_BATCH

LANES = 128
SUBLANES = 8
HEAD_PAD = 128
TM = 256
N_TILES = N_TOK // TM
CTX_TILES = N_CTX // TM
LAT_TILES_PER_SEQ = DEC_SEQ // TM
TQ_LAT = 512
CHUNK = 128
SSD_STEP = 2 * CHUNK
CTX_STEPS_PER_SEQ = SEQ // SSD_STEP
LAT_STEPS_PER_SEQ = DEC_SEQ // SSD_STEP
N_CTX_STEPS = N_CTX // SSD_STEP
N_SSD_STEPS = N_TOK // SSD_STEP
MOE_BLK = 512
MOE_PART = 128
N_ASSIGN = N_TOK * TOP_K
RUN_ALIGN = SUBLANES
RUN_BITS = tuple(range(8, 2, -1))
TAIL_BITS = tuple(range(8, 2, -1))
COMMON_BITS = 6
WAIT_BITS = tuple(range(10, 2, -1))
STAGE_ROWS = 1280
XS_COLS = D_MODEL + LANES
N_BLOCKS = -(-(N_ASSIGN + N_TILES * N_EXPERTS * (RUN_ALIGN - 1) + N_EXPERTS * (MOE_BLK - 1)) // MOE_BLK)
N_SLOTS = N_BLOCKS * MOE_BLK
VMEM_LIMIT = 56 * 1024 * 1024

C_A3 = 0
C_CQ = C_A3 + 3 * CONV_DIM
C_CKV = C_CQ + Q_RANK
C_Z = C_CKV + KV_RANK
C_XBC = C_Z + SSM_INNER
C_GATE = C_XBC + SSM_CONV_CH
C_SMALL = C_GATE + 3 * D_MODEL
IN_COLS2 = C_SMALL + LANES
SM_DTF = QK_ROPE
SM_DTB = QK_ROPE + SSM_HEADS
S_CQ = 3 * CONV_DIM
S_CKV = S_CQ + Q_RANK
S_KPE = S_CKV + KV_RANK
S_Z = S_KPE + QK_ROPE
S_XBC = S_Z + SSM_INNER
S_DTF = S_XBC + SSM_CONV_CH
S_GATE = S_DTF + 2 * SSM_HEADS
IN_COLS = S_GATE + 3 * D_MODEL
W_SEGMENTS = ((C_A3, 0, 3 * CONV_DIM), (C_CQ, S_CQ, Q_RANK), (C_CKV, S_CKV, KV_RANK), (C_Z, S_Z, SSM_INNER),
              (C_XBC, S_XBC, SSM_CONV_CH), (C_GATE, S_GATE, 3 * D_MODEL))
W_PIECE = 512


def _rms(x, w):
    return x * lax.rsqrt(jnp.mean(x * x, axis=-1, keepdims=True) + NORM_EPS) * w


def _silu(x):
    return x * jax.nn.sigmoid(x)


def _dot(a, b):
    return jnp.dot(a, b, preferred_element_type=F32)


def _dot_nt(a, b):
    return lax.dot_general(a, b, (((1,), (1,)), ((), ())), preferred_element_type=F32)


def _resident(shape):
    nd = len(shape)
    return pl.BlockSpec(shape, lambda *_: (0,) * nd, pipeline_mode=pl.Buffered(1))


def _mod_row(i):
    return jnp.where(i < CTX_TILES, 0, 1 + (i - CTX_TILES) // LAT_TILES_PER_SEQ)


def _pos_block(i):
    return jnp.where(i < CTX_TILES, 0, 1 + (i - CTX_TILES) % LAT_TILES_PER_SEQ)


def _ada_kernel(c_ref, w_ref, b_ref, o_ref):
    s = _silu(c_ref[...]).astype(BF16)
    o_ref[0] = _dot(s, w_ref[0].astype(BF16)) + b_ref[0]


def _ada_mods(cond8, w_ada, b_ada):
    tn = 1536
    n_mod = 6 * D_MODEL
    return pl.pallas_call(
        _ada_kernel,
        grid=(DEPTH, n_mod // tn),
        in_specs=[
            pl.BlockSpec((SUBLANES, D_MODEL), lambda l, j: (0, 0)),
            pl.BlockSpec((1, D_MODEL, tn), lambda l, j: (l, 0, j)),
            pl.BlockSpec((1, 1, tn), lambda l, j: (l, 0, j)),
        ],
        out_specs=pl.BlockSpec((1, SUBLANES, tn), lambda l, j: (l, 0, j)),
        out_shape=jax.ShapeDtypeStruct((DEPTH, SUBLANES, n_mod), F32),
        compiler_params=pltpu.CompilerParams(dimension_semantics=("arbitrary", "arbitrary")),
        name="ada_mods",
    )(cond8, w_ada, b_ada.reshape(DEPTH, 1, n_mod))


def _relayout_w_in(layer, wt_hbm, w2, stg, small_stg, sem, small_sem):
    pieces = [(dst + p, src + p, min(W_PIECE, width - p))
              for dst, src, width in W_SEGMENTS for p in range(0, width, W_PIECE)]
    fetch = lambda k: pltpu.make_async_copy(wt_hbm.at[layer, pl.ds(pieces[k][1], pieces[k][2]), :],
                                            stg.at[k % 2, pl.ds(0, pieces[k][2]), :], sem.at[k % 2])
    small_stg[...] = jnp.zeros_like(small_stg)
    small_copies = [
        pltpu.make_async_copy(wt_hbm.at[layer, pl.ds(S_KPE, QK_ROPE), :], small_stg.at[pl.ds(0, QK_ROPE), :],
                              small_sem.at[0]),
        pltpu.make_async_copy(wt_hbm.at[layer, pl.ds(S_DTF, 2 * SSM_HEADS), :],
                              small_stg.at[pl.ds(SM_DTF, 2 * SSM_HEADS), :], small_sem.at[1]),
    ]
    for cp in small_copies:
        cp.start()
    fetch(0).start()
    for k, (dst, _, width) in enumerate(pieces):
        if k + 1 < len(pieces):
            fetch(k + 1).start()
        fetch(k).wait()
        w2[:, dst:dst + width] = stg[k % 2, 0:width, :].T.astype(BF16)
    for cp in small_copies:
        cp.wait()
    w2[:, C_SMALL:IN_COLS2] = small_stg[...].T.astype(BF16)


def _place_rope_key(k_nope, kpe):
    shifted = pltpu.roll(kpe, QK_NOPE, 1)
    return jnp.concatenate([k_nope[:, h * HEAD_PAD:(h + 1) * HEAD_PAD] + shifted for h in range(MLA_HEADS)],
                           axis=1).astype(BF16)


def _rot_partner(x):
    n = x.shape[1]
    lane = lax.broadcasted_iota(I32, x.shape, 1)
    quarter = QK_ROPE // 4
    return jnp.where(lane % (2 * quarter) < quarter, pltpu.roll(x, n - quarter, 1), pltpu.roll(x, quarter, 1))


def _first_layer_slabs(ref, value):
    ref[0, 0] = value
    ref[0, 1:DEPTH] = jnp.zeros((DEPTH - 1,) + value.shape, value.dtype)


def _inproj_kernel(layer, *refs):
    n_in = 23 + (2 if layer > 0 else 0)
    (xc_ref, xcp_ref, xcn_ref, xl_ref, xlp_ref, xln_ref, mod_ref, n1w_ref, w_hbm, qnw_ref, wq_ref, kvnw_ref, wuk_ref,
     wuv_ref, wuvc_ref, vone_ref, cosq_ref, sinq_ref, cosk_ref, sink_ref, caw_ref, cw_ref, cb_ref) = refs[:23]
    (ta_ref, q_ref, ckv_ref, kpe_ref, kf_ref, vc_ref, vl_ref, small_ref, z_ref, xact_ref, g_ref,
     w2, stg, small_stg, sem, small_sem) = refs[n_in:]
    i = pl.program_id(0)

    @pl.when(i == 0)
    def _():
        _relayout_w_in(layer, w_hbm, w2, stg, small_stg, sem, small_sem)

    mod = mod_ref[0]
    shift1 = mod[:, 0:D_MODEL]
    scale1 = mod[:, D_MODEL:2 * D_MODEL]
    x_ext = jnp.where(i < CTX_TILES, jnp.concatenate([xcp_ref[...], xc_ref[...], xcn_ref[...]], axis=0),
                      jnp.concatenate([xlp_ref[...], xl_ref[...], xln_ref[...]], axis=0))
    h_ext = _rms(x_ext, n1w_ref[...]) * (1.0 + scale1) + shift1
    hb_ext = h_ext.astype(BF16)
    hb = h_ext[SUBLANES:SUBLANES + TM].astype(BF16)

    j = (i - CTX_TILES) % LAT_TILES_PER_SEQ
    is_ctx = i < CTX_TILES
    keep_prev = jnp.where(jnp.logical_or(is_ctx, j == 0), 0.0, 1.0)
    keep_next = jnp.where(jnp.logical_or(is_ctx, j == LAT_TILES_PER_SEQ - 1), 0.0, 1.0)
    row = lax.broadcasted_iota(I32, (TM, 1), 0)
    prev_mask = jnp.where(row == 0, keep_prev, 1.0)
    next_mask = jnp.where(row == TM - 1, keep_next, 1.0)
    ext = TM + 2 * SUBLANES

    def conv3(x, w_ref):
        x_prev = pltpu.roll(x, 1, 0)[SUBLANES:SUBLANES + TM] * prev_mask
        x_next = pltpu.roll(x, ext - 1, 0)[SUBLANES:SUBLANES + TM] * next_mask
        return x_prev * w_ref[0:1, :] + x[SUBLANES:SUBLANES + TM] * w_ref[1:2, :] + x_next * w_ref[2:3, :]

    a3 = _dot(hb_ext, w2[:, C_A3:C_CQ])
    s = a3[:, 2 * CONV_DIM:3 * CONV_DIM] * a3[:, 0:CONV_DIM]
    ta_ref[...] = (a3[SUBLANES:SUBLANES + TM, CONV_DIM:2 * CONV_DIM] * conv3(s, caw_ref)).astype(BF16)
    xact_ref[...] = _silu(conv3(_dot(hb_ext, w2[:, C_XBC:C_GATE]), cw_ref) + cb_ref[...])

    def seg(a, b):
        return _dot(hb, w2[:, a:b])

    cqn = _rms(seg(C_CQ, C_CKV), qnw_ref[...]).astype(BF16)
    qa = _dot(cqn, wq_ref[...])
    qa_rot = _rot_partner(qa)
    for h in range(MLA_HEADS):
        hs = slice(h * HEAD_PAD, (h + 1) * HEAD_PAD)
        q_ref[:, hs] = (qa[:, hs] * cosq_ref[...] + qa_rot[:, hs] * sinq_ref[...]).astype(BF16)

    ckv = _rms(seg(C_CKV, C_Z), kvnw_ref[...])
    ckvb = ckv.astype(BF16)
    small = seg(C_SMALL, IN_COLS2)
    small_ref[...] = small
    kpe = small * cosk_ref[...] + _rot_partner(small) * sink_ref[...]
    kf_ref[...] = _place_rope_key(_dot(ckvb, wuk_ref[...]), kpe)
    v_lat = (_dot(ckvb, wuv_ref[...]) + vone_ref[...]).astype(BF16)
    v_ctx = _dot(ckvb, wuvc_ref[...]).astype(BF16)

    z_ref[...] = seg(C_Z, C_XBC)
    g_ref[...] = jax.nn.sigmoid(seg(C_GATE, C_SMALL))

    @pl.when(i < CTX_TILES)
    def _():
        if layer == 0:
            _first_layer_slabs(ckv_ref, ckv)
            _first_layer_slabs(kpe_ref, small[:, 0:QK_ROPE])
        else:
            ckv_ref[0] = ckv
            kpe_ref[0] = small[:, 0:QK_ROPE]
        vc_ref[...] = v_ctx

    @pl.when(i >= CTX_TILES)
    def _():
        vl_ref[...] = v_lat


def _inproj(layer, xc, xl, mod3, w_in, lw, tabs, caches):
    row = lambda n: pl.BlockSpec((TM, n), lambda i: (i, 0))
    tab = lambda n: pl.BlockSpec((TM, n), lambda i: (_pos_block(i), 0))
    per = TM // SUBLANES
    qw = MLA_HEADS * HEAD_PAD
    ctx_row = lambda n: pl.BlockSpec((TM, n), lambda i: (jnp.minimum(i, CTX_TILES - 1), 0))
    lat_row = lambda n: pl.BlockSpec((TM, n), lambda i: (jnp.maximum(i - CTX_TILES, 0), 0))

    def halo(first_tile, n_rows, side):
        last = n_rows // SUBLANES - 1
        return pl.BlockSpec((SUBLANES, D_MODEL),
                            lambda i: (jnp.clip((i - first_tile + side) * per - 1 + side, 0, last), 0))

    out_shape = (
        jax.ShapeDtypeStruct((N_TOK, CONV_DIM), BF16),
        jax.ShapeDtypeStruct((N_TOK, qw), BF16),
        jax.ShapeDtypeStruct((BATCH, DEPTH, SEQ, KV_RANK), F32),
        jax.ShapeDtypeStruct((BATCH, DEPTH, SEQ, QK_ROPE), F32),
        jax.ShapeDtypeStruct((N_TOK, qw), BF16),
        jax.ShapeDtypeStruct((N_CTX, MLA_HEADS * V_HEAD), BF16),
        jax.ShapeDtypeStruct((N_LAT, qw), BF16),
        jax.ShapeDtypeStruct((N_TOK, LANES), F32),
        jax.ShapeDtypeStruct((N_TOK, SSM_INNER), F32),
        jax.ShapeDtypeStruct((N_TOK, SSM_CONV_CH), F32),
        jax.ShapeDtypeStruct((N_TOK, 3 * D_MODEL), F32),
    )
    out_specs = [row(s.shape[-1]) for s in out_shape]
    cache_block = (1, DEPTH, SEQ) if layer == 0 else (1, None, SEQ)
    cache_index = lambda i: (jnp.minimum(i, CTX_TILES - 1), 0 if layer == 0 else layer, 0, 0)
    out_specs[2] = pl.BlockSpec(cache_block + (KV_RANK,), cache_index)
    out_specs[3] = pl.BlockSpec(cache_block + (QK_ROPE,), cache_index)
    extra = {} if layer == 0 else dict(input_output_aliases={23: 2, 24: 3})
    out_specs[5] = ctx_row(MLA_HEADS * V_HEAD)
    out_specs[6] = lat_row(qw)
    return pl.pallas_call(
        functools.partial(_inproj_kernel, layer),
        grid=(N_TILES,),
        in_specs=[
            ctx_row(D_MODEL), halo(0, N_CTX, 0), halo(0, N_CTX, 1),
            lat_row(D_MODEL), halo(CTX_TILES, N_LAT, 0), halo(CTX_TILES, N_LAT, 1),
            pl.BlockSpec((1, 1, 6 * D_MODEL), lambda i: (_mod_row(i), 0, 0)),
            _resident((1, D_MODEL)),
            pl.BlockSpec(memory_space=pl.ANY),
            _resident((1, Q_RANK)),
            _resident((Q_RANK, qw)),
            _resident((1, KV_RANK)),
            _resident((KV_RANK, qw)),
            _resident((KV_RANK, qw)),
            _resident((KV_RANK, MLA_HEADS * V_HEAD)),
            _resident((1, qw)),
            tab(HEAD_PAD), tab(HEAD_PAD), tab(LANES), tab(LANES),
            _resident((3, CONV_DIM)), _resident((3, SSM_CONV_CH)), _resident((1, SSM_CONV_CH)),
        ] + [pl.BlockSpec(memory_space=pl.ANY)] * len(caches),
        out_specs=tuple(out_specs),
        out_shape=out_shape,
        scratch_shapes=[pltpu.VMEM((D_MODEL, IN_COLS2), BF16), pltpu.VMEM((2, W_PIECE, D_MODEL), F32),
                        pltpu.VMEM((LANES, D_MODEL), F32), pltpu.SemaphoreType.DMA((2,)),
                        pltpu.SemaphoreType.DMA((2,))],
        compiler_params=pltpu.CompilerParams(dimension_semantics=("arbitrary",), vmem_limit_bytes=VMEM_LIMIT),
        name="inproj",
        **extra,
    )(xc, xc, xc, xl, xl, xl, mod3, lw["norm1_w"], w_in, lw["q_norm_w"], lw["wq_a"], lw["kv_norm_w"],
      lw["wuk"], lw["wuv"], lw["wuv_c"], tabs["vone"], tabs["cosq"], tabs["sinq"], tabs["cosk"],
      tabs["sink"], lw["conv_a_w"], lw["ssm_conv_w"], lw["ssm_conv_b"], *caches)


def _kvcache_kernel(ckv_ref, kpe_ref, wuk_ref, wuv_ref, vone_ref, kf_ref, v_ref):
    ckvb = ckv_ref[...].astype(BF16)
    kf_ref[...] = _place_rope_key(_dot(ckvb, wuk_ref[...]), kpe_ref[...])
    v_ref[...] = (_dot(ckvb, wuv_ref[...]) + vone_ref[...]).astype(BF16)


def _kvcache(ckv, kpe128, lw, tabs):
    n = ckv.shape[0]
    qw = MLA_HEADS * HEAD_PAD
    return pl.pallas_call(
        _kvcache_kernel,
        grid=(n // PAST_LEN,),
        in_specs=[
            pl.BlockSpec((PAST_LEN, KV_RANK), lambda i: (i, 0)),
            pl.BlockSpec((PAST_LEN, LANES), lambda i: (i, 0)),
            _resident((KV_RANK, qw)),
            _resident((KV_RANK, qw)),
            _resident((1, qw)),
        ],
        out_specs=(pl.BlockSpec((PAST_LEN, qw), lambda i: (i, 0)), pl.BlockSpec((PAST_LEN, qw), lambda i: (i, 0))),
        out_shape=(jax.ShapeDtypeStruct((n, qw), BF16), jax.ShapeDtypeStruct((n, qw), BF16)),
        compiler_params=pltpu.CompilerParams(dimension_semantics=("arbitrary",)),
        name="kvcache",
    )(ckv, kpe128, lw["wuk"], lw["wuv"], tabs["vone"])


def _attn_heads(q_ref, kv_refs, o_ref, acc_ref, denom_from_matmul):
    log2_scale = MLA_SCALE * math.log2(math.e)
    for h in range(MLA_HEADS):
        hs = slice(h * HEAD_PAD, (h + 1) * HEAD_PAD)
        qh = q_ref[:, hs]
        ss = [_dot_nt(qh, k_ref[:, hs]) for k_ref, _ in kv_refs]
        if denom_from_matmul:
            m = functools.reduce(jnp.maximum, [jnp.max(s, axis=-1, keepdims=True) for s in ss])
            ps = [jnp.exp2((s - m) * log2_scale) for s in ss]
            ol = functools.reduce(jnp.add, [_dot(p.astype(BF16), v_ref[:, hs])
                                            for p, (_, v_ref) in zip(ps, kv_refs)])
            acc_ref[:, h * V_HEAD:(h + 1) * V_HEAD] = ol[:, 0:V_HEAD] / ol[:, V_HEAD:V_HEAD + 1]
        else:
            ss = [s * MLA_SCALE for s in ss]
            m = functools.reduce(jnp.maximum, [jnp.max(s, axis=-1, keepdims=True) for s in ss])
            ps = [jnp.exp(s - m) for s in ss]
            l = functools.reduce(jnp.add, [jnp.sum(p, axis=-1, keepdims=True) for p in ps])
            o = functools.reduce(jnp.add, [_dot(p.astype(BF16), v_ref[:, h * V_HEAD:(h + 1) * V_HEAD])
                                           for p, (_, v_ref) in zip(ps, kv_refs)])
            acc_ref[:, h * V_HEAD:(h + 1) * V_HEAD] = o / l
    o_ref[...] = acc_ref[...].astype(BF16)


def _attn_ctx_kernel(q_ref, k_ref, v_ref, o_ref, acc_ref):
    _attn_heads(q_ref, [(k_ref, v_ref)], o_ref, acc_ref, denom_from_matmul=False)


def _attn_lat_kernel(q_ref, k_ref, v_ref, kc_ref, vc_ref, o_ref, acc_ref):
    _attn_heads(q_ref, [(k_ref, v_ref), (kc_ref, vc_ref)], o_ref, acc_ref, denom_from_matmul=True)


def _attention(q, kf, v_ctx, v_lat, kf_c, v_c):
    qw = MLA_HEADS * HEAD_PAD
    vw = MLA_HEADS * V_HEAD
    att_ctx = pl.pallas_call(
        _attn_ctx_kernel,
        grid=(BATCH,),
        in_specs=[pl.BlockSpec((SEQ, qw), lambda b: (b, 0)), pl.BlockSpec((SEQ, qw), lambda b: (b, 0)),
                  pl.BlockSpec((SEQ, vw), lambda b: (b, 0))],
        out_specs=pl.BlockSpec((SEQ, vw), lambda b: (b, 0)),
        out_shape=jax.ShapeDtypeStruct((N_CTX, vw), BF16),
        scratch_shapes=[pltpu.VMEM((SEQ, vw), F32)],
        compiler_params=pltpu.CompilerParams(dimension_semantics=("arbitrary",)),
        name="attn_ctx",
    )(q, kf, v_ctx)
    lat0 = N_CTX // DEC_SEQ
    q_tiles = DEC_SEQ // TQ_LAT
    att_lat = pl.pallas_call(
        _attn_lat_kernel,
        grid=(DEC_BATCH, q_tiles),
        in_specs=[
            pl.BlockSpec((TQ_LAT, qw), lambda b, t: (N_CTX // TQ_LAT + b * q_tiles + t, 0)),
            pl.BlockSpec((DEC_SEQ, qw), lambda b, t: (lat0 + b, 0)),
            pl.BlockSpec((DEC_SEQ, qw), lambda b, t: (b, 0)),
            pl.BlockSpec((PAST_LEN, qw), lambda b, t: (b, 0)),
            pl.BlockSpec((PAST_LEN, qw), lambda b, t: (b, 0)),
        ],
        out_specs=pl.BlockSpec((TQ_LAT, vw), lambda b, t: (b * q_tiles + t, 0)),
        out_shape=jax.ShapeDtypeStruct((N_LAT, vw), BF16),
        scratch_shapes=[pltpu.VMEM((TQ_LAT, vw), F32)],
        compiler_params=pltpu.CompilerParams(dimension_semantics=("arbitrary", "arbitrary"),
                                             vmem_limit_bytes=VMEM_LIMIT),
        name="attn_lat",
    )(q, kf, v_lat, kf_c, v_c)
    return att_ctx, att_lat


def _seq_of_step(s):
    return jnp.where(s < N_CTX_STEPS, s // CTX_STEPS_PER_SEQ,
                     BATCH + (s - N_CTX_STEPS) // LAT_STEPS_PER_SEQ)


def _step_in_seq(s):
    return jnp.where(s < N_CTX_STEPS, s % CTX_STEPS_PER_SEQ, (s - N_CTX_STEPS) % LAT_STEPS_PER_SEQ)


def _steps_in_seq(s):
    return jnp.where(s < N_CTX_STEPS, CTX_STEPS_PER_SEQ, LAT_STEPS_PER_SEQ)


def _mirror_step(s):
    return s + _steps_in_seq(s) - 1 - 2 * _step_in_seq(s)


def _split3(a):
    a1 = a.astype(BF16)
    r1 = a - a1.astype(F32)
    a2 = r1.astype(BF16)
    a3 = (r1 - a2.astype(F32)).astype(BF16)
    return a1, a2, a3


def _ssd_direction(x_ref, sm_ref, par_ref, st_ref, y_ref, lane0, backward, r0):
    rows = slice(r0, r0 + CHUNK)
    ri = lax.broadcasted_iota(I32, (CHUNK, CHUNK), 0)
    ci = lax.broadcasted_iota(I32, (CHUNK, CHUNK), 1)
    tri = (ci >= ri) if backward else (ci <= ri)
    tri_b = jnp.where(tri, 1.0, 0.0).astype(BF16)
    tot_row = 0 if backward else CHUNK - 1

    dt = jax.nn.softplus(sm_ref[rows, :] + par_ref[0:1, :])
    a = dt * (-jnp.exp(par_ref[1:2, :])) * par_ref[2:3, :]
    a1, a2, a3 = _split3(a)
    acs = (_dot(tri_b, a1) + _dot(tri_b, a2) + _dot(tri_b, a3)) * math.log2(math.e)
    acs_t = acs.T
    dt_t = dt.T
    first_head = lax.broadcasted_iota(I32, (1, LANES), 1) < SSM_HEAD_DIM

    def block_diag(pair):
        return jnp.concatenate([jnp.where(first_head, pair, 0.0), jnp.where(first_head, 0.0, pair)],
                               axis=0).astype(BF16)

    for g in range(SSM_GROUPS):
        b0 = SSM_INNER + g * SSM_STATE
        c0 = SSM_INNER + SSM_GROUPS * SSM_STATE + g * SSM_STATE
        bg = x_ref[rows, b0:b0 + SSM_STATE]
        cg = x_ref[rows, c0:c0 + SSM_STATE]
        cb = _dot_nt(cg.astype(BF16), bg.astype(BF16))
        bg_t = bg.T
        heads = SSM_HEADS // SSM_GROUPS
        for pr in range(heads // 2):
            h0 = g * heads + 2 * pr
            sl = slice(h0 * SSM_HEAD_DIM, (h0 + 2) * SSM_HEAD_DIM)
            x_bd = block_diag(x_ref[rows, sl])
            st_old = st_ref[:, sl]
            within, carried, to_state, keep = [], [], [], []
            for h in (h0, h0 + 1):
                lane = lane0 + h
                col = jnp.broadcast_to(acs[:, lane:lane + 1], (CHUNK, CHUNK))
                row = acs_t[lane:lane + 1, :]
                dt_row = dt_t[lane:lane + 1, :]
                decay = jnp.exp2(jnp.where(tri, col - row, -jnp.inf))
                within.append((cb * decay * dt_row).astype(BF16))
                carried.append((cg * jnp.exp2(col)).astype(BF16))
                tot = acs[tot_row:tot_row + 1, lane:lane + 1]
                to_state.append((bg_t * (dt_row * jnp.exp2(tot - row))).astype(BF16))
                keep.append(jnp.exp2(tot))
            lhs = jnp.concatenate(within + carried, axis=1)
            y_ref[rows, sl] = _dot(lhs, jnp.concatenate([x_bd, block_diag(st_old)], axis=0))
            st_ref[:, sl] = (st_old * jnp.where(first_head, keep[0], keep[1])
                             + _dot(jnp.concatenate(to_state, axis=1), x_bd))


def _ssd_kernel(layer, *refs):
    n_in = 7 + (2 if layer > 0 else 0)
    xf_ref, xb_ref, smf_ref, smb_ref, if_ref, ib_ref, par_ref = refs[:7]
    yf_ref, yb_ref, sf_ref, sb_ref, stf_ref, stb_ref = refs[n_in:]
    s = pl.program_id(0)
    c = _step_in_seq(s)

    @pl.when(jnp.logical_and(c == 0, s < N_CTX_STEPS))
    def _():
        stf_ref[...] = jnp.zeros_like(stf_ref)
        stb_ref[...] = jnp.zeros_like(stb_ref)

    @pl.when(jnp.logical_and(c == 0, s >= N_CTX_STEPS))
    def _():
        stf_ref[...] = if_ref[0].T
        stb_ref[...] = ib_ref[0].T

    for r0 in range(0, SSD_STEP, CHUNK):
        _ssd_direction(xf_ref, smf_ref, par_ref, stf_ref, yf_ref, SM_DTF, False, r0)
        _ssd_direction(xb_ref, smb_ref, par_ref, stb_ref, yb_ref, SM_DTB, True, SSD_STEP - CHUNK - r0)

    @pl.when(jnp.logical_and(c == _steps_in_seq(s) - 1, s < N_CTX_STEPS))
    def _():
        if layer == 0:
            _first_layer_slabs(sf_ref, stf_ref[...].T)
            _first_layer_slabs(sb_ref, stb_ref[...].T)
        else:
            sf_ref[0] = stf_ref[...].T
            sb_ref[0] = stb_ref[...].T


def _ssd(layer, xact, small, init_f, init_b, par, states):
    hp = SSM_INNER
    fwd = lambda n: pl.BlockSpec((SSD_STEP, n), lambda s: (s, 0))
    bwd = lambda n: pl.BlockSpec((SSD_STEP, n), lambda s: (_mirror_step(s), 0))
    st = pl.BlockSpec((1, DEPTH, hp, SSM_STATE) if layer == 0 else (1, None, hp, SSM_STATE),
                      lambda s: (jnp.minimum(_seq_of_step(s), BATCH - 1), 0 if layer == 0 else layer, 0, 0))
    init = pl.BlockSpec((1, hp, SSM_STATE), lambda s: (jnp.maximum(_seq_of_step(s) - BATCH, 0), 0, 0))
    extra = {} if layer == 0 else dict(input_output_aliases={7: 2, 8: 3})
    return pl.pallas_call(
        functools.partial(_ssd_kernel, layer),
        grid=(N_SSD_STEPS,),
        in_specs=[fwd(SSM_CONV_CH), bwd(SSM_CONV_CH), fwd(LANES), bwd(LANES), init, init,
                  pl.BlockSpec((SUBLANES, LANES), lambda s: (0, 0))] + [pl.BlockSpec(memory_space=pl.ANY)] * len(states),
        out_specs=(fwd(hp), bwd(hp), st, st),
        out_shape=(jax.ShapeDtypeStruct((N_TOK, hp), F32), jax.ShapeDtypeStruct((N_TOK, hp), F32),
                   jax.ShapeDtypeStruct((BATCH, DEPTH, hp, SSM_STATE), F32),
                   jax.ShapeDtypeStruct((BATCH, DEPTH, hp, SSM_STATE), F32)),
        scratch_shapes=[pltpu.VMEM((SSM_STATE, hp), F32), pltpu.VMEM((SSM_STATE, hp), F32)],
        compiler_params=pltpu.CompilerParams(dimension_semantics=("arbitrary",)),
        name="ssd",
        **extra,
    )(xact, xact, small, small, init_f, init_b, par, *states)


def _merge_kernel(ta_ref, attc_ref, attl_ref, yf_ref, yb_ref, xs_ref, z_ref, g_ref, xc_ref, xl_ref, mod_ref,
                  woa_ref, wom_ref, dsk_ref, snw_ref, wos_ref, wo_ref, n2w_ref, rw_ref, rb_ref,
                  x1_ref, h2_ref, ti_ref, tg_ref, cnt_ref):
    mod = mod_ref[0]
    gate1 = mod[:, 2 * D_MODEL:3 * D_MODEL]
    shift2 = mod[:, 3 * D_MODEL:4 * D_MODEL]
    scale2 = mod[:, 4 * D_MODEL:5 * D_MODEL]
    y_a = _dot(ta_ref[...], woa_ref[...])
    att = jnp.where(pl.program_id(0) < CTX_TILES, attc_ref[...].astype(F32), attl_ref[...].astype(F32))
    y_b = _dot(att.astype(BF16), wom_ref[...])
    y_ssm = (yf_ref[...] + yb_ref[...] + dsk_ref[...] * xs_ref[...]) * _silu(z_ref[...])
    y_c = _dot(_rms(y_ssm, snw_ref[...]).astype(BF16), wos_ref[...])
    merged = (g_ref[:, 0:D_MODEL] * y_a + g_ref[:, D_MODEL:2 * D_MODEL] * y_b
              + g_ref[:, 2 * D_MODEL:3 * D_MODEL] * y_c)
    x = jnp.where(pl.program_id(0) < CTX_TILES, xc_ref[...], xl_ref[...])
    x1 = x + gate1 * _dot(merged.astype(BF16), wo_ref[...])
    x1_ref[...] = x1
    h2 = _rms(x1, n2w_ref[...]) * (1.0 + scale2) + shift2
    h2b = h2.astype(BF16)
    h2_ref[...] = h2b

    logits = (_dot(h2b, rw_ref[...]) + rb_ref[...]).T[0:N_EXPERTS, :]
    expert = lax.broadcasted_iota(I32, (N_EXPERTS, TM), 0)
    ids, vals = [], []
    for k in range(TOP_K):
        m = jnp.max(logits, axis=0, keepdims=True)
        idx = jnp.min(jnp.where(logits == m, expert, N_EXPERTS), axis=0, keepdims=True)
        ids.append(idx)
        vals.append(m)
        logits = jnp.where(expert == idx, -jnp.inf, logits)
    es = [jnp.exp(v - vals[0]) for v in vals]
    denom = functools.reduce(jnp.add, es)
    srow = lax.broadcasted_iota(I32, (LANES, TM), 0)
    ti_t = jnp.zeros((LANES, TM), F32)
    tg_t = jnp.zeros((LANES, TM), F32)
    chosen_t = jnp.zeros((LANES, TM), F32)
    for k in range(TOP_K):
        ti_t = jnp.where(srow == k, ids[k].astype(F32), ti_t)
        tg_t = jnp.where(srow == k, es[k] / denom, tg_t)
        chosen_t = jnp.where(srow == ids[k], 1.0, chosen_t)
    ti_ref[...] = ti_t.T.astype(I32)
    tg_ref[...] = tg_t.T
    cnt_ref[0] = _dot_nt(jnp.ones((SUBLANES, TM), BF16), chosen_t.astype(BF16)).astype(I32)


def _merge(ta, att_ctx, att_lat, yf, yb, xact, z, g, xc, xl, mod3, lw):
    row = lambda n: pl.BlockSpec((TM, n), lambda i: (i, 0))
    vw = MLA_HEADS * V_HEAD
    out_shape = (jax.ShapeDtypeStruct((N_TOK, D_MODEL), F32), jax.ShapeDtypeStruct((N_TOK, D_MODEL), BF16),
                 jax.ShapeDtypeStruct((N_TOK, LANES), I32), jax.ShapeDtypeStruct((N_TOK, LANES), F32))
    cnt_shape = jax.ShapeDtypeStruct((N_TILES, SUBLANES, LANES), I32)
    cnt_spec = pl.BlockSpec((1, SUBLANES, LANES), lambda i: (i, 0, 0))
    return pl.pallas_call(
        _merge_kernel,
        grid=(N_TILES,),
        in_specs=[
            row(CONV_DIM),
            pl.BlockSpec((TM, vw), lambda i: (jnp.minimum(i, CTX_TILES - 1), 0)),
            pl.BlockSpec((TM, vw), lambda i: (jnp.maximum(i - CTX_TILES, 0), 0)),
            row(SSM_INNER), row(SSM_INNER), row(SSM_INNER),
            row(SSM_INNER), row(3 * D_MODEL),
            pl.BlockSpec((TM, D_MODEL), lambda i: (jnp.minimum(i, CTX_TILES - 1), 0)),
            pl.BlockSpec((TM, D_MODEL), lambda i: (jnp.maximum(i - CTX_TILES, 0), 0)),
            pl.BlockSpec((1, 1, 6 * D_MODEL), lambda i: (_mod_row(i), 0, 0)),
            _resident((CONV_DIM, D_MODEL)), _resident((MLA_HEADS * V_HEAD, D_MODEL)),
            _resident((1, SSM_INNER)), _resident((1, SSM_INNER)), _resident((SSM_INNER, D_MODEL)),
            _resident((D_MODEL, D_MODEL)), _resident((1, D_MODEL)),
            _resident((D_MODEL, LANES)), _resident((1, LANES)),
        ],
        out_specs=tuple(row(s.shape[1]) for s in out_shape) + (cnt_spec,),
        out_shape=out_shape + (cnt_shape,),
        compiler_params=pltpu.CompilerParams(dimension_semantics=("arbitrary",), vmem_limit_bytes=VMEM_LIMIT),
        name="merge",
    )(ta, att_ctx, att_lat, yf, yb, xact, z, g, xc, xl, mod3, lw["w_out_a"], lw["w_o_mla"], lw["d_skip"],
      lw["ssm_norm_w"],
      lw["w_o_ssm"], lw["w_o"], lw["norm2_w"], lw["router_w"], lw["router_b"])


def _run_copies(cnt_ref, src_ref, dst_ref, first, count, bits, make_copy, start):
    def body(e, carry):
        n = cnt_ref[first + e]
        s0 = src_ref[first + e] if src_ref is not None else 0
        d0 = dst_ref[first + e]

        def pieces(some_bits):
            for b in some_bits:
                above = (n >> (b + 1)) << (b + 1)

                @pl.when(((n >> b) & 1) == 1)
                def _():
                    cp = make_copy(pl.multiple_of(s0 + above, RUN_ALIGN), pl.multiple_of(d0 + above, RUN_ALIGN),
                                   1 << b)
                    if start:
                        cp.start()
                    else:
                        cp.wait()

        large = [b for b in bits if b >= COMMON_BITS]

        @pl.when(n >= (1 << COMMON_BITS))
        def _():
            pieces(large)
        pieces([b for b in bits if b < COMMON_BITS])
        return carry
    lax.fori_loop(0, count, body, 0)


def _wait_rows(total, make_copy):
    for b in WAIT_BITS:
        @pl.when(((total >> b) & 1) == 1)
        def _():
            make_copy(0, 0, 1 << b).wait()


def _dispatch_kernel(cnt_ref, off_ref, run_ref, tcnt_ref, tdst_ref, nu_ref, h2_ref, ti_ref, tg_ref, offv_ref,
                     xs_ref, lp_ref, stage, zeros, perm_s, ghi_s, glo_s, sem, semz):
    i = pl.program_id(0)
    slot = i % 2

    def copy_out(s_):
        return lambda s, d, n: pltpu.make_async_copy(stage.at[s_, pl.ds(s, n), :], xs_ref.at[pl.ds(d, n), :],
                                                     sem.at[s_])

    def tile_rows(t):
        last = t * N_EXPERTS + N_EXPERTS - 1
        return off_ref[last] + cnt_ref[last]

    @pl.when(i == 0)
    def _():
        zeros[...] = jnp.zeros_like(zeros)
        zero_out = lambda s, d, n: pltpu.make_async_copy(zeros.at[pl.ds(0, n), :], xs_ref.at[pl.ds(d, n), :], semz)
        _run_copies(tcnt_ref, None, tdst_ref, 0, N_EXPERTS, TAIL_BITS, zero_out, True)
        _run_copies(tcnt_ref, None, tdst_ref, 0, N_EXPERTS, TAIL_BITS, zero_out, False)
        zrows = zeros.shape[0]

        def unused_blocks(start):
            def body(b, carry):
                for part in range(MOE_BLK // zrows):
                    cp = zero_out(0, pl.multiple_of(b * MOE_BLK + part * zrows, RUN_ALIGN), zrows)
                    if start:
                        cp.start()
                    else:
                        cp.wait()
                return carry
            lax.fori_loop(nu_ref[0], N_BLOCKS, body, 0)
        unused_blocks(True)
        unused_blocks(False)

    @pl.when(i >= 2)
    def _():
        _wait_rows(tile_rows(i - 2), copy_out(slot))

    lane = lax.broadcasted_iota(I32, (TM, LANES), 1)
    picks = [jnp.where(lane == ti_ref[:, k:k + 1], 1.0, 0.0) for k in range(TOP_K)]
    ri = lax.broadcasted_iota(I32, (TM, TM), 0)
    ci = lax.broadcasted_iota(I32, (TM, TM), 1)
    earlier = jnp.where(ci < ri, 1.0, 0.0).astype(BF16)
    base = _dot(earlier, functools.reduce(jnp.add, picks).astype(BF16)) + offv_ref[0][0:1, :]
    lp = jnp.full((TM, LANES), -1.0, F32)
    for k in range(TOP_K):
        lp = jnp.where(lane == k, jnp.sum(picks[k] * base, axis=-1, keepdims=True), lp)
    lp_ref[...] = lp.astype(I32)

    lp_t = lp.T
    tg_t = tg_ref[...].T
    for c in range(STAGE_ROWS // LANES):
        rows = slice(c * LANES, (c + 1) * LANES)
        row = (lax.broadcasted_iota(I32, (LANES, TM), 0) + c * LANES).astype(F32)
        perm = jnp.zeros((LANES, TM), F32)
        gates = jnp.zeros((LANES, TM), F32)
        for k in range(TOP_K):
            hit = row == lp_t[k:k + 1, :]
            perm = jnp.where(hit, 1.0, perm)
            gates = jnp.where(hit, tg_t[k:k + 1, :], gates)
        perm_s[rows, :] = perm.astype(BF16)
        g_hi = gates.astype(BF16)
        ghi_s[rows, :] = g_hi
        glo_s[rows, :] = (gates - g_hi.astype(F32)).astype(BF16)
    stage[slot, :, 0:D_MODEL] = _dot(perm_s[...], h2_ref[...])
    ones = jnp.ones((TM, LANES), BF16)
    stage[slot, :, D_MODEL:XS_COLS] = _dot(ghi_s[...], ones) + _dot(glo_s[...], ones)

    _run_copies(cnt_ref, off_ref, run_ref, i * N_EXPERTS, N_EXPERTS, RUN_BITS, copy_out(slot), True)

    @pl.when(i == N_TILES - 1)
    def _():
        _wait_rows(tile_rows(i - 1), copy_out(1 - slot))
        _wait_rows(tile_rows(i), copy_out(slot))


def _dispatch(rt, h2, top_i, top_g):
    row = lambda n: pl.BlockSpec((TM, n), lambda i, *_: (i, 0))
    return pl.pallas_call(
        _dispatch_kernel,
        grid_spec=pltpu.PrefetchScalarGridSpec(
            num_scalar_prefetch=6,
            grid=(N_TILES,),
            in_specs=[row(D_MODEL), row(LANES), row(LANES),
                      pl.BlockSpec((1, SUBLANES, LANES), lambda i, *_: (i, 0, 0))],
            out_specs=(pl.BlockSpec(memory_space=pl.ANY), row(LANES)),
            scratch_shapes=[
                pltpu.VMEM((2, STAGE_ROWS, XS_COLS), F32),
                pltpu.VMEM((1 << TAIL_BITS[0], XS_COLS), F32),
                pltpu.VMEM((STAGE_ROWS, TM), BF16), pltpu.VMEM((STAGE_ROWS, TM), BF16),
                pltpu.VMEM((STAGE_ROWS, TM), BF16),
                pltpu.SemaphoreType.DMA((2,)),
                pltpu.SemaphoreType.DMA,
            ],
        ),
        out_shape=(jax.ShapeDtypeStruct((N_SLOTS, XS_COLS), F32), jax.ShapeDtypeStruct((N_TOK, LANES), I32)),
        compiler_params=pltpu.CompilerParams(dimension_semantics=("arbitrary",), vmem_limit_bytes=VMEM_LIMIT),
        name="dispatch",
    )(rt["cnt"], rt["off"], rt["run"], rt["tail_cnt"], rt["tail_dst"], rt["n_used"], h2, top_i, top_g, rt["off_v"])


def _moe_kernel(layer, be_ref, nxt_ref, nv_ref, nu_ref, x_ref, wgu_hbm, wdn_hbm, bg_ref, bu_ref, bd_ref, sel_ref,
                y_ref, wgu_ref, wdn_ref, wgu_s, wdn_s, sem):
    i = pl.program_id(0)
    n_used = nu_ref[0]

    def fetch(e):
        return (pltpu.make_async_copy(wgu_hbm.at[layer, e], wgu_ref, sem.at[0]),
                pltpu.make_async_copy(wdn_hbm.at[layer, e], wdn_ref, sem.at[1]))

    @pl.when(i < n_used)
    def _():
        @pl.when(jnp.logical_or(i == 0, be_ref[i] != be_ref[jnp.maximum(i - 1, 0)]))
        def _():
            @pl.when(i == 0)
            def _():
                for cp in fetch(be_ref[0]):
                    cp.start()
            for cp in fetch(be_ref[i]):
                cp.wait()
            half = LANES
            for c in range(2 * EXPERT_FF // (2 * half)):
                r = _dot(wgu_ref[:, c * 2 * half:(c + 1) * 2 * half].astype(BF16), sel_ref[...])
                wgu_s[:, c * half:(c + 1) * half] = r[:, 0:half].astype(BF16)
                wgu_s[:, EXPERT_FF + c * half:EXPERT_FF + (c + 1) * half] = r[:, half:2 * half].astype(BF16)
            wdn_s[...] = wdn_ref[...].astype(BF16)
            nxt = nxt_ref[i]

            @pl.when(nxt >= 0)
            def _():
                for cp in fetch(nxt):
                    cp.start(priority=1)

        def expert_rows(rows):
            gu = _dot(x_ref[0:rows, 0:D_MODEL].astype(BF16), wgu_s[...])
            gate = jnp.minimum(gu[:, 0:EXPERT_FF] + bg_ref[0], SWIGLU_LIMIT)
            up = jnp.clip(gu[:, EXPERT_FF:2 * EXPERT_FF] + bu_ref[0], -SWIGLU_LIMIT, SWIGLU_LIMIT)
            act = gate * jax.nn.sigmoid(SWIGLU_ALPHA * gate) * (up + 1.0)
            y = _dot(act.astype(BF16), wdn_s[...]) + bd_ref[0]
            slot_gate = x_ref[0:rows, D_MODEL:XS_COLS]
            for j in range(D_MODEL // LANES):
                y_ref[0:rows, j * LANES:(j + 1) * LANES] = y[:, j * LANES:(j + 1) * LANES] * slot_gate

        parts = (nv_ref[i] + MOE_PART - 1) // MOE_PART
        for n_parts in range(1, MOE_BLK // MOE_PART + 1):
            @pl.when(parts == n_parts)
            def _():
                rows = n_parts * MOE_PART
                expert_rows(rows)
                if rows < MOE_BLK:
                    y_ref[rows:MOE_BLK, :] = jnp.zeros((MOE_BLK - rows, D_MODEL), F32)

    @pl.when(i >= n_used)
    def _():
        y_ref[...] = jnp.zeros_like(y_ref)


def _bias_split_kernel(b_ref, sel_ref, o_ref):
    for c in range(2 * EXPERT_FF // (2 * LANES)):
        terms = _split3(b_ref[:, c * 2 * LANES:(c + 1) * 2 * LANES])
        r = functools.reduce(jnp.add, [_dot(t, sel_ref[...]) for t in terms])
        o_ref[:, c * LANES:(c + 1) * LANES] = r[:, 0:LANES]
        o_ref[:, EXPERT_FF + c * LANES:EXPERT_FF + (c + 1) * LANES] = r[:, LANES:2 * LANES]


def _bias_split(b_gu, sel):
    n = DEPTH * N_EXPERTS
    out = pl.pallas_call(
        _bias_split_kernel,
        out_shape=jax.ShapeDtypeStruct((n, 2 * EXPERT_FF), F32),
        name="bias_split",
    )(b_gu.reshape(n, 2 * EXPERT_FF), sel)
    return out.reshape(n, 1, 2 * EXPERT_FF)


def _moe(layer, rt, xs, w_gu, w_down, b_gu_split, b_down, sel):
    first = layer * N_EXPERTS
    bg = pl.BlockSpec((1, 1, EXPERT_FF), lambda i, be, *_: (first + be[i], 0, 0))
    bu = pl.BlockSpec((1, 1, EXPERT_FF), lambda i, be, *_: (first + be[i], 0, 1))
    bd = pl.BlockSpec((1, 1, D_MODEL), lambda i, be, *_: (first + be[i], 0, 0))
    return pl.pallas_call(
        functools.partial(_moe_kernel, layer),
        grid_spec=pltpu.PrefetchScalarGridSpec(
            num_scalar_prefetch=4,
            grid=(N_BLOCKS,),
            in_specs=[
                pl.BlockSpec((MOE_BLK, XS_COLS), lambda i, be, nx, nv, nu: (jnp.minimum(i, nu[0] - 1), 0)),
                pl.BlockSpec(memory_space=pl.ANY),
                pl.BlockSpec(memory_space=pl.ANY),
                bg, bu, bd,
                pl.BlockSpec((2 * LANES, 2 * LANES), lambda i, *_: (0, 0)),
            ],
            out_specs=pl.BlockSpec((MOE_BLK, D_MODEL), lambda i, *_: (i, 0)),
            scratch_shapes=[
                pltpu.VMEM((D_MODEL, 2 * EXPERT_FF), F32),
                pltpu.VMEM((EXPERT_FF, D_MODEL), F32),
                pltpu.VMEM((D_MODEL, 2 * EXPERT_FF), BF16),
                pltpu.VMEM((EXPERT_FF, D_MODEL), BF16),
                pltpu.SemaphoreType.DMA((2,)),
            ],
        ),
        out_shape=jax.ShapeDtypeStruct((N_SLOTS, D_MODEL), F32),
        compiler_params=pltpu.CompilerParams(dimension_semantics=("arbitrary",), vmem_limit_bytes=VMEM_LIMIT),
        name="moe",
    )(rt["blk_e"], rt["blk_next"], rt["blk_rows"], rt["n_used"], xs, w_gu, w_down, b_gu_split, b_gu_split,
      b_down.reshape(DEPTH * N_EXPERTS, 1, D_MODEL), sel)


def _combine_kernel(cnt_ref, off_ref, run_ref, y_ref, lp_ref, x1_ref, mod_ref, fw_ref,
                    xc_ref, xl_ref, yc_ref, yl_ref, stage, mine_s, sem):
    i = pl.program_id(0)
    slot = i % 2

    def copy_in(s_):
        return lambda s, d, n: pltpu.make_async_copy(y_ref.at[pl.ds(d, n), :], stage.at[s_, pl.ds(s, n), :],
                                                     sem.at[s_])

    @pl.when(i == 0)
    def _():
        stage[...] = jnp.zeros_like(stage)
        _run_copies(cnt_ref, off_ref, run_ref, 0, N_EXPERTS, RUN_BITS, copy_in(0), True)

    @pl.when(i + 1 < N_TILES)
    def _():
        _run_copies(cnt_ref, off_ref, run_ref, (i + 1) * N_EXPERTS, N_EXPERTS, RUN_BITS, copy_in(1 - slot), True)

    last = i * N_EXPERTS + N_EXPERTS - 1
    _wait_rows(off_ref[last] + cnt_ref[last], copy_in(slot))
    lane = lax.broadcasted_iota(I32, (TM, LANES), 1)
    mine = [jnp.broadcast_to(lp_ref[:, k:k + 1], (TM, LANES)) for k in range(TOP_K)]
    for c in range(STAGE_ROWS // LANES):
        hit = jnp.zeros((TM, LANES), F32)
        for k in range(TOP_K):
            hit = jnp.where(lane + c * LANES == mine[k], 1.0, hit)
        mine_s[:, c * LANES:(c + 1) * LANES] = hit.astype(BF16)
    moe = _dot(mine_s[...], stage[slot].astype(BF16))
    gate2 = mod_ref[0][:, 5 * D_MODEL:6 * D_MODEL]
    x2 = x1_ref[...] + gate2 * moe
    y_norm = _rms(x2, fw_ref[...])

    @pl.when(i < CTX_TILES)
    def _():
        xc_ref[...] = x2
        yc_ref[...] = y_norm

    @pl.when(i >= CTX_TILES)
    def _():
        xl_ref[...] = x2
        yl_ref[...] = y_norm


def _combine(rt, y_slots, lp, x1, mod3, final_w):
    row = lambda n: pl.BlockSpec((TM, n), lambda i, *_: (i, 0))
    ctx_row = pl.BlockSpec((TM, D_MODEL), lambda i, *_: (jnp.minimum(i, CTX_TILES - 1), 0))
    lat_row = pl.BlockSpec((TM, D_MODEL), lambda i, *_: (jnp.maximum(i - CTX_TILES, 0), 0))
    return pl.pallas_call(
        _combine_kernel,
        grid_spec=pltpu.PrefetchScalarGridSpec(
            num_scalar_prefetch=3,
            grid=(N_TILES,),
            in_specs=[
                pl.BlockSpec(memory_space=pl.ANY),
                row(LANES), row(D_MODEL),
                pl.BlockSpec((1, 1, 6 * D_MODEL), lambda i, *_: (_mod_row(i), 0, 0)),
                pl.BlockSpec((1, D_MODEL), lambda i, *_: (0, 0)),
            ],
            out_specs=(ctx_row, lat_row, ctx_row, lat_row),
            scratch_shapes=[pltpu.VMEM((2, STAGE_ROWS, D_MODEL), F32), pltpu.VMEM((TM, STAGE_ROWS), BF16),
                            pltpu.SemaphoreType.DMA((2,))],
        ),
        out_shape=(jax.ShapeDtypeStruct((N_CTX, D_MODEL), F32), jax.ShapeDtypeStruct((N_LAT, D_MODEL), F32),
                   jax.ShapeDtypeStruct((N_CTX, D_MODEL), F32), jax.ShapeDtypeStruct((N_LAT, D_MODEL), F32)),
        compiler_params=pltpu.CompilerParams(dimension_semantics=("arbitrary",), vmem_limit_bytes=VMEM_LIMIT),
        name="combine",
    )(rt["cnt"], rt["off"], rt["run"], y_slots, lp, x1, mod3, final_w)


def _rope_tables():
    rows = DEC_SEQ // GRID_W
    t = jnp.arange(rows * GRID_W)
    row = (t // GRID_W).astype(F32)
    col = (t % GRID_W).astype(F32)
    half = QK_ROPE // 2
    inv = ROPE_BASE ** (-jnp.arange(0, half, 2, dtype=F32) / half)
    ang_r, ang_c = row[:, None] * inv, col[:, None] * inv
    cos32 = jnp.concatenate([jnp.cos(ang_r), jnp.cos(ang_r), jnp.cos(ang_c), jnp.cos(ang_c)], axis=-1)
    sin32 = jnp.concatenate([-jnp.sin(ang_r), jnp.sin(ang_r), -jnp.sin(ang_c), jnp.sin(ang_c)], axis=-1)
    cos32 = jnp.concatenate([jnp.ones((TM, QK_ROPE), F32), cos32], axis=0)
    sin32 = jnp.concatenate([jnp.zeros((TM, QK_ROPE), F32), sin32], axis=0)
    n = cos32.shape[0]
    pad = HEAD_PAD - QK_NOPE - QK_ROPE
    cos_h = jnp.concatenate([jnp.ones((n, QK_NOPE), F32), cos32, jnp.zeros((n, pad), F32)], axis=-1)
    sin_h = jnp.concatenate([jnp.zeros((n, QK_NOPE), F32), sin32, jnp.zeros((n, pad), F32)], axis=-1)
    zeros = jnp.zeros((n, LANES - QK_ROPE), F32)
    return {
        "cosq": cos_h, "sinq": sin_h,
        "cosk": jnp.concatenate([cos32, zeros], axis=-1), "sink": jnp.concatenate([sin32, zeros], axis=-1),
        "vone": jnp.zeros((MLA_HEADS, HEAD_PAD), F32).at[:, V_HEAD].set(1.0).reshape(1, MLA_HEADS * HEAD_PAD),
    }


def _layer_weights(p, l):
    hd = QK_NOPE + QK_ROPE
    pad = HEAD_PAD - hd
    wq = p["w_uq"][l].reshape(Q_RANK, MLA_HEADS, hd)
    wq_a = jnp.pad(wq, ((0, 0), (0, 0), (0, pad)))
    wuk = jnp.pad(p["w_uk"][l], ((0, 0), (0, 0), (0, HEAD_PAD - QK_NOPE)))
    rw = jnp.pad(p["router_w"][l], ((0, 0), (0, LANES - N_EXPERTS)))
    rb = jnp.concatenate([p["router_b"][l], jnp.full((LANES - N_EXPERTS,), -jnp.inf, F32)])
    return {
        "norm1_w": p["norm1_w"][l][None], "q_norm_w": p["q_norm_w"][l][None],
        "wq_a": wq_a.reshape(Q_RANK, -1).astype(BF16),
        "kv_norm_w": p["kv_norm_w"][l][None],
        "wuk": wuk.reshape(KV_RANK, -1).astype(BF16),
        "wuv": jnp.pad(p["w_uv"][l], ((0, 0), (0, 0), (0, HEAD_PAD - V_HEAD))).reshape(KV_RANK, -1).astype(BF16),
        "wuv_c": p["w_uv"][l].reshape(KV_RANK, -1).astype(BF16),
        "conv_a_w": p["conv_a_w"][l], "ssm_conv_w": p["ssm_conv_w"][l], "ssm_conv_b": p["ssm_conv_b"][l][None],
        "w_out_a": p["w_out_a"][l].astype(BF16), "w_o_mla": p["w_o_mla"][l].astype(BF16),
        "d_skip": jnp.repeat(p["d_skip"][l], SSM_HEAD_DIM)[None], "ssm_norm_w": p["ssm_norm_w"][l][None],
        "w_o_ssm": p["w_o_ssm"][l].astype(BF16), "w_o": p["w_o"][l].astype(BF16),
        "norm2_w": p["norm2_w"][l][None], "router_w": rw.astype(BF16), "router_b": rb[None],
    }


def _ssd_params(p, l):
    z = lambda n: jnp.zeros((n,), F32)
    lanes = lambda f, b: jnp.concatenate([z(SM_DTF), f, b, z(LANES - SM_DTB - SSM_HEADS)])
    ones = jnp.ones((SSM_HEADS,), F32)
    rows = [lanes(p["dt_bias_fwd"][l], p["dt_bias_bwd"][l]), lanes(p["a_log_fwd"][l], p["a_log_bwd"][l]),
            lanes(ones, ones)]
    return jnp.concatenate([jnp.stack(rows), jnp.zeros((SUBLANES - 3, LANES), F32)], axis=0)


def _routing(cnt_tiles):
    cnt = cnt_tiles[:, 0, 0:N_EXPERTS]
    cnt = (cnt + RUN_ALIGN - 1) // RUN_ALIGN * RUN_ALIGN
    per_expert = jnp.sum(cnt, axis=0)
    padded = (per_expert + MOE_BLK - 1) // MOE_BLK * MOE_BLK
    pad_end = jnp.cumsum(padded)
    pad_start = pad_end - padded
    run = pad_start[None, :] + jnp.cumsum(cnt, axis=0) - cnt
    off = jnp.cumsum(cnt, axis=1) - cnt
    starts = jnp.arange(N_BLOCKS, dtype=I32) * MOE_BLK
    blk_e = jnp.minimum(jnp.sum((pad_end[None, :] <= starts[:, None]).astype(I32), axis=1), N_EXPERTS - 1)
    off_v = jnp.zeros((N_TILES, SUBLANES, LANES), F32).at[:, 0, 0:N_EXPERTS].set(off.astype(F32))
    ids = jnp.arange(N_EXPERTS, dtype=I32)
    later = jnp.logical_and(ids[None, :] > ids[:, None], padded[None, :] > 0)
    nxt = jnp.min(jnp.where(later, ids[None, :], N_EXPERTS), axis=1)
    nxt = jnp.where(nxt == N_EXPERTS, -1, nxt)
    per_block = lambda v: jnp.sum(jnp.where(blk_e[:, None] == ids[None, :], v[None, :], 0), axis=1)
    blk_rows = jnp.clip(per_block(pad_start + per_expert) - starts, 0, MOE_BLK)
    return {
        "blk_next": per_block(nxt).astype(I32), "blk_rows": blk_rows.astype(I32),
        "cnt": cnt.reshape(-1).astype(I32), "off": off.reshape(-1).astype(I32), "run": run.reshape(-1).astype(I32),
        "tail_cnt": (padded - per_expert).astype(I32), "tail_dst": (pad_start + per_expert).astype(I32),
        "blk_e": blk_e.astype(I32), "n_used": (pad_end[-1] // MOE_BLK).astype(I32).reshape(1), "off_v": off_v,
    }


def _deinterleave_matrix():
    k = jnp.arange(2 * LANES)[:, None]
    n = jnp.arange(2 * LANES)[None, :]
    src = jnp.where(n < LANES, 2 * n, 2 * (n - LANES) + 1)
    return (k == src).astype(BF16)


def kernel(x_prompt, x_sample, cache_ckv, cache_kpe, state_ssm_fwd, state_ssm_bwd, c, c_ctx, w_ada, b_ada, norm1_w, w_in, conv_a_w, w_out_a, q_norm_w, w_uq, kv_norm_w, w_uk, w_uv, w_o_mla, ssm_conv_w, ssm_conv_b, dt_bias_fwd, dt_bias_bwd, a_log_fwd, a_log_bwd, d_skip, ssm_norm_w, w_o_ssm, w_o, norm2_w, router_w, router_b, w_gu, b_gu, w_down, b_down, final_norm_w):
    p = dict(norm1_w=norm1_w, w_in=w_in, conv_a_w=conv_a_w, w_out_a=w_out_a, q_norm_w=q_norm_w, w_uq=w_uq,
             kv_norm_w=kv_norm_w, w_uk=w_uk, w_uv=w_uv, w_o_mla=w_o_mla, ssm_conv_w=ssm_conv_w,
             ssm_conv_b=ssm_conv_b, dt_bias_fwd=dt_bias_fwd, dt_bias_bwd=dt_bias_bwd, a_log_fwd=a_log_fwd,
             a_log_bwd=a_log_bwd, d_skip=d_skip, ssm_norm_w=ssm_norm_w, w_o_ssm=w_o_ssm, w_o=w_o,
             norm2_w=norm2_w, router_w=router_w, router_b=router_b, b_gu=b_gu, b_down=b_down)
    xc = x_prompt.reshape(N_CTX, D_MODEL)
    xl = x_sample.reshape(N_LAT, D_MODEL)
    cond8 = jnp.concatenate([c_ctx[None], c, jnp.zeros((SUBLANES - 1 - DEC_BATCH, D_MODEL), F32)], axis=0)
    mods = _ada_mods(cond8, w_ada, b_ada)
    tabs = _rope_tables()
    sel = _deinterleave_matrix()
    b_gu_split = _bias_split(b_gu, sel)
    w_in_t = jnp.swapaxes(w_in, 1, 2)
    final_w = final_norm_w[None]
    hp = SSM_INNER

    caches, states = (), ()
    y_ctx = y_lat = None
    for l in range(DEPTH):
        lw = _layer_weights(p, l)
        mod3 = mods[l].reshape(SUBLANES, 1, 6 * D_MODEL)
        ta, q, ckv, kpe, kf, v_ctx, v_lat, small, z, xact, g = _inproj(l, xc, xl, mod3, w_in_t, lw, tabs, caches)
        caches = (ckv, kpe)
        kpe_c = jnp.pad(cache_kpe[:, l].reshape(DEC_BATCH * PAST_LEN, QK_ROPE), ((0, 0), (0, LANES - QK_ROPE)))
        kf_c, v_c = _kvcache(cache_ckv[:, l].reshape(DEC_BATCH * PAST_LEN, KV_RANK), kpe_c, lw, tabs)
        att_ctx, att_lat = _attention(q, kf, v_ctx, v_lat, kf_c, v_c)
        init_f = state_ssm_fwd[:, l].reshape(DEC_BATCH, hp, SSM_STATE)
        init_b = state_ssm_bwd[:, l].reshape(DEC_BATCH, hp, SSM_STATE)
        yf, yb, sf, sb = _ssd(l, xact, small, init_f, init_b, _ssd_params(p, l), states)
        states = (sf, sb)
        x1, h2, top_i, top_g, cnt_tiles = _merge(ta, att_ctx, att_lat, yf, yb, xact, z, g, xc, xl, mod3, lw)
        rt = _routing(cnt_tiles)
        xs, lp = _dispatch(rt, h2, top_i, top_g)
        y_slots = _moe(l, rt, xs, w_gu, w_down, b_gu_split, b_down, sel)
        xc, xl, y_ctx, y_lat = _combine(rt, y_slots, lp, x1, mod3, final_w)

    y_prompt = y_ctx.reshape(BATCH, SEQ, D_MODEL)
    y_sample = y_lat.reshape(DEC_BATCH, DEC_SEQ, D_MODEL)
    state_shape = (BATCH, DEPTH, SSM_HEADS, SSM_HEAD_DIM, SSM_STATE)
    return (y_prompt, y_sample, caches[0], caches[1], states[0].reshape(state_shape), states[1].reshape(state_shape))
```

```python
import functools
import math

import jax
import jax.numpy as jnp
from jax import lax
from jax.experimental import pallas as pl
from jax.experimental.pallas import tpu as pltpu

F32 = jnp.float32
BF16 = jnp.bfloat16
I32 = jnp.int32

D_MODEL = 1024
BATCH = 16
SEQ = 256
DEPTH = 2
DEC_BATCH = 2
DEC_SEQ = 2048
PAST_LEN = 512
GRID_W = 64
NORM_EPS = 1e-6
CONV_DIM = 512
MLA_HEADS = 8
Q_RANK = 384
KV_RANK = 256
QK_NOPE = 64
QK_ROPE = 32
V_HEAD = 64
ROPE_BASE = 10000.0
MLA_SCALE = (QK_NOPE + QK_ROPE) ** -0.5
SSM_HEADS = 16
SSM_HEAD_DIM = 64
SSM_INNER = SSM_HEADS * SSM_HEAD_DIM
SSM_GROUPS = 2
SSM_STATE = 128
SSM_CONV_CH = SSM_INNER + 2 * SSM_GROUPS * SSM_STATE
N_EXPERTS = 32
TOP_K = 4
EXPERT_FF = D_MODEL
SWIGLU_ALPHA = 1.702
SWIGLU_LIMIT = 7.0

N_CTX = BATCH * SEQ
N_LAT = DEC_BATCH * DEC_SEQ
N_TOK = N_CTX + N_LAT
N_SEQS = BATCH + DEC_BATCH

LANES = 128
SUBLANES = 8
HEAD_PAD = 128
TM = 256
N_TILES = N_TOK // TM
CTX_TILES = N_CTX // TM
LAT_TILES_PER_SEQ = DEC_SEQ // TM
TQ_LAT = 512
CHUNK = 128
SSD_STEP = 2 * CHUNK
CTX_STEPS_PER_SEQ = SEQ // SSD_STEP
LAT_STEPS_PER_SEQ = DEC_SEQ // SSD_STEP
N_CTX_STEPS = N_CTX // SSD_STEP
N_SSD_STEPS = N_TOK // SSD_STEP
MOE_BLK = 512
MOE_PART = 128
DISPATCH_TILES = 2
N_ASSIGN = N_TOK * TOP_K
RUN_ALIGN = SUBLANES
RUN_BITS = tuple(range(8, 2, -1))
TAIL_BITS = tuple(range(8, 2, -1))
COMMON_BITS = 6
WAIT_BITS = tuple(range(10, 2, -1))
STAGE_ROWS = 1280
XS_COLS = D_MODEL + LANES
N_BLOCKS = -(-(N_ASSIGN + N_TILES * N_EXPERTS * (RUN_ALIGN - 1) + N_EXPERTS * (MOE_BLK - 1)) // MOE_BLK)
N_SLOTS = N_BLOCKS * MOE_BLK
VMEM_LIMIT = 56 * 1024 * 1024

C_A3 = 0
C_CQ = C_A3 + 3 * CONV_DIM
C_CKV = C_CQ + Q_RANK
C_Z = C_CKV + KV_RANK
C_XBC = C_Z + SSM_INNER
C_GATE = C_XBC + SSM_CONV_CH
C_SMALL = C_GATE + 3 * D_MODEL
IN_COLS2 = C_SMALL + LANES
SM_DTF = QK_ROPE
SM_DTB = QK_ROPE + SSM_HEADS
S_CQ = 3 * CONV_DIM
S_CKV = S_CQ + Q_RANK
S_KPE = S_CKV + KV_RANK
S_Z = S_KPE + QK_ROPE
S_XBC = S_Z + SSM_INNER
S_DTF = S_XBC + SSM_CONV_CH
S_GATE = S_DTF + 2 * SSM_HEADS
IN_COLS = S_GATE + 3 * D_MODEL
W_SEGMENTS = ((C_A3, 0, 3 * CONV_DIM), (C_CQ, S_CQ, Q_RANK), (C_CKV, S_CKV, KV_RANK), (C_Z, S_Z, SSM_INNER),
              (C_XBC, S_XBC, SSM_CONV_CH), (C_GATE, S_GATE, 3 * D_MODEL))
W_PIECE = 512


def _rms(x, w):
    return x * lax.rsqrt(jnp.mean(x * x, axis=-1, keepdims=True) + NORM_EPS) * w


def _silu(x):
    return x * jax.nn.sigmoid(x)


def _dot(a, b):
    return jnp.dot(a, b, preferred_element_type=F32)


def _dot_nt(a, b):
    return lax.dot_general(a, b, (((1,), (1,)), ((), ())), preferred_element_type=F32)


def _resident(shape):
    nd = len(shape)
    return pl.BlockSpec(shape, lambda *_: (0,) * nd, pipeline_mode=pl.Buffered(1))


def _mod_row(i):
    return jnp.where(i < CTX_TILES, 0, 1 + (i - CTX_TILES) // LAT_TILES_PER_SEQ)


def _pos_block(i):
    return jnp.where(i < CTX_TILES, 0, 1 + (i - CTX_TILES) % LAT_TILES_PER_SEQ)


def _ada_kernel(c_ref, w_ref, b_ref, o_ref):
    s = _silu(c_ref[...]).astype(BF16)
    o_ref[0] = _dot(s, w_ref[0].astype(BF16)) + b_ref[0]


def _ada_mods(cond8, w_ada, b_ada):
    tn = 1536
    n_mod = 6 * D_MODEL
    return pl.pallas_call(
        _ada_kernel,
        grid=(DEPTH, n_mod // tn),
        in_specs=[
            pl.BlockSpec((SUBLANES, D_MODEL), lambda l, j: (0, 0)),
            pl.BlockSpec((1, D_MODEL, tn), lambda l, j: (l, 0, j)),
            pl.BlockSpec((1, 1, tn), lambda l, j: (l, 0, j)),
        ],
        out_specs=pl.BlockSpec((1, SUBLANES, tn), lambda l, j: (l, 0, j)),
        out_shape=jax.ShapeDtypeStruct((DEPTH, SUBLANES, n_mod), F32),
        compiler_params=pltpu.CompilerParams(dimension_semantics=("arbitrary", "arbitrary")),
        name="ada_mods",
    )(cond8, w_ada, b_ada.reshape(DEPTH, 1, n_mod))


def _relayout_w_in(layer, wt_hbm, w2, stg, small_stg, sem, small_sem):
    pieces = [(dst + p, src + p, min(W_PIECE, width - p))
              for dst, src, width in W_SEGMENTS for p in range(0, width, W_PIECE)]
    fetch = lambda k: pltpu.make_async_copy(wt_hbm.at[layer, pl.ds(pieces[k][1], pieces[k][2]), :],
                                            stg.at[k % 2, pl.ds(0, pieces[k][2]), :], sem.at[k % 2])
    small_stg[...] = jnp.zeros_like(small_stg)
    small_copies = [
        pltpu.make_async_copy(wt_hbm.at[layer, pl.ds(S_KPE, QK_ROPE), :], small_stg.at[pl.ds(0, QK_ROPE), :],
                              small_sem.at[0]),
        pltpu.make_async_copy(wt_hbm.at[layer, pl.ds(S_DTF, 2 * SSM_HEADS), :],
                              small_stg.at[pl.ds(SM_DTF, 2 * SSM_HEADS), :], small_sem.at[1]),
    ]
    for cp in small_copies:
        cp.start()
    fetch(0).start()
    for k, (dst, _, width) in enumerate(pieces):
        if k + 1 < len(pieces):
            fetch(k + 1).start()
        fetch(k).wait()
        w2[:, dst:dst + width] = stg[k % 2, 0:width, :].T.astype(BF16)
    for cp in small_copies:
        cp.wait()
    w2[:, C_SMALL:IN_COLS2] = small_stg[...].T.astype(BF16)


def _place_rope_key(k_nope, kpe):
    shifted = pltpu.roll(kpe, QK_NOPE, 1)
    return jnp.concatenate([k_nope[:, h * HEAD_PAD:(h + 1) * HEAD_PAD] + shifted for h in range(MLA_HEADS)],
                           axis=1).astype(BF16)


def _rot_partner(x):
    n = x.shape[1]
    lane = lax.broadcasted_iota(I32, x.shape, 1)
    quarter = QK_ROPE // 4
    return jnp.where(lane % (2 * quarter) < quarter, pltpu.roll(x, n - quarter, 1), pltpu.roll(x, quarter, 1))


def _first_layer_slabs(ref, value):
    ref[0, 0] = value
    ref[0, 1:DEPTH] = jnp.zeros((DEPTH - 1,) + value.shape, value.dtype)


def _inproj_kernel(layer, *refs):
    n_in = 23 + (2 if layer > 0 else 0)
    (xc_ref, xcp_ref, xcn_ref, xl_ref, xlp_ref, xln_ref, mod_ref, n1w_ref, w_hbm, qnw_ref, wq_ref, kvnw_ref, wuk_ref,
     wuv_ref, wuvc_ref, vone_ref, cosq_ref, sinq_ref, cosk_ref, sink_ref, caw_ref, cw_ref, cb_ref) = refs[:23]
    (ta_ref, q_ref, ckv_ref, kpe_ref, kf_ref, vc_ref, vl_ref, small_ref, z_ref, xact_ref, g_ref,
     w2, stg, small_stg, sem, small_sem) = refs[n_in:]
    i = pl.program_id(0)

    @pl.when(i == 0)
    def _():
        _relayout_w_in(layer, w_hbm, w2, stg, small_stg, sem, small_sem)

    mod = mod_ref[0]
    shift1 = mod[:, 0:D_MODEL]
    scale1 = mod[:, D_MODEL:2 * D_MODEL]
    x_ext = jnp.where(i < CTX_TILES, jnp.concatenate([xcp_ref[...], xc_ref[...], xcn_ref[...]], axis=0),
                      jnp.concatenate([xlp_ref[...], xl_ref[...], xln_ref[...]], axis=0))
    h_ext = _rms(x_ext, n1w_ref[...]) * (1.0 + scale1) + shift1
    hb_ext = h_ext.astype(BF16)
    hb = h_ext[SUBLANES:SUBLANES + TM].astype(BF16)

    j = (i - CTX_TILES) % LAT_TILES_PER_SEQ
    is_ctx = i < CTX_TILES
    keep_prev = jnp.where(jnp.logical_or(is_ctx, j == 0), 0.0, 1.0)
    keep_next = jnp.where(jnp.logical_or(is_ctx, j == LAT_TILES_PER_SEQ - 1), 0.0, 1.0)
    row = lax.broadcasted_iota(I32, (TM, 1), 0)
    prev_mask = jnp.where(row == 0, keep_prev, 1.0)
    next_mask = jnp.where(row == TM - 1, keep_next, 1.0)
    ext = TM + 2 * SUBLANES

    def conv3(x, w_ref):
        x_prev = pltpu.roll(x, 1, 0)[SUBLANES:SUBLANES + TM] * prev_mask
        x_next = pltpu.roll(x, ext - 1, 0)[SUBLANES:SUBLANES + TM] * next_mask
        return x_prev * w_ref[0:1, :] + x[SUBLANES:SUBLANES + TM] * w_ref[1:2, :] + x_next * w_ref[2:3, :]

    a3 = _dot(hb_ext, w2[:, C_A3:C_CQ])
    s = a3[:, 2 * CONV_DIM:3 * CONV_DIM] * a3[:, 0:CONV_DIM]
    ta_ref[...] = (a3[SUBLANES:SUBLANES + TM, CONV_DIM:2 * CONV_DIM] * conv3(s, caw_ref)).astype(BF16)
    xact_ref[...] = _silu(conv3(_dot(hb_ext, w2[:, C_XBC:C_GATE]), cw_ref) + cb_ref[...])

    def seg(a, b):
        return _dot(hb, w2[:, a:b])

    cqn = _rms(seg(C_CQ, C_CKV), qnw_ref[...]).astype(BF16)
    qa = _dot(cqn, wq_ref[...])
    qa_rot = _rot_partner(qa)
    for h in range(MLA_HEADS):
        hs = slice(h * HEAD_PAD, (h + 1) * HEAD_PAD)
        q_ref[:, hs] = (qa[:, hs] * cosq_ref[...] + qa_rot[:, hs] * sinq_ref[...]).astype(BF16)

    ckv = _rms(seg(C_CKV, C_Z), kvnw_ref[...])
    ckvb = ckv.astype(BF16)
    small = seg(C_SMALL, IN_COLS2)
    small_ref[...] = small
    kpe = small * cosk_ref[...] + _rot_partner(small) * sink_ref[...]
    kf_ref[...] = _place_rope_key(_dot(ckvb, wuk_ref[...]), kpe)
    v_lat = (_dot(ckvb, wuv_ref[...]) + vone_ref[...]).astype(BF16)
    v_ctx = _dot(ckvb, wuvc_ref[...]).astype(BF16)

    z_ref[...] = seg(C_Z, C_XBC)
    g_ref[...] = jax.nn.sigmoid(seg(C_GATE, C_SMALL))

    @pl.when(i < CTX_TILES)
    def _():
        if layer == 0:
            _first_layer_slabs(ckv_ref, ckv)
            _first_layer_slabs(kpe_ref, small[:, 0:QK_ROPE])
        else:
            ckv_ref[0] = ckv
            kpe_ref[0] = small[:, 0:QK_ROPE]
        vc_ref[...] = v_ctx

    @pl.when(i >= CTX_TILES)
    def _():
        vl_ref[...] = v_lat


def _inproj(layer, xc, xl, mod3, w_in, lw, tabs, caches):
    row = lambda n: pl.BlockSpec((TM, n), lambda i: (i, 0))
    tab = lambda n: pl.BlockSpec((TM, n), lambda i: (_pos_block(i), 0))
    per = TM // SUBLANES
    qw = MLA_HEADS * HEAD_PAD
    ctx_row = lambda n: pl.BlockSpec((TM, n), lambda i: (jnp.minimum(i, CTX_TILES - 1), 0))
    lat_row = lambda n: pl.BlockSpec((TM, n), lambda i: (jnp.maximum(i - CTX_TILES, 0), 0))

    def halo(first_tile, n_rows, side):
        last = n_rows // SUBLANES - 1
        return pl.BlockSpec((SUBLANES, D_MODEL),
                            lambda i: (jnp.clip((i - first_tile + side) * per - 1 + side, 0, last), 0))

    out_shape = (
        jax.ShapeDtypeStruct((N_TOK, CONV_DIM), BF16),
        jax.ShapeDtypeStruct((N_TOK, qw), BF16),
        jax.ShapeDtypeStruct((BATCH, DEPTH, SEQ, KV_RANK), F32),
        jax.ShapeDtypeStruct((BATCH, DEPTH, SEQ, QK_ROPE), F32),
        jax.ShapeDtypeStruct((N_TOK, qw), BF16),
        jax.ShapeDtypeStruct((N_CTX, MLA_HEADS * V_HEAD), BF16),
        jax.ShapeDtypeStruct((N_LAT, qw), BF16),
        jax.ShapeDtypeStruct((N_TOK, LANES), F32),
        jax.ShapeDtypeStruct((N_TOK, SSM_INNER), F32),
        jax.ShapeDtypeStruct((N_TOK, SSM_CONV_CH), F32),
        jax.ShapeDtypeStruct((N_TOK, 3 * D_MODEL), F32),
    )
    out_specs = [row(s.shape[-1]) for s in out_shape]
    cache_block = (1, DEPTH, SEQ) if layer == 0 else (1, None, SEQ)
    cache_index = lambda i: (jnp.minimum(i, CTX_TILES - 1), 0 if layer == 0 else layer, 0, 0)
    out_specs[2] = pl.BlockSpec(cache_block + (KV_RANK,), cache_index)
    out_specs[3] = pl.BlockSpec(cache_block + (QK_ROPE,), cache_index)
    extra = {} if layer == 0 else dict(input_output_aliases={23: 2, 24: 3})
    out_specs[5] = ctx_row(MLA_HEADS * V_HEAD)
    out_specs[6] = lat_row(qw)
    return pl.pallas_call(
        functools.partial(_inproj_kernel, layer),
        grid=(N_TILES,),
        in_specs=[
            ctx_row(D_MODEL), halo(0, N_CTX, 0), halo(0, N_CTX, 1),
            lat_row(D_MODEL), halo(CTX_TILES, N_LAT, 0), halo(CTX_TILES, N_LAT, 1),
            pl.BlockSpec((1, 1, 6 * D_MODEL), lambda i: (_mod_row(i), 0, 0)),
            _resident((1, D_MODEL)),
            pl.BlockSpec(memory_space=pl.ANY),
            _resident((1, Q_RANK)),
            _resident((Q_RANK, qw)),
            _resident((1, KV_RANK)),
            _resident((KV_RANK, qw)),
            _resident((KV_RANK, qw)),
            _resident((KV_RANK, MLA_HEADS * V_HEAD)),
            _resident((1, qw)),
            tab(HEAD_PAD), tab(HEAD_PAD), tab(LANES), tab(LANES),
            _resident((3, CONV_DIM)), _resident((3, SSM_CONV_CH)), _resident((1, SSM_CONV_CH)),
        ] + [pl.BlockSpec(memory_space=pl.ANY)] * len(caches),
        out_specs=tuple(out_specs),
        out_shape=out_shape,
        scratch_shapes=[pltpu.VMEM((D_MODEL, IN_COLS2), BF16), pltpu.VMEM((2, W_PIECE, D_MODEL), F32),
                        pltpu.VMEM((LANES, D_MODEL), F32), pltpu.SemaphoreType.DMA((2,)),
                        pltpu.SemaphoreType.DMA((2,))],
        compiler_params=pltpu.CompilerParams(dimension_semantics=("arbitrary",), vmem_limit_bytes=VMEM_LIMIT),
        name="inproj",
        **extra,
    )(xc, xc, xc, xl, xl, xl, mod3, lw["norm1_w"], w_in, lw["q_norm_w"], lw["wq_a"], lw["kv_norm_w"],
      lw["wuk"], lw["wuv"], lw["wuv_c"], tabs["vone"], tabs["cosq"], tabs["sinq"], tabs["cosk"],
      tabs["sink"], lw["conv_a_w"], lw["ssm_conv_w"], lw["ssm_conv_b"], *caches)


def _kvcache_kernel(ckv_ref, kpe_ref, wuk_ref, wuv_ref, vone_ref, kf_ref, v_ref):
    ckvb = ckv_ref[...].astype(BF16)
    kf_ref[...] = _place_rope_key(_dot(ckvb, wuk_ref[...]), kpe_ref[...])
    v_ref[...] = (_dot(ckvb, wuv_ref[...]) + vone_ref[...]).astype(BF16)


def _kvcache(ckv, kpe128, lw, tabs):
    n = ckv.shape[0]
    qw = MLA_HEADS * HEAD_PAD
    return pl.pallas_call(
        _kvcache_kernel,
        grid=(n // PAST_LEN,),
        in_specs=[
            pl.BlockSpec((PAST_LEN, KV_RANK), lambda i: (i, 0)),
            pl.BlockSpec((PAST_LEN, LANES), lambda i: (i, 0)),
            _resident((KV_RANK, qw)),
            _resident((KV_RANK, qw)),
            _resident((1, qw)),
        ],
        out_specs=(pl.BlockSpec((PAST_LEN, qw), lambda i: (i, 0)), pl.BlockSpec((PAST_LEN, qw), lambda i: (i, 0))),
        out_shape=(jax.ShapeDtypeStruct((n, qw), BF16), jax.ShapeDtypeStruct((n, qw), BF16)),
        compiler_params=pltpu.CompilerParams(dimension_semantics=("arbitrary",)),
        name="kvcache",
    )(ckv, kpe128, lw["wuk"], lw["wuv"], tabs["vone"])


def _attn_heads(q_ref, kv_refs, o_ref, acc_ref, denom_from_matmul):
    log2_scale = MLA_SCALE * math.log2(math.e)
    for h in range(MLA_HEADS):
        hs = slice(h * HEAD_PAD, (h + 1) * HEAD_PAD)
        qh = q_ref[:, hs]
        ss = [_dot_nt(qh, k_ref[:, hs]) for k_ref, _ in kv_refs]
        if denom_from_matmul:
            m = functools.reduce(jnp.maximum, [jnp.max(s, axis=-1, keepdims=True) for s in ss])
            ps = [jnp.exp2((s - m) * log2_scale) for s in ss]
            ol = functools.reduce(jnp.add, [_dot(p.astype(BF16), v_ref[:, hs])
                                            for p, (_, v_ref) in zip(ps, kv_refs)])
            acc_ref[:, h * V_HEAD:(h + 1) * V_HEAD] = ol[:, 0:V_HEAD] / ol[:, V_HEAD:V_HEAD + 1]
        else:
            ss = [s * MLA_SCALE for s in ss]
            m = functools.reduce(jnp.maximum, [jnp.max(s, axis=-1, keepdims=True) for s in ss])
            ps = [jnp.exp(s - m) for s in ss]
            l = functools.reduce(jnp.add, [jnp.sum(p, axis=-1, keepdims=True) for p in ps])
            o = functools.reduce(jnp.add, [_dot(p.astype(BF16), v_ref[:, h * V_HEAD:(h + 1) * V_HEAD])
                                           for p, (_, v_ref) in zip(ps, kv_refs)])
            acc_ref[:, h * V_HEAD:(h + 1) * V_HEAD] = o / l
    o_ref[...] = acc_ref[...].astype(BF16)


def _attn_ctx_kernel(q_ref, k_ref, v_ref, o_ref, acc_ref):
    _attn_heads(q_ref, [(k_ref, v_ref)], o_ref, acc_ref, denom_from_matmul=False)


def _attn_lat_kernel(q_ref, k_ref, v_ref, kc_ref, vc_ref, o_ref, acc_ref):
    _attn_heads(q_ref, [(k_ref, v_ref), (kc_ref, vc_ref)], o_ref, acc_ref, denom_from_matmul=True)


def _attention(q, kf, v_ctx, v_lat, kf_c, v_c):
    qw = MLA_HEADS * HEAD_PAD
    vw = MLA_HEADS * V_HEAD
    att_ctx = pl.pallas_call(
        _attn_ctx_kernel,
        grid=(BATCH,),
        in_specs=[pl.BlockSpec((SEQ, qw), lambda b: (b, 0)), pl.BlockSpec((SEQ, qw), lambda b: (b, 0)),
                  pl.BlockSpec((SEQ, vw), lambda b: (b, 0))],
        out_specs=pl.BlockSpec((SEQ, vw), lambda b: (b, 0)),
        out_shape=jax.ShapeDtypeStruct((N_CTX, vw), BF16),
        scratch_shapes=[pltpu.VMEM((SEQ, vw), F32)],
        compiler_params=pltpu.CompilerParams(dimension_semantics=("arbitrary",)),
        name="attn_ctx",
    )(q, kf, v_ctx)
    lat0 = N_CTX // DEC_SEQ
    q_tiles = DEC_SEQ // TQ_LAT
    att_lat = pl.pallas_call(
        _attn_lat_kernel,
        grid=(DEC_BATCH, q_tiles),
        in_specs=[
            pl.BlockSpec((TQ_LAT, qw), lambda b, t: (N_CTX // TQ_LAT + b * q_tiles + t, 0)),
            pl.BlockSpec((DEC_SEQ, qw), lambda b, t: (lat0 + b, 0)),
            pl.BlockSpec((DEC_SEQ, qw), lambda b, t: (b, 0)),
            pl.BlockSpec((PAST_LEN, qw), lambda b, t: (b, 0)),
            pl.BlockSpec((PAST_LEN, qw), lambda b, t: (b, 0)),
        ],
        out_specs=pl.BlockSpec((TQ_LAT, vw), lambda b, t: (b * q_tiles + t, 0)),
        out_shape=jax.ShapeDtypeStruct((N_LAT, vw), BF16),
        scratch_shapes=[pltpu.VMEM((TQ_LAT, vw), F32)],
        compiler_params=pltpu.CompilerParams(dimension_semantics=("arbitrary", "arbitrary"),
                                             vmem_limit_bytes=VMEM_LIMIT),
        name="attn_lat",
    )(q, kf, v_lat, kf_c, v_c)
    return att_ctx, att_lat


def _seq_of_step(s):
    return jnp.where(s < N_CTX_STEPS, s // CTX_STEPS_PER_SEQ,
                     BATCH + (s - N_CTX_STEPS) // LAT_STEPS_PER_SEQ)


def _step_in_seq(s):
    return jnp.where(s < N_CTX_STEPS, s % CTX_STEPS_PER_SEQ, (s - N_CTX_STEPS) % LAT_STEPS_PER_SEQ)


def _steps_in_seq(s):
    return jnp.where(s < N_CTX_STEPS, CTX_STEPS_PER_SEQ, LAT_STEPS_PER_SEQ)


def _mirror_step(s):
    return s + _steps_in_seq(s) - 1 - 2 * _step_in_seq(s)


def _split3(a):
    a1 = a.astype(BF16)
    r1 = a - a1.astype(F32)
    a2 = r1.astype(BF16)
    a3 = (r1 - a2.astype(F32)).astype(BF16)
    return a1, a2, a3


def _ssd_direction(x_ref, sm_ref, par_ref, st_ref, y_ref, lane0, backward, r0):
    rows = slice(r0, r0 + CHUNK)
    ri = lax.broadcasted_iota(I32, (CHUNK, CHUNK), 0)
    ci = lax.broadcasted_iota(I32, (CHUNK, CHUNK), 1)
    tri = (ci >= ri) if backward else (ci <= ri)
    tri_b = jnp.where(tri, 1.0, 0.0).astype(BF16)
    tot_row = 0 if backward else CHUNK - 1

    dt = jax.nn.softplus(sm_ref[rows, :] + par_ref[0:1, :])
    a = dt * (-jnp.exp(par_ref[1:2, :])) * par_ref[2:3, :]
    a1, a2, a3 = _split3(a)
    acs = (_dot(tri_b, a1) + _dot(tri_b, a2) + _dot(tri_b, a3)) * math.log2(math.e)
    acs_t = acs.T
    dt_t = dt.T
    first_head = lax.broadcasted_iota(I32, (1, LANES), 1) < SSM_HEAD_DIM

    def block_diag(pair):
        return jnp.concatenate([jnp.where(first_head, pair, 0.0), jnp.where(first_head, 0.0, pair)],
                               axis=0).astype(BF16)

    for g in range(SSM_GROUPS):
        b0 = SSM_INNER + g * SSM_STATE
        c0 = SSM_INNER + SSM_GROUPS * SSM_STATE + g * SSM_STATE
        bg = x_ref[rows, b0:b0 + SSM_STATE]
        cg = x_ref[rows, c0:c0 + SSM_STATE]
        cb = _dot_nt(cg.astype(BF16), bg.astype(BF16))
        bg_t = bg.T
        heads = SSM_HEADS // SSM_GROUPS
        for pr in range(heads // 2):
            h0 = g * heads + 2 * pr
            sl = slice(h0 * SSM_HEAD_DIM, (h0 + 2) * SSM_HEAD_DIM)
            x_bd = block_diag(x_ref[rows, sl])
            st_old = st_ref[:, sl]
            within, carried, to_state, keep = [], [], [], []
            for h in (h0, h0 + 1):
                lane = lane0 + h
                col = jnp.broadcast_to(acs[:, lane:lane + 1], (CHUNK, CHUNK))
                row = acs_t[lane:lane + 1, :]
                dt_row = dt_t[lane:lane + 1, :]
                decay = jnp.exp2(jnp.where(tri, col - row, -jnp.inf))
                within.append((cb * decay * dt_row).astype(BF16))
                carried.append((cg * jnp.exp2(col)).astype(BF16))
                tot = acs[tot_row:tot_row + 1, lane:lane + 1]
                to_state.append((bg_t * (dt_row * jnp.exp2(tot - row))).astype(BF16))
                keep.append(jnp.exp2(tot))
            lhs = jnp.concatenate(within + carried, axis=1)
            y_ref[rows, sl] = _dot(lhs, jnp.concatenate([x_bd, block_diag(st_old)], axis=0))
            st_ref[:, sl] = (st_old * jnp.where(first_head, keep[0], keep[1])
                             + _dot(jnp.concatenate(to_state, axis=1), x_bd))


def _ssd_kernel(layer, *refs):
    n_in = 7 + (2 if layer > 0 else 0)
    xf_ref, xb_ref, smf_ref, smb_ref, if_ref, ib_ref, par_ref = refs[:7]
    yf_ref, yb_ref, sf_ref, sb_ref, stf_ref, stb_ref = refs[n_in:]
    s = pl.program_id(0)
    c = _step_in_seq(s)

    @pl.when(jnp.logical_and(c == 0, s < N_CTX_STEPS))
    def _():
        stf_ref[...] = jnp.zeros_like(stf_ref)
        stb_ref[...] = jnp.zeros_like(stb_ref)

    @pl.when(jnp.logical_and(c == 0, s >= N_CTX_STEPS))
    def _():
        stf_ref[...] = if_ref[0].T
        stb_ref[...] = ib_ref[0].T

    for r0 in range(0, SSD_STEP, CHUNK):
        _ssd_direction(xf_ref, smf_ref, par_ref, stf_ref, yf_ref, SM_DTF, False, r0)
        _ssd_direction(xb_ref, smb_ref, par_ref, stb_ref, yb_ref, SM_DTB, True, SSD_STEP - CHUNK - r0)

    @pl.when(jnp.logical_and(c == _steps_in_seq(s) - 1, s < N_CTX_STEPS))
    def _():
        if layer == 0:
            _first_layer_slabs(sf_ref, stf_ref[...].T)
            _first_layer_slabs(sb_ref, stb_ref[...].T)
        else:
            sf_ref[0] = stf_ref[...].T
            sb_ref[0] = stb_ref[...].T


def _ssd(layer, xact, small, init_f, init_b, par, states):
    hp = SSM_INNER
    fwd = lambda n: pl.BlockSpec((SSD_STEP, n), lambda s: (s, 0))
    bwd = lambda n: pl.BlockSpec((SSD_STEP, n), lambda s: (_mirror_step(s), 0))
    st = pl.BlockSpec((1, DEPTH, hp, SSM_STATE) if layer == 0 else (1, None, hp, SSM_STATE),
                      lambda s: (jnp.minimum(_seq_of_step(s), BATCH - 1), 0 if layer == 0 else layer, 0, 0))
    init = pl.BlockSpec((1, hp, SSM_STATE), lambda s: (jnp.maximum(_seq_of_step(s) - BATCH, 0), 0, 0))
    extra = {} if layer == 0 else dict(input_output_aliases={7: 2, 8: 3})
    return pl.pallas_call(
        functools.partial(_ssd_kernel, layer),
        grid=(N_SSD_STEPS,),
        in_specs=[fwd(SSM_CONV_CH), bwd(SSM_CONV_CH), fwd(LANES), bwd(LANES), init, init,
                  pl.BlockSpec((SUBLANES, LANES), lambda s: (0, 0))] + [pl.BlockSpec(memory_space=pl.ANY)] * len(states),
        out_specs=(fwd(hp), bwd(hp), st, st),
        out_shape=(jax.ShapeDtypeStruct((N_TOK, hp), F32), jax.ShapeDtypeStruct((N_TOK, hp), F32),
                   jax.ShapeDtypeStruct((BATCH, DEPTH, hp, SSM_STATE), F32),
                   jax.ShapeDtypeStruct((BATCH, DEPTH, hp, SSM_STATE), F32)),
        scratch_shapes=[pltpu.VMEM((SSM_STATE, hp), F32), pltpu.VMEM((SSM_STATE, hp), F32)],
        compiler_params=pltpu.CompilerParams(dimension_semantics=("arbitrary",)),
        name="ssd",
        **extra,
    )(xact, xact, small, small, init_f, init_b, par, *states)


def _merge_kernel(ta_ref, attc_ref, attl_ref, yf_ref, yb_ref, xs_ref, z_ref, g_ref, xc_ref, xl_ref, mod_ref,
                  woa_ref, wom_ref, dsk_ref, snw_ref, wos_ref, wo_ref, n2w_ref, rw_ref, rb_ref,
                  x1_ref, h2_ref, ti_ref, tg_ref, cnt_ref):
    mod = mod_ref[0]
    gate1 = mod[:, 2 * D_MODEL:3 * D_MODEL]
    shift2 = mod[:, 3 * D_MODEL:4 * D_MODEL]
    scale2 = mod[:, 4 * D_MODEL:5 * D_MODEL]
    y_a = _dot(ta_ref[...], woa_ref[...])
    att = jnp.where(pl.program_id(0) < CTX_TILES, attc_ref[...].astype(F32), attl_ref[...].astype(F32))
    y_b = _dot(att.astype(BF16), wom_ref[...])
    y_ssm = (yf_ref[...] + yb_ref[...] + dsk_ref[...] * xs_ref[...]) * _silu(z_ref[...])
    y_c = _dot(_rms(y_ssm, snw_ref[...]).astype(BF16), wos_ref[...])
    merged = (g_ref[:, 0:D_MODEL] * y_a + g_ref[:, D_MODEL:2 * D_MODEL] * y_b
              + g_ref[:, 2 * D_MODEL:3 * D_MODEL] * y_c)
    x = jnp.where(pl.program_id(0) < CTX_TILES, xc_ref[...], xl_ref[...])
    x1 = x + gate1 * _dot(merged.astype(BF16), wo_ref[...])
    x1_ref[...] = x1
    h2 = _rms(x1, n2w_ref[...]) * (1.0 + scale2) + shift2
    h2b = h2.astype(BF16)
    h2_ref[...] = h2b

    logits = (_dot(h2b, rw_ref[...]) + rb_ref[...]).T[0:N_EXPERTS, :]
    expert = lax.broadcasted_iota(I32, (N_EXPERTS, TM), 0)
    ids, vals = [], []
    for k in range(TOP_K):
        m = jnp.max(logits, axis=0, keepdims=True)
        idx = jnp.min(jnp.where(logits == m, expert, N_EXPERTS), axis=0, keepdims=True)
        ids.append(idx)
        vals.append(m)
        logits = jnp.where(expert == idx, -jnp.inf, logits)
    es = [jnp.exp(v - vals[0]) for v in vals]
    denom = functools.reduce(jnp.add, es)
    srow = lax.broadcasted_iota(I32, (LANES, TM), 0)
    ti_t = jnp.zeros((LANES, TM), F32)
    tg_t = jnp.zeros((LANES, TM), F32)
    chosen_t = jnp.zeros((LANES, TM), F32)
    for k in range(TOP_K):
        ti_t = jnp.where(srow == k, ids[k].astype(F32), ti_t)
        tg_t = jnp.where(srow == k, es[k] / denom, tg_t)
        chosen_t = jnp.where(srow == ids[k], 1.0, chosen_t)
    ti_ref[...] = ti_t.T.astype(I32)
    tg_ref[...] = tg_t.T
    cnt_ref[0] = _dot_nt(jnp.ones((SUBLANES, TM), BF16), chosen_t.astype(BF16)).astype(I32)


def _merge(ta, att_ctx, att_lat, yf, yb, xact, z, g, xc, xl, mod3, lw):
    row = lambda n: pl.BlockSpec((TM, n), lambda i: (i, 0))
    vw = MLA_HEADS * V_HEAD
    out_shape = (jax.ShapeDtypeStruct((N_TOK, D_MODEL), F32), jax.ShapeDtypeStruct((N_TOK, D_MODEL), BF16),
                 jax.ShapeDtypeStruct((N_TOK, LANES), I32), jax.ShapeDtypeStruct((N_TOK, LANES), F32))
    cnt_shape = jax.ShapeDtypeStruct((N_TILES, SUBLANES, LANES), I32)
    cnt_spec = pl.BlockSpec((1, SUBLANES, LANES), lambda i: (i, 0, 0))
    return pl.pallas_call(
        _merge_kernel,
        grid=(N_TILES,),
        in_specs=[
            row(CONV_DIM),
            pl.BlockSpec((TM, vw), lambda i: (jnp.minimum(i, CTX_TILES - 1), 0)),
            pl.BlockSpec((TM, vw), lambda i: (jnp.maximum(i - CTX_TILES, 0), 0)),
            row(SSM_INNER), row(SSM_INNER), row(SSM_INNER),
            row(SSM_INNER), row(3 * D_MODEL),
            pl.BlockSpec((TM, D_MODEL), lambda i: (jnp.minimum(i, CTX_TILES - 1), 0)),
            pl.BlockSpec((TM, D_MODEL), lambda i: (jnp.maximum(i - CTX_TILES, 0), 0)),
            pl.BlockSpec((1, 1, 6 * D_MODEL), lambda i: (_mod_row(i), 0, 0)),
            _resident((CONV_DIM, D_MODEL)), _resident((MLA_HEADS * V_HEAD, D_MODEL)),
            _resident((1, SSM_INNER)), _resident((1, SSM_INNER)), _resident((SSM_INNER, D_MODEL)),
            _resident((D_MODEL, D_MODEL)), _resident((1, D_MODEL)),
            _resident((D_MODEL, LANES)), _resident((1, LANES)),
        ],
        out_specs=tuple(row(s.shape[1]) for s in out_shape) + (cnt_spec,),
        out_shape=out_shape + (cnt_shape,),
        compiler_params=pltpu.CompilerParams(dimension_semantics=("arbitrary",), vmem_limit_bytes=VMEM_LIMIT),
        name="merge",
    )(ta, att_ctx, att_lat, yf, yb, xact, z, g, xc, xl, mod3, lw["w_out_a"], lw["w_o_mla"], lw["d_skip"],
      lw["ssm_norm_w"],
      lw["w_o_ssm"], lw["w_o"], lw["norm2_w"], lw["router_w"], lw["router_b"])


def _run_copies(cnt_ref, src_ref, dst_ref, first, count, bits, make_copy, start):
    def body(e, carry):
        n = cnt_ref[first + e]
        s0 = src_ref[first + e] if src_ref is not None else 0
        d0 = dst_ref[first + e]

        def pieces(some_bits):
            for b in some_bits:
                above = (n >> (b + 1)) << (b + 1)

                @pl.when(((n >> b) & 1) == 1)
                def _():
                    cp = make_copy(pl.multiple_of(s0 + above, RUN_ALIGN), pl.multiple_of(d0 + above, RUN_ALIGN),
                                   1 << b)
                    if start:
                        cp.start()
                    else:
                        cp.wait()

        large = [b for b in bits if b >= COMMON_BITS]

        @pl.when(n >= (1 << COMMON_BITS))
        def _():
            pieces(large)
        pieces([b for b in bits if b < COMMON_BITS])
        return carry
    lax.fori_loop(0, count, body, 0)


def _wait_rows(total, make_copy):
    for b in WAIT_BITS:
        @pl.when(((total >> b) & 1) == 1)
        def _():
            make_copy(0, 0, 1 << b).wait()


def _dispatch_kernel(cnt_ref, off_ref, run_ref, tcnt_ref, tdst_ref, nu_ref, h2_ref, ti_ref, tg_ref, offv_ref,
                     xs_ref, lp_ref, stage, zeros, perm_s, ghi_s, glo_s, sem, semz):
    i = pl.program_id(0)

    def copy_out(s_):
        return lambda s, d, n: pltpu.make_async_copy(stage.at[s_, pl.ds(s, n), :], xs_ref.at[pl.ds(d, n), :],
                                                     sem.at[s_])

    def tile_rows(t):
        last = t * N_EXPERTS + N_EXPERTS - 1
        return off_ref[last] + cnt_ref[last]

    def zero_fill(start):
        zero_out = lambda s, d, n: pltpu.make_async_copy(zeros.at[pl.ds(0, n), :], xs_ref.at[pl.ds(d, n), :], semz)
        _run_copies(tcnt_ref, None, tdst_ref, 0, N_EXPERTS, TAIL_BITS, zero_out, start)
        zrows = zeros.shape[0]

        def body(b, carry):
            for part in range(MOE_BLK // zrows):
                cp = zero_out(0, pl.multiple_of(b * MOE_BLK + part * zrows, RUN_ALIGN), zrows)
                if start:
                    cp.start()
                else:
                    cp.wait()
            return carry
        lax.fori_loop(nu_ref[0], N_BLOCKS, body, 0)

    @pl.when(i == 0)
    def _():
        zeros[...] = jnp.zeros_like(zeros)
        zero_fill(True)

    slots = [(i % 2) * DISPATCH_TILES + u for u in range(DISPATCH_TILES)]

    @pl.when(i >= 2)
    def _():
        for u in range(DISPATCH_TILES):
            _wait_rows(tile_rows((i - 2) * DISPATCH_TILES + u), copy_out(slots[u]))

    for u in range(DISPATCH_TILES):
        tile = i * DISPATCH_TILES + u
        toks = slice(u * TM, (u + 1) * TM)
        lane = lax.broadcasted_iota(I32, (TM, LANES), 1)
        picks = [jnp.where(lane == ti_ref[toks, k:k + 1], 1.0, 0.0) for k in range(TOP_K)]
        ri = lax.broadcasted_iota(I32, (TM, TM), 0)
        ci = lax.broadcasted_iota(I32, (TM, TM), 1)
        earlier = jnp.where(ci < ri, 1.0, 0.0).astype(BF16)
        base = _dot(earlier, functools.reduce(jnp.add, picks).astype(BF16)) + offv_ref[u][0:1, :]
        lp = jnp.full((TM, LANES), -1.0, F32)
        for k in range(TOP_K):
            lp = jnp.where(lane == k, jnp.sum(picks[k] * base, axis=-1, keepdims=True), lp)
        lp_ref[toks, :] = lp.astype(I32)

        lp_t = lp.T
        tg_t = tg_ref[toks, :].T
        for c in range(STAGE_ROWS // LANES):
            rows = slice(c * LANES, (c + 1) * LANES)
            row = (lax.broadcasted_iota(I32, (LANES, TM), 0) + c * LANES).astype(F32)
            perm = jnp.zeros((LANES, TM), F32)
            gates = jnp.zeros((LANES, TM), F32)
            for k in range(TOP_K):
                hit = row == lp_t[k:k + 1, :]
                perm = jnp.where(hit, 1.0, perm)
                gates = jnp.where(hit, tg_t[k:k + 1, :], gates)
            perm_s[u, rows, :] = perm.astype(BF16)
            g_hi = gates.astype(BF16)
            ghi_s[u, rows, :] = g_hi
            glo_s[u, rows, :] = (gates - g_hi.astype(F32)).astype(BF16)
        stage[slots[u], :, 0:D_MODEL] = _dot(perm_s[u], h2_ref[toks, :])
        ones = jnp.ones((TM, LANES), BF16)
        stage[slots[u], :, D_MODEL:XS_COLS] = _dot(ghi_s[u], ones) + _dot(glo_s[u], ones)
        _run_copies(cnt_ref, off_ref, run_ref, tile * N_EXPERTS, N_EXPERTS, RUN_BITS, copy_out(slots[u]), True)

    @pl.when(i == N_TILES // DISPATCH_TILES - 1)
    def _():
        for u in range(DISPATCH_TILES):
            _wait_rows(tile_rows((i - 1) * DISPATCH_TILES + u), copy_out((1 - i % 2) * DISPATCH_TILES + u))
            _wait_rows(tile_rows(i * DISPATCH_TILES + u), copy_out(slots[u]))
        zero_fill(False)


def _dispatch(rt, h2, top_i, top_g):
    step_rows = DISPATCH_TILES * TM
    row = lambda n: pl.BlockSpec((step_rows, n), lambda i, *_: (i, 0))
    scratch = lambda: pltpu.VMEM((DISPATCH_TILES, STAGE_ROWS, TM), BF16)
    return pl.pallas_call(
        _dispatch_kernel,
        grid_spec=pltpu.PrefetchScalarGridSpec(
            num_scalar_prefetch=6,
            grid=(N_TILES // DISPATCH_TILES,),
            in_specs=[row(D_MODEL), row(LANES), row(LANES),
                      pl.BlockSpec((DISPATCH_TILES, SUBLANES, LANES), lambda i, *_: (i, 0, 0))],
            out_specs=(pl.BlockSpec(memory_space=pl.ANY), row(LANES)),
            scratch_shapes=[
                pltpu.VMEM((2 * DISPATCH_TILES, STAGE_ROWS, XS_COLS), F32),
                pltpu.VMEM((1 << TAIL_BITS[0], XS_COLS), F32),
                scratch(), scratch(), scratch(),
                pltpu.SemaphoreType.DMA((2 * DISPATCH_TILES,)),
                pltpu.SemaphoreType.DMA,
            ],
        ),
        out_shape=(jax.ShapeDtypeStruct((N_SLOTS, XS_COLS), F32), jax.ShapeDtypeStruct((N_TOK, LANES), I32)),
        compiler_params=pltpu.CompilerParams(dimension_semantics=("arbitrary",), vmem_limit_bytes=VMEM_LIMIT),
        name="dispatch",
    )(rt["cnt"], rt["off"], rt["run"], rt["tail_cnt"], rt["tail_dst"], rt["n_used"], h2, top_i, top_g, rt["off_v"])


def _moe_kernel(layer, be_ref, nxt_ref, nv_ref, nu_ref, x_ref, wgu_hbm, wdn_hbm, bg_ref, bu_ref, bd_ref, sel_ref,
                y_ref, wgu_ref, wdn_ref, wgu_s, wdn_s, sem):
    i = pl.program_id(0)
    n_used = nu_ref[0]

    def fetch(e):
        return (pltpu.make_async_copy(wgu_hbm.at[layer, e], wgu_ref, sem.at[0]),
                pltpu.make_async_copy(wdn_hbm.at[layer, e], wdn_ref, sem.at[1]))

    @pl.when(i < n_used)
    def _():
        @pl.when(jnp.logical_or(i == 0, be_ref[i] != be_ref[jnp.maximum(i - 1, 0)]))
        def _():
            @pl.when(i == 0)
            def _():
                for cp in fetch(be_ref[0]):
                    cp.start()
            for cp in fetch(be_ref[i]):
                cp.wait()
            half = LANES
            for c in range(2 * EXPERT_FF // (2 * half)):
                r = _dot(wgu_ref[:, c * 2 * half:(c + 1) * 2 * half].astype(BF16), sel_ref[...])
                wgu_s[:, c * half:(c + 1) * half] = r[:, 0:half].astype(BF16)
                wgu_s[:, EXPERT_FF + c * half:EXPERT_FF + (c + 1) * half] = r[:, half:2 * half].astype(BF16)
            wdn_s[...] = wdn_ref[...].astype(BF16)
            nxt = nxt_ref[i]

            @pl.when(nxt >= 0)
            def _():
                for cp in fetch(nxt):
                    cp.start(priority=1)

        def expert_rows(rows):
            gu = _dot(x_ref[0:rows, 0:D_MODEL].astype(BF16), wgu_s[...])
            gate = jnp.minimum(gu[:, 0:EXPERT_FF] + bg_ref[0], SWIGLU_LIMIT)
            up = jnp.clip(gu[:, EXPERT_FF:2 * EXPERT_FF] + bu_ref[0], -SWIGLU_LIMIT, SWIGLU_LIMIT)
            act = gate * jax.nn.sigmoid(SWIGLU_ALPHA * gate) * (up + 1.0)
            y = _dot(act.astype(BF16), wdn_s[...]) + bd_ref[0]
            slot_gate = x_ref[0:rows, D_MODEL:XS_COLS]
            for j in range(D_MODEL // LANES):
                y_ref[0:rows, j * LANES:(j + 1) * LANES] = y[:, j * LANES:(j + 1) * LANES] * slot_gate

        parts = (nv_ref[i] + MOE_PART - 1) // MOE_PART
        for n_parts in range(1, MOE_BLK // MOE_PART + 1):
            @pl.when(parts == n_parts)
            def _():
                rows = n_parts * MOE_PART
                expert_rows(rows)
                if rows < MOE_BLK:
                    y_ref[rows:MOE_BLK, :] = jnp.zeros((MOE_BLK - rows, D_MODEL), F32)

    @pl.when(i >= n_used)
    def _():
        y_ref[...] = jnp.zeros_like(y_ref)


def _bias_split_kernel(b_ref, sel_ref, o_ref):
    for c in range(2 * EXPERT_FF // (2 * LANES)):
        terms = _split3(b_ref[:, c * 2 * LANES:(c + 1) * 2 * LANES])
        r = functools.reduce(jnp.add, [_dot(t, sel_ref[...]) for t in terms])
        o_ref[:, c * LANES:(c + 1) * LANES] = r[:, 0:LANES]
        o_ref[:, EXPERT_FF + c * LANES:EXPERT_FF + (c + 1) * LANES] = r[:, LANES:2 * LANES]


def _bias_split(b_gu, sel):
    n = DEPTH * N_EXPERTS
    out = pl.pallas_call(
        _bias_split_kernel,
        out_shape=jax.ShapeDtypeStruct((n, 2 * EXPERT_FF), F32),
        name="bias_split",
    )(b_gu.reshape(n, 2 * EXPERT_FF), sel)
    return out.reshape(n, 1, 2 * EXPERT_FF)


def _moe(layer, rt, xs, w_gu, w_down, b_gu_split, b_down, sel):
    first = layer * N_EXPERTS
    bg = pl.BlockSpec((1, 1, EXPERT_FF), lambda i, be, *_: (first + be[i], 0, 0))
    bu = pl.BlockSpec((1, 1, EXPERT_FF), lambda i, be, *_: (first + be[i], 0, 1))
    bd = pl.BlockSpec((1, 1, D_MODEL), lambda i, be, *_: (first + be[i], 0, 0))
    return pl.pallas_call(
        functools.partial(_moe_kernel, layer),
        grid_spec=pltpu.PrefetchScalarGridSpec(
            num_scalar_prefetch=4,
            grid=(N_BLOCKS,),
            in_specs=[
                pl.BlockSpec((MOE_BLK, XS_COLS), lambda i, be, nx, nv, nu: (jnp.minimum(i, nu[0] - 1), 0)),
                pl.BlockSpec(memory_space=pl.ANY),
                pl.BlockSpec(memory_space=pl.ANY),
                bg, bu, bd,
                pl.BlockSpec((2 * LANES, 2 * LANES), lambda i, *_: (0, 0)),
            ],
            out_specs=pl.BlockSpec((MOE_BLK, D_MODEL), lambda i, *_: (i, 0)),
            scratch_shapes=[
                pltpu.VMEM((D_MODEL, 2 * EXPERT_FF), F32),
                pltpu.VMEM((EXPERT_FF, D_MODEL), F32),
                pltpu.VMEM((D_MODEL, 2 * EXPERT_FF), BF16),
                pltpu.VMEM((EXPERT_FF, D_MODEL), BF16),
                pltpu.SemaphoreType.DMA((2,)),
            ],
        ),
        out_shape=jax.ShapeDtypeStruct((N_SLOTS, D_MODEL), F32),
        compiler_params=pltpu.CompilerParams(dimension_semantics=("arbitrary",), vmem_limit_bytes=VMEM_LIMIT),
        name="moe",
    )(rt["blk_e"], rt["blk_next"], rt["blk_rows"], rt["n_used"], xs, w_gu, w_down, b_gu_split, b_gu_split,
      b_down.reshape(DEPTH * N_EXPERTS, 1, D_MODEL), sel)


def _combine_kernel(cnt_ref, off_ref, run_ref, y_ref, lp_ref, x1_ref, mod_ref, fw_ref,
                    xc_ref, xl_ref, yc_ref, yl_ref, stage, mine_s, sem):
    i = pl.program_id(0)
    slot = i % 2

    def copy_in(s_):
        return lambda s, d, n: pltpu.make_async_copy(y_ref.at[pl.ds(d, n), :], stage.at[s_, pl.ds(s, n), :],
                                                     sem.at[s_])

    @pl.when(i == 0)
    def _():
        stage[...] = jnp.zeros_like(stage)
        _run_copies(cnt_ref, off_ref, run_ref, 0, N_EXPERTS, RUN_BITS, copy_in(0), True)

    @pl.when(i + 1 < N_TILES)
    def _():
        _run_copies(cnt_ref, off_ref, run_ref, (i + 1) * N_EXPERTS, N_EXPERTS, RUN_BITS, copy_in(1 - slot), True)

    last = i * N_EXPERTS + N_EXPERTS - 1
    _wait_rows(off_ref[last] + cnt_ref[last], copy_in(slot))
    lane = lax.broadcasted_iota(I32, (TM, LANES), 1)
    mine = [jnp.broadcast_to(lp_ref[:, k:k + 1], (TM, LANES)) for k in range(TOP_K)]
    for c in range(STAGE_ROWS // LANES):
        hit = jnp.zeros((TM, LANES), F32)
        for k in range(TOP_K):
            hit = jnp.where(lane + c * LANES == mine[k], 1.0, hit)
        mine_s[:, c * LANES:(c + 1) * LANES] = hit.astype(BF16)
    moe = _dot(mine_s[...], stage[slot].astype(BF16))
    gate2 = mod_ref[0][:, 5 * D_MODEL:6 * D_MODEL]
    x2 = x1_ref[...] + gate2 * moe
    y_norm = _rms(x2, fw_ref[...])

    @pl.when(i < CTX_TILES)
    def _():
        xc_ref[...] = x2
        yc_ref[...] = y_norm

    @pl.when(i >= CTX_TILES)
    def _():
        xl_ref[...] = x2
        yl_ref[...] = y_norm


def _combine(rt, y_slots, lp, x1, mod3, final_w):
    row = lambda n: pl.BlockSpec((TM, n), lambda i, *_: (i, 0))
    ctx_row = pl.BlockSpec((TM, D_MODEL), lambda i, *_: (jnp.minimum(i, CTX_TILES - 1), 0))
    lat_row = pl.BlockSpec((TM, D_MODEL), lambda i, *_: (jnp.maximum(i - CTX_TILES, 0), 0))
    return pl.pallas_call(
        _combine_kernel,
        grid_spec=pltpu.PrefetchScalarGridSpec(
            num_scalar_prefetch=3,
            grid=(N_TILES,),
            in_specs=[
                pl.BlockSpec(memory_space=pl.ANY),
                row(LANES), row(D_MODEL),
                pl.BlockSpec((1, 1, 6 * D_MODEL), lambda i, *_: (_mod_row(i), 0, 0)),
                pl.BlockSpec((1, D_MODEL), lambda i, *_: (0, 0)),
            ],
            out_specs=(ctx_row, lat_row, ctx_row, lat_row),
            scratch_shapes=[pltpu.VMEM((2, STAGE_ROWS, D_MODEL), F32), pltpu.VMEM((TM, STAGE_ROWS), BF16),
                            pltpu.SemaphoreType.DMA((2,))],
        ),
        out_shape=(jax.ShapeDtypeStruct((N_CTX, D_MODEL), F32), jax.ShapeDtypeStruct((N_LAT, D_MODEL), F32),
                   jax.ShapeDtypeStruct((N_CTX, D_MODEL), F32), jax.ShapeDtypeStruct((N_LAT, D_MODEL), F32)),
        compiler_params=pltpu.CompilerParams(dimension_semantics=("arbitrary",), vmem_limit_bytes=VMEM_LIMIT),
        name="combine",
    )(rt["cnt"], rt["off"], rt["run"], y_slots, lp, x1, mod3, final_w)


def _rope_tables():
    rows = DEC_SEQ // GRID_W
    t = jnp.arange(rows * GRID_W)
    row = (t // GRID_W).astype(F32)
    col = (t % GRID_W).astype(F32)
    half = QK_ROPE // 2
    inv = ROPE_BASE ** (-jnp.arange(0, half, 2, dtype=F32) / half)
    ang_r, ang_c = row[:, None] * inv, col[:, None] * inv
    cos32 = jnp.concatenate([jnp.cos(ang_r), jnp.cos(ang_r), jnp.cos(ang_c), jnp.cos(ang_c)], axis=-1)
    sin32 = jnp.concatenate([-jnp.sin(ang_r), jnp.sin(ang_r), -jnp.sin(ang_c), jnp.sin(ang_c)], axis=-1)
    cos32 = jnp.concatenate([jnp.ones((TM, QK_ROPE), F32), cos32], axis=0)
    sin32 = jnp.concatenate([jnp.zeros((TM, QK_ROPE), F32), sin32], axis=0)
    n = cos32.shape[0]
    pad = HEAD_PAD - QK_NOPE - QK_ROPE
    cos_h = jnp.concatenate([jnp.ones((n, QK_NOPE), F32), cos32, jnp.zeros((n, pad), F32)], axis=-1)
    sin_h = jnp.concatenate([jnp.zeros((n, QK_NOPE), F32), sin32, jnp.zeros((n, pad), F32)], axis=-1)
    zeros = jnp.zeros((n, LANES - QK_ROPE), F32)
    return {
        "cosq": cos_h, "sinq": sin_h,
        "cosk": jnp.concatenate([cos32, zeros], axis=-1), "sink": jnp.concatenate([sin32, zeros], axis=-1),
        "vone": jnp.zeros((MLA_HEADS, HEAD_PAD), F32).at[:, V_HEAD].set(1.0).reshape(1, MLA_HEADS * HEAD_PAD),
    }


def _layer_weights(p, l):
    hd = QK_NOPE + QK_ROPE
    pad = HEAD_PAD - hd
    wq = p["w_uq"][l].reshape(Q_RANK, MLA_HEADS, hd)
    wq_a = jnp.pad(wq, ((0, 0), (0, 0), (0, pad)))
    wuk = jnp.pad(p["w_uk"][l], ((0, 0), (0, 0), (0, HEAD_PAD - QK_NOPE)))
    rw = jnp.pad(p["router_w"][l], ((0, 0), (0, LANES - N_EXPERTS)))
    rb = jnp.concatenate([p["router_b"][l], jnp.full((LANES - N_EXPERTS,), -jnp.inf, F32)])
    return {
        "norm1_w": p["norm1_w"][l][None], "q_norm_w": p["q_norm_w"][l][None],
        "wq_a": wq_a.reshape(Q_RANK, -1).astype(BF16),
        "kv_norm_w": p["kv_norm_w"][l][None],
        "wuk": wuk.reshape(KV_RANK, -1).astype(BF16),
        "wuv": jnp.pad(p["w_uv"][l], ((0, 0), (0, 0), (0, HEAD_PAD - V_HEAD))).reshape(KV_RANK, -1).astype(BF16),
        "wuv_c": p["w_uv"][l].reshape(KV_RANK, -1).astype(BF16),
        "conv_a_w": p["conv_a_w"][l], "ssm_conv_w": p["ssm_conv_w"][l], "ssm_conv_b": p["ssm_conv_b"][l][None],
        "w_out_a": p["w_out_a"][l].astype(BF16), "w_o_mla": p["w_o_mla"][l].astype(BF16),
        "d_skip": jnp.repeat(p["d_skip"][l], SSM_HEAD_DIM)[None], "ssm_norm_w": p["ssm_norm_w"][l][None],
        "w_o_ssm": p["w_o_ssm"][l].astype(BF16), "w_o": p["w_o"][l].astype(BF16),
        "norm2_w": p["norm2_w"][l][None], "router_w": rw.astype(BF16), "router_b": rb[None],
    }


def _ssd_params(p, l):
    z = lambda n: jnp.zeros((n,), F32)
    lanes = lambda f, b: jnp.concatenate([z(SM_DTF), f, b, z(LANES - SM_DTB - SSM_HEADS)])
    ones = jnp.ones((SSM_HEADS,), F32)
    rows = [lanes(p["dt_bias_fwd"][l], p["dt_bias_bwd"][l]), lanes(p["a_log_fwd"][l], p["a_log_bwd"][l]),
            lanes(ones, ones)]
    return jnp.concatenate([jnp.stack(rows), jnp.zeros((SUBLANES - 3, LANES), F32)], axis=0)


def _routing(cnt_tiles):
    cnt = cnt_tiles[:, 0, 0:N_EXPERTS]
    cnt = (cnt + RUN_ALIGN - 1) // RUN_ALIGN * RUN_ALIGN
    per_expert = jnp.sum(cnt, axis=0)
    padded = (per_expert + MOE_BLK - 1) // MOE_BLK * MOE_BLK
    pad_end = jnp.cumsum(padded)
    pad_start = pad_end - padded
    run = pad_start[None, :] + jnp.cumsum(cnt, axis=0) - cnt
    off = jnp.cumsum(cnt, axis=1) - cnt
    starts = jnp.arange(N_BLOCKS, dtype=I32) * MOE_BLK
    blk_e = jnp.minimum(jnp.sum((pad_end[None, :] <= starts[:, None]).astype(I32), axis=1), N_EXPERTS - 1)
    off_v = jnp.zeros((N_TILES, SUBLANES, LANES), F32).at[:, 0, 0:N_EXPERTS].set(off.astype(F32))
    ids = jnp.arange(N_EXPERTS, dtype=I32)
    later = jnp.logical_and(ids[None, :] > ids[:, None], padded[None, :] > 0)
    nxt = jnp.min(jnp.where(later, ids[None, :], N_EXPERTS), axis=1)
    nxt = jnp.where(nxt == N_EXPERTS, -1, nxt)
    per_block = lambda v: jnp.sum(jnp.where(blk_e[:, None] == ids[None, :], v[None, :], 0), axis=1)
    blk_rows = jnp.clip(per_block(pad_start + per_expert) - starts, 0, MOE_BLK)
    return {
        "blk_next": per_block(nxt).astype(I32), "blk_rows": blk_rows.astype(I32),
        "cnt": cnt.reshape(-1).astype(I32), "off": off.reshape(-1).astype(I32), "run": run.reshape(-1).astype(I32),
        "tail_cnt": (padded - per_expert).astype(I32), "tail_dst": (pad_start + per_expert).astype(I32),
        "blk_e": blk_e.astype(I32), "n_used": (pad_end[-1] // MOE_BLK).astype(I32).reshape(1), "off_v": off_v,
    }


def _deinterleave_matrix():
    k = jnp.arange(2 * LANES)[:, None]
    n = jnp.arange(2 * LANES)[None, :]
    src = jnp.where(n < LANES, 2 * n, 2 * (n - LANES) + 1)
    return (k == src).astype(BF16)


def kernel(x_prompt, x_sample, cache_ckv, cache_kpe, state_ssm_fwd, state_ssm_bwd, c, c_ctx, w_ada, b_ada, norm1_w, w_in, conv_a_w, w_out_a, q_norm_w, w_uq, kv_norm_w, w_uk, w_uv, w_o_mla, ssm_conv_w, ssm_conv_b, dt_bias_fwd, dt_bias_bwd, a_log_fwd, a_log_bwd, d_skip, ssm_norm_w, w_o_ssm, w_o, norm2_w, router_w, router_b, w_gu, b_gu, w_down, b_down, final_norm_w):
    p = dict(norm1_w=norm1_w, w_in=w_in, conv_a_w=conv_a_w, w_out_a=w_out_a, q_norm_w=q_norm_w, w_uq=w_uq,
             kv_norm_w=kv_norm_w, w_uk=w_uk, w_uv=w_uv, w_o_mla=w_o_mla, ssm_conv_w=ssm_conv_w,
             ssm_conv_b=ssm_conv_b, dt_bias_fwd=dt_bias_fwd, dt_bias_bwd=dt_bias_bwd, a_log_fwd=a_log_fwd,
             a_log_bwd=a_log_bwd, d_skip=d_skip, ssm_norm_w=ssm_norm_w, w_o_ssm=w_o_ssm, w_o=w_o,
             norm2_w=norm2_w, router_w=router_w, router_b=router_b, b_gu=b_gu, b_down=b_down)
    xc = x_prompt.reshape(N_CTX, D_MODEL)
    xl = x_sample.reshape(N_LAT, D_MODEL)
    cond8 = jnp.concatenate([c_ctx[None], c, jnp.zeros((SUBLANES - 1 - DEC_BATCH, D_MODEL), F32)], axis=0)
    mods = _ada_mods(cond8, w_ada, b_ada)
    tabs = _rope_tables()
    sel = _deinterleave_matrix()
    b_gu_split = _bias_split(b_gu, sel)
    w_in_t = jnp.swapaxes(w_in, 1, 2)
    final_w = final_norm_w[None]
    hp = SSM_INNER

    caches, states = (), ()
    y_ctx = y_lat = None
    for l in range(DEPTH):
        lw = _layer_weights(p, l)
        mod3 = mods[l].reshape(SUBLANES, 1, 6 * D_MODEL)
        ta, q, ckv, kpe, kf, v_ctx, v_lat, small, z, xact, g = _inproj(l, xc, xl, mod3, w_in_t, lw, tabs, caches)
        caches = (ckv, kpe)
        kpe_c = jnp.pad(cache_kpe[:, l].reshape(DEC_BATCH * PAST_LEN, QK_ROPE), ((0, 0), (0, LANES - QK_ROPE)))
        kf_c, v_c = _kvcache(cache_ckv[:, l].reshape(DEC_BATCH * PAST_LEN, KV_RANK), kpe_c, lw, tabs)
        att_ctx, att_lat = _attention(q, kf, v_ctx, v_lat, kf_c, v_c)
        init_f = state_ssm_fwd[:, l].reshape(DEC_BATCH, hp, SSM_STATE)
        init_b = state_ssm_bwd[:, l].reshape(DEC_BATCH, hp, SSM_STATE)
        yf, yb, sf, sb = _ssd(l, xact, small, init_f, init_b, _ssd_params(p, l), states)
        states = (sf, sb)
        x1, h2, top_i, top_g, cnt_tiles = _merge(ta, att_ctx, att_lat, yf, yb, xact, z, g, xc, xl, mod3, lw)
        rt = _routing(cnt_tiles)
        xs, lp = _dispatch(rt, h2, top_i, top_g)
        y_slots = _moe(l, rt, xs, w_gu, w_down, b_gu_split, b_down, sel)
        xc, xl, y_ctx, y_lat = _combine(rt, y_slots, lp, x1, mod3, final_w)

    y_prompt = y_ctx.reshape(BATCH, SEQ, D_MODEL)
    y_sample = y_lat.reshape(DEC_BATCH, DEC_SEQ, D_MODEL)
    state_shape = (BATCH, DEPTH, SSM_HEADS, SSM_HEAD_DIM, SSM_STATE)
    return (y_prompt, y_sample, caches[0], caches[1], states[0].reshape(state_shape), states[1].reshape(state_shape))
```

```python
import functools
import math

import jax
import jax.numpy as jnp
from jax import lax
from jax.experimental import pallas as pl
from jax.experimental.pallas import tpu as pltpu

F32 = jnp.float32
BF16 = jnp.bfloat16
I32 = jnp.int32

D_MODEL = 1024
BATCH = 16
SEQ = 256
DEPTH = 2
DEC_BATCH = 2
DEC_SEQ = 2048
PAST_LEN = 512
GRID_W = 64
NORM_EPS = 1e-6
CONV_DIM = 512
MLA_HEADS = 8
Q_RANK = 384
KV_RANK = 256
QK_NOPE = 64
QK_ROPE = 32
V_HEAD = 64
ROPE_BASE = 10000.0
MLA_SCALE = (QK_NOPE + QK_ROPE) ** -0.5
SSM_HEADS = 16
SSM_HEAD_DIM = 64
SSM_INNER = SSM_HEADS * SSM_HEAD_DIM
SSM_GROUPS = 2
SSM_STATE = 128
SSM_CONV_CH = SSM_INNER + 2 * SSM_GROUPS * SSM_STATE
N_EXPERTS = 32
TOP_K = 4
EXPERT_FF = D_MODEL
SWIGLU_ALPHA = 1.702
SWIGLU_LIMIT = 7.0

N_CTX = BATCH * SEQ
N_LAT = DEC_BATCH * DEC_SEQ
N_TOK = N_CTX + N_LAT
N_SEQS = BATCH + DEC_BATCH

LANES = 128
SUBLANES = 8
HEAD_PAD = 128
TM = 256
N_TILES = N_TOK // TM
CTX_TILES = N_CTX // TM
LAT_TILES_PER_SEQ = DEC_SEQ // TM
TQ_LAT = 512
CHUNK = 128
SSD_STEP = 2 * CHUNK
CTX_STEPS_PER_SEQ = SEQ // SSD_STEP
LAT_STEPS_PER_SEQ = DEC_SEQ // SSD_STEP
N_CTX_STEPS = N_CTX // SSD_STEP
N_SSD_STEPS = N_TOK // SSD_STEP
MOE_BLK = 512
MOE_PART = 128
DISPATCH_TILES = 2
N_ASSIGN = N_TOK * TOP_K
RUN_ALIGN = SUBLANES
RUN_BITS = tuple(range(8, 2, -1))
TAIL_BITS = tuple(range(8, 2, -1))
COMMON_BITS = 6
WAIT_BITS = tuple(range(10, 2, -1))
STAGE_ROWS = 1280
XS_COLS = D_MODEL + LANES
N_BLOCKS = -(-(N_ASSIGN + N_TILES * N_EXPERTS * (RUN_ALIGN - 1) + N_EXPERTS * (MOE_BLK - 1)) // MOE_BLK)
N_SLOTS = N_BLOCKS * MOE_BLK
VMEM_LIMIT = 56 * 1024 * 1024

C_A3 = 0
C_CQ = C_A3 + 3 * CONV_DIM
C_CKV = C_CQ + Q_RANK
C_Z = C_CKV + KV_RANK
C_XBC = C_Z + SSM_INNER
C_GATE = C_XBC + SSM_CONV_CH
C_SMALL = C_GATE + 3 * D_MODEL
IN_COLS2 = C_SMALL + LANES
SM_DTF = QK_ROPE
SM_DTB = QK_ROPE + SSM_HEADS
S_CQ = 3 * CONV_DIM
S_CKV = S_CQ + Q_RANK
S_KPE = S_CKV + KV_RANK
S_Z = S_KPE + QK_ROPE
S_XBC = S_Z + SSM_INNER
S_DTF = S_XBC + SSM_CONV_CH
S_GATE = S_DTF + 2 * SSM_HEADS
IN_COLS = S_GATE + 3 * D_MODEL
W_SEGMENTS = ((C_A3, 0, 3 * CONV_DIM), (C_CQ, S_CQ, Q_RANK), (C_CKV, S_CKV, KV_RANK), (C_Z, S_Z, SSM_INNER),
              (C_XBC, S_XBC, SSM_CONV_CH), (C_GATE, S_GATE, 3 * D_MODEL))
W_PIECE = 512


def _rms(x, w):
    return x * lax.rsqrt(jnp.mean(x * x, axis=-1, keepdims=True) + NORM_EPS) * w


def _silu(x):
    return x * jax.nn.sigmoid(x)


def _dot(a, b):
    return jnp.dot(a, b, preferred_element_type=F32)


def _dot_nt(a, b):
    return lax.dot_general(a, b, (((1,), (1,)), ((), ())), preferred_element_type=F32)


def _resident(shape):
    nd = len(shape)
    return pl.BlockSpec(shape, lambda *_: (0,) * nd, pipeline_mode=pl.Buffered(1))


def _mod_row(i):
    return jnp.where(i < CTX_TILES, 0, 1 + (i - CTX_TILES) // LAT_TILES_PER_SEQ)


def _pos_block(i):
    return jnp.where(i < CTX_TILES, 0, 1 + (i - CTX_TILES) % LAT_TILES_PER_SEQ)


def _ada_kernel(c_ref, w_ref, b_ref, o_ref):
    s = _silu(c_ref[...]).astype(BF16)
    o_ref[0] = _dot(s, w_ref[0].astype(BF16)) + b_ref[0]


def _ada_mods(cond8, w_ada, b_ada):
    tn = 1536
    n_mod = 6 * D_MODEL
    return pl.pallas_call(
        _ada_kernel,
        grid=(DEPTH, n_mod // tn),
        in_specs=[
            pl.BlockSpec((SUBLANES, D_MODEL), lambda l, j: (0, 0)),
            pl.BlockSpec((1, D_MODEL, tn), lambda l, j: (l, 0, j)),
            pl.BlockSpec((1, 1, tn), lambda l, j: (l, 0, j)),
        ],
        out_specs=pl.BlockSpec((1, SUBLANES, tn), lambda l, j: (l, 0, j)),
        out_shape=jax.ShapeDtypeStruct((DEPTH, SUBLANES, n_mod), F32),
        compiler_params=pltpu.CompilerParams(dimension_semantics=("arbitrary", "arbitrary")),
        name="ada_mods",
    )(cond8, w_ada, b_ada.reshape(DEPTH, 1, n_mod))


def _relayout_w_in(layer, wt_hbm, w2, stg, small_stg, sem, small_sem):
    pieces = [(dst + p, src + p, min(W_PIECE, width - p))
              for dst, src, width in W_SEGMENTS for p in range(0, width, W_PIECE)]
    fetch = lambda k: pltpu.make_async_copy(wt_hbm.at[layer, pl.ds(pieces[k][1], pieces[k][2]), :],
                                            stg.at[k % 2, pl.ds(0, pieces[k][2]), :], sem.at[k % 2])
    small_stg[...] = jnp.zeros_like(small_stg)
    small_copies = [
        pltpu.make_async_copy(wt_hbm.at[layer, pl.ds(S_KPE, QK_ROPE), :], small_stg.at[pl.ds(0, QK_ROPE), :],
                              small_sem.at[0]),
        pltpu.make_async_copy(wt_hbm.at[layer, pl.ds(S_DTF, 2 * SSM_HEADS), :],
                              small_stg.at[pl.ds(SM_DTF, 2 * SSM_HEADS), :], small_sem.at[1]),
    ]
    for cp in small_copies:
        cp.start()
    fetch(0).start()
    for k, (dst, _, width) in enumerate(pieces):
        if k + 1 < len(pieces):
            fetch(k + 1).start()
        fetch(k).wait()
        w2[:, dst:dst + width] = stg[k % 2, 0:width, :].T.astype(BF16)
    for cp in small_copies:
        cp.wait()
    w2[:, C_SMALL:IN_COLS2] = small_stg[...].T.astype(BF16)


def _place_rope_key(k_nope, kpe):
    shifted = pltpu.roll(kpe, QK_NOPE, 1)
    return jnp.concatenate([k_nope[:, h * HEAD_PAD:(h + 1) * HEAD_PAD] + shifted for h in range(MLA_HEADS)],
                           axis=1).astype(BF16)


def _rot_partner(x):
    n = x.shape[1]
    lane = lax.broadcasted_iota(I32, x.shape, 1)
    quarter = QK_ROPE // 4
    return jnp.where(lane % (2 * quarter) < quarter, pltpu.roll(x, n - quarter, 1), pltpu.roll(x, quarter, 1))


def _first_layer_slabs(ref, value):
    ref[0, 0] = value
    ref[0, 1:DEPTH] = jnp.zeros((DEPTH - 1,) + value.shape, value.dtype)


def _inproj_kernel(layer, *refs):
    n_in = 23 + (2 if layer > 0 else 0)
    (xc_ref, xcp_ref, xcn_ref, xl_ref, xlp_ref, xln_ref, mod_ref, n1w_ref, w_hbm, qnw_ref, wq_ref, kvnw_ref, wuk_ref,
     wuv_ref, wuvc_ref, vone_ref, cosq_ref, sinq_ref, cosk_ref, sink_ref, caw_ref, cw_ref, cb_ref) = refs[:23]
    (ta_ref, q_ref, ckv_ref, kpe_ref, kf_ref, vc_ref, vl_ref, small_ref, z_ref, xact_ref, g_ref,
     w2, stg, small_stg, sem, small_sem) = refs[n_in:]
    i = pl.program_id(0)

    @pl.when(i == 0)
    def _():
        _relayout_w_in(layer, w_hbm, w2, stg, small_stg, sem, small_sem)

    mod = mod_ref[0]
    shift1 = mod[:, 0:D_MODEL]
    scale1 = mod[:, D_MODEL:2 * D_MODEL]
    x_ext = jnp.where(i < CTX_TILES, jnp.concatenate([xcp_ref[...], xc_ref[...], xcn_ref[...]], axis=0),
                      jnp.concatenate([xlp_ref[...], xl_ref[...], xln_ref[...]], axis=0))
    h_ext = _rms(x_ext, n1w_ref[...]) * (1.0 + scale1) + shift1
    hb_ext = h_ext.astype(BF16)
    hb = h_ext[SUBLANES:SUBLANES + TM].astype(BF16)

    j = (i - CTX_TILES) % LAT_TILES_PER_SEQ
    is_ctx = i < CTX_TILES
    keep_prev = jnp.where(jnp.logical_or(is_ctx, j == 0), 0.0, 1.0)
    keep_next = jnp.where(jnp.logical_or(is_ctx, j == LAT_TILES_PER_SEQ - 1), 0.0, 1.0)
    row = lax.broadcasted_iota(I32, (TM, 1), 0)
    prev_mask = jnp.where(row == 0, keep_prev, 1.0)
    next_mask = jnp.where(row == TM - 1, keep_next, 1.0)
    ext = TM + 2 * SUBLANES

    def conv3(x, w_ref):
        x_prev = pltpu.roll(x, 1, 0)[SUBLANES:SUBLANES + TM] * prev_mask
        x_next = pltpu.roll(x, ext - 1, 0)[SUBLANES:SUBLANES + TM] * next_mask
        return x_prev * w_ref[0:1, :] + x[SUBLANES:SUBLANES + TM] * w_ref[1:2, :] + x_next * w_ref[2:3, :]

    a3 = _dot(hb_ext, w2[:, C_A3:C_CQ])
    s = a3[:, 2 * CONV_DIM:3 * CONV_DIM] * a3[:, 0:CONV_DIM]
    ta_ref[...] = (a3[SUBLANES:SUBLANES + TM, CONV_DIM:2 * CONV_DIM] * conv3(s, caw_ref)).astype(BF16)
    xact_ref[...] = _silu(conv3(_dot(hb_ext, w2[:, C_XBC:C_GATE]), cw_ref) + cb_ref[...])

    def seg(a, b):
        return _dot(hb, w2[:, a:b])

    cqn = _rms(seg(C_CQ, C_CKV), qnw_ref[...]).astype(BF16)
    qa = _dot(cqn, wq_ref[...])
    qa_rot = _rot_partner(qa)
    for h in range(MLA_HEADS):
        hs = slice(h * HEAD_PAD, (h + 1) * HEAD_PAD)
        q_ref[:, hs] = (qa[:, hs] * cosq_ref[...] + qa_rot[:, hs] * sinq_ref[...]).astype(BF16)

    ckv = _rms(seg(C_CKV, C_Z), kvnw_ref[...])
    ckvb = ckv.astype(BF16)
    small = seg(C_SMALL, IN_COLS2)
    small_ref[...] = small
    kpe = small * cosk_ref[...] + _rot_partner(small) * sink_ref[...]
    kf_ref[...] = _place_rope_key(_dot(ckvb, wuk_ref[...]), kpe)
    v_lat = (_dot(ckvb, wuv_ref[...]) + vone_ref[...]).astype(BF16)
    v_ctx = _dot(ckvb, wuvc_ref[...]).astype(BF16)

    z_ref[...] = seg(C_Z, C_XBC)
    g_ref[...] = jax.nn.sigmoid(seg(C_GATE, C_SMALL))

    @pl.when(i < CTX_TILES)
    def _():
        if layer == 0:
            _first_layer_slabs(ckv_ref, ckv)
            _first_layer_slabs(kpe_ref, small[:, 0:QK_ROPE])
        else:
            ckv_ref[0] = ckv
            kpe_ref[0] = small[:, 0:QK_ROPE]
        vc_ref[...] = v_ctx

    @pl.when(i >= CTX_TILES)
    def _():
        vl_ref[...] = v_lat


def _inproj(layer, xc, xl, mod3, w_in, lw, tabs, caches):
    row = lambda n: pl.BlockSpec((TM, n), lambda i: (i, 0))
    tab = lambda n: pl.BlockSpec((TM, n), lambda i: (_pos_block(i), 0))
    per = TM // SUBLANES
    qw = MLA_HEADS * HEAD_PAD
    ctx_row = lambda n: pl.BlockSpec((TM, n), lambda i: (jnp.minimum(i, CTX_TILES - 1), 0))
    lat_row = lambda n: pl.BlockSpec((TM, n), lambda i: (jnp.maximum(i - CTX_TILES, 0), 0))

    def halo(first_tile, n_rows, side):
        last = n_rows // SUBLANES - 1
        return pl.BlockSpec((SUBLANES, D_MODEL),
                            lambda i: (jnp.clip((i - first_tile + side) * per - 1 + side, 0, last), 0))

    out_shape = (
        jax.ShapeDtypeStruct((N_TOK, CONV_DIM), BF16),
        jax.ShapeDtypeStruct((N_TOK, qw), BF16),
        jax.ShapeDtypeStruct((BATCH, DEPTH, SEQ, KV_RANK), F32),
        jax.ShapeDtypeStruct((BATCH, DEPTH, SEQ, QK_ROPE), F32),
        jax.ShapeDtypeStruct((N_TOK, qw), BF16),
        jax.ShapeDtypeStruct((N_CTX, MLA_HEADS * V_HEAD), BF16),
        jax.ShapeDtypeStruct((N_LAT, qw), BF16),
        jax.ShapeDtypeStruct((N_TOK, LANES), F32),
        jax.ShapeDtypeStruct((N_TOK, SSM_INNER), F32),
        jax.ShapeDtypeStruct((N_TOK, SSM_CONV_CH), F32),
        jax.ShapeDtypeStruct((N_TOK, 3 * D_MODEL), F32),
    )
    out_specs = [row(s.shape[-1]) for s in out_shape]
    cache_block = (1, DEPTH, SEQ) if layer == 0 else (1, None, SEQ)
    cache_index = lambda i: (jnp.minimum(i, CTX_TILES - 1), 0 if layer == 0 else layer, 0, 0)
    out_specs[2] = pl.BlockSpec(cache_block + (KV_RANK,), cache_index)
    out_specs[3] = pl.BlockSpec(cache_block + (QK_ROPE,), cache_index)
    extra = {} if layer == 0 else dict(input_output_aliases={23: 2, 24: 3})
    out_specs[5] = ctx_row(MLA_HEADS * V_HEAD)
    out_specs[6] = lat_row(qw)
    return pl.pallas_call(
        functools.partial(_inproj_kernel, layer),
        grid=(N_TILES,),
        in_specs=[
            ctx_row(D_MODEL), halo(0, N_CTX, 0), halo(0, N_CTX, 1),
            lat_row(D_MODEL), halo(CTX_TILES, N_LAT, 0), halo(CTX_TILES, N_LAT, 1),
            pl.BlockSpec((1, 1, 6 * D_MODEL), lambda i: (_mod_row(i), 0, 0)),
            _resident((1, D_MODEL)),
            pl.BlockSpec(memory_space=pl.ANY),
            _resident((1, Q_RANK)),
            _resident((Q_RANK, qw)),
            _resident((1, KV_RANK)),
            _resident((KV_RANK, qw)),
            _resident((KV_RANK, qw)),
            _resident((KV_RANK, MLA_HEADS * V_HEAD)),
            _resident((1, qw)),
            tab(HEAD_PAD), tab(HEAD_PAD), tab(LANES), tab(LANES),
            _resident((3, CONV_DIM)), _resident((3, SSM_CONV_CH)), _resident((1, SSM_CONV_CH)),
        ] + [pl.BlockSpec(memory_space=pl.ANY)] * len(caches),
        out_specs=tuple(out_specs),
        out_shape=out_shape,
        scratch_shapes=[pltpu.VMEM((D_MODEL, IN_COLS2), BF16), pltpu.VMEM((2, W_PIECE, D_MODEL), F32),
                        pltpu.VMEM((LANES, D_MODEL), F32), pltpu.SemaphoreType.DMA((2,)),
                        pltpu.SemaphoreType.DMA((2,))],
        compiler_params=pltpu.CompilerParams(dimension_semantics=("arbitrary",), vmem_limit_bytes=VMEM_LIMIT),
        name="inproj",
        **extra,
    )(xc, xc, xc, xl, xl, xl, mod3, lw["norm1_w"], w_in, lw["q_norm_w"], lw["wq_a"], lw["kv_norm_w"],
      lw["wuk"], lw["wuv"], lw["wuv_c"], tabs["vone"], tabs["cosq"], tabs["sinq"], tabs["cosk"],
      tabs["sink"], lw["conv_a_w"], lw["ssm_conv_w"], lw["ssm_conv_b"], *caches)


def _kvcache_kernel(ckv_ref, kpe_ref, wuk_ref, wuv_ref, vone_ref, kf_ref, v_ref):
    ckvb = ckv_ref[...].astype(BF16)
    kf_ref[...] = _place_rope_key(_dot(ckvb, wuk_ref[...]), kpe_ref[...])
    v_ref[...] = (_dot(ckvb, wuv_ref[...]) + vone_ref[...]).astype(BF16)


def _kvcache(ckv, kpe128, lw, tabs):
    n = ckv.shape[0]
    qw = MLA_HEADS * HEAD_PAD
    return pl.pallas_call(
        _kvcache_kernel,
        grid=(n // PAST_LEN,),
        in_specs=[
            pl.BlockSpec((PAST_LEN, KV_RANK), lambda i: (i, 0)),
            pl.BlockSpec((PAST_LEN, LANES), lambda i: (i, 0)),
            _resident((KV_RANK, qw)),
            _resident((KV_RANK, qw)),
            _resident((1, qw)),
        ],
        out_specs=(pl.BlockSpec((PAST_LEN, qw), lambda i: (i, 0)), pl.BlockSpec((PAST_LEN, qw), lambda i: (i, 0))),
        out_shape=(jax.ShapeDtypeStruct((n, qw), BF16), jax.ShapeDtypeStruct((n, qw), BF16)),
        compiler_params=pltpu.CompilerParams(dimension_semantics=("arbitrary",)),
        name="kvcache",
    )(ckv, kpe128, lw["wuk"], lw["wuv"], tabs["vone"])


def _attn_heads(q_ref, kv_refs, o_ref, acc_ref, denom_from_matmul):
    log2_scale = MLA_SCALE * math.log2(math.e)
    for h in range(MLA_HEADS):
        hs = slice(h * HEAD_PAD, (h + 1) * HEAD_PAD)
        qh = q_ref[:, hs]
        ss = [_dot_nt(qh, k_ref[:, hs]) for k_ref, _ in kv_refs]
        if denom_from_matmul:
            m = functools.reduce(jnp.maximum, [jnp.max(s, axis=-1, keepdims=True) for s in ss])
            ps = [jnp.exp2((s - m) * log2_scale) for s in ss]
            ol = functools.reduce(jnp.add, [_dot(p.astype(BF16), v_ref[:, hs])
                                            for p, (_, v_ref) in zip(ps, kv_refs)])
            acc_ref[:, h * V_HEAD:(h + 1) * V_HEAD] = ol[:, 0:V_HEAD] / ol[:, V_HEAD:V_HEAD + 1]
        else:
            ss = [s * MLA_SCALE for s in ss]
            m = functools.reduce(jnp.maximum, [jnp.max(s, axis=-1, keepdims=True) for s in ss])
            ps = [jnp.exp(s - m) for s in ss]
            l = functools.reduce(jnp.add, [jnp.sum(p, axis=-1, keepdims=True) for p in ps])
            o = functools.reduce(jnp.add, [_dot(p.astype(BF16), v_ref[:, h * V_HEAD:(h + 1) * V_HEAD])
                                           for p, (_, v_ref) in zip(ps, kv_refs)])
            acc_ref[:, h * V_HEAD:(h + 1) * V_HEAD] = o / l
    o_ref[...] = acc_ref[...].astype(BF16)


def _attn_ctx_kernel(q_ref, k_ref, v_ref, o_ref, acc_ref):
    _attn_heads(q_ref, [(k_ref, v_ref)], o_ref, acc_ref, denom_from_matmul=False)


def _attn_lat_kernel(q_ref, k_ref, v_ref, kc_ref, vc_ref, o_ref, acc_ref):
    _attn_heads(q_ref, [(k_ref, v_ref), (kc_ref, vc_ref)], o_ref, acc_ref, denom_from_matmul=True)


def _attention(q, kf, v_ctx, v_lat, kf_c, v_c):
    qw = MLA_HEADS * HEAD_PAD
    vw = MLA_HEADS * V_HEAD
    att_ctx = pl.pallas_call(
        _attn_ctx_kernel,
        grid=(BATCH,),
        in_specs=[pl.BlockSpec((SEQ, qw), lambda b: (b, 0)), pl.BlockSpec((SEQ, qw), lambda b: (b, 0)),
                  pl.BlockSpec((SEQ, vw), lambda b: (b, 0))],
        out_specs=pl.BlockSpec((SEQ, vw), lambda b: (b, 0)),
        out_shape=jax.ShapeDtypeStruct((N_CTX, vw), BF16),
        scratch_shapes=[pltpu.VMEM((SEQ, vw), F32)],
        compiler_params=pltpu.CompilerParams(dimension_semantics=("arbitrary",)),
        name="attn_ctx",
    )(q, kf, v_ctx)
    lat0 = N_CTX // DEC_SEQ
    q_tiles = DEC_SEQ // TQ_LAT
    att_lat = pl.pallas_call(
        _attn_lat_kernel,
        grid=(DEC_BATCH, q_tiles),
        in_specs=[
            pl.BlockSpec((TQ_LAT, qw), lambda b, t: (N_CTX // TQ_LAT + b * q_tiles + t, 0)),
            pl.BlockSpec((DEC_SEQ, qw), lambda b, t: (lat0 + b, 0)),
            pl.BlockSpec((DEC_SEQ, qw), lambda b, t: (b, 0)),
            pl.BlockSpec((PAST_LEN, qw), lambda b, t: (b, 0)),
            pl.BlockSpec((PAST_LEN, qw), lambda b, t: (b, 0)),
        ],
        out_specs=pl.BlockSpec((TQ_LAT, vw), lambda b, t: (b * q_tiles + t, 0)),
        out_shape=jax.ShapeDtypeStruct((N_LAT, vw), BF16),
        scratch_shapes=[pltpu.VMEM((TQ_LAT, vw), F32)],
        compiler_params=pltpu.CompilerParams(dimension_semantics=("arbitrary", "arbitrary"),
                                             vmem_limit_bytes=VMEM_LIMIT),
        name="attn_lat",
    )(q, kf, v_lat, kf_c, v_c)
    return att_ctx, att_lat


def _seq_of_step(s):
    return jnp.where(s < N_CTX_STEPS, s // CTX_STEPS_PER_SEQ,
                     BATCH + (s - N_CTX_STEPS) // LAT_STEPS_PER_SEQ)


def _step_in_seq(s):
    return jnp.where(s < N_CTX_STEPS, s % CTX_STEPS_PER_SEQ, (s - N_CTX_STEPS) % LAT_STEPS_PER_SEQ)


def _steps_in_seq(s):
    return jnp.where(s < N_CTX_STEPS, CTX_STEPS_PER_SEQ, LAT_STEPS_PER_SEQ)


def _mirror_step(s):
    return s + _steps_in_seq(s) - 1 - 2 * _step_in_seq(s)


def _split3(a):
    a1 = a.astype(BF16)
    r1 = a - a1.astype(F32)
    a2 = r1.astype(BF16)
    a3 = (r1 - a2.astype(F32)).astype(BF16)
    return a1, a2, a3


def _ssd_direction(x_ref, sm_ref, par_ref, st_ref, y_ref, lane0, backward, r0):
    rows = slice(r0, r0 + CHUNK)
    ri = lax.broadcasted_iota(I32, (CHUNK, CHUNK), 0)
    ci = lax.broadcasted_iota(I32, (CHUNK, CHUNK), 1)
    tri = (ci >= ri) if backward else (ci <= ri)
    tri_b = jnp.where(tri, 1.0, 0.0).astype(BF16)
    tot_row = 0 if backward else CHUNK - 1

    dt = jax.nn.softplus(sm_ref[rows, :] + par_ref[0:1, :])
    a = dt * (-jnp.exp(par_ref[1:2, :])) * par_ref[2:3, :]
    a1, a2, a3 = _split3(a)
    acs = (_dot(tri_b, a1) + _dot(tri_b, a2) + _dot(tri_b, a3)) * math.log2(math.e)
    acs_t = acs.T
    dt_t = dt.T
    first_head = lax.broadcasted_iota(I32, (1, LANES), 1) < SSM_HEAD_DIM

    def block_diag(pair):
        return jnp.concatenate([jnp.where(first_head, pair, 0.0), jnp.where(first_head, 0.0, pair)],
                               axis=0).astype(BF16)

    for g in range(SSM_GROUPS):
        b0 = SSM_INNER + g * SSM_STATE
        c0 = SSM_INNER + SSM_GROUPS * SSM_STATE + g * SSM_STATE
        bg = x_ref[rows, b0:b0 + SSM_STATE]
        cg = x_ref[rows, c0:c0 + SSM_STATE]
        cb = _dot_nt(cg.astype(BF16), bg.astype(BF16))
        bg_t = bg.T
        heads = SSM_HEADS // SSM_GROUPS
        for pr in range(heads // 2):
            h0 = g * heads + 2 * pr
            sl = slice(h0 * SSM_HEAD_DIM, (h0 + 2) * SSM_HEAD_DIM)
            x_bd = block_diag(x_ref[rows, sl])
            st_old = st_ref[:, sl]
            within, carried, to_state, keep = [], [], [], []
            for h in (h0, h0 + 1):
                lane = lane0 + h
                col = jnp.broadcast_to(acs[:, lane:lane + 1], (CHUNK, CHUNK))
                row = acs_t[lane:lane + 1, :]
                dt_row = dt_t[lane:lane + 1, :]
                decay = jnp.exp2(jnp.where(tri, col - row, -jnp.inf))
                within.append((cb * decay * dt_row).astype(BF16))
                carried.append((cg * jnp.exp2(col)).astype(BF16))
                tot = acs[tot_row:tot_row + 1, lane:lane + 1]
                to_state.append((bg_t * (dt_row * jnp.exp2(tot - row))).astype(BF16))
                keep.append(jnp.exp2(tot))
            lhs = jnp.concatenate(within + carried, axis=1)
            y_ref[rows, sl] = _dot(lhs, jnp.concatenate([x_bd, block_diag(st_old)], axis=0))
            st_ref[:, sl] = (st_old * jnp.where(first_head, keep[0], keep[1])
                             + _dot(jnp.concatenate(to_state, axis=1), x_bd))


def _ssd_kernel(layer, *refs):
    n_in = 7 + (2 if layer > 0 else 0)
    xf_ref, xb_ref, smf_ref, smb_ref, if_ref, ib_ref, par_ref = refs[:7]
    yf_ref, yb_ref, sf_ref, sb_ref, stf_ref, stb_ref = refs[n_in:]
    s = pl.program_id(0)
    c = _step_in_seq(s)

    @pl.when(jnp.logical_and(c == 0, s < N_CTX_STEPS))
    def _():
        stf_ref[...] = jnp.zeros_like(stf_ref)
        stb_ref[...] = jnp.zeros_like(stb_ref)

    @pl.when(jnp.logical_and(c == 0, s >= N_CTX_STEPS))
    def _():
        stf_ref[...] = if_ref[0].T
        stb_ref[...] = ib_ref[0].T

    for r0 in range(0, SSD_STEP, CHUNK):
        _ssd_direction(xf_ref, smf_ref, par_ref, stf_ref, yf_ref, SM_DTF, False, r0)
        _ssd_direction(xb_ref, smb_ref, par_ref, stb_ref, yb_ref, SM_DTB, True, SSD_STEP - CHUNK - r0)

    @pl.when(jnp.logical_and(c == _steps_in_seq(s) - 1, s < N_CTX_STEPS))
    def _():
        if layer == 0:
            _first_layer_slabs(sf_ref, stf_ref[...].T)
            _first_layer_slabs(sb_ref, stb_ref[...].T)
        else:
            sf_ref[0] = stf_ref[...].T
            sb_ref[0] = stb_ref[...].T


def _ssd(layer, xact, small, init_f, init_b, par, states):
    hp = SSM_INNER
    fwd = lambda n: pl.BlockSpec((SSD_STEP, n), lambda s: (s, 0))
    bwd = lambda n: pl.BlockSpec((SSD_STEP, n), lambda s: (_mirror_step(s), 0))
    st = pl.BlockSpec((1, DEPTH, hp, SSM_STATE) if layer == 0 else (1, None, hp, SSM_STATE),
                      lambda s: (jnp.minimum(_seq_of_step(s), BATCH - 1), 0 if layer == 0 else layer, 0, 0))
    init = pl.BlockSpec((1, hp, SSM_STATE), lambda s: (jnp.maximum(_seq_of_step(s) - BATCH, 0), 0, 0))
    extra = {} if layer == 0 else dict(input_output_aliases={7: 2, 8: 3})
    return pl.pallas_call(
        functools.partial(_ssd_kernel, layer),
        grid=(N_SSD_STEPS,),
        in_specs=[fwd(SSM_CONV_CH), bwd(SSM_CONV_CH), fwd(LANES), bwd(LANES), init, init,
                  pl.BlockSpec((SUBLANES, LANES), lambda s: (0, 0))] + [pl.BlockSpec(memory_space=pl.ANY)] * len(states),
        out_specs=(fwd(hp), bwd(hp), st, st),
        out_shape=(jax.ShapeDtypeStruct((N_TOK, hp), F32), jax.ShapeDtypeStruct((N_TOK, hp), F32),
                   jax.ShapeDtypeStruct((BATCH, DEPTH, hp, SSM_STATE), F32),
                   jax.ShapeDtypeStruct((BATCH, DEPTH, hp, SSM_STATE), F32)),
        scratch_shapes=[pltpu.VMEM((SSM_STATE, hp), F32), pltpu.VMEM((SSM_STATE, hp), F32)],
        compiler_params=pltpu.CompilerParams(dimension_semantics=("arbitrary",)),
        name="ssd",
        **extra,
    )(xact, xact, small, small, init_f, init_b, par, *states)


def _merge_kernel(ta_ref, attc_ref, attl_ref, yf_ref, yb_ref, xs_ref, z_ref, g_ref, xc_ref, xl_ref, mod_ref,
                  woa_ref, wom_ref, dsk_ref, snw_ref, wos_ref, wo_ref, n2w_ref, rw_ref, rb_ref,
                  x1_ref, h2_ref, ti_ref, tg_ref, cnt_ref):
    mod = mod_ref[0]
    gate1 = mod[:, 2 * D_MODEL:3 * D_MODEL]
    shift2 = mod[:, 3 * D_MODEL:4 * D_MODEL]
    scale2 = mod[:, 4 * D_MODEL:5 * D_MODEL]
    y_a = _dot(ta_ref[...], woa_ref[...])
    att = jnp.where(pl.program_id(0) < CTX_TILES, attc_ref[...].astype(F32), attl_ref[...].astype(F32))
    y_b = _dot(att.astype(BF16), wom_ref[...])
    y_ssm = (yf_ref[...] + yb_ref[...] + dsk_ref[...] * xs_ref[...]) * _silu(z_ref[...])
    y_c = _dot(_rms(y_ssm, snw_ref[...]).astype(BF16), wos_ref[...])
    merged = (g_ref[:, 0:D_MODEL] * y_a + g_ref[:, D_MODEL:2 * D_MODEL] * y_b
              + g_ref[:, 2 * D_MODEL:3 * D_MODEL] * y_c)
    x = jnp.where(pl.program_id(0) < CTX_TILES, xc_ref[...], xl_ref[...])
    x1 = x + gate1 * _dot(merged.astype(BF16), wo_ref[...])
    x1_ref[...] = x1
    h2 = _rms(x1, n2w_ref[...]) * (1.0 + scale2) + shift2
    h2b = h2.astype(BF16)
    h2_ref[...] = h2b

    logits = (_dot(h2b, rw_ref[...]) + rb_ref[...]).T[0:N_EXPERTS, :]
    expert = lax.broadcasted_iota(I32, (N_EXPERTS, TM), 0)
    ids, vals = [], []
    for k in range(TOP_K):
        m = jnp.max(logits, axis=0, keepdims=True)
        idx = jnp.min(jnp.where(logits == m, expert, N_EXPERTS), axis=0, keepdims=True)
        ids.append(idx)
        vals.append(m)
        logits = jnp.where(expert == idx, -jnp.inf, logits)
    es = [jnp.exp(v - vals[0]) for v in vals]
    denom = functools.reduce(jnp.add, es)
    srow = lax.broadcasted_iota(I32, (LANES, TM), 0)
    ti_t = jnp.zeros((LANES, TM), F32)
    tg_t = jnp.zeros((LANES, TM), F32)
    chosen_t = jnp.zeros((LANES, TM), F32)
    for k in range(TOP_K):
        ti_t = jnp.where(srow == k, ids[k].astype(F32), ti_t)
        tg_t = jnp.where(srow == k, es[k] / denom, tg_t)
        chosen_t = jnp.where(srow == ids[k], 1.0, chosen_t)
    ti_ref[...] = ti_t.T.astype(I32)
    tg_ref[...] = tg_t.T
    cnt_ref[0] = _dot_nt(jnp.ones((SUBLANES, TM), BF16), chosen_t.astype(BF16)).astype(I32)


def _merge(ta, att_ctx, att_lat, yf, yb, xact, z, g, xc, xl, mod3, lw):
    row = lambda n: pl.BlockSpec((TM, n), lambda i: (i, 0))
    vw = MLA_HEADS * V_HEAD
    out_shape = (jax.ShapeDtypeStruct((N_TOK, D_MODEL), F32), jax.ShapeDtypeStruct((N_TOK, D_MODEL), BF16),
                 jax.ShapeDtypeStruct((N_TOK, LANES), I32), jax.ShapeDtypeStruct((N_TOK, LANES), F32))
    cnt_shape = jax.ShapeDtypeStruct((N_TILES, SUBLANES, LANES), I32)
    cnt_spec = pl.BlockSpec((1, SUBLANES, LANES), lambda i: (i, 0, 0))
    return pl.pallas_call(
        _merge_kernel,
        grid=(N_TILES,),
        in_specs=[
            row(CONV_DIM),
            pl.BlockSpec((TM, vw), lambda i: (jnp.minimum(i, CTX_TILES - 1), 0)),
            pl.BlockSpec((TM, vw), lambda i: (jnp.maximum(i - CTX_TILES, 0), 0)),
            row(SSM_INNER), row(SSM_INNER), row(SSM_INNER),
            row(SSM_INNER), row(3 * D_MODEL),
            pl.BlockSpec((TM, D_MODEL), lambda i: (jnp.minimum(i, CTX_TILES - 1), 0)),
            pl.BlockSpec((TM, D_MODEL), lambda i: (jnp.maximum(i - CTX_TILES, 0), 0)),
            pl.BlockSpec((1, 1, 6 * D_MODEL), lambda i: (_mod_row(i), 0, 0)),
            _resident((CONV_DIM, D_MODEL)), _resident((MLA_HEADS * V_HEAD, D_MODEL)),
            _resident((1, SSM_INNER)), _resident((1, SSM_INNER)), _resident((SSM_INNER, D_MODEL)),
            _resident((D_MODEL, D_MODEL)), _resident((1, D_MODEL)),
            _resident((D_MODEL, LANES)), _resident((1, LANES)),
        ],
        out_specs=tuple(row(s.shape[1]) for s in out_shape) + (cnt_spec,),
        out_shape=out_shape + (cnt_shape,),
        compiler_params=pltpu.CompilerParams(dimension_semantics=("arbitrary",), vmem_limit_bytes=VMEM_LIMIT),
        name="merge",
    )(ta, att_ctx, att_lat, yf, yb, xact, z, g, xc, xl, mod3, lw["w_out_a"], lw["w_o_mla"], lw["d_skip"],
      lw["ssm_norm_w"],
      lw["w_o_ssm"], lw["w_o"], lw["norm2_w"], lw["router_w"], lw["router_b"])


def _run_copies(cnt_ref, src_ref, dst_ref, first, count, bits, make_copy, start):
    def body(e, carry):
        n = cnt_ref[first + e]
        s0 = src_ref[first + e] if src_ref is not None else 0
        d0 = dst_ref[first + e]

        def pieces(some_bits):
            for b in some_bits:
                above = (n >> (b + 1)) << (b + 1)

                @pl.when(((n >> b) & 1) == 1)
                def _():
                    cp = make_copy(pl.multiple_of(s0 + above, RUN_ALIGN), pl.multiple_of(d0 + above, RUN_ALIGN),
                                   1 << b)
                    if start:
                        cp.start()
                    else:
                        cp.wait()

        large = [b for b in bits if b >= COMMON_BITS]

        @pl.when(n >= (1 << COMMON_BITS))
        def _():
            pieces(large)
        pieces([b for b in bits if b < COMMON_BITS])
        return carry
    lax.fori_loop(0, count, body, 0)


def _wait_rows(total, make_copy):
    for b in WAIT_BITS:
        @pl.when(((total >> b) & 1) == 1)
        def _():
            make_copy(0, 0, 1 << b).wait()


def _dispatch_kernel(cnt_ref, off_ref, run_ref, tcnt_ref, tdst_ref, nu_ref, h2_ref, ti_ref, tg_ref, offv_ref,
                     xs_ref, lp_ref, stage, zeros, perm_s, ghi_s, glo_s, sem, semz):
    i = pl.program_id(0)

    def copy_out(s_):
        return lambda s, d, n: pltpu.make_async_copy(stage.at[s_, pl.ds(s, n), :], xs_ref.at[pl.ds(d, n), :],
                                                     sem.at[s_])

    def tile_rows(t):
        last = t * N_EXPERTS + N_EXPERTS - 1
        return off_ref[last] + cnt_ref[last]

    def zero_fill(start):
        zero_out = lambda s, d, n: pltpu.make_async_copy(zeros.at[pl.ds(0, n), :], xs_ref.at[pl.ds(d, n), :], semz)
        _run_copies(tcnt_ref, None, tdst_ref, 0, N_EXPERTS, TAIL_BITS, zero_out, start)
        zrows = zeros.shape[0]

        def body(b, carry):
            for part in range(MOE_BLK // zrows):
                cp = zero_out(0, pl.multiple_of(b * MOE_BLK + part * zrows, RUN_ALIGN), zrows)
                if start:
                    cp.start()
                else:
                    cp.wait()
            return carry
        lax.fori_loop(nu_ref[0], N_BLOCKS, body, 0)

    @pl.when(i == 0)
    def _():
        zeros[...] = jnp.zeros_like(zeros)
        zero_fill(True)

    slots = [(i % 2) * DISPATCH_TILES + u for u in range(DISPATCH_TILES)]

    @pl.when(i >= 2)
    def _():
        for u in range(DISPATCH_TILES):
            _wait_rows(tile_rows((i - 2) * DISPATCH_TILES + u), copy_out(slots[u]))

    for u in range(DISPATCH_TILES):
        tile = i * DISPATCH_TILES + u
        toks = slice(u * TM, (u + 1) * TM)
        lane = lax.broadcasted_iota(I32, (TM, LANES), 1)
        picks = [jnp.where(lane == ti_ref[toks, k:k + 1], 1.0, 0.0) for k in range(TOP_K)]
        ri = lax.broadcasted_iota(I32, (TM, TM), 0)
        ci = lax.broadcasted_iota(I32, (TM, TM), 1)
        earlier = jnp.where(ci < ri, 1.0, 0.0).astype(BF16)
        base = _dot(earlier, functools.reduce(jnp.add, picks).astype(BF16)) + offv_ref[u][0:1, :]
        lp = jnp.full((TM, LANES), -1.0, F32)
        for k in range(TOP_K):
            lp = jnp.where(lane == k, jnp.sum(picks[k] * base, axis=-1, keepdims=True), lp)
        lp_ref[toks, :] = lp.astype(I32)

        lp_t = lp.T
        tg_t = tg_ref[toks, :].T
        for c in range(STAGE_ROWS // LANES):
            rows = slice(c * LANES, (c + 1) * LANES)
            row = (lax.broadcasted_iota(I32, (LANES, TM), 0) + c * LANES).astype(F32)
            perm = jnp.zeros((LANES, TM), F32)
            gates = jnp.zeros((LANES, TM), F32)
            for k in range(TOP_K):
                hit = row == lp_t[k:k + 1, :]
                perm = jnp.where(hit, 1.0, perm)
                gates = jnp.where(hit, tg_t[k:k + 1, :], gates)
            perm_s[u, rows, :] = perm.astype(BF16)
            g_hi = gates.astype(BF16)
            ghi_s[u, rows, :] = g_hi
            glo_s[u, rows, :] = (gates - g_hi.astype(F32)).astype(BF16)
        stage[slots[u], :, 0:D_MODEL] = _dot(perm_s[u], h2_ref[toks, :])
        ones = jnp.ones((TM, LANES), BF16)
        stage[slots[u], :, D_MODEL:XS_COLS] = _dot(ghi_s[u], ones) + _dot(glo_s[u], ones)
        _run_copies(cnt_ref, off_ref, run_ref, tile * N_EXPERTS, N_EXPERTS, RUN_BITS, copy_out(slots[u]), True)

    @pl.when(i == N_TILES // DISPATCH_TILES - 1)
    def _():
        for u in range(DISPATCH_TILES):
            _wait_rows(tile_rows((i - 1) * DISPATCH_TILES + u), copy_out((1 - i % 2) * DISPATCH_TILES + u))
            _wait_rows(tile_rows(i * DISPATCH_TILES + u), copy_out(slots[u]))
        zero_fill(False)


def _dispatch(rt, h2, top_i, top_g):
    step_rows = DISPATCH_TILES * TM
    row = lambda n: pl.BlockSpec((step_rows, n), lambda i, *_: (i, 0))
    scratch = lambda: pltpu.VMEM((DISPATCH_TILES, STAGE_ROWS, TM), BF16)
    return pl.pallas_call(
        _dispatch_kernel,
        grid_spec=pltpu.PrefetchScalarGridSpec(
            num_scalar_prefetch=6,
            grid=(N_TILES // DISPATCH_TILES,),
            in_specs=[row(D_MODEL), row(LANES), row(LANES),
                      pl.BlockSpec((DISPATCH_TILES, SUBLANES, LANES), lambda i, *_: (i, 0, 0))],
            out_specs=(pl.BlockSpec(memory_space=pl.ANY), row(LANES)),
            scratch_shapes=[
                pltpu.VMEM((2 * DISPATCH_TILES, STAGE_ROWS, XS_COLS), F32),
                pltpu.VMEM((1 << TAIL_BITS[0], XS_COLS), F32),
                scratch(), scratch(), scratch(),
                pltpu.SemaphoreType.DMA((2 * DISPATCH_TILES,)),
                pltpu.SemaphoreType.DMA,
            ],
        ),
        out_shape=(jax.ShapeDtypeStruct((N_SLOTS, XS_COLS), F32), jax.ShapeDtypeStruct((N_TOK, LANES), I32)),
        compiler_params=pltpu.CompilerParams(dimension_semantics=("arbitrary",), vmem_limit_bytes=VMEM_LIMIT),
        name="dispatch",
    )(rt["cnt"], rt["off"], rt["run"], rt["tail_cnt"], rt["tail_dst"], rt["n_used"], h2, top_i, top_g, rt["off_v"])


def _moe_kernel(layer, be_ref, nxt_ref, nv_ref, nu_ref, x_ref, wgu_hbm, wdn_hbm, bg_ref, bu_ref, bd_ref, sel_ref,
                y_ref, wgu_ref, wdn_ref, wgu_s, wdn_s, sem):
    i = pl.program_id(0)
    n_used = nu_ref[0]

    def fetch(e):
        return (pltpu.make_async_copy(wgu_hbm.at[layer, e], wgu_ref, sem.at[0]),
                pltpu.make_async_copy(wdn_hbm.at[layer, e], wdn_ref, sem.at[1]))

    @pl.when(i < n_used)
    def _():
        @pl.when(jnp.logical_or(i == 0, be_ref[i] != be_ref[jnp.maximum(i - 1, 0)]))
        def _():
            @pl.when(i == 0)
            def _():
                for cp in fetch(be_ref[0]):
                    cp.start()
            for cp in fetch(be_ref[i]):
                cp.wait()
            half = LANES
            for c in range(2 * EXPERT_FF // (2 * half)):
                r = _dot(wgu_ref[:, c * 2 * half:(c + 1) * 2 * half].astype(BF16), sel_ref[...])
                wgu_s[:, c * half:(c + 1) * half] = r[:, 0:half].astype(BF16)
                wgu_s[:, EXPERT_FF + c * half:EXPERT_FF + (c + 1) * half] = r[:, half:2 * half].astype(BF16)
            wdn_s[...] = wdn_ref[...].astype(BF16)
            nxt = nxt_ref[i]

            @pl.when(nxt >= 0)
            def _():
                for cp in fetch(nxt):
                    cp.start(priority=1)

        def expert_rows(rows):
            gu = _dot(x_ref[0:rows, 0:D_MODEL].astype(BF16), wgu_s[...])
            gate = jnp.minimum(gu[:, 0:EXPERT_FF] + bg_ref[0], SWIGLU_LIMIT)
            up = jnp.clip(gu[:, EXPERT_FF:2 * EXPERT_FF] + bu_ref[0], -SWIGLU_LIMIT, SWIGLU_LIMIT)
            act = gate * jax.nn.sigmoid(SWIGLU_ALPHA * gate) * (up + 1.0)
            y = _dot(act.astype(BF16), wdn_s[...]) + bd_ref[0]
            slot_gate = x_ref[0:rows, D_MODEL:XS_COLS]
            for j in range(D_MODEL // LANES):
                y_ref[0:rows, j * LANES:(j + 1) * LANES] = y[:, j * LANES:(j + 1) * LANES] * slot_gate

        parts = (nv_ref[i] + MOE_PART - 1) // MOE_PART
        for n_parts in range(1, MOE_BLK // MOE_PART + 1):
            @pl.when(parts == n_parts)
            def _():
                rows = n_parts * MOE_PART
                expert_rows(rows)
                if rows < MOE_BLK:
                    y_ref[rows:MOE_BLK, :] = jnp.zeros((MOE_BLK - rows, D_MODEL), F32)

    @pl.when(i >= n_used)
    def _():
        y_ref[...] = jnp.zeros_like(y_ref)


def _bias_split_kernel(b_ref, sel_ref, o_ref):
    for c in range(2 * EXPERT_FF // (2 * LANES)):
        terms = _split3(b_ref[:, c * 2 * LANES:(c + 1) * 2 * LANES])
        r = functools.reduce(jnp.add, [_dot(t, sel_ref[...]) for t in terms])
        o_ref[:, c * LANES:(c + 1) * LANES] = r[:, 0:LANES]
        o_ref[:, EXPERT_FF + c * LANES:EXPERT_FF + (c + 1) * LANES] = r[:, LANES:2 * LANES]


def _bias_split(b_gu, sel):
    n = DEPTH * N_EXPERTS
    out = pl.pallas_call(
        _bias_split_kernel,
        out_shape=jax.ShapeDtypeStruct((n, 2 * EXPERT_FF), F32),
        name="bias_split",
    )(b_gu.reshape(n, 2 * EXPERT_FF), sel)
    return out.reshape(n, 1, 2 * EXPERT_FF)


def _moe(layer, rt, xs, w_gu, w_down, b_gu_split, b_down, sel):
    first = layer * N_EXPERTS
    bg = pl.BlockSpec((1, 1, EXPERT_FF), lambda i, be, *_: (first + be[i], 0, 0))
    bu = pl.BlockSpec((1, 1, EXPERT_FF), lambda i, be, *_: (first + be[i], 0, 1))
    bd = pl.BlockSpec((1, 1, D_MODEL), lambda i, be, *_: (first + be[i], 0, 0))
    return pl.pallas_call(
        functools.partial(_moe_kernel, layer),
        grid_spec=pltpu.PrefetchScalarGridSpec(
            num_scalar_prefetch=4,
            grid=(N_BLOCKS,),
            in_specs=[
                pl.BlockSpec((MOE_BLK, XS_COLS), lambda i, be, nx, nv, nu: (jnp.minimum(i, nu[0] - 1), 0)),
                pl.BlockSpec(memory_space=pl.ANY),
                pl.BlockSpec(memory_space=pl.ANY),
                bg, bu, bd,
                pl.BlockSpec((2 * LANES, 2 * LANES), lambda i, *_: (0, 0)),
            ],
            out_specs=pl.BlockSpec((MOE_BLK, D_MODEL), lambda i, *_: (i, 0)),
            scratch_shapes=[
                pltpu.VMEM((D_MODEL, 2 * EXPERT_FF), F32),
                pltpu.VMEM((EXPERT_FF, D_MODEL), F32),
                pltpu.VMEM((D_MODEL, 2 * EXPERT_FF), BF16),
                pltpu.VMEM((EXPERT_FF, D_MODEL), BF16),
                pltpu.SemaphoreType.DMA((2,)),
            ],
        ),
        out_shape=jax.ShapeDtypeStruct((N_SLOTS, D_MODEL), F32),
        compiler_params=pltpu.CompilerParams(dimension_semantics=("arbitrary",), vmem_limit_bytes=VMEM_LIMIT),
        name="moe",
    )(rt["blk_e"], rt["blk_next"], rt["blk_rows"], rt["n_used"], xs, w_gu, w_down, b_gu_split, b_gu_split,
      b_down.reshape(DEPTH * N_EXPERTS, 1, D_MODEL), sel)


def _combine_kernel(cnt_ref, off_ref, run_ref, y_ref, lp_ref, x1_ref, mod_ref, fw_ref,
                    xc_ref, xl_ref, yc_ref, yl_ref, stage, mine_s, sem):
    i = pl.program_id(0)
    n_steps = N_TILES // DISPATCH_TILES
    slots = [(i % 2) * DISPATCH_TILES + u for u in range(DISPATCH_TILES)]

    def copy_in(s_):
        return lambda s, d, n: pltpu.make_async_copy(y_ref.at[pl.ds(d, n), :], stage.at[s_, pl.ds(s, n), :],
                                                     sem.at[s_])

    def fetch_step(step, parity):
        for u in range(DISPATCH_TILES):
            _run_copies(cnt_ref, off_ref, run_ref, (step * DISPATCH_TILES + u) * N_EXPERTS, N_EXPERTS, RUN_BITS,
                        copy_in(parity * DISPATCH_TILES + u), True)

    @pl.when(i == 0)
    def _():
        stage[...] = jnp.zeros_like(stage)
        fetch_step(0, 0)

    @pl.when(i + 1 < n_steps)
    def _():
        fetch_step(i + 1, 1 - i % 2)

    gate2 = mod_ref[0][:, 5 * D_MODEL:6 * D_MODEL]
    lane = lax.broadcasted_iota(I32, (TM, LANES), 1)
    x2s = []
    for u in range(DISPATCH_TILES):
        toks = slice(u * TM, (u + 1) * TM)
        last = (i * DISPATCH_TILES + u) * N_EXPERTS + N_EXPERTS - 1
        _wait_rows(off_ref[last] + cnt_ref[last], copy_in(slots[u]))
        mine = [jnp.broadcast_to(lp_ref[toks, k:k + 1], (TM, LANES)) for k in range(TOP_K)]
        for c in range(STAGE_ROWS // LANES):
            hit = jnp.zeros((TM, LANES), F32)
            for k in range(TOP_K):
                hit = jnp.where(lane + c * LANES == mine[k], 1.0, hit)
            mine_s[u, :, c * LANES:(c + 1) * LANES] = hit.astype(BF16)
        moe = _dot(mine_s[u], stage[slots[u]].astype(BF16))
        x2s.append(x1_ref[toks, :] + gate2 * moe)
    x2 = jnp.concatenate(x2s, axis=0)
    y_norm = _rms(x2, fw_ref[...])

    @pl.when(i < CTX_TILES // DISPATCH_TILES)
    def _():
        xc_ref[...] = x2
        yc_ref[...] = y_norm

    @pl.when(i >= CTX_TILES // DISPATCH_TILES)
    def _():
        xl_ref[...] = x2
        yl_ref[...] = y_norm


def _combine(rt, y_slots, lp, x1, mod3, final_w):
    step_rows = DISPATCH_TILES * TM
    ctx_steps = CTX_TILES // DISPATCH_TILES
    row = lambda n: pl.BlockSpec((step_rows, n), lambda i, *_: (i, 0))
    ctx_row = pl.BlockSpec((step_rows, D_MODEL), lambda i, *_: (jnp.minimum(i, ctx_steps - 1), 0))
    lat_row = pl.BlockSpec((step_rows, D_MODEL), lambda i, *_: (jnp.maximum(i - ctx_steps, 0), 0))
    return pl.pallas_call(
        _combine_kernel,
        grid_spec=pltpu.PrefetchScalarGridSpec(
            num_scalar_prefetch=3,
            grid=(N_TILES // DISPATCH_TILES,),
            in_specs=[
                pl.BlockSpec(memory_space=pl.ANY),
                row(LANES), row(D_MODEL),
                pl.BlockSpec((1, 1, 6 * D_MODEL), lambda i, *_: (_mod_row(i * DISPATCH_TILES), 0, 0)),
                pl.BlockSpec((1, D_MODEL), lambda i, *_: (0, 0)),
            ],
            out_specs=(ctx_row, lat_row, ctx_row, lat_row),
            scratch_shapes=[pltpu.VMEM((2 * DISPATCH_TILES, STAGE_ROWS, D_MODEL), F32),
                            pltpu.VMEM((DISPATCH_TILES, TM, STAGE_ROWS), BF16),
                            pltpu.SemaphoreType.DMA((2 * DISPATCH_TILES,))],
        ),
        out_shape=(jax.ShapeDtypeStruct((N_CTX, D_MODEL), F32), jax.ShapeDtypeStruct((N_LAT, D_MODEL), F32),
                   jax.ShapeDtypeStruct((N_CTX, D_MODEL), F32), jax.ShapeDtypeStruct((N_LAT, D_MODEL), F32)),
        compiler_params=pltpu.CompilerParams(dimension_semantics=("arbitrary",), vmem_limit_bytes=VMEM_LIMIT),
        name="combine",
    )(rt["cnt"], rt["off"], rt["run"], y_slots, lp, x1, mod3, final_w)


def _rope_tables():
    rows = DEC_SEQ // GRID_W
    t = jnp.arange(rows * GRID_W)
    row = (t // GRID_W).astype(F32)
    col = (t % GRID_W).astype(F32)
    half = QK_ROPE // 2
    inv = ROPE_BASE ** (-jnp.arange(0, half, 2, dtype=F32) / half)
    ang_r, ang_c = row[:, None] * inv, col[:, None] * inv
    cos32 = jnp.concatenate([jnp.cos(ang_r), jnp.cos(ang_r), jnp.cos(ang_c), jnp.cos(ang_c)], axis=-1)
    sin32 = jnp.concatenate([-jnp.sin(ang_r), jnp.sin(ang_r), -jnp.sin(ang_c), jnp.sin(ang_c)], axis=-1)
    cos32 = jnp.concatenate([jnp.ones((TM, QK_ROPE), F32), cos32], axis=0)
    sin32 = jnp.concatenate([jnp.zeros((TM, QK_ROPE), F32), sin32], axis=0)
    n = cos32.shape[0]
    pad = HEAD_PAD - QK_NOPE - QK_ROPE
    cos_h = jnp.concatenate([jnp.ones((n, QK_NOPE), F32), cos32, jnp.zeros((n, pad), F32)], axis=-1)
    sin_h = jnp.concatenate([jnp.zeros((n, QK_NOPE), F32), sin32, jnp.zeros((n, pad), F32)], axis=-1)
    zeros = jnp.zeros((n, LANES - QK_ROPE), F32)
    return {
        "cosq": cos_h, "sinq": sin_h,
        "cosk": jnp.concatenate([cos32, zeros], axis=-1), "sink": jnp.concatenate([sin32, zeros], axis=-1),
        "vone": jnp.zeros((MLA_HEADS, HEAD_PAD), F32).at[:, V_HEAD].set(1.0).reshape(1, MLA_HEADS * HEAD_PAD),
    }


def _layer_weights(p, l):
    hd = QK_NOPE + QK_ROPE
    pad = HEAD_PAD - hd
    wq = p["w_uq"][l].reshape(Q_RANK, MLA_HEADS, hd)
    wq_a = jnp.pad(wq, ((0, 0), (0, 0), (0, pad)))
    wuk = jnp.pad(p["w_uk"][l], ((0, 0), (0, 0), (0, HEAD_PAD - QK_NOPE)))
    rw = jnp.pad(p["router_w"][l], ((0, 0), (0, LANES - N_EXPERTS)))
    rb = jnp.concatenate([p["router_b"][l], jnp.full((LANES - N_EXPERTS,), -jnp.inf, F32)])
    return {
        "norm1_w": p["norm1_w"][l][None], "q_norm_w": p["q_norm_w"][l][None],
        "wq_a": wq_a.reshape(Q_RANK, -1).astype(BF16),
        "kv_norm_w": p["kv_norm_w"][l][None],
        "wuk": wuk.reshape(KV_RANK, -1).astype(BF16),
        "wuv": jnp.pad(p["w_uv"][l], ((0, 0), (0, 0), (0, HEAD_PAD - V_HEAD))).reshape(KV_RANK, -1).astype(BF16),
        "wuv_c": p["w_uv"][l].reshape(KV_RANK, -1).astype(BF16),
        "conv_a_w": p["conv_a_w"][l], "ssm_conv_w": p["ssm_conv_w"][l], "ssm_conv_b": p["ssm_conv_b"][l][None],
        "w_out_a": p["w_out_a"][l].astype(BF16), "w_o_mla": p["w_o_mla"][l].astype(BF16),
        "d_skip": jnp.repeat(p["d_skip"][l], SSM_HEAD_DIM)[None], "ssm_norm_w": p["ssm_norm_w"][l][None],
        "w_o_ssm": p["w_o_ssm"][l].astype(BF16), "w_o": p["w_o"][l].astype(BF16),
        "norm2_w": p["norm2_w"][l][None], "router_w": rw.astype(BF16), "router_b": rb[None],
    }


def _ssd_params(p, l):
    z = lambda n: jnp.zeros((n,), F32)
    lanes = lambda f, b: jnp.concatenate([z(SM_DTF), f, b, z(LANES - SM_DTB - SSM_HEADS)])
    ones = jnp.ones((SSM_HEADS,), F32)
    rows = [lanes(p["dt_bias_fwd"][l], p["dt_bias_bwd"][l]), lanes(p["a_log_fwd"][l], p["a_log_bwd"][l]),
            lanes(ones, ones)]
    return jnp.concatenate([jnp.stack(rows), jnp.zeros((SUBLANES - 3, LANES), F32)], axis=0)


def _routing(cnt_tiles):
    cnt = cnt_tiles[:, 0, 0:N_EXPERTS]
    cnt = (cnt + RUN_ALIGN - 1) // RUN_ALIGN * RUN_ALIGN
    per_expert = jnp.sum(cnt, axis=0)
    padded = (per_expert + MOE_BLK - 1) // MOE_BLK * MOE_BLK
    pad_end = jnp.cumsum(padded)
    pad_start = pad_end - padded
    run = pad_start[None, :] + jnp.cumsum(cnt, axis=0) - cnt
    off = jnp.cumsum(cnt, axis=1) - cnt
    starts = jnp.arange(N_BLOCKS, dtype=I32) * MOE_BLK
    blk_e = jnp.minimum(jnp.sum((pad_end[None, :] <= starts[:, None]).astype(I32), axis=1), N_EXPERTS - 1)
    off_v = jnp.zeros((N_TILES, SUBLANES, LANES), F32).at[:, 0, 0:N_EXPERTS].set(off.astype(F32))
    ids = jnp.arange(N_EXPERTS, dtype=I32)
    later = jnp.logical_and(ids[None, :] > ids[:, None], padded[None, :] > 0)
    nxt = jnp.min(jnp.where(later, ids[None, :], N_EXPERTS), axis=1)
    nxt = jnp.where(nxt == N_EXPERTS, -1, nxt)
    per_block = lambda v: jnp.sum(jnp.where(blk_e[:, None] == ids[None, :], v[None, :], 0), axis=1)
    blk_rows = jnp.clip(per_block(pad_start + per_expert) - starts, 0, MOE_BLK)
    return {
        "blk_next": per_block(nxt).astype(I32), "blk_rows": blk_rows.astype(I32),
        "cnt": cnt.reshape(-1).astype(I32), "off": off.reshape(-1).astype(I32), "run": run.reshape(-1).astype(I32),
        "tail_cnt": (padded - per_expert).astype(I32), "tail_dst": (pad_start + per_expert).astype(I32),
        "blk_e": blk_e.astype(I32), "n_used": (pad_end[-1] // MOE_BLK).astype(I32).reshape(1), "off_v": off_v,
    }


def _deinterleave_matrix():
    k = jnp.arange(2 * LANES)[:, None]
    n = jnp.arange(2 * LANES)[None, :]
    src = jnp.where(n < LANES, 2 * n, 2 * (n - LANES) + 1)
    return (k == src).astype(BF16)


def kernel(x_prompt, x_sample, cache_ckv, cache_kpe, state_ssm_fwd, state_ssm_bwd, c, c_ctx, w_ada, b_ada, norm1_w, w_in, conv_a_w, w_out_a, q_norm_w, w_uq, kv_norm_w, w_uk, w_uv, w_o_mla, ssm_conv_w, ssm_conv_b, dt_bias_fwd, dt_bias_bwd, a_log_fwd, a_log_bwd, d_skip, ssm_norm_w, w_o_ssm, w_o, norm2_w, router_w, router_b, w_gu, b_gu, w_down, b_down, final_norm_w):
    p = dict(norm1_w=norm1_w, w_in=w_in, conv_a_w=conv_a_w, w_out_a=w_out_a, q_norm_w=q_norm_w, w_uq=w_uq,
             kv_norm_w=kv_norm_w, w_uk=w_uk, w_uv=w_uv, w_o_mla=w_o_mla, ssm_conv_w=ssm_conv_w,
             ssm_conv_b=ssm_conv_b, dt_bias_fwd=dt_bias_fwd, dt_bias_bwd=dt_bias_bwd, a_log_fwd=a_log_fwd,
             a_log_bwd=a_log_bwd, d_skip=d_skip, ssm_norm_w=ssm_norm_w, w_o_ssm=w_o_ssm, w_o=w_o,
             norm2_w=norm2_w, router_w=router_w, router_b=router_b, b_gu=b_gu, b_down=b_down)
    xc = x_prompt.reshape(N_CTX, D_MODEL)
    xl = x_sample.reshape(N_LAT, D_MODEL)
    cond8 = jnp.concatenate([c_ctx[None], c, jnp.zeros((SUBLANES - 1 - DEC_BATCH, D_MODEL), F32)], axis=0)
    mods = _ada_mods(cond8, w_ada, b_ada)
    tabs = _rope_tables()
    sel = _deinterleave_matrix()
    b_gu_split = _bias_split(b_gu, sel)
    w_in_t = jnp.swapaxes(w_in, 1, 2)
    final_w = final_norm_w[None]
    hp = SSM_INNER

    caches, states = (), ()
    y_ctx = y_lat = None
    for l in range(DEPTH):
        lw = _layer_weights(p, l)
        mod3 = mods[l].reshape(SUBLANES, 1, 6 * D_MODEL)
        ta, q, ckv, kpe, kf, v_ctx, v_lat, small, z, xact, g = _inproj(l, xc, xl, mod3, w_in_t, lw, tabs, caches)
        caches = (ckv, kpe)
        kpe_c = jnp.pad(cache_kpe[:, l].reshape(DEC_BATCH * PAST_LEN, QK_ROPE), ((0, 0), (0, LANES - QK_ROPE)))
        kf_c, v_c = _kvcache(cache_ckv[:, l].reshape(DEC_BATCH * PAST_LEN, KV_RANK), kpe_c, lw, tabs)
        att_ctx, att_lat = _attention(q, kf, v_ctx, v_lat, kf_c, v_c)
        init_f = state_ssm_fwd[:, l].reshape(DEC_BATCH, hp, SSM_STATE)
        init_b = state_ssm_bwd[:, l].reshape(DEC_BATCH, hp, SSM_STATE)
        yf, yb, sf, sb = _ssd(l, xact, small, init_f, init_b, _ssd_params(p, l), states)
        states = (sf, sb)
        x1, h2, top_i, top_g, cnt_tiles = _merge(ta, att_ctx, att_lat, yf, yb, xact, z, g, xc, xl, mod3, lw)
        rt = _routing(cnt_tiles)
        xs, lp = _dispatch(rt, h2, top_i, top_g)
        y_slots = _moe(l, rt, xs, w_gu, w_down, b_gu_split, b_down, sel)
        xc, xl, y_ctx, y_lat = _combine(rt, y_slots, lp, x1, mod3, final_w)

    y_prompt = y_ctx.reshape(BATCH, SEQ, D_MODEL)
    y_sample = y_lat.reshape(DEC_BATCH, DEC_SEQ, D_MODEL)
    state_shape = (BATCH, DEPTH, SSM_HEADS, SSM_HEAD_DIM, SSM_STATE)
    return (y_prompt, y_sample, caches[0], caches[1], states[0].reshape(state_shape), states[1].reshape(state_shape))
```

```python
import functools
import math

import jax
import jax.numpy as jnp
from jax import lax
from jax.experimental import pallas as pl
from jax.experimental.pallas import tpu as pltpu

F32 = jnp.float32
BF16 = jnp.bfloat16
I32 = jnp.int32

D_MODEL = 1024
BATCH = 16
SEQ = 256
DEPTH = 2
DEC_BATCH = 2
DEC_SEQ = 2048
PAST_LEN = 512
GRID_W = 64
NORM_EPS = 1e-6
CONV_DIM = 512
MLA_HEADS = 8
Q_RANK = 384
KV_RANK = 256
QK_NOPE = 64
QK_ROPE = 32
V_HEAD = 64
ROPE_BASE = 10000.0
MLA_SCALE = (QK_NOPE + QK_ROPE) ** -0.5
SSM_HEADS = 16
SSM_HEAD_DIM = 64
SSM_INNER = SSM_HEADS * SSM_HEAD_DIM
SSM_GROUPS = 2
SSM_STATE = 128
SSM_CONV_CH = SSM_INNER + 2 * SSM_GROUPS * SSM_STATE
N_EXPERTS = 32
TOP_K = 4
EXPERT_FF = D_MODEL
SWIGLU_ALPHA = 1.702
SWIGLU_LIMIT = 7.0

N_CTX = BATCH * SEQ
N_LAT = DEC_BATCH * DEC_SEQ
N_TOK = N_CTX + N_LAT
N_SEQS = BATCH + DEC_BATCH

LANES = 128
SUBLANES = 8
HEAD_PAD = 128
TM = 256
N_TILES = N_TOK // TM
CTX_TILES = N_CTX // TM
LAT_TILES_PER_SEQ = DEC_SEQ // TM
TQ_LAT = 512
CHUNK = 128
SSD_STEP = 2 * CHUNK
CTX_STEPS_PER_SEQ = SEQ // SSD_STEP
LAT_STEPS_PER_SEQ = DEC_SEQ // SSD_STEP
N_CTX_STEPS = N_CTX // SSD_STEP
N_SSD_STEPS = N_TOK // SSD_STEP
MOE_BLK = 512
MOE_PART = 128
DISPATCH_TILES = 2
N_ASSIGN = N_TOK * TOP_K
RUN_ALIGN = SUBLANES
RUN_BITS = tuple(range(8, 2, -1))
TAIL_BITS = tuple(range(8, 2, -1))
COMMON_BITS = 6
WAIT_BITS = tuple(range(10, 2, -1))
STAGE_ROWS = 1280
XS_COLS = D_MODEL + LANES
N_BLOCKS = -(-(N_ASSIGN + N_TILES * N_EXPERTS * (RUN_ALIGN - 1) + N_EXPERTS * (MOE_BLK - 1)) // MOE_BLK)
N_SLOTS = N_BLOCKS * MOE_BLK
VMEM_LIMIT = 56 * 1024 * 1024

C_A3 = 0
C_CQ = C_A3 + 3 * CONV_DIM
C_CKV = C_CQ + Q_RANK
C_Z = C_CKV + KV_RANK
C_XBC = C_Z + SSM_INNER
C_GATE = C_XBC + SSM_CONV_CH
C_SMALL = C_GATE + 3 * D_MODEL
IN_COLS2 = C_SMALL + LANES
SM_DTF = QK_ROPE
SM_DTB = QK_ROPE + SSM_HEADS
S_CQ = 3 * CONV_DIM
S_CKV = S_CQ + Q_RANK
S_KPE = S_CKV + KV_RANK
S_Z = S_KPE + QK_ROPE
S_XBC = S_Z + SSM_INNER
S_DTF = S_XBC + SSM_CONV_CH
S_GATE = S_DTF + 2 * SSM_HEADS
IN_COLS = S_GATE + 3 * D_MODEL
W_SEGMENTS = ((C_A3, 0, 3 * CONV_DIM), (C_CQ, S_CQ, Q_RANK), (C_CKV, S_CKV, KV_RANK), (C_Z, S_Z, SSM_INNER),
              (C_XBC, S_XBC, SSM_CONV_CH), (C_GATE, S_GATE, 3 * D_MODEL))
W_PIECE = 512


def _rms(x, w):
    return x * lax.rsqrt(jnp.mean(x * x, axis=-1, keepdims=True) + NORM_EPS) * w


def _silu(x):
    return x * jax.nn.sigmoid(x)


def _dot(a, b):
    return jnp.dot(a, b, preferred_element_type=F32)


def _dot_nt(a, b):
    return lax.dot_general(a, b, (((1,), (1,)), ((), ())), preferred_element_type=F32)


def _resident(shape):
    nd = len(shape)
    return pl.BlockSpec(shape, lambda *_: (0,) * nd, pipeline_mode=pl.Buffered(1))


def _mod_row(i):
    return jnp.where(i < CTX_TILES, 0, 1 + (i - CTX_TILES) // LAT_TILES_PER_SEQ)


def _pos_block(i):
    return jnp.where(i < CTX_TILES, 0, 1 + (i - CTX_TILES) % LAT_TILES_PER_SEQ)


def _ada_kernel(c_ref, w_ref, b_ref, o_ref):
    s = _silu(c_ref[...]).astype(BF16)
    o_ref[0] = _dot(s, w_ref[0].astype(BF16)) + b_ref[0]


def _ada_mods(cond8, w_ada, b_ada):
    tn = 1536
    n_mod = 6 * D_MODEL
    return pl.pallas_call(
        _ada_kernel,
        grid=(DEPTH, n_mod // tn),
        in_specs=[
            pl.BlockSpec((SUBLANES, D_MODEL), lambda l, j: (0, 0)),
            pl.BlockSpec((1, D_MODEL, tn), lambda l, j: (l, 0, j)),
            pl.BlockSpec((1, 1, tn), lambda l, j: (l, 0, j)),
        ],
        out_specs=pl.BlockSpec((1, SUBLANES, tn), lambda l, j: (l, 0, j)),
        out_shape=jax.ShapeDtypeStruct((DEPTH, SUBLANES, n_mod), F32),
        compiler_params=pltpu.CompilerParams(dimension_semantics=("arbitrary", "arbitrary")),
        name="ada_mods",
    )(cond8, w_ada, b_ada.reshape(DEPTH, 1, n_mod))


def _relayout_w_in(layer, wt_hbm, w2, stg, small_stg, sem, small_sem):
    pieces = [(dst + p, src + p, min(W_PIECE, width - p))
              for dst, src, width in W_SEGMENTS for p in range(0, width, W_PIECE)]
    fetch = lambda k: pltpu.make_async_copy(wt_hbm.at[layer, pl.ds(pieces[k][1], pieces[k][2]), :],
                                            stg.at[k % 2, pl.ds(0, pieces[k][2]), :], sem.at[k % 2])
    small_stg[...] = jnp.zeros_like(small_stg)
    small_copies = [
        pltpu.make_async_copy(wt_hbm.at[layer, pl.ds(S_KPE, QK_ROPE), :], small_stg.at[pl.ds(0, QK_ROPE), :],
                              small_sem.at[0]),
        pltpu.make_async_copy(wt_hbm.at[layer, pl.ds(S_DTF, 2 * SSM_HEADS), :],
                              small_stg.at[pl.ds(SM_DTF, 2 * SSM_HEADS), :], small_sem.at[1]),
    ]
    for cp in small_copies:
        cp.start()
    fetch(0).start()
    for k, (dst, _, width) in enumerate(pieces):
        if k + 1 < len(pieces):
            fetch(k + 1).start()
        fetch(k).wait()
        w2[:, dst:dst + width] = stg[k % 2, 0:width, :].T.astype(BF16)
    for cp in small_copies:
        cp.wait()
    w2[:, C_SMALL:IN_COLS2] = small_stg[...].T.astype(BF16)


def _place_rope_key(k_nope, kpe):
    shifted = pltpu.roll(kpe, QK_NOPE, 1)
    return jnp.concatenate([k_nope[:, h * HEAD_PAD:(h + 1) * HEAD_PAD] + shifted for h in range(MLA_HEADS)],
                           axis=1).astype(BF16)


def _rot_partner(x):
    n = x.shape[1]
    lane = lax.broadcasted_iota(I32, x.shape, 1)
    quarter = QK_ROPE // 4
    return jnp.where(lane % (2 * quarter) < quarter, pltpu.roll(x, n - quarter, 1), pltpu.roll(x, quarter, 1))


def _first_layer_slabs(ref, value):
    ref[0, 0] = value
    ref[0, 1:DEPTH] = jnp.zeros((DEPTH - 1,) + value.shape, value.dtype)


def _inproj_kernel(layer, *refs):
    n_in = 23 + (2 if layer > 0 else 0)
    (xc_ref, xcp_ref, xcn_ref, xl_ref, xlp_ref, xln_ref, mod_ref, n1w_ref, w_hbm, qnw_ref, wq_ref, kvnw_ref, wuk_ref,
     wuv_ref, wuvc_ref, vone_ref, cosq_ref, sinq_ref, cosk_ref, sink_ref, caw_ref, cw_ref, cb_ref) = refs[:23]
    (ta_ref, q_ref, ckv_ref, kpe_ref, kf_ref, vc_ref, vl_ref, small_ref, z_ref, xact_ref, g_ref,
     w2, stg, small_stg, sem, small_sem) = refs[n_in:]
    i = pl.program_id(0)

    @pl.when(i == 0)
    def _():
        _relayout_w_in(layer, w_hbm, w2, stg, small_stg, sem, small_sem)

    mod = mod_ref[0]
    shift1 = mod[:, 0:D_MODEL]
    scale1 = mod[:, D_MODEL:2 * D_MODEL]
    x_ext = jnp.where(i < CTX_TILES, jnp.concatenate([xcp_ref[...], xc_ref[...], xcn_ref[...]], axis=0),
                      jnp.concatenate([xlp_ref[...], xl_ref[...], xln_ref[...]], axis=0))
    h_ext = _rms(x_ext, n1w_ref[...]) * (1.0 + scale1) + shift1
    hb_ext = h_ext.astype(BF16)
    hb = h_ext[SUBLANES:SUBLANES + TM].astype(BF16)

    j = (i - CTX_TILES) % LAT_TILES_PER_SEQ
    is_ctx = i < CTX_TILES
    keep_prev = jnp.where(jnp.logical_or(is_ctx, j == 0), 0.0, 1.0)
    keep_next = jnp.where(jnp.logical_or(is_ctx, j == LAT_TILES_PER_SEQ - 1), 0.0, 1.0)
    row = lax.broadcasted_iota(I32, (TM, 1), 0)
    prev_mask = jnp.where(row == 0, keep_prev, 1.0)
    next_mask = jnp.where(row == TM - 1, keep_next, 1.0)
    ext = TM + 2 * SUBLANES

    def conv3(x, w_ref):
        x_prev = pltpu.roll(x, 1, 0)[SUBLANES:SUBLANES + TM] * prev_mask
        x_next = pltpu.roll(x, ext - 1, 0)[SUBLANES:SUBLANES + TM] * next_mask
        return x_prev * w_ref[0:1, :] + x[SUBLANES:SUBLANES + TM] * w_ref[1:2, :] + x_next * w_ref[2:3, :]

    a3 = _dot(hb_ext, w2[:, C_A3:C_CQ])
    s = a3[:, 2 * CONV_DIM:3 * CONV_DIM] * a3[:, 0:CONV_DIM]
    ta_ref[...] = (a3[SUBLANES:SUBLANES + TM, CONV_DIM:2 * CONV_DIM] * conv3(s, caw_ref)).astype(BF16)
    xact_ref[...] = _silu(conv3(_dot(hb_ext, w2[:, C_XBC:C_GATE]), cw_ref) + cb_ref[...])

    def seg(a, b):
        return _dot(hb, w2[:, a:b])

    cqn = _rms(seg(C_CQ, C_CKV), qnw_ref[...]).astype(BF16)
    qa = _dot(cqn, wq_ref[...])
    qa_rot = _rot_partner(qa)
    for h in range(MLA_HEADS):
        hs = slice(h * HEAD_PAD, (h + 1) * HEAD_PAD)
        q_ref[:, hs] = (qa[:, hs] * cosq_ref[...] + qa_rot[:, hs] * sinq_ref[...]).astype(BF16)

    ckv = _rms(seg(C_CKV, C_Z), kvnw_ref[...])
    ckvb = ckv.astype(BF16)
    small = seg(C_SMALL, IN_COLS2)
    small_ref[...] = small
    kpe = small * cosk_ref[...] + _rot_partner(small) * sink_ref[...]
    kf_ref[...] = _place_rope_key(_dot(ckvb, wuk_ref[...]), kpe)
    v_lat = (_dot(ckvb, wuv_ref[...]) + vone_ref[...]).astype(BF16)
    v_ctx = _dot(ckvb, wuvc_ref[...]).astype(BF16)

    z_ref[...] = seg(C_Z, C_XBC)
    g_ref[...] = jax.nn.sigmoid(seg(C_GATE, C_SMALL))

    @pl.when(i < CTX_TILES)
    def _():
        if layer == 0:
            _first_layer_slabs(ckv_ref, ckv)
            _first_layer_slabs(kpe_ref, small[:, 0:QK_ROPE])
        else:
            ckv_ref[0] = ckv
            kpe_ref[0] = small[:, 0:QK_ROPE]
        vc_ref[...] = v_ctx

    @pl.when(i >= CTX_TILES)
    def _():
        vl_ref[...] = v_lat


def _inproj(layer, xc, xl, mod3, w_in, lw, tabs, caches):
    row = lambda n: pl.BlockSpec((TM, n), lambda i: (i, 0))
    tab = lambda n: pl.BlockSpec((TM, n), lambda i: (_pos_block(i), 0))
    per = TM // SUBLANES
    qw = MLA_HEADS * HEAD_PAD
    ctx_row = lambda n: pl.BlockSpec((TM, n), lambda i: (jnp.minimum(i, CTX_TILES - 1), 0))
    lat_row = lambda n: pl.BlockSpec((TM, n), lambda i: (jnp.maximum(i - CTX_TILES, 0), 0))

    def halo(first_tile, n_rows, side):
        last = n_rows // SUBLANES - 1
        return pl.BlockSpec((SUBLANES, D_MODEL),
                            lambda i: (jnp.clip((i - first_tile + side) * per - 1 + side, 0, last), 0))

    out_shape = (
        jax.ShapeDtypeStruct((N_TOK, CONV_DIM), BF16),
        jax.ShapeDtypeStruct((N_TOK, qw), BF16),
        jax.ShapeDtypeStruct((BATCH, DEPTH, SEQ, KV_RANK), F32),
        jax.ShapeDtypeStruct((BATCH, DEPTH, SEQ, QK_ROPE), F32),
        jax.ShapeDtypeStruct((N_TOK, qw), BF16),
        jax.ShapeDtypeStruct((N_CTX, MLA_HEADS * V_HEAD), BF16),
        jax.ShapeDtypeStruct((N_LAT, qw), BF16),
        jax.ShapeDtypeStruct((N_TOK, LANES), F32),
        jax.ShapeDtypeStruct((N_TOK, SSM_INNER), F32),
        jax.ShapeDtypeStruct((N_TOK, SSM_CONV_CH), F32),
        jax.ShapeDtypeStruct((N_TOK, 3 * D_MODEL), F32),
    )
    out_specs = [row(s.shape[-1]) for s in out_shape]
    cache_block = (1, DEPTH, SEQ) if layer == 0 else (1, None, SEQ)
    cache_index = lambda i: (jnp.minimum(i, CTX_TILES - 1), 0 if layer == 0 else layer, 0, 0)
    out_specs[2] = pl.BlockSpec(cache_block + (KV_RANK,), cache_index)
    out_specs[3] = pl.BlockSpec(cache_block + (QK_ROPE,), cache_index)
    extra = {} if layer == 0 else dict(input_output_aliases={23: 2, 24: 3})
    out_specs[5] = ctx_row(MLA_HEADS * V_HEAD)
    out_specs[6] = lat_row(qw)
    return pl.pallas_call(
        functools.partial(_inproj_kernel, layer),
        grid=(N_TILES,),
        in_specs=[
            ctx_row(D_MODEL), halo(0, N_CTX, 0), halo(0, N_CTX, 1),
            lat_row(D_MODEL), halo(CTX_TILES, N_LAT, 0), halo(CTX_TILES, N_LAT, 1),
            pl.BlockSpec((1, 1, 6 * D_MODEL), lambda i: (_mod_row(i), 0, 0)),
            _resident((1, D_MODEL)),
            pl.BlockSpec(memory_space=pl.ANY),
            _resident((1, Q_RANK)),
            _resident((Q_RANK, qw)),
            _resident((1, KV_RANK)),
            _resident((KV_RANK, qw)),
            _resident((KV_RANK, qw)),
            _resident((KV_RANK, MLA_HEADS * V_HEAD)),
            _resident((1, qw)),
            tab(HEAD_PAD), tab(HEAD_PAD), tab(LANES), tab(LANES),
            _resident((3, CONV_DIM)), _resident((3, SSM_CONV_CH)), _resident((1, SSM_CONV_CH)),
        ] + [pl.BlockSpec(memory_space=pl.ANY)] * len(caches),
        out_specs=tuple(out_specs),
        out_shape=out_shape,
        scratch_shapes=[pltpu.VMEM((D_MODEL, IN_COLS2), BF16), pltpu.VMEM((2, W_PIECE, D_MODEL), F32),
                        pltpu.VMEM((LANES, D_MODEL), F32), pltpu.SemaphoreType.DMA((2,)),
                        pltpu.SemaphoreType.DMA((2,))],
        compiler_params=pltpu.CompilerParams(dimension_semantics=("arbitrary",), vmem_limit_bytes=VMEM_LIMIT),
        name="inproj",
        **extra,
    )(xc, xc, xc, xl, xl, xl, mod3, lw["norm1_w"], w_in, lw["q_norm_w"], lw["wq_a"], lw["kv_norm_w"],
      lw["wuk"], lw["wuv"], lw["wuv_c"], tabs["vone"], tabs["cosq"], tabs["sinq"], tabs["cosk"],
      tabs["sink"], lw["conv_a_w"], lw["ssm_conv_w"], lw["ssm_conv_b"], *caches)


def _kvcache_kernel(ckv_ref, kpe_ref, wuk_ref, wuv_ref, vone_ref, kf_ref, v_ref):
    ckvb = ckv_ref[...].astype(BF16)
    kf_ref[...] = _place_rope_key(_dot(ckvb, wuk_ref[...]), kpe_ref[...])
    v_ref[...] = (_dot(ckvb, wuv_ref[...]) + vone_ref[...]).astype(BF16)


def _kvcache(ckv, kpe128, lw, tabs):
    n = ckv.shape[0]
    qw = MLA_HEADS * HEAD_PAD
    return pl.pallas_call(
        _kvcache_kernel,
        grid=(n // PAST_LEN,),
        in_specs=[
            pl.BlockSpec((PAST_LEN, KV_RANK), lambda i: (i, 0)),
            pl.BlockSpec((PAST_LEN, LANES), lambda i: (i, 0)),
            _resident((KV_RANK, qw)),
            _resident((KV_RANK, qw)),
            _resident((1, qw)),
        ],
        out_specs=(pl.BlockSpec((PAST_LEN, qw), lambda i: (i, 0)), pl.BlockSpec((PAST_LEN, qw), lambda i: (i, 0))),
        out_shape=(jax.ShapeDtypeStruct((n, qw), BF16), jax.ShapeDtypeStruct((n, qw), BF16)),
        compiler_params=pltpu.CompilerParams(dimension_semantics=("arbitrary",)),
        name="kvcache",
    )(ckv, kpe128, lw["wuk"], lw["wuv"], tabs["vone"])


def _attn_heads(q_ref, kv_refs, o_ref, acc_ref, denom_from_matmul):
    log2_scale = MLA_SCALE * math.log2(math.e)
    for h in range(MLA_HEADS):
        hs = slice(h * HEAD_PAD, (h + 1) * HEAD_PAD)
        qh = q_ref[:, hs]
        ss = [_dot_nt(qh, k_ref[:, hs]) for k_ref, _ in kv_refs]
        if denom_from_matmul:
            m = functools.reduce(jnp.maximum, [jnp.max(s, axis=-1, keepdims=True) for s in ss])
            ps = [jnp.exp2((s - m) * log2_scale) for s in ss]
            ol = functools.reduce(jnp.add, [_dot(p.astype(BF16), v_ref[:, hs])
                                            for p, (_, v_ref) in zip(ps, kv_refs)])
            acc_ref[:, h * V_HEAD:(h + 1) * V_HEAD] = ol[:, 0:V_HEAD] / ol[:, V_HEAD:V_HEAD + 1]
        else:
            ss = [s * MLA_SCALE for s in ss]
            m = functools.reduce(jnp.maximum, [jnp.max(s, axis=-1, keepdims=True) for s in ss])
            ps = [jnp.exp(s - m) for s in ss]
            l = functools.reduce(jnp.add, [jnp.sum(p, axis=-1, keepdims=True) for p in ps])
            o = functools.reduce(jnp.add, [_dot(p.astype(BF16), v_ref[:, h * V_HEAD:(h + 1) * V_HEAD])
                                           for p, (_, v_ref) in zip(ps, kv_refs)])
            acc_ref[:, h * V_HEAD:(h + 1) * V_HEAD] = o / l
    o_ref[...] = acc_ref[...].astype(BF16)


def _attn_ctx_kernel(q_ref, k_ref, v_ref, o_ref, acc_ref):
    _attn_heads(q_ref, [(k_ref, v_ref)], o_ref, acc_ref, denom_from_matmul=False)


def _attn_lat_kernel(q_ref, k_ref, v_ref, kc_ref, vc_ref, o_ref, acc_ref):
    _attn_heads(q_ref, [(k_ref, v_ref), (kc_ref, vc_ref)], o_ref, acc_ref, denom_from_matmul=True)


def _attention(q, kf, v_ctx, v_lat, kf_c, v_c):
    qw = MLA_HEADS * HEAD_PAD
    vw = MLA_HEADS * V_HEAD
    att_ctx = pl.pallas_call(
        _attn_ctx_kernel,
        grid=(BATCH,),
        in_specs=[pl.BlockSpec((SEQ, qw), lambda b: (b, 0)), pl.BlockSpec((SEQ, qw), lambda b: (b, 0)),
                  pl.BlockSpec((SEQ, vw), lambda b: (b, 0))],
        out_specs=pl.BlockSpec((SEQ, vw), lambda b: (b, 0)),
        out_shape=jax.ShapeDtypeStruct((N_CTX, vw), BF16),
        scratch_shapes=[pltpu.VMEM((SEQ, vw), F32)],
        compiler_params=pltpu.CompilerParams(dimension_semantics=("arbitrary",)),
        name="attn_ctx",
    )(q, kf, v_ctx)
    lat0 = N_CTX // DEC_SEQ
    q_tiles = DEC_SEQ // TQ_LAT
    att_lat = pl.pallas_call(
        _attn_lat_kernel,
        grid=(DEC_BATCH, q_tiles),
        in_specs=[
            pl.BlockSpec((TQ_LAT, qw), lambda b, t: (N_CTX // TQ_LAT + b * q_tiles + t, 0)),
            pl.BlockSpec((DEC_SEQ, qw), lambda b, t: (lat0 + b, 0)),
            pl.BlockSpec((DEC_SEQ, qw), lambda b, t: (b, 0)),
            pl.BlockSpec((PAST_LEN, qw), lambda b, t: (b, 0)),
            pl.BlockSpec((PAST_LEN, qw), lambda b, t: (b, 0)),
        ],
        out_specs=pl.BlockSpec((TQ_LAT, vw), lambda b, t: (b * q_tiles + t, 0)),
        out_shape=jax.ShapeDtypeStruct((N_LAT, vw), BF16),
        scratch_shapes=[pltpu.VMEM((TQ_LAT, vw), F32)],
        compiler_params=pltpu.CompilerParams(dimension_semantics=("arbitrary", "arbitrary"),
                                             vmem_limit_bytes=VMEM_LIMIT),
        name="attn_lat",
    )(q, kf, v_lat, kf_c, v_c)
    return att_ctx, att_lat


def _seq_of_step(s):
    return jnp.where(s < N_CTX_STEPS, s // CTX_STEPS_PER_SEQ,
                     BATCH + (s - N_CTX_STEPS) // LAT_STEPS_PER_SEQ)


def _step_in_seq(s):
    return jnp.where(s < N_CTX_STEPS, s % CTX_STEPS_PER_SEQ, (s - N_CTX_STEPS) % LAT_STEPS_PER_SEQ)


def _steps_in_seq(s):
    return jnp.where(s < N_CTX_STEPS, CTX_STEPS_PER_SEQ, LAT_STEPS_PER_SEQ)


def _mirror_step(s):
    return s + _steps_in_seq(s) - 1 - 2 * _step_in_seq(s)


def _split3(a):
    a1 = a.astype(BF16)
    r1 = a - a1.astype(F32)
    a2 = r1.astype(BF16)
    a3 = (r1 - a2.astype(F32)).astype(BF16)
    return a1, a2, a3


def _ssd_direction(x_ref, sm_ref, par_ref, st_ref, y_ref, lane0, backward, r0):
    rows = slice(r0, r0 + CHUNK)
    ri = lax.broadcasted_iota(I32, (CHUNK, CHUNK), 0)
    ci = lax.broadcasted_iota(I32, (CHUNK, CHUNK), 1)
    tri = (ci >= ri) if backward else (ci <= ri)
    tri_b = jnp.where(tri, 1.0, 0.0).astype(BF16)
    tot_row = 0 if backward else CHUNK - 1

    dt = jax.nn.softplus(sm_ref[rows, :] + par_ref[0:1, :])
    a = dt * (-jnp.exp(par_ref[1:2, :])) * par_ref[2:3, :]
    a1, a2, a3 = _split3(a)
    acs = (_dot(tri_b, a1) + _dot(tri_b, a2) + _dot(tri_b, a3)) * math.log2(math.e)
    acs_t = acs.T
    dt_t = dt.T
    first_head = lax.broadcasted_iota(I32, (1, LANES), 1) < SSM_HEAD_DIM

    def block_diag(pair):
        return jnp.concatenate([jnp.where(first_head, pair, 0.0), jnp.where(first_head, 0.0, pair)],
                               axis=0).astype(BF16)

    for g in range(SSM_GROUPS):
        b0 = SSM_INNER + g * SSM_STATE
        c0 = SSM_INNER + SSM_GROUPS * SSM_STATE + g * SSM_STATE
        bg = x_ref[rows, b0:b0 + SSM_STATE]
        cg = x_ref[rows, c0:c0 + SSM_STATE]
        cb = _dot_nt(cg.astype(BF16), bg.astype(BF16))
        bg_t = bg.T
        heads = SSM_HEADS // SSM_GROUPS
        for pr in range(heads // 2):
            h0 = g * heads + 2 * pr
            sl = slice(h0 * SSM_HEAD_DIM, (h0 + 2) * SSM_HEAD_DIM)
            x_bd = block_diag(x_ref[rows, sl])
            st_old = st_ref[:, sl]
            within, carried, to_state, keep = [], [], [], []
            for h in (h0, h0 + 1):
                lane = lane0 + h
                col = jnp.broadcast_to(acs[:, lane:lane + 1], (CHUNK, CHUNK))
                row = acs_t[lane:lane + 1, :]
                dt_row = dt_t[lane:lane + 1, :]
                decay = jnp.exp2(jnp.where(tri, col - row, -jnp.inf))
                within.append((cb * decay * dt_row).astype(BF16))
                carried.append((cg * jnp.exp2(col)).astype(BF16))
                tot = acs[tot_row:tot_row + 1, lane:lane + 1]
                to_state.append((bg_t * (dt_row * jnp.exp2(tot - row))).astype(BF16))
                keep.append(jnp.exp2(tot))
            lhs = jnp.concatenate(within + carried, axis=1)
            y_ref[rows, sl] = _dot(lhs, jnp.concatenate([x_bd, block_diag(st_old)], axis=0))
            st_ref[:, sl] = (st_old * jnp.where(first_head, keep[0], keep[1])
                             + _dot(jnp.concatenate(to_state, axis=1), x_bd))


def _ssd_kernel(layer, *refs):
    n_in = 7 + (2 if layer > 0 else 0)
    xf_ref, xb_ref, smf_ref, smb_ref, if_ref, ib_ref, par_ref = refs[:7]
    yf_ref, yb_ref, sf_ref, sb_ref, stf_ref, stb_ref = refs[n_in:]
    s = pl.program_id(0)
    c = _step_in_seq(s)

    @pl.when(jnp.logical_and(c == 0, s < N_CTX_STEPS))
    def _():
        stf_ref[...] = jnp.zeros_like(stf_ref)
        stb_ref[...] = jnp.zeros_like(stb_ref)

    @pl.when(jnp.logical_and(c == 0, s >= N_CTX_STEPS))
    def _():
        stf_ref[...] = if_ref[0].T
        stb_ref[...] = ib_ref[0].T

    for r0 in range(0, SSD_STEP, CHUNK):
        _ssd_direction(xf_ref, smf_ref, par_ref, stf_ref, yf_ref, SM_DTF, False, r0)
        _ssd_direction(xb_ref, smb_ref, par_ref, stb_ref, yb_ref, SM_DTB, True, SSD_STEP - CHUNK - r0)

    @pl.when(jnp.logical_and(c == _steps_in_seq(s) - 1, s < N_CTX_STEPS))
    def _():
        if layer == 0:
            _first_layer_slabs(sf_ref, stf_ref[...].T)
            _first_layer_slabs(sb_ref, stb_ref[...].T)
        else:
            sf_ref[0] = stf_ref[...].T
            sb_ref[0] = stb_ref[...].T


def _ssd(layer, xact, small, init_f, init_b, par, states):
    hp = SSM_INNER
    fwd = lambda n: pl.BlockSpec((SSD_STEP, n), lambda s: (s, 0))
    bwd = lambda n: pl.BlockSpec((SSD_STEP, n), lambda s: (_mirror_step(s), 0))
    st = pl.BlockSpec((1, DEPTH, hp, SSM_STATE) if layer == 0 else (1, None, hp, SSM_STATE),
                      lambda s: (jnp.minimum(_seq_of_step(s), BATCH - 1), 0 if layer == 0 else layer, 0, 0))
    init = pl.BlockSpec((1, hp, SSM_STATE), lambda s: (jnp.maximum(_seq_of_step(s) - BATCH, 0), 0, 0))
    extra = {} if layer == 0 else dict(input_output_aliases={7: 2, 8: 3})
    return pl.pallas_call(
        functools.partial(_ssd_kernel, layer),
        grid=(N_SSD_STEPS,),
        in_specs=[fwd(SSM_CONV_CH), bwd(SSM_CONV_CH), fwd(LANES), bwd(LANES), init, init,
                  pl.BlockSpec((SUBLANES, LANES), lambda s: (0, 0))] + [pl.BlockSpec(memory_space=pl.ANY)] * len(states),
        out_specs=(fwd(hp), bwd(hp), st, st),
        out_shape=(jax.ShapeDtypeStruct((N_TOK, hp), F32), jax.ShapeDtypeStruct((N_TOK, hp), F32),
                   jax.ShapeDtypeStruct((BATCH, DEPTH, hp, SSM_STATE), F32),
                   jax.ShapeDtypeStruct((BATCH, DEPTH, hp, SSM_STATE), F32)),
        scratch_shapes=[pltpu.VMEM((SSM_STATE, hp), F32), pltpu.VMEM((SSM_STATE, hp), F32)],
        compiler_params=pltpu.CompilerParams(dimension_semantics=("arbitrary",)),
        name="ssd",
        **extra,
    )(xact, xact, small, small, init_f, init_b, par, *states)


def _merge_kernel(ta_ref, attc_ref, attl_ref, yf_ref, yb_ref, xs_ref, z_ref, g_ref, xc_ref, xl_ref, mod_ref,
                  woa_ref, wom_ref, dsk_ref, snw_ref, wos_ref, wo_ref, n2w_ref, rw_ref, rb_ref,
                  x1_ref, h2_ref, ti_ref, tg_ref, cnt_ref):
    mod = mod_ref[0]
    gate1 = mod[:, 2 * D_MODEL:3 * D_MODEL]
    shift2 = mod[:, 3 * D_MODEL:4 * D_MODEL]
    scale2 = mod[:, 4 * D_MODEL:5 * D_MODEL]
    y_a = _dot(ta_ref[...], woa_ref[...])
    att = jnp.where(pl.program_id(0) < CTX_TILES, attc_ref[...].astype(F32), attl_ref[...].astype(F32))
    y_b = _dot(att.astype(BF16), wom_ref[...])
    y_ssm = (yf_ref[...] + yb_ref[...] + dsk_ref[...] * xs_ref[...]) * _silu(z_ref[...])
    y_c = _dot(_rms(y_ssm, snw_ref[...]).astype(BF16), wos_ref[...])
    merged = (g_ref[:, 0:D_MODEL] * y_a + g_ref[:, D_MODEL:2 * D_MODEL] * y_b
              + g_ref[:, 2 * D_MODEL:3 * D_MODEL] * y_c)
    x = jnp.where(pl.program_id(0) < CTX_TILES, xc_ref[...], xl_ref[...])
    x1 = x + gate1 * _dot(merged.astype(BF16), wo_ref[...])
    x1_ref[...] = x1
    h2 = _rms(x1, n2w_ref[...]) * (1.0 + scale2) + shift2
    h2b = h2.astype(BF16)
    h2_ref[...] = h2b

    logits = (_dot(h2b, rw_ref[...]) + rb_ref[...]).T[0:N_EXPERTS, :]
    expert = lax.broadcasted_iota(I32, (N_EXPERTS, TM), 0)
    ids, vals = [], []
    for k in range(TOP_K):
        m = jnp.max(logits, axis=0, keepdims=True)
        idx = jnp.min(jnp.where(logits == m, expert, N_EXPERTS), axis=0, keepdims=True)
        ids.append(idx)
        vals.append(m)
        logits = jnp.where(expert == idx, -jnp.inf, logits)
    es = [jnp.exp(v - vals[0]) for v in vals]
    denom = functools.reduce(jnp.add, es)
    srow = lax.broadcasted_iota(I32, (LANES, TM), 0)
    ti_t = jnp.zeros((LANES, TM), F32)
    tg_t = jnp.zeros((LANES, TM), F32)
    chosen_t = jnp.zeros((LANES, TM), F32)
    for k in range(TOP_K):
        ti_t = jnp.where(srow == k, ids[k].astype(F32), ti_t)
        tg_t = jnp.where(srow == k, es[k] / denom, tg_t)
        chosen_t = jnp.where(srow == ids[k], 1.0, chosen_t)
    ti_ref[...] = ti_t.T.astype(I32)
    tg_ref[...] = tg_t.T
    cnt_ref[0] = _dot_nt(jnp.ones((SUBLANES, TM), BF16), chosen_t.astype(BF16)).astype(I32)


def _merge(ta, att_ctx, att_lat, yf, yb, xact, z, g, xc, xl, mod3, lw):
    row = lambda n: pl.BlockSpec((TM, n), lambda i: (i, 0))
    vw = MLA_HEADS * V_HEAD
    out_shape = (jax.ShapeDtypeStruct((N_TOK, D_MODEL), F32), jax.ShapeDtypeStruct((N_TOK, D_MODEL), BF16),
                 jax.ShapeDtypeStruct((N_TOK, LANES), I32), jax.ShapeDtypeStruct((N_TOK, LANES), F32))
    cnt_shape = jax.ShapeDtypeStruct((N_TILES, SUBLANES, LANES), I32)
    cnt_spec = pl.BlockSpec((1, SUBLANES, LANES), lambda i: (i, 0, 0))
    return pl.pallas_call(
        _merge_kernel,
        grid=(N_TILES,),
        in_specs=[
            row(CONV_DIM),
            pl.BlockSpec((TM, vw), lambda i: (jnp.minimum(i, CTX_TILES - 1), 0)),
            pl.BlockSpec((TM, vw), lambda i: (jnp.maximum(i - CTX_TILES, 0), 0)),
            row(SSM_INNER), row(SSM_INNER), row(SSM_INNER),
            row(SSM_INNER), row(3 * D_MODEL),
            pl.BlockSpec((TM, D_MODEL), lambda i: (jnp.minimum(i, CTX_TILES - 1), 0)),
            pl.BlockSpec((TM, D_MODEL), lambda i: (jnp.maximum(i - CTX_TILES, 0), 0)),
            pl.BlockSpec((1, 1, 6 * D_MODEL), lambda i: (_mod_row(i), 0, 0)),
            _resident((CONV_DIM, D_MODEL)), _resident((MLA_HEADS * V_HEAD, D_MODEL)),
            _resident((1, SSM_INNER)), _resident((1, SSM_INNER)), _resident((SSM_INNER, D_MODEL)),
            _resident((D_MODEL, D_MODEL)), _resident((1, D_MODEL)),
            _resident((D_MODEL, LANES)), _resident((1, LANES)),
        ],
        out_specs=tuple(row(s.shape[1]) for s in out_shape) + (cnt_spec,),
        out_shape=out_shape + (cnt_shape,),
        compiler_params=pltpu.CompilerParams(dimension_semantics=("arbitrary",), vmem_limit_bytes=VMEM_LIMIT),
        name="merge",
    )(ta, att_ctx, att_lat, yf, yb, xact, z, g, xc, xl, mod3, lw["w_out_a"], lw["w_o_mla"], lw["d_skip"],
      lw["ssm_norm_w"],
      lw["w_o_ssm"], lw["w_o"], lw["norm2_w"], lw["router_w"], lw["router_b"])


def _run_copies(cnt_ref, src_ref, dst_ref, first, count, bits, make_copy, start):
    def body(e, carry):
        n = cnt_ref[first + e]
        s0 = src_ref[first + e] if src_ref is not None else 0
        d0 = dst_ref[first + e]

        def pieces(some_bits):
            for b in some_bits:
                above = (n >> (b + 1)) << (b + 1)

                @pl.when(((n >> b) & 1) == 1)
                def _():
                    cp = make_copy(pl.multiple_of(s0 + above, RUN_ALIGN), pl.multiple_of(d0 + above, RUN_ALIGN),
                                   1 << b)
                    if start:
                        cp.start(priority=b % 2)
                    else:
                        cp.wait()

        large = [b for b in bits if b >= COMMON_BITS]

        @pl.when(n >= (1 << COMMON_BITS))
        def _():
            pieces(large)
        pieces([b for b in bits if b < COMMON_BITS])
        return carry
    lax.fori_loop(0, count, body, 0)


def _wait_rows(total, make_copy):
    for b in WAIT_BITS:
        @pl.when(((total >> b) & 1) == 1)
        def _():
            make_copy(0, 0, 1 << b).wait()


def _dispatch_kernel(cnt_ref, off_ref, run_ref, tcnt_ref, tdst_ref, nu_ref, h2_ref, ti_ref, tg_ref, offv_ref,
                     xs_ref, lp_ref, stage, zeros, perm_s, ghi_s, glo_s, sem, semz):
    i = pl.program_id(0)

    def copy_out(s_):
        return lambda s, d, n: pltpu.make_async_copy(stage.at[s_, pl.ds(s, n), :], xs_ref.at[pl.ds(d, n), :],
                                                     sem.at[s_])

    def tile_rows(t):
        last = t * N_EXPERTS + N_EXPERTS - 1
        return off_ref[last] + cnt_ref[last]

    def zero_fill(start):
        zero_out = lambda s, d, n: pltpu.make_async_copy(zeros.at[pl.ds(0, n), :], xs_ref.at[pl.ds(d, n), :], semz)
        _run_copies(tcnt_ref, None, tdst_ref, 0, N_EXPERTS, TAIL_BITS, zero_out, start)
        zrows = zeros.shape[0]

        def body(b, carry):
            for part in range(MOE_BLK // zrows):
                cp = zero_out(0, pl.multiple_of(b * MOE_BLK + part * zrows, RUN_ALIGN), zrows)
                if start:
                    cp.start()
                else:
                    cp.wait()
            return carry
        lax.fori_loop(nu_ref[0], N_BLOCKS, body, 0)

    @pl.when(i == 0)
    def _():
        zeros[...] = jnp.zeros_like(zeros)
        zero_fill(True)

    slots = [(i % 2) * DISPATCH_TILES + u for u in range(DISPATCH_TILES)]

    @pl.when(i >= 2)
    def _():
        for u in range(DISPATCH_TILES):
            _wait_rows(tile_rows((i - 2) * DISPATCH_TILES + u), copy_out(slots[u]))

    for u in range(DISPATCH_TILES):
        tile = i * DISPATCH_TILES + u
        toks = slice(u * TM, (u + 1) * TM)
        lane = lax.broadcasted_iota(I32, (TM, LANES), 1)
        picks = [jnp.where(lane == ti_ref[toks, k:k + 1], 1.0, 0.0) for k in range(TOP_K)]
        ri = lax.broadcasted_iota(I32, (TM, TM), 0)
        ci = lax.broadcasted_iota(I32, (TM, TM), 1)
        earlier = jnp.where(ci < ri, 1.0, 0.0).astype(BF16)
        base = _dot(earlier, functools.reduce(jnp.add, picks).astype(BF16)) + offv_ref[u][0:1, :]
        lp = jnp.full((TM, LANES), -1.0, F32)
        for k in range(TOP_K):
            lp = jnp.where(lane == k, jnp.sum(picks[k] * base, axis=-1, keepdims=True), lp)
        lp_ref[toks, :] = lp.astype(I32)

        lp_t = lp.T
        tg_t = tg_ref[toks, :].T
        for c in range(STAGE_ROWS // LANES):
            rows = slice(c * LANES, (c + 1) * LANES)
            row = (lax.broadcasted_iota(I32, (LANES, TM), 0) + c * LANES).astype(F32)
            perm = jnp.zeros((LANES, TM), F32)
            gates = jnp.zeros((LANES, TM), F32)
            for k in range(TOP_K):
                hit = row == lp_t[k:k + 1, :]
                perm = jnp.where(hit, 1.0, perm)
                gates = jnp.where(hit, tg_t[k:k + 1, :], gates)
            perm_s[u, rows, :] = perm.astype(BF16)
            g_hi = gates.astype(BF16)
            ghi_s[u, rows, :] = g_hi
            glo_s[u, rows, :] = (gates - g_hi.astype(F32)).astype(BF16)
        stage[slots[u], :, 0:D_MODEL] = _dot(perm_s[u], h2_ref[toks, :])
        ones = jnp.ones((TM, LANES), BF16)
        stage[slots[u], :, D_MODEL:XS_COLS] = _dot(ghi_s[u], ones) + _dot(glo_s[u], ones)
        _run_copies(cnt_ref, off_ref, run_ref, tile * N_EXPERTS, N_EXPERTS, RUN_BITS, copy_out(slots[u]), True)

    @pl.when(i == N_TILES // DISPATCH_TILES - 1)
    def _():
        for u in range(DISPATCH_TILES):
            _wait_rows(tile_rows((i - 1) * DISPATCH_TILES + u), copy_out((1 - i % 2) * DISPATCH_TILES + u))
            _wait_rows(tile_rows(i * DISPATCH_TILES + u), copy_out(slots[u]))
        zero_fill(False)


def _dispatch(rt, h2, top_i, top_g):
    step_rows = DISPATCH_TILES * TM
    row = lambda n: pl.BlockSpec((step_rows, n), lambda i, *_: (i, 0))
    scratch = lambda: pltpu.VMEM((DISPATCH_TILES, STAGE_ROWS, TM), BF16)
    return pl.pallas_call(
        _dispatch_kernel,
        grid_spec=pltpu.PrefetchScalarGridSpec(
            num_scalar_prefetch=6,
            grid=(N_TILES // DISPATCH_TILES,),
            in_specs=[row(D_MODEL), row(LANES), row(LANES),
                      pl.BlockSpec((DISPATCH_TILES, SUBLANES, LANES), lambda i, *_: (i, 0, 0))],
            out_specs=(pl.BlockSpec(memory_space=pl.ANY), row(LANES)),
            scratch_shapes=[
                pltpu.VMEM((2 * DISPATCH_TILES, STAGE_ROWS, XS_COLS), F32),
                pltpu.VMEM((1 << TAIL_BITS[0], XS_COLS), F32),
                scratch(), scratch(), scratch(),
                pltpu.SemaphoreType.DMA((2 * DISPATCH_TILES,)),
                pltpu.SemaphoreType.DMA,
            ],
        ),
        out_shape=(jax.ShapeDtypeStruct((N_SLOTS, XS_COLS), F32), jax.ShapeDtypeStruct((N_TOK, LANES), I32)),
        compiler_params=pltpu.CompilerParams(dimension_semantics=("arbitrary",), vmem_limit_bytes=VMEM_LIMIT),
        name="dispatch",
    )(rt["cnt"], rt["off"], rt["run"], rt["tail_cnt"], rt["tail_dst"], rt["n_used"], h2, top_i, top_g, rt["off_v"])


def _moe_kernel(layer, be_ref, nxt_ref, nv_ref, nu_ref, x_ref, wgu_hbm, wdn_hbm, bg_ref, bu_ref, bd_ref, sel_ref,
                y_ref, wgu_ref, wdn_ref, wgu_s, wdn_s, sem):
    i = pl.program_id(0)
    n_used = nu_ref[0]

    def fetch(e):
        return (pltpu.make_async_copy(wgu_hbm.at[layer, e], wgu_ref, sem.at[0]),
                pltpu.make_async_copy(wdn_hbm.at[layer, e], wdn_ref, sem.at[1]))

    @pl.when(i < n_used)
    def _():
        @pl.when(jnp.logical_or(i == 0, be_ref[i] != be_ref[jnp.maximum(i - 1, 0)]))
        def _():
            @pl.when(i == 0)
            def _():
                for cp in fetch(be_ref[0]):
                    cp.start()
            for cp in fetch(be_ref[i]):
                cp.wait()
            half = LANES
            for c in range(2 * EXPERT_FF // (2 * half)):
                r = _dot(wgu_ref[:, c * 2 * half:(c + 1) * 2 * half].astype(BF16), sel_ref[...])
                wgu_s[:, c * half:(c + 1) * half] = r[:, 0:half].astype(BF16)
                wgu_s[:, EXPERT_FF + c * half:EXPERT_FF + (c + 1) * half] = r[:, half:2 * half].astype(BF16)
            wdn_s[...] = wdn_ref[...].astype(BF16)
            nxt = nxt_ref[i]

            @pl.when(nxt >= 0)
            def _():
                for cp in fetch(nxt):
                    cp.start(priority=1)

        def expert_rows(rows):
            gu = _dot(x_ref[0:rows, 0:D_MODEL].astype(BF16), wgu_s[...])
            gate = jnp.minimum(gu[:, 0:EXPERT_FF] + bg_ref[0], SWIGLU_LIMIT)
            up = jnp.clip(gu[:, EXPERT_FF:2 * EXPERT_FF] + bu_ref[0], -SWIGLU_LIMIT, SWIGLU_LIMIT)
            act = gate * jax.nn.sigmoid(SWIGLU_ALPHA * gate) * (up + 1.0)
            y = _dot(act.astype(BF16), wdn_s[...]) + bd_ref[0]
            slot_gate = x_ref[0:rows, D_MODEL:XS_COLS]
            for j in range(D_MODEL // LANES):
                y_ref[0:rows, j * LANES:(j + 1) * LANES] = y[:, j * LANES:(j + 1) * LANES] * slot_gate

        parts = (nv_ref[i] + MOE_PART - 1) // MOE_PART
        for n_parts in range(1, MOE_BLK // MOE_PART + 1):
            @pl.when(parts == n_parts)
            def _():
                rows = n_parts * MOE_PART
                expert_rows(rows)
                if rows < MOE_BLK:
                    y_ref[rows:MOE_BLK, :] = jnp.zeros((MOE_BLK - rows, D_MODEL), F32)

    @pl.when(i >= n_used)
    def _():
        y_ref[...] = jnp.zeros_like(y_ref)


def _bias_split_kernel(b_ref, sel_ref, o_ref):
    for c in range(2 * EXPERT_FF // (2 * LANES)):
        terms = _split3(b_ref[:, c * 2 * LANES:(c + 1) * 2 * LANES])
        r = functools.reduce(jnp.add, [_dot(t, sel_ref[...]) for t in terms])
        o_ref[:, c * LANES:(c + 1) * LANES] = r[:, 0:LANES]
        o_ref[:, EXPERT_FF + c * LANES:EXPERT_FF + (c + 1) * LANES] = r[:, LANES:2 * LANES]


def _bias_split(b_gu, sel):
    n = DEPTH * N_EXPERTS
    out = pl.pallas_call(
        _bias_split_kernel,
        out_shape=jax.ShapeDtypeStruct((n, 2 * EXPERT_FF), F32),
        name="bias_split",
    )(b_gu.reshape(n, 2 * EXPERT_FF), sel)
    return out.reshape(n, 1, 2 * EXPERT_FF)


def _moe(layer, rt, xs, w_gu, w_down, b_gu_split, b_down, sel):
    first = layer * N_EXPERTS
    bg = pl.BlockSpec((1, 1, EXPERT_FF), lambda i, be, *_: (first + be[i], 0, 0))
    bu = pl.BlockSpec((1, 1, EXPERT_FF), lambda i, be, *_: (first + be[i], 0, 1))
    bd = pl.BlockSpec((1, 1, D_MODEL), lambda i, be, *_: (first + be[i], 0, 0))
    return pl.pallas_call(
        functools.partial(_moe_kernel, layer),
        grid_spec=pltpu.PrefetchScalarGridSpec(
            num_scalar_prefetch=4,
            grid=(N_BLOCKS,),
            in_specs=[
                pl.BlockSpec((MOE_BLK, XS_COLS), lambda i, be, nx, nv, nu: (jnp.minimum(i, nu[0] - 1), 0)),
                pl.BlockSpec(memory_space=pl.ANY),
                pl.BlockSpec(memory_space=pl.ANY),
                bg, bu, bd,
                pl.BlockSpec((2 * LANES, 2 * LANES), lambda i, *_: (0, 0)),
            ],
            out_specs=pl.BlockSpec((MOE_BLK, D_MODEL), lambda i, *_: (i, 0)),
            scratch_shapes=[
                pltpu.VMEM((D_MODEL, 2 * EXPERT_FF), F32),
                pltpu.VMEM((EXPERT_FF, D_MODEL), F32),
                pltpu.VMEM((D_MODEL, 2 * EXPERT_FF), BF16),
                pltpu.VMEM((EXPERT_FF, D_MODEL), BF16),
                pltpu.SemaphoreType.DMA((2,)),
            ],
        ),
        out_shape=jax.ShapeDtypeStruct((N_SLOTS, D_MODEL), F32),
        compiler_params=pltpu.CompilerParams(dimension_semantics=("arbitrary",), vmem_limit_bytes=VMEM_LIMIT),
        name="moe",
    )(rt["blk_e"], rt["blk_next"], rt["blk_rows"], rt["n_used"], xs, w_gu, w_down, b_gu_split, b_gu_split,
      b_down.reshape(DEPTH * N_EXPERTS, 1, D_MODEL), sel)


def _combine_kernel(cnt_ref, off_ref, run_ref, y_ref, lp_ref, x1_ref, mod_ref, fw_ref,
                    xc_ref, xl_ref, yc_ref, yl_ref, stage, mine_s, sem):
    i = pl.program_id(0)
    slot = i % 2

    def copy_in(s_):
        return lambda s, d, n: pltpu.make_async_copy(y_ref.at[pl.ds(d, n), :], stage.at[s_, pl.ds(s, n), :],
                                                     sem.at[s_])

    @pl.when(i == 0)
    def _():
        stage[...] = jnp.zeros_like(stage)
        _run_copies(cnt_ref, off_ref, run_ref, 0, N_EXPERTS, RUN_BITS, copy_in(0), True)

    @pl.when(i + 1 < N_TILES)
    def _():
        _run_copies(cnt_ref, off_ref, run_ref, (i + 1) * N_EXPERTS, N_EXPERTS, RUN_BITS, copy_in(1 - slot), True)

    last = i * N_EXPERTS + N_EXPERTS - 1
    _wait_rows(off_ref[last] + cnt_ref[last], copy_in(slot))
    lane = lax.broadcasted_iota(I32, (TM, LANES), 1)
    mine = [jnp.broadcast_to(lp_ref[:, k:k + 1], (TM, LANES)) for k in range(TOP_K)]
    for c in range(STAGE_ROWS // LANES):
        hit = jnp.zeros((TM, LANES), F32)
        for k in range(TOP_K):
            hit = jnp.where(lane + c * LANES == mine[k], 1.0, hit)
        mine_s[:, c * LANES:(c + 1) * LANES] = hit.astype(BF16)
    moe = _dot(mine_s[...], stage[slot].astype(BF16))
    gate2 = mod_ref[0][:, 5 * D_MODEL:6 * D_MODEL]
    x2 = x1_ref[...] + gate2 * moe
    y_norm = _rms(x2, fw_ref[...])

    @pl.when(i < CTX_TILES)
    def _():
        xc_ref[...] = x2
        yc_ref[...] = y_norm

    @pl.when(i >= CTX_TILES)
    def _():
        xl_ref[...] = x2
        yl_ref[...] = y_norm


def _combine(rt, y_slots, lp, x1, mod3, final_w):
    row = lambda n: pl.BlockSpec((TM, n), lambda i, *_: (i, 0))
    ctx_row = pl.BlockSpec((TM, D_MODEL), lambda i, *_: (jnp.minimum(i, CTX_TILES - 1), 0))
    lat_row = pl.BlockSpec((TM, D_MODEL), lambda i, *_: (jnp.maximum(i - CTX_TILES, 0), 0))
    return pl.pallas_call(
        _combine_kernel,
        grid_spec=pltpu.PrefetchScalarGridSpec(
            num_scalar_prefetch=3,
            grid=(N_TILES,),
            in_specs=[
                pl.BlockSpec(memory_space=pl.ANY),
                row(LANES), row(D_MODEL),
                pl.BlockSpec((1, 1, 6 * D_MODEL), lambda i, *_: (_mod_row(i), 0, 0)),
                pl.BlockSpec((1, D_MODEL), lambda i, *_: (0, 0)),
            ],
            out_specs=(ctx_row, lat_row, ctx_row, lat_row),
            scratch_shapes=[pltpu.VMEM((2, STAGE_ROWS, D_MODEL), F32), pltpu.VMEM((TM, STAGE_ROWS), BF16),
                            pltpu.SemaphoreType.DMA((2,))],
        ),
        out_shape=(jax.ShapeDtypeStruct((N_CTX, D_MODEL), F32), jax.ShapeDtypeStruct((N_LAT, D_MODEL), F32),
                   jax.ShapeDtypeStruct((N_CTX, D_MODEL), F32), jax.ShapeDtypeStruct((N_LAT, D_MODEL), F32)),
        compiler_params=pltpu.CompilerParams(dimension_semantics=("arbitrary",), vmem_limit_bytes=VMEM_LIMIT),
        name="combine",
    )(rt["cnt"], rt["off"], rt["run"], y_slots, lp, x1, mod3, final_w)


def _rope_tables():
    rows = DEC_SEQ // GRID_W
    t = jnp.arange(rows * GRID_W)
    row = (t // GRID_W).astype(F32)
    col = (t % GRID_W).astype(F32)
    half = QK_ROPE // 2
    inv = ROPE_BASE ** (-jnp.arange(0, half, 2, dtype=F32) / half)
    ang_r, ang_c = row[:, None] * inv, col[:, None] * inv
    cos32 = jnp.concatenate([jnp.cos(ang_r), jnp.cos(ang_r), jnp.cos(ang_c), jnp.cos(ang_c)], axis=-1)
    sin32 = jnp.concatenate([-jnp.sin(ang_r), jnp.sin(ang_r), -jnp.sin(ang_c), jnp.sin(ang_c)], axis=-1)
    cos32 = jnp.concatenate([jnp.ones((TM, QK_ROPE), F32), cos32], axis=0)
    sin32 = jnp.concatenate([jnp.zeros((TM, QK_ROPE), F32), sin32], axis=0)
    n = cos32.shape[0]
    pad = HEAD_PAD - QK_NOPE - QK_ROPE
    cos_h = jnp.concatenate([jnp.ones((n, QK_NOPE), F32), cos32, jnp.zeros((n, pad), F32)], axis=-1)
    sin_h = jnp.concatenate([jnp.zeros((n, QK_NOPE), F32), sin32, jnp.zeros((n, pad), F32)], axis=-1)
    zeros = jnp.zeros((n, LANES - QK_ROPE), F32)
    return {
        "cosq": cos_h, "sinq": sin_h,
        "cosk": jnp.concatenate([cos32, zeros], axis=-1), "sink": jnp.concatenate([sin32, zeros], axis=-1),
        "vone": jnp.zeros((MLA_HEADS, HEAD_PAD), F32).at[:, V_HEAD].set(1.0).reshape(1, MLA_HEADS * HEAD_PAD),
    }


def _layer_weights(p, l):
    hd = QK_NOPE + QK_ROPE
    pad = HEAD_PAD - hd
    wq = p["w_uq"][l].reshape(Q_RANK, MLA_HEADS, hd)
    wq_a = jnp.pad(wq, ((0, 0), (0, 0), (0, pad)))
    wuk = jnp.pad(p["w_uk"][l], ((0, 0), (0, 0), (0, HEAD_PAD - QK_NOPE)))
    rw = jnp.pad(p["router_w"][l], ((0, 0), (0, LANES - N_EXPERTS)))
    rb = jnp.concatenate([p["router_b"][l], jnp.full((LANES - N_EXPERTS,), -jnp.inf, F32)])
    return {
        "norm1_w": p["norm1_w"][l][None], "q_norm_w": p["q_norm_w"][l][None],
        "wq_a": wq_a.reshape(Q_RANK, -1).astype(BF16),
        "kv_norm_w": p["kv_norm_w"][l][None],
        "wuk": wuk.reshape(KV_RANK, -1).astype(BF16),
        "wuv": jnp.pad(p["w_uv"][l], ((0, 0), (0, 0), (0, HEAD_PAD - V_HEAD))).reshape(KV_RANK, -1).astype(BF16),
        "wuv_c": p["w_uv"][l].reshape(KV_RANK, -1).astype(BF16),
        "conv_a_w": p["conv_a_w"][l], "ssm_conv_w": p["ssm_conv_w"][l], "ssm_conv_b": p["ssm_conv_b"][l][None],
        "w_out_a": p["w_out_a"][l].astype(BF16), "w_o_mla": p["w_o_mla"][l].astype(BF16),
        "d_skip": jnp.repeat(p["d_skip"][l], SSM_HEAD_DIM)[None], "ssm_norm_w": p["ssm_norm_w"][l][None],
        "w_o_ssm": p["w_o_ssm"][l].astype(BF16), "w_o": p["w_o"][l].astype(BF16),
        "norm2_w": p["norm2_w"][l][None], "router_w": rw.astype(BF16), "router_b": rb[None],
    }


def _ssd_params(p, l):
    z = lambda n: jnp.zeros((n,), F32)
    lanes = lambda f, b: jnp.concatenate([z(SM_DTF), f, b, z(LANES - SM_DTB - SSM_HEADS)])
    ones = jnp.ones((SSM_HEADS,), F32)
    rows = [lanes(p["dt_bias_fwd"][l], p["dt_bias_bwd"][l]), lanes(p["a_log_fwd"][l], p["a_log_bwd"][l]),
            lanes(ones, ones)]
    return jnp.concatenate([jnp.stack(rows), jnp.zeros((SUBLANES - 3, LANES), F32)], axis=0)


def _routing(cnt_tiles):
    cnt = cnt_tiles[:, 0, 0:N_EXPERTS]
    cnt = (cnt + RUN_ALIGN - 1) // RUN_ALIGN * RUN_ALIGN
    per_expert = jnp.sum(cnt, axis=0)
    padded = (per_expert + MOE_BLK - 1) // MOE_BLK * MOE_BLK
    pad_end = jnp.cumsum(padded)
    pad_start = pad_end - padded
    run = pad_start[None, :] + jnp.cumsum(cnt, axis=0) - cnt
    off = jnp.cumsum(cnt, axis=1) - cnt
    starts = jnp.arange(N_BLOCKS, dtype=I32) * MOE_BLK
    blk_e = jnp.minimum(jnp.sum((pad_end[None, :] <= starts[:, None]).astype(I32), axis=1), N_EXPERTS - 1)
    off_v = jnp.zeros((N_TILES, SUBLANES, LANES), F32).at[:, 0, 0:N_EXPERTS].set(off.astype(F32))
    ids = jnp.arange(N_EXPERTS, dtype=I32)
    later = jnp.logical_and(ids[None, :] > ids[:, None], padded[None, :] > 0)
    nxt = jnp.min(jnp.where(later, ids[None, :], N_EXPERTS), axis=1)
    nxt = jnp.where(nxt == N_EXPERTS, -1, nxt)
    per_block = lambda v: jnp.sum(jnp.where(blk_e[:, None] == ids[None, :], v[None, :], 0), axis=1)
    blk_rows = jnp.clip(per_block(pad_start + per_expert) - starts, 0, MOE_BLK)
    return {
        "blk_next": per_block(nxt).astype(I32), "blk_rows": blk_rows.astype(I32),
        "cnt": cnt.reshape(-1).astype(I32), "off": off.reshape(-1).astype(I32), "run": run.reshape(-1).astype(I32),
        "tail_cnt": (padded - per_expert).astype(I32), "tail_dst": (pad_start + per_expert).astype(I32),
        "blk_e": blk_e.astype(I32), "n_used": (pad_end[-1] // MOE_BLK).astype(I32).reshape(1), "off_v": off_v,
    }


def _deinterleave_matrix():
    k = jnp.arange(2 * LANES)[:, None]
    n = jnp.arange(2 * LANES)[None, :]
    src = jnp.where(n < LANES, 2 * n, 2 * (n - LANES) + 1)
    return (k == src).astype(BF16)


def kernel(x_prompt, x_sample, cache_ckv, cache_kpe, state_ssm_fwd, state_ssm_bwd, c, c_ctx, w_ada, b_ada, norm1_w, w_in, conv_a_w, w_out_a, q_norm_w, w_uq, kv_norm_w, w_uk, w_uv, w_o_mla, ssm_conv_w, ssm_conv_b, dt_bias_fwd, dt_bias_bwd, a_log_fwd, a_log_bwd, d_skip, ssm_norm_w, w_o_ssm, w_o, norm2_w, router_w, router_b, w_gu, b_gu, w_down, b_down, final_norm_w):
    p = dict(norm1_w=norm1_w, w_in=w_in, conv_a_w=conv_a_w, w_out_a=w_out_a, q_norm_w=q_norm_w, w_uq=w_uq,
             kv_norm_w=kv_norm_w, w_uk=w_uk, w_uv=w_uv, w_o_mla=w_o_mla, ssm_conv_w=ssm_conv_w,
             ssm_conv_b=ssm_conv_b, dt_bias_fwd=dt_bias_fwd, dt_bias_bwd=dt_bias_bwd, a_log_fwd=a_log_fwd,
             a_log_bwd=a_log_bwd, d_skip=d_skip, ssm_norm_w=ssm_norm_w, w_o_ssm=w_o_ssm, w_o=w_o,
             norm2_w=norm2_w, router_w=router_w, router_b=router_b, b_gu=b_gu, b_down=b_down)
    xc = x_prompt.reshape(N_CTX, D_MODEL)
    xl = x_sample.reshape(N_LAT, D_MODEL)
    cond8 = jnp.concatenate([c_ctx[None], c, jnp.zeros((SUBLANES - 1 - DEC_BATCH, D_MODEL), F32)], axis=0)
    mods = _ada_mods(cond8, w_ada, b_ada)
    tabs = _rope_tables()
    sel = _deinterleave_matrix()
    b_gu_split = _bias_split(b_gu, sel)
    w_in_t = jnp.swapaxes(w_in, 1, 2)
    final_w = final_norm_w[None]
    hp = SSM_INNER

    caches, states = (), ()
    y_ctx = y_lat = None
    for l in range(DEPTH):
        lw = _layer_weights(p, l)
        mod3 = mods[l].reshape(SUBLANES, 1, 6 * D_MODEL)
        ta, q, ckv, kpe, kf, v_ctx, v_lat, small, z, xact, g = _inproj(l, xc, xl, mod3, w_in_t, lw, tabs, caches)
        caches = (ckv, kpe)
        kpe_c = jnp.pad(cache_kpe[:, l].reshape(DEC_BATCH * PAST_LEN, QK_ROPE), ((0, 0), (0, LANES - QK_ROPE)))
        kf_c, v_c = _kvcache(cache_ckv[:, l].reshape(DEC_BATCH * PAST_LEN, KV_RANK), kpe_c, lw, tabs)
        att_ctx, att_lat = _attention(q, kf, v_ctx, v_lat, kf_c, v_c)
        init_f = state_ssm_fwd[:, l].reshape(DEC_BATCH, hp, SSM_STATE)
        init_b = state_ssm_bwd[:, l].reshape(DEC_BATCH, hp, SSM_STATE)
        yf, yb, sf, sb = _ssd(l, xact, small, init_f, init_b, _ssd_params(p, l), states)
        states = (sf, sb)
        x1, h2, top_i, top_g, cnt_tiles = _merge(ta, att_ctx, att_lat, yf, yb, xact, z, g, xc, xl, mod3, lw)
        rt = _routing(cnt_tiles)
        xs, lp = _dispatch(rt, h2, top_i, top_g)
        y_slots = _moe(l, rt, xs, w_gu, w_down, b_gu_split, b_down, sel)
        xc, xl, y_ctx, y_lat = _combine(rt, y_slots, lp, x1, mod3, final_w)

    y_prompt = y_ctx.reshape(BATCH, SEQ, D_MODEL)
    y_sample = y_lat.reshape(DEC_BATCH, DEC_SEQ, D_MODEL)
    state_shape = (BATCH, DEPTH, SSM_HEADS, SSM_HEAD_DIM, SSM_STATE)
    return (y_prompt, y_sample, caches[0], caches[1], states[0].reshape(state_shape), states[1].reshape(state_shape))
```

```python
import functools
import math

import jax
import jax.numpy as jnp
from jax import lax
from jax.experimental import pallas as pl
from jax.experimental.pallas import tpu as pltpu

F32 = jnp.float32
BF16 = jnp.bfloat16
I32 = jnp.int32

D_MODEL = 1024
BATCH = 16
SEQ = 256
DEPTH = 2
DEC_BATCH = 2
DEC_SEQ = 2048
PAST_LEN = 512
GRID_W = 64
NORM_EPS = 1e-6
CONV_DIM = 512
MLA_HEADS = 8
Q_RANK = 384
KV_RANK = 256
QK_NOPE = 64
QK_ROPE = 32
V_HEAD = 64
ROPE_BASE = 10000.0
MLA_SCALE = (QK_NOPE + QK_ROPE) ** -0.5
SSM_HEADS = 16
SSM_HEAD_DIM = 64
SSM_INNER = SSM_HEADS * SSM_HEAD_DIM
SSM_GROUPS = 2
SSM_STATE = 128
SSM_CONV_CH = SSM_INNER + 2 * SSM_GROUPS * SSM_STATE
N_EXPERTS = 32
TOP_K = 4
EXPERT_FF = D_MODEL
SWIGLU_ALPHA = 1.702
SWIGLU_LIMIT = 7.0

N_CTX = BATCH * SEQ
N_LAT = DEC_BATCH * DEC_SEQ
N_TOK = N_CTX + N_LAT
N_SEQS = BATCH + DEC_BATCH

LANES = 128
SUBLANES = 8
HEAD_PAD = 128
TM = 256
N_TILES = N_TOK // TM
CTX_TILES = N_CTX // TM
LAT_TILES_PER_SEQ = DEC_SEQ // TM
TQ_LAT = 512
CHUNK = 128
SSD_STEP = 2 * CHUNK
CTX_STEPS_PER_SEQ = SEQ // SSD_STEP
LAT_STEPS_PER_SEQ = DEC_SEQ // SSD_STEP
N_CTX_STEPS = N_CTX // SSD_STEP
N_SSD_STEPS = N_TOK // SSD_STEP
MOE_BLK = 512
MOE_PART = 128
DISPATCH_TILES = 2
N_ASSIGN = N_TOK * TOP_K
RUN_ALIGN = SUBLANES
RUN_BITS = tuple(range(8, 2, -1))
TAIL_BITS = tuple(range(8, 2, -1))
COMMON_BITS = 6
WAIT_BITS = tuple(range(10, 2, -1))
STAGE_ROWS = 1280
XS_COLS = D_MODEL + LANES
N_BLOCKS = -(-(N_ASSIGN + N_TILES * N_EXPERTS * (RUN_ALIGN - 1) + N_EXPERTS * (MOE_BLK - 1)) // MOE_BLK)
N_SLOTS = N_BLOCKS * MOE_BLK
VMEM_LIMIT = 56 * 1024 * 1024

C_A3 = 0
C_CQ = C_A3 + 3 * CONV_DIM
C_CKV = C_CQ + Q_RANK
C_Z = C_CKV + KV_RANK
C_XBC = C_Z + SSM_INNER
C_GATE = C_XBC + SSM_CONV_CH
C_SMALL = C_GATE + 3 * D_MODEL
IN_COLS2 = C_SMALL + LANES
SM_DTF = QK_ROPE
SM_DTB = QK_ROPE + SSM_HEADS
S_CQ = 3 * CONV_DIM
S_CKV = S_CQ + Q_RANK
S_KPE = S_CKV + KV_RANK
S_Z = S_KPE + QK_ROPE
S_XBC = S_Z + SSM_INNER
S_DTF = S_XBC + SSM_CONV_CH
S_GATE = S_DTF + 2 * SSM_HEADS
IN_COLS = S_GATE + 3 * D_MODEL
W_SEGMENTS = ((C_A3, 0, 3 * CONV_DIM), (C_CQ, S_CQ, Q_RANK), (C_CKV, S_CKV, KV_RANK), (C_Z, S_Z, SSM_INNER),
              (C_XBC, S_XBC, SSM_CONV_CH), (C_GATE, S_GATE, 3 * D_MODEL))
W_PIECE = 512


def _rms(x, w):
    return x * lax.rsqrt(jnp.mean(x * x, axis=-1, keepdims=True) + NORM_EPS) * w


def _silu(x):
    return x * jax.nn.sigmoid(x)


def _dot(a, b):
    return jnp.dot(a, b, preferred_element_type=F32)


def _dot_nt(a, b):
    return lax.dot_general(a, b, (((1,), (1,)), ((), ())), preferred_element_type=F32)


def _resident(shape):
    nd = len(shape)
    return pl.BlockSpec(shape, lambda *_: (0,) * nd, pipeline_mode=pl.Buffered(1))


def _mod_row(i):
    return jnp.where(i < CTX_TILES, 0, 1 + (i - CTX_TILES) // LAT_TILES_PER_SEQ)


def _pos_block(i):
    return jnp.where(i < CTX_TILES, 0, 1 + (i - CTX_TILES) % LAT_TILES_PER_SEQ)


def _ada_kernel(c_ref, w_ref, b_ref, o_ref):
    s = _silu(c_ref[...]).astype(BF16)
    o_ref[0] = _dot(s, w_ref[0].astype(BF16)) + b_ref[0]


def _ada_mods(cond8, w_ada, b_ada):
    tn = 1536
    n_mod = 6 * D_MODEL
    return pl.pallas_call(
        _ada_kernel,
        grid=(DEPTH, n_mod // tn),
        in_specs=[
            pl.BlockSpec((SUBLANES, D_MODEL), lambda l, j: (0, 0)),
            pl.BlockSpec((1, D_MODEL, tn), lambda l, j: (l, 0, j)),
            pl.BlockSpec((1, 1, tn), lambda l, j: (l, 0, j)),
        ],
        out_specs=pl.BlockSpec((1, SUBLANES, tn), lambda l, j: (l, 0, j)),
        out_shape=jax.ShapeDtypeStruct((DEPTH, SUBLANES, n_mod), F32),
        compiler_params=pltpu.CompilerParams(dimension_semantics=("arbitrary", "arbitrary")),
        name="ada_mods",
    )(cond8, w_ada, b_ada.reshape(DEPTH, 1, n_mod))


def _relayout_w_in(layer, wt_hbm, w2, stg, small_stg, sem, small_sem):
    pieces = [(dst + p, src + p, min(W_PIECE, width - p))
              for dst, src, width in W_SEGMENTS for p in range(0, width, W_PIECE)]
    fetch = lambda k: pltpu.make_async_copy(wt_hbm.at[layer, pl.ds(pieces[k][1], pieces[k][2]), :],
                                            stg.at[k % 2, pl.ds(0, pieces[k][2]), :], sem.at[k % 2])
    small_stg[...] = jnp.zeros_like(small_stg)
    small_copies = [
        pltpu.make_async_copy(wt_hbm.at[layer, pl.ds(S_KPE, QK_ROPE), :], small_stg.at[pl.ds(0, QK_ROPE), :],
                              small_sem.at[0]),
        pltpu.make_async_copy(wt_hbm.at[layer, pl.ds(S_DTF, 2 * SSM_HEADS), :],
                              small_stg.at[pl.ds(SM_DTF, 2 * SSM_HEADS), :], small_sem.at[1]),
    ]
    for cp in small_copies:
        cp.start()
    fetch(0).start()
    for k, (dst, _, width) in enumerate(pieces):
        if k + 1 < len(pieces):
            fetch(k + 1).start()
        fetch(k).wait()
        w2[:, dst:dst + width] = stg[k % 2, 0:width, :].T.astype(BF16)
    for cp in small_copies:
        cp.wait()
    w2[:, C_SMALL:IN_COLS2] = small_stg[...].T.astype(BF16)


def _place_rope_key(k_nope, kpe):
    shifted = pltpu.roll(kpe, QK_NOPE, 1)
    return jnp.concatenate([k_nope[:, h * HEAD_PAD:(h + 1) * HEAD_PAD] + shifted for h in range(MLA_HEADS)],
                           axis=1).astype(BF16)


def _rot_partner(x):
    n = x.shape[1]
    lane = lax.broadcasted_iota(I32, x.shape, 1)
    quarter = QK_ROPE // 4
    return jnp.where(lane % (2 * quarter) < quarter, pltpu.roll(x, n - quarter, 1), pltpu.roll(x, quarter, 1))


def _first_layer_slabs(ref, value):
    ref[0, 0] = value
    ref[0, 1:DEPTH] = jnp.zeros((DEPTH - 1,) + value.shape, value.dtype)


def _inproj_kernel(layer, *refs):
    n_in = 23 + (2 if layer > 0 else 0)
    (xc_ref, xcp_ref, xcn_ref, xl_ref, xlp_ref, xln_ref, mod_ref, n1w_ref, w_hbm, qnw_ref, wq_ref, kvnw_ref, wuk_ref,
     wuv_ref, wuvc_ref, vone_ref, cosq_ref, sinq_ref, cosk_ref, sink_ref, caw_ref, cw_ref, cb_ref) = refs[:23]
    (ta_ref, q_ref, ckv_ref, kpe_ref, kf_ref, vc_ref, vl_ref, small_ref, z_ref, xact_ref, g_ref,
     w2, stg, small_stg, sem, small_sem) = refs[n_in:]
    i = pl.program_id(0)

    @pl.when(i == 0)
    def _():
        _relayout_w_in(layer, w_hbm, w2, stg, small_stg, sem, small_sem)

    mod = mod_ref[0]
    shift1 = mod[:, 0:D_MODEL]
    scale1 = mod[:, D_MODEL:2 * D_MODEL]
    x_ext = jnp.where(i < CTX_TILES, jnp.concatenate([xcp_ref[...], xc_ref[...], xcn_ref[...]], axis=0),
                      jnp.concatenate([xlp_ref[...], xl_ref[...], xln_ref[...]], axis=0))
    h_ext = _rms(x_ext, n1w_ref[...]) * (1.0 + scale1) + shift1
    hb_ext = h_ext.astype(BF16)
    hb = h_ext[SUBLANES:SUBLANES + TM].astype(BF16)

    j = (i - CTX_TILES) % LAT_TILES_PER_SEQ
    is_ctx = i < CTX_TILES
    keep_prev = jnp.where(jnp.logical_or(is_ctx, j == 0), 0.0, 1.0)
    keep_next = jnp.where(jnp.logical_or(is_ctx, j == LAT_TILES_PER_SEQ - 1), 0.0, 1.0)
    row = lax.broadcasted_iota(I32, (TM, 1), 0)
    prev_mask = jnp.where(row == 0, keep_prev, 1.0)
    next_mask = jnp.where(row == TM - 1, keep_next, 1.0)
    ext = TM + 2 * SUBLANES

    def conv3(x, w_ref):
        x_prev = pltpu.roll(x, 1, 0)[SUBLANES:SUBLANES + TM] * prev_mask
        x_next = pltpu.roll(x, ext - 1, 0)[SUBLANES:SUBLANES + TM] * next_mask
        return x_prev * w_ref[0:1, :] + x[SUBLANES:SUBLANES + TM] * w_ref[1:2, :] + x_next * w_ref[2:3, :]

    a3 = _dot(hb_ext, w2[:, C_A3:C_CQ])
    s = a3[:, 2 * CONV_DIM:3 * CONV_DIM] * a3[:, 0:CONV_DIM]
    ta_ref[...] = (a3[SUBLANES:SUBLANES + TM, CONV_DIM:2 * CONV_DIM] * conv3(s, caw_ref)).astype(BF16)
    xact_ref[...] = _silu(conv3(_dot(hb_ext, w2[:, C_XBC:C_GATE]), cw_ref) + cb_ref[...])

    def seg(a, b):
        return _dot(hb, w2[:, a:b])

    cqn = _rms(seg(C_CQ, C_CKV), qnw_ref[...]).astype(BF16)
    qa = _dot(cqn, wq_ref[...])
    qa_rot = _rot_partner(qa)
    for h in range(MLA_HEADS):
        hs = slice(h * HEAD_PAD, (h + 1) * HEAD_PAD)
        q_ref[:, hs] = (qa[:, hs] * cosq_ref[...] + qa_rot[:, hs] * sinq_ref[...]).astype(BF16)

    ckv = _rms(seg(C_CKV, C_Z), kvnw_ref[...])
    ckvb = ckv.astype(BF16)
    small = seg(C_SMALL, IN_COLS2)
    small_ref[...] = small
    kpe = small * cosk_ref[...] + _rot_partner(small) * sink_ref[...]
    kf_ref[...] = _place_rope_key(_dot(ckvb, wuk_ref[...]), kpe)
    v_lat = (_dot(ckvb, wuv_ref[...]) + vone_ref[...]).astype(BF16)
    v_ctx = _dot(ckvb, wuvc_ref[...]).astype(BF16)

    z_ref[...] = seg(C_Z, C_XBC)
    g_ref[...] = jax.nn.sigmoid(seg(C_GATE, C_SMALL))

    @pl.when(i < CTX_TILES)
    def _():
        if layer == 0:
            _first_layer_slabs(ckv_ref, ckv)
            _first_layer_slabs(kpe_ref, small[:, 0:QK_ROPE])
        else:
            ckv_ref[0] = ckv
            kpe_ref[0] = small[:, 0:QK_ROPE]
        vc_ref[...] = v_ctx

    @pl.when(i >= CTX_TILES)
    def _():
        vl_ref[...] = v_lat


def _inproj(layer, xc, xl, mod3, w_in, lw, tabs, caches):
    row = lambda n: pl.BlockSpec((TM, n), lambda i: (i, 0))
    tab = lambda n: pl.BlockSpec((TM, n), lambda i: (_pos_block(i), 0))
    per = TM // SUBLANES
    qw = MLA_HEADS * HEAD_PAD
    ctx_row = lambda n: pl.BlockSpec((TM, n), lambda i: (jnp.minimum(i, CTX_TILES - 1), 0))
    lat_row = lambda n: pl.BlockSpec((TM, n), lambda i: (jnp.maximum(i - CTX_TILES, 0), 0))

    def halo(first_tile, n_rows, side):
        last = n_rows // SUBLANES - 1
        return pl.BlockSpec((SUBLANES, D_MODEL),
                            lambda i: (jnp.clip((i - first_tile + side) * per - 1 + side, 0, last), 0))

    out_shape = (
        jax.ShapeDtypeStruct((N_TOK, CONV_DIM), BF16),
        jax.ShapeDtypeStruct((N_TOK, qw), BF16),
        jax.ShapeDtypeStruct((BATCH, DEPTH, SEQ, KV_RANK), F32),
        jax.ShapeDtypeStruct((BATCH, DEPTH, SEQ, QK_ROPE), F32),
        jax.ShapeDtypeStruct((N_TOK, qw), BF16),
        jax.ShapeDtypeStruct((N_CTX, MLA_HEADS * V_HEAD), BF16),
        jax.ShapeDtypeStruct((N_LAT, qw), BF16),
        jax.ShapeDtypeStruct((N_TOK, LANES), F32),
        jax.ShapeDtypeStruct((N_TOK, SSM_INNER), F32),
        jax.ShapeDtypeStruct((N_TOK, SSM_CONV_CH), F32),
        jax.ShapeDtypeStruct((N_TOK, 3 * D_MODEL), F32),
    )
    out_specs = [row(s.shape[-1]) for s in out_shape]
    cache_block = (1, DEPTH, SEQ) if layer == 0 else (1, None, SEQ)
    cache_index = lambda i: (jnp.minimum(i, CTX_TILES - 1), 0 if layer == 0 else layer, 0, 0)
    out_specs[2] = pl.BlockSpec(cache_block + (KV_RANK,), cache_index)
    out_specs[3] = pl.BlockSpec(cache_block + (QK_ROPE,), cache_index)
    extra = {} if layer == 0 else dict(input_output_aliases={23: 2, 24: 3})
    out_specs[5] = ctx_row(MLA_HEADS * V_HEAD)
    out_specs[6] = lat_row(qw)
    return pl.pallas_call(
        functools.partial(_inproj_kernel, layer),
        grid=(N_TILES,),
        in_specs=[
            ctx_row(D_MODEL), halo(0, N_CTX, 0), halo(0, N_CTX, 1),
            lat_row(D_MODEL), halo(CTX_TILES, N_LAT, 0), halo(CTX_TILES, N_LAT, 1),
            pl.BlockSpec((1, 1, 6 * D_MODEL), lambda i: (_mod_row(i), 0, 0)),
            _resident((1, D_MODEL)),
            pl.BlockSpec(memory_space=pl.ANY),
            _resident((1, Q_RANK)),
            _resident((Q_RANK, qw)),
            _resident((1, KV_RANK)),
            _resident((KV_RANK, qw)),
            _resident((KV_RANK, qw)),
            _resident((KV_RANK, MLA_HEADS * V_HEAD)),
            _resident((1, qw)),
            tab(HEAD_PAD), tab(HEAD_PAD), tab(LANES), tab(LANES),
            _resident((3, CONV_DIM)), _resident((3, SSM_CONV_CH)), _resident((1, SSM_CONV_CH)),
        ] + [pl.BlockSpec(memory_space=pl.ANY)] * len(caches),
        out_specs=tuple(out_specs),
        out_shape=out_shape,
        scratch_shapes=[pltpu.VMEM((D_MODEL, IN_COLS2), BF16), pltpu.VMEM((2, W_PIECE, D_MODEL), F32),
                        pltpu.VMEM((LANES, D_MODEL), F32), pltpu.SemaphoreType.DMA((2,)),
                        pltpu.SemaphoreType.DMA((2,))],
        compiler_params=pltpu.CompilerParams(dimension_semantics=("arbitrary",), vmem_limit_bytes=VMEM_LIMIT),
        name="inproj",
        **extra,
    )(xc, xc, xc, xl, xl, xl, mod3, lw["norm1_w"], w_in, lw["q_norm_w"], lw["wq_a"], lw["kv_norm_w"],
      lw["wuk"], lw["wuv"], lw["wuv_c"], tabs["vone"], tabs["cosq"], tabs["sinq"], tabs["cosk"],
      tabs["sink"], lw["conv_a_w"], lw["ssm_conv_w"], lw["ssm_conv_b"], *caches)


def _kvcache_kernel(ckv_ref, kpe_ref, wuk_ref, wuv_ref, vone_ref, kf_ref, v_ref):
    ckvb = ckv_ref[...].astype(BF16)
    kf_ref[...] = _place_rope_key(_dot(ckvb, wuk_ref[...]), kpe_ref[...])
    v_ref[...] = (_dot(ckvb, wuv_ref[...]) + vone_ref[...]).astype(BF16)


def _kvcache(ckv, kpe128, lw, tabs):
    n = ckv.shape[0]
    qw = MLA_HEADS * HEAD_PAD
    return pl.pallas_call(
        _kvcache_kernel,
        grid=(n // PAST_LEN,),
        in_specs=[
            pl.BlockSpec((PAST_LEN, KV_RANK), lambda i: (i, 0)),
            pl.BlockSpec((PAST_LEN, LANES), lambda i: (i, 0)),
            _resident((KV_RANK, qw)),
            _resident((KV_RANK, qw)),
            _resident((1, qw)),
        ],
        out_specs=(pl.BlockSpec((PAST_LEN, qw), lambda i: (i, 0)), pl.BlockSpec((PAST_LEN, qw), lambda i: (i, 0))),
        out_shape=(jax.ShapeDtypeStruct((n, qw), BF16), jax.ShapeDtypeStruct((n, qw), BF16)),
        compiler_params=pltpu.CompilerParams(dimension_semantics=("arbitrary",)),
        name="kvcache",
    )(ckv, kpe128, lw["wuk"], lw["wuv"], tabs["vone"])


def _attn_heads(q_ref, kv_refs, o_ref, acc_ref, denom_from_matmul):
    log2_scale = MLA_SCALE * math.log2(math.e)
    for h in range(MLA_HEADS):
        hs = slice(h * HEAD_PAD, (h + 1) * HEAD_PAD)
        qh = q_ref[:, hs]
        ss = [_dot_nt(qh, k_ref[:, hs]) for k_ref, _ in kv_refs]
        if denom_from_matmul:
            m = functools.reduce(jnp.maximum, [jnp.max(s, axis=-1, keepdims=True) for s in ss])
            ps = [jnp.exp2((s - m) * log2_scale) for s in ss]
            ol = functools.reduce(jnp.add, [_dot(p.astype(BF16), v_ref[:, hs])
                                            for p, (_, v_ref) in zip(ps, kv_refs)])
            acc_ref[:, h * V_HEAD:(h + 1) * V_HEAD] = ol[:, 0:V_HEAD] / ol[:, V_HEAD:V_HEAD + 1]
        else:
            ss = [s * MLA_SCALE for s in ss]
            m = functools.reduce(jnp.maximum, [jnp.max(s, axis=-1, keepdims=True) for s in ss])
            ps = [jnp.exp(s - m) for s in ss]
            l = functools.reduce(jnp.add, [jnp.sum(p, axis=-1, keepdims=True) for p in ps])
            o = functools.reduce(jnp.add, [_dot(p.astype(BF16), v_ref[:, h * V_HEAD:(h + 1) * V_HEAD])
                                           for p, (_, v_ref) in zip(ps, kv_refs)])
            acc_ref[:, h * V_HEAD:(h + 1) * V_HEAD] = o / l
    o_ref[...] = acc_ref[...].astype(BF16)


def _attn_ctx_kernel(q_ref, k_ref, v_ref, o_ref, acc_ref):
    _attn_heads(q_ref, [(k_ref, v_ref)], o_ref, acc_ref, denom_from_matmul=False)


def _attn_lat_kernel(q_ref, k_ref, v_ref, kc_ref, vc_ref, o_ref, acc_ref):
    _attn_heads(q_ref, [(k_ref, v_ref), (kc_ref, vc_ref)], o_ref, acc_ref, denom_from_matmul=True)


def _attention(q, kf, v_ctx, v_lat, kf_c, v_c):
    qw = MLA_HEADS * HEAD_PAD
    vw = MLA_HEADS * V_HEAD
    att_ctx = pl.pallas_call(
        _attn_ctx_kernel,
        grid=(BATCH,),
        in_specs=[pl.BlockSpec((SEQ, qw), lambda b: (b, 0)), pl.BlockSpec((SEQ, qw), lambda b: (b, 0)),
                  pl.BlockSpec((SEQ, vw), lambda b: (b, 0))],
        out_specs=pl.BlockSpec((SEQ, vw), lambda b: (b, 0)),
        out_shape=jax.ShapeDtypeStruct((N_CTX, vw), BF16),
        scratch_shapes=[pltpu.VMEM((SEQ, vw), F32)],
        compiler_params=pltpu.CompilerParams(dimension_semantics=("arbitrary",)),
        name="attn_ctx",
    )(q, kf, v_ctx)
    lat0 = N_CTX // DEC_SEQ
    q_tiles = DEC_SEQ // TQ_LAT
    att_lat = pl.pallas_call(
        _attn_lat_kernel,
        grid=(DEC_BATCH, q_tiles),
        in_specs=[
            pl.BlockSpec((TQ_LAT, qw), lambda b, t: (N_CTX // TQ_LAT + b * q_tiles + t, 0)),
            pl.BlockSpec((DEC_SEQ, qw), lambda b, t: (lat0 + b, 0)),
            pl.BlockSpec((DEC_SEQ, qw), lambda b, t: (b, 0)),
            pl.BlockSpec((PAST_LEN, qw), lambda b, t: (b, 0)),
            pl.BlockSpec((PAST_LEN, qw), lambda b, t: (b, 0)),
        ],
        out_specs=pl.BlockSpec((TQ_LAT, vw), lambda b, t: (b * q_tiles + t, 0)),
        out_shape=jax.ShapeDtypeStruct((N_LAT, vw), BF16),
        scratch_shapes=[pltpu.VMEM((TQ_LAT, vw), F32)],
        compiler_params=pltpu.CompilerParams(dimension_semantics=("arbitrary", "arbitrary"),
                                             vmem_limit_bytes=VMEM_LIMIT),
        name="attn_lat",
    )(q, kf, v_lat, kf_c, v_c)
    return att_ctx, att_lat


def _seq_of_step(s):
    return jnp.where(s < N_CTX_STEPS, s // CTX_STEPS_PER_SEQ,
                     BATCH + (s - N_CTX_STEPS) // LAT_STEPS_PER_SEQ)


def _step_in_seq(s):
    return jnp.where(s < N_CTX_STEPS, s % CTX_STEPS_PER_SEQ, (s - N_CTX_STEPS) % LAT_STEPS_PER_SEQ)


def _steps_in_seq(s):
    return jnp.where(s < N_CTX_STEPS, CTX_STEPS_PER_SEQ, LAT_STEPS_PER_SEQ)


def _mirror_step(s):
    return s + _steps_in_seq(s) - 1 - 2 * _step_in_seq(s)


def _split3(a):
    a1 = a.astype(BF16)
    r1 = a - a1.astype(F32)
    a2 = r1.astype(BF16)
    a3 = (r1 - a2.astype(F32)).astype(BF16)
    return a1, a2, a3


def _ssd_direction(x_ref, sm_ref, par_ref, st_ref, y_ref, lane0, backward, r0):
    rows = slice(r0, r0 + CHUNK)
    ri = lax.broadcasted_iota(I32, (CHUNK, CHUNK), 0)
    ci = lax.broadcasted_iota(I32, (CHUNK, CHUNK), 1)
    tri = (ci >= ri) if backward else (ci <= ri)
    tri_b = jnp.where(tri, 1.0, 0.0).astype(BF16)
    tot_row = 0 if backward else CHUNK - 1

    dt = jax.nn.softplus(sm_ref[rows, :] + par_ref[0:1, :])
    a = dt * (-jnp.exp(par_ref[1:2, :])) * par_ref[2:3, :]
    a1, a2, a3 = _split3(a)
    acs = (_dot(tri_b, a1) + _dot(tri_b, a2) + _dot(tri_b, a3)) * math.log2(math.e)
    acs_t = acs.T
    dt_t = dt.T
    first_head = lax.broadcasted_iota(I32, (1, LANES), 1) < SSM_HEAD_DIM

    def block_diag(pair):
        return jnp.concatenate([jnp.where(first_head, pair, 0.0), jnp.where(first_head, 0.0, pair)],
                               axis=0).astype(BF16)

    for g in range(SSM_GROUPS):
        b0 = SSM_INNER + g * SSM_STATE
        c0 = SSM_INNER + SSM_GROUPS * SSM_STATE + g * SSM_STATE
        bg = x_ref[rows, b0:b0 + SSM_STATE]
        cg = x_ref[rows, c0:c0 + SSM_STATE]
        cb = _dot_nt(cg.astype(BF16), bg.astype(BF16))
        bg_t = bg.T
        heads = SSM_HEADS // SSM_GROUPS
        for pr in range(heads // 2):
            h0 = g * heads + 2 * pr
            sl = slice(h0 * SSM_HEAD_DIM, (h0 + 2) * SSM_HEAD_DIM)
            x_bd = block_diag(x_ref[rows, sl])
            st_old = st_ref[:, sl]
            within, carried, to_state, keep = [], [], [], []
            for h in (h0, h0 + 1):
                lane = lane0 + h
                col = jnp.broadcast_to(acs[:, lane:lane + 1], (CHUNK, CHUNK))
                row = acs_t[lane:lane + 1, :]
                dt_row = dt_t[lane:lane + 1, :]
                decay = jnp.exp2(jnp.where(tri, col - row, -jnp.inf))
                within.append((cb * decay * dt_row).astype(BF16))
                carried.append((cg * jnp.exp2(col)).astype(BF16))
                tot = acs[tot_row:tot_row + 1, lane:lane + 1]
                to_state.append((bg_t * (dt_row * jnp.exp2(tot - row))).astype(BF16))
                keep.append(jnp.exp2(tot))
            lhs = jnp.concatenate(within + carried, axis=1)
            y_ref[rows, sl] = _dot(lhs, jnp.concatenate([x_bd, block_diag(st_old)], axis=0))
            st_ref[:, sl] = (st_old * jnp.where(first_head, keep[0], keep[1])
                             + _dot(jnp.concatenate(to_state, axis=1), x_bd))


def _ssd_kernel(layer, *refs):
    n_in = 7 + (2 if layer > 0 else 0)
    xf_ref, xb_ref, smf_ref, smb_ref, if_ref, ib_ref, par_ref = refs[:7]
    yf_ref, yb_ref, sf_ref, sb_ref, stf_ref, stb_ref = refs[n_in:]
    s = pl.program_id(0)
    c = _step_in_seq(s)

    @pl.when(jnp.logical_and(c == 0, s < N_CTX_STEPS))
    def _():
        stf_ref[...] = jnp.zeros_like(stf_ref)
        stb_ref[...] = jnp.zeros_like(stb_ref)

    @pl.when(jnp.logical_and(c == 0, s >= N_CTX_STEPS))
    def _():
        stf_ref[...] = if_ref[0].T
        stb_ref[...] = ib_ref[0].T

    for r0 in range(0, SSD_STEP, CHUNK):
        _ssd_direction(xf_ref, smf_ref, par_ref, stf_ref, yf_ref, SM_DTF, False, r0)
        _ssd_direction(xb_ref, smb_ref, par_ref, stb_ref, yb_ref, SM_DTB, True, SSD_STEP - CHUNK - r0)

    @pl.when(jnp.logical_and(c == _steps_in_seq(s) - 1, s < N_CTX_STEPS))
    def _():
        if layer == 0:
            _first_layer_slabs(sf_ref, stf_ref[...].T)
            _first_layer_slabs(sb_ref, stb_ref[...].T)
        else:
            sf_ref[0] = stf_ref[...].T
            sb_ref[0] = stb_ref[...].T


def _ssd(layer, xact, small, init_f, init_b, par, states):
    hp = SSM_INNER
    fwd = lambda n: pl.BlockSpec((SSD_STEP, n), lambda s: (s, 0))
    bwd = lambda n: pl.BlockSpec((SSD_STEP, n), lambda s: (_mirror_step(s), 0))
    st = pl.BlockSpec((1, DEPTH, hp, SSM_STATE) if layer == 0 else (1, None, hp, SSM_STATE),
                      lambda s: (jnp.minimum(_seq_of_step(s), BATCH - 1), 0 if layer == 0 else layer, 0, 0))
    init = pl.BlockSpec((1, hp, SSM_STATE), lambda s: (jnp.maximum(_seq_of_step(s) - BATCH, 0), 0, 0))
    extra = {} if layer == 0 else dict(input_output_aliases={7: 2, 8: 3})
    return pl.pallas_call(
        functools.partial(_ssd_kernel, layer),
        grid=(N_SSD_STEPS,),
        in_specs=[fwd(SSM_CONV_CH), bwd(SSM_CONV_CH), fwd(LANES), bwd(LANES), init, init,
                  pl.BlockSpec((SUBLANES, LANES), lambda s: (0, 0))] + [pl.BlockSpec(memory_space=pl.ANY)] * len(states),
        out_specs=(fwd(hp), bwd(hp), st, st),
        out_shape=(jax.ShapeDtypeStruct((N_TOK, hp), F32), jax.ShapeDtypeStruct((N_TOK, hp), F32),
                   jax.ShapeDtypeStruct((BATCH, DEPTH, hp, SSM_STATE), F32),
                   jax.ShapeDtypeStruct((BATCH, DEPTH, hp, SSM_STATE), F32)),
        scratch_shapes=[pltpu.VMEM((SSM_STATE, hp), F32), pltpu.VMEM((SSM_STATE, hp), F32)],
        compiler_params=pltpu.CompilerParams(dimension_semantics=("arbitrary",)),
        name="ssd",
        **extra,
    )(xact, xact, small, small, init_f, init_b, par, *states)


def _merge_kernel(ta_ref, attc_ref, attl_ref, yf_ref, yb_ref, xs_ref, z_ref, g_ref, xc_ref, xl_ref, mod_ref,
                  woa_ref, wom_ref, dsk_ref, snw_ref, wos_ref, wo_ref, n2w_ref, rw_ref, rb_ref,
                  x1_ref, h2_ref, ti_ref, tg_ref, cnt_ref):
    mod = mod_ref[0]
    gate1 = mod[:, 2 * D_MODEL:3 * D_MODEL]
    shift2 = mod[:, 3 * D_MODEL:4 * D_MODEL]
    scale2 = mod[:, 4 * D_MODEL:5 * D_MODEL]
    y_a = _dot(ta_ref[...], woa_ref[...])
    att = jnp.where(pl.program_id(0) < CTX_TILES, attc_ref[...].astype(F32), attl_ref[...].astype(F32))
    y_b = _dot(att.astype(BF16), wom_ref[...])
    y_ssm = (yf_ref[...] + yb_ref[...] + dsk_ref[...] * xs_ref[...]) * _silu(z_ref[...])
    y_c = _dot(_rms(y_ssm, snw_ref[...]).astype(BF16), wos_ref[...])
    merged = (g_ref[:, 0:D_MODEL] * y_a + g_ref[:, D_MODEL:2 * D_MODEL] * y_b
              + g_ref[:, 2 * D_MODEL:3 * D_MODEL] * y_c)
    x = jnp.where(pl.program_id(0) < CTX_TILES, xc_ref[...], xl_ref[...])
    x1 = x + gate1 * _dot(merged.astype(BF16), wo_ref[...])
    x1_ref[...] = x1
    h2 = _rms(x1, n2w_ref[...]) * (1.0 + scale2) + shift2
    h2b = h2.astype(BF16)
    h2_ref[...] = h2b

    logits = (_dot(h2b, rw_ref[...]) + rb_ref[...]).T[0:N_EXPERTS, :]
    expert = lax.broadcasted_iota(I32, (N_EXPERTS, TM), 0)
    ids, vals = [], []
    for k in range(TOP_K):
        m = jnp.max(logits, axis=0, keepdims=True)
        idx = jnp.min(jnp.where(logits == m, expert, N_EXPERTS), axis=0, keepdims=True)
        ids.append(idx)
        vals.append(m)
        logits = jnp.where(expert == idx, -jnp.inf, logits)
    es = [jnp.exp(v - vals[0]) for v in vals]
    denom = functools.reduce(jnp.add, es)
    srow = lax.broadcasted_iota(I32, (LANES, TM), 0)
    ti_t = jnp.zeros((LANES, TM), F32)
    tg_t = jnp.zeros((LANES, TM), F32)
    chosen_t = jnp.zeros((LANES, TM), F32)
    for k in range(TOP_K):
        ti_t = jnp.where(srow == k, ids[k].astype(F32), ti_t)
        tg_t = jnp.where(srow == k, es[k] / denom, tg_t)
        chosen_t = jnp.where(srow == ids[k], 1.0, chosen_t)
    ti_ref[...] = ti_t.T.astype(I32)
    tg_ref[...] = tg_t.T
    cnt_ref[0] = _dot_nt(jnp.ones((SUBLANES, TM), BF16), chosen_t.astype(BF16)).astype(I32)


def _merge(ta, att_ctx, att_lat, yf, yb, xact, z, g, xc, xl, mod3, lw):
    row = lambda n: pl.BlockSpec((TM, n), lambda i: (i, 0))
    vw = MLA_HEADS * V_HEAD
    out_shape = (jax.ShapeDtypeStruct((N_TOK, D_MODEL), F32), jax.ShapeDtypeStruct((N_TOK, D_MODEL), BF16),
                 jax.ShapeDtypeStruct((N_TOK, LANES), I32), jax.ShapeDtypeStruct((N_TOK, LANES), F32))
    cnt_shape = jax.ShapeDtypeStruct((N_TILES, SUBLANES, LANES), I32)
    cnt_spec = pl.BlockSpec((1, SUBLANES, LANES), lambda i: (i, 0, 0))
    return pl.pallas_call(
        _merge_kernel,
        grid=(N_TILES,),
        in_specs=[
            row(CONV_DIM),
            pl.BlockSpec((TM, vw), lambda i: (jnp.minimum(i, CTX_TILES - 1), 0)),
            pl.BlockSpec((TM, vw), lambda i: (jnp.maximum(i - CTX_TILES, 0), 0)),
            row(SSM_INNER), row(SSM_INNER), row(SSM_INNER),
            row(SSM_INNER), row(3 * D_MODEL),
            pl.BlockSpec((TM, D_MODEL), lambda i: (jnp.minimum(i, CTX_TILES - 1), 0)),
            pl.BlockSpec((TM, D_MODEL), lambda i: (jnp.maximum(i - CTX_TILES, 0), 0)),
            pl.BlockSpec((1, 1, 6 * D_MODEL), lambda i: (_mod_row(i), 0, 0)),
            _resident((CONV_DIM, D_MODEL)), _resident((MLA_HEADS * V_HEAD, D_MODEL)),
            _resident((1, SSM_INNER)), _resident((1, SSM_INNER)), _resident((SSM_INNER, D_MODEL)),
            _resident((D_MODEL, D_MODEL)), _resident((1, D_MODEL)),
            _resident((D_MODEL, LANES)), _resident((1, LANES)),
        ],
        out_specs=tuple(row(s.shape[1]) for s in out_shape) + (cnt_spec,),
        out_shape=out_shape + (cnt_shape,),
        compiler_params=pltpu.CompilerParams(dimension_semantics=("arbitrary",), vmem_limit_bytes=VMEM_LIMIT),
        name="merge",
    )(ta, att_ctx, att_lat, yf, yb, xact, z, g, xc, xl, mod3, lw["w_out_a"], lw["w_o_mla"], lw["d_skip"],
      lw["ssm_norm_w"],
      lw["w_o_ssm"], lw["w_o"], lw["norm2_w"], lw["router_w"], lw["router_b"])


def _run_copies(cnt_ref, src_ref, dst_ref, first, count, bits, make_copy, start):
    def body(e, carry):
        n = cnt_ref[first + e]
        s0 = src_ref[first + e] if src_ref is not None else 0
        d0 = dst_ref[first + e]

        def pieces(some_bits):
            for b in some_bits:
                above = (n >> (b + 1)) << (b + 1)

                @pl.when(((n >> b) & 1) == 1)
                def _():
                    cp = make_copy(pl.multiple_of(s0 + above, RUN_ALIGN), pl.multiple_of(d0 + above, RUN_ALIGN),
                                   1 << b)
                    if start:
                        cp.start()
                    else:
                        cp.wait()

        large = [b for b in bits if b >= COMMON_BITS]

        @pl.when(n >= (1 << COMMON_BITS))
        def _():
            pieces(large)
        pieces([b for b in bits if b < COMMON_BITS])
        return carry
    lax.fori_loop(0, count, body, 0)


def _wait_rows(total, make_copy):
    for b in WAIT_BITS:
        @pl.when(((total >> b) & 1) == 1)
        def _():
            make_copy(0, 0, 1 << b).wait()


def _dispatch_kernel(cnt_ref, off_ref, run_ref, tcnt_ref, tdst_ref, nu_ref, h2_ref, ti_ref, tg_ref, offv_ref,
                     xs_ref, lp_ref, stage, zeros, perm_s, ghi_s, glo_s, sem, semz):
    i = pl.program_id(0)

    def copy_out(s_):
        return lambda s, d, n: pltpu.make_async_copy(stage.at[s_, pl.ds(s, n), :], xs_ref.at[pl.ds(d, n), :],
                                                     sem.at[s_])

    def tile_rows(t):
        last = t * N_EXPERTS + N_EXPERTS - 1
        return off_ref[last] + cnt_ref[last]

    def zero_fill(start):
        zero_out = lambda s, d, n: pltpu.make_async_copy(zeros.at[pl.ds(0, n), :], xs_ref.at[pl.ds(d, n), :], semz)
        _run_copies(tcnt_ref, None, tdst_ref, 0, N_EXPERTS, TAIL_BITS, zero_out, start)
        zrows = zeros.shape[0]

        def body(b, carry):
            for part in range(MOE_BLK // zrows):
                cp = zero_out(0, pl.multiple_of(b * MOE_BLK + part * zrows, RUN_ALIGN), zrows)
                if start:
                    cp.start()
                else:
                    cp.wait()
            return carry
        lax.fori_loop(nu_ref[0], N_BLOCKS, body, 0)

    @pl.when(i == 0)
    def _():
        zeros[...] = jnp.zeros_like(zeros)
        zero_fill(True)

    slots = [(i % 2) * DISPATCH_TILES + u for u in range(DISPATCH_TILES)]

    @pl.when(i >= 2)
    def _():
        for u in range(DISPATCH_TILES):
            _wait_rows(tile_rows((i - 2) * DISPATCH_TILES + u), copy_out(slots[u]))

    for u in range(DISPATCH_TILES):
        tile = i * DISPATCH_TILES + u
        toks = slice(u * TM, (u + 1) * TM)
        lane = lax.broadcasted_iota(I32, (TM, LANES), 1)
        picks = [jnp.where(lane == ti_ref[toks, k:k + 1], 1.0, 0.0) for k in range(TOP_K)]
        ri = lax.broadcasted_iota(I32, (TM, TM), 0)
        ci = lax.broadcasted_iota(I32, (TM, TM), 1)
        earlier = jnp.where(ci < ri, 1.0, 0.0).astype(BF16)
        base = _dot(earlier, functools.reduce(jnp.add, picks).astype(BF16)) + offv_ref[u][0:1, :]
        lp = jnp.full((TM, LANES), -1.0, F32)
        for k in range(TOP_K):
            lp = jnp.where(lane == k, jnp.sum(picks[k] * base, axis=-1, keepdims=True), lp)
        lp_ref[toks, :] = lp.astype(I32)

        lp_t = lp.T
        tg_t = tg_ref[toks, :].T
        for c in range(STAGE_ROWS // LANES):
            rows = slice(c * LANES, (c + 1) * LANES)
            row = (lax.broadcasted_iota(I32, (LANES, TM), 0) + c * LANES).astype(F32)
            perm = jnp.zeros((LANES, TM), F32)
            gates = jnp.zeros((LANES, TM), F32)
            for k in range(TOP_K):
                hit = row == lp_t[k:k + 1, :]
                perm = jnp.where(hit, 1.0, perm)
                gates = jnp.where(hit, tg_t[k:k + 1, :], gates)
            perm_s[u, rows, :] = perm.astype(BF16)
            g_hi = gates.astype(BF16)
            ghi_s[u, rows, :] = g_hi
            glo_s[u, rows, :] = (gates - g_hi.astype(F32)).astype(BF16)
        stage[slots[u], :, 0:D_MODEL] = _dot(perm_s[u], h2_ref[toks, :])
        ones = jnp.ones((TM, LANES), BF16)
        stage[slots[u], :, D_MODEL:XS_COLS] = _dot(ghi_s[u], ones) + _dot(glo_s[u], ones)

    for u in range(DISPATCH_TILES):
        _run_copies(cnt_ref, off_ref, run_ref, (i * DISPATCH_TILES + u) * N_EXPERTS, N_EXPERTS, RUN_BITS,
                    copy_out(slots[u]), True)

    @pl.when(i == N_TILES // DISPATCH_TILES - 1)
    def _():
        for u in range(DISPATCH_TILES):
            _wait_rows(tile_rows((i - 1) * DISPATCH_TILES + u), copy_out((1 - i % 2) * DISPATCH_TILES + u))
            _wait_rows(tile_rows(i * DISPATCH_TILES + u), copy_out(slots[u]))
        zero_fill(False)


def _dispatch(rt, h2, top_i, top_g):
    step_rows = DISPATCH_TILES * TM
    row = lambda n: pl.BlockSpec((step_rows, n), lambda i, *_: (i, 0))
    scratch = lambda: pltpu.VMEM((DISPATCH_TILES, STAGE_ROWS, TM), BF16)
    return pl.pallas_call(
        _dispatch_kernel,
        grid_spec=pltpu.PrefetchScalarGridSpec(
            num_scalar_prefetch=6,
            grid=(N_TILES // DISPATCH_TILES,),
            in_specs=[row(D_MODEL), row(LANES), row(LANES),
                      pl.BlockSpec((DISPATCH_TILES, SUBLANES, LANES), lambda i, *_: (i, 0, 0))],
            out_specs=(pl.BlockSpec(memory_space=pl.ANY), row(LANES)),
            scratch_shapes=[
                pltpu.VMEM((2 * DISPATCH_TILES, STAGE_ROWS, XS_COLS), F32),
                pltpu.VMEM((1 << TAIL_BITS[0], XS_COLS), F32),
                scratch(), scratch(), scratch(),
                pltpu.SemaphoreType.DMA((2 * DISPATCH_TILES,)),
                pltpu.SemaphoreType.DMA,
            ],
        ),
        out_shape=(jax.ShapeDtypeStruct((N_SLOTS, XS_COLS), F32), jax.ShapeDtypeStruct((N_TOK, LANES), I32)),
        compiler_params=pltpu.CompilerParams(dimension_semantics=("arbitrary",), vmem_limit_bytes=VMEM_LIMIT),
        name="dispatch",
    )(rt["cnt"], rt["off"], rt["run"], rt["tail_cnt"], rt["tail_dst"], rt["n_used"], h2, top_i, top_g, rt["off_v"])


def _moe_kernel(layer, be_ref, nxt_ref, nv_ref, nu_ref, x_ref, wgu_hbm, wdn_hbm, bg_ref, bu_ref, bd_ref, sel_ref,
                y_ref, wgu_ref, wdn_ref, wgu_s, wdn_s, sem):
    i = pl.program_id(0)
    n_used = nu_ref[0]

    def fetch(e):
        return (pltpu.make_async_copy(wgu_hbm.at[layer, e], wgu_ref, sem.at[0]),
                pltpu.make_async_copy(wdn_hbm.at[layer, e], wdn_ref, sem.at[1]))

    @pl.when(i < n_used)
    def _():
        @pl.when(jnp.logical_or(i == 0, be_ref[i] != be_ref[jnp.maximum(i - 1, 0)]))
        def _():
            @pl.when(i == 0)
            def _():
                for cp in fetch(be_ref[0]):
                    cp.start()
            for cp in fetch(be_ref[i]):
                cp.wait()
            half = LANES
            for c in range(2 * EXPERT_FF // (2 * half)):
                r = _dot(wgu_ref[:, c * 2 * half:(c + 1) * 2 * half].astype(BF16), sel_ref[...])
                wgu_s[:, c * half:(c + 1) * half] = r[:, 0:half].astype(BF16)
                wgu_s[:, EXPERT_FF + c * half:EXPERT_FF + (c + 1) * half] = r[:, half:2 * half].astype(BF16)
            wdn_s[...] = wdn_ref[...].astype(BF16)
            nxt = nxt_ref[i]

            @pl.when(nxt >= 0)
            def _():
                for cp in fetch(nxt):
                    cp.start(priority=1)

        def expert_rows(rows):
            gu = _dot(x_ref[0:rows, 0:D_MODEL].astype(BF16), wgu_s[...])
            gate = jnp.minimum(gu[:, 0:EXPERT_FF] + bg_ref[0], SWIGLU_LIMIT)
            up = jnp.clip(gu[:, EXPERT_FF:2 * EXPERT_FF] + bu_ref[0], -SWIGLU_LIMIT, SWIGLU_LIMIT)
            act = gate * jax.nn.sigmoid(SWIGLU_ALPHA * gate) * (up + 1.0)
            y = _dot(act.astype(BF16), wdn_s[...]) + bd_ref[0]
            slot_gate = x_ref[0:rows, D_MODEL:XS_COLS]
            for j in range(D_MODEL // LANES):
                y_ref[0:rows, j * LANES:(j + 1) * LANES] = y[:, j * LANES:(j + 1) * LANES] * slot_gate

        parts = (nv_ref[i] + MOE_PART - 1) // MOE_PART
        for n_parts in range(1, MOE_BLK // MOE_PART + 1):
            @pl.when(parts == n_parts)
            def _():
                rows = n_parts * MOE_PART
                expert_rows(rows)
                if rows < MOE_BLK:
                    y_ref[rows:MOE_BLK, :] = jnp.zeros((MOE_BLK - rows, D_MODEL), F32)

    @pl.when(i >= n_used)
    def _():
        y_ref[...] = jnp.zeros_like(y_ref)


def _bias_split_kernel(b_ref, sel_ref, o_ref):
    for c in range(2 * EXPERT_FF // (2 * LANES)):
        terms = _split3(b_ref[:, c * 2 * LANES:(c + 1) * 2 * LANES])
        r = functools.reduce(jnp.add, [_dot(t, sel_ref[...]) for t in terms])
        o_ref[:, c * LANES:(c + 1) * LANES] = r[:, 0:LANES]
        o_ref[:, EXPERT_FF + c * LANES:EXPERT_FF + (c + 1) * LANES] = r[:, LANES:2 * LANES]


def _bias_split(b_gu, sel):
    n = DEPTH * N_EXPERTS
    out = pl.pallas_call(
        _bias_split_kernel,
        out_shape=jax.ShapeDtypeStruct((n, 2 * EXPERT_FF), F32),
        name="bias_split",
    )(b_gu.reshape(n, 2 * EXPERT_FF), sel)
    return out.reshape(n, 1, 2 * EXPERT_FF)


def _moe(layer, rt, xs, w_gu, w_down, b_gu_split, b_down, sel):
    first = layer * N_EXPERTS
    bg = pl.BlockSpec((1, 1, EXPERT_FF), lambda i, be, *_: (first + be[i], 0, 0))
    bu = pl.BlockSpec((1, 1, EXPERT_FF), lambda i, be, *_: (first + be[i], 0, 1))
    bd = pl.BlockSpec((1, 1, D_MODEL), lambda i, be, *_: (first + be[i], 0, 0))
    return pl.pallas_call(
        functools.partial(_moe_kernel, layer),
        grid_spec=pltpu.PrefetchScalarGridSpec(
            num_scalar_prefetch=4,
            grid=(N_BLOCKS,),
            in_specs=[
                pl.BlockSpec((MOE_BLK, XS_COLS), lambda i, be, nx, nv, nu: (jnp.minimum(i, nu[0] - 1), 0)),
                pl.BlockSpec(memory_space=pl.ANY),
                pl.BlockSpec(memory_space=pl.ANY),
                bg, bu, bd,
                pl.BlockSpec((2 * LANES, 2 * LANES), lambda i, *_: (0, 0)),
            ],
            out_specs=pl.BlockSpec((MOE_BLK, D_MODEL), lambda i, *_: (i, 0)),
            scratch_shapes=[
                pltpu.VMEM((D_MODEL, 2 * EXPERT_FF), F32),
                pltpu.VMEM((EXPERT_FF, D_MODEL), F32),
                pltpu.VMEM((D_MODEL, 2 * EXPERT_FF), BF16),
                pltpu.VMEM((EXPERT_FF, D_MODEL), BF16),
                pltpu.SemaphoreType.DMA((2,)),
            ],
        ),
        out_shape=jax.ShapeDtypeStruct((N_SLOTS, D_MODEL), F32),
        compiler_params=pltpu.CompilerParams(dimension_semantics=("arbitrary",), vmem_limit_bytes=VMEM_LIMIT),
        name="moe",
    )(rt["blk_e"], rt["blk_next"], rt["blk_rows"], rt["n_used"], xs, w_gu, w_down, b_gu_split, b_gu_split,
      b_down.reshape(DEPTH * N_EXPERTS, 1, D_MODEL), sel)


def _combine_kernel(cnt_ref, off_ref, run_ref, y_ref, lp_ref, x1_ref, mod_ref, fw_ref,
                    xc_ref, xl_ref, yc_ref, yl_ref, stage, mine_s, sem):
    i = pl.program_id(0)
    slot = i % 2

    def copy_in(s_):
        return lambda s, d, n: pltpu.make_async_copy(y_ref.at[pl.ds(d, n), :], stage.at[s_, pl.ds(s, n), :],
                                                     sem.at[s_])

    @pl.when(i == 0)
    def _():
        stage[...] = jnp.zeros_like(stage)
        _run_copies(cnt_ref, off_ref, run_ref, 0, N_EXPERTS, RUN_BITS, copy_in(0), True)

    @pl.when(i + 1 < N_TILES)
    def _():
        _run_copies(cnt_ref, off_ref, run_ref, (i + 1) * N_EXPERTS, N_EXPERTS, RUN_BITS, copy_in(1 - slot), True)

    last = i * N_EXPERTS + N_EXPERTS - 1
    _wait_rows(off_ref[last] + cnt_ref[last], copy_in(slot))
    lane = lax.broadcasted_iota(I32, (TM, LANES), 1)
    mine = [jnp.broadcast_to(lp_ref[:, k:k + 1], (TM, LANES)) for k in range(TOP_K)]
    for c in range(STAGE_ROWS // LANES):
        hit = jnp.zeros((TM, LANES), F32)
        for k in range(TOP_K):
            hit = jnp.where(lane + c * LANES == mine[k], 1.0, hit)
        mine_s[:, c * LANES:(c + 1) * LANES] = hit.astype(BF16)
    moe = _dot(mine_s[...], stage[slot].astype(BF16))
    gate2 = mod_ref[0][:, 5 * D_MODEL:6 * D_MODEL]
    x2 = x1_ref[...] + gate2 * moe
    y_norm = _rms(x2, fw_ref[...])

    @pl.when(i < CTX_TILES)
    def _():
        xc_ref[...] = x2
        yc_ref[...] = y_norm

    @pl.when(i >= CTX_TILES)
    def _():
        xl_ref[...] = x2
        yl_ref[...] = y_norm


def _combine(rt, y_slots, lp, x1, mod3, final_w):
    row = lambda n: pl.BlockSpec((TM, n), lambda i, *_: (i, 0))
    ctx_row = pl.BlockSpec((TM, D_MODEL), lambda i, *_: (jnp.minimum(i, CTX_TILES - 1), 0))
    lat_row = pl.BlockSpec((TM, D_MODEL), lambda i, *_: (jnp.maximum(i - CTX_TILES, 0), 0))
    return pl.pallas_call(
        _combine_kernel,
        grid_spec=pltpu.PrefetchScalarGridSpec(
            num_scalar_prefetch=3,
            grid=(N_TILES,),
            in_specs=[
                pl.BlockSpec(memory_space=pl.ANY),
                row(LANES), row(D_MODEL),
                pl.BlockSpec((1, 1, 6 * D_MODEL), lambda i, *_: (_mod_row(i), 0, 0)),
                pl.BlockSpec((1, D_MODEL), lambda i, *_: (0, 0)),
            ],
            out_specs=(ctx_row, lat_row, ctx_row, lat_row),
            scratch_shapes=[pltpu.VMEM((2, STAGE_ROWS, D_MODEL), F32), pltpu.VMEM((TM, STAGE_ROWS), BF16),
                            pltpu.SemaphoreType.DMA((2,))],
        ),
        out_shape=(jax.ShapeDtypeStruct((N_CTX, D_MODEL), F32), jax.ShapeDtypeStruct((N_LAT, D_MODEL), F32),
                   jax.ShapeDtypeStruct((N_CTX, D_MODEL), F32), jax.ShapeDtypeStruct((N_LAT, D_MODEL), F32)),
        compiler_params=pltpu.CompilerParams(dimension_semantics=("arbitrary",), vmem_limit_bytes=VMEM_LIMIT),
        name="combine",
    )(rt["cnt"], rt["off"], rt["run"], y_slots, lp, x1, mod3, final_w)


def _rope_tables():
    rows = DEC_SEQ // GRID_W
    t = jnp.arange(rows * GRID_W)
    row = (t // GRID_W).astype(F32)
    col = (t % GRID_W).astype(F32)
    half = QK_ROPE // 2
    inv = ROPE_BASE ** (-jnp.arange(0, half, 2, dtype=F32) / half)
    ang_r, ang_c = row[:, None] * inv, col[:, None] * inv
    cos32 = jnp.concatenate([jnp.cos(ang_r), jnp.cos(ang_r), jnp.cos(ang_c), jnp.cos(ang_c)], axis=-1)
    sin32 = jnp.concatenate([-jnp.sin(ang_r), jnp.sin(ang_r), -jnp.sin(ang_c), jnp.sin(ang_c)], axis=-1)
    cos32 = jnp.concatenate([jnp.ones((TM, QK_ROPE), F32), cos32], axis=0)
    sin32 = jnp.concatenate([jnp.zeros((TM, QK_ROPE), F32), sin32], axis=0)
    n = cos32.shape[0]
    pad = HEAD_PAD - QK_NOPE - QK_ROPE
    cos_h = jnp.concatenate([jnp.ones((n, QK_NOPE), F32), cos32, jnp.zeros((n, pad), F32)], axis=-1)
    sin_h = jnp.concatenate([jnp.zeros((n, QK_NOPE), F32), sin32, jnp.zeros((n, pad), F32)], axis=-1)
    zeros = jnp.zeros((n, LANES - QK_ROPE), F32)
    return {
        "cosq": cos_h, "sinq": sin_h,
        "cosk": jnp.concatenate([cos32, zeros], axis=-1), "sink": jnp.concatenate([sin32, zeros], axis=-1),
        "vone": jnp.zeros((MLA_HEADS, HEAD_PAD), F32).at[:, V_HEAD].set(1.0).reshape(1, MLA_HEADS * HEAD_PAD),
    }


def _layer_weights(p, l):
    hd = QK_NOPE + QK_ROPE
    pad = HEAD_PAD - hd
    wq = p["w_uq"][l].reshape(Q_RANK, MLA_HEADS, hd)
    wq_a = jnp.pad(wq, ((0, 0), (0, 0), (0, pad)))
    wuk = jnp.pad(p["w_uk"][l], ((0, 0), (0, 0), (0, HEAD_PAD - QK_NOPE)))
    rw = jnp.pad(p["router_w"][l], ((0, 0), (0, LANES - N_EXPERTS)))
    rb = jnp.concatenate([p["router_b"][l], jnp.full((LANES - N_EXPERTS,), -jnp.inf, F32)])
    return {
        "norm1_w": p["norm1_w"][l][None], "q_norm_w": p["q_norm_w"][l][None],
        "wq_a": wq_a.reshape(Q_RANK, -1).astype(BF16),
        "kv_norm_w": p["kv_norm_w"][l][None],
        "wuk": wuk.reshape(KV_RANK, -1).astype(BF16),
        "wuv": jnp.pad(p["w_uv"][l], ((0, 0), (0, 0), (0, HEAD_PAD - V_HEAD))).reshape(KV_RANK, -1).astype(BF16),
        "wuv_c": p["w_uv"][l].reshape(KV_RANK, -1).astype(BF16),
        "conv_a_w": p["conv_a_w"][l], "ssm_conv_w": p["ssm_conv_w"][l], "ssm_conv_b": p["ssm_conv_b"][l][None],
        "w_out_a": p["w_out_a"][l].astype(BF16), "w_o_mla": p["w_o_mla"][l].astype(BF16),
        "d_skip": jnp.repeat(p["d_skip"][l], SSM_HEAD_DIM)[None], "ssm_norm_w": p["ssm_norm_w"][l][None],
        "w_o_ssm": p["w_o_ssm"][l].astype(BF16), "w_o": p["w_o"][l].astype(BF16),
        "norm2_w": p["norm2_w"][l][None], "router_w": rw.astype(BF16), "router_b": rb[None],
    }


def _ssd_params(p, l):
    z = lambda n: jnp.zeros((n,), F32)
    lanes = lambda f, b: jnp.concatenate([z(SM_DTF), f, b, z(LANES - SM_DTB - SSM_HEADS)])
    ones = jnp.ones((SSM_HEADS,), F32)
    rows = [lanes(p["dt_bias_fwd"][l], p["dt_bias_bwd"][l]), lanes(p["a_log_fwd"][l], p["a_log_bwd"][l]),
            lanes(ones, ones)]
    return jnp.concatenate([jnp.stack(rows), jnp.zeros((SUBLANES - 3, LANES), F32)], axis=0)


def _routing(cnt_tiles):
    cnt = cnt_tiles[:, 0, 0:N_EXPERTS]
    cnt = (cnt + RUN_ALIGN - 1) // RUN_ALIGN * RUN_ALIGN
    per_expert = jnp.sum(cnt, axis=0)
    padded = (per_expert + MOE_BLK - 1) // MOE_BLK * MOE_BLK
    pad_end = jnp.cumsum(padded)
    pad_start = pad_end - padded
    run = pad_start[None, :] + jnp.cumsum(cnt, axis=0) - cnt
    off = jnp.cumsum(cnt, axis=1) - cnt
    starts = jnp.arange(N_BLOCKS, dtype=I32) * MOE_BLK
    blk_e = jnp.minimum(jnp.sum((pad_end[None, :] <= starts[:, None]).astype(I32), axis=1), N_EXPERTS - 1)
    off_v = jnp.zeros((N_TILES, SUBLANES, LANES), F32).at[:, 0, 0:N_EXPERTS].set(off.astype(F32))
    ids = jnp.arange(N_EXPERTS, dtype=I32)
    later = jnp.logical_and(ids[None, :] > ids[:, None], padded[None, :] > 0)
    nxt = jnp.min(jnp.where(later, ids[None, :], N_EXPERTS), axis=1)
    nxt = jnp.where(nxt == N_EXPERTS, -1, nxt)
    per_block = lambda v: jnp.sum(jnp.where(blk_e[:, None] == ids[None, :], v[None, :], 0), axis=1)
    blk_rows = jnp.clip(per_block(pad_start + per_expert) - starts, 0, MOE_BLK)
    return {
        "blk_next": per_block(nxt).astype(I32), "blk_rows": blk_rows.astype(I32),
        "cnt": cnt.reshape(-1).astype(I32), "off": off.reshape(-1).astype(I32), "run": run.reshape(-1).astype(I32),
        "tail_cnt": (padded - per_expert).astype(I32), "tail_dst": (pad_start + per_expert).astype(I32),
        "blk_e": blk_e.astype(I32), "n_used": (pad_end[-1] // MOE_BLK).astype(I32).reshape(1), "off_v": off_v,
    }


def _deinterleave_matrix():
    k = jnp.arange(2 * LANES)[:, None]
    n = jnp.arange(2 * LANES)[None, :]
    src = jnp.where(n < LANES, 2 * n, 2 * (n - LANES) + 1)
    return (k == src).astype(BF16)


def kernel(x_prompt, x_sample, cache_ckv, cache_kpe, state_ssm_fwd, state_ssm_bwd, c, c_ctx, w_ada, b_ada, norm1_w, w_in, conv_a_w, w_out_a, q_norm_w, w_uq, kv_norm_w, w_uk, w_uv, w_o_mla, ssm_conv_w, ssm_conv_b, dt_bias_fwd, dt_bias_bwd, a_log_fwd, a_log_bwd, d_skip, ssm_norm_w, w_o_ssm, w_o, norm2_w, router_w, router_b, w_gu, b_gu, w_down, b_down, final_norm_w):
    p = dict(norm1_w=norm1_w, w_in=w_in, conv_a_w=conv_a_w, w_out_a=w_out_a, q_norm_w=q_norm_w, w_uq=w_uq,
             kv_norm_w=kv_norm_w, w_uk=w_uk, w_uv=w_uv, w_o_mla=w_o_mla, ssm_conv_w=ssm_conv_w,
             ssm_conv_b=ssm_conv_b, dt_bias_fwd=dt_bias_fwd, dt_bias_bwd=dt_bias_bwd, a_log_fwd=a_log_fwd,
             a_log_bwd=a_log_bwd, d_skip=d_skip, ssm_norm_w=ssm_norm_w, w_o_ssm=w_o_ssm, w_o=w_o,
             norm2_w=norm2_w, router_w=router_w, router_b=router_b, b_gu=b_gu, b_down=b_down)
    xc = x_prompt.reshape(N_CTX, D_MODEL)
    xl = x_sample.reshape(N_LAT, D_MODEL)
    cond8 = jnp.concatenate([c_ctx[None], c, jnp.zeros((SUBLANES - 1 - DEC_BATCH, D_MODEL), F32)], axis=0)
    mods = _ada_mods(cond8, w_ada, b_ada)
    tabs = _rope_tables()
    sel = _deinterleave_matrix()
    b_gu_split = _bias_split(b_gu, sel)
    w_in_t = jnp.swapaxes(w_in, 1, 2)
    final_w = final_norm_w[None]
    hp = SSM_INNER

    caches, states = (), ()
    y_ctx = y_lat = None
    for l in range(DEPTH):
        lw = _layer_weights(p, l)
        mod3 = mods[l].reshape(SUBLANES, 1, 6 * D_MODEL)
        ta, q, ckv, kpe, kf, v_ctx, v_lat, small, z, xact, g = _inproj(l, xc, xl, mod3, w_in_t, lw, tabs, caches)
        caches = (ckv, kpe)
        kpe_c = jnp.pad(cache_kpe[:, l].reshape(DEC_BATCH * PAST_LEN, QK_ROPE), ((0, 0), (0, LANES - QK_ROPE)))
        kf_c, v_c = _kvcache(cache_ckv[:, l].reshape(DEC_BATCH * PAST_LEN, KV_RANK), kpe_c, lw, tabs)
        att_ctx, att_lat = _attention(q, kf, v_ctx, v_lat, kf_c, v_c)
        init_f = state_ssm_fwd[:, l].reshape(DEC_BATCH, hp, SSM_STATE)
        init_b = state_ssm_bwd[:, l].reshape(DEC_BATCH, hp, SSM_STATE)
        yf, yb, sf, sb = _ssd(l, xact, small, init_f, init_b, _ssd_params(p, l), states)
        states = (sf, sb)
        x1, h2, top_i, top_g, cnt_tiles = _merge(ta, att_ctx, att_lat, yf, yb, xact, z, g, xc, xl, mod3, lw)
        rt = _routing(cnt_tiles)
        xs, lp = _dispatch(rt, h2, top_i, top_g)
        y_slots = _moe(l, rt, xs, w_gu, w_down, b_gu_split, b_down, sel)
        xc, xl, y_ctx, y_lat = _combine(rt, y_slots, lp, x1, mod3, final_w)

    y_prompt = y_ctx.reshape(BATCH, SEQ, D_MODEL)
    y_sample = y_lat.reshape(DEC_BATCH, DEC_SEQ, D_MODEL)
    state_shape = (BATCH, DEPTH, SSM_HEADS, SSM_HEAD_DIM, SSM_STATE)
    return (y_prompt, y_sample, caches[0], caches[1], states[0].reshape(state_shape), states[1].reshape(state_shape))
```
